```python
import jax, jax.numpy as jnp
from jax import lax
import numpy as np

D_MODEL = 2048
BATCH = 8
SEQ = 4096
DEPTH = 1

CHUNK = 64
Q_BLOCK = 128
PLE_DIM = 256
FOX_HEADS = 8
FOX_HEAD_DIM = D_MODEL // 16
GLA_HEADS = 4
GLA_KEY_DIM = D_MODEL // 16
GLA_VAL_DIM = D_MODEL // 8
GLA_GATE_RANK = 16
GLA_GATE_TAU = 16.0
D_FF = ((8 * D_MODEL // 3 + 255) // 256) * 256
EPS = 1e-6

FOX_W = FOX_HEADS * FOX_HEAD_DIM
GLA_KW = GLA_HEADS * GLA_KEY_DIM
GLA_VW = GLA_HEADS * GLA_VAL_DIM
IN_SPLITS = (FOX_W, FOX_W, FOX_W, FOX_HEADS, GLA_KW, GLA_KW, GLA_VW, GLA_VW, GLA_GATE_RANK)
D_IN = sum(IN_SPLITS)

kernel_name = "hybrid_fox_gla_macaron_ple"


def rms_norm(x, g):
    xf = x.astype(jnp.float32)
    y = xf * lax.rsqrt(jnp.mean(xf * xf, axis=-1, keepdims=True) + EPS)
    return (y * g.astype(jnp.float32)).astype(x.dtype)


def swiglu(x, w_gate, w_up, w_down):
    return (jax.nn.silu(x @ w_gate) * (x @ w_up)) @ w_down


def split_cols(t, sizes):
    out, start = [], 0
    for s in sizes:
        out.append(t[..., start:start + s])
        start += s
    return out


def forgetting_attention(q, k, v, log_f):
    b, s, h, d = q.shape
    nb = s // Q_BLOCK
    f_cum = jnp.cumsum(log_f.astype(jnp.float32), axis=1).transpose(0, 2, 1)
    q_blocks = q.reshape(b, nb, Q_BLOCK, h, d).transpose(1, 0, 2, 3, 4)
    fq_blocks = f_cum.reshape(b, h, nb, Q_BLOCK).transpose(2, 0, 1, 3)
    key_pos = jnp.arange(s)
    scale = d ** -0.5

    def block(args):
        q_i, fq_i, i = args
        logits = jnp.einsum('bqhd,bkhd->bhqk', q_i, k).astype(jnp.float32) * scale
        logits = logits + fq_i[..., :, None] - f_cum[:, :, None, :]
        q_pos = i * Q_BLOCK + jnp.arange(Q_BLOCK)
        causal = key_pos[None, :] <= q_pos[:, None]
        logits = jnp.where(causal, logits, -jnp.inf)
        probs = jax.nn.softmax(logits, axis=-1).astype(v.dtype)
        return jnp.einsum('bhqk,bkhd->bqhd', probs, v)

    out = lax.map(block, (q_blocks, fq_blocks, jnp.arange(nb)))
    return out.transpose(1, 0, 2, 3, 4).reshape(b, s, h * d)


def gla_chunk_causal(q, k, v, log_a):
    b, s, h, dk = q.shape
    dv = v.shape[-1]
    nc = s // CHUNK

    def chunks(t):
        return t.astype(jnp.float32).reshape(b, nc, CHUNK, h, t.shape[-1]).transpose(1, 0, 2, 3, 4)

    qc, kc, vc, ac = chunks(q), chunks(k), chunks(v), chunks(log_a)
    a_cum = jnp.cumsum(ac, axis=2)
    a_tot = a_cum[:, :, -1]
    k_dec = kc * jnp.exp(a_tot[:, :, None] - a_cum)
    qc = qc * (dk ** -0.5)

    def step(state, inp):
        q_c, k_c, v_c, a_c = inp
        state = jnp.exp(a_c)[..., None] * state + jnp.einsum('bchk,bchv->bhkv', k_c, v_c)
        o = jnp.einsum('bchk,bhkv->bchv', q_c, state)
        return state, o

    state0 = jnp.zeros((b, h, dk, dv), jnp.float32)
    _, o = lax.scan(step, state0, (qc, k_dec, vc, a_tot))
    return o.transpose(1, 0, 2, 3, 4).reshape(b, s, h, dv)


def _fwd_setup_inputs(seed: int = 0) -> dict:
    key = jax.random.key(seed)
    ks = iter(jax.random.split(key, 32))
    f32 = jnp.float32

    def w(shape, fan_in):
        return jax.random.normal(next(ks), (DEPTH,) + shape, f32) * (fan_in ** -0.5)

    def gain(shape):
        return 1.0 + 0.05 * jax.random.normal(next(ks), shape, f32)

    def bias(shape, mean=0.0, std=0.01):
        return mean + std * jax.random.normal(next(ks), shape, f32)

    return {
        "x": jax.random.normal(next(ks), (BATCH, SEQ, D_MODEL), f32),
        "p": jax.random.normal(next(ks), (DEPTH, BATCH, SEQ, PLE_DIM), f32),
        "ffn1_norm": gain((DEPTH, D_MODEL)),
        "ffn1_w_gate": w((D_MODEL, D_FF), D_MODEL),
        "ffn1_w_up": w((D_MODEL, D_FF), D_MODEL),
        "ffn1_w_down": w((D_FF, D_MODEL), D_FF),
        "mix_norm": gain((DEPTH, D_MODEL)),
        "w_in": w((D_MODEL, D_IN), D_MODEL),
        "fox_forget_bias": bias((DEPTH, FOX_HEADS), mean=3.0, std=0.1),
        "gla_gate_up": w((GLA_GATE_RANK, GLA_KW), GLA_GATE_RANK),
        "gla_gate_bias": bias((DEPTH, GLA_KW)),
        "gla_head_norm": gain((DEPTH, GLA_VAL_DIM)),
        "w_branch_fox": w((FOX_W, D_MODEL), FOX_W),
        "w_branch_gla": w((GLA_VW, D_MODEL), GLA_VW),
        "w_merge_gate": w((D_MODEL, 2 * D_MODEL), D_MODEL),
        "b_merge_gate": bias((DEPTH, 2 * D_MODEL)),
        "w_out": w((D_MODEL, D_MODEL), D_MODEL),
        "ffn2_norm": gain((DEPTH, D_MODEL)),
        "ffn2_w_gate": w((D_MODEL, D_FF), D_MODEL),
        "ffn2_w_up": w((D_MODEL, D_FF), D_MODEL),
        "ffn2_w_down": w((D_FF, D_MODEL), D_FF),
        "ple_norm": gain((DEPTH, D_MODEL)),
        "w_ple_proj": w((PLE_DIM, D_MODEL), PLE_DIM),
        "w_ple_gate": w((D_MODEL, D_MODEL), D_MODEL),
        "final_norm": gain((D_MODEL,)),
    }


def _fwd_reference(x, p, ffn1_norm, ffn1_w_gate, ffn1_w_up, ffn1_w_down, mix_norm, w_in,
              fox_forget_bias, gla_gate_up, gla_gate_bias, gla_head_norm,
              w_branch_fox, w_branch_gla, w_merge_gate, b_merge_gate, w_out,
              ffn2_norm, ffn2_w_gate, ffn2_w_up, ffn2_w_down,
              ple_norm, w_ple_proj, w_ple_gate, final_norm):
    b, s, _ = x.shape
    h = x
    for i in range(DEPTH):
        h = h + 0.5 * swiglu(rms_norm(h, ffn1_norm[i]), ffn1_w_gate[i], ffn1_w_up[i], ffn1_w_down[i])

        u = rms_norm(h, mix_norm[i])
        (fq, fk, fv, f_logit, gq, gk, gv, gr, g_down) = split_cols(u @ w_in[i], IN_SPLITS)

        log_f = jax.nn.log_sigmoid((f_logit + fox_forget_bias[i]).astype(jnp.float32))
        y_fox = forgetting_attention(
            fq.reshape(b, s, FOX_HEADS, FOX_HEAD_DIM),
            fk.reshape(b, s, FOX_HEADS, FOX_HEAD_DIM),
            fv.reshape(b, s, FOX_HEADS, FOX_HEAD_DIM),
            log_f.astype(x.dtype))

        log_a = jax.nn.log_sigmoid((g_down @ gla_gate_up[i] + gla_gate_bias[i]).astype(jnp.float32)) / GLA_GATE_TAU
        o_gla = gla_chunk_causal(
            gq.reshape(b, s, GLA_HEADS, GLA_KEY_DIM),
            gk.reshape(b, s, GLA_HEADS, GLA_KEY_DIM),
            gv.reshape(b, s, GLA_HEADS, GLA_VAL_DIM),
            log_a.reshape(b, s, GLA_HEADS, GLA_KEY_DIM))
        o_gla = rms_norm(o_gla, gla_head_norm[i]).reshape(b, s, GLA_VW).astype(x.dtype)
        y_gla = o_gla * jax.nn.silu(gr)

        gates = jax.nn.sigmoid(u @ w_merge_gate[i] + b_merge_gate[i])
        g_fox, g_gla = gates[..., :D_MODEL], gates[..., D_MODEL:]
        merged = g_fox * (y_fox @ w_branch_fox[i]) + g_gla * (y_gla @ w_branch_gla[i])
        h = h + merged @ w_out[i]

        h = h + 0.5 * swiglu(rms_norm(h, ffn2_norm[i]), ffn2_w_gate[i], ffn2_w_up[i], ffn2_w_down[i])

        ple_gate = jax.nn.sigmoid(rms_norm(h, ple_norm[i]) @ w_ple_gate[i])
        h = h + ple_gate * (p[i].astype(h.dtype) @ w_ple_proj[i])
    return rms_norm(h, final_norm)


import jax as _jax
import jax.numpy as _jnp

TWIN_FORMAT = 'train_step'
FWD_PARAMS = ['x', 'p', 'ffn1_norm', 'ffn1_w_gate', 'ffn1_w_up', 'ffn1_w_down', 'mix_norm', 'w_in', 'fox_forget_bias', 'gla_gate_up', 'gla_gate_bias', 'gla_head_norm', 'w_branch_fox', 'w_branch_gla', 'w_merge_gate', 'b_merge_gate', 'w_out', 'ffn2_norm', 'ffn2_w_gate', 'ffn2_w_up', 'ffn2_w_down', 'ple_norm', 'w_ple_proj', 'w_ple_gate', 'final_norm']
TWIN_WEIGHTS = ['ffn1_norm', 'ffn1_w_gate', 'ffn1_w_up', 'ffn1_w_down', 'mix_norm', 'w_in', 'fox_forget_bias', 'gla_gate_up', 'gla_gate_bias', 'gla_head_norm', 'w_branch_fox', 'w_branch_gla', 'w_merge_gate', 'b_merge_gate', 'w_out', 'ffn2_norm', 'ffn2_w_gate', 'ffn2_w_up', 'ffn2_w_down', 'ple_norm', 'w_ple_proj', 'w_ple_gate', 'final_norm']
TWIN_DIFF_INPUT = 'x'
TWIN_INPUTS = ['x', 'p', 'ffn1_norm', 'ffn1_w_gate', 'ffn1_w_up', 'ffn1_w_down', 'mix_norm', 'w_in', 'fox_forget_bias', 'gla_gate_up', 'gla_gate_bias', 'gla_head_norm', 'w_branch_fox', 'w_branch_gla', 'w_merge_gate', 'b_merge_gate', 'w_out', 'ffn2_norm', 'ffn2_w_gate', 'ffn2_w_up', 'ffn2_w_down', 'ple_norm', 'w_ple_proj', 'w_ple_gate', 'final_norm', 'loss_target', 'm_ffn1_norm', 'm_ffn1_w_gate', 'm_ffn1_w_up', 'm_ffn1_w_down', 'm_mix_norm', 'm_w_in', 'm_fox_forget_bias', 'm_gla_gate_up', 'm_gla_gate_bias', 'm_gla_head_norm', 'm_w_branch_fox', 'm_w_branch_gla', 'm_w_merge_gate', 'm_b_merge_gate', 'm_w_out', 'm_ffn2_norm', 'm_ffn2_w_gate', 'm_ffn2_w_up', 'm_ffn2_w_down', 'm_ple_norm', 'm_w_ple_proj', 'm_w_ple_gate', 'm_final_norm', 'v_ffn1_norm', 'v_ffn1_w_gate', 'v_ffn1_w_up', 'v_ffn1_w_down', 'v_mix_norm', 'v_w_in', 'v_fox_forget_bias', 'v_gla_gate_up', 'v_gla_gate_bias', 'v_gla_head_norm', 'v_w_branch_fox', 'v_w_branch_gla', 'v_w_merge_gate', 'v_b_merge_gate', 'v_w_out', 'v_ffn2_norm', 'v_ffn2_w_gate', 'v_ffn2_w_up', 'v_ffn2_w_down', 'v_ple_norm', 'v_w_ple_proj', 'v_w_ple_gate', 'v_final_norm']
TWIN_OUTPUTS = ['loss', 'grad_x', 'grad_ffn1_norm', 'grad_ffn1_w_gate', 'grad_ffn1_w_up', 'grad_ffn1_w_down', 'grad_mix_norm', 'grad_w_in', 'grad_fox_forget_bias', 'grad_gla_gate_up', 'grad_gla_gate_bias', 'grad_gla_head_norm', 'grad_w_branch_fox', 'grad_w_branch_gla', 'grad_w_merge_gate', 'grad_b_merge_gate', 'grad_w_out', 'grad_ffn2_norm', 'grad_ffn2_w_gate', 'grad_ffn2_w_up', 'grad_ffn2_w_down', 'grad_ple_norm', 'grad_w_ple_proj', 'grad_w_ple_gate', 'grad_final_norm', 'delta_ffn1_norm', 'delta_ffn1_w_gate', 'delta_ffn1_w_up', 'delta_ffn1_w_down', 'delta_mix_norm', 'delta_w_in', 'delta_fox_forget_bias', 'delta_gla_gate_up', 'delta_gla_gate_bias', 'delta_gla_head_norm', 'delta_w_branch_fox', 'delta_w_branch_gla', 'delta_w_merge_gate', 'delta_b_merge_gate', 'delta_w_out', 'delta_ffn2_norm', 'delta_ffn2_w_gate', 'delta_ffn2_w_up', 'delta_ffn2_w_down', 'delta_ple_norm', 'delta_w_ple_proj', 'delta_w_ple_gate', 'delta_final_norm', 'new_m_ffn1_norm', 'new_m_ffn1_w_gate', 'new_m_ffn1_w_up', 'new_m_ffn1_w_down', 'new_m_mix_norm', 'new_m_w_in', 'new_m_fox_forget_bias', 'new_m_gla_gate_up', 'new_m_gla_gate_bias', 'new_m_gla_head_norm', 'new_m_w_branch_fox', 'new_m_w_branch_gla', 'new_m_w_merge_gate', 'new_m_b_merge_gate', 'new_m_w_out', 'new_m_ffn2_norm', 'new_m_ffn2_w_gate', 'new_m_ffn2_w_up', 'new_m_ffn2_w_down', 'new_m_ple_norm', 'new_m_w_ple_proj', 'new_m_w_ple_gate', 'new_m_final_norm', 'new_v_ffn1_norm', 'new_v_ffn1_w_gate', 'new_v_ffn1_w_up', 'new_v_ffn1_w_down', 'new_v_mix_norm', 'new_v_w_in', 'new_v_fox_forget_bias', 'new_v_gla_gate_up', 'new_v_gla_gate_bias', 'new_v_gla_head_norm', 'new_v_w_branch_fox', 'new_v_w_branch_gla', 'new_v_w_merge_gate', 'new_v_b_merge_gate', 'new_v_w_out', 'new_v_ffn2_norm', 'new_v_ffn2_w_gate', 'new_v_ffn2_w_up', 'new_v_ffn2_w_down', 'new_v_ple_norm', 'new_v_w_ple_proj', 'new_v_w_ple_gate', 'new_v_final_norm']
TWIN_LEAF_KINDS = {'loss': 'loss', 'grad_x': 'grad_x', 'grad_ffn1_norm': 'grad_w', 'grad_ffn1_w_gate': 'grad_w', 'grad_ffn1_w_up': 'grad_w', 'grad_ffn1_w_down': 'grad_w', 'grad_mix_norm': 'grad_w', 'grad_w_in': 'grad_w', 'grad_fox_forget_bias': 'grad_w', 'grad_gla_gate_up': 'grad_w', 'grad_gla_gate_bias': 'grad_w', 'grad_gla_head_norm': 'grad_w', 'grad_w_branch_fox': 'grad_w', 'grad_w_branch_gla': 'grad_w', 'grad_w_merge_gate': 'grad_w', 'grad_b_merge_gate': 'grad_w', 'grad_w_out': 'grad_w', 'grad_ffn2_norm': 'grad_w', 'grad_ffn2_w_gate': 'grad_w', 'grad_ffn2_w_up': 'grad_w', 'grad_ffn2_w_down': 'grad_w', 'grad_ple_norm': 'grad_w', 'grad_w_ple_proj': 'grad_w', 'grad_w_ple_gate': 'grad_w', 'grad_final_norm': 'grad_w', 'delta_ffn1_norm': 'delta_w', 'delta_ffn1_w_gate': 'delta_w', 'delta_ffn1_w_up': 'delta_w', 'delta_ffn1_w_down': 'delta_w', 'delta_mix_norm': 'delta_w', 'delta_w_in': 'delta_w', 'delta_fox_forget_bias': 'delta_w', 'delta_gla_gate_up': 'delta_w', 'delta_gla_gate_bias': 'delta_w', 'delta_gla_head_norm': 'delta_w', 'delta_w_branch_fox': 'delta_w', 'delta_w_branch_gla': 'delta_w', 'delta_w_merge_gate': 'delta_w', 'delta_b_merge_gate': 'delta_w', 'delta_w_out': 'delta_w', 'delta_ffn2_norm': 'delta_w', 'delta_ffn2_w_gate': 'delta_w', 'delta_ffn2_w_up': 'delta_w', 'delta_ffn2_w_down': 'delta_w', 'delta_ple_norm': 'delta_w', 'delta_w_ple_proj': 'delta_w', 'delta_w_ple_gate': 'delta_w', 'delta_final_norm': 'delta_w', 'new_m_ffn1_norm': 'new_m', 'new_m_ffn1_w_gate': 'new_m', 'new_m_ffn1_w_up': 'new_m', 'new_m_ffn1_w_down': 'new_m', 'new_m_mix_norm': 'new_m', 'new_m_w_in': 'new_m', 'new_m_fox_forget_bias': 'new_m', 'new_m_gla_gate_up': 'new_m', 'new_m_gla_gate_bias': 'new_m', 'new_m_gla_head_norm': 'new_m', 'new_m_w_branch_fox': 'new_m', 'new_m_w_branch_gla': 'new_m', 'new_m_w_merge_gate': 'new_m', 'new_m_b_merge_gate': 'new_m', 'new_m_w_out': 'new_m', 'new_m_ffn2_norm': 'new_m', 'new_m_ffn2_w_gate': 'new_m', 'new_m_ffn2_w_up': 'new_m', 'new_m_ffn2_w_down': 'new_m', 'new_m_ple_norm': 'new_m', 'new_m_w_ple_proj': 'new_m', 'new_m_w_ple_gate': 'new_m', 'new_m_final_norm': 'new_m', 'new_v_ffn1_norm': 'new_v', 'new_v_ffn1_w_gate': 'new_v', 'new_v_ffn1_w_up': 'new_v', 'new_v_ffn1_w_down': 'new_v', 'new_v_mix_norm': 'new_v', 'new_v_w_in': 'new_v', 'new_v_fox_forget_bias': 'new_v', 'new_v_gla_gate_up': 'new_v', 'new_v_gla_gate_bias': 'new_v', 'new_v_gla_head_norm': 'new_v', 'new_v_w_branch_fox': 'new_v', 'new_v_w_branch_gla': 'new_v', 'new_v_w_merge_gate': 'new_v', 'new_v_b_merge_gate': 'new_v', 'new_v_w_out': 'new_v', 'new_v_ffn2_norm': 'new_v', 'new_v_ffn2_w_gate': 'new_v', 'new_v_ffn2_w_up': 'new_v', 'new_v_ffn2_w_down': 'new_v', 'new_v_ple_norm': 'new_v', 'new_v_w_ple_proj': 'new_v', 'new_v_w_ple_gate': 'new_v', 'new_v_final_norm': 'new_v'}


def _forward(args):
    return _fwd_reference(*[args[k] for k in FWD_PARAMS])


def _output_shape():
    def fwd():
        inp = _fwd_setup_inputs(0)
        return _fwd_reference(*[inp[k] for k in FWD_PARAMS])
    out = _jax.eval_shape(fwd)
    return out.shape, out.dtype

N_MICROBATCH = 1
ADAM_LR = 0.001
ADAM_B1 = 0.9
ADAM_B2 = 0.999
ADAM_EPS = 1e-08
ADAM_WD = 0.01
ADAM_STEP = 10
PER_EXAMPLE_BATCH_AXIS = {'x': 0, 'p': 1, 'loss_target': 0}
SHARED_INPUTS = []
_WEIGHT_DTYPES = {'ffn1_norm': _jnp.float32, 'ffn1_w_gate': _jnp.float32, 'ffn1_w_up': _jnp.float32, 'ffn1_w_down': _jnp.float32, 'mix_norm': _jnp.float32, 'w_in': _jnp.float32, 'fox_forget_bias': _jnp.float32, 'gla_gate_up': _jnp.float32, 'gla_gate_bias': _jnp.float32, 'gla_head_norm': _jnp.float32, 'w_branch_fox': _jnp.float32, 'w_branch_gla': _jnp.float32, 'w_merge_gate': _jnp.float32, 'b_merge_gate': _jnp.float32, 'w_out': _jnp.float32, 'ffn2_norm': _jnp.float32, 'ffn2_w_gate': _jnp.float32, 'ffn2_w_up': _jnp.float32, 'ffn2_w_down': _jnp.float32, 'ple_norm': _jnp.float32, 'w_ple_proj': _jnp.float32, 'w_ple_gate': _jnp.float32, 'final_norm': _jnp.float32}
MOMENT_SCALE = {'ffn1_norm': 4.292669e-02, 'ffn1_w_gate': 1.747002e-02, 'ffn1_w_up': 1.694849e-02, 'ffn1_w_down': 2.810427e-02, 'mix_norm': 5.668413e-02, 'w_in': 3.169905e-02, 'fox_forget_bias': 1.699836e-01, 'gla_gate_up': 5.809345e-03, 'gla_gate_bias': 2.285552e-02, 'gla_head_norm': 7.004121e-02, 'w_branch_fox': 1.325184e-02, 'w_branch_gla': 2.508634e-02, 'w_merge_gate': 7.806356e-03, 'b_merge_gate': 7.918090e-03, 'w_out': 2.821276e-02, 'ffn2_norm': 3.139075e-02, 'ffn2_w_gate': 1.339355e-02, 'ffn2_w_up': 1.299776e-02, 'ffn2_w_down': 2.157979e-02, 'ple_norm': 1.515543e-02, 'w_ple_proj': 4.021700e-02, 'w_ple_gate': 1.501038e-02, 'final_norm': 1.604465e+01}


def _to_microbatches(a, axis):
    t = _jnp.moveaxis(a, axis, 0)
    t = t.reshape((N_MICROBATCH, t.shape[0] // N_MICROBATCH) + t.shape[1:])
    return _jnp.moveaxis(t, 1, axis + 1)


def setup_inputs(seed: int = 0) -> dict:
    inp = _fwd_setup_inputs(seed)
    key = _jax.random.fold_in(_jax.random.key(seed), 7919)
    shape, _ = _output_shape()
    out = dict(inp)
    out["loss_target"] = _jax.random.normal(_jax.random.fold_in(key, 0), shape, _jnp.float32)
    for i, name in enumerate(TWIN_WEIGHTS):
        w = inp[name].astype(_jnp.float32)
        if MOMENT_SCALE is None:
            s = _jnp.sqrt(_jnp.mean(_jnp.square(w)) + 1e-30)
        else:
            s = MOMENT_SCALE[name]
        km, kv = _jax.random.split(_jax.random.fold_in(key, i + 1))
        out[name] = w
        out["m_" + name] = s * _jax.random.normal(km, w.shape, _jnp.float32)
        out["v_" + name] = (s * s) * _jax.random.uniform(kv, w.shape, _jnp.float32, 0.5, 1.5)
    if N_MICROBATCH > 1:
        for name, axis in PER_EXAMPLE_BATCH_AXIS.items():
            out[name] = _to_microbatches(out[name], axis)
    return {'x': out['x'], 'p': out['p'], 'ffn1_norm': out['ffn1_norm'], 'ffn1_w_gate': out['ffn1_w_gate'], 'ffn1_w_up': out['ffn1_w_up'], 'ffn1_w_down': out['ffn1_w_down'], 'mix_norm': out['mix_norm'], 'w_in': out['w_in'], 'fox_forget_bias': out['fox_forget_bias'], 'gla_gate_up': out['gla_gate_up'], 'gla_gate_bias': out['gla_gate_bias'], 'gla_head_norm': out['gla_head_norm'], 'w_branch_fox': out['w_branch_fox'], 'w_branch_gla': out['w_branch_gla'], 'w_merge_gate': out['w_merge_gate'], 'b_merge_gate': out['b_merge_gate'], 'w_out': out['w_out'], 'ffn2_norm': out['ffn2_norm'], 'ffn2_w_gate': out['ffn2_w_gate'], 'ffn2_w_up': out['ffn2_w_up'], 'ffn2_w_down': out['ffn2_w_down'], 'ple_norm': out['ple_norm'], 'w_ple_proj': out['w_ple_proj'], 'w_ple_gate': out['w_ple_gate'], 'final_norm': out['final_norm'], 'loss_target': out['loss_target'], 'm_ffn1_norm': out['m_ffn1_norm'], 'm_ffn1_w_gate': out['m_ffn1_w_gate'], 'm_ffn1_w_up': out['m_ffn1_w_up'], 'm_ffn1_w_down': out['m_ffn1_w_down'], 'm_mix_norm': out['m_mix_norm'], 'm_w_in': out['m_w_in'], 'm_fox_forget_bias': out['m_fox_forget_bias'], 'm_gla_gate_up': out['m_gla_gate_up'], 'm_gla_gate_bias': out['m_gla_gate_bias'], 'm_gla_head_norm': out['m_gla_head_norm'], 'm_w_branch_fox': out['m_w_branch_fox'], 'm_w_branch_gla': out['m_w_branch_gla'], 'm_w_merge_gate': out['m_w_merge_gate'], 'm_b_merge_gate': out['m_b_merge_gate'], 'm_w_out': out['m_w_out'], 'm_ffn2_norm': out['m_ffn2_norm'], 'm_ffn2_w_gate': out['m_ffn2_w_gate'], 'm_ffn2_w_up': out['m_ffn2_w_up'], 'm_ffn2_w_down': out['m_ffn2_w_down'], 'm_ple_norm': out['m_ple_norm'], 'm_w_ple_proj': out['m_w_ple_proj'], 'm_w_ple_gate': out['m_w_ple_gate'], 'm_final_norm': out['m_final_norm'], 'v_ffn1_norm': out['v_ffn1_norm'], 'v_ffn1_w_gate': out['v_ffn1_w_gate'], 'v_ffn1_w_up': out['v_ffn1_w_up'], 'v_ffn1_w_down': out['v_ffn1_w_down'], 'v_mix_norm': out['v_mix_norm'], 'v_w_in': out['v_w_in'], 'v_fox_forget_bias': out['v_fox_forget_bias'], 'v_gla_gate_up': out['v_gla_gate_up'], 'v_gla_gate_bias': out['v_gla_gate_bias'], 'v_gla_head_norm': out['v_gla_head_norm'], 'v_w_branch_fox': out['v_w_branch_fox'], 'v_w_branch_gla': out['v_w_branch_gla'], 'v_w_merge_gate': out['v_w_merge_gate'], 'v_b_merge_gate': out['v_b_merge_gate'], 'v_w_out': out['v_w_out'], 'v_ffn2_norm': out['v_ffn2_norm'], 'v_ffn2_w_gate': out['v_ffn2_w_gate'], 'v_ffn2_w_up': out['v_ffn2_w_up'], 'v_ffn2_w_down': out['v_ffn2_w_down'], 'v_ple_norm': out['v_ple_norm'], 'v_w_ple_proj': out['v_w_ple_proj'], 'v_w_ple_gate': out['v_w_ple_gate'], 'v_final_norm': out['v_final_norm']}


def _loss(weights, diff, rest, loss_target):
    with _jax.named_scope("forward"):
        args = {**rest, TWIN_DIFF_INPUT: diff, **{k: w.astype(_WEIGHT_DTYPES[k]) for k, w in weights.items()}}
        y = _forward(args)
    with _jax.named_scope("loss_head"):
        err = _jnp.square(y.astype(_jnp.float32) - loss_target)
        return 0.5 * _jnp.sum(_jnp.mean(err, axis=-1)) if err.ndim else 0.5 * err


def _adamw(w, g, m, v):
    m = ADAM_B1 * m + (1.0 - ADAM_B1) * g
    v = ADAM_B2 * v + (1.0 - ADAM_B2) * _jnp.square(g)
    m_hat = m / (1.0 - ADAM_B1 ** ADAM_STEP)
    v_hat = v / (1.0 - ADAM_B2 ** ADAM_STEP)
    delta = -ADAM_LR * (m_hat / (_jnp.sqrt(v_hat) + ADAM_EPS) + ADAM_WD * w)
    return delta, m, v


def reference(x, p, ffn1_norm, ffn1_w_gate, ffn1_w_up, ffn1_w_down, mix_norm, w_in, fox_forget_bias, gla_gate_up, gla_gate_bias, gla_head_norm, w_branch_fox, w_branch_gla, w_merge_gate, b_merge_gate, w_out, ffn2_norm, ffn2_w_gate, ffn2_w_up, ffn2_w_down, ple_norm, w_ple_proj, w_ple_gate, final_norm, loss_target, m_ffn1_norm, m_ffn1_w_gate, m_ffn1_w_up, m_ffn1_w_down, m_mix_norm, m_w_in, m_fox_forget_bias, m_gla_gate_up, m_gla_gate_bias, m_gla_head_norm, m_w_branch_fox, m_w_branch_gla, m_w_merge_gate, m_b_merge_gate, m_w_out, m_ffn2_norm, m_ffn2_w_gate, m_ffn2_w_up, m_ffn2_w_down, m_ple_norm, m_w_ple_proj, m_w_ple_gate, m_final_norm, v_ffn1_norm, v_ffn1_w_gate, v_ffn1_w_up, v_ffn1_w_down, v_mix_norm, v_w_in, v_fox_forget_bias, v_gla_gate_up, v_gla_gate_bias, v_gla_head_norm, v_w_branch_fox, v_w_branch_gla, v_w_merge_gate, v_b_merge_gate, v_w_out, v_ffn2_norm, v_ffn2_w_gate, v_ffn2_w_up, v_ffn2_w_down, v_ple_norm, v_w_ple_proj, v_w_ple_gate, v_final_norm):
    given = dict(x=x, p=p, ffn1_norm=ffn1_norm, ffn1_w_gate=ffn1_w_gate, ffn1_w_up=ffn1_w_up, ffn1_w_down=ffn1_w_down, mix_norm=mix_norm, w_in=w_in, fox_forget_bias=fox_forget_bias, gla_gate_up=gla_gate_up, gla_gate_bias=gla_gate_bias, gla_head_norm=gla_head_norm, w_branch_fox=w_branch_fox, w_branch_gla=w_branch_gla, w_merge_gate=w_merge_gate, b_merge_gate=b_merge_gate, w_out=w_out, ffn2_norm=ffn2_norm, ffn2_w_gate=ffn2_w_gate, ffn2_w_up=ffn2_w_up, ffn2_w_down=ffn2_w_down, ple_norm=ple_norm, w_ple_proj=w_ple_proj, w_ple_gate=w_ple_gate, final_norm=final_norm, loss_target=loss_target, m_ffn1_norm=m_ffn1_norm, m_ffn1_w_gate=m_ffn1_w_gate, m_ffn1_w_up=m_ffn1_w_up, m_ffn1_w_down=m_ffn1_w_down, m_mix_norm=m_mix_norm, m_w_in=m_w_in, m_fox_forget_bias=m_fox_forget_bias, m_gla_gate_up=m_gla_gate_up, m_gla_gate_bias=m_gla_gate_bias, m_gla_head_norm=m_gla_head_norm, m_w_branch_fox=m_w_branch_fox, m_w_branch_gla=m_w_branch_gla, m_w_merge_gate=m_w_merge_gate, m_b_merge_gate=m_b_merge_gate, m_w_out=m_w_out, m_ffn2_norm=m_ffn2_norm, m_ffn2_w_gate=m_ffn2_w_gate, m_ffn2_w_up=m_ffn2_w_up, m_ffn2_w_down=m_ffn2_w_down, m_ple_norm=m_ple_norm, m_w_ple_proj=m_w_ple_proj, m_w_ple_gate=m_w_ple_gate, m_final_norm=m_final_norm, v_ffn1_norm=v_ffn1_norm, v_ffn1_w_gate=v_ffn1_w_gate, v_ffn1_w_up=v_ffn1_w_up, v_ffn1_w_down=v_ffn1_w_down, v_mix_norm=v_mix_norm, v_w_in=v_w_in, v_fox_forget_bias=v_fox_forget_bias, v_gla_gate_up=v_gla_gate_up, v_gla_gate_bias=v_gla_gate_bias, v_gla_head_norm=v_gla_head_norm, v_w_branch_fox=v_w_branch_fox, v_w_branch_gla=v_w_branch_gla, v_w_merge_gate=v_w_merge_gate, v_b_merge_gate=v_b_merge_gate, v_w_out=v_w_out, v_ffn2_norm=v_ffn2_norm, v_ffn2_w_gate=v_ffn2_w_gate, v_ffn2_w_up=v_ffn2_w_up, v_ffn2_w_down=v_ffn2_w_down, v_ple_norm=v_ple_norm, v_w_ple_proj=v_w_ple_proj, v_w_ple_gate=v_w_ple_gate, v_final_norm=v_final_norm)
    weights = {n: given[n] for n in TWIN_WEIGHTS}
    shared = {n: given[n] for n in SHARED_INPUTS}
    per_example = {n: given[n] for n in ['x', 'p']}
    grad_fn = _jax.value_and_grad(_loss, argnums=(0, 1))

    def one_microbatch(ex, loss_target):
        ex = dict(ex)
        diff = ex.pop(TWIN_DIFF_INPUT)
        return grad_fn(weights, diff, {**shared, **ex}, loss_target)

    if N_MICROBATCH == 1:
        loss, (grad_w, grad_x) = one_microbatch(per_example, given["loss_target"])
    else:
        def body(carry, xs):
            loss_sum, grad_sum = carry
            l_k, (gw_k, gx_k) = one_microbatch(xs[0], xs[1])
            with _jax.named_scope("update"):
                return (loss_sum + l_k, _jax.tree.map(_jnp.add, grad_sum, gw_k)), gx_k

        init = (_jnp.zeros((), _jnp.float32), _jax.tree.map(_jnp.zeros_like, weights))
        (loss, grad_w), grad_x = _jax.lax.scan(body, init, (per_example, given["loss_target"]))
    with _jax.named_scope("update"):
        delta_w, new_m, new_v = {}, {}, {}
        for n in TWIN_WEIGHTS:
            delta_w[n], new_m[n], new_v[n] = _adamw(weights[n], grad_w[n], given["m_" + n], given["v_" + n])
    return (loss, grad_x, *[grad_w[n] for n in TWIN_WEIGHTS], *[delta_w[n] for n in TWIN_WEIGHTS],
            *[new_m[n] for n in TWIN_WEIGHTS], *[new_v[n] for n in TWIN_WEIGHTS])
```

```python
import functools

import jax
import jax.numpy as jnp
from jax import lax
from jax.experimental import pallas as pl
from jax.experimental.pallas import tpu as pltpu

F32 = jnp.float32
BF16 = jnp.bfloat16
MESH = pl.DeviceIdType.MESH
ANY = pl.BlockSpec(memory_space=pl.ANY)

D_MODEL = 2048
FOX_HEADS = 8
HEAD_DIM = 128
GLA_HEADS = 4
GLA_VAL_DIM = 256
GLA_RANK = 16
GLA_TAU = 16.0
CHUNK = 64
EPS = 1e-6
FOX_W = FOX_HEADS * HEAD_DIM
GLA_KW = GLA_HEADS * HEAD_DIM
GLA_VW = GLA_HEADS * GLA_VAL_DIM
Z_FQ, Z_FK, Z_FV, Z_GQ, Z_GK, Z_GV, Z_GR, Z_GL = 0, 1024, 2048, 3072, 3584, 4096, 5120, 6144
Z_W = Z_GL + 2 * D_MODEL
SMALL_W = 128
NEG = -1e30

ADAM_LR, ADAM_B1, ADAM_B2, ADAM_EPS, ADAM_WD, ADAM_STEP = 0.001, 0.9, 0.999, 1e-08, 0.01, 10

VMEM_LIMIT = 56 * 1024 * 1024


def _pick(n, target, mult=128):
    if n <= target:
        return n
    best = None
    for d in range(mult, target + 1, mult):
        if n % d == 0:
            best = d
    assert best is not None, (n, target)
    return best


def _mm(a, b, *, ta=False, tb=False, out_dtype=BF16, name):
    m, k = (a.shape[1], a.shape[0]) if ta else a.shape
    n = b.shape[0] if tb else b.shape[1]
    assert (b.shape[1] if tb else b.shape[0]) == k
    bm, bn, bk = _pick(m, 1024), _pick(n, 512), _pick(k, 2048)
    nk = k // bk
    dims = (((0 if ta else 1,), (1 if tb else 0,)), ((), ()))

    def body(a_ref, b_ref, o_ref, acc_ref):
        part = lax.dot_general(a_ref[...], b_ref[...], dims, preferred_element_type=F32)
        if nk == 1:
            o_ref[...] = part.astype(o_ref.dtype)
            return
        kk = pl.program_id(2)

        @pl.when(kk == 0)
        def _():
            acc_ref[...] = part

        @pl.when(kk > 0)
        def _():
            acc_ref[...] += part

        @pl.when(kk == nk - 1)
        def _():
            o_ref[...] = acc_ref[...].astype(o_ref.dtype)

    a_spec = pl.BlockSpec((bk, bm), lambda i, j, kk: (kk, i)) if ta else pl.BlockSpec((bm, bk), lambda i, j, kk: (i, kk))
    b_spec = pl.BlockSpec((bn, bk), lambda i, j, kk: (j, kk)) if tb else pl.BlockSpec((bk, bn), lambda i, j, kk: (kk, j))
    return pl.pallas_call(
        body, name=name, grid=(m // bm, n // bn, nk),
        in_specs=[a_spec, b_spec], out_specs=pl.BlockSpec((bm, bn), lambda i, j, kk: (i, j)),
        out_shape=jax.ShapeDtypeStruct((m, n), out_dtype),
        scratch_shapes=[pltpu.VMEM((bm, bn), F32)],
        compiler_params=pltpu.CompilerParams(dimension_semantics=("parallel", "parallel", "arbitrary"), vmem_limit_bytes=VMEM_LIMIT),
    )(a, b)


def _rowwise(fn, tiled, bcast, outs, reds=(), *, tt, name):
    t = tiled[0][0].shape[0]
    tt = min(tt, t)
    nin, nout = len(tiled) + len(bcast), len(outs)
    splits = [s[3] for s in tiled] + [s[1] for s in bcast]

    def store(ref, val, acc):
        off = 0
        for piece in val if isinstance(val, (tuple, list)) else (val,):
            w = piece.shape[-1]
            if acc:
                ref[:, off:off + w] += piece.astype(ref.dtype)
            else:
                ref[:, off:off + w] = piece.astype(ref.dtype)
            off += w
        assert off == ref.shape[-1], (name, off, ref.shape)

    def body(*refs):
        args = []
        for ref, sp in zip(refs[:nin], splits):
            if sp is None:
                args.append(ref[...])
            else:
                off = 0
                for w in sp:
                    args.append(ref[:, off:off + w])
                    off += w
        res = fn(*args)
        res = res if isinstance(res, (tuple, list)) else (res,)
        assert len(res) == nout + len(reds), (name, len(res))
        for ref, val in zip(refs[nin:nin + nout], res[:nout]):
            store(ref, val, False)
        if reds:
            @pl.when(pl.program_id(0) == 0)
            def _():
                for ref in refs[nin + nout:]:
                    ref[...] = jnp.zeros(ref.shape, ref.dtype)
            for ref, val in zip(refs[nin + nout:], res[nout:]):
                store(ref, val, True)

    in_specs = [pl.BlockSpec((tt, w), functools.partial(lambda i, cb: (i, cb), cb=cb)) for (_, w, cb, _) in tiled]
    in_specs += [pl.BlockSpec(arr.shape, lambda i: (0, 0)) for (arr, _) in bcast]
    out_specs = [pl.BlockSpec((tt, w), lambda i: (i, 0)) for (w, _) in outs]
    out_specs += [pl.BlockSpec((r, w), lambda i: (0, 0)) for (r, w) in reds]
    out_shape = [jax.ShapeDtypeStruct((t, w), dt) for (w, dt) in outs] + [jax.ShapeDtypeStruct((r, w), F32) for (r, w) in reds]
    return pl.pallas_call(
        body, name=name, grid=(t // tt,), in_specs=in_specs, out_specs=out_specs, out_shape=out_shape,
        compiler_params=pltpu.CompilerParams(dimension_semantics=("arbitrary" if reds else "parallel",), vmem_limit_bytes=VMEM_LIMIT),
    )(*[s[0] for s in tiled], *[s[0] for s in bcast])


def _full(arr):
    return (arr, arr.shape[1], 0, None)


def _f(x):
    return x.astype(F32)


def _rms(x, g):
    return x * lax.rsqrt(jnp.mean(x * x, axis=-1, keepdims=True) + EPS) * g


def _log_sigmoid(x):
    return jnp.minimum(x, 0.0) - jnp.log1p(jnp.exp(-jnp.abs(x)))


def _silu(x):
    return x * jax.nn.sigmoid(x)


def _norm_fwd(x, g, name):
    return _rowwise(lambda xb, gb: _rms(_f(xb), gb), [_full(x)], [(g, None)], [(x.shape[1], BF16)], tt=256, name=name)[0]


def _resnorm_fwd(res, branch, g, coef, name):
    def fn(rb, bb, gb):
        h = rb + coef * _f(bb)
        return h, _rms(h, gb)
    d = res.shape[1]
    return _rowwise(fn, [_full(res), _full(branch)], [(g, None)], [(d, F32), (d, BF16)], tt=256, name=name)


def _norm_bwd(h, dns, dres, g, coef, name):
    nd = len(dns)

    def fn(hb, *rest):
        dn = _f(rest[0])
        for extra in rest[1:nd]:
            dn = dn + _f(extra)
        dr, gb = rest[nd], rest[nd + 1]
        _, vjp = jax.vjp(_rms, hb, gb)
        dh, dg = vjp(dn)
        dh = dh + dr
        return dh, coef * dh, dg
    d = h.shape[1]
    return _rowwise(fn, [_full(h)] + [_full(x) for x in dns] + [_full(dres)], [(g, None)],
                    [(d, F32), (d, BF16)], [(1, d)], tt=256, name=name)


def _act_fwd(gu, name):
    ff = gu.shape[1] // 2
    return _rowwise(lambda gb, ub: _silu(_f(gb)) * _f(ub), [(gu, 2 * ff, 0, (ff, ff))], [], [(ff, BF16)], tt=256, name=name)[0]


def _act_bwd(gu, da, name):
    ff = gu.shape[1] // 2

    def fn(gb, ub, dab):
        _, vjp = jax.vjp(lambda p, q: _silu(p) * q, _f(gb), _f(ub))
        return (vjp(_f(dab)),)
    return _rowwise(fn, [(gu, 2 * ff, 0, (ff, ff)), _full(da)], [], [(2 * ff, BF16)], tt=128, name=name)[0]


def _merge(glf, glg, bf, bg, bmf, bmg):
    return jax.nn.sigmoid(_f(glf) + bmf) * _f(bf) + jax.nn.sigmoid(_f(glg) + bmg) * _f(bg)


def _merge_fwd(z, bf, bg, bm, name):
    d = D_MODEL
    return _rowwise(_merge, [(z, d, Z_GL // d, None), (z, d, Z_GL // d + 1, None), _full(bf), _full(bg)], [(bm, (d, d))],
                    [(d, BF16)], tt=256, name=name)[0]


def _merge_bwd(z, bf, bg, bm, dm, name):
    d = D_MODEL

    def fn(glf, glg, bfb, bgb, dmb, bmf, bmg):
        _, vjp = jax.vjp(_merge, _f(glf), _f(glg), _f(bfb), _f(bgb), bmf, bmg)
        dglf, dglg, dbf, dbg, dbmf, dbmg = vjp(_f(dmb))
        return (dglf, dglg), dbf, dbg, (dbmf, dbmg)
    return _rowwise(fn, [(z, d, Z_GL // d, None), (z, d, Z_GL // d + 1, None), _full(bf), _full(bg), _full(dm)], [(bm, (d, d))],
                    [(2 * d, BF16), (d, BF16), (d, BF16)], [(1, 2 * d)], tt=128, name=name)


def _gla_out(o, gr, g):
    return _rms(o, g) * _silu(_f(gr))


def _gla_out_fwd(o4, gr4, g, name):
    return _rowwise(lambda ob, rb, gb: _gla_out(ob, rb, gb), [_full(o4), _full(gr4)], [(g, None)], [(GLA_VAL_DIM, BF16)], tt=1024, name=name)[0]


def _gla_out_bwd(o4, gr4, g, dy4, name):
    def fn(ob, rb, dyb, gb):
        _, vjp = jax.vjp(_gla_out, ob, _f(rb), gb)
        return vjp(_f(dyb))
    return _rowwise(fn, [_full(o4), _full(gr4), _full(dy4)], [(g, None)], [(GLA_VAL_DIM, F32), (GLA_VAL_DIM, BF16)], [(1, GLA_VAL_DIM)],
                    tt=1024, name=name)


def _small_gates(s, fb, gup, gb):
    lane = lax.broadcasted_iota(jnp.int32, s.shape, 1)
    lf = jnp.where(lane < FOX_HEADS, _log_sigmoid(s + fb), 0.0)
    pre = jnp.dot(s.astype(BF16), gup.astype(BF16), preferred_element_type=F32) + gb
    return lf, _log_sigmoid(pre) / GLA_TAU


def _small_fwd(s, fb, gup, gb, name):
    return _rowwise(_small_gates, [_full(s)], [(fb, None), (gup, None), (gb, None)], [(SMALL_W, F32), (GLA_KW, F32)], tt=256, name=name)


def _small_bwd(s, fb, gup, gb, dlf, dla, name):
    def fn(sb, dlfb, dlab, fbb, gupb, gbb):
        _, vjp = jax.vjp(_small_gates, sb, fbb, gupb, gbb)
        return vjp((dlfb, dlab))
    return _rowwise(fn, [_full(s), _full(dlf), _full(dla)], [(fb, None), (gup, None), (gb, None)],
                    [(SMALL_W, BF16)], [(1, SMALL_W), (SMALL_W, GLA_KW), (1, GLA_KW)], tt=256, name=name)


def _head_fn(h3, pgl, pp, tgt, gf):
    h4 = h3 + jax.nn.sigmoid(pgl) * pp
    err = _rms(h4, gf) - tgt
    return 0.5 * jnp.sum(jnp.mean(err * err, axis=-1, keepdims=True))


def _head(h3, pgl, pp, tgt, gf, name):
    def fn(hb, gl, pb, tb, gfb):
        loss, vjp = jax.vjp(_head_fn, hb, _f(gl), _f(pb), tb, gfb)
        dh, dgl, dpp, _, dgf = vjp(jnp.ones((), F32))
        return dh, dgl, dpp, jnp.full((1, 128), loss, F32), dgf
    d = h3.shape[1]
    return _rowwise(fn, [_full(h3), _full(pgl), _full(pp), _full(tgt)], [(gf, None)],
                    [(d, F32), (d, BF16), (d, BF16)], [(1, 128), (1, d)], tt=256, name=name)


def _cumsum_tokens(a, reverse, name):
    t, w = a.shape
    r = min(256, t)
    nb = t // r

    def body(a_ref, o_ref, carry_ref):
        @pl.when(pl.program_id(0) == 0)
        def _():
            carry_ref[...] = jnp.zeros(carry_ref.shape, F32)
        row = lax.broadcasted_iota(jnp.int32, (r, r), 0)
        col = lax.broadcasted_iota(jnp.int32, (r, r), 1)
        tri = ((col >= row) if reverse else (col <= row)).astype(F32)
        blk = a_ref[...]
        o_ref[...] = jnp.dot(tri, blk, preferred_element_type=F32, precision=lax.Precision.HIGHEST) + carry_ref[...]
        carry_ref[...] += jnp.sum(blk, axis=0, keepdims=True)

    idx = (lambda i: (nb - 1 - i, 0)) if reverse else (lambda i: (i, 0))
    return pl.pallas_call(
        body, name=name, grid=(nb,), in_specs=[pl.BlockSpec((r, w), idx)], out_specs=pl.BlockSpec((r, w), idx),
        out_shape=jax.ShapeDtypeStruct((t, w), F32), scratch_shapes=[pltpu.VMEM((1, w), F32)],
        compiler_params=pltpu.CompilerParams(dimension_semantics=("arbitrary",)),
    )(a)


def _fox_blocks(t):
    tb = min(256, t)
    return tb, t // tb


def _fox_logits(q, k, fc, fr, i, j, tb):
    s = lax.dot_general(q, k, (((1,), (1,)), ((), ())), preferred_element_type=F32) * (HEAD_DIM ** -0.5)
    s = s + fc - fr
    row = i * tb + lax.broadcasted_iota(jnp.int32, (tb, tb), 0)
    col = j * tb + lax.broadcasted_iota(jnp.int32, (tb, tb), 1)
    return s, col <= row


def _fox_fwd(z, fcol, frow, name):
    t = z.shape[0]
    tb, nb = _fox_blocks(t)
    qb, kb, vb = Z_FQ // HEAD_DIM, Z_FK // HEAD_DIM, Z_FV // HEAD_DIM

    def body(q_ref, k_ref, v_ref, fc_ref, fr_ref, o_ref, lse_ref):
        i = pl.program_id(1)
        q = q_ref[...]
        fc = fc_ref[...]

        def step(j, carry):
            m, l, acc = carry
            rows = pl.ds(pl.multiple_of(j * tb, tb), tb)
            s, ok = _fox_logits(q, k_ref[rows, :], fc, fr_ref[j], i, j, tb)
            s = jnp.where(ok, s, NEG)
            m_new = jnp.maximum(m, jnp.max(s, axis=1, keepdims=True))
            alpha = jnp.exp(m - m_new)
            p = jnp.exp(s - m_new)
            l = alpha * l + jnp.sum(p, axis=1, keepdims=True)
            acc = alpha * acc + jnp.dot(p.astype(BF16), v_ref[rows, :], preferred_element_type=F32)
            return m_new, l, acc

        m, l, acc = lax.fori_loop(0, i + 1, step, (jnp.full((tb, 1), NEG, F32), jnp.zeros((tb, 1), F32), jnp.zeros((tb, HEAD_DIM), F32)))
        o_ref[...] = (acc / l).astype(o_ref.dtype)
        lse_ref[...] = m + jnp.log(l)

    return pl.pallas_call(
        body, name=name, grid=(FOX_HEADS, nb),
        in_specs=[pl.BlockSpec((tb, HEAD_DIM), lambda h, i: (i, qb + h)),
                  pl.BlockSpec((t, HEAD_DIM), lambda h, i: (0, kb + h)),
                  pl.BlockSpec((t, HEAD_DIM), lambda h, i: (0, vb + h)),
                  pl.BlockSpec((None, tb, 1), lambda h, i: (h, i, 0)),
                  pl.BlockSpec((None, nb, 1, tb), lambda h, i: (h, 0, 0, 0))],
        out_specs=[pl.BlockSpec((tb, HEAD_DIM), lambda h, i: (i, h)), pl.BlockSpec((None, tb, 1), lambda h, i: (h, i, 0))],
        out_shape=[jax.ShapeDtypeStruct((t, FOX_W), BF16), jax.ShapeDtypeStruct((FOX_HEADS, t, 1), F32)],
        compiler_params=pltpu.CompilerParams(dimension_semantics=("parallel", "parallel"), vmem_limit_bytes=VMEM_LIMIT),
    )(z, z, z, fcol, frow)


def _fox_bwd_q(z, o, do, lse, fcol, frow, name):
    t = z.shape[0]
    tb, nb = _fox_blocks(t)
    qb, kb, vb = Z_FQ // HEAD_DIM, Z_FK // HEAD_DIM, Z_FV // HEAD_DIM

    def body(q_ref, k_ref, v_ref, o_ref, do_ref, lse_ref, fc_ref, fr_ref, dq_ref, delta_ref, dfc_ref):
        i = pl.program_id(1)
        q, do = q_ref[...], do_ref[...]
        fc, lse = fc_ref[...], lse_ref[...]
        delta = jnp.sum(_f(do) * _f(o_ref[...]), axis=1, keepdims=True)
        delta_ref[...] = delta

        def step(j, carry):
            dq, dfc = carry
            rows = pl.ds(pl.multiple_of(j * tb, tb), tb)
            k = k_ref[rows, :]
            s, ok = _fox_logits(q, k, fc, fr_ref[j], i, j, tb)
            p = jnp.where(ok, jnp.exp(s - lse), 0.0)
            dp = lax.dot_general(do, v_ref[rows, :], (((1,), (1,)), ((), ())), preferred_element_type=F32)
            ds = p * (dp - delta)
            return dq + jnp.dot(ds.astype(BF16), k, preferred_element_type=F32), dfc + jnp.sum(ds, axis=1, keepdims=True)

        dq, dfc = lax.fori_loop(0, i + 1, step, (jnp.zeros((tb, HEAD_DIM), F32), jnp.zeros((tb, 1), F32)))
        dq_ref[...] = (dq * (HEAD_DIM ** -0.5)).astype(dq_ref.dtype)
        dfc_ref[...] = dfc

    col = lambda h, i: (h, i, 0)
    return pl.pallas_call(
        body, name=name, grid=(FOX_HEADS, nb),
        in_specs=[pl.BlockSpec((tb, HEAD_DIM), lambda h, i: (i, qb + h)),
                  pl.BlockSpec((t, HEAD_DIM), lambda h, i: (0, kb + h)),
                  pl.BlockSpec((t, HEAD_DIM), lambda h, i: (0, vb + h)),
                  pl.BlockSpec((tb, HEAD_DIM), lambda h, i: (i, h)),
                  pl.BlockSpec((tb, HEAD_DIM), lambda h, i: (i, h)),
                  pl.BlockSpec((None, tb, 1), col), pl.BlockSpec((None, tb, 1), col),
                  pl.BlockSpec((None, nb, 1, tb), lambda h, i: (h, 0, 0, 0))],
        out_specs=[pl.BlockSpec((tb, HEAD_DIM), lambda h, i: (i, h)), pl.BlockSpec((None, tb, 1), col), pl.BlockSpec((None, tb, 1), col)],
        out_shape=[jax.ShapeDtypeStruct((t, FOX_W), BF16), jax.ShapeDtypeStruct((FOX_HEADS, t, 1), F32), jax.ShapeDtypeStruct((FOX_HEADS, t, 1), F32)],
        compiler_params=pltpu.CompilerParams(dimension_semantics=("parallel", "parallel"), vmem_limit_bytes=VMEM_LIMIT),
    )(z, z, z, o, do, lse, fcol, frow)


def _fox_bwd_kv(z, do, lse, delta, fcol, frow, name):
    t = z.shape[0]
    tb, nb = _fox_blocks(t)
    qb, kb, vb = Z_FQ // HEAD_DIM, Z_FK // HEAD_DIM, Z_FV // HEAD_DIM

    def body(q_ref, k_ref, v_ref, do_ref, lse_ref, delta_ref, fc_ref, fr_ref, dk_ref, dv_ref, dfr_ref):
        j = pl.program_id(1)
        k, v, fr = k_ref[...], v_ref[...], fr_ref[...]

        def step(i, carry):
            dk, dv, dfr = carry
            rows = pl.ds(pl.multiple_of(i * tb, tb), tb)
            q, do = q_ref[rows, :], do_ref[rows, :]
            s, ok = _fox_logits(q, k, fc_ref[rows, :], fr, i, j, tb)
            p = jnp.where(ok, jnp.exp(s - lse_ref[rows, :]), 0.0)
            dv = dv + lax.dot_general(p.astype(BF16), do, (((0,), (0,)), ((), ())), preferred_element_type=F32)
            dp = lax.dot_general(do, v, (((1,), (1,)), ((), ())), preferred_element_type=F32)
            ds = p * (dp - delta_ref[rows, :])
            dk = dk + lax.dot_general(ds.astype(BF16), q, (((0,), (0,)), ((), ())), preferred_element_type=F32)
            return dk, dv, dfr - jnp.sum(ds, axis=0, keepdims=True)

        zero = jnp.zeros((tb, HEAD_DIM), F32)
        dk, dv, dfr = lax.fori_loop(j, nb, step, (zero, zero, jnp.zeros((1, tb), F32)))
        dk_ref[...] = (dk * (HEAD_DIM ** -0.5)).astype(dk_ref.dtype)
        dv_ref[...] = dv.astype(dv_ref.dtype)
        dfr_ref[...] = dfr

    whole = lambda h, j: (h, 0, 0)
    return pl.pallas_call(
        body, name=name, grid=(FOX_HEADS, nb),
        in_specs=[pl.BlockSpec((t, HEAD_DIM), lambda h, j: (0, qb + h)),
                  pl.BlockSpec((tb, HEAD_DIM), lambda h, j: (j, kb + h)),
                  pl.BlockSpec((tb, HEAD_DIM), lambda h, j: (j, vb + h)),
                  pl.BlockSpec((t, HEAD_DIM), lambda h, j: (0, h)),
                  pl.BlockSpec((None, t, 1), whole), pl.BlockSpec((None, t, 1), whole), pl.BlockSpec((None, t, 1), whole),
                  pl.BlockSpec((None, None, 1, tb), lambda h, j: (h, j, 0, 0))],
        out_specs=[pl.BlockSpec((tb, HEAD_DIM), lambda h, j: (j, h)), pl.BlockSpec((tb, HEAD_DIM), lambda h, j: (j, h)),
                   pl.BlockSpec((None, None, 1, tb), lambda h, j: (h, j, 0, 0))],
        out_shape=[jax.ShapeDtypeStruct((t, FOX_W), BF16), jax.ShapeDtypeStruct((t, FOX_W), BF16),
                   jax.ShapeDtypeStruct((FOX_HEADS, nb, 1, tb), F32)],
        compiler_params=pltpu.CompilerParams(dimension_semantics=("parallel", "parallel"), vmem_limit_bytes=VMEM_LIMIT),
    )(z, z, z, do, lse, delta, fcol, frow)


def _gla_step(st, q, k, v, la):
    row = lax.broadcasted_iota(jnp.int32, (CHUNK, CHUNK), 0)
    col = lax.broadcasted_iota(jnp.int32, (CHUNK, CHUNK), 1)
    tri = (col <= row).astype(F32)
    a_cum = jnp.dot(tri, la, preferred_element_type=F32, precision=lax.Precision.HIGHEST)
    a_tot = jnp.sum(la, axis=0, keepdims=True)
    k_dec = (_f(k) * jnp.exp(a_tot - a_cum)).astype(BF16)
    qs = (_f(q) * (HEAD_DIM ** -0.5)).astype(BF16)
    st = st * jnp.exp(a_tot) + lax.dot_general(v.astype(BF16), k_dec, (((0,), (0,)), ((), ())), preferred_element_type=F32)
    o = lax.dot_general(qs, st.astype(BF16), (((1,), (1,)), ((), ())), preferred_element_type=F32)
    return st, o


def _gla_blocks(t):
    r = min(256, t)
    return r, t // r, r // CHUNK


def _gla_fwd(z, la, name):
    t = z.shape[0]
    r, nb, nch = _gla_blocks(t)

    def body(q_ref, k_ref, v_ref, la_ref, o_ref, sp_ref, st_ref):
        @pl.when(pl.program_id(0) == 0)
        def _():
            st_ref[...] = jnp.zeros(st_ref.shape, F32)
        for c in range(nch):
            rows = slice(c * CHUNK, (c + 1) * CHUNK)
            for h in range(GLA_HEADS):
                kc = slice(h * HEAD_DIM, (h + 1) * HEAD_DIM)
                vc = slice(h * GLA_VAL_DIM, (h + 1) * GLA_VAL_DIM)
                st = st_ref[h]
                sp_ref[c, h] = st
                st, o = _gla_step(st, q_ref[rows, kc], k_ref[rows, kc], v_ref[rows, vc], la_ref[rows, kc])
                st_ref[h] = st
                o_ref[rows, vc] = o

    return pl.pallas_call(
        body, name=name, grid=(nb,),
        in_specs=[pl.BlockSpec((r, GLA_KW), lambda i: (i, Z_GQ // GLA_KW)), pl.BlockSpec((r, GLA_KW), lambda i: (i, Z_GK // GLA_KW)),
                  pl.BlockSpec((r, GLA_VW), lambda i: (i, Z_GV // GLA_VW)), pl.BlockSpec((r, GLA_KW), lambda i: (i, 0))],
        out_specs=[pl.BlockSpec((r, GLA_VW), lambda i: (i, 0)),
                   pl.BlockSpec((nch, GLA_HEADS, GLA_VAL_DIM, HEAD_DIM), lambda i: (i, 0, 0, 0))],
        out_shape=[jax.ShapeDtypeStruct((t, GLA_VW), F32),
                   jax.ShapeDtypeStruct((t // CHUNK, GLA_HEADS, GLA_VAL_DIM, HEAD_DIM), F32)],
        scratch_shapes=[pltpu.VMEM((GLA_HEADS, GLA_VAL_DIM, HEAD_DIM), F32)],
        compiler_params=pltpu.CompilerParams(dimension_semantics=("arbitrary",), vmem_limit_bytes=VMEM_LIMIT),
    )(z, z, z, la)


def _gla_bwd(z, la, sprev, do, name):
    t = z.shape[0]
    r, nb, nch = _gla_blocks(t)

    def body(q_ref, k_ref, v_ref, la_ref, sp_ref, do_ref, dq_ref, dk_ref, dv_ref, dla_ref, dst_ref):
        @pl.when(pl.program_id(0) == 0)
        def _():
            dst_ref[...] = jnp.zeros(dst_ref.shape, F32)
        for c in reversed(range(nch)):
            rows = slice(c * CHUNK, (c + 1) * CHUNK)
            for h in range(GLA_HEADS):
                kc = slice(h * HEAD_DIM, (h + 1) * HEAD_DIM)
                vc = slice(h * GLA_VAL_DIM, (h + 1) * GLA_VAL_DIM)
                _, vjp = jax.vjp(_gla_step, sp_ref[c, h], q_ref[rows, kc], k_ref[rows, kc], v_ref[rows, vc], la_ref[rows, kc])
                dst, dq, dk, dv, dla = vjp((dst_ref[h], do_ref[rows, vc]))
                dst_ref[h] = dst
                dq_ref[rows, kc] = dq
                dk_ref[rows, kc] = dk
                dv_ref[rows, vc] = dv
                dla_ref[rows, kc] = dla

    rev = lambda i: (nb - 1 - i, 0)
    return pl.pallas_call(
        body, name=name, grid=(nb,),
        in_specs=[pl.BlockSpec((r, GLA_KW), lambda i: (nb - 1 - i, Z_GQ // GLA_KW)), pl.BlockSpec((r, GLA_KW), lambda i: (nb - 1 - i, Z_GK // GLA_KW)),
                  pl.BlockSpec((r, GLA_VW), lambda i: (nb - 1 - i, Z_GV // GLA_VW)), pl.BlockSpec((r, GLA_KW), rev),
                  pl.BlockSpec((nch, GLA_HEADS, GLA_VAL_DIM, HEAD_DIM), lambda i: (nb - 1 - i, 0, 0, 0)),
                  pl.BlockSpec((r, GLA_VW), rev)],
        out_specs=[pl.BlockSpec((r, GLA_KW), rev), pl.BlockSpec((r, GLA_KW), rev), pl.BlockSpec((r, GLA_VW), rev), pl.BlockSpec((r, GLA_KW), rev)],
        out_shape=[jax.ShapeDtypeStruct((t, GLA_KW), BF16), jax.ShapeDtypeStruct((t, GLA_KW), BF16),
                   jax.ShapeDtypeStruct((t, GLA_VW), BF16), jax.ShapeDtypeStruct((t, GLA_KW), F32)],
        scratch_shapes=[pltpu.VMEM((GLA_HEADS, GLA_VAL_DIM, HEAD_DIM), F32)],
        compiler_params=pltpu.CompilerParams(dimension_semantics=("arbitrary",), vmem_limit_bytes=VMEM_LIMIT),
    )(z, z, z, la, sprev, do)


def _local_step(x, p, tgt, w, sp):
    t = x.shape[0]
    tb, nb = _fox_blocks(t)

    n1 = _norm_fwd(x, sp["ffn1_norm"], "norm1_fwd")
    gu1 = _mm(n1, w["gu1"], name="mm_gu")
    a1 = _act_fwd(gu1, "act_fwd")
    f1 = _mm(a1, w["d1"], out_dtype=F32, name="mm_down")
    h1, u = _resnorm_fwd(x, f1, sp["mix_norm"], 0.5, "resnorm_fwd_half")
    z = _mm(u, w["big"], name="mm_in")
    s = _mm(u, w["sm"], out_dtype=F32, name="mm_in_small")
    lf, la = _small_fwd(s, sp["fb"], sp["gup"], sp["gb"], "small_fwd")
    fp = _cumsum_tokens(lf, False, "cumsum_fwd")
    fh = fp[:, :FOX_HEADS].T
    fcol, frow = fh.reshape(FOX_HEADS, t, 1), fh.reshape(FOX_HEADS, nb, 1, tb)
    y_fox, lse = _fox_fwd(z, fcol, frow, "fox_fwd")
    o_gla, sprev = _gla_fwd(z, la, "gla_fwd")
    o4 = o_gla.reshape(t * GLA_HEADS, GLA_VAL_DIM)
    gr4 = z[:, Z_GR:Z_GR + GLA_VW].reshape(t * GLA_HEADS, GLA_VAL_DIM)
    y_gla = _gla_out_fwd(o4, gr4, sp["ghn"], "gla_out_fwd").reshape(t, GLA_VW)
    bf = _mm(y_fox, w["bf"], name="mm_branch")
    bg = _mm(y_gla, w["bg"], name="mm_branch")
    merged = _merge_fwd(z, bf, bg, sp["bm"], "merge_fwd")
    mo = _mm(merged, w["out"], out_dtype=F32, name="mm_out")
    h2, n2 = _resnorm_fwd(h1, mo, sp["ffn2_norm"], 1.0, "resnorm_fwd_one")
    gu2 = _mm(n2, w["gu2"], name="mm_gu")
    a2 = _act_fwd(gu2, "act_fwd")
    f2 = _mm(a2, w["d2"], out_dtype=F32, name="mm_down")
    h3, n4 = _resnorm_fwd(h2, f2, sp["ple_norm"], 0.5, "resnorm_fwd_half")
    pgl = _mm(n4, w["pg"], name="mm_pg")
    pb = p.astype(BF16)
    pp = _mm(pb, w["pp"], name="mm_pp")

    dh3, dpgl, dpp, loss, d_final = _head(h3, pgl, pp, tgt, sp["final_norm"], "head")
    dw, ds_ = {}, {"final_norm": d_final}
    dw["pg"] = _mm(n4, dpgl, ta=True, name="mm_dw_sq")
    dw["pp"] = _mm(pb, dpp, ta=True, name="mm_dw_pp")
    dn4 = _mm(dpgl, w["pg"], tb=True, out_dtype=F32, name="mm_dx_sq_f32")
    dh3, df2, ds_["ple_norm"] = _norm_bwd(h3, [dn4], dh3, sp["ple_norm"], 0.5, "norm_bwd_1")

    def ffn_bwd(n, gu, a, df, wgu, wd):
        dwd = _mm(a, df, ta=True, name="mm_dw_down")
        da = _mm(df, wd, tb=True, name="mm_dx_down")
        dgu = _act_bwd(gu, da, "act_bwd")
        dwgu = _mm(n, dgu, ta=True, name="mm_dw_gu")
        dn = _mm(dgu, wgu, tb=True, out_dtype=F32, name="mm_dx_gu")
        return dwgu, dwd, dn

    dw["gu2"], dw["d2"], dn2 = ffn_bwd(n2, gu2, a2, df2, w["gu2"], w["d2"])
    dh2, dmix, ds_["ffn2_norm"] = _norm_bwd(h2, [dn2], dh3, sp["ffn2_norm"], 1.0, "norm_bwd_1")

    dw["out"] = _mm(merged, dmix, ta=True, name="mm_dw_sq")
    dmerged = _mm(dmix, w["out"], tb=True, name="mm_dx_sq")
    dgl, dbf, dbg, ds_["bm"] = _merge_bwd(z, bf, bg, sp["bm"], dmerged, "merge_bwd")
    dw["bf"] = _mm(y_fox, dbf, ta=True, name="mm_dw_branch")
    dw["bg"] = _mm(y_gla, dbg, ta=True, name="mm_dw_branch")
    dy_fox = _mm(dbf, w["bf"], tb=True, name="mm_dx_branch")
    dy_gla = _mm(dbg, w["bg"], tb=True, name="mm_dx_branch")

    do4, dgr4, ds_["ghn"] = _gla_out_bwd(o4, gr4, sp["ghn"], dy_gla.reshape(t * GLA_HEADS, GLA_VAL_DIM), "gla_out_bwd")
    dgq, dgk, dgv, dla = _gla_bwd(z, la, sprev, do4.reshape(t, GLA_VW), "gla_bwd")
    dfq, delta, dfcol = _fox_bwd_q(z, y_fox, dy_fox, lse, fcol, frow, "fox_bwd_q")
    dfk, dfv, dfrow = _fox_bwd_kv(z, dy_fox, lse, delta, fcol, frow, "fox_bwd_kv")
    dfp = jnp.pad((dfrow.reshape(FOX_HEADS, t) + dfcol.reshape(FOX_HEADS, t)).T, ((0, 0), (0, SMALL_W - FOX_HEADS)))
    dlf = _cumsum_tokens(dfp, True, "cumsum_bwd")
    dsm, ds_["fb"], ds_["gup"], ds_["gb"] = _small_bwd(s, sp["fb"], sp["gup"], sp["gb"], dlf, dla, "small_bwd")
    dz = jnp.concatenate([dfq, dfk, dfv, dgq, dgk, dgv, dgr4.reshape(t, GLA_VW), dgl], axis=1)
    dw["big"] = _mm(u, dz, ta=True, name="mm_dw_in")
    dw["sm"] = _mm(u, dsm, ta=True, out_dtype=F32, name="mm_dw_in_small")
    du1 = _mm(dz, w["big"], tb=True, out_dtype=F32, name="mm_dx_in")
    du2 = _mm(dsm, w["sm"], tb=True, out_dtype=F32, name="mm_dx_in_small")
    dh1, df1, ds_["mix_norm"] = _norm_bwd(h1, [du1, du2], dh2, sp["mix_norm"], 0.5, "norm_bwd_2")

    dw["gu1"], dw["d1"], dn1 = ffn_bwd(n1, gu1, a1, df1, w["gu1"], w["d1"])
    grad_x, _, ds_["ffn1_norm"] = _norm_bwd(x, [dn1], dh1, sp["ffn1_norm"], 1.0, "norm_bwd_1")
    return loss, grad_x, dw, ds_


def _all_gather(shards, name):
    n = len(shards)

    def body(*refs):
        ins, outs = refs[:n], refs[n:2 * n]
        send_sems, recv_sems, loc_sems = refs[2 * n:]
        x, y, c = lax.axis_index("x"), lax.axis_index("y"), lax.axis_index("c")
        chips = [(1 - x, y), (x, 1 - y), (1 - x, 1 - y)]

        def remote(wi, j, slot):
            return pltpu.make_async_remote_copy(
                src_ref=ins[wi], dst_ref=outs[wi].at[slot], send_sem=send_sems.at[3 * wi + j], recv_sem=recv_sems.at[3 * wi + j],
                device_id=(chips[j][0], chips[j][1], c), device_id_type=MESH)

        local = [pltpu.make_async_copy(ins[wi], outs[wi].at[2 * x + y], loc_sems.at[wi]) for wi in range(n)]
        sends = [remote(wi, j, 2 * x + y) for wi in range(n) for j in range(3)]
        for cp in local + sends:
            cp.start()
        for wi in range(n):
            for j in range(3):
                remote(wi, j, 2 * chips[j][0] + chips[j][1]).wait_recv()
        for cp in sends:
            cp.wait_send()
        for cp in local:
            cp.wait()

    return pl.pallas_call(
        body, name=name, in_specs=[ANY] * n, out_specs=[ANY] * n,
        out_shape=[jax.ShapeDtypeStruct((4,) + s.shape, s.dtype) for s in shards],
        scratch_shapes=[pltpu.SemaphoreType.DMA((3 * n,)), pltpu.SemaphoreType.DMA((3 * n,)), pltpu.SemaphoreType.DMA((n,))],
        compiler_params=pltpu.CompilerParams(has_side_effects=True),
    )(*shards)


def _exchange(parts, name):
    n = len(parts)

    def body(*refs):
        ins, outs = refs[:n], refs[n:2 * n]
        send_sems, recv_sems, loc_sems = refs[2 * n:]
        x, y, c = lax.axis_index("x"), lax.axis_index("y"), lax.axis_index("c")
        me = 4 * x + 2 * y + c
        peers = [(x ^ dx, y ^ dy, c ^ dc) for dx in (0, 1) for dy in (0, 1) for dc in (0, 1)][1:]

        def src(wi, chip):
            return ins[wi].at[chip if parts[wi].shape[0] == 4 else 0]

        def remote(wi, j, slot):
            px, py, pc = peers[j]
            return pltpu.make_async_remote_copy(
                src_ref=src(wi, 2 * px + py), dst_ref=outs[wi].at[slot], send_sem=send_sems.at[7 * wi + j], recv_sem=recv_sems.at[7 * wi + j],
                device_id=(px, py, pc), device_id_type=MESH)

        local = [pltpu.make_async_copy(src(wi, 2 * x + y), outs[wi].at[me], loc_sems.at[wi]) for wi in range(n)]
        sends = [remote(wi, j, me) for wi in range(n) for j in range(7)]
        for cp in local + sends:
            cp.start()
        for wi in range(n):
            for j, (px, py, pc) in enumerate(peers):
                remote(wi, j, 4 * px + 2 * py + pc).wait_recv()
        for cp in sends:
            cp.wait_send()
        for cp in local:
            cp.wait()

    return pl.pallas_call(
        body, name=name, in_specs=[ANY] * n, out_specs=[ANY] * n,
        out_shape=[jax.ShapeDtypeStruct((8,) + s.shape[1:], s.dtype) for s in parts],
        scratch_shapes=[pltpu.SemaphoreType.DMA((7 * n,)), pltpu.SemaphoreType.DMA((7 * n,)), pltpu.SemaphoreType.DMA((n,))],
        compiler_params=pltpu.CompilerParams(has_side_effects=True),
    )(*parts)


def _adamw(parts, w, m, v, name):
    r, c = w.shape
    tr = r if r <= 8 else _pick(r, max(8, (1 << 19) // c), 8)

    def body(p_ref, w_ref, m_ref, v_ref, g_ref, d_ref, nm_ref, nv_ref):
        g = _f(p_ref[0])
        for k in range(1, 8):
            g = g + _f(p_ref[k])
        m_new = ADAM_B1 * m_ref[...] + (1.0 - ADAM_B1) * g
        v_new = ADAM_B2 * v_ref[...] + (1.0 - ADAM_B2) * jnp.square(g)
        m_hat = m_new / (1.0 - ADAM_B1 ** ADAM_STEP)
        v_hat = v_new / (1.0 - ADAM_B2 ** ADAM_STEP)
        g_ref[...] = g
        d_ref[...] = -ADAM_LR * (m_hat / (jnp.sqrt(v_hat) + ADAM_EPS) + ADAM_WD * w_ref[...])
        nm_ref[...] = m_new
        nv_ref[...] = v_new

    blk = pl.BlockSpec((tr, c), lambda i: (i, 0))
    return pl.pallas_call(
        body, name=name, grid=(r // tr,),
        in_specs=[pl.BlockSpec((8, tr, c), lambda i: (0, i, 0)), blk, blk, blk], out_specs=[blk] * 4,
        out_shape=[jax.ShapeDtypeStruct((r, c), F32)] * 4,
        compiler_params=pltpu.CompilerParams(dimension_semantics=("parallel",), vmem_limit_bytes=VMEM_LIMIT),
    )(parts, w, m, v)


BIG = ["ffn1_w_gate", "ffn1_w_up", "ffn1_w_down", "w_in", "gla_gate_up", "w_branch_fox", "w_branch_gla", "w_merge_gate", "w_out",
       "ffn2_w_gate", "ffn2_w_up", "ffn2_w_down", "w_ple_proj", "w_ple_gate"]
ROW_SHARDED = ("ffn1_w_down", "w_out", "ffn2_w_down", "w_ple_gate")
SMALL = ["ffn1_norm", "mix_norm", "fox_forget_bias", "gla_gate_bias", "gla_head_norm", "b_merge_gate", "ffn2_norm", "ple_norm", "final_norm"]
NAMES = ["ffn1_norm", "ffn1_w_gate", "ffn1_w_up", "ffn1_w_down", "mix_norm", "w_in", "fox_forget_bias", "gla_gate_up", "gla_gate_bias",
         "gla_head_norm", "w_branch_fox", "w_branch_gla", "w_merge_gate", "b_merge_gate", "w_out", "ffn2_norm", "ffn2_w_gate", "ffn2_w_up",
         "ffn2_w_down", "ple_norm", "w_ple_proj", "w_ple_gate", "final_norm"]
W_IN_COLS = (FOX_W, FOX_W, FOX_W, FOX_HEADS, GLA_KW, GLA_KW, GLA_VW, GLA_VW, GLA_RANK)


def _full_matrix(g, name):
    if name in ROW_SHARDED:
        return g.reshape(g.shape[0] * g.shape[1], g.shape[2])
    return jnp.transpose(g, (1, 0, 2)).reshape(g.shape[1], g.shape[0] * g.shape[2])


def _shard_parts(full, name):
    if name in ROW_SHARDED:
        return full.reshape(4, full.shape[0] // 4, full.shape[1])
    return jnp.transpose(full.reshape(full.shape[0], 4, full.shape[1] // 4), (1, 0, 2))


def _gathered_weights(full):
    w_in = full["w_in"]
    offs = [0]
    for cw in W_IN_COLS:
        offs.append(offs[-1] + cw)
    col = lambda i: w_in[:, offs[i]:offs[i + 1]]
    big = jnp.concatenate([col(0), col(1), col(2), col(4), col(5), col(6), col(7), full["w_merge_gate"]], axis=1)
    sm = jnp.concatenate([col(3), col(8), jnp.zeros((D_MODEL, SMALL_W - FOX_HEADS - GLA_RANK), BF16)], axis=1)
    return {
        "gu1": jnp.concatenate([full["ffn1_w_gate"], full["ffn1_w_up"]], axis=1), "d1": full["ffn1_w_down"],
        "big": big, "sm": sm, "bf": full["w_branch_fox"], "bg": full["w_branch_gla"], "out": full["w_out"],
        "gu2": jnp.concatenate([full["ffn2_w_gate"], full["ffn2_w_up"]], axis=1), "d2": full["ffn2_w_down"],
        "pp": full["w_ple_proj"], "pg": full["w_ple_gate"],
    }


def _whole_gradients(dw, dgup):
    ff = dw["d1"].shape[0]
    big, sm = dw["big"], dw["sm"].astype(BF16)
    w_in = jnp.concatenate([big[:, Z_FQ:Z_GQ], sm[:, :FOX_HEADS], big[:, Z_GQ:Z_GL], sm[:, FOX_HEADS:FOX_HEADS + GLA_RANK]], axis=1)
    return {
        "ffn1_w_gate": dw["gu1"][:, :ff], "ffn1_w_up": dw["gu1"][:, ff:], "ffn1_w_down": dw["d1"],
        "w_in": w_in, "gla_gate_up": dgup[FOX_HEADS:FOX_HEADS + GLA_RANK].astype(BF16),
        "w_branch_fox": dw["bf"], "w_branch_gla": dw["bg"], "w_merge_gate": big[:, Z_GL:], "w_out": dw["out"],
        "ffn2_w_gate": dw["gu2"][:, :ff], "ffn2_w_up": dw["gu2"][:, ff:], "ffn2_w_down": dw["d2"],
        "w_ple_proj": dw["pp"], "w_ple_gate": dw["pg"],
    }


def _pad_lanes(a, width):
    return jnp.pad(a, ((0, 0), (0, width - a.shape[1])))


def kernel(x, p, ffn1_norm, ffn1_w_gate, ffn1_w_up, ffn1_w_down, mix_norm, w_in, fox_forget_bias, gla_gate_up, gla_gate_bias, gla_head_norm, w_branch_fox, w_branch_gla, w_merge_gate, b_merge_gate, w_out, ffn2_norm, ffn2_w_gate, ffn2_w_up, ffn2_w_down, ple_norm, w_ple_proj, w_ple_gate, final_norm, loss_target, m_ffn1_norm, m_ffn1_w_gate, m_ffn1_w_up, m_ffn1_w_down, m_mix_norm, m_w_in, m_fox_forget_bias, m_gla_gate_up, m_gla_gate_bias, m_gla_head_norm, m_w_branch_fox, m_w_branch_gla, m_w_merge_gate, m_b_merge_gate, m_w_out, m_ffn2_norm, m_ffn2_w_gate, m_ffn2_w_up, m_ffn2_w_down, m_ple_norm, m_w_ple_proj, m_w_ple_gate, m_final_norm, v_ffn1_norm, v_ffn1_w_gate, v_ffn1_w_up, v_ffn1_w_down, v_mix_norm, v_w_in, v_fox_forget_bias, v_gla_gate_up, v_gla_gate_bias, v_gla_head_norm, v_w_branch_fox, v_w_branch_gla, v_w_merge_gate, v_b_merge_gate, v_w_out, v_ffn2_norm, v_ffn2_w_gate, v_ffn2_w_up, v_ffn2_w_down, v_ple_norm, v_w_ple_proj, v_w_ple_gate, v_final_norm):
    args = dict(locals())
    wts = {n: args[n] for n in NAMES}
    mom = {n: args["m_" + n] for n in NAMES}
    var = {n: args["v_" + n] for n in NAMES}
    two_d = lambda a: a.reshape(-1, a.shape[-1])

    gathered = _all_gather([two_d(wts[n]).astype(BF16) for n in BIG], "all_gather")
    full = {n: _full_matrix(g, n) for n, g in zip(BIG, gathered)}
    w = _gathered_weights(full)
    gup = jnp.zeros((SMALL_W, GLA_KW), F32).at[FOX_HEADS:FOX_HEADS + GLA_RANK].set(full["gla_gate_up"].astype(F32))
    sp = {
        "ffn1_norm": two_d(ffn1_norm), "mix_norm": two_d(mix_norm), "fb": _pad_lanes(two_d(fox_forget_bias), SMALL_W), "gup": gup,
        "gb": two_d(gla_gate_bias), "ghn": two_d(gla_head_norm), "bm": two_d(b_merge_gate), "ffn2_norm": two_d(ffn2_norm),
        "ple_norm": two_d(ple_norm), "final_norm": two_d(final_norm),
    }

    loss, grad_x, dw, ds_ = _local_step(x[0], p[0, 0], loss_target[0], w, sp)

    whole = _whole_gradients(dw, ds_["gup"])
    small_g = {"ffn1_norm": ds_["ffn1_norm"], "mix_norm": ds_["mix_norm"], "fox_forget_bias": ds_["fb"][:, :FOX_HEADS],
               "gla_gate_bias": ds_["gb"], "gla_head_norm": ds_["ghn"], "b_merge_gate": ds_["bm"], "ffn2_norm": ds_["ffn2_norm"],
               "ple_norm": ds_["ple_norm"], "final_norm": ds_["final_norm"]}
    pack = lambda d: jnp.concatenate([two_d(d[n]) for n in SMALL], axis=1)
    small_w = sum(two_d(wts[n]).shape[1] for n in SMALL)
    small_pad = -small_w % 1024
    packed = lambda d: _pad_lanes(pack(d), small_w + small_pad)
    received = _exchange([_shard_parts(whole[n], n) for n in BIG] + [packed(small_g)[None]], "exchange")

    out = {}
    for n, parts in zip(BIG, received[:-1]):
        out[n] = [r.reshape(wts[n].shape) for r in _adamw(parts, two_d(wts[n]), two_d(mom[n]), two_d(var[n]), "adamw_" + n)]
    small_out = _adamw(received[-1], packed(wts), packed(mom), packed(var), "adamw_small")
    off = 0
    for n in SMALL:
        cw = two_d(wts[n]).shape[1]
        out[n] = [r[:, off:off + cw].reshape(wts[n].shape) for r in small_out]
        off += cw

    total = lax.psum(loss[0, 0], ("x", "y", "c"))
    return (total, grad_x[None], *[out[n][0] for n in NAMES], *[out[n][1] for n in NAMES],
            *[out[n][2] for n in NAMES], *[out[n][3] for n in NAMES])
```

```python
import functools

import jax
import jax.numpy as jnp
from jax import lax
from jax.experimental import pallas as pl
from jax.experimental.pallas import tpu as pltpu

F32 = jnp.float32
BF16 = jnp.bfloat16
MESH = pl.DeviceIdType.MESH
ANY = pl.BlockSpec(memory_space=pl.ANY)

D_MODEL = 2048
FOX_HEADS = 8
HEAD_DIM = 128
GLA_HEADS = 4
GLA_VAL_DIM = 256
GLA_RANK = 16
GLA_TAU = 16.0
CHUNK = 64
EPS = 1e-6
FOX_W = FOX_HEADS * HEAD_DIM
GLA_KW = GLA_HEADS * HEAD_DIM
GLA_VW = GLA_HEADS * GLA_VAL_DIM
Z_FQ, Z_FK, Z_FV, Z_GQ, Z_GK, Z_GV, Z_GR, Z_GL = 0, 1024, 2048, 3072, 3584, 4096, 5120, 6144
Z_W = Z_GL + 2 * D_MODEL
SMALL_W = 128
NEG = -1e30

ADAM_LR, ADAM_B1, ADAM_B2, ADAM_EPS, ADAM_WD, ADAM_STEP = 0.001, 0.9, 0.999, 1e-08, 0.01, 10

VMEM_LIMIT = 56 * 1024 * 1024


def _pick(n, target, mult=128):
    if n <= target:
        return n
    best = None
    for d in range(mult, target + 1, mult):
        if n % d == 0:
            best = d
    assert best is not None, (n, target)
    return best


def _mm(a, b, *, ta=False, tb=False, out_dtype=BF16, name):
    m, k = (a.shape[1], a.shape[0]) if ta else a.shape
    n = b.shape[0] if tb else b.shape[1]
    assert (b.shape[1] if tb else b.shape[0]) == k
    bm, bn, bk = _pick(m, 1024), _pick(n, 512), _pick(k, 2048)
    nk = k // bk
    dims = (((0 if ta else 1,), (1 if tb else 0,)), ((), ()))

    def body(a_ref, b_ref, o_ref, acc_ref):
        part = lax.dot_general(a_ref[...], b_ref[...], dims, preferred_element_type=F32)
        if nk == 1:
            o_ref[...] = part.astype(o_ref.dtype)
            return
        kk = pl.program_id(2)

        @pl.when(kk == 0)
        def _():
            acc_ref[...] = part

        @pl.when(kk > 0)
        def _():
            acc_ref[...] += part

        @pl.when(kk == nk - 1)
        def _():
            o_ref[...] = acc_ref[...].astype(o_ref.dtype)

    a_spec = pl.BlockSpec((bk, bm), lambda i, j, kk: (kk, i)) if ta else pl.BlockSpec((bm, bk), lambda i, j, kk: (i, kk))
    b_spec = pl.BlockSpec((bn, bk), lambda i, j, kk: (j, kk)) if tb else pl.BlockSpec((bk, bn), lambda i, j, kk: (kk, j))
    return pl.pallas_call(
        body, name=name, grid=(m // bm, n // bn, nk),
        in_specs=[a_spec, b_spec], out_specs=pl.BlockSpec((bm, bn), lambda i, j, kk: (i, j)),
        out_shape=jax.ShapeDtypeStruct((m, n), out_dtype),
        scratch_shapes=[pltpu.VMEM((bm, bn), F32)],
        compiler_params=pltpu.CompilerParams(dimension_semantics=("parallel", "parallel", "arbitrary"), vmem_limit_bytes=VMEM_LIMIT),
    )(a, b)


def _rowwise(fn, tiled, bcast, outs, reds=(), *, tt, name):
    t = tiled[0][0].shape[0]
    tt = min(tt, t)
    nin, nout = len(tiled) + len(bcast), len(outs)
    splits = [s[3] for s in tiled] + [s[1] for s in bcast]

    def store(ref, val, acc):
        off = 0
        for piece in val if isinstance(val, (tuple, list)) else (val,):
            w = piece.shape[-1]
            if acc:
                ref[:, off:off + w] += piece.astype(ref.dtype)
            else:
                ref[:, off:off + w] = piece.astype(ref.dtype)
            off += w
        assert off == ref.shape[-1], (name, off, ref.shape)

    def body(*refs):
        args = []
        for ref, sp in zip(refs[:nin], splits):
            if sp is None:
                args.append(ref[...])
            else:
                off = 0
                for w in sp:
                    args.append(ref[:, off:off + w])
                    off += w
        res = fn(*args)
        res = res if isinstance(res, (tuple, list)) else (res,)
        assert len(res) == nout + len(reds), (name, len(res))
        for ref, val in zip(refs[nin:nin + nout], res[:nout]):
            store(ref, val, False)
        if reds:
            @pl.when(pl.program_id(0) == 0)
            def _():
                for ref in refs[nin + nout:]:
                    ref[...] = jnp.zeros(ref.shape, ref.dtype)
            for ref, val in zip(refs[nin + nout:], res[nout:]):
                store(ref, val, True)

    in_specs = [pl.BlockSpec((tt, w), functools.partial(lambda i, cb: (i, cb), cb=cb)) for (_, w, cb, _) in tiled]
    in_specs += [pl.BlockSpec(arr.shape, lambda i: (0, 0)) for (arr, _) in bcast]
    out_specs = [pl.BlockSpec((tt, w), lambda i: (i, 0)) for (w, _) in outs]
    out_specs += [pl.BlockSpec((r, w), lambda i: (0, 0)) for (r, w) in reds]
    out_shape = [jax.ShapeDtypeStruct((t, w), dt) for (w, dt) in outs] + [jax.ShapeDtypeStruct((r, w), F32) for (r, w) in reds]
    return pl.pallas_call(
        body, name=name, grid=(t // tt,), in_specs=in_specs, out_specs=out_specs, out_shape=out_shape,
        compiler_params=pltpu.CompilerParams(dimension_semantics=("arbitrary" if reds else "parallel",), vmem_limit_bytes=VMEM_LIMIT),
    )(*[s[0] for s in tiled], *[s[0] for s in bcast])


def _full(arr):
    return (arr, arr.shape[1], 0, None)


def _f(x):
    return x.astype(F32)


def _rms(x, g):
    return x * lax.rsqrt(jnp.mean(x * x, axis=-1, keepdims=True) + EPS) * g


def _log_sigmoid(x):
    return jnp.minimum(x, 0.0) - jnp.log1p(jnp.exp(-jnp.abs(x)))


def _silu(x):
    return x * jax.nn.sigmoid(x)


def _norm_fwd(x, g, name):
    return _rowwise(lambda xb, gb: _rms(_f(xb), gb), [_full(x)], [(g, None)], [(x.shape[1], BF16)], tt=256, name=name)[0]


def _resnorm_fwd(res, branch, g, coef, name):
    def fn(rb, bb, gb):
        h = rb + coef * _f(bb)
        return h, _rms(h, gb)
    d = res.shape[1]
    return _rowwise(fn, [_full(res), _full(branch)], [(g, None)], [(d, F32), (d, BF16)], tt=256, name=name)


def _norm_bwd(h, dns, dres, g, coef, name):
    nd = len(dns)

    def fn(hb, *rest):
        dn = _f(rest[0])
        for extra in rest[1:nd]:
            dn = dn + _f(extra)
        dr, gb = rest[nd], rest[nd + 1]
        _, vjp = jax.vjp(_rms, hb, gb)
        dh, dg = vjp(dn)
        dh = dh + dr
        return dh, coef * dh, dg
    d = h.shape[1]
    return _rowwise(fn, [_full(h)] + [_full(x) for x in dns] + [_full(dres)], [(g, None)],
                    [(d, F32), (d, BF16)], [(1, d)], tt=256, name=name)


def _act_fwd(gu, name):
    ff = gu.shape[1] // 2
    return _rowwise(lambda gb, ub: _silu(_f(gb)) * _f(ub), [(gu, 2 * ff, 0, (ff, ff))], [], [(ff, BF16)], tt=256, name=name)[0]


def _act_bwd(gu, da, name):
    ff = gu.shape[1] // 2

    def fn(gb, ub, dab):
        _, vjp = jax.vjp(lambda p, q: _silu(p) * q, _f(gb), _f(ub))
        return (vjp(_f(dab)),)
    return _rowwise(fn, [(gu, 2 * ff, 0, (ff, ff)), _full(da)], [], [(2 * ff, BF16)], tt=128, name=name)[0]


def _merge(glf, glg, bf, bg, bmf, bmg):
    return jax.nn.sigmoid(_f(glf) + bmf) * _f(bf) + jax.nn.sigmoid(_f(glg) + bmg) * _f(bg)


def _merge_fwd(z, bf, bg, bm, name):
    d = D_MODEL
    return _rowwise(_merge, [(z, d, Z_GL // d, None), (z, d, Z_GL // d + 1, None), _full(bf), _full(bg)], [(bm, (d, d))],
                    [(d, BF16)], tt=256, name=name)[0]


def _merge_bwd(z, bf, bg, bm, dm, name):
    d = D_MODEL

    def fn(glf, glg, bfb, bgb, dmb, bmf, bmg):
        _, vjp = jax.vjp(_merge, _f(glf), _f(glg), _f(bfb), _f(bgb), bmf, bmg)
        dglf, dglg, dbf, dbg, dbmf, dbmg = vjp(_f(dmb))
        return (dglf, dglg), dbf, dbg, (dbmf, dbmg)
    return _rowwise(fn, [(z, d, Z_GL // d, None), (z, d, Z_GL // d + 1, None), _full(bf), _full(bg), _full(dm)], [(bm, (d, d))],
                    [(2 * d, BF16), (d, BF16), (d, BF16)], [(1, 2 * d)], tt=128, name=name)


def _gla_out(o, gr, g):
    return _rms(o, g) * _silu(_f(gr))


def _gla_out_fwd(o4, gr4, g, name):
    return _rowwise(lambda ob, rb, gb: _gla_out(ob, rb, gb), [_full(o4), _full(gr4)], [(g, None)], [(GLA_VAL_DIM, BF16)], tt=1024, name=name)[0]


def _gla_out_bwd(o4, gr4, g, dy4, name):
    def fn(ob, rb, dyb, gb):
        _, vjp = jax.vjp(_gla_out, ob, _f(rb), gb)
        return vjp(_f(dyb))
    return _rowwise(fn, [_full(o4), _full(gr4), _full(dy4)], [(g, None)], [(GLA_VAL_DIM, F32), (GLA_VAL_DIM, BF16)], [(1, GLA_VAL_DIM)],
                    tt=1024, name=name)


def _small_gates(s, fb, gup, gb):
    lane = lax.broadcasted_iota(jnp.int32, s.shape, 1)
    lf = jnp.where(lane < FOX_HEADS, _log_sigmoid(s + fb), 0.0)
    pre = jnp.dot(s.astype(BF16), gup.astype(BF16), preferred_element_type=F32) + gb
    return lf, _log_sigmoid(pre) / GLA_TAU


def _small_fwd(s, fb, gup, gb, name):
    return _rowwise(_small_gates, [_full(s)], [(fb, None), (gup, None), (gb, None)], [(SMALL_W, F32), (GLA_KW, F32)], tt=256, name=name)


def _small_bwd(s, fb, gup, gb, dlf, dla, name):
    def fn(sb, dlfb, dlab, fbb, gupb, gbb):
        _, vjp = jax.vjp(_small_gates, sb, fbb, gupb, gbb)
        return vjp((dlfb, dlab))
    return _rowwise(fn, [_full(s), _full(dlf), _full(dla)], [(fb, None), (gup, None), (gb, None)],
                    [(SMALL_W, BF16)], [(1, SMALL_W), (SMALL_W, GLA_KW), (1, GLA_KW)], tt=256, name=name)


def _head_fn(h3, pgl, pp, tgt, gf):
    h4 = h3 + jax.nn.sigmoid(pgl) * pp
    err = _rms(h4, gf) - tgt
    return 0.5 * jnp.sum(jnp.mean(err * err, axis=-1, keepdims=True))


def _head(h3, pgl, pp, tgt, gf, name):
    def fn(hb, gl, pb, tb, gfb):
        loss, vjp = jax.vjp(_head_fn, hb, _f(gl), _f(pb), tb, gfb)
        dh, dgl, dpp, _, dgf = vjp(jnp.ones((), F32))
        return dh, dgl, dpp, jnp.full((1, 128), loss, F32), dgf
    d = h3.shape[1]
    return _rowwise(fn, [_full(h3), _full(pgl), _full(pp), _full(tgt)], [(gf, None)],
                    [(d, F32), (d, BF16), (d, BF16)], [(1, 128), (1, d)], tt=256, name=name)


def _cumsum_tokens(a, reverse, name):
    t, w = a.shape
    r = min(256, t)
    nb = t // r

    def body(a_ref, o_ref, carry_ref):
        @pl.when(pl.program_id(0) == 0)
        def _():
            carry_ref[...] = jnp.zeros(carry_ref.shape, F32)
        row = lax.broadcasted_iota(jnp.int32, (r, r), 0)
        col = lax.broadcasted_iota(jnp.int32, (r, r), 1)
        tri = ((col >= row) if reverse else (col <= row)).astype(F32)
        blk = a_ref[...]
        o_ref[...] = jnp.dot(tri, blk, preferred_element_type=F32, precision=lax.Precision.HIGHEST) + carry_ref[...]
        carry_ref[...] += jnp.sum(blk, axis=0, keepdims=True)

    idx = (lambda i: (nb - 1 - i, 0)) if reverse else (lambda i: (i, 0))
    return pl.pallas_call(
        body, name=name, grid=(nb,), in_specs=[pl.BlockSpec((r, w), idx)], out_specs=pl.BlockSpec((r, w), idx),
        out_shape=jax.ShapeDtypeStruct((t, w), F32), scratch_shapes=[pltpu.VMEM((1, w), F32)],
        compiler_params=pltpu.CompilerParams(dimension_semantics=("arbitrary",)),
    )(a)


def _fox_blocks(t):
    tb = min(256, t)
    return tb, t // tb


def _fox_logits(q, k, fc, fr, i, j, tb):
    s = lax.dot_general(q, k, (((1,), (1,)), ((), ())), preferred_element_type=F32) * (HEAD_DIM ** -0.5)
    s = s + fc - fr
    row = i * tb + lax.broadcasted_iota(jnp.int32, (tb, tb), 0)
    col = j * tb + lax.broadcasted_iota(jnp.int32, (tb, tb), 1)
    return s, col <= row


def _fox_fwd(z, fcol, frow, name):
    t = z.shape[0]
    tb, nb = _fox_blocks(t)
    qb, kb, vb = Z_FQ // HEAD_DIM, Z_FK // HEAD_DIM, Z_FV // HEAD_DIM

    def body(q_ref, k_ref, v_ref, fc_ref, fr_ref, o_ref, lse_ref):
        i = pl.program_id(1)
        q = q_ref[...]
        fc = fc_ref[...]

        def step(j, carry):
            m, l, acc = carry
            rows = pl.ds(pl.multiple_of(j * tb, tb), tb)
            s, ok = _fox_logits(q, k_ref[rows, :], fc, fr_ref[j], i, j, tb)
            s = jnp.where(ok, s, NEG)
            m_new = jnp.maximum(m, jnp.max(s, axis=1, keepdims=True))
            alpha = jnp.exp(m - m_new)
            p = jnp.exp(s - m_new)
            l = alpha * l + jnp.sum(p, axis=1, keepdims=True)
            acc = alpha * acc + jnp.dot(p.astype(BF16), v_ref[rows, :], preferred_element_type=F32)
            return m_new, l, acc

        m, l, acc = lax.fori_loop(0, i + 1, step, (jnp.full((tb, 1), NEG, F32), jnp.zeros((tb, 1), F32), jnp.zeros((tb, HEAD_DIM), F32)))
        o_ref[...] = (acc / l).astype(o_ref.dtype)
        lse_ref[...] = m + jnp.log(l)

    return pl.pallas_call(
        body, name=name, grid=(FOX_HEADS, nb),
        in_specs=[pl.BlockSpec((tb, HEAD_DIM), lambda h, i: (i, qb + h)),
                  pl.BlockSpec((t, HEAD_DIM), lambda h, i: (0, kb + h)),
                  pl.BlockSpec((t, HEAD_DIM), lambda h, i: (0, vb + h)),
                  pl.BlockSpec((None, tb, 1), lambda h, i: (h, i, 0)),
                  pl.BlockSpec((None, nb, 1, tb), lambda h, i: (h, 0, 0, 0))],
        out_specs=[pl.BlockSpec((tb, HEAD_DIM), lambda h, i: (i, h)), pl.BlockSpec((None, tb, 1), lambda h, i: (h, i, 0))],
        out_shape=[jax.ShapeDtypeStruct((t, FOX_W), BF16), jax.ShapeDtypeStruct((FOX_HEADS, t, 1), F32)],
        compiler_params=pltpu.CompilerParams(dimension_semantics=("parallel", "parallel"), vmem_limit_bytes=VMEM_LIMIT),
    )(z, z, z, fcol, frow)


def _fox_bwd_q(z, o, do, lse, fcol, frow, name):
    t = z.shape[0]
    tb, nb = _fox_blocks(t)
    qb, kb, vb = Z_FQ // HEAD_DIM, Z_FK // HEAD_DIM, Z_FV // HEAD_DIM

    def body(q_ref, k_ref, v_ref, o_ref, do_ref, lse_ref, fc_ref, fr_ref, dq_ref, delta_ref, dfc_ref):
        i = pl.program_id(1)
        q, do = q_ref[...], do_ref[...]
        fc, lse = fc_ref[...], lse_ref[...]
        delta = jnp.sum(_f(do) * _f(o_ref[...]), axis=1, keepdims=True)
        delta_ref[...] = delta

        def step(j, carry):
            dq, dfc = carry
            rows = pl.ds(pl.multiple_of(j * tb, tb), tb)
            k = k_ref[rows, :]
            s, ok = _fox_logits(q, k, fc, fr_ref[j], i, j, tb)
            p = jnp.where(ok, jnp.exp(s - lse), 0.0)
            dp = lax.dot_general(do, v_ref[rows, :], (((1,), (1,)), ((), ())), preferred_element_type=F32)
            ds = p * (dp - delta)
            return dq + jnp.dot(ds.astype(BF16), k, preferred_element_type=F32), dfc + jnp.sum(ds, axis=1, keepdims=True)

        dq, dfc = lax.fori_loop(0, i + 1, step, (jnp.zeros((tb, HEAD_DIM), F32), jnp.zeros((tb, 1), F32)))
        dq_ref[...] = (dq * (HEAD_DIM ** -0.5)).astype(dq_ref.dtype)
        dfc_ref[...] = dfc

    col = lambda h, i: (h, i, 0)
    return pl.pallas_call(
        body, name=name, grid=(FOX_HEADS, nb),
        in_specs=[pl.BlockSpec((tb, HEAD_DIM), lambda h, i: (i, qb + h)),
                  pl.BlockSpec((t, HEAD_DIM), lambda h, i: (0, kb + h)),
                  pl.BlockSpec((t, HEAD_DIM), lambda h, i: (0, vb + h)),
                  pl.BlockSpec((tb, HEAD_DIM), lambda h, i: (i, h)),
                  pl.BlockSpec((tb, HEAD_DIM), lambda h, i: (i, h)),
                  pl.BlockSpec((None, tb, 1), col), pl.BlockSpec((None, tb, 1), col),
                  pl.BlockSpec((None, nb, 1, tb), lambda h, i: (h, 0, 0, 0))],
        out_specs=[pl.BlockSpec((tb, HEAD_DIM), lambda h, i: (i, h)), pl.BlockSpec((None, tb, 1), col), pl.BlockSpec((None, tb, 1), col)],
        out_shape=[jax.ShapeDtypeStruct((t, FOX_W), BF16), jax.ShapeDtypeStruct((FOX_HEADS, t, 1), F32), jax.ShapeDtypeStruct((FOX_HEADS, t, 1), F32)],
        compiler_params=pltpu.CompilerParams(dimension_semantics=("parallel", "parallel"), vmem_limit_bytes=VMEM_LIMIT),
    )(z, z, z, o, do, lse, fcol, frow)


def _fox_bwd_kv(z, do, lse, delta, fcol, frow, name):
    t = z.shape[0]
    tb, nb = _fox_blocks(t)
    qb, kb, vb = Z_FQ // HEAD_DIM, Z_FK // HEAD_DIM, Z_FV // HEAD_DIM

    def body(q_ref, k_ref, v_ref, do_ref, lse_ref, delta_ref, fc_ref, fr_ref, dk_ref, dv_ref, dfr_ref):
        j = pl.program_id(1)
        k, v, fr = k_ref[...], v_ref[...], fr_ref[...]

        def step(i, carry):
            dk, dv, dfr = carry
            rows = pl.ds(pl.multiple_of(i * tb, tb), tb)
            q, do = q_ref[rows, :], do_ref[rows, :]
            s, ok = _fox_logits(q, k, fc_ref[rows, :], fr, i, j, tb)
            p = jnp.where(ok, jnp.exp(s - lse_ref[rows, :]), 0.0)
            dv = dv + lax.dot_general(p.astype(BF16), do, (((0,), (0,)), ((), ())), preferred_element_type=F32)
            dp = lax.dot_general(do, v, (((1,), (1,)), ((), ())), preferred_element_type=F32)
            ds = p * (dp - delta_ref[rows, :])
            dk = dk + lax.dot_general(ds.astype(BF16), q, (((0,), (0,)), ((), ())), preferred_element_type=F32)
            return dk, dv, dfr - jnp.sum(ds, axis=0, keepdims=True)

        zero = jnp.zeros((tb, HEAD_DIM), F32)
        dk, dv, dfr = lax.fori_loop(j, nb, step, (zero, zero, jnp.zeros((1, tb), F32)))
        dk_ref[...] = (dk * (HEAD_DIM ** -0.5)).astype(dk_ref.dtype)
        dv_ref[...] = dv.astype(dv_ref.dtype)
        dfr_ref[...] = dfr

    whole = lambda h, j: (h, 0, 0)
    return pl.pallas_call(
        body, name=name, grid=(FOX_HEADS, nb),
        in_specs=[pl.BlockSpec((t, HEAD_DIM), lambda h, j: (0, qb + h)),
                  pl.BlockSpec((tb, HEAD_DIM), lambda h, j: (j, kb + h)),
                  pl.BlockSpec((tb, HEAD_DIM), lambda h, j: (j, vb + h)),
                  pl.BlockSpec((t, HEAD_DIM), lambda h, j: (0, h)),
                  pl.BlockSpec((None, t, 1), whole), pl.BlockSpec((None, t, 1), whole), pl.BlockSpec((None, t, 1), whole),
                  pl.BlockSpec((None, None, 1, tb), lambda h, j: (h, j, 0, 0))],
        out_specs=[pl.BlockSpec((tb, HEAD_DIM), lambda h, j: (j, h)), pl.BlockSpec((tb, HEAD_DIM), lambda h, j: (j, h)),
                   pl.BlockSpec((None, None, 1, tb), lambda h, j: (h, j, 0, 0))],
        out_shape=[jax.ShapeDtypeStruct((t, FOX_W), BF16), jax.ShapeDtypeStruct((t, FOX_W), BF16),
                   jax.ShapeDtypeStruct((FOX_HEADS, nb, 1, tb), F32)],
        compiler_params=pltpu.CompilerParams(dimension_semantics=("parallel", "parallel"), vmem_limit_bytes=VMEM_LIMIT),
    )(z, z, z, do, lse, delta, fcol, frow)


def _gla_step(st, q, k, v, la):
    row = lax.broadcasted_iota(jnp.int32, (CHUNK, CHUNK), 0)
    col = lax.broadcasted_iota(jnp.int32, (CHUNK, CHUNK), 1)
    tri = (col <= row).astype(F32)
    a_cum = jnp.dot(tri, la, preferred_element_type=F32, precision=lax.Precision.HIGHEST)
    a_tot = jnp.sum(la, axis=0, keepdims=True)
    k_dec = (_f(k) * jnp.exp(a_tot - a_cum)).astype(BF16)
    qs = (_f(q) * (HEAD_DIM ** -0.5)).astype(BF16)
    st = st * jnp.exp(a_tot) + lax.dot_general(v.astype(BF16), k_dec, (((0,), (0,)), ((), ())), preferred_element_type=F32)
    o = lax.dot_general(qs, st.astype(BF16), (((1,), (1,)), ((), ())), preferred_element_type=F32)
    return st, o


def _gla_blocks(t):
    r = min(256, t)
    return r, t // r, r // CHUNK


def _gla_fwd(z, la, name):
    t = z.shape[0]
    r, nb, nch = _gla_blocks(t)

    def body(q_ref, k_ref, v_ref, la_ref, o_ref, sp_ref, st_ref):
        @pl.when(pl.program_id(0) == 0)
        def _():
            st_ref[...] = jnp.zeros(st_ref.shape, F32)
        for c in range(nch):
            rows = slice(c * CHUNK, (c + 1) * CHUNK)
            for h in range(GLA_HEADS):
                kc = slice(h * HEAD_DIM, (h + 1) * HEAD_DIM)
                vc = slice(h * GLA_VAL_DIM, (h + 1) * GLA_VAL_DIM)
                st = st_ref[h]
                sp_ref[c, h] = st
                st, o = _gla_step(st, q_ref[rows, kc], k_ref[rows, kc], v_ref[rows, vc], la_ref[rows, kc])
                st_ref[h] = st
                o_ref[rows, vc] = o

    return pl.pallas_call(
        body, name=name, grid=(nb,),
        in_specs=[pl.BlockSpec((r, GLA_KW), lambda i: (i, Z_GQ // GLA_KW)), pl.BlockSpec((r, GLA_KW), lambda i: (i, Z_GK // GLA_KW)),
                  pl.BlockSpec((r, GLA_VW), lambda i: (i, Z_GV // GLA_VW)), pl.BlockSpec((r, GLA_KW), lambda i: (i, 0))],
        out_specs=[pl.BlockSpec((r, GLA_VW), lambda i: (i, 0)),
                   pl.BlockSpec((nch, GLA_HEADS, GLA_VAL_DIM, HEAD_DIM), lambda i: (i, 0, 0, 0))],
        out_shape=[jax.ShapeDtypeStruct((t, GLA_VW), F32),
                   jax.ShapeDtypeStruct((t // CHUNK, GLA_HEADS, GLA_VAL_DIM, HEAD_DIM), F32)],
        scratch_shapes=[pltpu.VMEM((GLA_HEADS, GLA_VAL_DIM, HEAD_DIM), F32)],
        compiler_params=pltpu.CompilerParams(dimension_semantics=("arbitrary",), vmem_limit_bytes=VMEM_LIMIT),
    )(z, z, z, la)


def _gla_bwd(z, la, sprev, do, name):
    t = z.shape[0]
    r, nb, nch = _gla_blocks(t)

    def body(q_ref, k_ref, v_ref, la_ref, sp_ref, do_ref, dq_ref, dk_ref, dv_ref, dla_ref, dst_ref):
        @pl.when(pl.program_id(0) == 0)
        def _():
            dst_ref[...] = jnp.zeros(dst_ref.shape, F32)
        for c in reversed(range(nch)):
            rows = slice(c * CHUNK, (c + 1) * CHUNK)
            for h in range(GLA_HEADS):
                kc = slice(h * HEAD_DIM, (h + 1) * HEAD_DIM)
                vc = slice(h * GLA_VAL_DIM, (h + 1) * GLA_VAL_DIM)
                _, vjp = jax.vjp(_gla_step, sp_ref[c, h], q_ref[rows, kc], k_ref[rows, kc], v_ref[rows, vc], la_ref[rows, kc])
                dst, dq, dk, dv, dla = vjp((dst_ref[h], do_ref[rows, vc]))
                dst_ref[h] = dst
                dq_ref[rows, kc] = dq
                dk_ref[rows, kc] = dk
                dv_ref[rows, vc] = dv
                dla_ref[rows, kc] = dla

    rev = lambda i: (nb - 1 - i, 0)
    return pl.pallas_call(
        body, name=name, grid=(nb,),
        in_specs=[pl.BlockSpec((r, GLA_KW), lambda i: (nb - 1 - i, Z_GQ // GLA_KW)), pl.BlockSpec((r, GLA_KW), lambda i: (nb - 1 - i, Z_GK // GLA_KW)),
                  pl.BlockSpec((r, GLA_VW), lambda i: (nb - 1 - i, Z_GV // GLA_VW)), pl.BlockSpec((r, GLA_KW), rev),
                  pl.BlockSpec((nch, GLA_HEADS, GLA_VAL_DIM, HEAD_DIM), lambda i: (nb - 1 - i, 0, 0, 0)),
                  pl.BlockSpec((r, GLA_VW), rev)],
        out_specs=[pl.BlockSpec((r, GLA_KW), rev), pl.BlockSpec((r, GLA_KW), rev), pl.BlockSpec((r, GLA_VW), rev), pl.BlockSpec((r, GLA_KW), rev)],
        out_shape=[jax.ShapeDtypeStruct((t, GLA_KW), BF16), jax.ShapeDtypeStruct((t, GLA_KW), BF16),
                   jax.ShapeDtypeStruct((t, GLA_VW), BF16), jax.ShapeDtypeStruct((t, GLA_KW), F32)],
        scratch_shapes=[pltpu.VMEM((GLA_HEADS, GLA_VAL_DIM, HEAD_DIM), F32)],
        compiler_params=pltpu.CompilerParams(dimension_semantics=("arbitrary",), vmem_limit_bytes=VMEM_LIMIT),
    )(z, z, z, la, sprev, do)


def _local_step(x, p, tgt, w, sp):
    t = x.shape[0]
    tb, nb = _fox_blocks(t)

    n1 = _norm_fwd(x, sp["ffn1_norm"], "norm1_fwd")
    gu1 = _mm(n1, w["gu1"], name="mm_gu")
    a1 = _act_fwd(gu1, "act_fwd")
    f1 = _mm(a1, w["d1"], out_dtype=F32, name="mm_down")
    h1, u = _resnorm_fwd(x, f1, sp["mix_norm"], 0.5, "resnorm_fwd_half")
    z = _mm(u, w["big"], name="mm_in")
    s = _mm(u, w["sm"], out_dtype=F32, name="mm_in_small")
    lf, la = _small_fwd(s, sp["fb"], sp["gup"], sp["gb"], "small_fwd")
    fp = _cumsum_tokens(lf, False, "cumsum_fwd")
    fh = fp[:, :FOX_HEADS].T
    fcol, frow = fh.reshape(FOX_HEADS, t, 1), fh.reshape(FOX_HEADS, nb, 1, tb)
    y_fox, lse = _fox_fwd(z, fcol, frow, "fox_fwd")
    o_gla, sprev = _gla_fwd(z, la, "gla_fwd")
    o4 = o_gla.reshape(t * GLA_HEADS, GLA_VAL_DIM)
    gr4 = z[:, Z_GR:Z_GR + GLA_VW].reshape(t * GLA_HEADS, GLA_VAL_DIM)
    y_gla = _gla_out_fwd(o4, gr4, sp["ghn"], "gla_out_fwd").reshape(t, GLA_VW)
    bf = _mm(y_fox, w["bf"], name="mm_branch")
    bg = _mm(y_gla, w["bg"], name="mm_branch")
    merged = _merge_fwd(z, bf, bg, sp["bm"], "merge_fwd")
    mo = _mm(merged, w["out"], out_dtype=F32, name="mm_out")
    h2, n2 = _resnorm_fwd(h1, mo, sp["ffn2_norm"], 1.0, "resnorm_fwd_one")
    gu2 = _mm(n2, w["gu2"], name="mm_gu")
    a2 = _act_fwd(gu2, "act_fwd")
    f2 = _mm(a2, w["d2"], out_dtype=F32, name="mm_down")
    h3, n4 = _resnorm_fwd(h2, f2, sp["ple_norm"], 0.5, "resnorm_fwd_half")
    pgl = _mm(n4, w["pg"], name="mm_pg")
    pb = p.astype(BF16)
    pp = _mm(pb, w["pp"], name="mm_pp")

    dh3, dpgl, dpp, loss, d_final = _head(h3, pgl, pp, tgt, sp["final_norm"], "head")
    dw, ds_ = {}, {"final_norm": d_final}
    dw["pg"] = _mm(n4, dpgl, ta=True, name="mm_dw_sq")
    dw["pp"] = _mm(pb, dpp, ta=True, name="mm_dw_pp")
    dn4 = _mm(dpgl, w["pg"], tb=True, out_dtype=F32, name="mm_dx_sq_f32")
    dh3, df2, ds_["ple_norm"] = _norm_bwd(h3, [dn4], dh3, sp["ple_norm"], 0.5, "norm_bwd_1")

    def ffn_bwd(n, gu, a, df, wgu, wd):
        dwd = _mm(a, df, ta=True, name="mm_dw_down")
        da = _mm(df, wd, tb=True, name="mm_dx_down")
        dgu = _act_bwd(gu, da, "act_bwd")
        dwgu = _mm(n, dgu, ta=True, name="mm_dw_gu")
        dn = _mm(dgu, wgu, tb=True, out_dtype=F32, name="mm_dx_gu")
        return dwgu, dwd, dn

    dw["gu2"], dw["d2"], dn2 = ffn_bwd(n2, gu2, a2, df2, w["gu2"], w["d2"])
    dh2, dmix, ds_["ffn2_norm"] = _norm_bwd(h2, [dn2], dh3, sp["ffn2_norm"], 1.0, "norm_bwd_1")

    dw["out"] = _mm(merged, dmix, ta=True, name="mm_dw_sq")
    dmerged = _mm(dmix, w["out"], tb=True, name="mm_dx_sq")
    dgl, dbf, dbg, ds_["bm"] = _merge_bwd(z, bf, bg, sp["bm"], dmerged, "merge_bwd")
    dw["bf"] = _mm(y_fox, dbf, ta=True, name="mm_dw_branch")
    dw["bg"] = _mm(y_gla, dbg, ta=True, name="mm_dw_branch")
    dy_fox = _mm(dbf, w["bf"], tb=True, name="mm_dx_branch")
    dy_gla = _mm(dbg, w["bg"], tb=True, name="mm_dx_branch")

    do4, dgr4, ds_["ghn"] = _gla_out_bwd(o4, gr4, sp["ghn"], dy_gla.reshape(t * GLA_HEADS, GLA_VAL_DIM), "gla_out_bwd")
    dgq, dgk, dgv, dla = _gla_bwd(z, la, sprev, do4.reshape(t, GLA_VW), "gla_bwd")
    dfq, delta, dfcol = _fox_bwd_q(z, y_fox, dy_fox, lse, fcol, frow, "fox_bwd_q")
    dfk, dfv, dfrow = _fox_bwd_kv(z, dy_fox, lse, delta, fcol, frow, "fox_bwd_kv")
    dfp = jnp.pad((dfrow.reshape(FOX_HEADS, t) + dfcol.reshape(FOX_HEADS, t)).T, ((0, 0), (0, SMALL_W - FOX_HEADS)))
    dlf = _cumsum_tokens(dfp, True, "cumsum_bwd")
    dsm, ds_["fb"], ds_["gup"], ds_["gb"] = _small_bwd(s, sp["fb"], sp["gup"], sp["gb"], dlf, dla, "small_bwd")
    dz = jnp.concatenate([dfq, dfk, dfv, dgq, dgk, dgv, dgr4.reshape(t, GLA_VW), dgl], axis=1)
    dw["big"] = _mm(u, dz, ta=True, name="mm_dw_in")
    dw["sm"] = _mm(u, dsm, ta=True, out_dtype=F32, name="mm_dw_in_small")
    du1 = _mm(dz, w["big"], tb=True, out_dtype=F32, name="mm_dx_in")
    du2 = _mm(dsm, w["sm"], tb=True, out_dtype=F32, name="mm_dx_in_small")
    dh1, df1, ds_["mix_norm"] = _norm_bwd(h1, [du1, du2], dh2, sp["mix_norm"], 0.5, "norm_bwd_2")

    dw["gu1"], dw["d1"], dn1 = ffn_bwd(n1, gu1, a1, df1, w["gu1"], w["d1"])
    grad_x, _, ds_["ffn1_norm"] = _norm_bwd(x, [dn1], dh1, sp["ffn1_norm"], 1.0, "norm_bwd_1")
    return loss, grad_x, dw, ds_


def _half_rows(ref, which):
    r2 = ref.shape[0] // 2
    return ref.at[pl.ds(pl.multiple_of(which * r2, r2), r2)]


def _all_gather(shards, name):
    n = len(shards)

    def body(*refs):
        ins, outs = refs[:n], refs[n:2 * n]
        ici_send, ici_recv, d2d_send, d2d_recv, loc_sems = refs[2 * n:]
        x, y, c = lax.axis_index("x"), lax.axis_index("y"), lax.axis_index("c")
        chips = [(1 - x, y), (x, 1 - y), (1 - x, 1 - y)]
        slot = lambda chip: 2 * chip[0] + chip[1]

        def over_ici(wi, j, origin):
            return pltpu.make_async_remote_copy(
                src_ref=_half_rows(ins[wi], c), dst_ref=_half_rows(outs[wi].at[slot(origin)], c),
                send_sem=ici_send.at[3 * wi + j], recv_sem=ici_recv.at[3 * wi + j],
                device_id=(chips[j][0], chips[j][1], c), device_id_type=MESH)

        def over_d2d(wi, j, half):
            place = _half_rows(outs[wi].at[slot(chips[j])], half)
            return pltpu.make_async_remote_copy(
                src_ref=place, dst_ref=place, send_sem=d2d_send.at[3 * wi + j], recv_sem=d2d_recv.at[3 * wi + j],
                device_id=(x, y, 1 - c), device_id_type=MESH)

        local = [pltpu.make_async_copy(ins[wi], outs[wi].at[slot((x, y))], loc_sems.at[wi]) for wi in range(n)]
        sends = [over_ici(wi, j, (x, y)) for wi in range(n) for j in range(3)]
        for cp in local + sends:
            cp.start()
        passed = []
        for wi in range(n):
            for j in range(3):
                over_ici(wi, j, chips[j]).wait_recv()
                passed.append(over_d2d(wi, j, c))
                passed[-1].start()
        for wi in range(n):
            for j in range(3):
                over_d2d(wi, j, 1 - c).wait_recv()
        for cp in sends + passed:
            cp.wait_send()
        for cp in local:
            cp.wait()

    sems = [pltpu.SemaphoreType.DMA((3 * n,))] * 4 + [pltpu.SemaphoreType.DMA((n,))]
    return pl.pallas_call(
        body, name=name, in_specs=[ANY] * n, out_specs=[ANY] * n,
        out_shape=[jax.ShapeDtypeStruct((4,) + s.shape, s.dtype) for s in shards],
        scratch_shapes=sems, compiler_params=pltpu.CompilerParams(has_side_effects=True),
    )(*shards)


def _pair_swap(parts, name):
    n = len(parts)

    def body(*refs):
        ins, outs = refs[:n], refs[n:2 * n]
        send_sems, recv_sems = refs[2 * n:]
        x, y, c = lax.axis_index("x"), lax.axis_index("y"), lax.axis_index("c")

        def swap(wi):
            r2 = parts[wi].shape[1] // 2
            return pltpu.make_async_remote_copy(
                src_ref=ins[wi].at[:, pl.ds(pl.multiple_of((1 - c) * r2, r2), r2)], dst_ref=outs[wi],
                send_sem=send_sems.at[wi], recv_sem=recv_sems.at[wi], device_id=(x, y, 1 - c), device_id_type=MESH)

        copies = [swap(wi) for wi in range(n)]
        for cp in copies:
            cp.start()
        for cp in copies:
            cp.wait()

    return pl.pallas_call(
        body, name=name, in_specs=[ANY] * n, out_specs=[ANY] * n,
        out_shape=[jax.ShapeDtypeStruct((s.shape[0], s.shape[1] // 2, s.shape[2]), s.dtype) for s in parts],
        scratch_shapes=[pltpu.SemaphoreType.DMA((n,))] * 2, compiler_params=pltpu.CompilerParams(has_side_effects=True),
    )(*parts)


def _row_tile(r, c, budget=1 << 19):
    return r if r <= 8 else _pick(r, max(8, budget // c), 8)


def _sum_half(parts, other, cidx, name):
    nl, r, cc = parts.shape
    r2 = r // 2
    tr = _row_tile(r2, cc)

    def body(c_ref, p_ref, q_ref, o_ref):
        o_ref[...] = (_f(p_ref[...]) + _f(q_ref[...])).astype(o_ref.dtype)

    return pl.pallas_call(
        body, name=name, out_shape=jax.ShapeDtypeStruct((nl, r2, cc), parts.dtype),
        grid_spec=pltpu.PrefetchScalarGridSpec(
            num_scalar_prefetch=1, grid=(nl, r2 // tr),
            in_specs=[pl.BlockSpec((None, None, tr, cc), lambda l, i, c_ref: (l, c_ref[0], i, 0)),
                      pl.BlockSpec((None, tr, cc), lambda l, i, c_ref: (l, i, 0))],
            out_specs=pl.BlockSpec((None, tr, cc), lambda l, i, c_ref: (l, i, 0))),
        compiler_params=pltpu.CompilerParams(dimension_semantics=("parallel", "parallel"), vmem_limit_bytes=VMEM_LIMIT),
    )(cidx, parts.reshape(nl, 2, r2, cc), other)


def _chip_exchange(sums, name):
    n = len(sums)

    def body(*refs):
        ins, outs = refs[:n], refs[n:2 * n]
        send_sems, recv_sems, loc_sems = refs[2 * n:]
        x, y, c = lax.axis_index("x"), lax.axis_index("y"), lax.axis_index("c")
        chips = [(1 - x, y), (x, 1 - y), (1 - x, 1 - y)]
        slot = lambda chip: 2 * chip[0] + chip[1]

        def src(wi, chip):
            return ins[wi].at[slot(chip) if sums[wi].shape[0] == 4 else 0]

        def remote(wi, j, origin):
            return pltpu.make_async_remote_copy(
                src_ref=src(wi, chips[j]), dst_ref=outs[wi].at[slot(origin)], send_sem=send_sems.at[3 * wi + j], recv_sem=recv_sems.at[3 * wi + j],
                device_id=(chips[j][0], chips[j][1], c), device_id_type=MESH)

        local = [pltpu.make_async_copy(src(wi, (x, y)), outs[wi].at[slot((x, y))], loc_sems.at[wi]) for wi in range(n)]
        sends = [remote(wi, j, (x, y)) for wi in range(n) for j in range(3)]
        for cp in local + sends:
            cp.start()
        for wi in range(n):
            for j in range(3):
                remote(wi, j, chips[j]).wait_recv()
        for cp in sends:
            cp.wait_send()
        for cp in local:
            cp.wait()

    return pl.pallas_call(
        body, name=name, in_specs=[ANY] * n, out_specs=[ANY] * n,
        out_shape=[jax.ShapeDtypeStruct((4,) + s.shape[1:], s.dtype) for s in sums],
        scratch_shapes=[pltpu.SemaphoreType.DMA((3 * n,)), pltpu.SemaphoreType.DMA((3 * n,)), pltpu.SemaphoreType.DMA((n,))],
        compiler_params=pltpu.CompilerParams(has_side_effects=True),
    )(*sums)


def _sum_chips(got, name):
    _, r2, cc = got.shape
    tr = _row_tile(r2, cc)

    def body(g_ref, o_ref):
        o_ref[...] = ((_f(g_ref[0]) + _f(g_ref[1])) + _f(g_ref[2])) + _f(g_ref[3])

    return pl.pallas_call(
        body, name=name, grid=(r2 // tr,), in_specs=[pl.BlockSpec((4, tr, cc), lambda i: (0, i, 0))],
        out_specs=pl.BlockSpec((tr, cc), lambda i: (i, 0)), out_shape=jax.ShapeDtypeStruct((r2, cc), F32),
        compiler_params=pltpu.CompilerParams(dimension_semantics=("parallel",), vmem_limit_bytes=VMEM_LIMIT),
    )(got)


def _pair_gather(halves, name):
    n = len(halves)

    def body(*refs):
        ins, outs = refs[:n], refs[n:2 * n]
        send_sems, recv_sems, loc_sems = refs[2 * n:]
        x, y, c = lax.axis_index("x"), lax.axis_index("y"), lax.axis_index("c")

        def remote(wi, which):
            return pltpu.make_async_remote_copy(
                src_ref=ins[wi], dst_ref=outs[wi].at[which], send_sem=send_sems.at[wi], recv_sem=recv_sems.at[wi],
                device_id=(x, y, 1 - c), device_id_type=MESH)

        local = [pltpu.make_async_copy(ins[wi], outs[wi].at[c], loc_sems.at[wi]) for wi in range(n)]
        sends = [remote(wi, c) for wi in range(n)]
        for cp in local + sends:
            cp.start()
        for wi in range(n):
            remote(wi, 1 - c).wait_recv()
        for cp in sends:
            cp.wait_send()
        for cp in local:
            cp.wait()

    return pl.pallas_call(
        body, name=name, in_specs=[ANY] * n, out_specs=[ANY] * n,
        out_shape=[jax.ShapeDtypeStruct((2,) + s.shape, s.dtype) for s in halves],
        scratch_shapes=[pltpu.SemaphoreType.DMA((n,))] * 3, compiler_params=pltpu.CompilerParams(has_side_effects=True),
    )(*halves)


def _adamw(g, w, m, v, name):
    r, c = w.shape
    tr = _row_tile(r, c, 1 << 18)

    def body(g_ref, w_ref, m_ref, v_ref, d_ref, nm_ref, nv_ref):
        g = g_ref[...]
        m_new = ADAM_B1 * m_ref[...] + (1.0 - ADAM_B1) * g
        v_new = ADAM_B2 * v_ref[...] + (1.0 - ADAM_B2) * jnp.square(g)
        m_hat = m_new / (1.0 - ADAM_B1 ** ADAM_STEP)
        v_hat = v_new / (1.0 - ADAM_B2 ** ADAM_STEP)
        d_ref[...] = -ADAM_LR * (m_hat / (jnp.sqrt(v_hat) + ADAM_EPS) + ADAM_WD * w_ref[...])
        nm_ref[...] = m_new
        nv_ref[...] = v_new

    blk = pl.BlockSpec((tr, c), lambda i: (i, 0))
    return pl.pallas_call(
        body, name=name, grid=(r // tr,), in_specs=[blk] * 4, out_specs=[blk] * 3,
        out_shape=[jax.ShapeDtypeStruct((r, c), F32)] * 3,
        compiler_params=pltpu.CompilerParams(dimension_semantics=("parallel",), vmem_limit_bytes=VMEM_LIMIT),
    )(g, w, m, v)


BIG = ["ffn1_w_gate", "ffn1_w_up", "ffn1_w_down", "w_in", "gla_gate_up", "w_branch_fox", "w_branch_gla", "w_merge_gate", "w_out",
       "ffn2_w_gate", "ffn2_w_up", "ffn2_w_down", "w_ple_proj", "w_ple_gate"]
ROW_SHARDED = ("ffn1_w_down", "w_out", "ffn2_w_down", "w_ple_gate")
SMALL = ["ffn1_norm", "mix_norm", "fox_forget_bias", "gla_gate_bias", "gla_head_norm", "b_merge_gate", "ffn2_norm", "ple_norm", "final_norm"]
NAMES = ["ffn1_norm", "ffn1_w_gate", "ffn1_w_up", "ffn1_w_down", "mix_norm", "w_in", "fox_forget_bias", "gla_gate_up", "gla_gate_bias",
         "gla_head_norm", "w_branch_fox", "w_branch_gla", "w_merge_gate", "b_merge_gate", "w_out", "ffn2_norm", "ffn2_w_gate", "ffn2_w_up",
         "ffn2_w_down", "ple_norm", "w_ple_proj", "w_ple_gate", "final_norm"]
W_IN_COLS = (FOX_W, FOX_W, FOX_W, FOX_HEADS, GLA_KW, GLA_KW, GLA_VW, GLA_VW, GLA_RANK)
SMALL_ROWS, SMALL_COLS = 16, 1024


def _full_matrix(g, name):
    if name in ROW_SHARDED:
        return g.reshape(g.shape[0] * g.shape[1], g.shape[2])
    return jnp.transpose(g, (1, 0, 2)).reshape(g.shape[1], g.shape[0] * g.shape[2])


def _shard_parts(full, name):
    if name in ROW_SHARDED:
        return full.reshape(4, full.shape[0] // 4, full.shape[1])
    return jnp.transpose(full.reshape(full.shape[0], 4, full.shape[1] // 4), (1, 0, 2))


def _gathered_weights(full):
    w_in = full["w_in"]
    offs = [0]
    for cw in W_IN_COLS:
        offs.append(offs[-1] + cw)
    col = lambda i: w_in[:, offs[i]:offs[i + 1]]
    big = jnp.concatenate([col(0), col(1), col(2), col(4), col(5), col(6), col(7), full["w_merge_gate"]], axis=1)
    sm = jnp.concatenate([col(3), col(8), jnp.zeros((D_MODEL, SMALL_W - FOX_HEADS - GLA_RANK), BF16)], axis=1)
    return {
        "gu1": jnp.concatenate([full["ffn1_w_gate"], full["ffn1_w_up"]], axis=1), "d1": full["ffn1_w_down"],
        "big": big, "sm": sm, "bf": full["w_branch_fox"], "bg": full["w_branch_gla"], "out": full["w_out"],
        "gu2": jnp.concatenate([full["ffn2_w_gate"], full["ffn2_w_up"]], axis=1), "d2": full["ffn2_w_down"],
        "pp": full["w_ple_proj"], "pg": full["w_ple_gate"],
    }


def _whole_gradients(dw, dgup):
    ff = dw["d1"].shape[0]
    big, sm = dw["big"], dw["sm"].astype(BF16)
    w_in = jnp.concatenate([big[:, Z_FQ:Z_GQ], sm[:, :FOX_HEADS], big[:, Z_GQ:Z_GL], sm[:, FOX_HEADS:FOX_HEADS + GLA_RANK]], axis=1)
    return {
        "ffn1_w_gate": dw["gu1"][:, :ff], "ffn1_w_up": dw["gu1"][:, ff:], "ffn1_w_down": dw["d1"],
        "w_in": w_in, "gla_gate_up": dgup[FOX_HEADS:FOX_HEADS + GLA_RANK],
        "w_branch_fox": dw["bf"], "w_branch_gla": dw["bg"], "w_merge_gate": big[:, Z_GL:], "w_out": dw["out"],
        "ffn2_w_gate": dw["gu2"][:, :ff], "ffn2_w_up": dw["gu2"][:, ff:], "ffn2_w_down": dw["d2"],
        "w_ple_proj": dw["pp"], "w_ple_gate": dw["pg"],
    }


def _pad_lanes(a, width):
    return jnp.pad(a, ((0, 0), (0, width - a.shape[1])))


def kernel(x, p, ffn1_norm, ffn1_w_gate, ffn1_w_up, ffn1_w_down, mix_norm, w_in, fox_forget_bias, gla_gate_up, gla_gate_bias, gla_head_norm, w_branch_fox, w_branch_gla, w_merge_gate, b_merge_gate, w_out, ffn2_norm, ffn2_w_gate, ffn2_w_up, ffn2_w_down, ple_norm, w_ple_proj, w_ple_gate, final_norm, loss_target, m_ffn1_norm, m_ffn1_w_gate, m_ffn1_w_up, m_ffn1_w_down, m_mix_norm, m_w_in, m_fox_forget_bias, m_gla_gate_up, m_gla_gate_bias, m_gla_head_norm, m_w_branch_fox, m_w_branch_gla, m_w_merge_gate, m_b_merge_gate, m_w_out, m_ffn2_norm, m_ffn2_w_gate, m_ffn2_w_up, m_ffn2_w_down, m_ple_norm, m_w_ple_proj, m_w_ple_gate, m_final_norm, v_ffn1_norm, v_ffn1_w_gate, v_ffn1_w_up, v_ffn1_w_down, v_mix_norm, v_w_in, v_fox_forget_bias, v_gla_gate_up, v_gla_gate_bias, v_gla_head_norm, v_w_branch_fox, v_w_branch_gla, v_w_merge_gate, v_b_merge_gate, v_w_out, v_ffn2_norm, v_ffn2_w_gate, v_ffn2_w_up, v_ffn2_w_down, v_ple_norm, v_w_ple_proj, v_w_ple_gate, v_final_norm):
    args = dict(locals())
    wts = {n: args[n] for n in NAMES}
    mom = {n: args["m_" + n] for n in NAMES}
    var = {n: args["v_" + n] for n in NAMES}
    two_d = lambda a: a.reshape(-1, a.shape[-1])

    wire = lambda n: F32 if n == "gla_gate_up" else BF16
    gathered = _all_gather([two_d(wts[n]).astype(wire(n)) for n in BIG], "all_gather")
    full = {n: _full_matrix(g, n) for n, g in zip(BIG, gathered)}
    w = _gathered_weights(full)
    gup = jnp.zeros((SMALL_W, GLA_KW), F32).at[FOX_HEADS:FOX_HEADS + GLA_RANK].set(full["gla_gate_up"])
    sp = {
        "ffn1_norm": two_d(ffn1_norm), "mix_norm": two_d(mix_norm), "fb": _pad_lanes(two_d(fox_forget_bias), SMALL_W), "gup": gup,
        "gb": two_d(gla_gate_bias), "ghn": two_d(gla_head_norm), "bm": two_d(b_merge_gate), "ffn2_norm": two_d(ffn2_norm),
        "ple_norm": two_d(ple_norm), "final_norm": two_d(final_norm),
    }

    loss, grad_x, dw, ds_ = _local_step(x[0], p[0, 0], loss_target[0], w, sp)

    whole = _whole_gradients(dw, ds_["gup"])
    small_g = {"ffn1_norm": ds_["ffn1_norm"], "mix_norm": ds_["mix_norm"], "fox_forget_bias": ds_["fb"][:, :FOX_HEADS],
               "gla_gate_bias": ds_["gb"], "gla_head_norm": ds_["ghn"], "b_merge_gate": ds_["bm"], "ffn2_norm": ds_["ffn2_norm"],
               "ple_norm": ds_["ple_norm"], "final_norm": ds_["final_norm"]}
    small_w = sum(two_d(wts[n]).shape[1] for n in SMALL)
    assert small_w <= SMALL_ROWS * SMALL_COLS
    packed = lambda d: _pad_lanes(jnp.concatenate([two_d(d[n]) for n in SMALL], axis=1), SMALL_ROWS * SMALL_COLS).reshape(SMALL_ROWS, SMALL_COLS)
    parts = [_shard_parts(whole[n], n) for n in BIG] + [packed(small_g)[None]]
    cidx = lax.axis_index("c").astype(jnp.int32).reshape(1)
    swapped = _pair_swap(parts, "pair_swap")
    pair_sums = [_sum_half(a, b, cidx, "sum_half") for a, b in zip(parts, swapped)]
    got = _chip_exchange(pair_sums, "chip_exchange")
    grads = _pair_gather([_sum_chips(a, "sum_chips") for a in got], "pair_gather")

    out = {}
    for n, g in zip(BIG, grads[:-1]):
        g = g.reshape(two_d(wts[n]).shape)
        out[n] = [r.reshape(wts[n].shape) for r in [g, *_adamw(g, two_d(wts[n]), two_d(mom[n]), two_d(var[n]), "adamw_" + n)]]
    g = grads[-1].reshape(SMALL_ROWS, SMALL_COLS)
    small_out = [r.reshape(1, SMALL_ROWS * SMALL_COLS) for r in [g, *_adamw(g, packed(wts), packed(mom), packed(var), "adamw_small")]]
    off = 0
    for n in SMALL:
        cw = two_d(wts[n]).shape[1]
        out[n] = [r[:, off:off + cw].reshape(wts[n].shape) for r in small_out]
        off += cw

    total = lax.psum(loss[0, 0], ("x", "y", "c"))
    return (total, grad_x[None], *[out[n][0] for n in NAMES], *[out[n][1] for n in NAMES],
            *[out[n][2] for n in NAMES], *[out[n][3] for n in NAMES])
```

```python
import functools

import jax
import jax.numpy as jnp
from jax import lax
from jax.experimental import pallas as pl
from jax.experimental.pallas import tpu as pltpu

F32 = jnp.float32
BF16 = jnp.bfloat16
MESH = pl.DeviceIdType.MESH
ANY = pl.BlockSpec(memory_space=pl.ANY)

D_MODEL = 2048
FOX_HEADS = 8
HEAD_DIM = 128
GLA_HEADS = 4
GLA_VAL_DIM = 256
GLA_RANK = 16
GLA_TAU = 16.0
CHUNK = 64
EPS = 1e-6
FOX_W = FOX_HEADS * HEAD_DIM
GLA_KW = GLA_HEADS * HEAD_DIM
GLA_VW = GLA_HEADS * GLA_VAL_DIM
Z_FQ, Z_FK, Z_FV, Z_GQ, Z_GK, Z_GV, Z_GR, Z_GL = 0, 1024, 2048, 3072, 3584, 4096, 5120, 6144
Z_W = Z_GL + 2 * D_MODEL
SMALL_W = 128
NEG = -1e30

ADAM_LR, ADAM_B1, ADAM_B2, ADAM_EPS, ADAM_WD, ADAM_STEP = 0.001, 0.9, 0.999, 1e-08, 0.01, 10

VMEM_LIMIT = 56 * 1024 * 1024


def _pick(n, target, mult=128):
    if n <= target:
        return n
    best = None
    for d in range(mult, target + 1, mult):
        if n % d == 0:
            best = d
    assert best is not None, (n, target)
    return best


def _mm(a, b, *, ta=False, tb=False, out_dtype=BF16, name):
    m, k = (a.shape[1], a.shape[0]) if ta else a.shape
    n = b.shape[0] if tb else b.shape[1]
    assert (b.shape[1] if tb else b.shape[0]) == k
    bm, bn, bk = _pick(m, 1024), _pick(n, 512), _pick(k, 2048)
    nk = k // bk
    dims = (((0 if ta else 1,), (1 if tb else 0,)), ((), ()))

    def body(a_ref, b_ref, o_ref, acc_ref):
        part = lax.dot_general(a_ref[...], b_ref[...], dims, preferred_element_type=F32)
        if nk == 1:
            o_ref[...] = part.astype(o_ref.dtype)
            return
        kk = pl.program_id(2)

        @pl.when(kk == 0)
        def _():
            acc_ref[...] = part

        @pl.when(kk > 0)
        def _():
            acc_ref[...] += part

        @pl.when(kk == nk - 1)
        def _():
            o_ref[...] = acc_ref[...].astype(o_ref.dtype)

    a_spec = pl.BlockSpec((bk, bm), lambda i, j, kk: (kk, i)) if ta else pl.BlockSpec((bm, bk), lambda i, j, kk: (i, kk))
    b_spec = pl.BlockSpec((bn, bk), lambda i, j, kk: (j, kk)) if tb else pl.BlockSpec((bk, bn), lambda i, j, kk: (kk, j))
    return pl.pallas_call(
        body, name=name, grid=(m // bm, n // bn, nk),
        in_specs=[a_spec, b_spec], out_specs=pl.BlockSpec((bm, bn), lambda i, j, kk: (i, j)),
        out_shape=jax.ShapeDtypeStruct((m, n), out_dtype),
        scratch_shapes=[pltpu.VMEM((bm, bn), F32)],
        compiler_params=pltpu.CompilerParams(dimension_semantics=("parallel", "parallel", "arbitrary"), vmem_limit_bytes=VMEM_LIMIT),
    )(a, b)


def _rowwise(fn, tiled, bcast, outs, reds=(), *, tt, name):
    t = tiled[0][0].shape[0]
    tt = min(tt, t)
    nin, nout = len(tiled) + len(bcast), len(outs)
    splits = [s[3] for s in tiled] + [s[1] for s in bcast]

    def store(ref, val, acc):
        off = 0
        for piece in val if isinstance(val, (tuple, list)) else (val,):
            w = piece.shape[-1]
            if acc:
                ref[:, off:off + w] += piece.astype(ref.dtype)
            else:
                ref[:, off:off + w] = piece.astype(ref.dtype)
            off += w
        assert off == ref.shape[-1], (name, off, ref.shape)

    def body(*refs):
        args = []
        for ref, sp in zip(refs[:nin], splits):
            if sp is None:
                args.append(ref[...])
            else:
                off = 0
                for w in sp:
                    args.append(ref[:, off:off + w])
                    off += w
        res = fn(*args)
        res = res if isinstance(res, (tuple, list)) else (res,)
        assert len(res) == nout + len(reds), (name, len(res))
        for ref, val in zip(refs[nin:nin + nout], res[:nout]):
            store(ref, val, False)
        if reds:
            @pl.when(pl.program_id(0) == 0)
            def _():
                for ref in refs[nin + nout:]:
                    ref[...] = jnp.zeros(ref.shape, ref.dtype)
            for ref, val in zip(refs[nin + nout:], res[nout:]):
                store(ref, val, True)

    in_specs = [pl.BlockSpec((tt, w), functools.partial(lambda i, cb: (i, cb), cb=cb)) for (_, w, cb, _) in tiled]
    in_specs += [pl.BlockSpec(arr.shape, lambda i: (0, 0)) for (arr, _) in bcast]
    out_specs = [pl.BlockSpec((tt, w), lambda i: (i, 0)) for (w, _) in outs]
    out_specs += [pl.BlockSpec((r, w), lambda i: (0, 0)) for (r, w) in reds]
    out_shape = [jax.ShapeDtypeStruct((t, w), dt) for (w, dt) in outs] + [jax.ShapeDtypeStruct((r, w), F32) for (r, w) in reds]
    return pl.pallas_call(
        body, name=name, grid=(t // tt,), in_specs=in_specs, out_specs=out_specs, out_shape=out_shape,
        compiler_params=pltpu.CompilerParams(dimension_semantics=("arbitrary" if reds else "parallel",), vmem_limit_bytes=VMEM_LIMIT),
    )(*[s[0] for s in tiled], *[s[0] for s in bcast])


def _full(arr):
    return (arr, arr.shape[1], 0, None)


def _f(x):
    return x.astype(F32)


def _rms(x, g):
    return x * lax.rsqrt(jnp.mean(x * x, axis=-1, keepdims=True) + EPS) * g


def _log_sigmoid(x):
    return jnp.minimum(x, 0.0) - jnp.log1p(jnp.exp(-jnp.abs(x)))


def _silu(x):
    return x * jax.nn.sigmoid(x)


def _norm_fwd(x, g, name):
    return _rowwise(lambda xb, gb: _rms(_f(xb), gb), [_full(x)], [(g, None)], [(x.shape[1], BF16)], tt=256, name=name)[0]


def _resnorm_fwd(res, branch, g, coef, name):
    def fn(rb, bb, gb):
        h = rb + coef * _f(bb)
        return h, _rms(h, gb)
    d = res.shape[1]
    return _rowwise(fn, [_full(res), _full(branch)], [(g, None)], [(d, F32), (d, BF16)], tt=256, name=name)


def _norm_bwd(h, dns, dres, g, coef, name):
    nd = len(dns)

    def fn(hb, *rest):
        dn = _f(rest[0])
        for extra in rest[1:nd]:
            dn = dn + _f(extra)
        dr, gb = rest[nd], rest[nd + 1]
        _, vjp = jax.vjp(_rms, hb, gb)
        dh, dg = vjp(dn)
        dh = dh + dr
        return dh, coef * dh, dg
    d = h.shape[1]
    return _rowwise(fn, [_full(h)] + [_full(x) for x in dns] + [_full(dres)], [(g, None)],
                    [(d, F32), (d, BF16)], [(1, d)], tt=256, name=name)


def _act_fwd(gu, name):
    ff = gu.shape[1] // 2
    return _rowwise(lambda gb, ub: _silu(_f(gb)) * _f(ub), [(gu, 2 * ff, 0, (ff, ff))], [], [(ff, BF16)], tt=256, name=name)[0]


def _act_bwd(gu, da, name):
    ff = gu.shape[1] // 2

    def fn(gb, ub, dab):
        _, vjp = jax.vjp(lambda p, q: _silu(p) * q, _f(gb), _f(ub))
        return (vjp(_f(dab)),)
    return _rowwise(fn, [(gu, 2 * ff, 0, (ff, ff)), _full(da)], [], [(2 * ff, BF16)], tt=128, name=name)[0]


def _merge(glf, glg, bf, bg, bmf, bmg):
    return jax.nn.sigmoid(_f(glf) + bmf) * _f(bf) + jax.nn.sigmoid(_f(glg) + bmg) * _f(bg)


def _merge_fwd(z, bf, bg, bm, name):
    d = D_MODEL
    return _rowwise(_merge, [(z, d, Z_GL // d, None), (z, d, Z_GL // d + 1, None), _full(bf), _full(bg)], [(bm, (d, d))],
                    [(d, BF16)], tt=256, name=name)[0]


def _merge_bwd(z, bf, bg, bm, dm, name):
    d = D_MODEL

    def fn(glf, glg, bfb, bgb, dmb, bmf, bmg):
        _, vjp = jax.vjp(_merge, _f(glf), _f(glg), _f(bfb), _f(bgb), bmf, bmg)
        dglf, dglg, dbf, dbg, dbmf, dbmg = vjp(_f(dmb))
        return (dglf, dglg), dbf, dbg, (dbmf, dbmg)
    return _rowwise(fn, [(z, d, Z_GL // d, None), (z, d, Z_GL // d + 1, None), _full(bf), _full(bg), _full(dm)], [(bm, (d, d))],
                    [(2 * d, BF16), (d, BF16), (d, BF16)], [(1, 2 * d)], tt=128, name=name)


def _gla_out(o, gr, g):
    return _rms(o, g) * _silu(_f(gr))


def _gla_out_fwd(o4, gr4, g, name):
    return _rowwise(lambda ob, rb, gb: _gla_out(ob, rb, gb), [_full(o4), _full(gr4)], [(g, None)], [(GLA_VAL_DIM, BF16)], tt=1024, name=name)[0]


def _gla_out_bwd(o4, gr4, g, dy4, name):
    def fn(ob, rb, dyb, gb):
        _, vjp = jax.vjp(_gla_out, ob, _f(rb), gb)
        return vjp(_f(dyb))
    return _rowwise(fn, [_full(o4), _full(gr4), _full(dy4)], [(g, None)], [(GLA_VAL_DIM, F32), (GLA_VAL_DIM, BF16)], [(1, GLA_VAL_DIM)],
                    tt=1024, name=name)


def _small_gates(s, fb, gup, gb):
    lane = lax.broadcasted_iota(jnp.int32, s.shape, 1)
    lf = jnp.where(lane < FOX_HEADS, _log_sigmoid(s + fb), 0.0)
    pre = jnp.dot(s.astype(BF16), gup.astype(BF16), preferred_element_type=F32) + gb
    return lf, _log_sigmoid(pre) / GLA_TAU


def _small_fwd(s, fb, gup, gb, name):
    return _rowwise(_small_gates, [_full(s)], [(fb, None), (gup, None), (gb, None)], [(SMALL_W, F32), (GLA_KW, F32)], tt=256, name=name)


def _small_bwd(s, fb, gup, gb, dlf, dla, name):
    def fn(sb, dlfb, dlab, fbb, gupb, gbb):
        _, vjp = jax.vjp(_small_gates, sb, fbb, gupb, gbb)
        return vjp((dlfb, dlab))
    return _rowwise(fn, [_full(s), _full(dlf), _full(dla)], [(fb, None), (gup, None), (gb, None)],
                    [(SMALL_W, BF16)], [(1, SMALL_W), (SMALL_W, GLA_KW), (1, GLA_KW)], tt=256, name=name)


def _head_fn(h3, pgl, pp, tgt, gf):
    h4 = h3 + jax.nn.sigmoid(pgl) * pp
    err = _rms(h4, gf) - tgt
    return 0.5 * jnp.sum(jnp.mean(err * err, axis=-1, keepdims=True))


def _head(h3, pgl, pp, tgt, gf, name):
    def fn(hb, gl, pb, tb, gfb):
        loss, vjp = jax.vjp(_head_fn, hb, _f(gl), _f(pb), tb, gfb)
        dh, dgl, dpp, _, dgf = vjp(jnp.ones((), F32))
        return dh, dgl, dpp, jnp.full((1, 128), loss, F32), dgf
    d = h3.shape[1]
    return _rowwise(fn, [_full(h3), _full(pgl), _full(pp), _full(tgt)], [(gf, None)],
                    [(d, F32), (d, BF16), (d, BF16)], [(1, 128), (1, d)], tt=256, name=name)


def _cumsum_tokens(a, reverse, name):
    t, w = a.shape
    r = min(256, t)
    nb = t // r

    def body(a_ref, o_ref, carry_ref):
        @pl.when(pl.program_id(0) == 0)
        def _():
            carry_ref[...] = jnp.zeros(carry_ref.shape, F32)
        row = lax.broadcasted_iota(jnp.int32, (r, r), 0)
        col = lax.broadcasted_iota(jnp.int32, (r, r), 1)
        tri = ((col >= row) if reverse else (col <= row)).astype(F32)
        blk = a_ref[...]
        o_ref[...] = jnp.dot(tri, blk, preferred_element_type=F32, precision=lax.Precision.HIGHEST) + carry_ref[...]
        carry_ref[...] += jnp.sum(blk, axis=0, keepdims=True)

    idx = (lambda i: (nb - 1 - i, 0)) if reverse else (lambda i: (i, 0))
    return pl.pallas_call(
        body, name=name, grid=(nb,), in_specs=[pl.BlockSpec((r, w), idx)], out_specs=pl.BlockSpec((r, w), idx),
        out_shape=jax.ShapeDtypeStruct((t, w), F32), scratch_shapes=[pltpu.VMEM((1, w), F32)],
        compiler_params=pltpu.CompilerParams(dimension_semantics=("arbitrary",)),
    )(a)


def _fox_blocks(t):
    tb = min(256, t)
    return tb, t // tb


def _fox_logits(q, k, fc, fr, i, j, tb):
    s = lax.dot_general(q, k, (((1,), (1,)), ((), ())), preferred_element_type=F32) * (HEAD_DIM ** -0.5)
    s = s + fc - fr
    row = i * tb + lax.broadcasted_iota(jnp.int32, (tb, tb), 0)
    col = j * tb + lax.broadcasted_iota(jnp.int32, (tb, tb), 1)
    return s, col <= row


def _fox_fwd(z, fcol, frow, name):
    t = z.shape[0]
    tb, nb = _fox_blocks(t)
    qb, kb, vb = Z_FQ // HEAD_DIM, Z_FK // HEAD_DIM, Z_FV // HEAD_DIM

    def body(q_ref, k_ref, v_ref, fc_ref, fr_ref, o_ref, lse_ref):
        i = pl.program_id(1)
        q = q_ref[...]
        fc = fc_ref[...]

        def step(j, carry):
            m, l, acc = carry
            rows = pl.ds(pl.multiple_of(j * tb, tb), tb)
            s, ok = _fox_logits(q, k_ref[rows, :], fc, fr_ref[j], i, j, tb)
            s = jnp.where(ok, s, NEG)
            m_new = jnp.maximum(m, jnp.max(s, axis=1, keepdims=True))
            alpha = jnp.exp(m - m_new)
            p = jnp.exp(s - m_new)
            l = alpha * l + jnp.sum(p, axis=1, keepdims=True)
            acc = alpha * acc + jnp.dot(p.astype(BF16), v_ref[rows, :], preferred_element_type=F32)
            return m_new, l, acc

        m, l, acc = lax.fori_loop(0, i + 1, step, (jnp.full((tb, 1), NEG, F32), jnp.zeros((tb, 1), F32), jnp.zeros((tb, HEAD_DIM), F32)))
        o_ref[...] = (acc / l).astype(o_ref.dtype)
        lse_ref[...] = m + jnp.log(l)

    return pl.pallas_call(
        body, name=name, grid=(FOX_HEADS, nb),
        in_specs=[pl.BlockSpec((tb, HEAD_DIM), lambda h, i: (i, qb + h)),
                  pl.BlockSpec((t, HEAD_DIM), lambda h, i: (0, kb + h)),
                  pl.BlockSpec((t, HEAD_DIM), lambda h, i: (0, vb + h)),
                  pl.BlockSpec((None, tb, 1), lambda h, i: (h, i, 0)),
                  pl.BlockSpec((None, nb, 1, tb), lambda h, i: (h, 0, 0, 0))],
        out_specs=[pl.BlockSpec((tb, HEAD_DIM), lambda h, i: (i, h)), pl.BlockSpec((None, tb, 1), lambda h, i: (h, i, 0))],
        out_shape=[jax.ShapeDtypeStruct((t, FOX_W), BF16), jax.ShapeDtypeStruct((FOX_HEADS, t, 1), F32)],
        compiler_params=pltpu.CompilerParams(dimension_semantics=("parallel", "parallel"), vmem_limit_bytes=VMEM_LIMIT),
    )(z, z, z, fcol, frow)


def _fox_bwd_q(z, o, do, lse, fcol, frow, name):
    t = z.shape[0]
    tb, nb = _fox_blocks(t)
    qb, kb, vb = Z_FQ // HEAD_DIM, Z_FK // HEAD_DIM, Z_FV // HEAD_DIM

    def body(q_ref, k_ref, v_ref, o_ref, do_ref, lse_ref, fc_ref, fr_ref, dq_ref, delta_ref, dfc_ref):
        i = pl.program_id(1)
        q, do = q_ref[...], do_ref[...]
        fc, lse = fc_ref[...], lse_ref[...]
        delta = jnp.sum(_f(do) * _f(o_ref[...]), axis=1, keepdims=True)
        delta_ref[...] = delta

        def step(j, carry):
            dq, dfc = carry
            rows = pl.ds(pl.multiple_of(j * tb, tb), tb)
            k = k_ref[rows, :]
            s, ok = _fox_logits(q, k, fc, fr_ref[j], i, j, tb)
            p = jnp.where(ok, jnp.exp(s - lse), 0.0)
            dp = lax.dot_general(do, v_ref[rows, :], (((1,), (1,)), ((), ())), preferred_element_type=F32)
            ds = p * (dp - delta)
            return dq + jnp.dot(ds.astype(BF16), k, preferred_element_type=F32), dfc + jnp.sum(ds, axis=1, keepdims=True)

        dq, dfc = lax.fori_loop(0, i + 1, step, (jnp.zeros((tb, HEAD_DIM), F32), jnp.zeros((tb, 1), F32)))
        dq_ref[...] = (dq * (HEAD_DIM ** -0.5)).astype(dq_ref.dtype)
        dfc_ref[...] = dfc

    col = lambda h, i: (h, i, 0)
    return pl.pallas_call(
        body, name=name, grid=(FOX_HEADS, nb),
        in_specs=[pl.BlockSpec((tb, HEAD_DIM), lambda h, i: (i, qb + h)),
                  pl.BlockSpec((t, HEAD_DIM), lambda h, i: (0, kb + h)),
                  pl.BlockSpec((t, HEAD_DIM), lambda h, i: (0, vb + h)),
                  pl.BlockSpec((tb, HEAD_DIM), lambda h, i: (i, h)),
                  pl.BlockSpec((tb, HEAD_DIM), lambda h, i: (i, h)),
                  pl.BlockSpec((None, tb, 1), col), pl.BlockSpec((None, tb, 1), col),
                  pl.BlockSpec((None, nb, 1, tb), lambda h, i: (h, 0, 0, 0))],
        out_specs=[pl.BlockSpec((tb, HEAD_DIM), lambda h, i: (i, h)), pl.BlockSpec((None, tb, 1), col), pl.BlockSpec((None, tb, 1), col)],
        out_shape=[jax.ShapeDtypeStruct((t, FOX_W), BF16), jax.ShapeDtypeStruct((FOX_HEADS, t, 1), F32), jax.ShapeDtypeStruct((FOX_HEADS, t, 1), F32)],
        compiler_params=pltpu.CompilerParams(dimension_semantics=("parallel", "parallel"), vmem_limit_bytes=VMEM_LIMIT),
    )(z, z, z, o, do, lse, fcol, frow)


def _fox_bwd_kv(z, do, lse, delta, fcol, frow, name):
    t = z.shape[0]
    tb, nb = _fox_blocks(t)
    qb, kb, vb = Z_FQ // HEAD_DIM, Z_FK // HEAD_DIM, Z_FV // HEAD_DIM

    def body(q_ref, k_ref, v_ref, do_ref, lse_ref, delta_ref, fc_ref, fr_ref, dk_ref, dv_ref, dfr_ref):
        j = pl.program_id(1)
        k, v, fr = k_ref[...], v_ref[...], fr_ref[...]

        def step(i, carry):
            dk, dv, dfr = carry
            rows = pl.ds(pl.multiple_of(i * tb, tb), tb)
            q, do = q_ref[rows, :], do_ref[rows, :]
            s, ok = _fox_logits(q, k, fc_ref[rows, :], fr, i, j, tb)
            p = jnp.where(ok, jnp.exp(s - lse_ref[rows, :]), 0.0)
            dv = dv + lax.dot_general(p.astype(BF16), do, (((0,), (0,)), ((), ())), preferred_element_type=F32)
            dp = lax.dot_general(do, v, (((1,), (1,)), ((), ())), preferred_element_type=F32)
            ds = p * (dp - delta_ref[rows, :])
            dk = dk + lax.dot_general(ds.astype(BF16), q, (((0,), (0,)), ((), ())), preferred_element_type=F32)
            return dk, dv, dfr - jnp.sum(ds, axis=0, keepdims=True)

        zero = jnp.zeros((tb, HEAD_DIM), F32)
        dk, dv, dfr = lax.fori_loop(j, nb, step, (zero, zero, jnp.zeros((1, tb), F32)))
        dk_ref[...] = (dk * (HEAD_DIM ** -0.5)).astype(dk_ref.dtype)
        dv_ref[...] = dv.astype(dv_ref.dtype)
        dfr_ref[...] = dfr

    whole = lambda h, j: (h, 0, 0)
    return pl.pallas_call(
        body, name=name, grid=(FOX_HEADS, nb),
        in_specs=[pl.BlockSpec((t, HEAD_DIM), lambda h, j: (0, qb + h)),
                  pl.BlockSpec((tb, HEAD_DIM), lambda h, j: (j, kb + h)),
                  pl.BlockSpec((tb, HEAD_DIM), lambda h, j: (j, vb + h)),
                  pl.BlockSpec((t, HEAD_DIM), lambda h, j: (0, h)),
                  pl.BlockSpec((None, t, 1), whole), pl.BlockSpec((None, t, 1), whole), pl.BlockSpec((None, t, 1), whole),
                  pl.BlockSpec((None, None, 1, tb), lambda h, j: (h, j, 0, 0))],
        out_specs=[pl.BlockSpec((tb, HEAD_DIM), lambda h, j: (j, h)), pl.BlockSpec((tb, HEAD_DIM), lambda h, j: (j, h)),
                   pl.BlockSpec((None, None, 1, tb), lambda h, j: (h, j, 0, 0))],
        out_shape=[jax.ShapeDtypeStruct((t, FOX_W), BF16), jax.ShapeDtypeStruct((t, FOX_W), BF16),
                   jax.ShapeDtypeStruct((FOX_HEADS, nb, 1, tb), F32)],
        compiler_params=pltpu.CompilerParams(dimension_semantics=("parallel", "parallel"), vmem_limit_bytes=VMEM_LIMIT),
    )(z, z, z, do, lse, delta, fcol, frow)


def _gla_step(st, q, k, v, la):
    row = lax.broadcasted_iota(jnp.int32, (CHUNK, CHUNK), 0)
    col = lax.broadcasted_iota(jnp.int32, (CHUNK, CHUNK), 1)
    tri = (col <= row).astype(F32)
    a_cum = jnp.dot(tri, la, preferred_element_type=F32, precision=lax.Precision.HIGHEST)
    a_tot = jnp.sum(la, axis=0, keepdims=True)
    k_dec = (_f(k) * jnp.exp(a_tot - a_cum)).astype(BF16)
    qs = (_f(q) * (HEAD_DIM ** -0.5)).astype(BF16)
    st = st * jnp.exp(a_tot) + lax.dot_general(v.astype(BF16), k_dec, (((0,), (0,)), ((), ())), preferred_element_type=F32)
    o = lax.dot_general(qs, st.astype(BF16), (((1,), (1,)), ((), ())), preferred_element_type=F32)
    return st, o


def _gla_blocks(t):
    r = min(256, t)
    return r, t // r, r // CHUNK


def _gla_fwd(z, la, name):
    t = z.shape[0]
    r, nb, nch = _gla_blocks(t)

    def body(q_ref, k_ref, v_ref, la_ref, o_ref, sp_ref, st_ref):
        @pl.when(pl.program_id(0) == 0)
        def _():
            st_ref[...] = jnp.zeros(st_ref.shape, F32)
        for c in range(nch):
            rows = slice(c * CHUNK, (c + 1) * CHUNK)
            for h in range(GLA_HEADS):
                kc = slice(h * HEAD_DIM, (h + 1) * HEAD_DIM)
                vc = slice(h * GLA_VAL_DIM, (h + 1) * GLA_VAL_DIM)
                st = st_ref[h]
                sp_ref[c, h] = st
                st, o = _gla_step(st, q_ref[rows, kc], k_ref[rows, kc], v_ref[rows, vc], la_ref[rows, kc])
                st_ref[h] = st
                o_ref[rows, vc] = o

    return pl.pallas_call(
        body, name=name, grid=(nb,),
        in_specs=[pl.BlockSpec((r, GLA_KW), lambda i: (i, Z_GQ // GLA_KW)), pl.BlockSpec((r, GLA_KW), lambda i: (i, Z_GK // GLA_KW)),
                  pl.BlockSpec((r, GLA_VW), lambda i: (i, Z_GV // GLA_VW)), pl.BlockSpec((r, GLA_KW), lambda i: (i, 0))],
        out_specs=[pl.BlockSpec((r, GLA_VW), lambda i: (i, 0)),
                   pl.BlockSpec((nch, GLA_HEADS, GLA_VAL_DIM, HEAD_DIM), lambda i: (i, 0, 0, 0))],
        out_shape=[jax.ShapeDtypeStruct((t, GLA_VW), F32),
                   jax.ShapeDtypeStruct((t // CHUNK, GLA_HEADS, GLA_VAL_DIM, HEAD_DIM), F32)],
        scratch_shapes=[pltpu.VMEM((GLA_HEADS, GLA_VAL_DIM, HEAD_DIM), F32)],
        compiler_params=pltpu.CompilerParams(dimension_semantics=("arbitrary",), vmem_limit_bytes=VMEM_LIMIT),
    )(z, z, z, la)


def _gla_bwd(z, la, sprev, do, name):
    t = z.shape[0]
    r, nb, nch = _gla_blocks(t)

    def body(q_ref, k_ref, v_ref, la_ref, sp_ref, do_ref, dq_ref, dk_ref, dv_ref, dla_ref, dst_ref):
        @pl.when(pl.program_id(0) == 0)
        def _():
            dst_ref[...] = jnp.zeros(dst_ref.shape, F32)
        for c in reversed(range(nch)):
            rows = slice(c * CHUNK, (c + 1) * CHUNK)
            for h in range(GLA_HEADS):
                kc = slice(h * HEAD_DIM, (h + 1) * HEAD_DIM)
                vc = slice(h * GLA_VAL_DIM, (h + 1) * GLA_VAL_DIM)
                _, vjp = jax.vjp(_gla_step, sp_ref[c, h], q_ref[rows, kc], k_ref[rows, kc], v_ref[rows, vc], la_ref[rows, kc])
                dst, dq, dk, dv, dla = vjp((dst_ref[h], do_ref[rows, vc]))
                dst_ref[h] = dst
                dq_ref[rows, kc] = dq
                dk_ref[rows, kc] = dk
                dv_ref[rows, vc] = dv
                dla_ref[rows, kc] = dla

    rev = lambda i: (nb - 1 - i, 0)
    return pl.pallas_call(
        body, name=name, grid=(nb,),
        in_specs=[pl.BlockSpec((r, GLA_KW), lambda i: (nb - 1 - i, Z_GQ // GLA_KW)), pl.BlockSpec((r, GLA_KW), lambda i: (nb - 1 - i, Z_GK // GLA_KW)),
                  pl.BlockSpec((r, GLA_VW), lambda i: (nb - 1 - i, Z_GV // GLA_VW)), pl.BlockSpec((r, GLA_KW), rev),
                  pl.BlockSpec((nch, GLA_HEADS, GLA_VAL_DIM, HEAD_DIM), lambda i: (nb - 1 - i, 0, 0, 0)),
                  pl.BlockSpec((r, GLA_VW), rev)],
        out_specs=[pl.BlockSpec((r, GLA_KW), rev), pl.BlockSpec((r, GLA_KW), rev), pl.BlockSpec((r, GLA_VW), rev), pl.BlockSpec((r, GLA_KW), rev)],
        out_shape=[jax.ShapeDtypeStruct((t, GLA_KW), BF16), jax.ShapeDtypeStruct((t, GLA_KW), BF16),
                   jax.ShapeDtypeStruct((t, GLA_VW), BF16), jax.ShapeDtypeStruct((t, GLA_KW), F32)],
        scratch_shapes=[pltpu.VMEM((GLA_HEADS, GLA_VAL_DIM, HEAD_DIM), F32)],
        compiler_params=pltpu.CompilerParams(dimension_semantics=("arbitrary",), vmem_limit_bytes=VMEM_LIMIT),
    )(z, z, z, la, sprev, do)


def _local_step(x, p, tgt, w, sp):
    t = x.shape[0]
    tb, nb = _fox_blocks(t)

    n1 = _norm_fwd(x, sp["ffn1_norm"], "norm1_fwd")
    gu1 = _mm(n1, w["gu1"], name="mm_gu")
    a1 = _act_fwd(gu1, "act_fwd")
    f1 = _mm(a1, w["d1"], out_dtype=F32, name="mm_down")
    h1, u = _resnorm_fwd(x, f1, sp["mix_norm"], 0.5, "resnorm_fwd_half")
    z = _mm(u, w["big"], name="mm_in")
    s = _mm(u, w["sm"], out_dtype=F32, name="mm_in_small")
    lf, la = _small_fwd(s, sp["fb"], sp["gup"], sp["gb"], "small_fwd")
    fp = _cumsum_tokens(lf, False, "cumsum_fwd")
    fh = fp[:, :FOX_HEADS].T
    fcol, frow = fh.reshape(FOX_HEADS, t, 1), fh.reshape(FOX_HEADS, nb, 1, tb)
    y_fox, lse = _fox_fwd(z, fcol, frow, "fox_fwd")
    o_gla, sprev = _gla_fwd(z, la, "gla_fwd")
    o4 = o_gla.reshape(t * GLA_HEADS, GLA_VAL_DIM)
    gr4 = z[:, Z_GR:Z_GR + GLA_VW].reshape(t * GLA_HEADS, GLA_VAL_DIM)
    y_gla = _gla_out_fwd(o4, gr4, sp["ghn"], "gla_out_fwd").reshape(t, GLA_VW)
    bf = _mm(y_fox, w["bf"], name="mm_branch")
    bg = _mm(y_gla, w["bg"], name="mm_branch")
    merged = _merge_fwd(z, bf, bg, sp["bm"], "merge_fwd")
    mo = _mm(merged, w["out"], out_dtype=F32, name="mm_out")
    h2, n2 = _resnorm_fwd(h1, mo, sp["ffn2_norm"], 1.0, "resnorm_fwd_one")
    gu2 = _mm(n2, w["gu2"], name="mm_gu")
    a2 = _act_fwd(gu2, "act_fwd")
    f2 = _mm(a2, w["d2"], out_dtype=F32, name="mm_down")
    h3, n4 = _resnorm_fwd(h2, f2, sp["ple_norm"], 0.5, "resnorm_fwd_half")
    pgl = _mm(n4, w["pg"], name="mm_pg")
    pb = p.astype(BF16)
    pp = _mm(pb, w["pp"], name="mm_pp")

    dh3, dpgl, dpp, loss, d_final = _head(h3, pgl, pp, tgt, sp["final_norm"], "head")
    dw, ds_ = {}, {"final_norm": d_final}
    dw["pg"] = _mm(n4, dpgl, ta=True, name="mm_dw_sq")
    dw["pp"] = _mm(pb, dpp, ta=True, name="mm_dw_pp")
    dn4 = _mm(dpgl, w["pg"], tb=True, out_dtype=F32, name="mm_dx_sq_f32")
    dh3, df2, ds_["ple_norm"] = _norm_bwd(h3, [dn4], dh3, sp["ple_norm"], 0.5, "norm_bwd_1")

    def ffn_bwd(n, gu, a, df, wgu, wd):
        dwd = _mm(a, df, ta=True, name="mm_dw_down")
        da = _mm(df, wd, tb=True, name="mm_dx_down")
        dgu = _act_bwd(gu, da, "act_bwd")
        dwgu = _mm(n, dgu, ta=True, name="mm_dw_gu")
        dn = _mm(dgu, wgu, tb=True, out_dtype=F32, name="mm_dx_gu")
        return dwgu, dwd, dn

    dw["gu2"], dw["d2"], dn2 = ffn_bwd(n2, gu2, a2, df2, w["gu2"], w["d2"])
    dh2, dmix, ds_["ffn2_norm"] = _norm_bwd(h2, [dn2], dh3, sp["ffn2_norm"], 1.0, "norm_bwd_1")

    dw["out"] = _mm(merged, dmix, ta=True, name="mm_dw_sq")
    dmerged = _mm(dmix, w["out"], tb=True, name="mm_dx_sq")
    dgl, dbf, dbg, ds_["bm"] = _merge_bwd(z, bf, bg, sp["bm"], dmerged, "merge_bwd")
    dw["bf"] = _mm(y_fox, dbf, ta=True, name="mm_dw_branch")
    dw["bg"] = _mm(y_gla, dbg, ta=True, name="mm_dw_branch")
    dy_fox = _mm(dbf, w["bf"], tb=True, name="mm_dx_branch")
    dy_gla = _mm(dbg, w["bg"], tb=True, name="mm_dx_branch")

    do4, dgr4, ds_["ghn"] = _gla_out_bwd(o4, gr4, sp["ghn"], dy_gla.reshape(t * GLA_HEADS, GLA_VAL_DIM), "gla_out_bwd")
    dgq, dgk, dgv, dla = _gla_bwd(z, la, sprev, do4.reshape(t, GLA_VW), "gla_bwd")
    dfq, delta, dfcol = _fox_bwd_q(z, y_fox, dy_fox, lse, fcol, frow, "fox_bwd_q")
    dfk, dfv, dfrow = _fox_bwd_kv(z, dy_fox, lse, delta, fcol, frow, "fox_bwd_kv")
    dfp = jnp.pad((dfrow.reshape(FOX_HEADS, t) + dfcol.reshape(FOX_HEADS, t)).T, ((0, 0), (0, SMALL_W - FOX_HEADS)))
    dlf = _cumsum_tokens(dfp, True, "cumsum_bwd")
    dsm, ds_["fb"], ds_["gup"], ds_["gb"] = _small_bwd(s, sp["fb"], sp["gup"], sp["gb"], dlf, dla, "small_bwd")
    dz = jnp.concatenate([dfq, dfk, dfv, dgq, dgk, dgv, dgr4.reshape(t, GLA_VW), dgl], axis=1)
    dw["big"] = _mm(u, dz, ta=True, name="mm_dw_in")
    dw["sm"] = _mm(u, dsm, ta=True, out_dtype=F32, name="mm_dw_in_small")
    du1 = _mm(dz, w["big"], tb=True, out_dtype=F32, name="mm_dx_in")
    du2 = _mm(dsm, w["sm"], tb=True, out_dtype=F32, name="mm_dx_in_small")
    dh1, df1, ds_["mix_norm"] = _norm_bwd(h1, [du1, du2], dh2, sp["mix_norm"], 0.5, "norm_bwd_2")

    dw["gu1"], dw["d1"], dn1 = ffn_bwd(n1, gu1, a1, df1, w["gu1"], w["d1"])
    grad_x, _, ds_["ffn1_norm"] = _norm_bwd(x, [dn1], dh1, sp["ffn1_norm"], 1.0, "norm_bwd_1")
    return loss, grad_x, dw, ds_


def _half_rows(ref, which):
    r2 = ref.shape[0] // 2
    return ref.at[pl.ds(pl.multiple_of(which * r2, r2), r2)]


def _all_gather(shards, name):
    n = len(shards)

    def body(*refs):
        ins, outs = refs[:n], refs[n:2 * n]
        ici_send, ici_recv, d2d_send, d2d_recv = refs[2 * n:]
        x, y, c = lax.axis_index("x"), lax.axis_index("y"), lax.axis_index("c")
        chips = [(1 - x, y), (x, 1 - y), (1 - x, 1 - y)]
        slot = lambda chip: 2 * chip[0] + chip[1]

        def over_ici(wi, j, origin):
            return pltpu.make_async_remote_copy(
                src_ref=_half_rows(ins[wi], c), dst_ref=_half_rows(outs[wi].at[slot(origin)], c),
                send_sem=ici_send.at[3 * wi + j], recv_sem=ici_recv.at[3 * wi + j],
                device_id=(chips[j][0], chips[j][1], c), device_id_type=MESH)

        def over_d2d(wi, j, half):
            place = _half_rows(outs[wi].at[slot(chips[j])], half)
            return pltpu.make_async_remote_copy(
                src_ref=place, dst_ref=place, send_sem=d2d_send.at[3 * wi + j], recv_sem=d2d_recv.at[3 * wi + j],
                device_id=(x, y, 1 - c), device_id_type=MESH)

        sends = [over_ici(wi, j, (x, y)) for wi in range(n) for j in range(3)]
        for cp in sends:
            cp.start()
        passed = []
        for wi in range(n):
            for j in range(3):
                over_ici(wi, j, chips[j]).wait_recv()
                passed.append(over_d2d(wi, j, c))
                passed[-1].start()
        for wi in range(n):
            for j in range(3):
                over_d2d(wi, j, 1 - c).wait_recv()
        for cp in sends + passed:
            cp.wait_send()

    sems = [pltpu.SemaphoreType.DMA((3 * n,))] * 4
    return pl.pallas_call(
        body, name=name, in_specs=[ANY] * n, out_specs=[ANY] * n,
        out_shape=[jax.ShapeDtypeStruct((4,) + s.shape, s.dtype) for s in shards],
        scratch_shapes=sems, compiler_params=pltpu.CompilerParams(has_side_effects=True),
    )(*shards)


def _pair_swap(parts, name):
    n = len(parts)

    def body(*refs):
        ins, outs = refs[:n], refs[n:2 * n]
        send_sems, recv_sems = refs[2 * n:]
        x, y, c = lax.axis_index("x"), lax.axis_index("y"), lax.axis_index("c")

        def swap(wi):
            r2 = parts[wi].shape[1] // 2
            return pltpu.make_async_remote_copy(
                src_ref=ins[wi].at[:, pl.ds(pl.multiple_of((1 - c) * r2, r2), r2)], dst_ref=outs[wi],
                send_sem=send_sems.at[wi], recv_sem=recv_sems.at[wi], device_id=(x, y, 1 - c), device_id_type=MESH)

        copies = [swap(wi) for wi in range(n)]
        for cp in copies:
            cp.start()
        for cp in copies:
            cp.wait()

    return pl.pallas_call(
        body, name=name, in_specs=[ANY] * n, out_specs=[ANY] * n,
        out_shape=[jax.ShapeDtypeStruct((s.shape[0], s.shape[1] // 2, s.shape[2]), s.dtype) for s in parts],
        scratch_shapes=[pltpu.SemaphoreType.DMA((n,))] * 2, compiler_params=pltpu.CompilerParams(has_side_effects=True),
    )(*parts)


def _row_tile(r, c, budget=1 << 19):
    return r if r <= 8 else _pick(r, max(8, budget // c), 8)


def _sum_half(parts, other, cidx, name):
    nl, r, cc = parts.shape
    r2 = r // 2
    tr = _row_tile(r2, cc)

    def body(c_ref, p_ref, q_ref, o_ref):
        o_ref[...] = (_f(p_ref[...]) + _f(q_ref[...])).astype(o_ref.dtype)

    return pl.pallas_call(
        body, name=name, out_shape=jax.ShapeDtypeStruct((nl, r2, cc), parts.dtype),
        grid_spec=pltpu.PrefetchScalarGridSpec(
            num_scalar_prefetch=1, grid=(nl, r2 // tr),
            in_specs=[pl.BlockSpec((None, None, tr, cc), lambda l, i, c_ref: (l, c_ref[0], i, 0)),
                      pl.BlockSpec((None, tr, cc), lambda l, i, c_ref: (l, i, 0))],
            out_specs=pl.BlockSpec((None, tr, cc), lambda l, i, c_ref: (l, i, 0))),
        compiler_params=pltpu.CompilerParams(dimension_semantics=("parallel", "parallel"), vmem_limit_bytes=VMEM_LIMIT),
    )(cidx, parts.reshape(nl, 2, r2, cc), other)


def _chip_exchange(sums, name):
    n = len(sums)

    def body(*refs):
        ins, outs = refs[:n], refs[n:2 * n]
        send_sems, recv_sems = refs[2 * n:]
        x, y, c = lax.axis_index("x"), lax.axis_index("y"), lax.axis_index("c")
        chips = [(1 - x, y), (x, 1 - y), (1 - x, 1 - y)]
        slot = lambda chip: 2 * chip[0] + chip[1]

        def src(wi, chip):
            return ins[wi].at[slot(chip) if sums[wi].shape[0] == 4 else 0]

        def remote(wi, j, origin):
            return pltpu.make_async_remote_copy(
                src_ref=src(wi, chips[j]), dst_ref=outs[wi].at[slot(origin)], send_sem=send_sems.at[3 * wi + j], recv_sem=recv_sems.at[3 * wi + j],
                device_id=(chips[j][0], chips[j][1], c), device_id_type=MESH)

        sends = [remote(wi, j, (x, y)) for wi in range(n) for j in range(3)]
        for cp in sends:
            cp.start()
        for wi in range(n):
            for j in range(3):
                remote(wi, j, chips[j]).wait_recv()
        for cp in sends:
            cp.wait_send()

    return pl.pallas_call(
        body, name=name, in_specs=[ANY] * n, out_specs=[ANY] * n,
        out_shape=[jax.ShapeDtypeStruct((4,) + s.shape[1:], s.dtype) for s in sums],
        scratch_shapes=[pltpu.SemaphoreType.DMA((3 * n,)), pltpu.SemaphoreType.DMA((3 * n,))],
        compiler_params=pltpu.CompilerParams(has_side_effects=True),
    )(*sums)


def _sum_chips(got, own, chip, name):
    _, r2, cc = got.shape
    tr = _row_tile(r2, cc)
    per_chip = own.shape[0] == 4

    def body(chip_ref, g_ref, own_ref, o_ref):
        term = lambda k: jnp.where(chip_ref[0] == k, _f(own_ref[...]), _f(g_ref[k]))
        o_ref[...] = ((term(0) + term(1)) + term(2)) + term(3)

    return pl.pallas_call(
        body, name=name, out_shape=jax.ShapeDtypeStruct((r2, cc), F32),
        grid_spec=pltpu.PrefetchScalarGridSpec(
            num_scalar_prefetch=1, grid=(r2 // tr,),
            in_specs=[pl.BlockSpec((4, tr, cc), lambda i, chip_ref: (0, i, 0)),
                      pl.BlockSpec((None, tr, cc), lambda i, chip_ref: (chip_ref[0] if per_chip else 0, i, 0))],
            out_specs=pl.BlockSpec((tr, cc), lambda i, chip_ref: (i, 0))),
        compiler_params=pltpu.CompilerParams(dimension_semantics=("parallel",), vmem_limit_bytes=VMEM_LIMIT),
    )(chip, got, own)


def _pair_gather(halves, name):
    n = len(halves)

    def body(*refs):
        ins, outs = refs[:n], refs[n:2 * n]
        send_sems, recv_sems = refs[2 * n:]
        x, y, c = lax.axis_index("x"), lax.axis_index("y"), lax.axis_index("c")
        copies = [pltpu.make_async_remote_copy(
            src_ref=ins[wi], dst_ref=outs[wi], send_sem=send_sems.at[wi], recv_sem=recv_sems.at[wi],
            device_id=(x, y, 1 - c), device_id_type=MESH) for wi in range(n)]
        for cp in copies:
            cp.start()
        for cp in copies:
            cp.wait()

    return pl.pallas_call(
        body, name=name, in_specs=[ANY] * n, out_specs=[ANY] * n,
        out_shape=[jax.ShapeDtypeStruct(s.shape, s.dtype) for s in halves],
        scratch_shapes=[pltpu.SemaphoreType.DMA((n,))] * 2, compiler_params=pltpu.CompilerParams(has_side_effects=True),
    )(*halves)


def _adamw(mine, other, cidx, w, m, v, name):
    r, c = w.shape
    tr = _row_tile(r // 2, c, 1 << 18)
    nh = (r // 2) // tr

    def body(c_ref, mine_ref, other_ref, w_ref, m_ref, v_ref, g_ref, d_ref, nm_ref, nv_ref):
        g = jnp.where(pl.program_id(0) // nh == c_ref[0], mine_ref[...], other_ref[...])
        g_ref[...] = g
        m_new = ADAM_B1 * m_ref[...] + (1.0 - ADAM_B1) * g
        v_new = ADAM_B2 * v_ref[...] + (1.0 - ADAM_B2) * jnp.square(g)
        m_hat = m_new / (1.0 - ADAM_B1 ** ADAM_STEP)
        v_hat = v_new / (1.0 - ADAM_B2 ** ADAM_STEP)
        d_ref[...] = -ADAM_LR * (m_hat / (jnp.sqrt(v_hat) + ADAM_EPS) + ADAM_WD * w_ref[...])
        nm_ref[...] = m_new
        nv_ref[...] = v_new

    blk = pl.BlockSpec((tr, c), lambda i, c_ref: (i, 0))
    mine_spec = pl.BlockSpec((tr, c), lambda i, c_ref: (jnp.where(i // nh == c_ref[0], i % nh, 0), 0))
    other_spec = pl.BlockSpec((tr, c), lambda i, c_ref: (jnp.where(i // nh == c_ref[0], 0, i % nh), 0))
    return pl.pallas_call(
        body, name=name, out_shape=[jax.ShapeDtypeStruct((r, c), F32)] * 4,
        grid_spec=pltpu.PrefetchScalarGridSpec(
            num_scalar_prefetch=1, grid=(r // tr,), in_specs=[mine_spec, other_spec, blk, blk, blk], out_specs=[blk] * 4),
        compiler_params=pltpu.CompilerParams(dimension_semantics=("arbitrary",), vmem_limit_bytes=VMEM_LIMIT),
    )(cidx, mine, other, w, m, v)


BIG = ["ffn1_w_gate", "ffn1_w_up", "ffn1_w_down", "w_in", "gla_gate_up", "w_branch_fox", "w_branch_gla", "w_merge_gate", "w_out",
       "ffn2_w_gate", "ffn2_w_up", "ffn2_w_down", "w_ple_proj", "w_ple_gate"]
ROW_SHARDED = ("ffn1_w_down", "w_out", "ffn2_w_down", "w_ple_gate")
SMALL = ["ffn1_norm", "mix_norm", "fox_forget_bias", "gla_gate_bias", "gla_head_norm", "b_merge_gate", "ffn2_norm", "ple_norm", "final_norm"]
NAMES = ["ffn1_norm", "ffn1_w_gate", "ffn1_w_up", "ffn1_w_down", "mix_norm", "w_in", "fox_forget_bias", "gla_gate_up", "gla_gate_bias",
         "gla_head_norm", "w_branch_fox", "w_branch_gla", "w_merge_gate", "b_merge_gate", "w_out", "ffn2_norm", "ffn2_w_gate", "ffn2_w_up",
         "ffn2_w_down", "ple_norm", "w_ple_proj", "w_ple_gate", "final_norm"]
W_IN_COLS = (FOX_W, FOX_W, FOX_W, FOX_HEADS, GLA_KW, GLA_KW, GLA_VW, GLA_VW, GLA_RANK)
SMALL_ROWS, SMALL_COLS = 16, 1024


def _full_matrix(g, name):
    if name in ROW_SHARDED:
        return g.reshape(g.shape[0] * g.shape[1], g.shape[2])
    return jnp.transpose(g, (1, 0, 2)).reshape(g.shape[1], g.shape[0] * g.shape[2])


def _shard_parts(full, name):
    if name in ROW_SHARDED:
        return full.reshape(4, full.shape[0] // 4, full.shape[1])
    return jnp.transpose(full.reshape(full.shape[0], 4, full.shape[1] // 4), (1, 0, 2))


def _gathered_weights(full):
    w_in = full["w_in"]
    offs = [0]
    for cw in W_IN_COLS:
        offs.append(offs[-1] + cw)
    col = lambda i: w_in[:, offs[i]:offs[i + 1]]
    big = jnp.concatenate([col(0), col(1), col(2), col(4), col(5), col(6), col(7), full["w_merge_gate"]], axis=1)
    sm = jnp.concatenate([col(3), col(8), jnp.zeros((D_MODEL, SMALL_W - FOX_HEADS - GLA_RANK), BF16)], axis=1)
    return {
        "gu1": jnp.concatenate([full["ffn1_w_gate"], full["ffn1_w_up"]], axis=1), "d1": full["ffn1_w_down"],
        "big": big, "sm": sm, "bf": full["w_branch_fox"], "bg": full["w_branch_gla"], "out": full["w_out"],
        "gu2": jnp.concatenate([full["ffn2_w_gate"], full["ffn2_w_up"]], axis=1), "d2": full["ffn2_w_down"],
        "pp": full["w_ple_proj"], "pg": full["w_ple_gate"],
    }


def _whole_gradients(dw, dgup):
    ff = dw["d1"].shape[0]
    big, sm = dw["big"], dw["sm"].astype(BF16)
    w_in = jnp.concatenate([big[:, Z_FQ:Z_GQ], sm[:, :FOX_HEADS], big[:, Z_GQ:Z_GL], sm[:, FOX_HEADS:FOX_HEADS + GLA_RANK]], axis=1)
    return {
        "ffn1_w_gate": dw["gu1"][:, :ff], "ffn1_w_up": dw["gu1"][:, ff:], "ffn1_w_down": dw["d1"],
        "w_in": w_in, "gla_gate_up": dgup[FOX_HEADS:FOX_HEADS + GLA_RANK],
        "w_branch_fox": dw["bf"], "w_branch_gla": dw["bg"], "w_merge_gate": big[:, Z_GL:], "w_out": dw["out"],
        "ffn2_w_gate": dw["gu2"][:, :ff], "ffn2_w_up": dw["gu2"][:, ff:], "ffn2_w_down": dw["d2"],
        "w_ple_proj": dw["pp"], "w_ple_gate": dw["pg"],
    }


def _pad_lanes(a, width):
    return jnp.pad(a, ((0, 0), (0, width - a.shape[1])))


def kernel(x, p, ffn1_norm, ffn1_w_gate, ffn1_w_up, ffn1_w_down, mix_norm, w_in, fox_forget_bias, gla_gate_up, gla_gate_bias, gla_head_norm, w_branch_fox, w_branch_gla, w_merge_gate, b_merge_gate, w_out, ffn2_norm, ffn2_w_gate, ffn2_w_up, ffn2_w_down, ple_norm, w_ple_proj, w_ple_gate, final_norm, loss_target, m_ffn1_norm, m_ffn1_w_gate, m_ffn1_w_up, m_ffn1_w_down, m_mix_norm, m_w_in, m_fox_forget_bias, m_gla_gate_up, m_gla_gate_bias, m_gla_head_norm, m_w_branch_fox, m_w_branch_gla, m_w_merge_gate, m_b_merge_gate, m_w_out, m_ffn2_norm, m_ffn2_w_gate, m_ffn2_w_up, m_ffn2_w_down, m_ple_norm, m_w_ple_proj, m_w_ple_gate, m_final_norm, v_ffn1_norm, v_ffn1_w_gate, v_ffn1_w_up, v_ffn1_w_down, v_mix_norm, v_w_in, v_fox_forget_bias, v_gla_gate_up, v_gla_gate_bias, v_gla_head_norm, v_w_branch_fox, v_w_branch_gla, v_w_merge_gate, v_b_merge_gate, v_w_out, v_ffn2_norm, v_ffn2_w_gate, v_ffn2_w_up, v_ffn2_w_down, v_ple_norm, v_w_ple_proj, v_w_ple_gate, v_final_norm):
    args = dict(locals())
    wts = {n: args[n] for n in NAMES}
    mom = {n: args["m_" + n] for n in NAMES}
    var = {n: args["v_" + n] for n in NAMES}
    two_d = lambda a: a.reshape(-1, a.shape[-1])

    wire = lambda n: F32 if n == "gla_gate_up" else BF16
    cidx = lax.axis_index("c").astype(jnp.int32).reshape(1)
    chip = (2 * lax.axis_index("x") + lax.axis_index("y")).astype(jnp.int32)
    shards = [two_d(wts[n]).astype(wire(n)) for n in BIG]
    gathered = [lax.dynamic_update_slice(g, s[None], (chip, 0, 0)) for g, s in zip(_all_gather(shards, "all_gather"), shards)]
    full = {n: _full_matrix(g, n) for n, g in zip(BIG, gathered)}
    w = _gathered_weights(full)
    gup = jnp.zeros((SMALL_W, GLA_KW), F32).at[FOX_HEADS:FOX_HEADS + GLA_RANK].set(full["gla_gate_up"])
    sp = {
        "ffn1_norm": two_d(ffn1_norm), "mix_norm": two_d(mix_norm), "fb": _pad_lanes(two_d(fox_forget_bias), SMALL_W), "gup": gup,
        "gb": two_d(gla_gate_bias), "ghn": two_d(gla_head_norm), "bm": two_d(b_merge_gate), "ffn2_norm": two_d(ffn2_norm),
        "ple_norm": two_d(ple_norm), "final_norm": two_d(final_norm),
    }

    loss, grad_x, dw, ds_ = _local_step(x[0], p[0, 0], loss_target[0], w, sp)

    whole = _whole_gradients(dw, ds_["gup"])
    small_g = {"ffn1_norm": ds_["ffn1_norm"], "mix_norm": ds_["mix_norm"], "fox_forget_bias": ds_["fb"][:, :FOX_HEADS],
               "gla_gate_bias": ds_["gb"], "gla_head_norm": ds_["ghn"], "b_merge_gate": ds_["bm"], "ffn2_norm": ds_["ffn2_norm"],
               "ple_norm": ds_["ple_norm"], "final_norm": ds_["final_norm"]}
    small_w = sum(two_d(wts[n]).shape[1] for n in SMALL)
    assert small_w <= SMALL_ROWS * SMALL_COLS
    packed = lambda d: _pad_lanes(jnp.concatenate([two_d(d[n]) for n in SMALL], axis=1), SMALL_ROWS * SMALL_COLS).reshape(SMALL_ROWS, SMALL_COLS)
    parts = [_shard_parts(whole[n], n) for n in BIG] + [packed(small_g)[None]]
    swapped = _pair_swap(parts, "pair_swap")
    pair_sums = [_sum_half(a, b, cidx, "sum_half") for a, b in zip(parts, swapped)]
    got = _chip_exchange(pair_sums, "chip_exchange")
    mine = [_sum_chips(a, b, chip.reshape(1), "sum_chips") for a, b in zip(got, pair_sums)]
    other = _pair_gather(mine, "pair_gather")

    out = {}
    for n, a, b in zip(BIG, mine[:-1], other[:-1]):
        out[n] = [r.reshape(wts[n].shape) for r in _adamw(a, b, cidx, two_d(wts[n]), two_d(mom[n]), two_d(var[n]), "adamw_" + n)]
    small_out = [r.reshape(1, SMALL_ROWS * SMALL_COLS) for r in _adamw(mine[-1], other[-1], cidx, packed(wts), packed(mom), packed(var), "adamw_small")]
    off = 0
    for n in SMALL:
        cw = two_d(wts[n]).shape[1]
        out[n] = [r[:, off:off + cw].reshape(wts[n].shape) for r in small_out]
        off += cw

    total = lax.psum(loss[0, 0], ("x", "y", "c"))
    return (total, grad_x[None], *[out[n][0] for n in NAMES], *[out[n][1] for n in NAMES],
            *[out[n][2] for n in NAMES], *[out[n][3] for n in NAMES])
```

```python
import functools

import jax
import jax.numpy as jnp
from jax import lax
from jax.experimental import pallas as pl
from jax.experimental.pallas import tpu as pltpu

F32 = jnp.float32
BF16 = jnp.bfloat16
MESH = pl.DeviceIdType.MESH
ANY = pl.BlockSpec(memory_space=pl.ANY)

D_MODEL = 2048
FOX_HEADS = 8
HEAD_DIM = 128
GLA_HEADS = 4
GLA_VAL_DIM = 256
GLA_RANK = 16
GLA_TAU = 16.0
CHUNK = 64
EPS = 1e-6
FOX_W = FOX_HEADS * HEAD_DIM
GLA_KW = GLA_HEADS * HEAD_DIM
GLA_VW = GLA_HEADS * GLA_VAL_DIM
Z_FQ, Z_FK, Z_FV, Z_GQ, Z_GK, Z_GV, Z_GR, Z_GL = 0, 1024, 2048, 3072, 3584, 4096, 5120, 6144
Z_W = Z_GL + 2 * D_MODEL
SMALL_W = 128
NEG = -1e30

ADAM_LR, ADAM_B1, ADAM_B2, ADAM_EPS, ADAM_WD, ADAM_STEP = 0.001, 0.9, 0.999, 1e-08, 0.01, 10

VMEM_LIMIT = 56 * 1024 * 1024


def _pick(n, target, mult=128):
    if n <= target:
        return n
    best = None
    for d in range(mult, target + 1, mult):
        if n % d == 0:
            best = d
    assert best is not None, (n, target)
    return best


def _mm(a, b, *, ta=False, tb=False, out_dtype=BF16, name):
    m, k = (a.shape[1], a.shape[0]) if ta else a.shape
    n = b.shape[0] if tb else b.shape[1]
    assert (b.shape[1] if tb else b.shape[0]) == k
    bk = _pick(k, 2048)
    nk = k // bk
    bm, bn = _pick(m, 1024), _pick(n, 1024 if nk > 1 else 512)
    dims = (((0 if ta else 1,), (1 if tb else 0,)), ((), ()))

    def body(a_ref, b_ref, o_ref, acc_ref):
        part = lax.dot_general(a_ref[...], b_ref[...], dims, preferred_element_type=F32)
        if nk == 1:
            o_ref[...] = part.astype(o_ref.dtype)
            return
        kk = pl.program_id(2)

        @pl.when(kk == 0)
        def _():
            acc_ref[...] = part

        @pl.when(kk > 0)
        def _():
            acc_ref[...] += part

        @pl.when(kk == nk - 1)
        def _():
            o_ref[...] = acc_ref[...].astype(o_ref.dtype)

    a_spec = pl.BlockSpec((bk, bm), lambda i, j, kk: (kk, i)) if ta else pl.BlockSpec((bm, bk), lambda i, j, kk: (i, kk))
    b_spec = pl.BlockSpec((bn, bk), lambda i, j, kk: (j, kk)) if tb else pl.BlockSpec((bk, bn), lambda i, j, kk: (kk, j))
    return pl.pallas_call(
        body, name=name, grid=(m // bm, n // bn, nk),
        in_specs=[a_spec, b_spec], out_specs=pl.BlockSpec((bm, bn), lambda i, j, kk: (i, j)),
        out_shape=jax.ShapeDtypeStruct((m, n), out_dtype),
        scratch_shapes=[pltpu.VMEM((bm, bn), F32)],
        compiler_params=pltpu.CompilerParams(dimension_semantics=("parallel", "parallel", "arbitrary"), vmem_limit_bytes=VMEM_LIMIT),
    )(a, b)


def _rowwise(fn, tiled, bcast, outs, reds=(), *, tt, name):
    t = tiled[0][0].shape[0]
    tt = min(tt, t)
    nin, nout = len(tiled) + len(bcast), len(outs)
    splits = [s[3] for s in tiled] + [s[1] for s in bcast]

    def store(ref, val, acc):
        off = 0
        for piece in val if isinstance(val, (tuple, list)) else (val,):
            w = piece.shape[-1]
            if acc:
                ref[:, off:off + w] += piece.astype(ref.dtype)
            else:
                ref[:, off:off + w] = piece.astype(ref.dtype)
            off += w
        assert off == ref.shape[-1], (name, off, ref.shape)

    def body(*refs):
        args = []
        for ref, sp in zip(refs[:nin], splits):
            if sp is None:
                args.append(ref[...])
            else:
                off = 0
                for w in sp:
                    args.append(ref[:, off:off + w])
                    off += w
        res = fn(*args)
        res = res if isinstance(res, (tuple, list)) else (res,)
        assert len(res) == nout + len(reds), (name, len(res))
        for ref, val in zip(refs[nin:nin + nout], res[:nout]):
            store(ref, val, False)
        if reds:
            @pl.when(pl.program_id(0) == 0)
            def _():
                for ref in refs[nin + nout:]:
                    ref[...] = jnp.zeros(ref.shape, ref.dtype)
            for ref, val in zip(refs[nin + nout:], res[nout:]):
                store(ref, val, True)

    in_specs = [pl.BlockSpec((tt, w), functools.partial(lambda i, cb: (i, cb), cb=cb)) for (_, w, cb, _) in tiled]
    in_specs += [pl.BlockSpec(arr.shape, lambda i: (0, 0)) for (arr, _) in bcast]
    out_specs = [pl.BlockSpec((tt, w), lambda i: (i, 0)) for (w, _) in outs]
    out_specs += [pl.BlockSpec((r, w), lambda i: (0, 0)) for (r, w) in reds]
    out_shape = [jax.ShapeDtypeStruct((t, w), dt) for (w, dt) in outs] + [jax.ShapeDtypeStruct((r, w), F32) for (r, w) in reds]
    return pl.pallas_call(
        body, name=name, grid=(t // tt,), in_specs=in_specs, out_specs=out_specs, out_shape=out_shape,
        compiler_params=pltpu.CompilerParams(dimension_semantics=("arbitrary" if reds else "parallel",), vmem_limit_bytes=VMEM_LIMIT),
    )(*[s[0] for s in tiled], *[s[0] for s in bcast])


def _full(arr):
    return (arr, arr.shape[1], 0, None)


def _f(x):
    return x.astype(F32)


def _rms(x, g):
    return x * lax.rsqrt(jnp.mean(x * x, axis=-1, keepdims=True) + EPS) * g


def _log_sigmoid(x):
    return jnp.minimum(x, 0.0) - jnp.log1p(jnp.exp(-jnp.abs(x)))


def _silu(x):
    return x * jax.nn.sigmoid(x)


def _norm_fwd(x, g, name):
    return _rowwise(lambda xb, gb: _rms(_f(xb), gb), [_full(x)], [(g, None)], [(x.shape[1], BF16)], tt=256, name=name)[0]


def _resnorm_fwd(res, branch, g, coef, name):
    def fn(rb, bb, gb):
        h = rb + coef * _f(bb)
        return h, _rms(h, gb)
    d = res.shape[1]
    return _rowwise(fn, [_full(res), _full(branch)], [(g, None)], [(d, F32), (d, BF16)], tt=256, name=name)


def _norm_bwd(h, dns, dres, g, coef, name):
    nd = len(dns)

    def fn(hb, *rest):
        dn = _f(rest[0])
        for extra in rest[1:nd]:
            dn = dn + _f(extra)
        dr, gb = rest[nd], rest[nd + 1]
        _, vjp = jax.vjp(_rms, hb, gb)
        dh, dg = vjp(dn)
        dh = dh + dr
        return dh, coef * dh, dg
    d = h.shape[1]
    return _rowwise(fn, [_full(h)] + [_full(x) for x in dns] + [_full(dres)], [(g, None)],
                    [(d, F32), (d, BF16)], [(1, d)], tt=256, name=name)


def _act_fwd(gu, name):
    ff = gu.shape[1] // 2
    return _rowwise(lambda gb, ub: _silu(_f(gb)) * _f(ub), [(gu, 2 * ff, 0, (ff, ff))], [], [(ff, BF16)], tt=256, name=name)[0]


def _act_bwd(gu, da, name):
    ff = gu.shape[1] // 2

    def fn(gb, ub, dab):
        _, vjp = jax.vjp(lambda p, q: _silu(p) * q, _f(gb), _f(ub))
        return (vjp(_f(dab)),)
    return _rowwise(fn, [(gu, 2 * ff, 0, (ff, ff)), _full(da)], [], [(2 * ff, BF16)], tt=128, name=name)[0]


def _merge(glf, glg, bf, bg, bmf, bmg):
    return jax.nn.sigmoid(_f(glf) + bmf) * _f(bf) + jax.nn.sigmoid(_f(glg) + bmg) * _f(bg)


def _merge_fwd(z, bf, bg, bm, name):
    d = D_MODEL
    return _rowwise(_merge, [(z, d, Z_GL // d, None), (z, d, Z_GL // d + 1, None), _full(bf), _full(bg)], [(bm, (d, d))],
                    [(d, BF16)], tt=256, name=name)[0]


def _merge_bwd(z, bf, bg, bm, dm, name):
    d = D_MODEL

    def fn(glf, glg, bfb, bgb, dmb, bmf, bmg):
        _, vjp = jax.vjp(_merge, _f(glf), _f(glg), _f(bfb), _f(bgb), bmf, bmg)
        dglf, dglg, dbf, dbg, dbmf, dbmg = vjp(_f(dmb))
        return (dglf, dglg), dbf, dbg, (dbmf, dbmg)
    return _rowwise(fn, [(z, d, Z_GL // d, None), (z, d, Z_GL // d + 1, None), _full(bf), _full(bg), _full(dm)], [(bm, (d, d))],
                    [(2 * d, BF16), (d, BF16), (d, BF16)], [(1, 2 * d)], tt=128, name=name)


def _gla_out(o, gr, g):
    return _rms(o, g) * _silu(_f(gr))


def _gla_out_fwd(o4, gr4, g, name):
    return _rowwise(lambda ob, rb, gb: _gla_out(ob, rb, gb), [_full(o4), _full(gr4)], [(g, None)], [(GLA_VAL_DIM, BF16)], tt=1024, name=name)[0]


def _gla_out_bwd(o4, gr4, g, dy4, name):
    def fn(ob, rb, dyb, gb):
        _, vjp = jax.vjp(_gla_out, ob, _f(rb), gb)
        return vjp(_f(dyb))
    return _rowwise(fn, [_full(o4), _full(gr4), _full(dy4)], [(g, None)], [(GLA_VAL_DIM, F32), (GLA_VAL_DIM, BF16)], [(1, GLA_VAL_DIM)],
                    tt=1024, name=name)


def _small_gates(s, fb, gup, gb):
    lane = lax.broadcasted_iota(jnp.int32, s.shape, 1)
    lf = jnp.where(lane < FOX_HEADS, _log_sigmoid(s + fb), 0.0)
    pre = jnp.dot(s.astype(BF16), gup.astype(BF16), preferred_element_type=F32) + gb
    return lf, _log_sigmoid(pre) / GLA_TAU


def _small_fwd(s, fb, gup, gb, name):
    return _rowwise(_small_gates, [_full(s)], [(fb, None), (gup, None), (gb, None)], [(SMALL_W, F32), (GLA_KW, F32)], tt=256, name=name)


def _small_bwd(s, fb, gup, gb, dlf, dla, name):
    def fn(sb, dlfb, dlab, fbb, gupb, gbb):
        _, vjp = jax.vjp(_small_gates, sb, fbb, gupb, gbb)
        return vjp((dlfb, dlab))
    return _rowwise(fn, [_full(s), _full(dlf), _full(dla)], [(fb, None), (gup, None), (gb, None)],
                    [(SMALL_W, BF16)], [(1, SMALL_W), (SMALL_W, GLA_KW), (1, GLA_KW)], tt=256, name=name)


def _head_fn(h3, pgl, pp, tgt, gf):
    h4 = h3 + jax.nn.sigmoid(pgl) * pp
    err = _rms(h4, gf) - tgt
    return 0.5 * jnp.sum(jnp.mean(err * err, axis=-1, keepdims=True))


def _head(h3, pgl, pp, tgt, gf, name):
    def fn(hb, gl, pb, tb, gfb):
        loss, vjp = jax.vjp(_head_fn, hb, _f(gl), _f(pb), tb, gfb)
        dh, dgl, dpp, _, dgf = vjp(jnp.ones((), F32))
        return dh, dgl, dpp, jnp.full((1, 128), loss, F32), dgf
    d = h3.shape[1]
    return _rowwise(fn, [_full(h3), _full(pgl), _full(pp), _full(tgt)], [(gf, None)],
                    [(d, F32), (d, BF16), (d, BF16)], [(1, 128), (1, d)], tt=256, name=name)


def _cumsum_tokens(a, reverse, name):
    t, w = a.shape
    r = min(256, t)
    nb = t // r

    def body(a_ref, o_ref, carry_ref):
        @pl.when(pl.program_id(0) == 0)
        def _():
            carry_ref[...] = jnp.zeros(carry_ref.shape, F32)
        row = lax.broadcasted_iota(jnp.int32, (r, r), 0)
        col = lax.broadcasted_iota(jnp.int32, (r, r), 1)
        tri = ((col >= row) if reverse else (col <= row)).astype(F32)
        blk = a_ref[...]
        o_ref[...] = jnp.dot(tri, blk, preferred_element_type=F32, precision=lax.Precision.HIGHEST) + carry_ref[...]
        carry_ref[...] += jnp.sum(blk, axis=0, keepdims=True)

    idx = (lambda i: (nb - 1 - i, 0)) if reverse else (lambda i: (i, 0))
    return pl.pallas_call(
        body, name=name, grid=(nb,), in_specs=[pl.BlockSpec((r, w), idx)], out_specs=pl.BlockSpec((r, w), idx),
        out_shape=jax.ShapeDtypeStruct((t, w), F32), scratch_shapes=[pltpu.VMEM((1, w), F32)],
        compiler_params=pltpu.CompilerParams(dimension_semantics=("arbitrary",)),
    )(a)


FOX_TQ, FOX_TK = 256, 512
FOX_SCALE = HEAD_DIM ** -0.5


def _fox_tiles(t):
    tq, tk = min(FOX_TQ, t), min(FOX_TK, t)
    return tq, tk, t // tq, t // tk


def _blocked_t(a, blk):
    return a.reshape(a.shape[0] // blk, blk, a.shape[1]).transpose(0, 2, 1)


def _unblocked_t(b):
    return b.transpose(0, 2, 1).reshape(b.shape[0] * b.shape[2], b.shape[1])


def _fox_scores(k, qt, frep, i, j, masked):
    tk, tq = k.shape[0], qt.shape[1]
    st = jnp.dot(k, qt, preferred_element_type=F32) * FOX_SCALE - jnp.tile(frep, (1, tq // HEAD_DIM))
    if masked:
        key = j * tk + lax.broadcasted_iota(jnp.int32, (tk, tq), 0)
        query = i * tq + lax.broadcasted_iota(jnp.int32, (tk, tq), 1)
        st = jnp.where(key <= query, st, NEG)
    return st


def _fox_fwd(z, qt, vt, frep, name):
    t = z.shape[0]
    tq, tk, nq, nk = _fox_tiles(t)
    kb = Z_FK // HEAD_DIM

    def body(qt_ref, k_ref, vt_ref, frep_ref, ot_ref, lse_ref):
        i = pl.program_id(1)
        qt = qt_ref[...]
        last = ((i + 1) * tq - 1) // tk

        def block(j, carry, masked):
            m, l, acc = carry
            rows = pl.ds(pl.multiple_of(j * tk, tk), tk)
            st = _fox_scores(k_ref[rows, :], qt, frep_ref[rows, :], i, j, masked)
            m_new = jnp.maximum(m, jnp.max(st, axis=0, keepdims=True))
            alpha = jnp.exp(m - m_new)
            p = jnp.exp(st - m_new)
            l = alpha * l + jnp.sum(p, axis=0, keepdims=True)
            acc = alpha * acc + jnp.dot(vt_ref[j], p.astype(BF16), preferred_element_type=F32)
            return m_new, l, acc

        init = (jnp.full((1, tq), NEG, F32), jnp.zeros((1, tq), F32), jnp.zeros((HEAD_DIM, tq), F32))
        m, l, acc = block(last, lax.fori_loop(0, last, lambda j, c: block(j, c, False), init), True)
        ot_ref[...] = (acc / l).astype(ot_ref.dtype)
        lse_ref[...] = m + jnp.log(l)

    stat = pl.BlockSpec((None, None, 1, tq), lambda h, i: (h, i, 0, 0))
    return pl.pallas_call(
        body, name=name, grid=(FOX_HEADS, nq),
        in_specs=[pl.BlockSpec((None, HEAD_DIM, tq), lambda h, i: (i, h, 0)),
                  pl.BlockSpec((t, HEAD_DIM), lambda h, i: (0, kb + h)),
                  pl.BlockSpec((nk, HEAD_DIM, tk), lambda h, i: (0, h, 0)),
                  pl.BlockSpec((None, t, HEAD_DIM), lambda h, i: (h, 0, 0))],
        out_specs=[pl.BlockSpec((None, HEAD_DIM, tq), lambda h, i: (i, h, 0)), stat],
        out_shape=[jax.ShapeDtypeStruct((nq, FOX_W, tq), BF16), jax.ShapeDtypeStruct((FOX_HEADS, nq, 1, tq), F32)],
        compiler_params=pltpu.CompilerParams(dimension_semantics=("parallel", "parallel"), vmem_limit_bytes=VMEM_LIMIT),
    )(qt, z, vt, frep)


def _fox_bwd_q(z, qt, kt, ot, dot, lse, frep, name):
    t = z.shape[0]
    tq, tk, nq, nk = _fox_tiles(t)
    kb, vb = Z_FK // HEAD_DIM, Z_FV // HEAD_DIM

    def body(qt_ref, k_ref, kt_ref, v_ref, ot_ref, dot_ref, lse_ref, frep_ref, dqt_ref, delta_ref, dfq_ref):
        i = pl.program_id(1)
        qt, dot = qt_ref[...], dot_ref[...]
        lse = lse_ref[...]
        delta = jnp.sum(_f(dot) * _f(ot_ref[...]), axis=0, keepdims=True)
        delta_ref[...] = delta
        last = ((i + 1) * tq - 1) // tk

        def block(j, carry, masked):
            dq, dfq = carry
            rows = pl.ds(pl.multiple_of(j * tk, tk), tk)
            p = jnp.exp(_fox_scores(k_ref[rows, :], qt, frep_ref[rows, :], i, j, masked) - lse)
            dp = jnp.dot(v_ref[rows, :], dot, preferred_element_type=F32)
            ds = p * (dp - delta)
            return dq + jnp.dot(kt_ref[j], ds.astype(BF16), preferred_element_type=F32), dfq + jnp.sum(ds, axis=0, keepdims=True)

        init = (jnp.zeros((HEAD_DIM, tq), F32), jnp.zeros((1, tq), F32))
        dq, dfq = block(last, lax.fori_loop(0, last, lambda j, c: block(j, c, False), init), True)
        dqt_ref[...] = (dq * FOX_SCALE).astype(dqt_ref.dtype)
        dfq_ref[...] = dfq

    mine = pl.BlockSpec((None, HEAD_DIM, tq), lambda h, i: (i, h, 0))
    stat = pl.BlockSpec((None, None, 1, tq), lambda h, i: (h, i, 0, 0))
    return pl.pallas_call(
        body, name=name, grid=(FOX_HEADS, nq),
        in_specs=[mine,
                  pl.BlockSpec((t, HEAD_DIM), lambda h, i: (0, kb + h)),
                  pl.BlockSpec((nk, HEAD_DIM, tk), lambda h, i: (0, h, 0)),
                  pl.BlockSpec((t, HEAD_DIM), lambda h, i: (0, vb + h)),
                  mine, mine, stat,
                  pl.BlockSpec((None, t, HEAD_DIM), lambda h, i: (h, 0, 0))],
        out_specs=[mine, stat, stat],
        out_shape=[jax.ShapeDtypeStruct((nq, FOX_W, tq), BF16), jax.ShapeDtypeStruct((FOX_HEADS, nq, 1, tq), F32),
                   jax.ShapeDtypeStruct((FOX_HEADS, nq, 1, tq), F32)],
        compiler_params=pltpu.CompilerParams(dimension_semantics=("parallel", "parallel"), vmem_limit_bytes=VMEM_LIMIT),
    )(qt, z, kt, z, ot, dot, lse, frep)


def _fox_bwd_kv(z, qt, do, dot, lse, delta, frep, name):
    t = z.shape[0]
    tq, tk, nq, nk = _fox_tiles(t)
    qb, kb, vb = Z_FQ // HEAD_DIM, Z_FK // HEAD_DIM, Z_FV // HEAD_DIM
    per = tk // tq

    def body(k_ref, v_ref, frep_ref, q_ref, qt_ref, do_ref, dot_ref, lse_ref, delta_ref, dk_ref, dv_ref, dfk_ref):
        j = pl.program_id(1)
        k, v, frep = k_ref[...], v_ref[...], frep_ref[...]

        def block(i, carry, masked):
            dk, dv, dfk = carry
            rows = pl.ds(pl.multiple_of(i * tq, tq), tq)
            p = jnp.exp(_fox_scores(k, qt_ref[i], frep, i, j, masked) - lse_ref[i])
            dv = dv + jnp.dot(p.astype(BF16), do_ref[rows, :], preferred_element_type=F32)
            dp = jnp.dot(v, dot_ref[i], preferred_element_type=F32)
            ds = p * (dp - delta_ref[i])
            dk = dk + jnp.dot(ds.astype(BF16), q_ref[rows, :], preferred_element_type=F32)
            for part in range(tq // HEAD_DIM):
                dfk = dfk + ds[:, part * HEAD_DIM:(part + 1) * HEAD_DIM]
            return dk, dv, dfk

        zero = jnp.zeros((tk, HEAD_DIM), F32)
        carry = (zero, zero, zero)
        for step in range(per):
            carry = block(j * per + step, carry, True)
        dk, dv, dfk = lax.fori_loop((j + 1) * per, nq, lambda i, c: block(i, c, False), carry)
        dk_ref[...] = (dk * FOX_SCALE).astype(dk_ref.dtype)
        dv_ref[...] = dv.astype(dv_ref.dtype)
        dfk_ref[...] = jnp.sum(dfk, axis=1, keepdims=True)

    whole_t = pl.BlockSpec((nq, HEAD_DIM, tq), lambda h, j: (0, h, 0))
    whole_stat = pl.BlockSpec((None, nq, 1, tq), lambda h, j: (h, 0, 0, 0))
    return pl.pallas_call(
        body, name=name, grid=(FOX_HEADS, nk),
        in_specs=[pl.BlockSpec((tk, HEAD_DIM), lambda h, j: (j, kb + h)),
                  pl.BlockSpec((tk, HEAD_DIM), lambda h, j: (j, vb + h)),
                  pl.BlockSpec((None, tk, HEAD_DIM), lambda h, j: (h, j, 0)),
                  pl.BlockSpec((t, HEAD_DIM), lambda h, j: (0, qb + h)),
                  whole_t,
                  pl.BlockSpec((t, HEAD_DIM), lambda h, j: (0, h)),
                  whole_t, whole_stat, whole_stat],
        out_specs=[pl.BlockSpec((tk, HEAD_DIM), lambda h, j: (j, h)), pl.BlockSpec((tk, HEAD_DIM), lambda h, j: (j, h)),
                   pl.BlockSpec((None, tk, 1), lambda h, j: (h, j, 0))],
        out_shape=[jax.ShapeDtypeStruct((t, FOX_W), BF16), jax.ShapeDtypeStruct((t, FOX_W), BF16),
                   jax.ShapeDtypeStruct((FOX_HEADS, t, 1), F32)],
        compiler_params=pltpu.CompilerParams(dimension_semantics=("parallel", "parallel"), vmem_limit_bytes=VMEM_LIMIT),
    )(z, z, frep, z, qt, do, dot, lse, delta)


def _gla_step(st, q, k, v, la):
    row = lax.broadcasted_iota(jnp.int32, (CHUNK, CHUNK), 0)
    col = lax.broadcasted_iota(jnp.int32, (CHUNK, CHUNK), 1)
    tri = (col <= row).astype(F32)
    a_cum = jnp.dot(tri, la, preferred_element_type=F32, precision=lax.Precision.HIGHEST)
    a_tot = jnp.sum(la, axis=0, keepdims=True)
    k_dec = (_f(k) * jnp.exp(a_tot - a_cum)).astype(BF16)
    qs = (_f(q) * (HEAD_DIM ** -0.5)).astype(BF16)
    st = st * jnp.exp(a_tot) + lax.dot_general(v.astype(BF16), k_dec, (((0,), (0,)), ((), ())), preferred_element_type=F32)
    o = lax.dot_general(qs, st.astype(BF16), (((1,), (1,)), ((), ())), preferred_element_type=F32)
    return st, o


def _gla_blocks(t):
    r = min(256, t)
    return r, t // r, r // CHUNK


def _gla_fwd(z, la, name):
    t = z.shape[0]
    r, nb, nch = _gla_blocks(t)

    def body(q_ref, k_ref, v_ref, la_ref, o_ref, sp_ref, st_ref):
        @pl.when(pl.program_id(0) == 0)
        def _():
            st_ref[...] = jnp.zeros(st_ref.shape, F32)
        for c in range(nch):
            rows = slice(c * CHUNK, (c + 1) * CHUNK)
            for h in range(GLA_HEADS):
                kc = slice(h * HEAD_DIM, (h + 1) * HEAD_DIM)
                vc = slice(h * GLA_VAL_DIM, (h + 1) * GLA_VAL_DIM)
                st = st_ref[h]
                sp_ref[c, h] = st
                st, o = _gla_step(st, q_ref[rows, kc], k_ref[rows, kc], v_ref[rows, vc], la_ref[rows, kc])
                st_ref[h] = st
                o_ref[rows, vc] = o

    return pl.pallas_call(
        body, name=name, grid=(nb,),
        in_specs=[pl.BlockSpec((r, GLA_KW), lambda i: (i, Z_GQ // GLA_KW)), pl.BlockSpec((r, GLA_KW), lambda i: (i, Z_GK // GLA_KW)),
                  pl.BlockSpec((r, GLA_VW), lambda i: (i, Z_GV // GLA_VW)), pl.BlockSpec((r, GLA_KW), lambda i: (i, 0))],
        out_specs=[pl.BlockSpec((r, GLA_VW), lambda i: (i, 0)),
                   pl.BlockSpec((nch, GLA_HEADS, GLA_VAL_DIM, HEAD_DIM), lambda i: (i, 0, 0, 0))],
        out_shape=[jax.ShapeDtypeStruct((t, GLA_VW), F32),
                   jax.ShapeDtypeStruct((t // CHUNK, GLA_HEADS, GLA_VAL_DIM, HEAD_DIM), F32)],
        scratch_shapes=[pltpu.VMEM((GLA_HEADS, GLA_VAL_DIM, HEAD_DIM), F32)],
        compiler_params=pltpu.CompilerParams(dimension_semantics=("arbitrary",), vmem_limit_bytes=VMEM_LIMIT),
    )(z, z, z, la)


def _gla_bwd(z, la, sprev, do, name):
    t = z.shape[0]
    r, nb, nch = _gla_blocks(t)

    def body(q_ref, k_ref, v_ref, la_ref, sp_ref, do_ref, dq_ref, dk_ref, dv_ref, dla_ref, dst_ref):
        @pl.when(pl.program_id(0) == 0)
        def _():
            dst_ref[...] = jnp.zeros(dst_ref.shape, F32)
        for c in reversed(range(nch)):
            rows = slice(c * CHUNK, (c + 1) * CHUNK)
            for h in range(GLA_HEADS):
                kc = slice(h * HEAD_DIM, (h + 1) * HEAD_DIM)
                vc = slice(h * GLA_VAL_DIM, (h + 1) * GLA_VAL_DIM)
                _, vjp = jax.vjp(_gla_step, sp_ref[c, h], q_ref[rows, kc], k_ref[rows, kc], v_ref[rows, vc], la_ref[rows, kc])
                dst, dq, dk, dv, dla = vjp((dst_ref[h], do_ref[rows, vc]))
                dst_ref[h] = dst
                dq_ref[rows, kc] = dq
                dk_ref[rows, kc] = dk
                dv_ref[rows, vc] = dv
                dla_ref[rows, kc] = dla

    rev = lambda i: (nb - 1 - i, 0)
    return pl.pallas_call(
        body, name=name, grid=(nb,),
        in_specs=[pl.BlockSpec((r, GLA_KW), lambda i: (nb - 1 - i, Z_GQ // GLA_KW)), pl.BlockSpec((r, GLA_KW), lambda i: (nb - 1 - i, Z_GK // GLA_KW)),
                  pl.BlockSpec((r, GLA_VW), lambda i: (nb - 1 - i, Z_GV // GLA_VW)), pl.BlockSpec((r, GLA_KW), rev),
                  pl.BlockSpec((nch, GLA_HEADS, GLA_VAL_DIM, HEAD_DIM), lambda i: (nb - 1 - i, 0, 0, 0)),
                  pl.BlockSpec((r, GLA_VW), rev)],
        out_specs=[pl.BlockSpec((r, GLA_KW), rev), pl.BlockSpec((r, GLA_KW), rev), pl.BlockSpec((r, GLA_VW), rev), pl.BlockSpec((r, GLA_KW), rev)],
        out_shape=[jax.ShapeDtypeStruct((t, GLA_KW), BF16), jax.ShapeDtypeStruct((t, GLA_KW), BF16),
                   jax.ShapeDtypeStruct((t, GLA_VW), BF16), jax.ShapeDtypeStruct((t, GLA_KW), F32)],
        scratch_shapes=[pltpu.VMEM((GLA_HEADS, GLA_VAL_DIM, HEAD_DIM), F32)],
        compiler_params=pltpu.CompilerParams(dimension_semantics=("arbitrary",), vmem_limit_bytes=VMEM_LIMIT),
    )(z, z, z, la, sprev, do)


def _local_step(x, p, tgt, w, sp):
    t = x.shape[0]
    tq, tk, _, _ = _fox_tiles(t)

    n1 = _norm_fwd(x, sp["ffn1_norm"], "norm1_fwd")
    gu1 = _mm(n1, w["gu1"], name="mm_gu")
    a1 = _act_fwd(gu1, "act_fwd")
    f1 = _mm(a1, w["d1"], out_dtype=F32, name="mm_down")
    h1, u = _resnorm_fwd(x, f1, sp["mix_norm"], 0.5, "resnorm_fwd_half")
    z = _mm(u, w["big"], name="mm_in")
    s = _mm(u, w["sm"], out_dtype=F32, name="mm_in_small")
    lf, la = _small_fwd(s, sp["fb"], sp["gup"], sp["gb"], "small_fwd")
    fp = _cumsum_tokens(lf, False, "cumsum_fwd")
    frep = jnp.broadcast_to(fp[:, :FOX_HEADS].T[:, :, None], (FOX_HEADS, t, HEAD_DIM))
    qt = _blocked_t(z[:, Z_FQ:Z_FQ + FOX_W], tq)
    kt = _blocked_t(z[:, Z_FK:Z_FK + FOX_W], tk)
    vt = _blocked_t(z[:, Z_FV:Z_FV + FOX_W], tk)
    ot, lse = _fox_fwd(z, qt, vt, frep, "fox_fwd")
    y_fox = _unblocked_t(ot)
    o_gla, sprev = _gla_fwd(z, la, "gla_fwd")
    o4 = o_gla.reshape(t * GLA_HEADS, GLA_VAL_DIM)
    gr4 = z[:, Z_GR:Z_GR + GLA_VW].reshape(t * GLA_HEADS, GLA_VAL_DIM)
    y_gla = _gla_out_fwd(o4, gr4, sp["ghn"], "gla_out_fwd").reshape(t, GLA_VW)
    bf = _mm(y_fox, w["bf"], name="mm_branch")
    bg = _mm(y_gla, w["bg"], name="mm_branch")
    merged = _merge_fwd(z, bf, bg, sp["bm"], "merge_fwd")
    mo = _mm(merged, w["out"], out_dtype=F32, name="mm_out")
    h2, n2 = _resnorm_fwd(h1, mo, sp["ffn2_norm"], 1.0, "resnorm_fwd_one")
    gu2 = _mm(n2, w["gu2"], name="mm_gu")
    a2 = _act_fwd(gu2, "act_fwd")
    f2 = _mm(a2, w["d2"], out_dtype=F32, name="mm_down")
    h3, n4 = _resnorm_fwd(h2, f2, sp["ple_norm"], 0.5, "resnorm_fwd_half")
    pgl = _mm(n4, w["pg"], name="mm_pg")
    pb = p.astype(BF16)
    pp = _mm(pb, w["pp"], name="mm_pp")

    dh3, dpgl, dpp, loss, d_final = _head(h3, pgl, pp, tgt, sp["final_norm"], "head")
    dw, ds_ = {}, {"final_norm": d_final}
    dw["pg"] = _mm(n4, dpgl, ta=True, name="mm_dw_sq")
    dw["pp"] = _mm(pb, dpp, ta=True, name="mm_dw_pp")
    dn4 = _mm(dpgl, w["pg"], tb=True, out_dtype=F32, name="mm_dx_sq_f32")
    dh3, df2, ds_["ple_norm"] = _norm_bwd(h3, [dn4], dh3, sp["ple_norm"], 0.5, "norm_bwd_1")

    def ffn_bwd(n, gu, a, df, wgu, wd):
        dwd = _mm(a, df, ta=True, name="mm_dw_down")
        da = _mm(df, wd, tb=True, name="mm_dx_down")
        dgu = _act_bwd(gu, da, "act_bwd")
        dwgu = _mm(n, dgu, ta=True, name="mm_dw_gu")
        dn = _mm(dgu, wgu, tb=True, out_dtype=F32, name="mm_dx_gu")
        return dwgu, dwd, dn

    dw["gu2"], dw["d2"], dn2 = ffn_bwd(n2, gu2, a2, df2, w["gu2"], w["d2"])
    dh2, dmix, ds_["ffn2_norm"] = _norm_bwd(h2, [dn2], dh3, sp["ffn2_norm"], 1.0, "norm_bwd_1")

    dw["out"] = _mm(merged, dmix, ta=True, name="mm_dw_sq")
    dmerged = _mm(dmix, w["out"], tb=True, name="mm_dx_sq")
    dgl, dbf, dbg, ds_["bm"] = _merge_bwd(z, bf, bg, sp["bm"], dmerged, "merge_bwd")
    dw["bf"] = _mm(y_fox, dbf, ta=True, name="mm_dw_branch")
    dw["bg"] = _mm(y_gla, dbg, ta=True, name="mm_dw_branch")
    dy_fox = _mm(dbf, w["bf"], tb=True, name="mm_dx_branch")
    dy_gla = _mm(dbg, w["bg"], tb=True, name="mm_dx_branch")

    do4, dgr4, ds_["ghn"] = _gla_out_bwd(o4, gr4, sp["ghn"], dy_gla.reshape(t * GLA_HEADS, GLA_VAL_DIM), "gla_out_bwd")
    dgq, dgk, dgv, dla = _gla_bwd(z, la, sprev, do4.reshape(t, GLA_VW), "gla_bwd")
    dot = _blocked_t(dy_fox, tq)
    dqt, delta, df_query = _fox_bwd_q(z, qt, kt, ot, dot, lse, frep, "fox_bwd_q")
    dfq = _unblocked_t(dqt)
    dfk, dfv, df_key = _fox_bwd_kv(z, qt, dy_fox, dot, lse, delta, frep, "fox_bwd_kv")
    df = df_query.reshape(FOX_HEADS, t) - df_key.reshape(FOX_HEADS, t)
    dfp = jnp.pad(df.T, ((0, 0), (0, SMALL_W - FOX_HEADS)))
    dlf = _cumsum_tokens(dfp, True, "cumsum_bwd")
    dsm, ds_["fb"], ds_["gup"], ds_["gb"] = _small_bwd(s, sp["fb"], sp["gup"], sp["gb"], dlf, dla, "small_bwd")
    dz = jnp.concatenate([dfq, dfk, dfv, dgq, dgk, dgv, dgr4.reshape(t, GLA_VW), dgl], axis=1)
    dw["big"] = _mm(u, dz, ta=True, name="mm_dw_in")
    dw["sm"] = _mm(u, dsm, ta=True, out_dtype=F32, name="mm_dw_in_small")
    du1 = _mm(dz, w["big"], tb=True, out_dtype=F32, name="mm_dx_in")
    du2 = _mm(dsm, w["sm"], tb=True, out_dtype=F32, name="mm_dx_in_small")
    dh1, df1, ds_["mix_norm"] = _norm_bwd(h1, [du1, du2], dh2, sp["mix_norm"], 0.5, "norm_bwd_2")

    dw["gu1"], dw["d1"], dn1 = ffn_bwd(n1, gu1, a1, df1, w["gu1"], w["d1"])
    grad_x, _, ds_["ffn1_norm"] = _norm_bwd(x, [dn1], dh1, sp["ffn1_norm"], 1.0, "norm_bwd_1")
    return loss, grad_x, dw, ds_


def _half_rows(ref, which):
    r2 = ref.shape[0] // 2
    return ref.at[pl.ds(pl.multiple_of(which * r2, r2), r2)]


def _all_gather(shards, name):
    n = len(shards)

    def body(*refs):
        ins, outs = refs[:n], refs[n:2 * n]
        ici_send, ici_recv, d2d_send, d2d_recv = refs[2 * n:]
        x, y, c = lax.axis_index("x"), lax.axis_index("y"), lax.axis_index("c")
        chips = [(1 - x, y), (x, 1 - y), (1 - x, 1 - y)]
        slot = lambda chip: 2 * chip[0] + chip[1]

        def over_ici(wi, j, origin):
            return pltpu.make_async_remote_copy(
                src_ref=_half_rows(ins[wi], c), dst_ref=_half_rows(outs[wi].at[slot(origin)], c),
                send_sem=ici_send.at[3 * wi + j], recv_sem=ici_recv.at[3 * wi + j],
                device_id=(chips[j][0], chips[j][1], c), device_id_type=MESH)

        def over_d2d(wi, j, half):
            place = _half_rows(outs[wi].at[slot(chips[j])], half)
            return pltpu.make_async_remote_copy(
                src_ref=place, dst_ref=place, send_sem=d2d_send.at[3 * wi + j], recv_sem=d2d_recv.at[3 * wi + j],
                device_id=(x, y, 1 - c), device_id_type=MESH)

        sends = [over_ici(wi, j, (x, y)) for wi in range(n) for j in range(3)]
        for cp in sends:
            cp.start()
        passed = []
        for wi in range(n):
            for j in range(3):
                over_ici(wi, j, chips[j]).wait_recv()
                passed.append(over_d2d(wi, j, c))
                passed[-1].start()
        for wi in range(n):
            for j in range(3):
                over_d2d(wi, j, 1 - c).wait_recv()
        for cp in sends + passed:
            cp.wait_send()

    sems = [pltpu.SemaphoreType.DMA((3 * n,))] * 4
    return pl.pallas_call(
        body, name=name, in_specs=[ANY] * n, out_specs=[ANY] * n,
        out_shape=[jax.ShapeDtypeStruct((4,) + s.shape, s.dtype) for s in shards],
        scratch_shapes=sems, compiler_params=pltpu.CompilerParams(has_side_effects=True),
    )(*shards)


def _pair_swap(parts, name):
    n = len(parts)

    def body(*refs):
        ins, outs = refs[:n], refs[n:2 * n]
        send_sems, recv_sems = refs[2 * n:]
        x, y, c = lax.axis_index("x"), lax.axis_index("y"), lax.axis_index("c")

        def swap(wi):
            r2 = parts[wi].shape[1] // 2
            return pltpu.make_async_remote_copy(
                src_ref=ins[wi].at[:, pl.ds(pl.multiple_of((1 - c) * r2, r2), r2)], dst_ref=outs[wi],
                send_sem=send_sems.at[wi], recv_sem=recv_sems.at[wi], device_id=(x, y, 1 - c), device_id_type=MESH)

        copies = [swap(wi) for wi in range(n)]
        for cp in copies:
            cp.start()
        for cp in copies:
            cp.wait()

    return pl.pallas_call(
        body, name=name, in_specs=[ANY] * n, out_specs=[ANY] * n,
        out_shape=[jax.ShapeDtypeStruct((s.shape[0], s.shape[1] // 2, s.shape[2]), s.dtype) for s in parts],
        scratch_shapes=[pltpu.SemaphoreType.DMA((n,))] * 2, compiler_params=pltpu.CompilerParams(has_side_effects=True),
    )(*parts)


def _row_tile(r, c, budget=1 << 19):
    return r if r <= 8 else _pick(r, max(8, budget // c), 8)


def _sum_half(parts, other, cidx, name):
    nl, r, cc = parts.shape
    r2 = r // 2
    tr = _row_tile(r2, cc)

    def body(c_ref, p_ref, q_ref, o_ref):
        o_ref[...] = (_f(p_ref[...]) + _f(q_ref[...])).astype(o_ref.dtype)

    return pl.pallas_call(
        body, name=name, out_shape=jax.ShapeDtypeStruct((nl, r2, cc), parts.dtype),
        grid_spec=pltpu.PrefetchScalarGridSpec(
            num_scalar_prefetch=1, grid=(nl, r2 // tr),
            in_specs=[pl.BlockSpec((None, None, tr, cc), lambda l, i, c_ref: (l, c_ref[0], i, 0)),
                      pl.BlockSpec((None, tr, cc), lambda l, i, c_ref: (l, i, 0))],
            out_specs=pl.BlockSpec((None, tr, cc), lambda l, i, c_ref: (l, i, 0))),
        compiler_params=pltpu.CompilerParams(dimension_semantics=("parallel", "parallel"), vmem_limit_bytes=VMEM_LIMIT),
    )(cidx, parts.reshape(nl, 2, r2, cc), other)


def _chip_exchange(sums, name):
    n = len(sums)

    def body(*refs):
        ins, outs = refs[:n], refs[n:2 * n]
        send_sems, recv_sems = refs[2 * n:]
        x, y, c = lax.axis_index("x"), lax.axis_index("y"), lax.axis_index("c")
        chips = [(1 - x, y), (x, 1 - y), (1 - x, 1 - y)]
        slot = lambda chip: 2 * chip[0] + chip[1]

        def src(wi, chip):
            return ins[wi].at[slot(chip) if sums[wi].shape[0] == 4 else 0]

        def remote(wi, j, origin):
            return pltpu.make_async_remote_copy(
                src_ref=src(wi, chips[j]), dst_ref=outs[wi].at[slot(origin)], send_sem=send_sems.at[3 * wi + j], recv_sem=recv_sems.at[3 * wi + j],
                device_id=(chips[j][0], chips[j][1], c), device_id_type=MESH)

        sends = [remote(wi, j, (x, y)) for wi in range(n) for j in range(3)]
        for cp in sends:
            cp.start()
        for wi in range(n):
            for j in range(3):
                remote(wi, j, chips[j]).wait_recv()
        for cp in sends:
            cp.wait_send()

    return pl.pallas_call(
        body, name=name, in_specs=[ANY] * n, out_specs=[ANY] * n,
        out_shape=[jax.ShapeDtypeStruct((4,) + s.shape[1:], s.dtype) for s in sums],
        scratch_shapes=[pltpu.SemaphoreType.DMA((3 * n,)), pltpu.SemaphoreType.DMA((3 * n,))],
        compiler_params=pltpu.CompilerParams(has_side_effects=True),
    )(*sums)


def _sum_chips(got, own, chip, name):
    _, r2, cc = got.shape
    tr = _row_tile(r2, cc)
    per_chip = own.shape[0] == 4

    def body(chip_ref, g_ref, own_ref, o_ref):
        term = lambda k: jnp.where(chip_ref[0] == k, _f(own_ref[...]), _f(g_ref[k]))
        o_ref[...] = ((term(0) + term(1)) + term(2)) + term(3)

    return pl.pallas_call(
        body, name=name, out_shape=jax.ShapeDtypeStruct((r2, cc), F32),
        grid_spec=pltpu.PrefetchScalarGridSpec(
            num_scalar_prefetch=1, grid=(r2 // tr,),
            in_specs=[pl.BlockSpec((4, tr, cc), lambda i, chip_ref: (0, i, 0)),
                      pl.BlockSpec((None, tr, cc), lambda i, chip_ref: (chip_ref[0] if per_chip else 0, i, 0))],
            out_specs=pl.BlockSpec((tr, cc), lambda i, chip_ref: (i, 0))),
        compiler_params=pltpu.CompilerParams(dimension_semantics=("parallel",), vmem_limit_bytes=VMEM_LIMIT),
    )(chip, got, own)


def _pair_gather(halves, name):
    n = len(halves)

    def body(*refs):
        ins, outs = refs[:n], refs[n:2 * n]
        send_sems, recv_sems = refs[2 * n:]
        x, y, c = lax.axis_index("x"), lax.axis_index("y"), lax.axis_index("c")
        copies = [pltpu.make_async_remote_copy(
            src_ref=ins[wi], dst_ref=outs[wi], send_sem=send_sems.at[wi], recv_sem=recv_sems.at[wi],
            device_id=(x, y, 1 - c), device_id_type=MESH) for wi in range(n)]
        for cp in copies:
            cp.start()
        for cp in copies:
            cp.wait()

    return pl.pallas_call(
        body, name=name, in_specs=[ANY] * n, out_specs=[ANY] * n,
        out_shape=[jax.ShapeDtypeStruct(s.shape, s.dtype) for s in halves],
        scratch_shapes=[pltpu.SemaphoreType.DMA((n,))] * 2, compiler_params=pltpu.CompilerParams(has_side_effects=True),
    )(*halves)


def _adamw(mine, other, cidx, w, m, v, name):
    r, c = w.shape
    tr = _row_tile(r // 2, c, 1 << 18)
    nh = (r // 2) // tr

    def body(c_ref, mine_ref, other_ref, w_ref, m_ref, v_ref, g_ref, d_ref, nm_ref, nv_ref):
        g = jnp.where(pl.program_id(0) // nh == c_ref[0], mine_ref[...], other_ref[...])
        g_ref[...] = g
        m_new = ADAM_B1 * m_ref[...] + (1.0 - ADAM_B1) * g
        v_new = ADAM_B2 * v_ref[...] + (1.0 - ADAM_B2) * jnp.square(g)
        m_hat = m_new / (1.0 - ADAM_B1 ** ADAM_STEP)
        v_hat = v_new / (1.0 - ADAM_B2 ** ADAM_STEP)
        d_ref[...] = -ADAM_LR * (m_hat / (jnp.sqrt(v_hat) + ADAM_EPS) + ADAM_WD * w_ref[...])
        nm_ref[...] = m_new
        nv_ref[...] = v_new

    blk = pl.BlockSpec((tr, c), lambda i, c_ref: (i, 0))
    mine_spec = pl.BlockSpec((tr, c), lambda i, c_ref: (jnp.where(i // nh == c_ref[0], i % nh, 0), 0))
    other_spec = pl.BlockSpec((tr, c), lambda i, c_ref: (jnp.where(i // nh == c_ref[0], 0, i % nh), 0))
    return pl.pallas_call(
        body, name=name, out_shape=[jax.ShapeDtypeStruct((r, c), F32)] * 4,
        grid_spec=pltpu.PrefetchScalarGridSpec(
            num_scalar_prefetch=1, grid=(r // tr,), in_specs=[mine_spec, other_spec, blk, blk, blk], out_specs=[blk] * 4),
        compiler_params=pltpu.CompilerParams(dimension_semantics=("arbitrary",), vmem_limit_bytes=VMEM_LIMIT),
    )(cidx, mine, other, w, m, v)


BIG = ["ffn1_w_gate", "ffn1_w_up", "ffn1_w_down", "w_in", "gla_gate_up", "w_branch_fox", "w_branch_gla", "w_merge_gate", "w_out",
       "ffn2_w_gate", "ffn2_w_up", "ffn2_w_down", "w_ple_proj", "w_ple_gate"]
ROW_SHARDED = ("ffn1_w_down", "w_out", "ffn2_w_down", "w_ple_gate")
SMALL = ["ffn1_norm", "mix_norm", "fox_forget_bias", "gla_gate_bias", "gla_head_norm", "b_merge_gate", "ffn2_norm", "ple_norm", "final_norm"]
NAMES = ["ffn1_norm", "ffn1_w_gate", "ffn1_w_up", "ffn1_w_down", "mix_norm", "w_in", "fox_forget_bias", "gla_gate_up", "gla_gate_bias",
         "gla_head_norm", "w_branch_fox", "w_branch_gla", "w_merge_gate", "b_merge_gate", "w_out", "ffn2_norm", "ffn2_w_gate", "ffn2_w_up",
         "ffn2_w_down", "ple_norm", "w_ple_proj", "w_ple_gate", "final_norm"]
W_IN_COLS = (FOX_W, FOX_W, FOX_W, FOX_HEADS, GLA_KW, GLA_KW, GLA_VW, GLA_VW, GLA_RANK)
SMALL_ROWS, SMALL_COLS = 16, 1024


def _full_matrix(g, name):
    if name in ROW_SHARDED:
        return g.reshape(g.shape[0] * g.shape[1], g.shape[2])
    return jnp.transpose(g, (1, 0, 2)).reshape(g.shape[1], g.shape[0] * g.shape[2])


def _shard_parts(full, name):
    if name in ROW_SHARDED:
        return full.reshape(4, full.shape[0] // 4, full.shape[1])
    return jnp.transpose(full.reshape(full.shape[0], 4, full.shape[1] // 4), (1, 0, 2))


def _gathered_weights(full):
    w_in = full["w_in"]
    offs = [0]
    for cw in W_IN_COLS:
        offs.append(offs[-1] + cw)
    col = lambda i: w_in[:, offs[i]:offs[i + 1]]
    big = jnp.concatenate([col(0), col(1), col(2), col(4), col(5), col(6), col(7), full["w_merge_gate"]], axis=1)
    sm = jnp.concatenate([col(3), col(8), jnp.zeros((D_MODEL, SMALL_W - FOX_HEADS - GLA_RANK), BF16)], axis=1)
    return {
        "gu1": jnp.concatenate([full["ffn1_w_gate"], full["ffn1_w_up"]], axis=1), "d1": full["ffn1_w_down"],
        "big": big, "sm": sm, "bf": full["w_branch_fox"], "bg": full["w_branch_gla"], "out": full["w_out"],
        "gu2": jnp.concatenate([full["ffn2_w_gate"], full["ffn2_w_up"]], axis=1), "d2": full["ffn2_w_down"],
        "pp": full["w_ple_proj"], "pg": full["w_ple_gate"],
    }


def _whole_gradients(dw, dgup):
    ff = dw["d1"].shape[0]
    big, sm = dw["big"], dw["sm"].astype(BF16)
    w_in = jnp.concatenate([big[:, Z_FQ:Z_GQ], sm[:, :FOX_HEADS], big[:, Z_GQ:Z_GL], sm[:, FOX_HEADS:FOX_HEADS + GLA_RANK]], axis=1)
    return {
        "ffn1_w_gate": dw["gu1"][:, :ff], "ffn1_w_up": dw["gu1"][:, ff:], "ffn1_w_down": dw["d1"],
        "w_in": w_in, "gla_gate_up": dgup[FOX_HEADS:FOX_HEADS + GLA_RANK],
        "w_branch_fox": dw["bf"], "w_branch_gla": dw["bg"], "w_merge_gate": big[:, Z_GL:], "w_out": dw["out"],
        "ffn2_w_gate": dw["gu2"][:, :ff], "ffn2_w_up": dw["gu2"][:, ff:], "ffn2_w_down": dw["d2"],
        "w_ple_proj": dw["pp"], "w_ple_gate": dw["pg"],
    }


def _pad_lanes(a, width):
    return jnp.pad(a, ((0, 0), (0, width - a.shape[1])))


def kernel(x, p, ffn1_norm, ffn1_w_gate, ffn1_w_up, ffn1_w_down, mix_norm, w_in, fox_forget_bias, gla_gate_up, gla_gate_bias, gla_head_norm, w_branch_fox, w_branch_gla, w_merge_gate, b_merge_gate, w_out, ffn2_norm, ffn2_w_gate, ffn2_w_up, ffn2_w_down, ple_norm, w_ple_proj, w_ple_gate, final_norm, loss_target, m_ffn1_norm, m_ffn1_w_gate, m_ffn1_w_up, m_ffn1_w_down, m_mix_norm, m_w_in, m_fox_forget_bias, m_gla_gate_up, m_gla_gate_bias, m_gla_head_norm, m_w_branch_fox, m_w_branch_gla, m_w_merge_gate, m_b_merge_gate, m_w_out, m_ffn2_norm, m_ffn2_w_gate, m_ffn2_w_up, m_ffn2_w_down, m_ple_norm, m_w_ple_proj, m_w_ple_gate, m_final_norm, v_ffn1_norm, v_ffn1_w_gate, v_ffn1_w_up, v_ffn1_w_down, v_mix_norm, v_w_in, v_fox_forget_bias, v_gla_gate_up, v_gla_gate_bias, v_gla_head_norm, v_w_branch_fox, v_w_branch_gla, v_w_merge_gate, v_b_merge_gate, v_w_out, v_ffn2_norm, v_ffn2_w_gate, v_ffn2_w_up, v_ffn2_w_down, v_ple_norm, v_w_ple_proj, v_w_ple_gate, v_final_norm):
    args = dict(locals())
    wts = {n: args[n] for n in NAMES}
    mom = {n: args["m_" + n] for n in NAMES}
    var = {n: args["v_" + n] for n in NAMES}
    two_d = lambda a: a.reshape(-1, a.shape[-1])

    wire = lambda n: F32 if n == "gla_gate_up" else BF16
    cidx = lax.axis_index("c").astype(jnp.int32).reshape(1)
    chip = (2 * lax.axis_index("x") + lax.axis_index("y")).astype(jnp.int32)
    shards = [two_d(wts[n]).astype(wire(n)) for n in BIG]
    gathered = [lax.dynamic_update_slice(g, s[None], (chip, 0, 0)) for g, s in zip(_all_gather(shards, "all_gather"), shards)]
    full = {n: _full_matrix(g, n) for n, g in zip(BIG, gathered)}
    w = _gathered_weights(full)
    gup = jnp.zeros((SMALL_W, GLA_KW), F32).at[FOX_HEADS:FOX_HEADS + GLA_RANK].set(full["gla_gate_up"])
    sp = {
        "ffn1_norm": two_d(ffn1_norm), "mix_norm": two_d(mix_norm), "fb": _pad_lanes(two_d(fox_forget_bias), SMALL_W), "gup": gup,
        "gb": two_d(gla_gate_bias), "ghn": two_d(gla_head_norm), "bm": two_d(b_merge_gate), "ffn2_norm": two_d(ffn2_norm),
        "ple_norm": two_d(ple_norm), "final_norm": two_d(final_norm),
    }

    loss, grad_x, dw, ds_ = _local_step(x[0], p[0, 0], loss_target[0], w, sp)

    whole = _whole_gradients(dw, ds_["gup"])
    small_g = {"ffn1_norm": ds_["ffn1_norm"], "mix_norm": ds_["mix_norm"], "fox_forget_bias": ds_["fb"][:, :FOX_HEADS],
               "gla_gate_bias": ds_["gb"], "gla_head_norm": ds_["ghn"], "b_merge_gate": ds_["bm"], "ffn2_norm": ds_["ffn2_norm"],
               "ple_norm": ds_["ple_norm"], "final_norm": ds_["final_norm"]}
    small_w = sum(two_d(wts[n]).shape[1] for n in SMALL)
    assert small_w <= SMALL_ROWS * SMALL_COLS
    packed = lambda d: _pad_lanes(jnp.concatenate([two_d(d[n]) for n in SMALL], axis=1), SMALL_ROWS * SMALL_COLS).reshape(SMALL_ROWS, SMALL_COLS)
    parts = [_shard_parts(whole[n], n) for n in BIG] + [packed(small_g)[None]]
    swapped = _pair_swap(parts, "pair_swap")
    pair_sums = [_sum_half(a, b, cidx, "sum_half") for a, b in zip(parts, swapped)]
    got = _chip_exchange(pair_sums, "chip_exchange")
    mine = [_sum_chips(a, b, chip.reshape(1), "sum_chips") for a, b in zip(got, pair_sums)]
    other = _pair_gather(mine, "pair_gather")

    out = {}
    for n, a, b in zip(BIG, mine[:-1], other[:-1]):
        out[n] = [r.reshape(wts[n].shape) for r in _adamw(a, b, cidx, two_d(wts[n]), two_d(mom[n]), two_d(var[n]), "adamw_" + n)]
    small_out = [r.reshape(1, SMALL_ROWS * SMALL_COLS) for r in _adamw(mine[-1], other[-1], cidx, packed(wts), packed(mom), packed(var), "adamw_small")]
    off = 0
    for n in SMALL:
        cw = two_d(wts[n]).shape[1]
        out[n] = [r[:, off:off + cw].reshape(wts[n].shape) for r in small_out]
        off += cw

    total = lax.psum(loss[0, 0], ("x", "y", "c"))
    return (total, grad_x[None], *[out[n][0] for n in NAMES], *[out[n][1] for n in NAMES],
            *[out[n][2] for n in NAMES], *[out[n][3] for n in NAMES])
```

```python
import functools

import jax
import jax.numpy as jnp
from jax import lax
from jax.experimental import pallas as pl
from jax.experimental.pallas import tpu as pltpu

F32 = jnp.float32
BF16 = jnp.bfloat16
MESH = pl.DeviceIdType.MESH
ANY = pl.BlockSpec(memory_space=pl.ANY)

D_MODEL = 2048
FOX_HEADS = 8
HEAD_DIM = 128
GLA_HEADS = 4
GLA_VAL_DIM = 256
GLA_RANK = 16
GLA_TAU = 16.0
CHUNK = 64
EPS = 1e-6
FOX_W = FOX_HEADS * HEAD_DIM
GLA_KW = GLA_HEADS * HEAD_DIM
GLA_VW = GLA_HEADS * GLA_VAL_DIM
Z_FQ, Z_FK, Z_FV, Z_GQ, Z_GK, Z_GV, Z_GR, Z_GL = 0, 1024, 2048, 3072, 3584, 4096, 5120, 6144
Z_W = Z_GL + 2 * D_MODEL
SMALL_W = 128
NEG = -1e30

ADAM_LR, ADAM_B1, ADAM_B2, ADAM_EPS, ADAM_WD, ADAM_STEP = 0.001, 0.9, 0.999, 1e-08, 0.01, 10

VMEM_LIMIT = 56 * 1024 * 1024


def _pick(n, target, mult=128):
    if n <= target:
        return n
    best = None
    for d in range(mult, target + 1, mult):
        if n % d == 0:
            best = d
    assert best is not None, (n, target)
    return best


def _mm(a, b, *, ta=False, tb=False, out_dtype=BF16, name, comm=None):
    m, k = (a.shape[1], a.shape[0]) if ta else a.shape
    n = b.shape[0] if tb else b.shape[1]
    assert (b.shape[1] if tb else b.shape[0]) == k
    bk = _pick(k, 2048)
    nk = k // bk
    bm, bn = _pick(m, 1024), _pick(n, 1024 if nk > 1 else 512)
    dims = (((0 if ta else 1,), (1 if tb else 0,)), ((), ()))

    def body(a_ref, b_ref, o_ref, acc_ref):
        part = lax.dot_general(a_ref[...], b_ref[...], dims, preferred_element_type=F32)
        if nk == 1:
            o_ref[...] = part.astype(o_ref.dtype)
            return
        kk = pl.program_id(2)

        @pl.when(kk == 0)
        def _():
            acc_ref[...] = part

        @pl.when(kk > 0)
        def _():
            acc_ref[...] += part

        @pl.when(kk == nk - 1)
        def _():
            o_ref[...] = acc_ref[...].astype(o_ref.dtype)

    a_spec = pl.BlockSpec((bk, bm), lambda i, j, kk: (kk, i)) if ta else pl.BlockSpec((bm, bk), lambda i, j, kk: (i, kk))
    b_spec = pl.BlockSpec((bn, bk), lambda i, j, kk: (j, kk)) if tb else pl.BlockSpec((bk, bn), lambda i, j, kk: (kk, j))
    (out,), travelled = _hosted(
        body, comm, name=name, grid=(m // bm, n // bn, nk),
        in_specs=[a_spec, b_spec], out_specs=[pl.BlockSpec((bm, bn), lambda i, j, kk: (i, j))],
        out_shape=[jax.ShapeDtypeStruct((m, n), out_dtype)], scratch_shapes=[pltpu.VMEM((bm, bn), F32)],
        semantics=("parallel", "parallel", "arbitrary"), args=(a, b))
    return out if comm is None else (out, travelled)


def _rowwise(fn, tiled, bcast, outs, reds=(), *, tt, name):
    t = tiled[0][0].shape[0]
    tt = min(tt, t)
    nin, nout = len(tiled) + len(bcast), len(outs)
    splits = [s[3] for s in tiled] + [s[1] for s in bcast]

    def store(ref, val, acc):
        off = 0
        for piece in val if isinstance(val, (tuple, list)) else (val,):
            w = piece.shape[-1]
            if acc:
                ref[:, off:off + w] += piece.astype(ref.dtype)
            else:
                ref[:, off:off + w] = piece.astype(ref.dtype)
            off += w
        assert off == ref.shape[-1], (name, off, ref.shape)

    def body(*refs):
        args = []
        for ref, sp in zip(refs[:nin], splits):
            if sp is None:
                args.append(ref[...])
            else:
                off = 0
                for w in sp:
                    args.append(ref[:, off:off + w])
                    off += w
        res = fn(*args)
        res = res if isinstance(res, (tuple, list)) else (res,)
        assert len(res) == nout + len(reds), (name, len(res))
        for ref, val in zip(refs[nin:nin + nout], res[:nout]):
            store(ref, val, False)
        if reds:
            @pl.when(pl.program_id(0) == 0)
            def _():
                for ref in refs[nin + nout:]:
                    ref[...] = jnp.zeros(ref.shape, ref.dtype)
            for ref, val in zip(refs[nin + nout:], res[nout:]):
                store(ref, val, True)

    in_specs = [pl.BlockSpec((tt, w), functools.partial(lambda i, cb: (i, cb), cb=cb)) for (_, w, cb, _) in tiled]
    in_specs += [pl.BlockSpec(arr.shape, lambda i: (0, 0)) for (arr, _) in bcast]
    out_specs = [pl.BlockSpec((tt, w), lambda i: (i, 0)) for (w, _) in outs]
    out_specs += [pl.BlockSpec((r, w), lambda i: (0, 0)) for (r, w) in reds]
    out_shape = [jax.ShapeDtypeStruct((t, w), dt) for (w, dt) in outs] + [jax.ShapeDtypeStruct((r, w), F32) for (r, w) in reds]
    return pl.pallas_call(
        body, name=name, grid=(t // tt,), in_specs=in_specs, out_specs=out_specs, out_shape=out_shape,
        compiler_params=pltpu.CompilerParams(dimension_semantics=("arbitrary" if reds else "parallel",), vmem_limit_bytes=VMEM_LIMIT),
    )(*[s[0] for s in tiled], *[s[0] for s in bcast])


def _full(arr):
    return (arr, arr.shape[1], 0, None)


def _f(x):
    return x.astype(F32)


def _rms(x, g):
    return x * lax.rsqrt(jnp.mean(x * x, axis=-1, keepdims=True) + EPS) * g


def _log_sigmoid(x):
    return jnp.minimum(x, 0.0) - jnp.log1p(jnp.exp(-jnp.abs(x)))


def _silu(x):
    return x * jax.nn.sigmoid(x)


def _norm_fwd(x, g, name):
    return _rowwise(lambda xb, gb: _rms(_f(xb), gb), [_full(x)], [(g, None)], [(x.shape[1], BF16)], tt=256, name=name)[0]


def _resnorm_fwd(res, branch, g, coef, name):
    def fn(rb, bb, gb):
        h = rb + coef * _f(bb)
        return h, _rms(h, gb)
    d = res.shape[1]
    return _rowwise(fn, [_full(res), _full(branch)], [(g, None)], [(d, F32), (d, BF16)], tt=256, name=name)


def _norm_bwd(h, dns, dres, g, coef, name):
    nd = len(dns)

    def fn(hb, *rest):
        dn = _f(rest[0])
        for extra in rest[1:nd]:
            dn = dn + _f(extra)
        dr, gb = rest[nd], rest[nd + 1]
        _, vjp = jax.vjp(_rms, hb, gb)
        dh, dg = vjp(dn)
        dh = dh + dr
        return dh, coef * dh, dg
    d = h.shape[1]
    return _rowwise(fn, [_full(h)] + [_full(x) for x in dns] + [_full(dres)], [(g, None)],
                    [(d, F32), (d, BF16)], [(1, d)], tt=256, name=name)


def _act_fwd(gu, name):
    ff = gu.shape[1] // 2
    return _rowwise(lambda gb, ub: _silu(_f(gb)) * _f(ub), [(gu, 2 * ff, 0, (ff, ff))], [], [(ff, BF16)], tt=256, name=name)[0]


def _act_bwd(gu, da, name):
    ff = gu.shape[1] // 2

    def fn(gb, ub, dab):
        _, vjp = jax.vjp(lambda p, q: _silu(p) * q, _f(gb), _f(ub))
        return (vjp(_f(dab)),)
    return _rowwise(fn, [(gu, 2 * ff, 0, (ff, ff)), _full(da)], [], [(2 * ff, BF16)], tt=128, name=name)[0]


def _merge(glf, glg, bf, bg, bmf, bmg):
    return jax.nn.sigmoid(_f(glf) + bmf) * _f(bf) + jax.nn.sigmoid(_f(glg) + bmg) * _f(bg)


def _merge_fwd(z, bf, bg, bm, name):
    d = D_MODEL
    return _rowwise(_merge, [(z, d, Z_GL // d, None), (z, d, Z_GL // d + 1, None), _full(bf), _full(bg)], [(bm, (d, d))],
                    [(d, BF16)], tt=256, name=name)[0]


def _merge_bwd(z, bf, bg, bm, dm, name):
    d = D_MODEL

    def fn(glf, glg, bfb, bgb, dmb, bmf, bmg):
        _, vjp = jax.vjp(_merge, _f(glf), _f(glg), _f(bfb), _f(bgb), bmf, bmg)
        dglf, dglg, dbf, dbg, dbmf, dbmg = vjp(_f(dmb))
        return (dglf, dglg), dbf, dbg, (dbmf, dbmg)
    return _rowwise(fn, [(z, d, Z_GL // d, None), (z, d, Z_GL // d + 1, None), _full(bf), _full(bg), _full(dm)], [(bm, (d, d))],
                    [(2 * d, BF16), (d, BF16), (d, BF16)], [(1, 2 * d)], tt=128, name=name)


def _gla_out(o, gr, g):
    return _rms(o, g) * _silu(_f(gr))


def _gla_out_fwd(o4, gr4, g, name):
    return _rowwise(lambda ob, rb, gb: _gla_out(ob, rb, gb), [_full(o4), _full(gr4)], [(g, None)], [(GLA_VAL_DIM, BF16)], tt=1024, name=name)[0]


def _gla_out_bwd(o4, gr4, g, dy4, name):
    def fn(ob, rb, dyb, gb):
        _, vjp = jax.vjp(_gla_out, ob, _f(rb), gb)
        return vjp(_f(dyb))
    return _rowwise(fn, [_full(o4), _full(gr4), _full(dy4)], [(g, None)], [(GLA_VAL_DIM, F32), (GLA_VAL_DIM, BF16)], [(1, GLA_VAL_DIM)],
                    tt=1024, name=name)


def _small_gates(s, fb, gup, gb):
    lane = lax.broadcasted_iota(jnp.int32, s.shape, 1)
    lf = jnp.where(lane < FOX_HEADS, _log_sigmoid(s + fb), 0.0)
    pre = jnp.dot(s.astype(BF16), gup.astype(BF16), preferred_element_type=F32) + gb
    return lf, _log_sigmoid(pre) / GLA_TAU


def _small_fwd(s, fb, gup, gb, name):
    return _rowwise(_small_gates, [_full(s)], [(fb, None), (gup, None), (gb, None)], [(SMALL_W, F32), (GLA_KW, F32)], tt=256, name=name)


def _small_bwd(s, fb, gup, gb, dlf, dla, name):
    def fn(sb, dlfb, dlab, fbb, gupb, gbb):
        _, vjp = jax.vjp(_small_gates, sb, fbb, gupb, gbb)
        return vjp((dlfb, dlab))
    return _rowwise(fn, [_full(s), _full(dlf), _full(dla)], [(fb, None), (gup, None), (gb, None)],
                    [(SMALL_W, BF16)], [(1, SMALL_W), (SMALL_W, GLA_KW), (1, GLA_KW)], tt=256, name=name)


def _head_fn(h3, pgl, pp, tgt, gf):
    h4 = h3 + jax.nn.sigmoid(pgl) * pp
    err = _rms(h4, gf) - tgt
    return 0.5 * jnp.sum(jnp.mean(err * err, axis=-1, keepdims=True))


def _head(h3, pgl, pp, tgt, gf, name):
    def fn(hb, gl, pb, tb, gfb):
        loss, vjp = jax.vjp(_head_fn, hb, _f(gl), _f(pb), tb, gfb)
        dh, dgl, dpp, _, dgf = vjp(jnp.ones((), F32))
        return dh, dgl, dpp, jnp.full((1, 128), loss, F32), dgf
    d = h3.shape[1]
    return _rowwise(fn, [_full(h3), _full(pgl), _full(pp), _full(tgt)], [(gf, None)],
                    [(d, F32), (d, BF16), (d, BF16)], [(1, 128), (1, d)], tt=256, name=name)


def _cumsum_tokens(a, reverse, name):
    t, w = a.shape
    r = min(256, t)
    nb = t // r

    def body(a_ref, o_ref, carry_ref):
        @pl.when(pl.program_id(0) == 0)
        def _():
            carry_ref[...] = jnp.zeros(carry_ref.shape, F32)
        row = lax.broadcasted_iota(jnp.int32, (r, r), 0)
        col = lax.broadcasted_iota(jnp.int32, (r, r), 1)
        tri = ((col >= row) if reverse else (col <= row)).astype(F32)
        blk = a_ref[...]
        o_ref[...] = jnp.dot(tri, blk, preferred_element_type=F32, precision=lax.Precision.HIGHEST) + carry_ref[...]
        carry_ref[...] += jnp.sum(blk, axis=0, keepdims=True)

    idx = (lambda i: (nb - 1 - i, 0)) if reverse else (lambda i: (i, 0))
    return pl.pallas_call(
        body, name=name, grid=(nb,), in_specs=[pl.BlockSpec((r, w), idx)], out_specs=pl.BlockSpec((r, w), idx),
        out_shape=jax.ShapeDtypeStruct((t, w), F32), scratch_shapes=[pltpu.VMEM((1, w), F32)],
        compiler_params=pltpu.CompilerParams(dimension_semantics=("arbitrary",)),
    )(a)


FOX_TQ, FOX_TK = 256, 512
FOX_SCALE = HEAD_DIM ** -0.5


def _fox_tiles(t):
    tq, tk = min(FOX_TQ, t), min(FOX_TK, t)
    return tq, tk, t // tq, t // tk


def _blocked_t(a, blk):
    return a.reshape(a.shape[0] // blk, blk, a.shape[1]).transpose(0, 2, 1)


def _unblocked_t(b):
    return b.transpose(0, 2, 1).reshape(b.shape[0] * b.shape[2], b.shape[1])


def _fox_scores(k, qt, frep, i, j, masked):
    tk, tq = k.shape[0], qt.shape[1]
    st = jnp.dot(k, qt, preferred_element_type=F32) * FOX_SCALE - jnp.tile(frep, (1, tq // HEAD_DIM))
    if masked:
        key = j * tk + lax.broadcasted_iota(jnp.int32, (tk, tq), 0)
        query = i * tq + lax.broadcasted_iota(jnp.int32, (tk, tq), 1)
        st = jnp.where(key <= query, st, NEG)
    return st


def _fox_fwd(z, qt, vt, frep, name, comm=None):
    t = z.shape[0]
    tq, tk, nq, nk = _fox_tiles(t)
    kb = Z_FK // HEAD_DIM

    def body(qt_ref, k_ref, vt_ref, frep_ref, ot_ref, lse_ref):
        i = pl.program_id(1)
        qt = qt_ref[...]
        last = ((i + 1) * tq - 1) // tk

        def block(j, carry, masked):
            m, l, acc = carry
            rows = pl.ds(pl.multiple_of(j * tk, tk), tk)
            st = _fox_scores(k_ref[rows, :], qt, frep_ref[rows, :], i, j, masked)
            m_new = jnp.maximum(m, jnp.max(st, axis=0, keepdims=True))
            alpha = jnp.exp(m - m_new)
            p = jnp.exp(st - m_new)
            l = alpha * l + jnp.sum(p, axis=0, keepdims=True)
            acc = alpha * acc + jnp.dot(vt_ref[j], p.astype(BF16), preferred_element_type=F32)
            return m_new, l, acc

        init = (jnp.full((1, tq), NEG, F32), jnp.zeros((1, tq), F32), jnp.zeros((HEAD_DIM, tq), F32))
        m, l, acc = block(last, lax.fori_loop(0, last, lambda j, c: block(j, c, False), init), True)
        ot_ref[...] = (acc / l).astype(ot_ref.dtype)
        lse_ref[...] = m + jnp.log(l)

    stat = pl.BlockSpec((None, None, 1, tq), lambda h, i: (h, i, 0, 0))
    (ot, lse), travelled = _hosted(
        body, comm, name=name, grid=(FOX_HEADS, nq),
        in_specs=[pl.BlockSpec((None, HEAD_DIM, tq), lambda h, i: (i, h, 0)),
                  pl.BlockSpec((t, HEAD_DIM), lambda h, i: (0, kb + h)),
                  pl.BlockSpec((nk, HEAD_DIM, tk), lambda h, i: (0, h, 0)),
                  pl.BlockSpec((None, t, HEAD_DIM), lambda h, i: (h, 0, 0))],
        out_specs=[pl.BlockSpec((None, HEAD_DIM, tq), lambda h, i: (i, h, 0)), stat],
        out_shape=[jax.ShapeDtypeStruct((nq, FOX_W, tq), BF16), jax.ShapeDtypeStruct((FOX_HEADS, nq, 1, tq), F32)],
        scratch_shapes=[], semantics=("parallel", "parallel"), args=(qt, z, vt, frep))
    return ot, lse, travelled


def _fox_bwd_q(z, qt, kt, ot, dot, lse, frep, name):
    t = z.shape[0]
    tq, tk, nq, nk = _fox_tiles(t)
    kb, vb = Z_FK // HEAD_DIM, Z_FV // HEAD_DIM

    def body(qt_ref, k_ref, kt_ref, v_ref, ot_ref, dot_ref, lse_ref, frep_ref, dqt_ref, delta_ref, dfq_ref):
        i = pl.program_id(1)
        qt, dot = qt_ref[...], dot_ref[...]
        lse = lse_ref[...]
        delta = jnp.sum(_f(dot) * _f(ot_ref[...]), axis=0, keepdims=True)
        delta_ref[...] = delta
        last = ((i + 1) * tq - 1) // tk

        def block(j, carry, masked):
            dq, dfq = carry
            rows = pl.ds(pl.multiple_of(j * tk, tk), tk)
            p = jnp.exp(_fox_scores(k_ref[rows, :], qt, frep_ref[rows, :], i, j, masked) - lse)
            dp = jnp.dot(v_ref[rows, :], dot, preferred_element_type=F32)
            ds = p * (dp - delta)
            return dq + jnp.dot(kt_ref[j], ds.astype(BF16), preferred_element_type=F32), dfq + jnp.sum(ds, axis=0, keepdims=True)

        init = (jnp.zeros((HEAD_DIM, tq), F32), jnp.zeros((1, tq), F32))
        dq, dfq = block(last, lax.fori_loop(0, last, lambda j, c: block(j, c, False), init), True)
        dqt_ref[...] = (dq * FOX_SCALE).astype(dqt_ref.dtype)
        dfq_ref[...] = dfq

    mine = pl.BlockSpec((None, HEAD_DIM, tq), lambda h, i: (i, h, 0))
    stat = pl.BlockSpec((None, None, 1, tq), lambda h, i: (h, i, 0, 0))
    return pl.pallas_call(
        body, name=name, grid=(FOX_HEADS, nq),
        in_specs=[mine,
                  pl.BlockSpec((t, HEAD_DIM), lambda h, i: (0, kb + h)),
                  pl.BlockSpec((nk, HEAD_DIM, tk), lambda h, i: (0, h, 0)),
                  pl.BlockSpec((t, HEAD_DIM), lambda h, i: (0, vb + h)),
                  mine, mine, stat,
                  pl.BlockSpec((None, t, HEAD_DIM), lambda h, i: (h, 0, 0))],
        out_specs=[mine, stat, stat],
        out_shape=[jax.ShapeDtypeStruct((nq, FOX_W, tq), BF16), jax.ShapeDtypeStruct((FOX_HEADS, nq, 1, tq), F32),
                   jax.ShapeDtypeStruct((FOX_HEADS, nq, 1, tq), F32)],
        compiler_params=pltpu.CompilerParams(dimension_semantics=("parallel", "parallel"), vmem_limit_bytes=VMEM_LIMIT),
    )(qt, z, kt, z, ot, dot, lse, frep)


def _fox_bwd_kv(z, qt, do, dot, lse, delta, frep, name):
    t = z.shape[0]
    tq, tk, nq, nk = _fox_tiles(t)
    qb, kb, vb = Z_FQ // HEAD_DIM, Z_FK // HEAD_DIM, Z_FV // HEAD_DIM
    per = tk // tq

    def body(k_ref, v_ref, frep_ref, q_ref, qt_ref, do_ref, dot_ref, lse_ref, delta_ref, dk_ref, dv_ref, dfk_ref):
        j = pl.program_id(1)
        k, v, frep = k_ref[...], v_ref[...], frep_ref[...]

        def block(i, carry, masked):
            dk, dv, dfk = carry
            rows = pl.ds(pl.multiple_of(i * tq, tq), tq)
            p = jnp.exp(_fox_scores(k, qt_ref[i], frep, i, j, masked) - lse_ref[i])
            dv = dv + jnp.dot(p.astype(BF16), do_ref[rows, :], preferred_element_type=F32)
            dp = jnp.dot(v, dot_ref[i], preferred_element_type=F32)
            ds = p * (dp - delta_ref[i])
            dk = dk + jnp.dot(ds.astype(BF16), q_ref[rows, :], preferred_element_type=F32)
            for part in range(tq // HEAD_DIM):
                dfk = dfk + ds[:, part * HEAD_DIM:(part + 1) * HEAD_DIM]
            return dk, dv, dfk

        zero = jnp.zeros((tk, HEAD_DIM), F32)
        carry = (zero, zero, zero)
        for step in range(per):
            carry = block(j * per + step, carry, True)
        dk, dv, dfk = lax.fori_loop((j + 1) * per, nq, lambda i, c: block(i, c, False), carry)
        dk_ref[...] = (dk * FOX_SCALE).astype(dk_ref.dtype)
        dv_ref[...] = dv.astype(dv_ref.dtype)
        dfk_ref[...] = jnp.sum(dfk, axis=1, keepdims=True)

    whole_t = pl.BlockSpec((nq, HEAD_DIM, tq), lambda h, j: (0, h, 0))
    whole_stat = pl.BlockSpec((None, nq, 1, tq), lambda h, j: (h, 0, 0, 0))
    return pl.pallas_call(
        body, name=name, grid=(FOX_HEADS, nk),
        in_specs=[pl.BlockSpec((tk, HEAD_DIM), lambda h, j: (j, kb + h)),
                  pl.BlockSpec((tk, HEAD_DIM), lambda h, j: (j, vb + h)),
                  pl.BlockSpec((None, tk, HEAD_DIM), lambda h, j: (h, j, 0)),
                  pl.BlockSpec((t, HEAD_DIM), lambda h, j: (0, qb + h)),
                  whole_t,
                  pl.BlockSpec((t, HEAD_DIM), lambda h, j: (0, h)),
                  whole_t, whole_stat, whole_stat],
        out_specs=[pl.BlockSpec((tk, HEAD_DIM), lambda h, j: (j, h)), pl.BlockSpec((tk, HEAD_DIM), lambda h, j: (j, h)),
                   pl.BlockSpec((None, tk, 1), lambda h, j: (h, j, 0))],
        out_shape=[jax.ShapeDtypeStruct((t, FOX_W), BF16), jax.ShapeDtypeStruct((t, FOX_W), BF16),
                   jax.ShapeDtypeStruct((FOX_HEADS, t, 1), F32)],
        compiler_params=pltpu.CompilerParams(dimension_semantics=("parallel", "parallel"), vmem_limit_bytes=VMEM_LIMIT),
    )(z, z, frep, z, qt, do, dot, lse, delta)


def _gla_step(st, q, k, v, la):
    row = lax.broadcasted_iota(jnp.int32, (CHUNK, CHUNK), 0)
    col = lax.broadcasted_iota(jnp.int32, (CHUNK, CHUNK), 1)
    tri = (col <= row).astype(F32)
    a_cum = jnp.dot(tri, la, preferred_element_type=F32, precision=lax.Precision.HIGHEST)
    a_tot = jnp.sum(la, axis=0, keepdims=True)
    k_dec = (_f(k) * jnp.exp(a_tot - a_cum)).astype(BF16)
    qs = (_f(q) * (HEAD_DIM ** -0.5)).astype(BF16)
    st = st * jnp.exp(a_tot) + lax.dot_general(v.astype(BF16), k_dec, (((0,), (0,)), ((), ())), preferred_element_type=F32)
    o = lax.dot_general(qs, st.astype(BF16), (((1,), (1,)), ((), ())), preferred_element_type=F32)
    return st, o


def _gla_blocks(t):
    r = min(256, t)
    return r, t // r, r // CHUNK


def _gla_fwd(z, la, name):
    t = z.shape[0]
    r, nb, nch = _gla_blocks(t)

    def body(q_ref, k_ref, v_ref, la_ref, o_ref, sp_ref, st_ref):
        @pl.when(pl.program_id(0) == 0)
        def _():
            st_ref[...] = jnp.zeros(st_ref.shape, F32)
        for c in range(nch):
            rows = slice(c * CHUNK, (c + 1) * CHUNK)
            for h in range(GLA_HEADS):
                kc = slice(h * HEAD_DIM, (h + 1) * HEAD_DIM)
                vc = slice(h * GLA_VAL_DIM, (h + 1) * GLA_VAL_DIM)
                st = st_ref[h]
                sp_ref[c, h] = st
                st, o = _gla_step(st, q_ref[rows, kc], k_ref[rows, kc], v_ref[rows, vc], la_ref[rows, kc])
                st_ref[h] = st
                o_ref[rows, vc] = o

    return pl.pallas_call(
        body, name=name, grid=(nb,),
        in_specs=[pl.BlockSpec((r, GLA_KW), lambda i: (i, Z_GQ // GLA_KW)), pl.BlockSpec((r, GLA_KW), lambda i: (i, Z_GK // GLA_KW)),
                  pl.BlockSpec((r, GLA_VW), lambda i: (i, Z_GV // GLA_VW)), pl.BlockSpec((r, GLA_KW), lambda i: (i, 0))],
        out_specs=[pl.BlockSpec((r, GLA_VW), lambda i: (i, 0)),
                   pl.BlockSpec((nch, GLA_HEADS, GLA_VAL_DIM, HEAD_DIM), lambda i: (i, 0, 0, 0))],
        out_shape=[jax.ShapeDtypeStruct((t, GLA_VW), F32),
                   jax.ShapeDtypeStruct((t // CHUNK, GLA_HEADS, GLA_VAL_DIM, HEAD_DIM), F32)],
        scratch_shapes=[pltpu.VMEM((GLA_HEADS, GLA_VAL_DIM, HEAD_DIM), F32)],
        compiler_params=pltpu.CompilerParams(dimension_semantics=("arbitrary",), vmem_limit_bytes=VMEM_LIMIT),
    )(z, z, z, la)


def _gla_bwd(z, la, sprev, do, name):
    t = z.shape[0]
    r, nb, nch = _gla_blocks(t)

    def body(q_ref, k_ref, v_ref, la_ref, sp_ref, do_ref, dq_ref, dk_ref, dv_ref, dla_ref, dst_ref):
        @pl.when(pl.program_id(0) == 0)
        def _():
            dst_ref[...] = jnp.zeros(dst_ref.shape, F32)
        for c in reversed(range(nch)):
            rows = slice(c * CHUNK, (c + 1) * CHUNK)
            for h in range(GLA_HEADS):
                kc = slice(h * HEAD_DIM, (h + 1) * HEAD_DIM)
                vc = slice(h * GLA_VAL_DIM, (h + 1) * GLA_VAL_DIM)
                _, vjp = jax.vjp(_gla_step, sp_ref[c, h], q_ref[rows, kc], k_ref[rows, kc], v_ref[rows, vc], la_ref[rows, kc])
                dst, dq, dk, dv, dla = vjp((dst_ref[h], do_ref[rows, vc]))
                dst_ref[h] = dst
                dq_ref[rows, kc] = dq
                dk_ref[rows, kc] = dk
                dv_ref[rows, vc] = dv
                dla_ref[rows, kc] = dla

    rev = lambda i: (nb - 1 - i, 0)
    return pl.pallas_call(
        body, name=name, grid=(nb,),
        in_specs=[pl.BlockSpec((r, GLA_KW), lambda i: (nb - 1 - i, Z_GQ // GLA_KW)), pl.BlockSpec((r, GLA_KW), lambda i: (nb - 1 - i, Z_GK // GLA_KW)),
                  pl.BlockSpec((r, GLA_VW), lambda i: (nb - 1 - i, Z_GV // GLA_VW)), pl.BlockSpec((r, GLA_KW), rev),
                  pl.BlockSpec((nch, GLA_HEADS, GLA_VAL_DIM, HEAD_DIM), lambda i: (nb - 1 - i, 0, 0, 0)),
                  pl.BlockSpec((r, GLA_VW), rev)],
        out_specs=[pl.BlockSpec((r, GLA_KW), rev), pl.BlockSpec((r, GLA_KW), rev), pl.BlockSpec((r, GLA_VW), rev), pl.BlockSpec((r, GLA_KW), rev)],
        out_shape=[jax.ShapeDtypeStruct((t, GLA_KW), BF16), jax.ShapeDtypeStruct((t, GLA_KW), BF16),
                   jax.ShapeDtypeStruct((t, GLA_VW), BF16), jax.ShapeDtypeStruct((t, GLA_KW), F32)],
        scratch_shapes=[pltpu.VMEM((GLA_HEADS, GLA_VAL_DIM, HEAD_DIM), F32)],
        compiler_params=pltpu.CompilerParams(dimension_semantics=("arbitrary",), vmem_limit_bytes=VMEM_LIMIT),
    )(z, z, z, la, sprev, do)


def _local_step(x, p, tgt, shards, sp, cidx, chip):
    t = x.shape[0]
    tq, tk, _, _ = _fox_tiles(t)
    full, reduced = {}, {}

    def gather(names):
        return _ag_comm([shards[n] for n in names])

    def landed(names, got):
        for n, g in zip(names, got):
            full[n] = _full_matrix(lax.dynamic_update_slice(g, shards[n][None], (chip, 0, 0)), n)

    def pair_sums(grads):
        parts = [_shard_parts(g, n) for n, g in grads.items()]
        swapped = _pair_swap(parts, "pair_swap")
        return list(grads), [_sum_half(a, b, cidx, "sum_half") for a, b in zip(parts, swapped)]

    def exchanged(names, sums, got):
        for n, own, g in zip(names, sums, got):
            reduced[n] = (g, own)

    first = ["ffn1_w_gate", "ffn1_w_up"]
    landed(first, _run_comm(gather(first), "all_gather"))
    w_gu1 = jnp.concatenate([full["ffn1_w_gate"], full["ffn1_w_up"]], axis=1)
    n1 = _norm_fwd(x, sp["ffn1_norm"], "norm1_fwd")
    names = ["ffn1_w_down", "w_in", "gla_gate_up"]
    gu1, got = _mm(n1, w_gu1, name="mm_gu_gather", comm=gather(names))
    landed(names, got)
    a1 = _act_fwd(gu1, "act_fwd")
    names = ["w_merge_gate", "w_branch_fox", "w_branch_gla"]
    f1, got = _mm(a1, full["ffn1_w_down"], out_dtype=F32, name="mm_down_gather", comm=gather(names))
    landed(names, got)
    w_big, w_sm = _in_layout(full["w_in"], full["w_merge_gate"])
    gup = jnp.zeros((SMALL_W, GLA_KW), F32).at[FOX_HEADS:FOX_HEADS + GLA_RANK].set(full["gla_gate_up"])
    h1, u = _resnorm_fwd(x, f1, sp["mix_norm"], 0.5, "resnorm_fwd_half")
    names = ["w_out", "ffn2_w_gate"]
    z, got = _mm(u, w_big, name="mm_in_gather", comm=gather(names))
    landed(names, got)
    s = _mm(u, w_sm, out_dtype=F32, name="mm_in_small")
    lf, la = _small_fwd(s, sp["fb"], gup, sp["gb"], "small_fwd")
    fp = _cumsum_tokens(lf, False, "cumsum_fwd")
    frep = jnp.broadcast_to(fp[:, :FOX_HEADS].T[:, :, None], (FOX_HEADS, t, HEAD_DIM))
    qt = _blocked_t(z[:, Z_FQ:Z_FQ + FOX_W], tq)
    kt = _blocked_t(z[:, Z_FK:Z_FK + FOX_W], tk)
    vt = _blocked_t(z[:, Z_FV:Z_FV + FOX_W], tk)
    names = ["ffn2_w_up", "ffn2_w_down", "w_ple_proj", "w_ple_gate"]
    ot, lse, got = _fox_fwd(z, qt, vt, frep, "fox_fwd_gather", comm=gather(names))
    landed(names, got)
    w_gu2 = jnp.concatenate([full["ffn2_w_gate"], full["ffn2_w_up"]], axis=1)
    y_fox = _unblocked_t(ot)
    o_gla, sprev = _gla_fwd(z, la, "gla_fwd")
    o4 = o_gla.reshape(t * GLA_HEADS, GLA_VAL_DIM)
    gr4 = z[:, Z_GR:Z_GR + GLA_VW].reshape(t * GLA_HEADS, GLA_VAL_DIM)
    y_gla = _gla_out_fwd(o4, gr4, sp["ghn"], "gla_out_fwd").reshape(t, GLA_VW)
    bf = _mm(y_fox, full["w_branch_fox"], name="mm_branch")
    bg = _mm(y_gla, full["w_branch_gla"], name="mm_branch")
    merged = _merge_fwd(z, bf, bg, sp["bm"], "merge_fwd")
    mo = _mm(merged, full["w_out"], out_dtype=F32, name="mm_out")
    h2, n2 = _resnorm_fwd(h1, mo, sp["ffn2_norm"], 1.0, "resnorm_fwd_one")
    gu2 = _mm(n2, w_gu2, name="mm_gu")
    a2 = _act_fwd(gu2, "act_fwd")
    f2 = _mm(a2, full["ffn2_w_down"], out_dtype=F32, name="mm_down")
    h3, n4 = _resnorm_fwd(h2, f2, sp["ple_norm"], 0.5, "resnorm_fwd_half")
    pgl = _mm(n4, full["w_ple_gate"], name="mm_pg")
    pb = p.astype(BF16)
    pp = _mm(pb, full["w_ple_proj"], name="mm_pp")

    dh3, dpgl, dpp, loss, d_final = _head(h3, pgl, pp, tgt, sp["final_norm"], "head")
    ds_ = {"final_norm": d_final}
    names, sums = pair_sums({"w_ple_gate": _mm(n4, dpgl, ta=True, name="mm_dw_sq"), "w_ple_proj": _mm(pb, dpp, ta=True, name="mm_dw_pp")})
    dn4, got = _mm(dpgl, full["w_ple_gate"], tb=True, out_dtype=F32, name="mm_dx_sq_f32_exchange", comm=_exchange_comm(sums))
    exchanged(names, sums, got)
    dh3, df2, ds_["ple_norm"] = _norm_bwd(h3, [dn4], dh3, sp["ple_norm"], 0.5, "norm_bwd_1")

    def ffn_bwd(n, gu, a, df, wgu, wd, which):
        ff = wd.shape[0]
        names, sums = pair_sums({which + "_w_down": _mm(a, df, ta=True, name="mm_dw_down")})
        da, got = _mm(df, wd, tb=True, name="mm_dx_down_exchange", comm=_exchange_comm(sums))
        exchanged(names, sums, got)
        dgu = _act_bwd(gu, da, "act_bwd")
        dwgu = _mm(n, dgu, ta=True, name="mm_dw_gu")
        names, sums = pair_sums({which + "_w_gate": dwgu[:, :ff], which + "_w_up": dwgu[:, ff:]})
        dn, got = _mm(dgu, wgu, tb=True, out_dtype=F32, name="mm_dx_gu_exchange", comm=_exchange_comm(sums))
        exchanged(names, sums, got)
        return dn

    dn2 = ffn_bwd(n2, gu2, a2, df2, w_gu2, full["ffn2_w_down"], "ffn2")
    dh2, dmix, ds_["ffn2_norm"] = _norm_bwd(h2, [dn2], dh3, sp["ffn2_norm"], 1.0, "norm_bwd_1")

    dw_out = _mm(merged, dmix, ta=True, name="mm_dw_sq")
    dmerged = _mm(dmix, full["w_out"], tb=True, name="mm_dx_sq")
    dgl, dbf, dbg, ds_["bm"] = _merge_bwd(z, bf, bg, sp["bm"], dmerged, "merge_bwd")
    mix_names, mix_sums = pair_sums({"w_out": dw_out, "w_branch_fox": _mm(y_fox, dbf, ta=True, name="mm_dw_branch"),
                                     "w_branch_gla": _mm(y_gla, dbg, ta=True, name="mm_dw_branch")})
    dy_fox = _mm(dbf, full["w_branch_fox"], tb=True, name="mm_dx_branch")
    dy_gla = _mm(dbg, full["w_branch_gla"], tb=True, name="mm_dx_branch")

    do4, dgr4, ds_["ghn"] = _gla_out_bwd(o4, gr4, sp["ghn"], dy_gla.reshape(t * GLA_HEADS, GLA_VAL_DIM), "gla_out_bwd")
    dgq, dgk, dgv, dla = _gla_bwd(z, la, sprev, do4.reshape(t, GLA_VW), "gla_bwd")
    dot = _blocked_t(dy_fox, tq)
    dqt, delta, df_query = _fox_bwd_q(z, qt, kt, ot, dot, lse, frep, "fox_bwd_q")
    dfq = _unblocked_t(dqt)
    dfk, dfv, df_key = _fox_bwd_kv(z, qt, dy_fox, dot, lse, delta, frep, "fox_bwd_kv")
    df = df_query.reshape(FOX_HEADS, t) - df_key.reshape(FOX_HEADS, t)
    dfp = jnp.pad(df.T, ((0, 0), (0, SMALL_W - FOX_HEADS)))
    dlf = _cumsum_tokens(dfp, True, "cumsum_bwd")
    dsm, ds_["fb"], dgup, ds_["gb"] = _small_bwd(s, sp["fb"], gup, sp["gb"], dlf, dla, "small_bwd")
    dz = jnp.concatenate([dfq, dfk, dfv, dgq, dgk, dgv, dgr4.reshape(t, GLA_VW), dgl], axis=1)
    dw_big, got = _mm(u, dz, ta=True, name="mm_dw_in_exchange", comm=_exchange_comm(mix_sums))
    exchanged(mix_names, mix_sums, got)
    dw_sm = _mm(u, dsm, ta=True, out_dtype=F32, name="mm_dw_in_small").astype(BF16)
    dw_in = jnp.concatenate([dw_big[:, Z_FQ:Z_GQ], dw_sm[:, :FOX_HEADS], dw_big[:, Z_GQ:Z_GL], dw_sm[:, FOX_HEADS:FOX_HEADS + GLA_RANK]], axis=1)
    names, sums = pair_sums({"w_in": dw_in, "w_merge_gate": dw_big[:, Z_GL:], "gla_gate_up": dgup[FOX_HEADS:FOX_HEADS + GLA_RANK]})
    du1, got = _mm(dz, w_big, tb=True, out_dtype=F32, name="mm_dx_in_exchange", comm=_exchange_comm(sums))
    exchanged(names, sums, got)
    du2 = _mm(dsm, w_sm, tb=True, out_dtype=F32, name="mm_dx_in_small")
    dh1, df1, ds_["mix_norm"] = _norm_bwd(h1, [du1, du2], dh2, sp["mix_norm"], 0.5, "norm_bwd_2")

    dn1 = ffn_bwd(n1, gu1, a1, df1, w_gu1, full["ffn1_w_down"], "ffn1")
    grad_x, _, ds_["ffn1_norm"] = _norm_bwd(x, [dn1], dh1, sp["ffn1_norm"], 1.0, "norm_bwd_1")
    return loss, grad_x, reduced, ds_


def _half_rows(ref, which):
    r2 = ref.shape[0] // 2
    return ref.at[pl.ds(pl.multiple_of(which * r2, r2), r2)]


class _Comm:
    def __init__(self, ins, out_shape, sems, start, finish):
        self.ins, self.out_shape, self.sems, self.start, self.finish = ins, out_shape, sems, start, finish


def _run_comm(comm, name):
    n_in, n_out = len(comm.ins), len(comm.out_shape)

    def body(*refs):
        parts = refs[:n_in], refs[n_in:n_in + n_out], refs[n_in + n_out:]
        comm.start(*parts)
        comm.finish(*parts)

    return pl.pallas_call(
        body, name=name, in_specs=[ANY] * n_in, out_specs=[ANY] * n_out, out_shape=comm.out_shape,
        scratch_shapes=comm.sems, compiler_params=pltpu.CompilerParams(has_side_effects=True),
    )(*comm.ins)


def _hosted(body, comm, *, name, grid, in_specs, out_specs, out_shape, scratch_shapes, semantics, args):
    if comm is None:
        res = pl.pallas_call(
            body, name=name, grid=grid, in_specs=in_specs, out_specs=out_specs, out_shape=out_shape, scratch_shapes=scratch_shapes,
            compiler_params=pltpu.CompilerParams(dimension_semantics=semantics, vmem_limit_bytes=VMEM_LIMIT),
        )(*args)
        return res, None
    ni, no, ns = len(in_specs), len(out_shape), len(scratch_shapes)
    ci, co = len(comm.ins), len(comm.out_shape)

    def wrapped(*refs):
        h_in, c_in = refs[:ni], refs[ni:ni + ci]
        h_out, c_out = refs[ni + ci:ni + ci + no], refs[ni + ci + no:ni + ci + no + co]
        h_scr, c_sem = refs[ni + ci + no + co:ni + ci + no + co + ns], refs[ni + ci + no + co + ns:]
        ids = [pl.program_id(axis) for axis in range(len(grid))]
        first = functools.reduce(jnp.logical_and, [i == 0 for i in ids])
        last = functools.reduce(jnp.logical_and, [i == g - 1 for i, g in zip(ids, grid)])

        @pl.when(first)
        def _():
            comm.start(c_in, c_out, c_sem)

        body(*h_in, *h_out, *h_scr)

        @pl.when(last)
        def _():
            comm.finish(c_in, c_out, c_sem)

    res = pl.pallas_call(
        wrapped, name=name, grid=grid, in_specs=list(in_specs) + [ANY] * ci, out_specs=list(out_specs) + [ANY] * co,
        out_shape=list(out_shape) + list(comm.out_shape), scratch_shapes=list(scratch_shapes) + list(comm.sems),
        compiler_params=pltpu.CompilerParams(dimension_semantics=("arbitrary",) * len(grid), vmem_limit_bytes=VMEM_LIMIT, has_side_effects=True),
    )(*args, *comm.ins)
    return res[:no], res[no:]


def _ag_comm(shards):
    n = len(shards)

    def copies(ins, outs, sems):
        ici_send, ici_recv, d2d_send, d2d_recv = sems
        x, y, c = lax.axis_index("x"), lax.axis_index("y"), lax.axis_index("c")
        chips = [(1 - x, y), (x, 1 - y), (1 - x, 1 - y)]
        slot = lambda chip: 2 * chip[0] + chip[1]

        def over_ici(wi, j, origin):
            return pltpu.make_async_remote_copy(
                src_ref=_half_rows(ins[wi], c), dst_ref=_half_rows(outs[wi].at[slot(origin)], c),
                send_sem=ici_send.at[3 * wi + j], recv_sem=ici_recv.at[3 * wi + j],
                device_id=(chips[j][0], chips[j][1], c), device_id_type=MESH)

        def over_d2d(wi, j, half):
            place = _half_rows(outs[wi].at[slot(chips[j])], half)
            return pltpu.make_async_remote_copy(
                src_ref=place, dst_ref=place, send_sem=d2d_send.at[3 * wi + j], recv_sem=d2d_recv.at[3 * wi + j],
                device_id=(x, y, 1 - c), device_id_type=MESH)

        return over_ici, over_d2d, (x, y), chips, c

    def start(ins, outs, sems):
        over_ici, _, me, _, _ = copies(ins, outs, sems)
        for wi in range(n):
            for j in range(3):
                over_ici(wi, j, me).start()

    def finish(ins, outs, sems):
        over_ici, over_d2d, me, chips, c = copies(ins, outs, sems)
        for wi in range(n):
            for j in range(3):
                over_ici(wi, j, chips[j]).wait_recv()
                over_d2d(wi, j, c).start()
        for wi in range(n):
            for j in range(3):
                over_d2d(wi, j, 1 - c).wait_recv()
        for wi in range(n):
            for j in range(3):
                over_ici(wi, j, me).wait_send()
                over_d2d(wi, j, c).wait_send()

    return _Comm(list(shards), [jax.ShapeDtypeStruct((4,) + s.shape, s.dtype) for s in shards],
                 [pltpu.SemaphoreType.DMA((3 * n,))] * 4, start, finish)


def _pair_swap(parts, name):
    n = len(parts)

    def body(*refs):
        ins, outs = refs[:n], refs[n:2 * n]
        send_sems, recv_sems = refs[2 * n:]
        x, y, c = lax.axis_index("x"), lax.axis_index("y"), lax.axis_index("c")

        def swap(wi):
            r2 = parts[wi].shape[1] // 2
            return pltpu.make_async_remote_copy(
                src_ref=ins[wi].at[:, pl.ds(pl.multiple_of((1 - c) * r2, r2), r2)], dst_ref=outs[wi],
                send_sem=send_sems.at[wi], recv_sem=recv_sems.at[wi], device_id=(x, y, 1 - c), device_id_type=MESH)

        copies = [swap(wi) for wi in range(n)]
        for cp in copies:
            cp.start()
        for cp in copies:
            cp.wait()

    return pl.pallas_call(
        body, name=name, in_specs=[ANY] * n, out_specs=[ANY] * n,
        out_shape=[jax.ShapeDtypeStruct((s.shape[0], s.shape[1] // 2, s.shape[2]), s.dtype) for s in parts],
        scratch_shapes=[pltpu.SemaphoreType.DMA((n,))] * 2, compiler_params=pltpu.CompilerParams(has_side_effects=True),
    )(*parts)


def _row_tile(r, c, budget=1 << 19):
    return r if r <= 8 else _pick(r, max(8, budget // c), 8)


def _sum_half(parts, other, cidx, name):
    nl, r, cc = parts.shape
    r2 = r // 2
    tr = _row_tile(r2, cc)

    def body(c_ref, p_ref, q_ref, o_ref):
        o_ref[...] = (_f(p_ref[...]) + _f(q_ref[...])).astype(o_ref.dtype)

    return pl.pallas_call(
        body, name=name, out_shape=jax.ShapeDtypeStruct((nl, r2, cc), parts.dtype),
        grid_spec=pltpu.PrefetchScalarGridSpec(
            num_scalar_prefetch=1, grid=(nl, r2 // tr),
            in_specs=[pl.BlockSpec((None, None, tr, cc), lambda l, i, c_ref: (l, c_ref[0], i, 0)),
                      pl.BlockSpec((None, tr, cc), lambda l, i, c_ref: (l, i, 0))],
            out_specs=pl.BlockSpec((None, tr, cc), lambda l, i, c_ref: (l, i, 0))),
        compiler_params=pltpu.CompilerParams(dimension_semantics=("parallel", "parallel"), vmem_limit_bytes=VMEM_LIMIT),
    )(cidx, parts.reshape(nl, 2, r2, cc), other)


def _exchange_comm(sums):
    n = len(sums)

    def copies(ins, outs, sems):
        send_sems, recv_sems = sems
        x, y, c = lax.axis_index("x"), lax.axis_index("y"), lax.axis_index("c")
        chips = [(1 - x, y), (x, 1 - y), (1 - x, 1 - y)]
        slot = lambda chip: 2 * chip[0] + chip[1]

        def remote(wi, j, origin):
            return pltpu.make_async_remote_copy(
                src_ref=ins[wi].at[slot(chips[j]) if sums[wi].shape[0] == 4 else 0], dst_ref=outs[wi].at[slot(origin)],
                send_sem=send_sems.at[3 * wi + j], recv_sem=recv_sems.at[3 * wi + j],
                device_id=(chips[j][0], chips[j][1], c), device_id_type=MESH)

        return remote, (x, y), chips

    def start(ins, outs, sems):
        remote, me, _ = copies(ins, outs, sems)
        for wi in range(n):
            for j in range(3):
                remote(wi, j, me).start()

    def finish(ins, outs, sems):
        remote, me, chips = copies(ins, outs, sems)
        for wi in range(n):
            for j in range(3):
                remote(wi, j, chips[j]).wait_recv()
        for wi in range(n):
            for j in range(3):
                remote(wi, j, me).wait_send()

    return _Comm(list(sums), [jax.ShapeDtypeStruct((4,) + s.shape[1:], s.dtype) for s in sums],
                 [pltpu.SemaphoreType.DMA((3 * n,))] * 2, start, finish)


def _sum_chips(got, own, chip, name):
    _, r2, cc = got.shape
    tr = _row_tile(r2, cc)
    per_chip = own.shape[0] == 4

    def body(chip_ref, g_ref, own_ref, o_ref):
        term = lambda k: jnp.where(chip_ref[0] == k, _f(own_ref[...]), _f(g_ref[k]))
        o_ref[...] = ((term(0) + term(1)) + term(2)) + term(3)

    return pl.pallas_call(
        body, name=name, out_shape=jax.ShapeDtypeStruct((r2, cc), F32),
        grid_spec=pltpu.PrefetchScalarGridSpec(
            num_scalar_prefetch=1, grid=(r2 // tr,),
            in_specs=[pl.BlockSpec((4, tr, cc), lambda i, chip_ref: (0, i, 0)),
                      pl.BlockSpec((None, tr, cc), lambda i, chip_ref: (chip_ref[0] if per_chip else 0, i, 0))],
            out_specs=pl.BlockSpec((tr, cc), lambda i, chip_ref: (i, 0))),
        compiler_params=pltpu.CompilerParams(dimension_semantics=("parallel",), vmem_limit_bytes=VMEM_LIMIT),
    )(chip, got, own)


def _pair_gather(halves, name):
    n = len(halves)

    def body(*refs):
        ins, outs = refs[:n], refs[n:2 * n]
        send_sems, recv_sems = refs[2 * n:]
        x, y, c = lax.axis_index("x"), lax.axis_index("y"), lax.axis_index("c")
        copies = [pltpu.make_async_remote_copy(
            src_ref=ins[wi], dst_ref=outs[wi], send_sem=send_sems.at[wi], recv_sem=recv_sems.at[wi],
            device_id=(x, y, 1 - c), device_id_type=MESH) for wi in range(n)]
        for cp in copies:
            cp.start()
        for cp in copies:
            cp.wait()

    return pl.pallas_call(
        body, name=name, in_specs=[ANY] * n, out_specs=[ANY] * n,
        out_shape=[jax.ShapeDtypeStruct(s.shape, s.dtype) for s in halves],
        scratch_shapes=[pltpu.SemaphoreType.DMA((n,))] * 2, compiler_params=pltpu.CompilerParams(has_side_effects=True),
    )(*halves)


def _adamw(mine, other, cidx, w, m, v, name):
    r, c = w.shape
    tr = _row_tile(r // 2, c, 1 << 18)
    nh = (r // 2) // tr

    def body(c_ref, mine_ref, other_ref, w_ref, m_ref, v_ref, g_ref, d_ref, nm_ref, nv_ref):
        g = jnp.where(pl.program_id(0) // nh == c_ref[0], mine_ref[...], other_ref[...])
        g_ref[...] = g
        m_new = ADAM_B1 * m_ref[...] + (1.0 - ADAM_B1) * g
        v_new = ADAM_B2 * v_ref[...] + (1.0 - ADAM_B2) * jnp.square(g)
        m_hat = m_new / (1.0 - ADAM_B1 ** ADAM_STEP)
        v_hat = v_new / (1.0 - ADAM_B2 ** ADAM_STEP)
        d_ref[...] = -ADAM_LR * (m_hat / (jnp.sqrt(v_hat) + ADAM_EPS) + ADAM_WD * w_ref[...])
        nm_ref[...] = m_new
        nv_ref[...] = v_new

    blk = pl.BlockSpec((tr, c), lambda i, c_ref: (i, 0))
    mine_spec = pl.BlockSpec((tr, c), lambda i, c_ref: (jnp.where(i // nh == c_ref[0], i % nh, 0), 0))
    other_spec = pl.BlockSpec((tr, c), lambda i, c_ref: (jnp.where(i // nh == c_ref[0], 0, i % nh), 0))
    return pl.pallas_call(
        body, name=name, out_shape=[jax.ShapeDtypeStruct((r, c), F32)] * 4,
        grid_spec=pltpu.PrefetchScalarGridSpec(
            num_scalar_prefetch=1, grid=(r // tr,), in_specs=[mine_spec, other_spec, blk, blk, blk], out_specs=[blk] * 4),
        compiler_params=pltpu.CompilerParams(dimension_semantics=("arbitrary",), vmem_limit_bytes=VMEM_LIMIT),
    )(cidx, mine, other, w, m, v)


BIG = ["ffn1_w_gate", "ffn1_w_up", "ffn1_w_down", "w_in", "gla_gate_up", "w_branch_fox", "w_branch_gla", "w_merge_gate", "w_out",
       "ffn2_w_gate", "ffn2_w_up", "ffn2_w_down", "w_ple_proj", "w_ple_gate"]
ROW_SHARDED = ("ffn1_w_down", "w_out", "ffn2_w_down", "w_ple_gate")
SMALL = ["ffn1_norm", "mix_norm", "fox_forget_bias", "gla_gate_bias", "gla_head_norm", "b_merge_gate", "ffn2_norm", "ple_norm", "final_norm"]
NAMES = ["ffn1_norm", "ffn1_w_gate", "ffn1_w_up", "ffn1_w_down", "mix_norm", "w_in", "fox_forget_bias", "gla_gate_up", "gla_gate_bias",
         "gla_head_norm", "w_branch_fox", "w_branch_gla", "w_merge_gate", "b_merge_gate", "w_out", "ffn2_norm", "ffn2_w_gate", "ffn2_w_up",
         "ffn2_w_down", "ple_norm", "w_ple_proj", "w_ple_gate", "final_norm"]
W_IN_COLS = (FOX_W, FOX_W, FOX_W, FOX_HEADS, GLA_KW, GLA_KW, GLA_VW, GLA_VW, GLA_RANK)
SMALL_ROWS, SMALL_COLS = 16, 1024


def _full_matrix(g, name):
    if name in ROW_SHARDED:
        return g.reshape(g.shape[0] * g.shape[1], g.shape[2])
    return jnp.transpose(g, (1, 0, 2)).reshape(g.shape[1], g.shape[0] * g.shape[2])


def _shard_parts(full, name):
    if name in ROW_SHARDED:
        return full.reshape(4, full.shape[0] // 4, full.shape[1])
    return jnp.transpose(full.reshape(full.shape[0], 4, full.shape[1] // 4), (1, 0, 2))


def _in_layout(w_in, w_merge_gate):
    offs = [0]
    for cw in W_IN_COLS:
        offs.append(offs[-1] + cw)
    col = lambda i: w_in[:, offs[i]:offs[i + 1]]
    big = jnp.concatenate([col(0), col(1), col(2), col(4), col(5), col(6), col(7), w_merge_gate], axis=1)
    sm = jnp.concatenate([col(3), col(8), jnp.zeros((D_MODEL, SMALL_W - FOX_HEADS - GLA_RANK), BF16)], axis=1)
    return big, sm


def _pad_lanes(a, width):
    return jnp.pad(a, ((0, 0), (0, width - a.shape[1])))


def kernel(x, p, ffn1_norm, ffn1_w_gate, ffn1_w_up, ffn1_w_down, mix_norm, w_in, fox_forget_bias, gla_gate_up, gla_gate_bias, gla_head_norm, w_branch_fox, w_branch_gla, w_merge_gate, b_merge_gate, w_out, ffn2_norm, ffn2_w_gate, ffn2_w_up, ffn2_w_down, ple_norm, w_ple_proj, w_ple_gate, final_norm, loss_target, m_ffn1_norm, m_ffn1_w_gate, m_ffn1_w_up, m_ffn1_w_down, m_mix_norm, m_w_in, m_fox_forget_bias, m_gla_gate_up, m_gla_gate_bias, m_gla_head_norm, m_w_branch_fox, m_w_branch_gla, m_w_merge_gate, m_b_merge_gate, m_w_out, m_ffn2_norm, m_ffn2_w_gate, m_ffn2_w_up, m_ffn2_w_down, m_ple_norm, m_w_ple_proj, m_w_ple_gate, m_final_norm, v_ffn1_norm, v_ffn1_w_gate, v_ffn1_w_up, v_ffn1_w_down, v_mix_norm, v_w_in, v_fox_forget_bias, v_gla_gate_up, v_gla_gate_bias, v_gla_head_norm, v_w_branch_fox, v_w_branch_gla, v_w_merge_gate, v_b_merge_gate, v_w_out, v_ffn2_norm, v_ffn2_w_gate, v_ffn2_w_up, v_ffn2_w_down, v_ple_norm, v_w_ple_proj, v_w_ple_gate, v_final_norm):
    args = dict(locals())
    wts = {n: args[n] for n in NAMES}
    mom = {n: args["m_" + n] for n in NAMES}
    var = {n: args["v_" + n] for n in NAMES}
    two_d = lambda a: a.reshape(-1, a.shape[-1])

    wire = lambda n: F32 if n == "gla_gate_up" else BF16
    cidx = lax.axis_index("c").astype(jnp.int32).reshape(1)
    chip = (2 * lax.axis_index("x") + lax.axis_index("y")).astype(jnp.int32)
    shards = {n: two_d(wts[n]).astype(wire(n)) for n in BIG}
    sp = {
        "ffn1_norm": two_d(ffn1_norm), "mix_norm": two_d(mix_norm), "fb": _pad_lanes(two_d(fox_forget_bias), SMALL_W),
        "gb": two_d(gla_gate_bias), "ghn": two_d(gla_head_norm), "bm": two_d(b_merge_gate), "ffn2_norm": two_d(ffn2_norm),
        "ple_norm": two_d(ple_norm), "final_norm": two_d(final_norm),
    }

    loss, grad_x, reduced, ds_ = _local_step(x[0], p[0, 0], loss_target[0], shards, sp, cidx, chip)

    small_g = {"ffn1_norm": ds_["ffn1_norm"], "mix_norm": ds_["mix_norm"], "fox_forget_bias": ds_["fb"][:, :FOX_HEADS],
               "gla_gate_bias": ds_["gb"], "gla_head_norm": ds_["ghn"], "b_merge_gate": ds_["bm"], "ffn2_norm": ds_["ffn2_norm"],
               "ple_norm": ds_["ple_norm"], "final_norm": ds_["final_norm"]}
    small_w = sum(two_d(wts[n]).shape[1] for n in SMALL)
    assert small_w <= SMALL_ROWS * SMALL_COLS
    packed = lambda d: _pad_lanes(jnp.concatenate([two_d(d[n]) for n in SMALL], axis=1), SMALL_ROWS * SMALL_COLS).reshape(SMALL_ROWS, SMALL_COLS)
    parts = [packed(small_g)[None]]
    pair_sums = [_sum_half(a, b, cidx, "sum_half") for a, b in zip(parts, _pair_swap(parts, "pair_swap"))]
    reduced["small"] = (_run_comm(_exchange_comm(pair_sums), "chip_exchange")[0], pair_sums[0])
    mine = [_sum_chips(*reduced[n], chip.reshape(1), "sum_chips") for n in BIG + ["small"]]
    other = _pair_gather(mine, "pair_gather")

    out = {}
    for n, a, b in zip(BIG, mine[:-1], other[:-1]):
        out[n] = [r.reshape(wts[n].shape) for r in _adamw(a, b, cidx, two_d(wts[n]), two_d(mom[n]), two_d(var[n]), "adamw_" + n)]
    small_out = [r.reshape(1, SMALL_ROWS * SMALL_COLS) for r in _adamw(mine[-1], other[-1], cidx, packed(wts), packed(mom), packed(var), "adamw_small")]
    off = 0
    for n in SMALL:
        cw = two_d(wts[n]).shape[1]
        out[n] = [r[:, off:off + cw].reshape(wts[n].shape) for r in small_out]
        off += cw

    total = lax.psum(loss[0, 0], ("x", "y", "c"))
    return (total, grad_x[None], *[out[n][0] for n in NAMES], *[out[n][1] for n in NAMES],
            *[out[n][2] for n in NAMES], *[out[n][3] for n in NAMES])
```

```python
import functools

import jax
import jax.numpy as jnp
from jax import lax
from jax.experimental import pallas as pl
from jax.experimental.pallas import tpu as pltpu

F32 = jnp.float32
BF16 = jnp.bfloat16
MESH = pl.DeviceIdType.MESH
ANY = pl.BlockSpec(memory_space=pl.ANY)

D_MODEL = 2048
FOX_HEADS = 8
HEAD_DIM = 128
GLA_HEADS = 4
GLA_VAL_DIM = 256
GLA_RANK = 16
GLA_TAU = 16.0
CHUNK = 64
EPS = 1e-6
FOX_W = FOX_HEADS * HEAD_DIM
GLA_KW = GLA_HEADS * HEAD_DIM
GLA_VW = GLA_HEADS * GLA_VAL_DIM
Z_FQ, Z_FK, Z_FV, Z_GQ, Z_GK, Z_GV, Z_GR, Z_GL = 0, 1024, 2048, 3072, 3584, 4096, 5120, 6144
Z_W = Z_GL + 2 * D_MODEL
SMALL_W = 128
NEG = -1e30

ADAM_LR, ADAM_B1, ADAM_B2, ADAM_EPS, ADAM_WD, ADAM_STEP = 0.001, 0.9, 0.999, 1e-08, 0.01, 10

VMEM_LIMIT = 56 * 1024 * 1024


def _pick(n, target, mult=128):
    if n <= target:
        return n
    best = None
    for d in range(mult, target + 1, mult):
        if n % d == 0:
            best = d
    assert best is not None, (n, target)
    return best


def _mm(a, b, *, ta=False, tb=False, out_dtype=BF16, name, comm=None):
    m, k = (a.shape[1], a.shape[0]) if ta else a.shape
    n = b.shape[0] if tb else b.shape[1]
    assert (b.shape[1] if tb else b.shape[0]) == k
    bk = _pick(k, 2048)
    nk = k // bk
    bm, bn = _pick(m, 1024), _pick(n, 1024 if nk > 1 else 512)
    dims = (((0 if ta else 1,), (1 if tb else 0,)), ((), ()))

    def body(a_ref, b_ref, o_ref, acc_ref):
        part = lax.dot_general(a_ref[...], b_ref[...], dims, preferred_element_type=F32)
        if nk == 1:
            o_ref[...] = part.astype(o_ref.dtype)
            return
        kk = pl.program_id(2)

        @pl.when(kk == 0)
        def _():
            acc_ref[...] = part

        @pl.when(kk > 0)
        def _():
            acc_ref[...] += part

        @pl.when(kk == nk - 1)
        def _():
            o_ref[...] = acc_ref[...].astype(o_ref.dtype)

    a_spec = pl.BlockSpec((bk, bm), lambda i, j, kk: (kk, i)) if ta else pl.BlockSpec((bm, bk), lambda i, j, kk: (i, kk))
    b_spec = pl.BlockSpec((bn, bk), lambda i, j, kk: (j, kk)) if tb else pl.BlockSpec((bk, bn), lambda i, j, kk: (kk, j))
    (out,), travelled = _hosted(
        body, comm, name=name, grid=(m // bm, n // bn, nk),
        in_specs=[a_spec, b_spec], out_specs=[pl.BlockSpec((bm, bn), lambda i, j, kk: (i, j))],
        out_shape=[jax.ShapeDtypeStruct((m, n), out_dtype)], scratch_shapes=[pltpu.VMEM((bm, bn), F32)],
        semantics=("parallel", "parallel", "arbitrary"), args=(a, b))
    return out if comm is None else (out, travelled)


def _rowwise(fn, tiled, bcast, outs, reds=(), *, tt, name):
    t = tiled[0][0].shape[0]
    tt = min(tt, t)
    nin, nout = len(tiled) + len(bcast), len(outs)
    splits = [s[3] for s in tiled] + [s[1] for s in bcast]

    def store(ref, val, acc):
        off = 0
        for piece in val if isinstance(val, (tuple, list)) else (val,):
            w = piece.shape[-1]
            if acc:
                ref[:, off:off + w] += piece.astype(ref.dtype)
            else:
                ref[:, off:off + w] = piece.astype(ref.dtype)
            off += w
        assert off == ref.shape[-1], (name, off, ref.shape)

    def body(*refs):
        args = []
        for ref, sp in zip(refs[:nin], splits):
            if sp is None:
                args.append(ref[...])
            else:
                off = 0
                for w in sp:
                    args.append(ref[:, off:off + w])
                    off += w
        res = fn(*args)
        res = res if isinstance(res, (tuple, list)) else (res,)
        assert len(res) == nout + len(reds), (name, len(res))
        for ref, val in zip(refs[nin:nin + nout], res[:nout]):
            store(ref, val, False)
        if reds:
            @pl.when(pl.program_id(0) == 0)
            def _():
                for ref in refs[nin + nout:]:
                    ref[...] = jnp.zeros(ref.shape, ref.dtype)
            for ref, val in zip(refs[nin + nout:], res[nout:]):
                store(ref, val, True)

    in_specs = [pl.BlockSpec((tt, w), functools.partial(lambda i, cb: (i, cb), cb=cb)) for (_, w, cb, _) in tiled]
    in_specs += [pl.BlockSpec(arr.shape, lambda i: (0, 0)) for (arr, _) in bcast]
    out_specs = [pl.BlockSpec((tt, w), lambda i: (i, 0)) for (w, _) in outs]
    out_specs += [pl.BlockSpec((r, w), lambda i: (0, 0)) for (r, w) in reds]
    out_shape = [jax.ShapeDtypeStruct((t, w), dt) for (w, dt) in outs] + [jax.ShapeDtypeStruct((r, w), F32) for (r, w) in reds]
    return pl.pallas_call(
        body, name=name, grid=(t // tt,), in_specs=in_specs, out_specs=out_specs, out_shape=out_shape,
        compiler_params=pltpu.CompilerParams(dimension_semantics=("arbitrary" if reds else "parallel",), vmem_limit_bytes=VMEM_LIMIT),
    )(*[s[0] for s in tiled], *[s[0] for s in bcast])


def _full(arr):
    return (arr, arr.shape[1], 0, None)


def _f(x):
    return x.astype(F32)


def _rms(x, g):
    return x * lax.rsqrt(jnp.mean(x * x, axis=-1, keepdims=True) + EPS) * g


def _log_sigmoid(x):
    return jnp.minimum(x, 0.0) - jnp.log1p(jnp.exp(-jnp.abs(x)))


def _silu(x):
    return x * jax.nn.sigmoid(x)


def _norm_fwd(x, g, name):
    return _rowwise(lambda xb, gb: _rms(_f(xb), gb), [_full(x)], [(g, None)], [(x.shape[1], BF16)], tt=256, name=name)[0]


def _resnorm_fwd(res, branch, g, coef, name):
    def fn(rb, bb, gb):
        h = rb + coef * _f(bb)
        return h, _rms(h, gb)
    d = res.shape[1]
    return _rowwise(fn, [_full(res), _full(branch)], [(g, None)], [(d, F32), (d, BF16)], tt=256, name=name)


def _norm_bwd(h, dns, dres, g, coef, name):
    nd = len(dns)

    def fn(hb, *rest):
        dn = _f(rest[0])
        for extra in rest[1:nd]:
            dn = dn + _f(extra)
        dr, gb = rest[nd], rest[nd + 1]
        _, vjp = jax.vjp(_rms, hb, gb)
        dh, dg = vjp(dn)
        dh = dh + dr
        return dh, coef * dh, dg
    d = h.shape[1]
    return _rowwise(fn, [_full(h)] + [_full(x) for x in dns] + [_full(dres)], [(g, None)],
                    [(d, F32), (d, BF16)], [(1, d)], tt=256, name=name)


def _act_fwd(gu, name):
    ff = gu.shape[1] // 2
    return _rowwise(lambda gb, ub: _silu(_f(gb)) * _f(ub), [(gu, 2 * ff, 0, (ff, ff))], [], [(ff, BF16)], tt=256, name=name)[0]


def _act_bwd(gu, da, name):
    ff = gu.shape[1] // 2

    def fn(gb, ub, dab):
        _, vjp = jax.vjp(lambda p, q: _silu(p) * q, _f(gb), _f(ub))
        return (vjp(_f(dab)),)
    return _rowwise(fn, [(gu, 2 * ff, 0, (ff, ff)), _full(da)], [], [(2 * ff, BF16)], tt=128, name=name)[0]


def _merge(glf, glg, bf, bg, bmf, bmg):
    return jax.nn.sigmoid(_f(glf) + bmf) * _f(bf) + jax.nn.sigmoid(_f(glg) + bmg) * _f(bg)


def _merge_fwd(z, bf, bg, bm, name):
    d = D_MODEL
    return _rowwise(_merge, [(z, d, Z_GL // d, None), (z, d, Z_GL // d + 1, None), _full(bf), _full(bg)], [(bm, (d, d))],
                    [(d, BF16)], tt=256, name=name)[0]


def _merge_bwd(z, bf, bg, bm, dm, name):
    d = D_MODEL

    def fn(glf, glg, bfb, bgb, dmb, bmf, bmg):
        _, vjp = jax.vjp(_merge, _f(glf), _f(glg), _f(bfb), _f(bgb), bmf, bmg)
        dglf, dglg, dbf, dbg, dbmf, dbmg = vjp(_f(dmb))
        return (dglf, dglg), dbf, dbg, (dbmf, dbmg)
    return _rowwise(fn, [(z, d, Z_GL // d, None), (z, d, Z_GL // d + 1, None), _full(bf), _full(bg), _full(dm)], [(bm, (d, d))],
                    [(2 * d, BF16), (d, BF16), (d, BF16)], [(1, 2 * d)], tt=128, name=name)


def _gla_out(o, gr, g):
    return _rms(o, g) * _silu(_f(gr))


_PER_HEAD = (GLA_VAL_DIM,) * GLA_HEADS


def _gla_out_fwd(o, z, g, name):
    nh = GLA_HEADS

    def fn(*blocks):
        return (tuple(_gla_out(blocks[h], blocks[nh + h], blocks[2 * nh]) for h in range(nh)),)
    return _rowwise(fn, [(o, GLA_VW, 0, _PER_HEAD), (z, GLA_VW, Z_GR // GLA_VW, _PER_HEAD)], [(g, None)], [(GLA_VW, BF16)], tt=256, name=name)[0]


def _gla_out_bwd(o, z, g, dy, name):
    nh = GLA_HEADS

    def fn(*blocks):
        gb = blocks[3 * nh]
        grads = []
        for h in range(nh):
            _, vjp = jax.vjp(_gla_out, blocks[h], _f(blocks[nh + h]), gb)
            grads.append(vjp(_f(blocks[2 * nh + h])))
        dg = grads[0][2]
        for h in range(1, nh):
            dg = dg + grads[h][2]
        return tuple(gr[0] for gr in grads), tuple(gr[1] for gr in grads), dg
    return _rowwise(fn, [(o, GLA_VW, 0, _PER_HEAD), (z, GLA_VW, Z_GR // GLA_VW, _PER_HEAD), (dy, GLA_VW, 0, _PER_HEAD)], [(g, None)],
                    [(GLA_VW, F32), (GLA_VW, BF16)], [(1, GLA_VAL_DIM)], tt=256, name=name)


def _small_gates(s, fb, gup, gb):
    lane = lax.broadcasted_iota(jnp.int32, s.shape, 1)
    lf = jnp.where(lane < FOX_HEADS, _log_sigmoid(s + fb), 0.0)
    pre = jnp.dot(s.astype(BF16), gup.astype(BF16), preferred_element_type=F32) + gb
    return lf, _log_sigmoid(pre) / GLA_TAU


def _small_fwd(s, fb, gup, gb, name):
    return _rowwise(_small_gates, [_full(s)], [(fb, None), (gup, None), (gb, None)], [(SMALL_W, F32), (GLA_KW, F32)], tt=256, name=name)


def _small_bwd(s, fb, gup, gb, dlf, dla, name):
    def fn(sb, dlfb, dlab, fbb, gupb, gbb):
        _, vjp = jax.vjp(_small_gates, sb, fbb, gupb, gbb)
        return vjp((dlfb, dlab))
    return _rowwise(fn, [_full(s), _full(dlf), _full(dla)], [(fb, None), (gup, None), (gb, None)],
                    [(SMALL_W, BF16)], [(1, SMALL_W), (SMALL_W, GLA_KW), (1, GLA_KW)], tt=256, name=name)


def _head_fn(h3, pgl, pp, tgt, gf):
    h4 = h3 + jax.nn.sigmoid(pgl) * pp
    err = _rms(h4, gf) - tgt
    return 0.5 * jnp.sum(jnp.mean(err * err, axis=-1, keepdims=True))


def _head(h3, pgl, pp, tgt, gf, name):
    def fn(hb, gl, pb, tb, gfb):
        loss, vjp = jax.vjp(_head_fn, hb, _f(gl), _f(pb), tb, gfb)
        dh, dgl, dpp, _, dgf = vjp(jnp.ones((), F32))
        return dh, dgl, dpp, jnp.full((1, 128), loss, F32), dgf
    d = h3.shape[1]
    return _rowwise(fn, [_full(h3), _full(pgl), _full(pp), _full(tgt)], [(gf, None)],
                    [(d, F32), (d, BF16), (d, BF16)], [(1, 128), (1, d)], tt=256, name=name)


def _cumsum_tokens(a, reverse, name):
    t, w = a.shape
    r = min(256, t)
    nb = t // r

    def body(a_ref, o_ref, carry_ref):
        @pl.when(pl.program_id(0) == 0)
        def _():
            carry_ref[...] = jnp.zeros(carry_ref.shape, F32)
        row = lax.broadcasted_iota(jnp.int32, (r, r), 0)
        col = lax.broadcasted_iota(jnp.int32, (r, r), 1)
        tri = ((col >= row) if reverse else (col <= row)).astype(F32)
        blk = a_ref[...]
        o_ref[...] = jnp.dot(tri, blk, preferred_element_type=F32, precision=lax.Precision.HIGHEST) + carry_ref[...]
        carry_ref[...] += jnp.sum(blk, axis=0, keepdims=True)

    idx = (lambda i: (nb - 1 - i, 0)) if reverse else (lambda i: (i, 0))
    return pl.pallas_call(
        body, name=name, grid=(nb,), in_specs=[pl.BlockSpec((r, w), idx)], out_specs=pl.BlockSpec((r, w), idx),
        out_shape=jax.ShapeDtypeStruct((t, w), F32), scratch_shapes=[pltpu.VMEM((1, w), F32)],
        compiler_params=pltpu.CompilerParams(dimension_semantics=("arbitrary",)),
    )(a)


FOX_TQ, FOX_TK = 256, 512
FOX_SCALE = HEAD_DIM ** -0.5


def _fox_tiles(t):
    tq, tk = min(FOX_TQ, t), min(FOX_TK, t)
    return tq, tk, t // tq, t // tk


def _blocked_t(a, blk):
    return a.reshape(a.shape[0] // blk, blk, a.shape[1]).transpose(0, 2, 1)


def _unblocked_t(b):
    return b.transpose(0, 2, 1).reshape(b.shape[0] * b.shape[2], b.shape[1])


def _fox_scores(k, qt, frep, i, j, masked):
    tk, tq = k.shape[0], qt.shape[1]
    st = jnp.dot(k, qt, preferred_element_type=F32) * FOX_SCALE - jnp.tile(frep, (1, tq // HEAD_DIM))
    if masked:
        key = j * tk + lax.broadcasted_iota(jnp.int32, (tk, tq), 0)
        query = i * tq + lax.broadcasted_iota(jnp.int32, (tk, tq), 1)
        st = jnp.where(key <= query, st, NEG)
    return st


def _fox_fwd(z, qt, vt, frep, name, comm=None):
    t = z.shape[0]
    tq, tk, nq, nk = _fox_tiles(t)
    kb = Z_FK // HEAD_DIM

    def body(qt_ref, k_ref, vt_ref, frep_ref, ot_ref, lse_ref):
        i = pl.program_id(1)
        qt = qt_ref[...]
        last = ((i + 1) * tq - 1) // tk

        def block(j, carry, masked):
            m, l, acc = carry
            rows = pl.ds(pl.multiple_of(j * tk, tk), tk)
            st = _fox_scores(k_ref[rows, :], qt, frep_ref[rows, :], i, j, masked)
            m_new = jnp.maximum(m, jnp.max(st, axis=0, keepdims=True))
            alpha = jnp.exp(m - m_new)
            p = jnp.exp(st - m_new)
            l = alpha * l + jnp.sum(p, axis=0, keepdims=True)
            acc = alpha * acc + jnp.dot(vt_ref[j], p.astype(BF16), preferred_element_type=F32)
            return m_new, l, acc

        init = (jnp.full((1, tq), NEG, F32), jnp.zeros((1, tq), F32), jnp.zeros((HEAD_DIM, tq), F32))
        m, l, acc = block(last, lax.fori_loop(0, last, lambda j, c: block(j, c, False), init), True)
        ot_ref[...] = (acc / l).astype(ot_ref.dtype)
        lse_ref[...] = m + jnp.log(l)

    stat = pl.BlockSpec((None, None, 1, tq), lambda h, i: (h, i, 0, 0))
    (ot, lse), travelled = _hosted(
        body, comm, name=name, grid=(FOX_HEADS, nq),
        in_specs=[pl.BlockSpec((None, HEAD_DIM, tq), lambda h, i: (i, h, 0)),
                  pl.BlockSpec((t, HEAD_DIM), lambda h, i: (0, kb + h)),
                  pl.BlockSpec((nk, HEAD_DIM, tk), lambda h, i: (0, h, 0)),
                  pl.BlockSpec((None, t, HEAD_DIM), lambda h, i: (h, 0, 0))],
        out_specs=[pl.BlockSpec((None, HEAD_DIM, tq), lambda h, i: (i, h, 0)), stat],
        out_shape=[jax.ShapeDtypeStruct((nq, FOX_W, tq), BF16), jax.ShapeDtypeStruct((FOX_HEADS, nq, 1, tq), F32)],
        scratch_shapes=[], semantics=("parallel", "parallel"), args=(qt, z, vt, frep))
    return ot, lse, travelled


def _fox_bwd_q(z, qt, kt, ot, dot, lse, frep, name):
    t = z.shape[0]
    tq, tk, nq, nk = _fox_tiles(t)
    kb, vb = Z_FK // HEAD_DIM, Z_FV // HEAD_DIM

    def body(qt_ref, k_ref, kt_ref, v_ref, ot_ref, dot_ref, lse_ref, frep_ref, dqt_ref, delta_ref, dfq_ref):
        i = pl.program_id(1)
        qt, dot = qt_ref[...], dot_ref[...]
        lse = lse_ref[...]
        delta = jnp.sum(_f(dot) * _f(ot_ref[...]), axis=0, keepdims=True)
        delta_ref[...] = delta
        last = ((i + 1) * tq - 1) // tk

        def block(j, carry, masked):
            dq, dfq = carry
            rows = pl.ds(pl.multiple_of(j * tk, tk), tk)
            p = jnp.exp(_fox_scores(k_ref[rows, :], qt, frep_ref[rows, :], i, j, masked) - lse)
            dp = jnp.dot(v_ref[rows, :], dot, preferred_element_type=F32)
            ds = p * (dp - delta)
            return dq + jnp.dot(kt_ref[j], ds.astype(BF16), preferred_element_type=F32), dfq + jnp.sum(ds, axis=0, keepdims=True)

        init = (jnp.zeros((HEAD_DIM, tq), F32), jnp.zeros((1, tq), F32))
        dq, dfq = block(last, lax.fori_loop(0, last, lambda j, c: block(j, c, False), init), True)
        dqt_ref[...] = (dq * FOX_SCALE).astype(dqt_ref.dtype)
        dfq_ref[...] = dfq

    mine = pl.BlockSpec((None, HEAD_DIM, tq), lambda h, i: (i, h, 0))
    stat = pl.BlockSpec((None, None, 1, tq), lambda h, i: (h, i, 0, 0))
    return pl.pallas_call(
        body, name=name, grid=(FOX_HEADS, nq),
        in_specs=[mine,
                  pl.BlockSpec((t, HEAD_DIM), lambda h, i: (0, kb + h)),
                  pl.BlockSpec((nk, HEAD_DIM, tk), lambda h, i: (0, h, 0)),
                  pl.BlockSpec((t, HEAD_DIM), lambda h, i: (0, vb + h)),
                  mine, mine, stat,
                  pl.BlockSpec((None, t, HEAD_DIM), lambda h, i: (h, 0, 0))],
        out_specs=[mine, stat, stat],
        out_shape=[jax.ShapeDtypeStruct((nq, FOX_W, tq), BF16), jax.ShapeDtypeStruct((FOX_HEADS, nq, 1, tq), F32),
                   jax.ShapeDtypeStruct((FOX_HEADS, nq, 1, tq), F32)],
        compiler_params=pltpu.CompilerParams(dimension_semantics=("parallel", "parallel"), vmem_limit_bytes=VMEM_LIMIT),
    )(qt, z, kt, z, ot, dot, lse, frep)


def _fox_bwd_kv(z, qt, do, dot, lse, delta, frep, name):
    t = z.shape[0]
    tq, tk, nq, nk = _fox_tiles(t)
    qb, kb, vb = Z_FQ // HEAD_DIM, Z_FK // HEAD_DIM, Z_FV // HEAD_DIM
    per = tk // tq

    def body(k_ref, v_ref, frep_ref, q_ref, qt_ref, do_ref, dot_ref, lse_ref, delta_ref, dk_ref, dv_ref, dfk_ref):
        j = pl.program_id(1)
        k, v, frep = k_ref[...], v_ref[...], frep_ref[...]

        def block(i, carry, masked):
            dk, dv, dfk = carry
            rows = pl.ds(pl.multiple_of(i * tq, tq), tq)
            p = jnp.exp(_fox_scores(k, qt_ref[i], frep, i, j, masked) - lse_ref[i])
            dv = dv + jnp.dot(p.astype(BF16), do_ref[rows, :], preferred_element_type=F32)
            dp = jnp.dot(v, dot_ref[i], preferred_element_type=F32)
            ds = p * (dp - delta_ref[i])
            dk = dk + jnp.dot(ds.astype(BF16), q_ref[rows, :], preferred_element_type=F32)
            for part in range(tq // HEAD_DIM):
                dfk = dfk + ds[:, part * HEAD_DIM:(part + 1) * HEAD_DIM]
            return dk, dv, dfk

        zero = jnp.zeros((tk, HEAD_DIM), F32)
        carry = (zero, zero, zero)
        for step in range(per):
            carry = block(j * per + step, carry, True)
        dk, dv, dfk = lax.fori_loop((j + 1) * per, nq, lambda i, c: block(i, c, False), carry)
        dk_ref[...] = (dk * FOX_SCALE).astype(dk_ref.dtype)
        dv_ref[...] = dv.astype(dv_ref.dtype)
        dfk_ref[...] = jnp.sum(dfk, axis=1, keepdims=True)

    whole_t = pl.BlockSpec((nq, HEAD_DIM, tq), lambda h, j: (0, h, 0))
    whole_stat = pl.BlockSpec((None, nq, 1, tq), lambda h, j: (h, 0, 0, 0))
    return pl.pallas_call(
        body, name=name, grid=(FOX_HEADS, nk),
        in_specs=[pl.BlockSpec((tk, HEAD_DIM), lambda h, j: (j, kb + h)),
                  pl.BlockSpec((tk, HEAD_DIM), lambda h, j: (j, vb + h)),
                  pl.BlockSpec((None, tk, HEAD_DIM), lambda h, j: (h, j, 0)),
                  pl.BlockSpec((t, HEAD_DIM), lambda h, j: (0, qb + h)),
                  whole_t,
                  pl.BlockSpec((t, HEAD_DIM), lambda h, j: (0, h)),
                  whole_t, whole_stat, whole_stat],
        out_specs=[pl.BlockSpec((tk, HEAD_DIM), lambda h, j: (j, h)), pl.BlockSpec((tk, HEAD_DIM), lambda h, j: (j, h)),
                   pl.BlockSpec((None, tk, 1), lambda h, j: (h, j, 0))],
        out_shape=[jax.ShapeDtypeStruct((t, FOX_W), BF16), jax.ShapeDtypeStruct((t, FOX_W), BF16),
                   jax.ShapeDtypeStruct((FOX_HEADS, t, 1), F32)],
        compiler_params=pltpu.CompilerParams(dimension_semantics=("parallel", "parallel"), vmem_limit_bytes=VMEM_LIMIT),
    )(z, z, frep, z, qt, do, dot, lse, delta)


def _gla_step(st, q, k, v, la):
    row = lax.broadcasted_iota(jnp.int32, (CHUNK, CHUNK), 0)
    col = lax.broadcasted_iota(jnp.int32, (CHUNK, CHUNK), 1)
    tri = (col <= row).astype(F32)
    a_cum = jnp.dot(tri, la, preferred_element_type=F32, precision=lax.Precision.HIGHEST)
    a_tot = jnp.sum(la, axis=0, keepdims=True)
    k_dec = (_f(k) * jnp.exp(a_tot - a_cum)).astype(BF16)
    qs = (_f(q) * (HEAD_DIM ** -0.5)).astype(BF16)
    st = st * jnp.exp(a_tot) + lax.dot_general(v.astype(BF16), k_dec, (((0,), (0,)), ((), ())), preferred_element_type=F32)
    o = lax.dot_general(qs, st.astype(BF16), (((1,), (1,)), ((), ())), preferred_element_type=F32)
    return st, o


def _gla_blocks(t):
    r = min(256, t)
    return r, t // r, r // CHUNK


def _gla_fwd(z, la, name):
    t = z.shape[0]
    r, nb, nch = _gla_blocks(t)

    def body(q_ref, k_ref, v_ref, la_ref, o_ref, sp_ref, st_ref):
        @pl.when(pl.program_id(0) == 0)
        def _():
            st_ref[...] = jnp.zeros(st_ref.shape, F32)
        for c in range(nch):
            rows = slice(c * CHUNK, (c + 1) * CHUNK)
            for h in range(GLA_HEADS):
                kc = slice(h * HEAD_DIM, (h + 1) * HEAD_DIM)
                vc = slice(h * GLA_VAL_DIM, (h + 1) * GLA_VAL_DIM)
                st = st_ref[h]
                sp_ref[c, h] = st
                st, o = _gla_step(st, q_ref[rows, kc], k_ref[rows, kc], v_ref[rows, vc], la_ref[rows, kc])
                st_ref[h] = st
                o_ref[rows, vc] = o

    return pl.pallas_call(
        body, name=name, grid=(nb,),
        in_specs=[pl.BlockSpec((r, GLA_KW), lambda i: (i, Z_GQ // GLA_KW)), pl.BlockSpec((r, GLA_KW), lambda i: (i, Z_GK // GLA_KW)),
                  pl.BlockSpec((r, GLA_VW), lambda i: (i, Z_GV // GLA_VW)), pl.BlockSpec((r, GLA_KW), lambda i: (i, 0))],
        out_specs=[pl.BlockSpec((r, GLA_VW), lambda i: (i, 0)),
                   pl.BlockSpec((nch, GLA_HEADS, GLA_VAL_DIM, HEAD_DIM), lambda i: (i, 0, 0, 0))],
        out_shape=[jax.ShapeDtypeStruct((t, GLA_VW), F32),
                   jax.ShapeDtypeStruct((t // CHUNK, GLA_HEADS, GLA_VAL_DIM, HEAD_DIM), F32)],
        scratch_shapes=[pltpu.VMEM((GLA_HEADS, GLA_VAL_DIM, HEAD_DIM), F32)],
        compiler_params=pltpu.CompilerParams(dimension_semantics=("arbitrary",), vmem_limit_bytes=VMEM_LIMIT),
    )(z, z, z, la)


def _gla_bwd(z, la, sprev, do, name):
    t = z.shape[0]
    r, nb, nch = _gla_blocks(t)

    def body(q_ref, k_ref, v_ref, la_ref, sp_ref, do_ref, dq_ref, dk_ref, dv_ref, dla_ref, dst_ref):
        @pl.when(pl.program_id(0) == 0)
        def _():
            dst_ref[...] = jnp.zeros(dst_ref.shape, F32)
        for c in reversed(range(nch)):
            rows = slice(c * CHUNK, (c + 1) * CHUNK)
            for h in range(GLA_HEADS):
                kc = slice(h * HEAD_DIM, (h + 1) * HEAD_DIM)
                vc = slice(h * GLA_VAL_DIM, (h + 1) * GLA_VAL_DIM)
                _, vjp = jax.vjp(_gla_step, sp_ref[c, h], q_ref[rows, kc], k_ref[rows, kc], v_ref[rows, vc], la_ref[rows, kc])
                dst, dq, dk, dv, dla = vjp((dst_ref[h], do_ref[rows, vc]))
                dst_ref[h] = dst
                dq_ref[rows, kc] = dq
                dk_ref[rows, kc] = dk
                dv_ref[rows, vc] = dv
                dla_ref[rows, kc] = dla

    rev = lambda i: (nb - 1 - i, 0)
    return pl.pallas_call(
        body, name=name, grid=(nb,),
        in_specs=[pl.BlockSpec((r, GLA_KW), lambda i: (nb - 1 - i, Z_GQ // GLA_KW)), pl.BlockSpec((r, GLA_KW), lambda i: (nb - 1 - i, Z_GK // GLA_KW)),
                  pl.BlockSpec((r, GLA_VW), lambda i: (nb - 1 - i, Z_GV // GLA_VW)), pl.BlockSpec((r, GLA_KW), rev),
                  pl.BlockSpec((nch, GLA_HEADS, GLA_VAL_DIM, HEAD_DIM), lambda i: (nb - 1 - i, 0, 0, 0)),
                  pl.BlockSpec((r, GLA_VW), rev)],
        out_specs=[pl.BlockSpec((r, GLA_KW), rev), pl.BlockSpec((r, GLA_KW), rev), pl.BlockSpec((r, GLA_VW), rev), pl.BlockSpec((r, GLA_KW), rev)],
        out_shape=[jax.ShapeDtypeStruct((t, GLA_KW), BF16), jax.ShapeDtypeStruct((t, GLA_KW), BF16),
                   jax.ShapeDtypeStruct((t, GLA_VW), BF16), jax.ShapeDtypeStruct((t, GLA_KW), F32)],
        scratch_shapes=[pltpu.VMEM((GLA_HEADS, GLA_VAL_DIM, HEAD_DIM), F32)],
        compiler_params=pltpu.CompilerParams(dimension_semantics=("arbitrary",), vmem_limit_bytes=VMEM_LIMIT),
    )(z, z, z, la, sprev, do)


def _local_step(x, p, tgt, shards, sp, cidx, chip):
    t = x.shape[0]
    tq, tk, _, _ = _fox_tiles(t)
    full, reduced = {}, {}

    def plan(names):
        keys, shapes, places = [], [], []
        for n in names:
            r, cc = shards[n].shape
            key, part, parts = FUSED.get(n, (n, 0, 1))
            if key not in keys:
                keys.append(key)
                stacked = n in ROW_SHARDED or n == "w_in"
                shapes.append(jax.ShapeDtypeStruct((4 * r, cc) if stacked else (r, 4 * cc * parts), shards[n].dtype))
            places.append((keys.index(key), r, 0, 0) if n in ROW_SHARDED or n == "w_in" else (keys.index(key), 0, part * 4 * cc, cc))
        return keys, shapes, places

    def gather(names):
        _, shapes, places = plan(names)
        return _ag_comm([shards[n] for n in names], shapes, places)

    def landed(names, got):
        keys, _, places = plan(names)
        got = list(got)
        for n, (out, row_step, col_base, col_step) in zip(names, places):
            got[out] = lax.dynamic_update_slice(got[out], shards[n], (chip * row_step, col_base + chip * col_step))
        for key, g in zip(keys, got):
            full[key] = g
        if "w_in" in keys:
            r = shards["w_in"].shape[0]
            full["w_in"] = jnp.transpose(full["w_in"].reshape(4, r, -1), (1, 0, 2)).reshape(r, -1)

    def pair_sums(grads):
        parts = [_shard_parts(g, n) for n, g in grads.items()]
        swapped = _pair_swap(parts, "pair_swap")
        return list(grads), [_sum_half(a, b, cidx, "sum_half") for a, b in zip(parts, swapped)]

    def exchanged(names, sums, got):
        for n, own, g in zip(names, sums, got):
            reduced[n] = (g, own)

    first = ["ffn1_w_gate", "ffn1_w_up"]
    landed(first, _run_comm(gather(first), "all_gather"))
    w_gu1 = full["gu1"]
    n1 = _norm_fwd(x, sp["ffn1_norm"], "norm1_fwd")
    names = ["ffn1_w_down", "w_in", "gla_gate_up"]
    gu1, got = _mm(n1, w_gu1, name="mm_gu_gather", comm=gather(names))
    landed(names, got)
    a1 = _act_fwd(gu1, "act_fwd")
    names = ["w_merge_gate", "w_branch_fox", "w_branch_gla"]
    f1, got = _mm(a1, full["ffn1_w_down"], out_dtype=F32, name="mm_down_gather", comm=gather(names))
    landed(names, got)
    w_big, w_sm = _in_layout(full["w_in"], full["w_merge_gate"])
    gup = jnp.zeros((SMALL_W, GLA_KW), F32).at[FOX_HEADS:FOX_HEADS + GLA_RANK].set(full["gla_gate_up"])
    h1, u = _resnorm_fwd(x, f1, sp["mix_norm"], 0.5, "resnorm_fwd_half")
    names = ["w_out", "ffn2_w_down", "w_ple_proj", "w_ple_gate"]
    z, got = _mm(u, w_big, name="mm_in_gather", comm=gather(names))
    landed(names, got)
    s = _mm(u, w_sm, out_dtype=F32, name="mm_in_small")
    lf, la = _small_fwd(s, sp["fb"], gup, sp["gb"], "small_fwd")
    fp = _cumsum_tokens(lf, False, "cumsum_fwd")
    frep = jnp.broadcast_to(fp[:, :FOX_HEADS].T[:, :, None], (FOX_HEADS, t, HEAD_DIM))
    qt = _blocked_t(z[:, Z_FQ:Z_FQ + FOX_W], tq)
    kt = _blocked_t(z[:, Z_FK:Z_FK + FOX_W], tk)
    vt = _blocked_t(z[:, Z_FV:Z_FV + FOX_W], tk)
    names = ["ffn2_w_gate", "ffn2_w_up"]
    ot, lse, got = _fox_fwd(z, qt, vt, frep, "fox_fwd_gather", comm=gather(names))
    landed(names, got)
    w_gu2 = full["gu2"]
    y_fox = _unblocked_t(ot)
    o_gla, sprev = _gla_fwd(z, la, "gla_fwd")
    y_gla = _gla_out_fwd(o_gla, z, sp["ghn"], "gla_out_fwd")
    bf = _mm(y_fox, full["w_branch_fox"], name="mm_branch")
    bg = _mm(y_gla, full["w_branch_gla"], name="mm_branch")
    merged = _merge_fwd(z, bf, bg, sp["bm"], "merge_fwd")
    mo = _mm(merged, full["w_out"], out_dtype=F32, name="mm_out")
    h2, n2 = _resnorm_fwd(h1, mo, sp["ffn2_norm"], 1.0, "resnorm_fwd_one")
    gu2 = _mm(n2, w_gu2, name="mm_gu")
    a2 = _act_fwd(gu2, "act_fwd")
    f2 = _mm(a2, full["ffn2_w_down"], out_dtype=F32, name="mm_down")
    h3, n4 = _resnorm_fwd(h2, f2, sp["ple_norm"], 0.5, "resnorm_fwd_half")
    pgl = _mm(n4, full["w_ple_gate"], name="mm_pg")
    pb = p.astype(BF16)
    pp = _mm(pb, full["w_ple_proj"], name="mm_pp")

    dh3, dpgl, dpp, loss, d_final = _head(h3, pgl, pp, tgt, sp["final_norm"], "head")
    ds_ = {"final_norm": d_final}
    names, sums = pair_sums({"w_ple_gate": _mm(n4, dpgl, ta=True, name="mm_dw_sq"), "w_ple_proj": _mm(pb, dpp, ta=True, name="mm_dw_pp")})
    dn4, got = _mm(dpgl, full["w_ple_gate"], tb=True, out_dtype=F32, name="mm_dx_sq_f32_exchange", comm=_exchange_comm(sums))
    exchanged(names, sums, got)
    dh3, df2, ds_["ple_norm"] = _norm_bwd(h3, [dn4], dh3, sp["ple_norm"], 0.5, "norm_bwd_1")

    def ffn_bwd(n, gu, a, df, wgu, wd, which):
        ff = wd.shape[0]
        names, sums = pair_sums({which + "_w_down": _mm(a, df, ta=True, name="mm_dw_down")})
        da, got = _mm(df, wd, tb=True, name="mm_dx_down_exchange", comm=_exchange_comm(sums))
        exchanged(names, sums, got)
        dgu = _act_bwd(gu, da, "act_bwd")
        dwgu = _mm(n, dgu, ta=True, name="mm_dw_gu")
        names, sums = pair_sums({which + "_w_gate": dwgu[:, :ff], which + "_w_up": dwgu[:, ff:]})
        dn, got = _mm(dgu, wgu, tb=True, out_dtype=F32, name="mm_dx_gu_exchange", comm=_exchange_comm(sums))
        exchanged(names, sums, got)
        return dn

    dn2 = ffn_bwd(n2, gu2, a2, df2, w_gu2, full["ffn2_w_down"], "ffn2")
    dh2, dmix, ds_["ffn2_norm"] = _norm_bwd(h2, [dn2], dh3, sp["ffn2_norm"], 1.0, "norm_bwd_1")

    dw_out = _mm(merged, dmix, ta=True, name="mm_dw_sq")
    dmerged = _mm(dmix, full["w_out"], tb=True, name="mm_dx_sq")
    dgl, dbf, dbg, ds_["bm"] = _merge_bwd(z, bf, bg, sp["bm"], dmerged, "merge_bwd")
    mix_names, mix_sums = pair_sums({"w_out": dw_out, "w_branch_fox": _mm(y_fox, dbf, ta=True, name="mm_dw_branch"),
                                     "w_branch_gla": _mm(y_gla, dbg, ta=True, name="mm_dw_branch")})
    dy_fox = _mm(dbf, full["w_branch_fox"], tb=True, name="mm_dx_branch")
    dy_gla = _mm(dbg, full["w_branch_gla"], tb=True, name="mm_dx_branch")

    do_gla, dgr, ds_["ghn"] = _gla_out_bwd(o_gla, z, sp["ghn"], dy_gla, "gla_out_bwd")
    dgq, dgk, dgv, dla = _gla_bwd(z, la, sprev, do_gla, "gla_bwd")
    dot = _blocked_t(dy_fox, tq)
    dqt, delta, df_query = _fox_bwd_q(z, qt, kt, ot, dot, lse, frep, "fox_bwd_q")
    dfq = _unblocked_t(dqt)
    dfk, dfv, df_key = _fox_bwd_kv(z, qt, dy_fox, dot, lse, delta, frep, "fox_bwd_kv")
    df = df_query.reshape(FOX_HEADS, t) - df_key.reshape(FOX_HEADS, t)
    dfp = jnp.pad(df.T, ((0, 0), (0, SMALL_W - FOX_HEADS)))
    dlf = _cumsum_tokens(dfp, True, "cumsum_bwd")
    dsm, ds_["fb"], dgup, ds_["gb"] = _small_bwd(s, sp["fb"], gup, sp["gb"], dlf, dla, "small_bwd")
    dz = jnp.concatenate([dfq, dfk, dfv, dgq, dgk, dgv, dgr, dgl], axis=1)
    dw_big, got = _mm(u, dz, ta=True, name="mm_dw_in_exchange", comm=_exchange_comm(mix_sums))
    exchanged(mix_names, mix_sums, got)
    dw_sm = _mm(u, dsm, ta=True, out_dtype=F32, name="mm_dw_in_small").astype(BF16)
    dw_in = jnp.concatenate([dw_big[:, Z_FQ:Z_GQ], dw_sm[:, :FOX_HEADS], dw_big[:, Z_GQ:Z_GL], dw_sm[:, FOX_HEADS:FOX_HEADS + GLA_RANK]], axis=1)
    names, sums = pair_sums({"w_in": dw_in, "w_merge_gate": dw_big[:, Z_GL:], "gla_gate_up": dgup[FOX_HEADS:FOX_HEADS + GLA_RANK]})
    du1, got = _mm(dz, w_big, tb=True, out_dtype=F32, name="mm_dx_in_exchange", comm=_exchange_comm(sums))
    exchanged(names, sums, got)
    du2 = _mm(dsm, w_sm, tb=True, out_dtype=F32, name="mm_dx_in_small")
    dh1, df1, ds_["mix_norm"] = _norm_bwd(h1, [du1, du2], dh2, sp["mix_norm"], 0.5, "norm_bwd_2")

    dn1 = ffn_bwd(n1, gu1, a1, df1, w_gu1, full["ffn1_w_down"], "ffn1")
    grad_x, _, ds_["ffn1_norm"] = _norm_bwd(x, [dn1], dh1, sp["ffn1_norm"], 1.0, "norm_bwd_1")
    return loss, grad_x, reduced, ds_


def _half_rows(ref, which):
    r2 = ref.shape[0] // 2
    return ref.at[pl.ds(pl.multiple_of(which * r2, r2), r2)]


class _Comm:
    def __init__(self, ins, out_shape, sems, start, finish):
        self.ins, self.out_shape, self.sems, self.start, self.finish = ins, out_shape, sems, start, finish


def _run_comm(comm, name):
    n_in, n_out = len(comm.ins), len(comm.out_shape)

    def body(*refs):
        parts = refs[:n_in], refs[n_in:n_in + n_out], refs[n_in + n_out:]
        comm.start(*parts)
        comm.finish(*parts)

    return pl.pallas_call(
        body, name=name, in_specs=[ANY] * n_in, out_specs=[ANY] * n_out, out_shape=comm.out_shape,
        scratch_shapes=comm.sems, compiler_params=pltpu.CompilerParams(has_side_effects=True),
    )(*comm.ins)


def _hosted(body, comm, *, name, grid, in_specs, out_specs, out_shape, scratch_shapes, semantics, args):
    if comm is None:
        res = pl.pallas_call(
            body, name=name, grid=grid, in_specs=in_specs, out_specs=out_specs, out_shape=out_shape, scratch_shapes=scratch_shapes,
            compiler_params=pltpu.CompilerParams(dimension_semantics=semantics, vmem_limit_bytes=VMEM_LIMIT),
        )(*args)
        return res, None
    ni, no, ns = len(in_specs), len(out_shape), len(scratch_shapes)
    ci, co = len(comm.ins), len(comm.out_shape)

    def wrapped(*refs):
        h_in, c_in = refs[:ni], refs[ni:ni + ci]
        h_out, c_out = refs[ni + ci:ni + ci + no], refs[ni + ci + no:ni + ci + no + co]
        h_scr, c_sem = refs[ni + ci + no + co:ni + ci + no + co + ns], refs[ni + ci + no + co + ns:]
        ids = [pl.program_id(axis) for axis in range(len(grid))]
        first = functools.reduce(jnp.logical_and, [i == 0 for i in ids])
        last = functools.reduce(jnp.logical_and, [i == g - 1 for i, g in zip(ids, grid)])

        @pl.when(first)
        def _():
            comm.start(c_in, c_out, c_sem)

        body(*h_in, *h_out, *h_scr)

        @pl.when(last)
        def _():
            comm.finish(c_in, c_out, c_sem)

    res = pl.pallas_call(
        wrapped, name=name, grid=grid, in_specs=list(in_specs) + [ANY] * ci, out_specs=list(out_specs) + [ANY] * co,
        out_shape=list(out_shape) + list(comm.out_shape), scratch_shapes=list(scratch_shapes) + list(comm.sems),
        compiler_params=pltpu.CompilerParams(dimension_semantics=("arbitrary",) * len(grid), vmem_limit_bytes=VMEM_LIMIT, has_side_effects=True),
    )(*args, *comm.ins)
    return res[:no], res[no:]


def _ag_comm(shards, out_shape, places):
    n = len(shards)

    def copies(ins, outs, sems):
        ici_send, ici_recv, d2d_send, d2d_recv = sems
        x, y, c = lax.axis_index("x"), lax.axis_index("y"), lax.axis_index("c")
        chips = [(1 - x, y), (x, 1 - y), (1 - x, 1 - y)]
        slot = lambda chip: 2 * chip[0] + chip[1]

        def window(wi, origin, half):
            out, row_step, col_base, col_step = places[wi]
            r, cc = shards[wi].shape
            rows = pl.ds(pl.multiple_of(slot(origin) * row_step + half * (r // 2), r // 2), r // 2)
            cols = pl.ds(pl.multiple_of(col_base + slot(origin) * col_step, HEAD_DIM), cc) if col_step else pl.ds(col_base, cc)
            return outs[out].at[rows, cols]

        def over_ici(wi, j, origin):
            return pltpu.make_async_remote_copy(
                src_ref=_half_rows(ins[wi], c), dst_ref=window(wi, origin, c),
                send_sem=ici_send.at[3 * wi + j], recv_sem=ici_recv.at[3 * wi + j],
                device_id=(chips[j][0], chips[j][1], c), device_id_type=MESH)

        def over_d2d(wi, j, half):
            place = window(wi, chips[j], half)
            return pltpu.make_async_remote_copy(
                src_ref=place, dst_ref=place, send_sem=d2d_send.at[3 * wi + j], recv_sem=d2d_recv.at[3 * wi + j],
                device_id=(x, y, 1 - c), device_id_type=MESH)

        return over_ici, over_d2d, (x, y), chips, c

    def start(ins, outs, sems):
        over_ici, _, me, _, _ = copies(ins, outs, sems)
        for wi in range(n):
            for j in range(3):
                over_ici(wi, j, me).start()

    def finish(ins, outs, sems):
        over_ici, over_d2d, me, chips, c = copies(ins, outs, sems)
        for wi in range(n):
            for j in range(3):
                over_ici(wi, j, chips[j]).wait_recv()
                over_d2d(wi, j, c).start()
        for wi in range(n):
            for j in range(3):
                over_d2d(wi, j, 1 - c).wait_recv()
        for wi in range(n):
            for j in range(3):
                over_ici(wi, j, me).wait_send()
                over_d2d(wi, j, c).wait_send()

    return _Comm(list(shards), list(out_shape), [pltpu.SemaphoreType.DMA((3 * n,))] * 4, start, finish)


def _pair_swap(parts, name):
    n = len(parts)

    def body(*refs):
        ins, outs = refs[:n], refs[n:2 * n]
        send_sems, recv_sems = refs[2 * n:]
        x, y, c = lax.axis_index("x"), lax.axis_index("y"), lax.axis_index("c")

        def swap(wi):
            r2 = parts[wi].shape[1] // 2
            return pltpu.make_async_remote_copy(
                src_ref=ins[wi].at[:, pl.ds(pl.multiple_of((1 - c) * r2, r2), r2)], dst_ref=outs[wi],
                send_sem=send_sems.at[wi], recv_sem=recv_sems.at[wi], device_id=(x, y, 1 - c), device_id_type=MESH)

        copies = [swap(wi) for wi in range(n)]
        for cp in copies:
            cp.start()
        for cp in copies:
            cp.wait()

    return pl.pallas_call(
        body, name=name, in_specs=[ANY] * n, out_specs=[ANY] * n,
        out_shape=[jax.ShapeDtypeStruct((s.shape[0], s.shape[1] // 2, s.shape[2]), s.dtype) for s in parts],
        scratch_shapes=[pltpu.SemaphoreType.DMA((n,))] * 2, compiler_params=pltpu.CompilerParams(has_side_effects=True),
    )(*parts)


def _row_tile(r, c, budget=1 << 19):
    return r if r <= 8 else _pick(r, max(8, budget // c), 8)


def _sum_half(parts, other, cidx, name):
    nl, r, cc = parts.shape
    r2 = r // 2
    tr = _row_tile(r2, cc)

    def body(c_ref, p_ref, q_ref, o_ref):
        o_ref[...] = (_f(p_ref[...]) + _f(q_ref[...])).astype(o_ref.dtype)

    return pl.pallas_call(
        body, name=name, out_shape=jax.ShapeDtypeStruct((nl, r2, cc), parts.dtype),
        grid_spec=pltpu.PrefetchScalarGridSpec(
            num_scalar_prefetch=1, grid=(nl, r2 // tr),
            in_specs=[pl.BlockSpec((None, None, tr, cc), lambda l, i, c_ref: (l, c_ref[0], i, 0)),
                      pl.BlockSpec((None, tr, cc), lambda l, i, c_ref: (l, i, 0))],
            out_specs=pl.BlockSpec((None, tr, cc), lambda l, i, c_ref: (l, i, 0))),
        compiler_params=pltpu.CompilerParams(dimension_semantics=("parallel", "parallel"), vmem_limit_bytes=VMEM_LIMIT),
    )(cidx, parts.reshape(nl, 2, r2, cc), other)


def _exchange_comm(sums):
    n = len(sums)

    def copies(ins, outs, sems):
        send_sems, recv_sems = sems
        x, y, c = lax.axis_index("x"), lax.axis_index("y"), lax.axis_index("c")
        chips = [(1 - x, y), (x, 1 - y), (1 - x, 1 - y)]
        slot = lambda chip: 2 * chip[0] + chip[1]

        def remote(wi, j, origin):
            return pltpu.make_async_remote_copy(
                src_ref=ins[wi].at[slot(chips[j]) if sums[wi].shape[0] == 4 else 0], dst_ref=outs[wi].at[slot(origin)],
                send_sem=send_sems.at[3 * wi + j], recv_sem=recv_sems.at[3 * wi + j],
                device_id=(chips[j][0], chips[j][1], c), device_id_type=MESH)

        return remote, (x, y), chips

    def start(ins, outs, sems):
        remote, me, _ = copies(ins, outs, sems)
        for wi in range(n):
            for j in range(3):
                remote(wi, j, me).start()

    def finish(ins, outs, sems):
        remote, me, chips = copies(ins, outs, sems)
        for wi in range(n):
            for j in range(3):
                remote(wi, j, chips[j]).wait_recv()
        for wi in range(n):
            for j in range(3):
                remote(wi, j, me).wait_send()

    return _Comm(list(sums), [jax.ShapeDtypeStruct((4,) + s.shape[1:], s.dtype) for s in sums],
                 [pltpu.SemaphoreType.DMA((3 * n,))] * 2, start, finish)


def _sum_chips(got, own, chip, name):
    _, r2, cc = got.shape
    tr = _row_tile(r2, cc)
    per_chip = own.shape[0] == 4

    def body(chip_ref, g_ref, own_ref, o_ref):
        term = lambda k: jnp.where(chip_ref[0] == k, _f(own_ref[...]), _f(g_ref[k]))
        o_ref[...] = ((term(0) + term(1)) + term(2)) + term(3)

    return pl.pallas_call(
        body, name=name, out_shape=jax.ShapeDtypeStruct((r2, cc), F32),
        grid_spec=pltpu.PrefetchScalarGridSpec(
            num_scalar_prefetch=1, grid=(r2 // tr,),
            in_specs=[pl.BlockSpec((4, tr, cc), lambda i, chip_ref: (0, i, 0)),
                      pl.BlockSpec((None, tr, cc), lambda i, chip_ref: (chip_ref[0] if per_chip else 0, i, 0))],
            out_specs=pl.BlockSpec((tr, cc), lambda i, chip_ref: (i, 0))),
        compiler_params=pltpu.CompilerParams(dimension_semantics=("parallel",), vmem_limit_bytes=VMEM_LIMIT),
    )(chip, got, own)


def _pair_gather(halves, name):
    n = len(halves)

    def body(*refs):
        ins, outs = refs[:n], refs[n:2 * n]
        send_sems, recv_sems = refs[2 * n:]
        x, y, c = lax.axis_index("x"), lax.axis_index("y"), lax.axis_index("c")
        copies = [pltpu.make_async_remote_copy(
            src_ref=ins[wi], dst_ref=outs[wi], send_sem=send_sems.at[wi], recv_sem=recv_sems.at[wi],
            device_id=(x, y, 1 - c), device_id_type=MESH) for wi in range(n)]
        for cp in copies:
            cp.start()
        for cp in copies:
            cp.wait()

    return pl.pallas_call(
        body, name=name, in_specs=[ANY] * n, out_specs=[ANY] * n,
        out_shape=[jax.ShapeDtypeStruct(s.shape, s.dtype) for s in halves],
        scratch_shapes=[pltpu.SemaphoreType.DMA((n,))] * 2, compiler_params=pltpu.CompilerParams(has_side_effects=True),
    )(*halves)


def _adamw(mine, other, cidx, w, m, v, name):
    r, c = w.shape
    tr = _row_tile(r // 2, c, 1 << 18)
    nh = (r // 2) // tr

    def body(c_ref, mine_ref, other_ref, w_ref, m_ref, v_ref, g_ref, d_ref, nm_ref, nv_ref):
        g = jnp.where(pl.program_id(0) // nh == c_ref[0], mine_ref[...], other_ref[...])
        g_ref[...] = g
        m_new = ADAM_B1 * m_ref[...] + (1.0 - ADAM_B1) * g
        v_new = ADAM_B2 * v_ref[...] + (1.0 - ADAM_B2) * jnp.square(g)
        m_hat = m_new / (1.0 - ADAM_B1 ** ADAM_STEP)
        v_hat = v_new / (1.0 - ADAM_B2 ** ADAM_STEP)
        d_ref[...] = -ADAM_LR * (m_hat / (jnp.sqrt(v_hat) + ADAM_EPS) + ADAM_WD * w_ref[...])
        nm_ref[...] = m_new
        nv_ref[...] = v_new

    blk = pl.BlockSpec((tr, c), lambda i, c_ref: (i, 0))
    mine_spec = pl.BlockSpec((tr, c), lambda i, c_ref: (jnp.where(i // nh == c_ref[0], i % nh, 0), 0))
    other_spec = pl.BlockSpec((tr, c), lambda i, c_ref: (jnp.where(i // nh == c_ref[0], 0, i % nh), 0))
    return pl.pallas_call(
        body, name=name, out_shape=[jax.ShapeDtypeStruct((r, c), F32)] * 4,
        grid_spec=pltpu.PrefetchScalarGridSpec(
            num_scalar_prefetch=1, grid=(r // tr,), in_specs=[mine_spec, other_spec, blk, blk, blk], out_specs=[blk] * 4),
        compiler_params=pltpu.CompilerParams(dimension_semantics=("arbitrary",), vmem_limit_bytes=VMEM_LIMIT),
    )(cidx, mine, other, w, m, v)


BIG = ["ffn1_w_gate", "ffn1_w_up", "ffn1_w_down", "w_in", "gla_gate_up", "w_branch_fox", "w_branch_gla", "w_merge_gate", "w_out",
       "ffn2_w_gate", "ffn2_w_up", "ffn2_w_down", "w_ple_proj", "w_ple_gate"]
ROW_SHARDED = ("ffn1_w_down", "w_out", "ffn2_w_down", "w_ple_gate")
FUSED = {"ffn1_w_gate": ("gu1", 0, 2), "ffn1_w_up": ("gu1", 1, 2), "ffn2_w_gate": ("gu2", 0, 2), "ffn2_w_up": ("gu2", 1, 2)}
SMALL = ["ffn1_norm", "mix_norm", "fox_forget_bias", "gla_gate_bias", "gla_head_norm", "b_merge_gate", "ffn2_norm", "ple_norm", "final_norm"]
NAMES = ["ffn1_norm", "ffn1_w_gate", "ffn1_w_up", "ffn1_w_down", "mix_norm", "w_in", "fox_forget_bias", "gla_gate_up", "gla_gate_bias",
         "gla_head_norm", "w_branch_fox", "w_branch_gla", "w_merge_gate", "b_merge_gate", "w_out", "ffn2_norm", "ffn2_w_gate", "ffn2_w_up",
         "ffn2_w_down", "ple_norm", "w_ple_proj", "w_ple_gate", "final_norm"]
W_IN_COLS = (FOX_W, FOX_W, FOX_W, FOX_HEADS, GLA_KW, GLA_KW, GLA_VW, GLA_VW, GLA_RANK)
SMALL_ROWS, SMALL_COLS = 16, 1024


def _shard_parts(full, name):
    if name in ROW_SHARDED:
        return full.reshape(4, full.shape[0] // 4, full.shape[1])
    return jnp.transpose(full.reshape(full.shape[0], 4, full.shape[1] // 4), (1, 0, 2))


def _in_layout(w_in, w_merge_gate):
    offs = [0]
    for cw in W_IN_COLS:
        offs.append(offs[-1] + cw)
    col = lambda i: w_in[:, offs[i]:offs[i + 1]]
    big = jnp.concatenate([col(0), col(1), col(2), col(4), col(5), col(6), col(7), w_merge_gate], axis=1)
    sm = jnp.concatenate([col(3), col(8), jnp.zeros((D_MODEL, SMALL_W - FOX_HEADS - GLA_RANK), BF16)], axis=1)
    return big, sm


def _pad_lanes(a, width):
    return jnp.pad(a, ((0, 0), (0, width - a.shape[1])))


def kernel(x, p, ffn1_norm, ffn1_w_gate, ffn1_w_up, ffn1_w_down, mix_norm, w_in, fox_forget_bias, gla_gate_up, gla_gate_bias, gla_head_norm, w_branch_fox, w_branch_gla, w_merge_gate, b_merge_gate, w_out, ffn2_norm, ffn2_w_gate, ffn2_w_up, ffn2_w_down, ple_norm, w_ple_proj, w_ple_gate, final_norm, loss_target, m_ffn1_norm, m_ffn1_w_gate, m_ffn1_w_up, m_ffn1_w_down, m_mix_norm, m_w_in, m_fox_forget_bias, m_gla_gate_up, m_gla_gate_bias, m_gla_head_norm, m_w_branch_fox, m_w_branch_gla, m_w_merge_gate, m_b_merge_gate, m_w_out, m_ffn2_norm, m_ffn2_w_gate, m_ffn2_w_up, m_ffn2_w_down, m_ple_norm, m_w_ple_proj, m_w_ple_gate, m_final_norm, v_ffn1_norm, v_ffn1_w_gate, v_ffn1_w_up, v_ffn1_w_down, v_mix_norm, v_w_in, v_fox_forget_bias, v_gla_gate_up, v_gla_gate_bias, v_gla_head_norm, v_w_branch_fox, v_w_branch_gla, v_w_merge_gate, v_b_merge_gate, v_w_out, v_ffn2_norm, v_ffn2_w_gate, v_ffn2_w_up, v_ffn2_w_down, v_ple_norm, v_w_ple_proj, v_w_ple_gate, v_final_norm):
    args = dict(locals())
    wts = {n: args[n] for n in NAMES}
    mom = {n: args["m_" + n] for n in NAMES}
    var = {n: args["v_" + n] for n in NAMES}
    two_d = lambda a: a.reshape(-1, a.shape[-1])

    wire = lambda n: F32 if n == "gla_gate_up" else BF16
    cidx = lax.axis_index("c").astype(jnp.int32).reshape(1)
    chip = (2 * lax.axis_index("x") + lax.axis_index("y")).astype(jnp.int32)
    shards = {n: two_d(wts[n]).astype(wire(n)) for n in BIG}
    sp = {
        "ffn1_norm": two_d(ffn1_norm), "mix_norm": two_d(mix_norm), "fb": _pad_lanes(two_d(fox_forget_bias), SMALL_W),
        "gb": two_d(gla_gate_bias), "ghn": two_d(gla_head_norm), "bm": two_d(b_merge_gate), "ffn2_norm": two_d(ffn2_norm),
        "ple_norm": two_d(ple_norm), "final_norm": two_d(final_norm),
    }

    loss, grad_x, reduced, ds_ = _local_step(x[0], p[0, 0], loss_target[0], shards, sp, cidx, chip)

    small_g = {"ffn1_norm": ds_["ffn1_norm"], "mix_norm": ds_["mix_norm"], "fox_forget_bias": ds_["fb"][:, :FOX_HEADS],
               "gla_gate_bias": ds_["gb"], "gla_head_norm": ds_["ghn"], "b_merge_gate": ds_["bm"], "ffn2_norm": ds_["ffn2_norm"],
               "ple_norm": ds_["ple_norm"], "final_norm": ds_["final_norm"]}
    small_w = sum(two_d(wts[n]).shape[1] for n in SMALL)
    assert small_w <= SMALL_ROWS * SMALL_COLS
    packed = lambda d: _pad_lanes(jnp.concatenate([two_d(d[n]) for n in SMALL], axis=1), SMALL_ROWS * SMALL_COLS).reshape(SMALL_ROWS, SMALL_COLS)
    parts = [packed(small_g)[None]]
    pair_sums = [_sum_half(a, b, cidx, "sum_half") for a, b in zip(parts, _pair_swap(parts, "pair_swap"))]
    reduced["small"] = (_run_comm(_exchange_comm(pair_sums), "chip_exchange")[0], pair_sums[0])
    mine = [_sum_chips(*reduced[n], chip.reshape(1), "sum_chips") for n in BIG + ["small"]]
    other = _pair_gather(mine, "pair_gather")

    out = {}
    for n, a, b in zip(BIG, mine[:-1], other[:-1]):
        out[n] = [r.reshape(wts[n].shape) for r in _adamw(a, b, cidx, two_d(wts[n]), two_d(mom[n]), two_d(var[n]), "adamw_" + n)]
    small_out = [r.reshape(1, SMALL_ROWS * SMALL_COLS) for r in _adamw(mine[-1], other[-1], cidx, packed(wts), packed(mom), packed(var), "adamw_small")]
    off = 0
    for n in SMALL:
        cw = two_d(wts[n]).shape[1]
        out[n] = [r[:, off:off + cw].reshape(wts[n].shape) for r in small_out]
        off += cw

    total = lax.psum(loss[0, 0], ("x", "y", "c"))
    return (total, grad_x[None], *[out[n][0] for n in NAMES], *[out[n][1] for n in NAMES],
            *[out[n][2] for n in NAMES], *[out[n][3] for n in NAMES])
```

```python
import functools

import jax
import jax.numpy as jnp
from jax import lax
from jax.experimental import pallas as pl
from jax.experimental.pallas import tpu as pltpu

F32 = jnp.float32
BF16 = jnp.bfloat16
MESH = pl.DeviceIdType.MESH
ANY = pl.BlockSpec(memory_space=pl.ANY)

D_MODEL = 2048
FOX_HEADS = 8
HEAD_DIM = 128
GLA_HEADS = 4
GLA_VAL_DIM = 256
GLA_RANK = 16
GLA_TAU = 16.0
CHUNK = 64
EPS = 1e-6
FOX_W = FOX_HEADS * HEAD_DIM
GLA_KW = GLA_HEADS * HEAD_DIM
GLA_VW = GLA_HEADS * GLA_VAL_DIM
Z_FQ, Z_FK, Z_FV, Z_GQ, Z_GK, Z_GV, Z_GR, Z_GL = 0, 1024, 2048, 3072, 3584, 4096, 5120, 6144
Z_W = Z_GL + 2 * D_MODEL
SMALL_W = 128
NEG = -1e30

ADAM_LR, ADAM_B1, ADAM_B2, ADAM_EPS, ADAM_WD, ADAM_STEP = 0.001, 0.9, 0.999, 1e-08, 0.01, 10

VMEM_LIMIT = 56 * 1024 * 1024
BOUNCE_BYTES = 2 * 1024 * 1024


def _pick(n, target, mult=128):
    if n <= target:
        return n
    best = None
    for d in range(mult, target + 1, mult):
        if n % d == 0:
            best = d
    assert best is not None, (n, target)
    return best


def _mm(a, b, *, ta=False, tb=False, out_dtype=BF16, name, comm=None):
    m, k = (a.shape[1], a.shape[0]) if ta else a.shape
    n = b.shape[0] if tb else b.shape[1]
    assert (b.shape[1] if tb else b.shape[0]) == k
    bk = _pick(k, 2048)
    nk = k // bk
    bm, bn = _pick(m, 1024), _pick(n, 1024 if nk > 1 else 512)
    dims = (((0 if ta else 1,), (1 if tb else 0,)), ((), ()))

    def body(a_ref, b_ref, o_ref, acc_ref):
        part = lax.dot_general(a_ref[...], b_ref[...], dims, preferred_element_type=F32)
        if nk == 1:
            o_ref[...] = part.astype(o_ref.dtype)
            return
        kk = pl.program_id(2)

        @pl.when(kk == 0)
        def _():
            acc_ref[...] = part

        @pl.when(kk > 0)
        def _():
            acc_ref[...] += part

        @pl.when(kk == nk - 1)
        def _():
            o_ref[...] = acc_ref[...].astype(o_ref.dtype)

    a_spec = pl.BlockSpec((bk, bm), lambda i, j, kk: (kk, i)) if ta else pl.BlockSpec((bm, bk), lambda i, j, kk: (i, kk))
    b_spec = pl.BlockSpec((bn, bk), lambda i, j, kk: (j, kk)) if tb else pl.BlockSpec((bk, bn), lambda i, j, kk: (kk, j))
    (out,), travelled = _hosted(
        body, comm, name=name, grid=(m // bm, n // bn, nk),
        in_specs=[a_spec, b_spec], out_specs=[pl.BlockSpec((bm, bn), lambda i, j, kk: (i, j))],
        out_shape=[jax.ShapeDtypeStruct((m, n), out_dtype)], scratch_shapes=[pltpu.VMEM((bm, bn), F32)],
        semantics=("parallel", "parallel", "arbitrary"), args=(a, b))
    return out if comm is None else (out, travelled)


def _rowwise(fn, tiled, bcast, outs, reds=(), *, tt, name):
    t = tiled[0][0].shape[0]
    tt = min(tt, t)
    nin, nout = len(tiled) + len(bcast), len(outs)
    splits = [s[3] for s in tiled] + [s[1] for s in bcast]

    def store(ref, val, acc):
        off = 0
        for piece in val if isinstance(val, (tuple, list)) else (val,):
            w = piece.shape[-1]
            if acc:
                ref[:, off:off + w] += piece.astype(ref.dtype)
            else:
                ref[:, off:off + w] = piece.astype(ref.dtype)
            off += w
        assert off == ref.shape[-1], (name, off, ref.shape)

    def body(*refs):
        args = []
        for ref, sp in zip(refs[:nin], splits):
            if sp is None:
                args.append(ref[...])
            else:
                off = 0
                for w in sp:
                    args.append(ref[:, off:off + w])
                    off += w
        res = fn(*args)
        res = res if isinstance(res, (tuple, list)) else (res,)
        assert len(res) == nout + len(reds), (name, len(res))
        for ref, val in zip(refs[nin:nin + nout], res[:nout]):
            store(ref, val, False)
        if reds:
            @pl.when(pl.program_id(0) == 0)
            def _():
                for ref in refs[nin + nout:]:
                    ref[...] = jnp.zeros(ref.shape, ref.dtype)
            for ref, val in zip(refs[nin + nout:], res[nout:]):
                store(ref, val, True)

    in_specs = [pl.BlockSpec((tt, w), functools.partial(lambda i, cb: (i, cb), cb=cb)) for (_, w, cb, _) in tiled]
    in_specs += [pl.BlockSpec(arr.shape, lambda i: (0, 0)) for (arr, _) in bcast]
    out_specs = [pl.BlockSpec((tt, w), lambda i: (i, 0)) for (w, _) in outs]
    out_specs += [pl.BlockSpec((r, w), lambda i: (0, 0)) for (r, w) in reds]
    out_shape = [jax.ShapeDtypeStruct((t, w), dt) for (w, dt) in outs] + [jax.ShapeDtypeStruct((r, w), F32) for (r, w) in reds]
    return pl.pallas_call(
        body, name=name, grid=(t // tt,), in_specs=in_specs, out_specs=out_specs, out_shape=out_shape,
        compiler_params=pltpu.CompilerParams(dimension_semantics=("arbitrary" if reds else "parallel",), vmem_limit_bytes=VMEM_LIMIT),
    )(*[s[0] for s in tiled], *[s[0] for s in bcast])


def _full(arr):
    return (arr, arr.shape[1], 0, None)


def _f(x):
    return x.astype(F32)


def _rms(x, g):
    return x * lax.rsqrt(jnp.mean(x * x, axis=-1, keepdims=True) + EPS) * g


def _log_sigmoid(x):
    return jnp.minimum(x, 0.0) - jnp.log1p(jnp.exp(-jnp.abs(x)))


def _silu(x):
    return x * jax.nn.sigmoid(x)


def _norm_fwd(x, g, name):
    return _rowwise(lambda xb, gb: _rms(_f(xb), gb), [_full(x)], [(g, None)], [(x.shape[1], BF16)], tt=256, name=name)[0]


def _resnorm_fwd(res, branch, g, coef, name):
    def fn(rb, bb, gb):
        h = rb + coef * _f(bb)
        return h, _rms(h, gb)
    d = res.shape[1]
    return _rowwise(fn, [_full(res), _full(branch)], [(g, None)], [(d, F32), (d, BF16)], tt=256, name=name)


def _norm_bwd(h, dns, dres, g, coef, name):
    nd = len(dns)

    def fn(hb, *rest):
        dn = _f(rest[0])
        for extra in rest[1:nd]:
            dn = dn + _f(extra)
        dr, gb = rest[nd], rest[nd + 1]
        _, vjp = jax.vjp(_rms, hb, gb)
        dh, dg = vjp(dn)
        dh = dh + dr
        return dh, coef * dh, dg
    d = h.shape[1]
    return _rowwise(fn, [_full(h)] + [_full(x) for x in dns] + [_full(dres)], [(g, None)],
                    [(d, F32), (d, BF16)], [(1, d)], tt=256, name=name)


def _act_fwd(gu, name):
    ff = gu.shape[1] // 2
    return _rowwise(lambda gb, ub: _silu(_f(gb)) * _f(ub), [(gu, 2 * ff, 0, (ff, ff))], [], [(ff, BF16)], tt=256, name=name)[0]


def _act_bwd(gu, da, name):
    ff = gu.shape[1] // 2

    def fn(gb, ub, dab):
        _, vjp = jax.vjp(lambda p, q: _silu(p) * q, _f(gb), _f(ub))
        return (vjp(_f(dab)),)
    return _rowwise(fn, [(gu, 2 * ff, 0, (ff, ff)), _full(da)], [], [(2 * ff, BF16)], tt=128, name=name)[0]


def _merge(glf, glg, bf, bg, bmf, bmg):
    return jax.nn.sigmoid(_f(glf) + bmf) * _f(bf) + jax.nn.sigmoid(_f(glg) + bmg) * _f(bg)


def _merge_fwd(z, bf, bg, bm, name):
    d = D_MODEL
    return _rowwise(_merge, [(z, d, Z_GL // d, None), (z, d, Z_GL // d + 1, None), _full(bf), _full(bg)], [(bm, (d, d))],
                    [(d, BF16)], tt=256, name=name)[0]


def _merge_bwd(z, bf, bg, bm, dm, name):
    d = D_MODEL

    def fn(glf, glg, bfb, bgb, dmb, bmf, bmg):
        _, vjp = jax.vjp(_merge, _f(glf), _f(glg), _f(bfb), _f(bgb), bmf, bmg)
        dglf, dglg, dbf, dbg, dbmf, dbmg = vjp(_f(dmb))
        return (dglf, dglg), dbf, dbg, (dbmf, dbmg)
    return _rowwise(fn, [(z, d, Z_GL // d, None), (z, d, Z_GL // d + 1, None), _full(bf), _full(bg), _full(dm)], [(bm, (d, d))],
                    [(2 * d, BF16), (d, BF16), (d, BF16)], [(1, 2 * d)], tt=128, name=name)


def _gla_out(o, gr, g):
    return _rms(o, g) * _silu(_f(gr))


_PER_HEAD = (GLA_VAL_DIM,) * GLA_HEADS


def _gla_out_fwd(o, z, g, name):
    nh = GLA_HEADS

    def fn(*blocks):
        return (tuple(_gla_out(blocks[h], blocks[nh + h], blocks[2 * nh]) for h in range(nh)),)
    return _rowwise(fn, [(o, GLA_VW, 0, _PER_HEAD), (z, GLA_VW, Z_GR // GLA_VW, _PER_HEAD)], [(g, None)], [(GLA_VW, BF16)], tt=256, name=name)[0]


def _gla_out_bwd(o, z, g, dy, name):
    nh = GLA_HEADS

    def fn(*blocks):
        gb = blocks[3 * nh]
        grads = []
        for h in range(nh):
            _, vjp = jax.vjp(_gla_out, blocks[h], _f(blocks[nh + h]), gb)
            grads.append(vjp(_f(blocks[2 * nh + h])))
        dg = grads[0][2]
        for h in range(1, nh):
            dg = dg + grads[h][2]
        return tuple(gr[0] for gr in grads), tuple(gr[1] for gr in grads), dg
    return _rowwise(fn, [(o, GLA_VW, 0, _PER_HEAD), (z, GLA_VW, Z_GR // GLA_VW, _PER_HEAD), (dy, GLA_VW, 0, _PER_HEAD)], [(g, None)],
                    [(GLA_VW, F32), (GLA_VW, BF16)], [(1, GLA_VAL_DIM)], tt=256, name=name)


def _small_gates(s, fb, gup, gb):
    lane = lax.broadcasted_iota(jnp.int32, s.shape, 1)
    lf = jnp.where(lane < FOX_HEADS, _log_sigmoid(s + fb), 0.0)
    pre = jnp.dot(s.astype(BF16), gup.astype(BF16), preferred_element_type=F32) + gb
    return lf, _log_sigmoid(pre) / GLA_TAU


def _small_fwd(s, fb, gup, gb, name):
    return _rowwise(_small_gates, [_full(s)], [(fb, None), (gup, None), (gb, None)], [(SMALL_W, F32), (GLA_KW, F32)], tt=256, name=name)


def _small_bwd(s, fb, gup, gb, dlf, dla, name):
    def fn(sb, dlfb, dlab, fbb, gupb, gbb):
        _, vjp = jax.vjp(_small_gates, sb, fbb, gupb, gbb)
        return vjp((dlfb, dlab))
    return _rowwise(fn, [_full(s), _full(dlf), _full(dla)], [(fb, None), (gup, None), (gb, None)],
                    [(SMALL_W, BF16)], [(1, SMALL_W), (SMALL_W, GLA_KW), (1, GLA_KW)], tt=256, name=name)


def _head_fn(h3, pgl, pp, tgt, gf):
    h4 = h3 + jax.nn.sigmoid(pgl) * pp
    err = _rms(h4, gf) - tgt
    return 0.5 * jnp.sum(jnp.mean(err * err, axis=-1, keepdims=True))


def _head(h3, pgl, pp, tgt, gf, name):
    def fn(hb, gl, pb, tb, gfb):
        loss, vjp = jax.vjp(_head_fn, hb, _f(gl), _f(pb), tb, gfb)
        dh, dgl, dpp, _, dgf = vjp(jnp.ones((), F32))
        return dh, dgl, dpp, jnp.full((1, 128), loss, F32), dgf
    d = h3.shape[1]
    return _rowwise(fn, [_full(h3), _full(pgl), _full(pp), _full(tgt)], [(gf, None)],
                    [(d, F32), (d, BF16), (d, BF16)], [(1, 128), (1, d)], tt=256, name=name)


def _cumsum_tokens(a, reverse, name):
    t, w = a.shape
    r = min(256, t)
    nb = t // r

    def body(a_ref, o_ref, carry_ref):
        @pl.when(pl.program_id(0) == 0)
        def _():
            carry_ref[...] = jnp.zeros(carry_ref.shape, F32)
        row = lax.broadcasted_iota(jnp.int32, (r, r), 0)
        col = lax.broadcasted_iota(jnp.int32, (r, r), 1)
        tri = ((col >= row) if reverse else (col <= row)).astype(F32)
        blk = a_ref[...]
        o_ref[...] = jnp.dot(tri, blk, preferred_element_type=F32, precision=lax.Precision.HIGHEST) + carry_ref[...]
        carry_ref[...] += jnp.sum(blk, axis=0, keepdims=True)

    idx = (lambda i: (nb - 1 - i, 0)) if reverse else (lambda i: (i, 0))
    return pl.pallas_call(
        body, name=name, grid=(nb,), in_specs=[pl.BlockSpec((r, w), idx)], out_specs=pl.BlockSpec((r, w), idx),
        out_shape=jax.ShapeDtypeStruct((t, w), F32), scratch_shapes=[pltpu.VMEM((1, w), F32)],
        compiler_params=pltpu.CompilerParams(dimension_semantics=("arbitrary",)),
    )(a)


FOX_TQ, FOX_TK = 256, 512
FOX_SCALE = HEAD_DIM ** -0.5


def _fox_tiles(t):
    tq, tk = min(FOX_TQ, t), min(FOX_TK, t)
    return tq, tk, t // tq, t // tk


def _blocked_t(a, blk):
    return a.reshape(a.shape[0] // blk, blk, a.shape[1]).transpose(0, 2, 1)


def _unblocked_t(b):
    return b.transpose(0, 2, 1).reshape(b.shape[0] * b.shape[2], b.shape[1])


def _fox_scores(k, qt, frep, i, j, masked):
    tk, tq = k.shape[0], qt.shape[1]
    st = jnp.dot(k, qt, preferred_element_type=F32) * FOX_SCALE - jnp.tile(frep, (1, tq // HEAD_DIM))
    if masked:
        key = j * tk + lax.broadcasted_iota(jnp.int32, (tk, tq), 0)
        query = i * tq + lax.broadcasted_iota(jnp.int32, (tk, tq), 1)
        st = jnp.where(key <= query, st, NEG)
    return st


def _fox_fwd(z, qt, vt, frep, name, comm=None):
    t = z.shape[0]
    tq, tk, nq, nk = _fox_tiles(t)
    kb = Z_FK // HEAD_DIM

    def body(qt_ref, k_ref, vt_ref, frep_ref, ot_ref, lse_ref):
        i = pl.program_id(1)
        qt = qt_ref[...]
        last = ((i + 1) * tq - 1) // tk

        def block(j, carry, masked):
            m, l, acc = carry
            rows = pl.ds(pl.multiple_of(j * tk, tk), tk)
            st = _fox_scores(k_ref[rows, :], qt, frep_ref[rows, :], i, j, masked)
            m_new = jnp.maximum(m, jnp.max(st, axis=0, keepdims=True))
            alpha = jnp.exp(m - m_new)
            p = jnp.exp(st - m_new)
            l = alpha * l + jnp.sum(p, axis=0, keepdims=True)
            acc = alpha * acc + jnp.dot(vt_ref[j], p.astype(BF16), preferred_element_type=F32)
            return m_new, l, acc

        init = (jnp.full((1, tq), NEG, F32), jnp.zeros((1, tq), F32), jnp.zeros((HEAD_DIM, tq), F32))
        m, l, acc = block(last, lax.fori_loop(0, last, lambda j, c: block(j, c, False), init), True)
        ot_ref[...] = (acc / l).astype(ot_ref.dtype)
        lse_ref[...] = m + jnp.log(l)

    stat = pl.BlockSpec((None, None, 1, tq), lambda h, i: (h, i, 0, 0))
    (ot, lse), travelled = _hosted(
        body, comm, name=name, grid=(FOX_HEADS, nq),
        in_specs=[pl.BlockSpec((None, HEAD_DIM, tq), lambda h, i: (i, h, 0)),
                  pl.BlockSpec((t, HEAD_DIM), lambda h, i: (0, kb + h)),
                  pl.BlockSpec((nk, HEAD_DIM, tk), lambda h, i: (0, h, 0)),
                  pl.BlockSpec((None, t, HEAD_DIM), lambda h, i: (h, 0, 0))],
        out_specs=[pl.BlockSpec((None, HEAD_DIM, tq), lambda h, i: (i, h, 0)), stat],
        out_shape=[jax.ShapeDtypeStruct((nq, FOX_W, tq), BF16), jax.ShapeDtypeStruct((FOX_HEADS, nq, 1, tq), F32)],
        scratch_shapes=[], semantics=("parallel", "parallel"), args=(qt, z, vt, frep))
    return ot, lse, travelled


def _fox_bwd_q(z, qt, kt, ot, dot, lse, frep, name):
    t = z.shape[0]
    tq, tk, nq, nk = _fox_tiles(t)
    kb, vb = Z_FK // HEAD_DIM, Z_FV // HEAD_DIM

    def body(qt_ref, k_ref, kt_ref, v_ref, ot_ref, dot_ref, lse_ref, frep_ref, dqt_ref, delta_ref, dfq_ref):
        i = pl.program_id(1)
        qt, dot = qt_ref[...], dot_ref[...]
        lse = lse_ref[...]
        delta = jnp.sum(_f(dot) * _f(ot_ref[...]), axis=0, keepdims=True)
        delta_ref[...] = delta
        last = ((i + 1) * tq - 1) // tk

        def block(j, carry, masked):
            dq, dfq = carry
            rows = pl.ds(pl.multiple_of(j * tk, tk), tk)
            p = jnp.exp(_fox_scores(k_ref[rows, :], qt, frep_ref[rows, :], i, j, masked) - lse)
            dp = jnp.dot(v_ref[rows, :], dot, preferred_element_type=F32)
            ds = p * (dp - delta)
            return dq + jnp.dot(kt_ref[j], ds.astype(BF16), preferred_element_type=F32), dfq + jnp.sum(ds, axis=0, keepdims=True)

        init = (jnp.zeros((HEAD_DIM, tq), F32), jnp.zeros((1, tq), F32))
        dq, dfq = block(last, lax.fori_loop(0, last, lambda j, c: block(j, c, False), init), True)
        dqt_ref[...] = (dq * FOX_SCALE).astype(dqt_ref.dtype)
        dfq_ref[...] = dfq

    mine = pl.BlockSpec((None, HEAD_DIM, tq), lambda h, i: (i, h, 0))
    stat = pl.BlockSpec((None, None, 1, tq), lambda h, i: (h, i, 0, 0))
    return pl.pallas_call(
        body, name=name, grid=(FOX_HEADS, nq),
        in_specs=[mine,
                  pl.BlockSpec((t, HEAD_DIM), lambda h, i: (0, kb + h)),
                  pl.BlockSpec((nk, HEAD_DIM, tk), lambda h, i: (0, h, 0)),
                  pl.BlockSpec((t, HEAD_DIM), lambda h, i: (0, vb + h)),
                  mine, mine, stat,
                  pl.BlockSpec((None, t, HEAD_DIM), lambda h, i: (h, 0, 0))],
        out_specs=[mine, stat, stat],
        out_shape=[jax.ShapeDtypeStruct((nq, FOX_W, tq), BF16), jax.ShapeDtypeStruct((FOX_HEADS, nq, 1, tq), F32),
                   jax.ShapeDtypeStruct((FOX_HEADS, nq, 1, tq), F32)],
        compiler_params=pltpu.CompilerParams(dimension_semantics=("parallel", "parallel"), vmem_limit_bytes=VMEM_LIMIT),
    )(qt, z, kt, z, ot, dot, lse, frep)


def _fox_bwd_kv(z, qt, do, dot, lse, delta, frep, name):
    t = z.shape[0]
    tq, tk, nq, nk = _fox_tiles(t)
    qb, kb, vb = Z_FQ // HEAD_DIM, Z_FK // HEAD_DIM, Z_FV // HEAD_DIM
    per = tk // tq

    def body(k_ref, v_ref, frep_ref, q_ref, qt_ref, do_ref, dot_ref, lse_ref, delta_ref, dk_ref, dv_ref, dfk_ref):
        j = pl.program_id(1)
        k, v, frep = k_ref[...], v_ref[...], frep_ref[...]

        def block(i, carry, masked):
            dk, dv, dfk = carry
            rows = pl.ds(pl.multiple_of(i * tq, tq), tq)
            p = jnp.exp(_fox_scores(k, qt_ref[i], frep, i, j, masked) - lse_ref[i])
            dv = dv + jnp.dot(p.astype(BF16), do_ref[rows, :], preferred_element_type=F32)
            dp = jnp.dot(v, dot_ref[i], preferred_element_type=F32)
            ds = p * (dp - delta_ref[i])
            dk = dk + jnp.dot(ds.astype(BF16), q_ref[rows, :], preferred_element_type=F32)
            for part in range(tq // HEAD_DIM):
                dfk = dfk + ds[:, part * HEAD_DIM:(part + 1) * HEAD_DIM]
            return dk, dv, dfk

        zero = jnp.zeros((tk, HEAD_DIM), F32)
        carry = (zero, zero, zero)
        for step in range(per):
            carry = block(j * per + step, carry, True)
        dk, dv, dfk = lax.fori_loop((j + 1) * per, nq, lambda i, c: block(i, c, False), carry)
        dk_ref[...] = (dk * FOX_SCALE).astype(dk_ref.dtype)
        dv_ref[...] = dv.astype(dv_ref.dtype)
        dfk_ref[...] = jnp.sum(dfk, axis=1, keepdims=True)

    whole_t = pl.BlockSpec((nq, HEAD_DIM, tq), lambda h, j: (0, h, 0))
    whole_stat = pl.BlockSpec((None, nq, 1, tq), lambda h, j: (h, 0, 0, 0))
    return pl.pallas_call(
        body, name=name, grid=(FOX_HEADS, nk),
        in_specs=[pl.BlockSpec((tk, HEAD_DIM), lambda h, j: (j, kb + h)),
                  pl.BlockSpec((tk, HEAD_DIM), lambda h, j: (j, vb + h)),
                  pl.BlockSpec((None, tk, HEAD_DIM), lambda h, j: (h, j, 0)),
                  pl.BlockSpec((t, HEAD_DIM), lambda h, j: (0, qb + h)),
                  whole_t,
                  pl.BlockSpec((t, HEAD_DIM), lambda h, j: (0, h)),
                  whole_t, whole_stat, whole_stat],
        out_specs=[pl.BlockSpec((tk, HEAD_DIM), lambda h, j: (j, h)), pl.BlockSpec((tk, HEAD_DIM), lambda h, j: (j, h)),
                   pl.BlockSpec((None, tk, 1), lambda h, j: (h, j, 0))],
        out_shape=[jax.ShapeDtypeStruct((t, FOX_W), BF16), jax.ShapeDtypeStruct((t, FOX_W), BF16),
                   jax.ShapeDtypeStruct((FOX_HEADS, t, 1), F32)],
        compiler_params=pltpu.CompilerParams(dimension_semantics=("parallel", "parallel"), vmem_limit_bytes=VMEM_LIMIT),
    )(z, z, frep, z, qt, do, dot, lse, delta)


def _gla_step(st, q, k, v, la):
    row = lax.broadcasted_iota(jnp.int32, (CHUNK, CHUNK), 0)
    col = lax.broadcasted_iota(jnp.int32, (CHUNK, CHUNK), 1)
    tri = (col <= row).astype(F32)
    a_cum = jnp.dot(tri, la, preferred_element_type=F32, precision=lax.Precision.HIGHEST)
    a_tot = jnp.sum(la, axis=0, keepdims=True)
    k_dec = (_f(k) * jnp.exp(a_tot - a_cum)).astype(BF16)
    qs = (_f(q) * (HEAD_DIM ** -0.5)).astype(BF16)
    st = st * jnp.exp(a_tot) + lax.dot_general(v.astype(BF16), k_dec, (((0,), (0,)), ((), ())), preferred_element_type=F32)
    o = lax.dot_general(qs, st.astype(BF16), (((1,), (1,)), ((), ())), preferred_element_type=F32)
    return st, o


def _gla_blocks(t):
    r = min(256, t)
    return r, t // r, r // CHUNK


def _gla_fwd(z, la, name):
    t = z.shape[0]
    r, nb, nch = _gla_blocks(t)

    def body(q_ref, k_ref, v_ref, la_ref, o_ref, sp_ref, st_ref):
        @pl.when(pl.program_id(0) == 0)
        def _():
            st_ref[...] = jnp.zeros(st_ref.shape, F32)
        for c in range(nch):
            rows = slice(c * CHUNK, (c + 1) * CHUNK)
            for h in range(GLA_HEADS):
                kc = slice(h * HEAD_DIM, (h + 1) * HEAD_DIM)
                vc = slice(h * GLA_VAL_DIM, (h + 1) * GLA_VAL_DIM)
                st = st_ref[h]
                sp_ref[c, h] = st
                st, o = _gla_step(st, q_ref[rows, kc], k_ref[rows, kc], v_ref[rows, vc], la_ref[rows, kc])
                st_ref[h] = st
                o_ref[rows, vc] = o

    return pl.pallas_call(
        body, name=name, grid=(nb,),
        in_specs=[pl.BlockSpec((r, GLA_KW), lambda i: (i, Z_GQ // GLA_KW)), pl.BlockSpec((r, GLA_KW), lambda i: (i, Z_GK // GLA_KW)),
                  pl.BlockSpec((r, GLA_VW), lambda i: (i, Z_GV // GLA_VW)), pl.BlockSpec((r, GLA_KW), lambda i: (i, 0))],
        out_specs=[pl.BlockSpec((r, GLA_VW), lambda i: (i, 0)),
                   pl.BlockSpec((nch, GLA_HEADS, GLA_VAL_DIM, HEAD_DIM), lambda i: (i, 0, 0, 0))],
        out_shape=[jax.ShapeDtypeStruct((t, GLA_VW), F32),
                   jax.ShapeDtypeStruct((t // CHUNK, GLA_HEADS, GLA_VAL_DIM, HEAD_DIM), F32)],
        scratch_shapes=[pltpu.VMEM((GLA_HEADS, GLA_VAL_DIM, HEAD_DIM), F32)],
        compiler_params=pltpu.CompilerParams(dimension_semantics=("arbitrary",), vmem_limit_bytes=VMEM_LIMIT),
    )(z, z, z, la)


def _gla_bwd(z, la, sprev, do, name):
    t = z.shape[0]
    r, nb, nch = _gla_blocks(t)

    def body(q_ref, k_ref, v_ref, la_ref, sp_ref, do_ref, dq_ref, dk_ref, dv_ref, dla_ref, dst_ref):
        @pl.when(pl.program_id(0) == 0)
        def _():
            dst_ref[...] = jnp.zeros(dst_ref.shape, F32)
        for c in reversed(range(nch)):
            rows = slice(c * CHUNK, (c + 1) * CHUNK)
            for h in range(GLA_HEADS):
                kc = slice(h * HEAD_DIM, (h + 1) * HEAD_DIM)
                vc = slice(h * GLA_VAL_DIM, (h + 1) * GLA_VAL_DIM)
                _, vjp = jax.vjp(_gla_step, sp_ref[c, h], q_ref[rows, kc], k_ref[rows, kc], v_ref[rows, vc], la_ref[rows, kc])
                dst, dq, dk, dv, dla = vjp((dst_ref[h], do_ref[rows, vc]))
                dst_ref[h] = dst
                dq_ref[rows, kc] = dq
                dk_ref[rows, kc] = dk
                dv_ref[rows, vc] = dv
                dla_ref[rows, kc] = dla

    rev = lambda i: (nb - 1 - i, 0)
    return pl.pallas_call(
        body, name=name, grid=(nb,),
        in_specs=[pl.BlockSpec((r, GLA_KW), lambda i: (nb - 1 - i, Z_GQ // GLA_KW)), pl.BlockSpec((r, GLA_KW), lambda i: (nb - 1 - i, Z_GK // GLA_KW)),
                  pl.BlockSpec((r, GLA_VW), lambda i: (nb - 1 - i, Z_GV // GLA_VW)), pl.BlockSpec((r, GLA_KW), rev),
                  pl.BlockSpec((nch, GLA_HEADS, GLA_VAL_DIM, HEAD_DIM), lambda i: (nb - 1 - i, 0, 0, 0)),
                  pl.BlockSpec((r, GLA_VW), rev)],
        out_specs=[pl.BlockSpec((r, GLA_KW), rev), pl.BlockSpec((r, GLA_KW), rev), pl.BlockSpec((r, GLA_VW), rev), pl.BlockSpec((r, GLA_KW), rev)],
        out_shape=[jax.ShapeDtypeStruct((t, GLA_KW), BF16), jax.ShapeDtypeStruct((t, GLA_KW), BF16),
                   jax.ShapeDtypeStruct((t, GLA_VW), BF16), jax.ShapeDtypeStruct((t, GLA_KW), F32)],
        scratch_shapes=[pltpu.VMEM((GLA_HEADS, GLA_VAL_DIM, HEAD_DIM), F32)],
        compiler_params=pltpu.CompilerParams(dimension_semantics=("arbitrary",), vmem_limit_bytes=VMEM_LIMIT),
    )(z, z, z, la, sprev, do)


def _local_step(x, p, tgt, shards, sp, cidx, chip):
    t = x.shape[0]
    tq, tk, _, _ = _fox_tiles(t)
    full, reduced = {}, {}

    def plan(names):
        keys, shapes, places = [], [], []
        for n in names:
            r, cc = shards[n].shape
            key, part, parts = FUSED.get(n, (n, 0, 1))
            if key not in keys:
                keys.append(key)
                stacked = n in ROW_SHARDED or n == "w_in"
                shapes.append(jax.ShapeDtypeStruct((4 * r, cc) if stacked else (r, 4 * cc * parts), shards[n].dtype))
            places.append((keys.index(key), r, 0, 0) if n in ROW_SHARDED or n == "w_in" else (keys.index(key), 0, part * 4 * cc, cc))
        return keys, shapes, places

    def gather(names):
        _, shapes, places = plan(names)
        return _ag_comm([shards[n] for n in names], shapes, places)

    def landed(names, got):
        keys, _, _ = plan(names)
        for key, g in zip(keys, got):
            full[key] = g
        if "w_in" in keys:
            r = shards["w_in"].shape[0]
            full["w_in"] = jnp.transpose(full["w_in"].reshape(4, r, -1), (1, 0, 2)).reshape(r, -1)

    def pair_sums(grads):
        parts = [_shard_parts(g, n) for n, g in grads.items()]
        swapped = _pair_swap(parts, "pair_swap")
        return list(grads), [_sum_half(a, b, cidx, "sum_half") for a, b in zip(parts, swapped)]

    def exchanged(names, sums, got):
        for n, own, g in zip(names, sums, got):
            reduced[n] = (g, own)

    first = ["ffn1_w_gate", "ffn1_w_up"]
    landed(first, _run_comm(gather(first), "all_gather"))
    w_gu1 = full["gu1"]
    n1 = _norm_fwd(x, sp["ffn1_norm"], "norm1_fwd")
    names = ["ffn1_w_down", "w_in", "gla_gate_up"]
    gu1, got = _mm(n1, w_gu1, name="mm_gu_gather", comm=gather(names))
    landed(names, got)
    a1 = _act_fwd(gu1, "act_fwd")
    names = ["w_merge_gate", "w_branch_fox", "w_branch_gla"]
    f1, got = _mm(a1, full["ffn1_w_down"], out_dtype=F32, name="mm_down_gather", comm=gather(names))
    landed(names, got)
    w_big, w_sm = _in_layout(full["w_in"], full["w_merge_gate"])
    gup = jnp.zeros((SMALL_W, GLA_KW), F32).at[FOX_HEADS:FOX_HEADS + GLA_RANK].set(full["gla_gate_up"])
    h1, u = _resnorm_fwd(x, f1, sp["mix_norm"], 0.5, "resnorm_fwd_half")
    names = ["w_out", "ffn2_w_down", "w_ple_proj", "w_ple_gate"]
    z, got = _mm(u, w_big, name="mm_in_gather", comm=gather(names))
    landed(names, got)
    s = _mm(u, w_sm, out_dtype=F32, name="mm_in_small")
    lf, la = _small_fwd(s, sp["fb"], gup, sp["gb"], "small_fwd")
    fp = _cumsum_tokens(lf, False, "cumsum_fwd")
    frep = jnp.broadcast_to(fp[:, :FOX_HEADS].T[:, :, None], (FOX_HEADS, t, HEAD_DIM))
    qt = _blocked_t(z[:, Z_FQ:Z_FQ + FOX_W], tq)
    kt = _blocked_t(z[:, Z_FK:Z_FK + FOX_W], tk)
    vt = _blocked_t(z[:, Z_FV:Z_FV + FOX_W], tk)
    names = ["ffn2_w_gate", "ffn2_w_up"]
    ot, lse, got = _fox_fwd(z, qt, vt, frep, "fox_fwd_gather", comm=gather(names))
    landed(names, got)
    w_gu2 = full["gu2"]
    y_fox = _unblocked_t(ot)
    o_gla, sprev = _gla_fwd(z, la, "gla_fwd")
    y_gla = _gla_out_fwd(o_gla, z, sp["ghn"], "gla_out_fwd")
    bf = _mm(y_fox, full["w_branch_fox"], name="mm_branch")
    bg = _mm(y_gla, full["w_branch_gla"], name="mm_branch")
    merged = _merge_fwd(z, bf, bg, sp["bm"], "merge_fwd")
    mo = _mm(merged, full["w_out"], out_dtype=F32, name="mm_out")
    h2, n2 = _resnorm_fwd(h1, mo, sp["ffn2_norm"], 1.0, "resnorm_fwd_one")
    gu2 = _mm(n2, w_gu2, name="mm_gu")
    a2 = _act_fwd(gu2, "act_fwd")
    f2 = _mm(a2, full["ffn2_w_down"], out_dtype=F32, name="mm_down")
    h3, n4 = _resnorm_fwd(h2, f2, sp["ple_norm"], 0.5, "resnorm_fwd_half")
    pgl = _mm(n4, full["w_ple_gate"], name="mm_pg")
    pb = p.astype(BF16)
    pp = _mm(pb, full["w_ple_proj"], name="mm_pp")

    dh3, dpgl, dpp, loss, d_final = _head(h3, pgl, pp, tgt, sp["final_norm"], "head")
    ds_ = {"final_norm": d_final}
    names, sums = pair_sums({"w_ple_gate": _mm(n4, dpgl, ta=True, name="mm_dw_sq"), "w_ple_proj": _mm(pb, dpp, ta=True, name="mm_dw_pp")})
    dn4, got = _mm(dpgl, full["w_ple_gate"], tb=True, out_dtype=F32, name="mm_dx_sq_f32_exchange", comm=_exchange_comm(sums))
    exchanged(names, sums, got)
    dh3, df2, ds_["ple_norm"] = _norm_bwd(h3, [dn4], dh3, sp["ple_norm"], 0.5, "norm_bwd_1")

    def ffn_bwd(n, gu, a, df, wgu, wd, which):
        ff = wd.shape[0]
        names, sums = pair_sums({which + "_w_down": _mm(a, df, ta=True, name="mm_dw_down")})
        da, got = _mm(df, wd, tb=True, name="mm_dx_down_exchange", comm=_exchange_comm(sums))
        exchanged(names, sums, got)
        dgu = _act_bwd(gu, da, "act_bwd")
        dwgu = _mm(n, dgu, ta=True, name="mm_dw_gu")
        names, sums = pair_sums({which + "_w_gate": dwgu[:, :ff], which + "_w_up": dwgu[:, ff:]})
        dn, got = _mm(dgu, wgu, tb=True, out_dtype=F32, name="mm_dx_gu_exchange", comm=_exchange_comm(sums))
        exchanged(names, sums, got)
        return dn

    dn2 = ffn_bwd(n2, gu2, a2, df2, w_gu2, full["ffn2_w_down"], "ffn2")
    dh2, dmix, ds_["ffn2_norm"] = _norm_bwd(h2, [dn2], dh3, sp["ffn2_norm"], 1.0, "norm_bwd_1")

    dw_out = _mm(merged, dmix, ta=True, name="mm_dw_sq")
    dmerged = _mm(dmix, full["w_out"], tb=True, name="mm_dx_sq")
    dgl, dbf, dbg, ds_["bm"] = _merge_bwd(z, bf, bg, sp["bm"], dmerged, "merge_bwd")
    mix_names, mix_sums = pair_sums({"w_out": dw_out, "w_branch_fox": _mm(y_fox, dbf, ta=True, name="mm_dw_branch"),
                                     "w_branch_gla": _mm(y_gla, dbg, ta=True, name="mm_dw_branch")})
    dy_fox = _mm(dbf, full["w_branch_fox"], tb=True, name="mm_dx_branch")
    dy_gla = _mm(dbg, full["w_branch_gla"], tb=True, name="mm_dx_branch")

    do_gla, dgr, ds_["ghn"] = _gla_out_bwd(o_gla, z, sp["ghn"], dy_gla, "gla_out_bwd")
    dgq, dgk, dgv, dla = _gla_bwd(z, la, sprev, do_gla, "gla_bwd")
    dot = _blocked_t(dy_fox, tq)
    dqt, delta, df_query = _fox_bwd_q(z, qt, kt, ot, dot, lse, frep, "fox_bwd_q")
    dfq = _unblocked_t(dqt)
    dfk, dfv, df_key = _fox_bwd_kv(z, qt, dy_fox, dot, lse, delta, frep, "fox_bwd_kv")
    df = df_query.reshape(FOX_HEADS, t) - df_key.reshape(FOX_HEADS, t)
    dfp = jnp.pad(df.T, ((0, 0), (0, SMALL_W - FOX_HEADS)))
    dlf = _cumsum_tokens(dfp, True, "cumsum_bwd")
    dsm, ds_["fb"], dgup, ds_["gb"] = _small_bwd(s, sp["fb"], gup, sp["gb"], dlf, dla, "small_bwd")
    dz = jnp.concatenate([dfq, dfk, dfv, dgq, dgk, dgv, dgr, dgl], axis=1)
    dw_big, got = _mm(u, dz, ta=True, name="mm_dw_in_exchange", comm=_exchange_comm(mix_sums))
    exchanged(mix_names, mix_sums, got)
    dw_sm = _mm(u, dsm, ta=True, out_dtype=F32, name="mm_dw_in_small").astype(BF16)
    dw_in = jnp.concatenate([dw_big[:, Z_FQ:Z_GQ], dw_sm[:, :FOX_HEADS], dw_big[:, Z_GQ:Z_GL], dw_sm[:, FOX_HEADS:FOX_HEADS + GLA_RANK]], axis=1)
    names, sums = pair_sums({"w_in": dw_in, "w_merge_gate": dw_big[:, Z_GL:], "gla_gate_up": dgup[FOX_HEADS:FOX_HEADS + GLA_RANK]})
    du1, got = _mm(dz, w_big, tb=True, out_dtype=F32, name="mm_dx_in_exchange", comm=_exchange_comm(sums))
    exchanged(names, sums, got)
    du2 = _mm(dsm, w_sm, tb=True, out_dtype=F32, name="mm_dx_in_small")
    dh1, df1, ds_["mix_norm"] = _norm_bwd(h1, [du1, du2], dh2, sp["mix_norm"], 0.5, "norm_bwd_2")

    dn1 = ffn_bwd(n1, gu1, a1, df1, w_gu1, full["ffn1_w_down"], "ffn1")
    grad_x, _, ds_["ffn1_norm"] = _norm_bwd(x, [dn1], dh1, sp["ffn1_norm"], 1.0, "norm_bwd_1")
    return loss, grad_x, reduced, ds_


def _half_rows(ref, which):
    r2 = ref.shape[0] // 2
    return ref.at[pl.ds(pl.multiple_of(which * r2, r2), r2)]


class _Comm:
    def __init__(self, ins, out_shape, sems, start, finish):
        self.ins, self.out_shape, self.sems, self.start, self.finish = ins, out_shape, sems, start, finish


def _run_comm(comm, name):
    n_in, n_out = len(comm.ins), len(comm.out_shape)

    def body(*refs):
        parts = refs[:n_in], refs[n_in:n_in + n_out], refs[n_in + n_out:]
        comm.start(*parts)
        comm.finish(*parts)

    return pl.pallas_call(
        body, name=name, in_specs=[ANY] * n_in, out_specs=[ANY] * n_out, out_shape=comm.out_shape,
        scratch_shapes=comm.sems, compiler_params=pltpu.CompilerParams(has_side_effects=True),
    )(*comm.ins)


def _hosted(body, comm, *, name, grid, in_specs, out_specs, out_shape, scratch_shapes, semantics, args):
    if comm is None:
        res = pl.pallas_call(
            body, name=name, grid=grid, in_specs=in_specs, out_specs=out_specs, out_shape=out_shape, scratch_shapes=scratch_shapes,
            compiler_params=pltpu.CompilerParams(dimension_semantics=semantics, vmem_limit_bytes=VMEM_LIMIT),
        )(*args)
        return res, None
    ni, no, ns = len(in_specs), len(out_shape), len(scratch_shapes)
    ci, co = len(comm.ins), len(comm.out_shape)

    def wrapped(*refs):
        h_in, c_in = refs[:ni], refs[ni:ni + ci]
        h_out, c_out = refs[ni + ci:ni + ci + no], refs[ni + ci + no:ni + ci + no + co]
        h_scr, c_sem = refs[ni + ci + no + co:ni + ci + no + co + ns], refs[ni + ci + no + co + ns:]
        ids = [pl.program_id(axis) for axis in range(len(grid))]
        first = functools.reduce(jnp.logical_and, [i == 0 for i in ids])
        last = functools.reduce(jnp.logical_and, [i == g - 1 for i, g in zip(ids, grid)])

        @pl.when(first)
        def _():
            comm.start(c_in, c_out, c_sem)

        body(*h_in, *h_out, *h_scr)

        @pl.when(last)
        def _():
            comm.finish(c_in, c_out, c_sem)

    res = pl.pallas_call(
        wrapped, name=name, grid=grid, in_specs=list(in_specs) + [ANY] * ci, out_specs=list(out_specs) + [ANY] * co,
        out_shape=list(out_shape) + list(comm.out_shape), scratch_shapes=list(scratch_shapes) + list(comm.sems),
        compiler_params=pltpu.CompilerParams(dimension_semantics=("arbitrary",) * len(grid), vmem_limit_bytes=VMEM_LIMIT, has_side_effects=True),
    )(*args, *comm.ins)
    return res[:no], res[no:]


def _ag_comm(shards, out_shape, places):
    n = len(shards)

    def copies(ins, outs, sems):
        ici_send, ici_recv, d2d_send, d2d_recv = sems
        x, y, c = lax.axis_index("x"), lax.axis_index("y"), lax.axis_index("c")
        chips = [(1 - x, y), (x, 1 - y), (1 - x, 1 - y)]
        slot = lambda chip: 2 * chip[0] + chip[1]

        def window(wi, origin, half):
            out, row_step, col_base, col_step = places[wi]
            r, cc = shards[wi].shape
            rows = pl.ds(pl.multiple_of(slot(origin) * row_step + half * (r // 2), r // 2), r // 2)
            cols = pl.ds(pl.multiple_of(col_base + slot(origin) * col_step, HEAD_DIM), cc) if col_step else pl.ds(col_base, cc)
            return outs[out].at[rows, cols]

        def over_ici(wi, j, origin):
            return pltpu.make_async_remote_copy(
                src_ref=_half_rows(ins[wi], c), dst_ref=window(wi, origin, c),
                send_sem=ici_send.at[3 * wi + j], recv_sem=ici_recv.at[3 * wi + j],
                device_id=(chips[j][0], chips[j][1], c), device_id_type=MESH)

        def over_d2d(wi, j, half):
            place = window(wi, chips[j], half)
            return pltpu.make_async_remote_copy(
                src_ref=place, dst_ref=place, send_sem=d2d_send.at[3 * wi + j], recv_sem=d2d_recv.at[3 * wi + j],
                device_id=(x, y, 1 - c), device_id_type=MESH)

        return over_ici, over_d2d, (x, y), chips, c

    def chunk_rows(wi):
        r, cc = shards[wi].shape
        item = shards[wi].dtype.itemsize
        return _pick(r, max(32 // item, BOUNCE_BYTES // (cc * item)), 32 // item)

    def start(ins, outs, scratch):
        over_ici, _, me, _, _ = copies(ins, outs, scratch[:4])
        for wi in range(n):
            for j in range(3):
                over_ici(wi, j, me).start()
        loc_sems = scratch[4]
        for wi in range(n):
            out, row_step, col_base, col_step = places[wi]
            r, cc = shards[wi].shape
            rc = chunk_rows(wi)
            buf = scratch[5 + wi]
            slot = 2 * me[0] + me[1]
            cols = pl.ds(pl.multiple_of(col_base + slot * col_step, HEAD_DIM), cc) if col_step else pl.ds(col_base, cc)

            def load(k):
                return pltpu.make_async_copy(ins[wi].at[pl.ds(k * rc, rc)], buf.at[k % 2], loc_sems.at[2 * wi])

            def store(k):
                rows = pl.ds(pl.multiple_of(slot * row_step + k * rc, rc), rc)
                return pltpu.make_async_copy(buf.at[k % 2], outs[out].at[rows, cols], loc_sems.at[2 * wi + 1])

            load(0).start()
            for k in range(r // rc):
                load(k).wait()
                if k + 1 < r // rc:
                    load(k + 1).start()
                store(k).start()
                store(k).wait()

    def finish(ins, outs, scratch):
        over_ici, over_d2d, me, chips, c = copies(ins, outs, scratch[:4])
        for wi in range(n):
            for j in range(3):
                over_ici(wi, j, chips[j]).wait_recv()
                over_d2d(wi, j, c).start()
        for wi in range(n):
            for j in range(3):
                over_d2d(wi, j, 1 - c).wait_recv()
        for wi in range(n):
            for j in range(3):
                over_ici(wi, j, me).wait_send()
                over_d2d(wi, j, c).wait_send()

    bounce = [pltpu.VMEM((min(2, s.shape[0] // chunk_rows(wi)), chunk_rows(wi), s.shape[1]), s.dtype) for wi, s in enumerate(shards)]
    return _Comm(list(shards), list(out_shape), [pltpu.SemaphoreType.DMA((3 * n,))] * 4 + [pltpu.SemaphoreType.DMA((2 * n,))] + bounce,
                 start, finish)


def _pair_swap(parts, name):
    n = len(parts)

    def body(*refs):
        ins, outs = refs[:n], refs[n:2 * n]
        send_sems, recv_sems = refs[2 * n:]
        x, y, c = lax.axis_index("x"), lax.axis_index("y"), lax.axis_index("c")

        def swap(wi):
            r2 = parts[wi].shape[1] // 2
            return pltpu.make_async_remote_copy(
                src_ref=ins[wi].at[:, pl.ds(pl.multiple_of((1 - c) * r2, r2), r2)], dst_ref=outs[wi],
                send_sem=send_sems.at[wi], recv_sem=recv_sems.at[wi], device_id=(x, y, 1 - c), device_id_type=MESH)

        copies = [swap(wi) for wi in range(n)]
        for cp in copies:
            cp.start()
        for cp in copies:
            cp.wait()

    return pl.pallas_call(
        body, name=name, in_specs=[ANY] * n, out_specs=[ANY] * n,
        out_shape=[jax.ShapeDtypeStruct((s.shape[0], s.shape[1] // 2, s.shape[2]), s.dtype) for s in parts],
        scratch_shapes=[pltpu.SemaphoreType.DMA((n,))] * 2, compiler_params=pltpu.CompilerParams(has_side_effects=True),
    )(*parts)


def _row_tile(r, c, budget=1 << 19):
    return r if r <= 8 else _pick(r, max(8, budget // c), 8)


def _sum_half(parts, other, cidx, name):
    nl, r, cc = parts.shape
    r2 = r // 2
    tr = _row_tile(r2, cc)

    def body(c_ref, p_ref, q_ref, o_ref):
        o_ref[...] = (_f(p_ref[...]) + _f(q_ref[...])).astype(o_ref.dtype)

    return pl.pallas_call(
        body, name=name, out_shape=jax.ShapeDtypeStruct((nl, r2, cc), parts.dtype),
        grid_spec=pltpu.PrefetchScalarGridSpec(
            num_scalar_prefetch=1, grid=(nl, r2 // tr),
            in_specs=[pl.BlockSpec((None, None, tr, cc), lambda l, i, c_ref: (l, c_ref[0], i, 0)),
                      pl.BlockSpec((None, tr, cc), lambda l, i, c_ref: (l, i, 0))],
            out_specs=pl.BlockSpec((None, tr, cc), lambda l, i, c_ref: (l, i, 0))),
        compiler_params=pltpu.CompilerParams(dimension_semantics=("parallel", "parallel"), vmem_limit_bytes=VMEM_LIMIT),
    )(cidx, parts.reshape(nl, 2, r2, cc), other)


def _exchange_comm(sums):
    n = len(sums)

    def copies(ins, outs, sems):
        send_sems, recv_sems = sems
        x, y, c = lax.axis_index("x"), lax.axis_index("y"), lax.axis_index("c")
        chips = [(1 - x, y), (x, 1 - y), (1 - x, 1 - y)]
        slot = lambda chip: 2 * chip[0] + chip[1]

        def remote(wi, j, origin):
            return pltpu.make_async_remote_copy(
                src_ref=ins[wi].at[slot(chips[j]) if sums[wi].shape[0] == 4 else 0], dst_ref=outs[wi].at[slot(origin)],
                send_sem=send_sems.at[3 * wi + j], recv_sem=recv_sems.at[3 * wi + j],
                device_id=(chips[j][0], chips[j][1], c), device_id_type=MESH)

        return remote, (x, y), chips

    def start(ins, outs, sems):
        remote, me, _ = copies(ins, outs, sems)
        for wi in range(n):
            for j in range(3):
                remote(wi, j, me).start()

    def finish(ins, outs, sems):
        remote, me, chips = copies(ins, outs, sems)
        for wi in range(n):
            for j in range(3):
                remote(wi, j, chips[j]).wait_recv()
        for wi in range(n):
            for j in range(3):
                remote(wi, j, me).wait_send()

    return _Comm(list(sums), [jax.ShapeDtypeStruct((4,) + s.shape[1:], s.dtype) for s in sums],
                 [pltpu.SemaphoreType.DMA((3 * n,))] * 2, start, finish)


def _sum_chips(got, own, chip, name):
    _, r2, cc = got.shape
    tr = _row_tile(r2, cc)
    per_chip = own.shape[0] == 4

    def body(chip_ref, g_ref, own_ref, o_ref):
        term = lambda k: jnp.where(chip_ref[0] == k, _f(own_ref[...]), _f(g_ref[k]))
        o_ref[...] = ((term(0) + term(1)) + term(2)) + term(3)

    return pl.pallas_call(
        body, name=name, out_shape=jax.ShapeDtypeStruct((r2, cc), F32),
        grid_spec=pltpu.PrefetchScalarGridSpec(
            num_scalar_prefetch=1, grid=(r2 // tr,),
            in_specs=[pl.BlockSpec((4, tr, cc), lambda i, chip_ref: (0, i, 0)),
                      pl.BlockSpec((None, tr, cc), lambda i, chip_ref: (chip_ref[0] if per_chip else 0, i, 0))],
            out_specs=pl.BlockSpec((tr, cc), lambda i, chip_ref: (i, 0))),
        compiler_params=pltpu.CompilerParams(dimension_semantics=("parallel",), vmem_limit_bytes=VMEM_LIMIT),
    )(chip, got, own)


def _pair_gather(halves, name):
    n = len(halves)

    def body(*refs):
        ins, outs = refs[:n], refs[n:2 * n]
        send_sems, recv_sems = refs[2 * n:]
        x, y, c = lax.axis_index("x"), lax.axis_index("y"), lax.axis_index("c")
        copies = [pltpu.make_async_remote_copy(
            src_ref=ins[wi], dst_ref=outs[wi], send_sem=send_sems.at[wi], recv_sem=recv_sems.at[wi],
            device_id=(x, y, 1 - c), device_id_type=MESH) for wi in range(n)]
        for cp in copies:
            cp.start()
        for cp in copies:
            cp.wait()

    return pl.pallas_call(
        body, name=name, in_specs=[ANY] * n, out_specs=[ANY] * n,
        out_shape=[jax.ShapeDtypeStruct(s.shape, s.dtype) for s in halves],
        scratch_shapes=[pltpu.SemaphoreType.DMA((n,))] * 2, compiler_params=pltpu.CompilerParams(has_side_effects=True),
    )(*halves)


def _adamw(mine, other, cidx, w, m, v, name):
    r, c = w.shape
    tr = _row_tile(r // 2, c, 1 << 18)
    nh = (r // 2) // tr

    def body(c_ref, mine_ref, other_ref, w_ref, m_ref, v_ref, g_ref, d_ref, nm_ref, nv_ref):
        g = jnp.where(pl.program_id(0) // nh == c_ref[0], mine_ref[...], other_ref[...])
        g_ref[...] = g
        m_new = ADAM_B1 * m_ref[...] + (1.0 - ADAM_B1) * g
        v_new = ADAM_B2 * v_ref[...] + (1.0 - ADAM_B2) * jnp.square(g)
        m_hat = m_new / (1.0 - ADAM_B1 ** ADAM_STEP)
        v_hat = v_new / (1.0 - ADAM_B2 ** ADAM_STEP)
        d_ref[...] = -ADAM_LR * (m_hat / (jnp.sqrt(v_hat) + ADAM_EPS) + ADAM_WD * w_ref[...])
        nm_ref[...] = m_new
        nv_ref[...] = v_new

    blk = pl.BlockSpec((tr, c), lambda i, c_ref: (i, 0))
    mine_spec = pl.BlockSpec((tr, c), lambda i, c_ref: (jnp.where(i // nh == c_ref[0], i % nh, 0), 0))
    other_spec = pl.BlockSpec((tr, c), lambda i, c_ref: (jnp.where(i // nh == c_ref[0], 0, i % nh), 0))
    return pl.pallas_call(
        body, name=name, out_shape=[jax.ShapeDtypeStruct((r, c), F32)] * 4,
        grid_spec=pltpu.PrefetchScalarGridSpec(
            num_scalar_prefetch=1, grid=(r // tr,), in_specs=[mine_spec, other_spec, blk, blk, blk], out_specs=[blk] * 4),
        compiler_params=pltpu.CompilerParams(dimension_semantics=("arbitrary",), vmem_limit_bytes=VMEM_LIMIT),
    )(cidx, mine, other, w, m, v)


BIG = ["ffn1_w_gate", "ffn1_w_up", "ffn1_w_down", "w_in", "gla_gate_up", "w_branch_fox", "w_branch_gla", "w_merge_gate", "w_out",
       "ffn2_w_gate", "ffn2_w_up", "ffn2_w_down", "w_ple_proj", "w_ple_gate"]
ROW_SHARDED = ("ffn1_w_down", "w_out", "ffn2_w_down", "w_ple_gate")
FUSED = {"ffn1_w_gate": ("gu1", 0, 2), "ffn1_w_up": ("gu1", 1, 2), "ffn2_w_gate": ("gu2", 0, 2), "ffn2_w_up": ("gu2", 1, 2)}
SMALL = ["ffn1_norm", "mix_norm", "fox_forget_bias", "gla_gate_bias", "gla_head_norm", "b_merge_gate", "ffn2_norm", "ple_norm", "final_norm"]
NAMES = ["ffn1_norm", "ffn1_w_gate", "ffn1_w_up", "ffn1_w_down", "mix_norm", "w_in", "fox_forget_bias", "gla_gate_up", "gla_gate_bias",
         "gla_head_norm", "w_branch_fox", "w_branch_gla", "w_merge_gate", "b_merge_gate", "w_out", "ffn2_norm", "ffn2_w_gate", "ffn2_w_up",
         "ffn2_w_down", "ple_norm", "w_ple_proj", "w_ple_gate", "final_norm"]
W_IN_COLS = (FOX_W, FOX_W, FOX_W, FOX_HEADS, GLA_KW, GLA_KW, GLA_VW, GLA_VW, GLA_RANK)
SMALL_ROWS, SMALL_COLS = 16, 1024


def _shard_parts(full, name):
    if name in ROW_SHARDED:
        return full.reshape(4, full.shape[0] // 4, full.shape[1])
    return jnp.transpose(full.reshape(full.shape[0], 4, full.shape[1] // 4), (1, 0, 2))


def _in_layout(w_in, w_merge_gate):
    offs = [0]
    for cw in W_IN_COLS:
        offs.append(offs[-1] + cw)
    col = lambda i: w_in[:, offs[i]:offs[i + 1]]
    big = jnp.concatenate([col(0), col(1), col(2), col(4), col(5), col(6), col(7), w_merge_gate], axis=1)
    sm = jnp.concatenate([col(3), col(8), jnp.zeros((D_MODEL, SMALL_W - FOX_HEADS - GLA_RANK), BF16)], axis=1)
    return big, sm


def _pad_lanes(a, width):
    return jnp.pad(a, ((0, 0), (0, width - a.shape[1])))


def kernel(x, p, ffn1_norm, ffn1_w_gate, ffn1_w_up, ffn1_w_down, mix_norm, w_in, fox_forget_bias, gla_gate_up, gla_gate_bias, gla_head_norm, w_branch_fox, w_branch_gla, w_merge_gate, b_merge_gate, w_out, ffn2_norm, ffn2_w_gate, ffn2_w_up, ffn2_w_down, ple_norm, w_ple_proj, w_ple_gate, final_norm, loss_target, m_ffn1_norm, m_ffn1_w_gate, m_ffn1_w_up, m_ffn1_w_down, m_mix_norm, m_w_in, m_fox_forget_bias, m_gla_gate_up, m_gla_gate_bias, m_gla_head_norm, m_w_branch_fox, m_w_branch_gla, m_w_merge_gate, m_b_merge_gate, m_w_out, m_ffn2_norm, m_ffn2_w_gate, m_ffn2_w_up, m_ffn2_w_down, m_ple_norm, m_w_ple_proj, m_w_ple_gate, m_final_norm, v_ffn1_norm, v_ffn1_w_gate, v_ffn1_w_up, v_ffn1_w_down, v_mix_norm, v_w_in, v_fox_forget_bias, v_gla_gate_up, v_gla_gate_bias, v_gla_head_norm, v_w_branch_fox, v_w_branch_gla, v_w_merge_gate, v_b_merge_gate, v_w_out, v_ffn2_norm, v_ffn2_w_gate, v_ffn2_w_up, v_ffn2_w_down, v_ple_norm, v_w_ple_proj, v_w_ple_gate, v_final_norm):
    args = dict(locals())
    wts = {n: args[n] for n in NAMES}
    mom = {n: args["m_" + n] for n in NAMES}
    var = {n: args["v_" + n] for n in NAMES}
    two_d = lambda a: a.reshape(-1, a.shape[-1])

    wire = lambda n: F32 if n == "gla_gate_up" else BF16
    cidx = lax.axis_index("c").astype(jnp.int32).reshape(1)
    chip = (2 * lax.axis_index("x") + lax.axis_index("y")).astype(jnp.int32)
    shards = {n: two_d(wts[n]).astype(wire(n)) for n in BIG}
    sp = {
        "ffn1_norm": two_d(ffn1_norm), "mix_norm": two_d(mix_norm), "fb": _pad_lanes(two_d(fox_forget_bias), SMALL_W),
        "gb": two_d(gla_gate_bias), "ghn": two_d(gla_head_norm), "bm": two_d(b_merge_gate), "ffn2_norm": two_d(ffn2_norm),
        "ple_norm": two_d(ple_norm), "final_norm": two_d(final_norm),
    }

    loss, grad_x, reduced, ds_ = _local_step(x[0], p[0, 0], loss_target[0], shards, sp, cidx, chip)

    small_g = {"ffn1_norm": ds_["ffn1_norm"], "mix_norm": ds_["mix_norm"], "fox_forget_bias": ds_["fb"][:, :FOX_HEADS],
               "gla_gate_bias": ds_["gb"], "gla_head_norm": ds_["ghn"], "b_merge_gate": ds_["bm"], "ffn2_norm": ds_["ffn2_norm"],
               "ple_norm": ds_["ple_norm"], "final_norm": ds_["final_norm"]}
    small_w = sum(two_d(wts[n]).shape[1] for n in SMALL)
    assert small_w <= SMALL_ROWS * SMALL_COLS
    packed = lambda d: _pad_lanes(jnp.concatenate([two_d(d[n]) for n in SMALL], axis=1), SMALL_ROWS * SMALL_COLS).reshape(SMALL_ROWS, SMALL_COLS)
    parts = [packed(small_g)[None]]
    pair_sums = [_sum_half(a, b, cidx, "sum_half") for a, b in zip(parts, _pair_swap(parts, "pair_swap"))]
    reduced["small"] = (_run_comm(_exchange_comm(pair_sums), "chip_exchange")[0], pair_sums[0])
    mine = [_sum_chips(*reduced[n], chip.reshape(1), "sum_chips") for n in BIG + ["small"]]
    other = _pair_gather(mine, "pair_gather")

    out = {}
    for n, a, b in zip(BIG, mine[:-1], other[:-1]):
        out[n] = [r.reshape(wts[n].shape) for r in _adamw(a, b, cidx, two_d(wts[n]), two_d(mom[n]), two_d(var[n]), "adamw_" + n)]
    small_out = [r.reshape(1, SMALL_ROWS * SMALL_COLS) for r in _adamw(mine[-1], other[-1], cidx, packed(wts), packed(mom), packed(var), "adamw_small")]
    off = 0
    for n in SMALL:
        cw = two_d(wts[n]).shape[1]
        out[n] = [r[:, off:off + cw].reshape(wts[n].shape) for r in small_out]
        off += cw

    total = lax.psum(loss[0, 0], ("x", "y", "c"))
    return (total, grad_x[None], *[out[n][0] for n in NAMES], *[out[n][1] for n in NAMES],
            *[out[n][2] for n in NAMES], *[out[n][3] for n in NAMES])
```

```python
import functools

import jax
import jax.numpy as jnp
from jax import lax
from jax.experimental import pallas as pl
from jax.experimental.pallas import tpu as pltpu

F32 = jnp.float32
BF16 = jnp.bfloat16
MESH = pl.DeviceIdType.MESH
ANY = pl.BlockSpec(memory_space=pl.ANY)

D_MODEL = 2048
FOX_HEADS = 8
HEAD_DIM = 128
GLA_HEADS = 4
GLA_VAL_DIM = 256
GLA_RANK = 16
GLA_TAU = 16.0
CHUNK = 64
EPS = 1e-6
FOX_W = FOX_HEADS * HEAD_DIM
GLA_KW = GLA_HEADS * HEAD_DIM
GLA_VW = GLA_HEADS * GLA_VAL_DIM
Z_FQ, Z_FK, Z_FV, Z_GQ, Z_GK, Z_GV, Z_GR, Z_GL = 0, 1024, 2048, 3072, 3584, 4096, 5120, 6144
Z_W = Z_GL + 2 * D_MODEL
SMALL_W = 128
NEG = -1e30

ADAM_LR, ADAM_B1, ADAM_B2, ADAM_EPS, ADAM_WD, ADAM_STEP = 0.001, 0.9, 0.999, 1e-08, 0.01, 10

VMEM_LIMIT = 56 * 1024 * 1024
BOUNCE_BYTES = 2 * 1024 * 1024


def _pick(n, target, mult=128):
    if n <= target:
        return n
    best = None
    for d in range(mult, target + 1, mult):
        if n % d == 0:
            best = d
    assert best is not None, (n, target)
    return best


def _mm(a, b, *, ta=False, tb=False, out_dtype=BF16, name, comm=None):
    m, k = (a.shape[1], a.shape[0]) if ta else a.shape
    n = b.shape[0] if tb else b.shape[1]
    assert (b.shape[1] if tb else b.shape[0]) == k
    bk = _pick(k, 2048)
    nk = k // bk
    bm, bn = _pick(m, 1024), _pick(n, 1024 if nk > 1 else 512)
    dims = (((0 if ta else 1,), (1 if tb else 0,)), ((), ()))

    def body(a_ref, b_ref, o_ref, acc_ref):
        part = lax.dot_general(a_ref[...], b_ref[...], dims, preferred_element_type=F32)
        if nk == 1:
            o_ref[...] = part.astype(o_ref.dtype)
            return
        kk = pl.program_id(2)

        @pl.when(kk == 0)
        def _():
            acc_ref[...] = part

        @pl.when(kk > 0)
        def _():
            acc_ref[...] += part

        @pl.when(kk == nk - 1)
        def _():
            o_ref[...] = acc_ref[...].astype(o_ref.dtype)

    a_spec = pl.BlockSpec((bk, bm), lambda i, j, kk: (kk, i)) if ta else pl.BlockSpec((bm, bk), lambda i, j, kk: (i, kk))
    b_spec = pl.BlockSpec((bn, bk), lambda i, j, kk: (j, kk)) if tb else pl.BlockSpec((bk, bn), lambda i, j, kk: (kk, j))
    (out,), travelled = _hosted(
        body, comm, name=name, grid=(m // bm, n // bn, nk),
        in_specs=[a_spec, b_spec], out_specs=[pl.BlockSpec((bm, bn), lambda i, j, kk: (i, j))],
        out_shape=[jax.ShapeDtypeStruct((m, n), out_dtype)], scratch_shapes=[pltpu.VMEM((bm, bn), F32)],
        semantics=("parallel", "parallel", "arbitrary"), args=(a, b))
    return out if comm is None else (out, travelled)


def _rowwise(fn, tiled, bcast, outs, reds=(), *, tt, name):
    t = tiled[0][0].shape[0]
    tt = min(tt, t)
    nin, nout = len(tiled) + len(bcast), len(outs)
    splits = [s[3] for s in tiled] + [s[1] for s in bcast]

    def store(ref, val, acc):
        off = 0
        for piece in val if isinstance(val, (tuple, list)) else (val,):
            w = piece.shape[-1]
            if acc:
                ref[:, off:off + w] += piece.astype(ref.dtype)
            else:
                ref[:, off:off + w] = piece.astype(ref.dtype)
            off += w
        assert off == ref.shape[-1], (name, off, ref.shape)

    def body(*refs):
        args = []
        for ref, sp in zip(refs[:nin], splits):
            if sp is None:
                args.append(ref[...])
            else:
                off = 0
                for w in sp:
                    args.append(ref[:, off:off + w])
                    off += w
        res = fn(*args)
        res = res if isinstance(res, (tuple, list)) else (res,)
        assert len(res) == nout + len(reds), (name, len(res))
        for ref, val in zip(refs[nin:nin + nout], res[:nout]):
            store(ref, val, False)
        if reds:
            @pl.when(pl.program_id(0) == 0)
            def _():
                for ref in refs[nin + nout:]:
                    ref[...] = jnp.zeros(ref.shape, ref.dtype)
            for ref, val in zip(refs[nin + nout:], res[nout:]):
                store(ref, val, True)

    in_specs = [pl.BlockSpec((tt, w), functools.partial(lambda i, cb: (i, cb), cb=cb)) for (_, w, cb, _) in tiled]
    in_specs += [pl.BlockSpec(arr.shape, lambda i: (0, 0)) for (arr, _) in bcast]
    out_specs = [pl.BlockSpec((tt, w), lambda i: (i, 0)) for (w, _) in outs]
    out_specs += [pl.BlockSpec((r, w), lambda i: (0, 0)) for (r, w) in reds]
    out_shape = [jax.ShapeDtypeStruct((t, w), dt) for (w, dt) in outs] + [jax.ShapeDtypeStruct((r, w), F32) for (r, w) in reds]
    return pl.pallas_call(
        body, name=name, grid=(t // tt,), in_specs=in_specs, out_specs=out_specs, out_shape=out_shape,
        compiler_params=pltpu.CompilerParams(dimension_semantics=("arbitrary" if reds else "parallel",), vmem_limit_bytes=VMEM_LIMIT),
    )(*[s[0] for s in tiled], *[s[0] for s in bcast])


def _full(arr):
    return (arr, arr.shape[1], 0, None)


def _f(x):
    return x.astype(F32)


def _rms(x, g):
    return x * lax.rsqrt(jnp.mean(x * x, axis=-1, keepdims=True) + EPS) * g


def _log_sigmoid(x):
    return jnp.minimum(x, 0.0) - jnp.log1p(jnp.exp(-jnp.abs(x)))


def _silu(x):
    return x * jax.nn.sigmoid(x)


def _norm_fwd(x, g, name):
    return _rowwise(lambda xb, gb: _rms(_f(xb), gb), [_full(x)], [(g, None)], [(x.shape[1], BF16)], tt=256, name=name)[0]


def _resnorm_fwd(res, branch, g, coef, name):
    def fn(rb, bb, gb):
        h = rb + coef * _f(bb)
        return h, _rms(h, gb)
    d = res.shape[1]
    return _rowwise(fn, [_full(res), _full(branch)], [(g, None)], [(d, F32), (d, BF16)], tt=256, name=name)


def _norm_bwd(h, dns, dres, g, coef, name):
    nd = len(dns)

    def fn(hb, *rest):
        dn = _f(rest[0])
        for extra in rest[1:nd]:
            dn = dn + _f(extra)
        dr, gb = rest[nd], rest[nd + 1]
        _, vjp = jax.vjp(_rms, hb, gb)
        dh, dg = vjp(dn)
        dh = dh + dr
        return dh, coef * dh, dg
    d = h.shape[1]
    return _rowwise(fn, [_full(h)] + [_full(x) for x in dns] + [_full(dres)], [(g, None)],
                    [(d, F32), (d, BF16)], [(1, d)], tt=256, name=name)


def _act_fwd(gu, name):
    ff = gu.shape[1] // 2
    return _rowwise(lambda gb, ub: _silu(_f(gb)) * _f(ub), [(gu, 2 * ff, 0, (ff, ff))], [], [(ff, BF16)], tt=256, name=name)[0]


def _act_bwd(gu, da, name):
    ff = gu.shape[1] // 2

    def fn(gb, ub, dab):
        _, vjp = jax.vjp(lambda p, q: _silu(p) * q, _f(gb), _f(ub))
        return (vjp(_f(dab)),)
    return _rowwise(fn, [(gu, 2 * ff, 0, (ff, ff)), _full(da)], [], [(2 * ff, BF16)], tt=128, name=name)[0]


def _merge(glf, glg, bf, bg, bmf, bmg):
    return jax.nn.sigmoid(_f(glf) + bmf) * _f(bf) + jax.nn.sigmoid(_f(glg) + bmg) * _f(bg)


def _merge_fwd(z, bf, bg, bm, name):
    d = D_MODEL
    return _rowwise(_merge, [(z, d, Z_GL // d, None), (z, d, Z_GL // d + 1, None), _full(bf), _full(bg)], [(bm, (d, d))],
                    [(d, BF16)], tt=256, name=name)[0]


def _merge_bwd(z, bf, bg, bm, dm, name):
    d = D_MODEL

    def fn(glf, glg, bfb, bgb, dmb, bmf, bmg):
        _, vjp = jax.vjp(_merge, _f(glf), _f(glg), _f(bfb), _f(bgb), bmf, bmg)
        dglf, dglg, dbf, dbg, dbmf, dbmg = vjp(_f(dmb))
        return (dglf, dglg), dbf, dbg, (dbmf, dbmg)
    return _rowwise(fn, [(z, d, Z_GL // d, None), (z, d, Z_GL // d + 1, None), _full(bf), _full(bg), _full(dm)], [(bm, (d, d))],
                    [(2 * d, BF16), (d, BF16), (d, BF16)], [(1, 2 * d)], tt=128, name=name)


def _gla_out(o, gr, g):
    return _rms(o, g) * _silu(_f(gr))


_PER_HEAD = (GLA_VAL_DIM,) * GLA_HEADS


def _gla_out_fwd(o, z, g, name):
    nh = GLA_HEADS

    def fn(*blocks):
        return (tuple(_gla_out(blocks[h], blocks[nh + h], blocks[2 * nh]) for h in range(nh)),)
    return _rowwise(fn, [(o, GLA_VW, 0, _PER_HEAD), (z, GLA_VW, Z_GR // GLA_VW, _PER_HEAD)], [(g, None)], [(GLA_VW, BF16)], tt=256, name=name)[0]


def _gla_out_bwd(o, z, g, dy, name):
    nh = GLA_HEADS

    def fn(*blocks):
        gb = blocks[3 * nh]
        grads = []
        for h in range(nh):
            _, vjp = jax.vjp(_gla_out, blocks[h], _f(blocks[nh + h]), gb)
            grads.append(vjp(_f(blocks[2 * nh + h])))
        dg = grads[0][2]
        for h in range(1, nh):
            dg = dg + grads[h][2]
        return tuple(gr[0] for gr in grads), tuple(gr[1] for gr in grads), dg
    return _rowwise(fn, [(o, GLA_VW, 0, _PER_HEAD), (z, GLA_VW, Z_GR // GLA_VW, _PER_HEAD), (dy, GLA_VW, 0, _PER_HEAD)], [(g, None)],
                    [(GLA_VW, F32), (GLA_VW, BF16)], [(1, GLA_VAL_DIM)], tt=256, name=name)


def _small_gates(s, fb, gup, gb):
    lane = lax.broadcasted_iota(jnp.int32, s.shape, 1)
    lf = jnp.where(lane < FOX_HEADS, _log_sigmoid(s + fb), 0.0)
    pre = jnp.dot(s.astype(BF16), gup.astype(BF16), preferred_element_type=F32) + gb
    return lf, _log_sigmoid(pre) / GLA_TAU


def _small_fwd(s, fb, gup, gb, name):
    return _rowwise(_small_gates, [_full(s)], [(fb, None), (gup, None), (gb, None)], [(SMALL_W, F32), (GLA_KW, F32)], tt=256, name=name)


def _small_bwd(s, fb, gup, gb, dlf, dla, name):
    def fn(sb, dlfb, dlab, fbb, gupb, gbb):
        _, vjp = jax.vjp(_small_gates, sb, fbb, gupb, gbb)
        return vjp((dlfb, dlab))
    return _rowwise(fn, [_full(s), _full(dlf), _full(dla)], [(fb, None), (gup, None), (gb, None)],
                    [(SMALL_W, BF16)], [(1, SMALL_W), (SMALL_W, GLA_KW), (1, GLA_KW)], tt=256, name=name)


def _head_fn(h3, pgl, pp, tgt, gf):
    h4 = h3 + jax.nn.sigmoid(pgl) * pp
    err = _rms(h4, gf) - tgt
    return 0.5 * jnp.sum(jnp.mean(err * err, axis=-1, keepdims=True))


def _head(h3, pgl, pp, tgt, gf, name):
    def fn(hb, gl, pb, tb, gfb):
        loss, vjp = jax.vjp(_head_fn, hb, _f(gl), _f(pb), tb, gfb)
        dh, dgl, dpp, _, dgf = vjp(jnp.ones((), F32))
        return dh, dgl, dpp, jnp.full((1, 128), loss, F32), dgf
    d = h3.shape[1]
    return _rowwise(fn, [_full(h3), _full(pgl), _full(pp), _full(tgt)], [(gf, None)],
                    [(d, F32), (d, BF16), (d, BF16)], [(1, 128), (1, d)], tt=256, name=name)


def _cumsum_tokens(a, reverse, name):
    t, w = a.shape
    r = min(256, t)
    nb = t // r

    def body(a_ref, o_ref, carry_ref):
        @pl.when(pl.program_id(0) == 0)
        def _():
            carry_ref[...] = jnp.zeros(carry_ref.shape, F32)
        row = lax.broadcasted_iota(jnp.int32, (r, r), 0)
        col = lax.broadcasted_iota(jnp.int32, (r, r), 1)
        tri = ((col >= row) if reverse else (col <= row)).astype(F32)
        blk = a_ref[...]
        o_ref[...] = jnp.dot(tri, blk, preferred_element_type=F32, precision=lax.Precision.HIGHEST) + carry_ref[...]
        carry_ref[...] += jnp.sum(blk, axis=0, keepdims=True)

    idx = (lambda i: (nb - 1 - i, 0)) if reverse else (lambda i: (i, 0))
    return pl.pallas_call(
        body, name=name, grid=(nb,), in_specs=[pl.BlockSpec((r, w), idx)], out_specs=pl.BlockSpec((r, w), idx),
        out_shape=jax.ShapeDtypeStruct((t, w), F32), scratch_shapes=[pltpu.VMEM((1, w), F32)],
        compiler_params=pltpu.CompilerParams(dimension_semantics=("arbitrary",)),
    )(a)


FOX_TQ, FOX_TK = 256, 512
FOX_SCALE = HEAD_DIM ** -0.5


def _fox_tiles(t):
    tq, tk = min(FOX_TQ, t), min(FOX_TK, t)
    return tq, tk, t // tq, t // tk


def _blocked_t(a, blk):
    return a.reshape(a.shape[0] // blk, blk, a.shape[1]).transpose(0, 2, 1)


def _unblocked_t(b):
    return b.transpose(0, 2, 1).reshape(b.shape[0] * b.shape[2], b.shape[1])


def _fox_scores(k, qt, frep, i, j, masked):
    tk, tq = k.shape[0], qt.shape[1]
    st = jnp.dot(k, qt, preferred_element_type=F32) * FOX_SCALE - jnp.tile(frep, (1, tq // HEAD_DIM))
    if masked:
        key = j * tk + lax.broadcasted_iota(jnp.int32, (tk, tq), 0)
        query = i * tq + lax.broadcasted_iota(jnp.int32, (tk, tq), 1)
        st = jnp.where(key <= query, st, NEG)
    return st


def _fox_fwd(z, qt, vt, frep, name, comm=None):
    t = z.shape[0]
    tq, tk, nq, nk = _fox_tiles(t)
    kb = Z_FK // HEAD_DIM

    def body(qt_ref, k_ref, vt_ref, frep_ref, ot_ref, lse_ref):
        i = pl.program_id(1)
        qt = qt_ref[...]
        last = ((i + 1) * tq - 1) // tk

        def block(j, carry, masked):
            m, l, acc = carry
            rows = pl.ds(pl.multiple_of(j * tk, tk), tk)
            st = _fox_scores(k_ref[rows, :], qt, frep_ref[rows, :], i, j, masked)
            m_new = jnp.maximum(m, jnp.max(st, axis=0, keepdims=True))
            alpha = jnp.exp(m - m_new)
            p = jnp.exp(st - m_new)
            l = alpha * l + jnp.sum(p, axis=0, keepdims=True)
            acc = alpha * acc + jnp.dot(vt_ref[j], p.astype(BF16), preferred_element_type=F32)
            return m_new, l, acc

        init = (jnp.full((1, tq), NEG, F32), jnp.zeros((1, tq), F32), jnp.zeros((HEAD_DIM, tq), F32))
        m, l, acc = block(last, lax.fori_loop(0, last, lambda j, c: block(j, c, False), init), True)
        ot_ref[...] = (acc / l).astype(ot_ref.dtype)
        lse_ref[...] = m + jnp.log(l)

    stat = pl.BlockSpec((None, None, 1, tq), lambda h, i: (h, i, 0, 0))
    (ot, lse), travelled = _hosted(
        body, comm, name=name, grid=(FOX_HEADS, nq),
        in_specs=[pl.BlockSpec((None, HEAD_DIM, tq), lambda h, i: (i, h, 0)),
                  pl.BlockSpec((t, HEAD_DIM), lambda h, i: (0, kb + h)),
                  pl.BlockSpec((nk, HEAD_DIM, tk), lambda h, i: (0, h, 0)),
                  pl.BlockSpec((None, t, HEAD_DIM), lambda h, i: (h, 0, 0))],
        out_specs=[pl.BlockSpec((None, HEAD_DIM, tq), lambda h, i: (i, h, 0)), stat],
        out_shape=[jax.ShapeDtypeStruct((nq, FOX_W, tq), BF16), jax.ShapeDtypeStruct((FOX_HEADS, nq, 1, tq), F32)],
        scratch_shapes=[], semantics=("parallel", "parallel"), args=(qt, z, vt, frep))
    return ot, lse, travelled


def _fox_bwd_q(z, qt, kt, ot, dot, lse, frep, name):
    t = z.shape[0]
    tq, tk, nq, nk = _fox_tiles(t)
    kb, vb = Z_FK // HEAD_DIM, Z_FV // HEAD_DIM

    def body(qt_ref, k_ref, kt_ref, v_ref, ot_ref, dot_ref, lse_ref, frep_ref, dqt_ref, delta_ref, dfq_ref):
        i = pl.program_id(1)
        qt, dot = qt_ref[...], dot_ref[...]
        lse = lse_ref[...]
        delta = jnp.sum(_f(dot) * _f(ot_ref[...]), axis=0, keepdims=True)
        delta_ref[...] = delta
        last = ((i + 1) * tq - 1) // tk

        def block(j, carry, masked):
            dq, dfq = carry
            rows = pl.ds(pl.multiple_of(j * tk, tk), tk)
            p = jnp.exp(_fox_scores(k_ref[rows, :], qt, frep_ref[rows, :], i, j, masked) - lse)
            dp = jnp.dot(v_ref[rows, :], dot, preferred_element_type=F32)
            ds = p * (dp - delta)
            return dq + jnp.dot(kt_ref[j], ds.astype(BF16), preferred_element_type=F32), dfq + jnp.sum(ds, axis=0, keepdims=True)

        init = (jnp.zeros((HEAD_DIM, tq), F32), jnp.zeros((1, tq), F32))
        dq, dfq = block(last, lax.fori_loop(0, last, lambda j, c: block(j, c, False), init), True)
        dqt_ref[...] = (dq * FOX_SCALE).astype(dqt_ref.dtype)
        dfq_ref[...] = dfq

    mine = pl.BlockSpec((None, HEAD_DIM, tq), lambda h, i: (i, h, 0))
    stat = pl.BlockSpec((None, None, 1, tq), lambda h, i: (h, i, 0, 0))
    return pl.pallas_call(
        body, name=name, grid=(FOX_HEADS, nq),
        in_specs=[mine,
                  pl.BlockSpec((t, HEAD_DIM), lambda h, i: (0, kb + h)),
                  pl.BlockSpec((nk, HEAD_DIM, tk), lambda h, i: (0, h, 0)),
                  pl.BlockSpec((t, HEAD_DIM), lambda h, i: (0, vb + h)),
                  mine, mine, stat,
                  pl.BlockSpec((None, t, HEAD_DIM), lambda h, i: (h, 0, 0))],
        out_specs=[mine, stat, stat],
        out_shape=[jax.ShapeDtypeStruct((nq, FOX_W, tq), BF16), jax.ShapeDtypeStruct((FOX_HEADS, nq, 1, tq), F32),
                   jax.ShapeDtypeStruct((FOX_HEADS, nq, 1, tq), F32)],
        compiler_params=pltpu.CompilerParams(dimension_semantics=("parallel", "parallel"), vmem_limit_bytes=VMEM_LIMIT),
    )(qt, z, kt, z, ot, dot, lse, frep)


def _fox_bwd_kv(z, qt, do, dot, lse, delta, frep, name):
    t = z.shape[0]
    tq, tk, nq, nk = _fox_tiles(t)
    qb, kb, vb = Z_FQ // HEAD_DIM, Z_FK // HEAD_DIM, Z_FV // HEAD_DIM
    per = tk // tq

    def body(k_ref, v_ref, frep_ref, q_ref, qt_ref, do_ref, dot_ref, lse_ref, delta_ref, dk_ref, dv_ref, dfk_ref):
        j = pl.program_id(1)
        k, v, frep = k_ref[...], v_ref[...], frep_ref[...]

        def block(i, carry, masked):
            dk, dv, dfk = carry
            rows = pl.ds(pl.multiple_of(i * tq, tq), tq)
            p = jnp.exp(_fox_scores(k, qt_ref[i], frep, i, j, masked) - lse_ref[i])
            dv = dv + jnp.dot(p.astype(BF16), do_ref[rows, :], preferred_element_type=F32)
            dp = jnp.dot(v, dot_ref[i], preferred_element_type=F32)
            ds = p * (dp - delta_ref[i])
            dk = dk + jnp.dot(ds.astype(BF16), q_ref[rows, :], preferred_element_type=F32)
            for part in range(tq // HEAD_DIM):
                dfk = dfk + ds[:, part * HEAD_DIM:(part + 1) * HEAD_DIM]
            return dk, dv, dfk

        zero = jnp.zeros((tk, HEAD_DIM), F32)
        carry = (zero, zero, zero)
        for step in range(per):
            carry = block(j * per + step, carry, True)
        dk, dv, dfk = lax.fori_loop((j + 1) * per, nq, lambda i, c: block(i, c, False), carry)
        dk_ref[...] = (dk * FOX_SCALE).astype(dk_ref.dtype)
        dv_ref[...] = dv.astype(dv_ref.dtype)
        dfk_ref[...] = jnp.sum(dfk, axis=1, keepdims=True)

    whole_t = pl.BlockSpec((nq, HEAD_DIM, tq), lambda h, j: (0, h, 0))
    whole_stat = pl.BlockSpec((None, nq, 1, tq), lambda h, j: (h, 0, 0, 0))
    return pl.pallas_call(
        body, name=name, grid=(FOX_HEADS, nk),
        in_specs=[pl.BlockSpec((tk, HEAD_DIM), lambda h, j: (j, kb + h)),
                  pl.BlockSpec((tk, HEAD_DIM), lambda h, j: (j, vb + h)),
                  pl.BlockSpec((None, tk, HEAD_DIM), lambda h, j: (h, j, 0)),
                  pl.BlockSpec((t, HEAD_DIM), lambda h, j: (0, qb + h)),
                  whole_t,
                  pl.BlockSpec((t, HEAD_DIM), lambda h, j: (0, h)),
                  whole_t, whole_stat, whole_stat],
        out_specs=[pl.BlockSpec((tk, HEAD_DIM), lambda h, j: (j, h)), pl.BlockSpec((tk, HEAD_DIM), lambda h, j: (j, h)),
                   pl.BlockSpec((None, tk, 1), lambda h, j: (h, j, 0))],
        out_shape=[jax.ShapeDtypeStruct((t, FOX_W), BF16), jax.ShapeDtypeStruct((t, FOX_W), BF16),
                   jax.ShapeDtypeStruct((FOX_HEADS, t, 1), F32)],
        compiler_params=pltpu.CompilerParams(dimension_semantics=("parallel", "parallel"), vmem_limit_bytes=VMEM_LIMIT),
    )(z, z, frep, z, qt, do, dot, lse, delta)


def _gla_step(st, q, k, v, la):
    row = lax.broadcasted_iota(jnp.int32, (CHUNK, CHUNK), 0)
    col = lax.broadcasted_iota(jnp.int32, (CHUNK, CHUNK), 1)
    tri = (col <= row).astype(F32)
    a_cum = jnp.dot(tri, la, preferred_element_type=F32, precision=lax.Precision.HIGHEST)
    a_tot = jnp.sum(la, axis=0, keepdims=True)
    k_dec = (_f(k) * jnp.exp(a_tot - a_cum)).astype(BF16)
    qs = (_f(q) * (HEAD_DIM ** -0.5)).astype(BF16)
    st = st * jnp.exp(a_tot) + lax.dot_general(v.astype(BF16), k_dec, (((0,), (0,)), ((), ())), preferred_element_type=F32)
    o = lax.dot_general(qs, st.astype(BF16), (((1,), (1,)), ((), ())), preferred_element_type=F32)
    return st, o


def _gla_blocks(t):
    r = min(256, t)
    return r, t // r, r // CHUNK


def _gla_fwd(z, la, name):
    t = z.shape[0]
    r, nb, nch = _gla_blocks(t)

    def body(q_ref, k_ref, v_ref, la_ref, o_ref, sp_ref, st_ref):
        @pl.when(pl.program_id(0) == 0)
        def _():
            st_ref[...] = jnp.zeros(st_ref.shape, F32)
        for c in range(nch):
            rows = slice(c * CHUNK, (c + 1) * CHUNK)
            for h in range(GLA_HEADS):
                kc = slice(h * HEAD_DIM, (h + 1) * HEAD_DIM)
                vc = slice(h * GLA_VAL_DIM, (h + 1) * GLA_VAL_DIM)
                st = st_ref[h]
                sp_ref[c, h] = st
                st, o = _gla_step(st, q_ref[rows, kc], k_ref[rows, kc], v_ref[rows, vc], la_ref[rows, kc])
                st_ref[h] = st
                o_ref[rows, vc] = o

    return pl.pallas_call(
        body, name=name, grid=(nb,),
        in_specs=[pl.BlockSpec((r, GLA_KW), lambda i: (i, Z_GQ // GLA_KW)), pl.BlockSpec((r, GLA_KW), lambda i: (i, Z_GK // GLA_KW)),
                  pl.BlockSpec((r, GLA_VW), lambda i: (i, Z_GV // GLA_VW)), pl.BlockSpec((r, GLA_KW), lambda i: (i, 0))],
        out_specs=[pl.BlockSpec((r, GLA_VW), lambda i: (i, 0)),
                   pl.BlockSpec((nch, GLA_HEADS, GLA_VAL_DIM, HEAD_DIM), lambda i: (i, 0, 0, 0))],
        out_shape=[jax.ShapeDtypeStruct((t, GLA_VW), F32),
                   jax.ShapeDtypeStruct((t // CHUNK, GLA_HEADS, GLA_VAL_DIM, HEAD_DIM), F32)],
        scratch_shapes=[pltpu.VMEM((GLA_HEADS, GLA_VAL_DIM, HEAD_DIM), F32)],
        compiler_params=pltpu.CompilerParams(dimension_semantics=("arbitrary",), vmem_limit_bytes=VMEM_LIMIT),
    )(z, z, z, la)


def _gla_bwd(z, la, sprev, do, name):
    t = z.shape[0]
    r, nb, nch = _gla_blocks(t)

    def body(q_ref, k_ref, v_ref, la_ref, sp_ref, do_ref, dq_ref, dk_ref, dv_ref, dla_ref, dst_ref):
        @pl.when(pl.program_id(0) == 0)
        def _():
            dst_ref[...] = jnp.zeros(dst_ref.shape, F32)
        for c in reversed(range(nch)):
            rows = slice(c * CHUNK, (c + 1) * CHUNK)
            for h in range(GLA_HEADS):
                kc = slice(h * HEAD_DIM, (h + 1) * HEAD_DIM)
                vc = slice(h * GLA_VAL_DIM, (h + 1) * GLA_VAL_DIM)
                _, vjp = jax.vjp(_gla_step, sp_ref[c, h], q_ref[rows, kc], k_ref[rows, kc], v_ref[rows, vc], la_ref[rows, kc])
                dst, dq, dk, dv, dla = vjp((dst_ref[h], do_ref[rows, vc]))
                dst_ref[h] = dst
                dq_ref[rows, kc] = dq
                dk_ref[rows, kc] = dk
                dv_ref[rows, vc] = dv
                dla_ref[rows, kc] = dla

    rev = lambda i: (nb - 1 - i, 0)
    return pl.pallas_call(
        body, name=name, grid=(nb,),
        in_specs=[pl.BlockSpec((r, GLA_KW), lambda i: (nb - 1 - i, Z_GQ // GLA_KW)), pl.BlockSpec((r, GLA_KW), lambda i: (nb - 1 - i, Z_GK // GLA_KW)),
                  pl.BlockSpec((r, GLA_VW), lambda i: (nb - 1 - i, Z_GV // GLA_VW)), pl.BlockSpec((r, GLA_KW), rev),
                  pl.BlockSpec((nch, GLA_HEADS, GLA_VAL_DIM, HEAD_DIM), lambda i: (nb - 1 - i, 0, 0, 0)),
                  pl.BlockSpec((r, GLA_VW), rev)],
        out_specs=[pl.BlockSpec((r, GLA_KW), rev), pl.BlockSpec((r, GLA_KW), rev), pl.BlockSpec((r, GLA_VW), rev), pl.BlockSpec((r, GLA_KW), rev)],
        out_shape=[jax.ShapeDtypeStruct((t, GLA_KW), BF16), jax.ShapeDtypeStruct((t, GLA_KW), BF16),
                   jax.ShapeDtypeStruct((t, GLA_VW), BF16), jax.ShapeDtypeStruct((t, GLA_KW), F32)],
        scratch_shapes=[pltpu.VMEM((GLA_HEADS, GLA_VAL_DIM, HEAD_DIM), F32)],
        compiler_params=pltpu.CompilerParams(dimension_semantics=("arbitrary",), vmem_limit_bytes=VMEM_LIMIT),
    )(z, z, z, la, sprev, do)


def _local_step(x, p, tgt, shards, sp, cidx, chip):
    t = x.shape[0]
    tq, tk, _, _ = _fox_tiles(t)
    full, reduced = {}, {}

    def plan(names):
        keys, shapes, places = [], [], []
        for n in names:
            r, cc = shards[n].shape
            key, part, parts = FUSED.get(n, (n, 0, 1))
            if key not in keys:
                keys.append(key)
                stacked = n in ROW_SHARDED or n == "w_in"
                shapes.append(jax.ShapeDtypeStruct((4 * r, cc) if stacked else (r, 4 * cc * parts), shards[n].dtype))
            places.append((keys.index(key), r, 0, 0) if n in ROW_SHARDED or n == "w_in" else (keys.index(key), 0, part * 4 * cc, cc))
        return keys, shapes, places

    def gather(names):
        _, shapes, places = plan(names)
        return _ag_comm([shards[n] for n in names], shapes, places)

    def landed(names, got):
        keys, _, _ = plan(names)
        for key, g in zip(keys, got):
            full[key] = g
        if "w_in" in keys:
            r = shards["w_in"].shape[0]
            full["w_in"] = jnp.transpose(full["w_in"].reshape(4, r, -1), (1, 0, 2)).reshape(r, -1)

    def pair_sums(grads):
        parts, entries = [], []
        for g, names in grads:
            cols = g.shape[1] // (4 * len(names))
            if names[0] in ROW_SHARDED or names[0] == "w_in":
                parts.append(_shard_parts(g, names[0]))
                entries.append((names[0], len(parts) - 1, "slot", 0, parts[-1].shape[2]))
            else:
                parts.append(g[None])
                entries += [(n, len(parts) - 1, "cols", k * 4 * cols, cols) for k, n in enumerate(names)]
        swapped = _pair_swap(parts, "pair_swap")
        return entries, [_sum_half(a, b, cidx, "sum_half") for a, b in zip(parts, swapped)]

    def exchange(entries, sums):
        return _exchange_comm(sums, [e[1:] for e in entries])

    def exchanged(entries, sums, got):
        for (n, si, mode, first, _), g in zip(entries, got):
            reduced[n] = (g, sums[si], mode, first)

    first = ["ffn1_w_gate", "ffn1_w_up"]
    landed(first, _run_comm(gather(first), "all_gather"))
    w_gu1 = full["gu1"]
    n1 = _norm_fwd(x, sp["ffn1_norm"], "norm1_fwd")
    names = ["ffn1_w_down", "w_in", "gla_gate_up"]
    gu1, got = _mm(n1, w_gu1, name="mm_gu_gather", comm=gather(names))
    landed(names, got)
    a1 = _act_fwd(gu1, "act_fwd")
    names = ["w_merge_gate", "w_branch_fox", "w_branch_gla"]
    f1, got = _mm(a1, full["ffn1_w_down"], out_dtype=F32, name="mm_down_gather", comm=gather(names))
    landed(names, got)
    w_big, w_sm = _in_layout(full["w_in"], full["w_merge_gate"])
    gup = jnp.zeros((SMALL_W, GLA_KW), F32).at[FOX_HEADS:FOX_HEADS + GLA_RANK].set(full["gla_gate_up"])
    h1, u = _resnorm_fwd(x, f1, sp["mix_norm"], 0.5, "resnorm_fwd_half")
    names = ["w_out", "ffn2_w_down", "w_ple_proj", "w_ple_gate"]
    z, got = _mm(u, w_big, name="mm_in_gather", comm=gather(names))
    landed(names, got)
    s = _mm(u, w_sm, out_dtype=F32, name="mm_in_small")
    lf, la = _small_fwd(s, sp["fb"], gup, sp["gb"], "small_fwd")
    fp = _cumsum_tokens(lf, False, "cumsum_fwd")
    frep = jnp.broadcast_to(fp[:, :FOX_HEADS].T[:, :, None], (FOX_HEADS, t, HEAD_DIM))
    qt = _blocked_t(z[:, Z_FQ:Z_FQ + FOX_W], tq)
    kt = _blocked_t(z[:, Z_FK:Z_FK + FOX_W], tk)
    vt = _blocked_t(z[:, Z_FV:Z_FV + FOX_W], tk)
    names = ["ffn2_w_gate", "ffn2_w_up"]
    ot, lse, got = _fox_fwd(z, qt, vt, frep, "fox_fwd_gather", comm=gather(names))
    landed(names, got)
    w_gu2 = full["gu2"]
    y_fox = _unblocked_t(ot)
    o_gla, sprev = _gla_fwd(z, la, "gla_fwd")
    y_gla = _gla_out_fwd(o_gla, z, sp["ghn"], "gla_out_fwd")
    bf = _mm(y_fox, full["w_branch_fox"], name="mm_branch")
    bg = _mm(y_gla, full["w_branch_gla"], name="mm_branch")
    merged = _merge_fwd(z, bf, bg, sp["bm"], "merge_fwd")
    mo = _mm(merged, full["w_out"], out_dtype=F32, name="mm_out")
    h2, n2 = _resnorm_fwd(h1, mo, sp["ffn2_norm"], 1.0, "resnorm_fwd_one")
    gu2 = _mm(n2, w_gu2, name="mm_gu")
    a2 = _act_fwd(gu2, "act_fwd")
    f2 = _mm(a2, full["ffn2_w_down"], out_dtype=F32, name="mm_down")
    h3, n4 = _resnorm_fwd(h2, f2, sp["ple_norm"], 0.5, "resnorm_fwd_half")
    pgl = _mm(n4, full["w_ple_gate"], name="mm_pg")
    pb = p.astype(BF16)
    pp = _mm(pb, full["w_ple_proj"], name="mm_pp")

    dh3, dpgl, dpp, loss, d_final = _head(h3, pgl, pp, tgt, sp["final_norm"], "head")
    ds_ = {"final_norm": d_final}
    entries, sums = pair_sums([(_mm(n4, dpgl, ta=True, name="mm_dw_sq"), ["w_ple_gate"]), (_mm(pb, dpp, ta=True, name="mm_dw_pp"), ["w_ple_proj"])])
    dn4, got = _mm(dpgl, full["w_ple_gate"], tb=True, out_dtype=F32, name="mm_dx_sq_f32_exchange", comm=exchange(entries, sums))
    exchanged(entries, sums, got)
    dh3, df2, ds_["ple_norm"] = _norm_bwd(h3, [dn4], dh3, sp["ple_norm"], 0.5, "norm_bwd_1")

    def ffn_bwd(n, gu, a, df, wgu, wd, which):
        entries, sums = pair_sums([(_mm(a, df, ta=True, name="mm_dw_down"), [which + "_w_down"])])
        da, got = _mm(df, wd, tb=True, name="mm_dx_down_exchange", comm=exchange(entries, sums))
        exchanged(entries, sums, got)
        dgu = _act_bwd(gu, da, "act_bwd")
        entries, sums = pair_sums([(_mm(n, dgu, ta=True, name="mm_dw_gu"), [which + "_w_gate", which + "_w_up"])])
        dn, got = _mm(dgu, wgu, tb=True, out_dtype=F32, name="mm_dx_gu_exchange", comm=exchange(entries, sums))
        exchanged(entries, sums, got)
        return dn

    dn2 = ffn_bwd(n2, gu2, a2, df2, w_gu2, full["ffn2_w_down"], "ffn2")
    dh2, dmix, ds_["ffn2_norm"] = _norm_bwd(h2, [dn2], dh3, sp["ffn2_norm"], 1.0, "norm_bwd_1")

    dw_out = _mm(merged, dmix, ta=True, name="mm_dw_sq")
    dmerged = _mm(dmix, full["w_out"], tb=True, name="mm_dx_sq")
    dgl, dbf, dbg, ds_["bm"] = _merge_bwd(z, bf, bg, sp["bm"], dmerged, "merge_bwd")
    mix_entries, mix_sums = pair_sums([(dw_out, ["w_out"]), (_mm(y_fox, dbf, ta=True, name="mm_dw_branch"), ["w_branch_fox"]),
                                       (_mm(y_gla, dbg, ta=True, name="mm_dw_branch"), ["w_branch_gla"])])
    dy_fox = _mm(dbf, full["w_branch_fox"], tb=True, name="mm_dx_branch")
    dy_gla = _mm(dbg, full["w_branch_gla"], tb=True, name="mm_dx_branch")

    do_gla, dgr, ds_["ghn"] = _gla_out_bwd(o_gla, z, sp["ghn"], dy_gla, "gla_out_bwd")
    dgq, dgk, dgv, dla = _gla_bwd(z, la, sprev, do_gla, "gla_bwd")
    dot = _blocked_t(dy_fox, tq)
    dqt, delta, df_query = _fox_bwd_q(z, qt, kt, ot, dot, lse, frep, "fox_bwd_q")
    dfq = _unblocked_t(dqt)
    dfk, dfv, df_key = _fox_bwd_kv(z, qt, dy_fox, dot, lse, delta, frep, "fox_bwd_kv")
    df = df_query.reshape(FOX_HEADS, t) - df_key.reshape(FOX_HEADS, t)
    dfp = jnp.pad(df.T, ((0, 0), (0, SMALL_W - FOX_HEADS)))
    dlf = _cumsum_tokens(dfp, True, "cumsum_bwd")
    dsm, ds_["fb"], dgup, ds_["gb"] = _small_bwd(s, sp["fb"], gup, sp["gb"], dlf, dla, "small_bwd")
    dz = jnp.concatenate([dfq, dfk, dfv, dgq, dgk, dgv, dgr, dgl], axis=1)
    dw_big, got = _mm(u, dz, ta=True, name="mm_dw_in_exchange", comm=exchange(mix_entries, mix_sums))
    exchanged(mix_entries, mix_sums, got)
    dw_sm = _mm(u, dsm, ta=True, out_dtype=F32, name="mm_dw_in_small").astype(BF16)
    dw_in = jnp.concatenate([dw_big[:, Z_FQ:Z_GQ], dw_sm[:, :FOX_HEADS], dw_big[:, Z_GQ:Z_GL], dw_sm[:, FOX_HEADS:FOX_HEADS + GLA_RANK]], axis=1)
    entries, sums = pair_sums([(dw_in, ["w_in"]), (dw_big[:, Z_GL:], ["w_merge_gate"]), (dgup[FOX_HEADS:FOX_HEADS + GLA_RANK], ["gla_gate_up"])])
    du1, got = _mm(dz, w_big, tb=True, out_dtype=F32, name="mm_dx_in_exchange", comm=exchange(entries, sums))
    exchanged(entries, sums, got)
    du2 = _mm(dsm, w_sm, tb=True, out_dtype=F32, name="mm_dx_in_small")
    dh1, df1, ds_["mix_norm"] = _norm_bwd(h1, [du1, du2], dh2, sp["mix_norm"], 0.5, "norm_bwd_2")

    dn1 = ffn_bwd(n1, gu1, a1, df1, w_gu1, full["ffn1_w_down"], "ffn1")
    grad_x, _, ds_["ffn1_norm"] = _norm_bwd(x, [dn1], dh1, sp["ffn1_norm"], 1.0, "norm_bwd_1")
    return loss, grad_x, reduced, ds_


def _half_rows(ref, which):
    r2 = ref.shape[0] // 2
    return ref.at[pl.ds(pl.multiple_of(which * r2, r2), r2)]


class _Comm:
    def __init__(self, ins, out_shape, sems, start, finish):
        self.ins, self.out_shape, self.sems, self.start, self.finish = ins, out_shape, sems, start, finish


def _run_comm(comm, name):
    n_in, n_out = len(comm.ins), len(comm.out_shape)

    def body(*refs):
        parts = refs[:n_in], refs[n_in:n_in + n_out], refs[n_in + n_out:]
        comm.start(*parts)
        comm.finish(*parts)

    return pl.pallas_call(
        body, name=name, in_specs=[ANY] * n_in, out_specs=[ANY] * n_out, out_shape=comm.out_shape,
        scratch_shapes=comm.sems, compiler_params=pltpu.CompilerParams(has_side_effects=True),
    )(*comm.ins)


def _hosted(body, comm, *, name, grid, in_specs, out_specs, out_shape, scratch_shapes, semantics, args):
    if comm is None:
        res = pl.pallas_call(
            body, name=name, grid=grid, in_specs=in_specs, out_specs=out_specs, out_shape=out_shape, scratch_shapes=scratch_shapes,
            compiler_params=pltpu.CompilerParams(dimension_semantics=semantics, vmem_limit_bytes=VMEM_LIMIT),
        )(*args)
        return res, None
    ni, no, ns = len(in_specs), len(out_shape), len(scratch_shapes)
    ci, co = len(comm.ins), len(comm.out_shape)

    def wrapped(*refs):
        h_in, c_in = refs[:ni], refs[ni:ni + ci]
        h_out, c_out = refs[ni + ci:ni + ci + no], refs[ni + ci + no:ni + ci + no + co]
        h_scr, c_sem = refs[ni + ci + no + co:ni + ci + no + co + ns], refs[ni + ci + no + co + ns:]
        ids = [pl.program_id(axis) for axis in range(len(grid))]
        first = functools.reduce(jnp.logical_and, [i == 0 for i in ids])
        last = functools.reduce(jnp.logical_and, [i == g - 1 for i, g in zip(ids, grid)])

        @pl.when(first)
        def _():
            comm.start(c_in, c_out, c_sem)

        body(*h_in, *h_out, *h_scr)

        @pl.when(last)
        def _():
            comm.finish(c_in, c_out, c_sem)

    res = pl.pallas_call(
        wrapped, name=name, grid=grid, in_specs=list(in_specs) + [ANY] * ci, out_specs=list(out_specs) + [ANY] * co,
        out_shape=list(out_shape) + list(comm.out_shape), scratch_shapes=list(scratch_shapes) + list(comm.sems),
        compiler_params=pltpu.CompilerParams(dimension_semantics=("arbitrary",) * len(grid), vmem_limit_bytes=VMEM_LIMIT, has_side_effects=True),
    )(*args, *comm.ins)
    return res[:no], res[no:]


def _ag_comm(shards, out_shape, places):
    n = len(shards)

    def copies(ins, outs, sems):
        ici_send, ici_recv, d2d_send, d2d_recv = sems
        x, y, c = lax.axis_index("x"), lax.axis_index("y"), lax.axis_index("c")
        chips = [(1 - x, y), (x, 1 - y), (1 - x, 1 - y)]
        slot = lambda chip: 2 * chip[0] + chip[1]

        def window(wi, origin, half):
            out, row_step, col_base, col_step = places[wi]
            r, cc = shards[wi].shape
            rows = pl.ds(pl.multiple_of(slot(origin) * row_step + half * (r // 2), r // 2), r // 2)
            cols = pl.ds(pl.multiple_of(col_base + slot(origin) * col_step, HEAD_DIM), cc) if col_step else pl.ds(col_base, cc)
            return outs[out].at[rows, cols]

        def over_ici(wi, j, origin):
            return pltpu.make_async_remote_copy(
                src_ref=_half_rows(ins[wi], c), dst_ref=window(wi, origin, c),
                send_sem=ici_send.at[3 * wi + j], recv_sem=ici_recv.at[3 * wi + j],
                device_id=(chips[j][0], chips[j][1], c), device_id_type=MESH)

        def over_d2d(wi, j, half):
            place = window(wi, chips[j], half)
            return pltpu.make_async_remote_copy(
                src_ref=place, dst_ref=place, send_sem=d2d_send.at[3 * wi + j], recv_sem=d2d_recv.at[3 * wi + j],
                device_id=(x, y, 1 - c), device_id_type=MESH)

        return over_ici, over_d2d, (x, y), chips, c

    def chunk_rows(wi):
        r, cc = shards[wi].shape
        item = shards[wi].dtype.itemsize
        return _pick(r, max(32 // item, BOUNCE_BYTES // (cc * item)), 32 // item)

    def start(ins, outs, scratch):
        over_ici, _, me, _, _ = copies(ins, outs, scratch[:4])
        for wi in range(n):
            for j in range(3):
                over_ici(wi, j, me).start()
        loc_sems = scratch[4]
        for wi in range(n):
            out, row_step, col_base, col_step = places[wi]
            r, cc = shards[wi].shape
            rc = chunk_rows(wi)
            buf = scratch[5 + wi]
            slot = 2 * me[0] + me[1]
            cols = pl.ds(pl.multiple_of(col_base + slot * col_step, HEAD_DIM), cc) if col_step else pl.ds(col_base, cc)

            def load(k):
                return pltpu.make_async_copy(ins[wi].at[pl.ds(k * rc, rc)], buf.at[k % 2], loc_sems.at[2 * wi])

            def store(k):
                rows = pl.ds(pl.multiple_of(slot * row_step + k * rc, rc), rc)
                return pltpu.make_async_copy(buf.at[k % 2], outs[out].at[rows, cols], loc_sems.at[2 * wi + 1])

            load(0).start()
            for k in range(r // rc):
                load(k).wait()
                if k + 1 < r // rc:
                    load(k + 1).start()
                store(k).start()
                store(k).wait()

    def finish(ins, outs, scratch):
        over_ici, over_d2d, me, chips, c = copies(ins, outs, scratch[:4])
        for wi in range(n):
            for j in range(3):
                over_ici(wi, j, chips[j]).wait_recv()
                over_d2d(wi, j, c).start()
        for wi in range(n):
            for j in range(3):
                over_d2d(wi, j, 1 - c).wait_recv()
        for wi in range(n):
            for j in range(3):
                over_ici(wi, j, me).wait_send()
                over_d2d(wi, j, c).wait_send()

    bounce = [pltpu.VMEM((min(2, s.shape[0] // chunk_rows(wi)), chunk_rows(wi), s.shape[1]), s.dtype) for wi, s in enumerate(shards)]
    return _Comm(list(shards), list(out_shape), [pltpu.SemaphoreType.DMA((3 * n,))] * 4 + [pltpu.SemaphoreType.DMA((2 * n,))] + bounce,
                 start, finish)


def _pair_swap(parts, name):
    n = len(parts)

    def body(*refs):
        ins, outs = refs[:n], refs[n:2 * n]
        send_sems, recv_sems = refs[2 * n:]
        x, y, c = lax.axis_index("x"), lax.axis_index("y"), lax.axis_index("c")

        def swap(wi):
            r2 = parts[wi].shape[1] // 2
            return pltpu.make_async_remote_copy(
                src_ref=ins[wi].at[:, pl.ds(pl.multiple_of((1 - c) * r2, r2), r2)], dst_ref=outs[wi],
                send_sem=send_sems.at[wi], recv_sem=recv_sems.at[wi], device_id=(x, y, 1 - c), device_id_type=MESH)

        copies = [swap(wi) for wi in range(n)]
        for cp in copies:
            cp.start()
        for cp in copies:
            cp.wait()

    return pl.pallas_call(
        body, name=name, in_specs=[ANY] * n, out_specs=[ANY] * n,
        out_shape=[jax.ShapeDtypeStruct((s.shape[0], s.shape[1] // 2, s.shape[2]), s.dtype) for s in parts],
        scratch_shapes=[pltpu.SemaphoreType.DMA((n,))] * 2, compiler_params=pltpu.CompilerParams(has_side_effects=True),
    )(*parts)


def _row_tile(r, c, budget=1 << 19):
    return r if r <= 8 else _pick(r, max(8, budget // c), 8)


def _sum_half(parts, other, cidx, name):
    nl, r, cc = parts.shape
    r2 = r // 2
    tr = _row_tile(r2, cc)

    def body(c_ref, p_ref, q_ref, o_ref):
        o_ref[...] = (_f(p_ref[...]) + _f(q_ref[...])).astype(o_ref.dtype)

    return pl.pallas_call(
        body, name=name, out_shape=jax.ShapeDtypeStruct((nl, r2, cc), parts.dtype),
        grid_spec=pltpu.PrefetchScalarGridSpec(
            num_scalar_prefetch=1, grid=(nl, r2 // tr),
            in_specs=[pl.BlockSpec((None, None, tr, cc), lambda l, i, c_ref: (l, c_ref[0], i, 0)),
                      pl.BlockSpec((None, tr, cc), lambda l, i, c_ref: (l, i, 0))],
            out_specs=pl.BlockSpec((None, tr, cc), lambda l, i, c_ref: (l, i, 0))),
        compiler_params=pltpu.CompilerParams(dimension_semantics=("parallel", "parallel"), vmem_limit_bytes=VMEM_LIMIT),
    )(cidx, parts.reshape(nl, 2, r2, cc), other)


def _exchange_comm(sums, entries):
    n = len(entries)

    def copies(ins, outs, sems):
        send_sems, recv_sems = sems
        x, y, c = lax.axis_index("x"), lax.axis_index("y"), lax.axis_index("c")
        chips = [(1 - x, y), (x, 1 - y), (1 - x, 1 - y)]
        slot = lambda chip: 2 * chip[0] + chip[1]

        def piece(wi, dest):
            si, mode, first, cols = entries[wi]
            if mode == "cols":
                return ins[si].at[0, :, pl.ds(pl.multiple_of(first + slot(dest) * cols, HEAD_DIM), cols)]
            return ins[si].at[slot(dest) if mode == "slot" else 0]

        def remote(wi, j, origin):
            return pltpu.make_async_remote_copy(
                src_ref=piece(wi, chips[j]), dst_ref=outs[wi].at[slot(origin)],
                send_sem=send_sems.at[3 * wi + j], recv_sem=recv_sems.at[3 * wi + j],
                device_id=(chips[j][0], chips[j][1], c), device_id_type=MESH)

        return remote, (x, y), chips

    def start(ins, outs, sems):
        remote, me, _ = copies(ins, outs, sems)
        for wi in range(n):
            for j in range(3):
                remote(wi, j, me).start()

    def finish(ins, outs, sems):
        remote, me, chips = copies(ins, outs, sems)
        for wi in range(n):
            for j in range(3):
                remote(wi, j, chips[j]).wait_recv()
        for wi in range(n):
            for j in range(3):
                remote(wi, j, me).wait_send()

    out_shape = [jax.ShapeDtypeStruct((4, sums[si].shape[1], cols), sums[si].dtype) for si, _, _, cols in entries]
    return _Comm(list(sums), out_shape, [pltpu.SemaphoreType.DMA((3 * n,))] * 2, start, finish)


def _sum_chips(got, own, mode, first, chip, name):
    _, r2, cc = got.shape
    tr = _row_tile(r2, cc)
    own_block = {"slot": lambda i, chip_ref: (chip_ref[0], i, 0), "same": lambda i, chip_ref: (0, i, 0),
                 "cols": lambda i, chip_ref: (0, i, first // cc + chip_ref[0])}[mode]

    def body(chip_ref, g_ref, own_ref, o_ref):
        term = lambda k: jnp.where(chip_ref[0] == k, _f(own_ref[...]), _f(g_ref[k]))
        o_ref[...] = ((term(0) + term(1)) + term(2)) + term(3)

    return pl.pallas_call(
        body, name=name, out_shape=jax.ShapeDtypeStruct((r2, cc), F32),
        grid_spec=pltpu.PrefetchScalarGridSpec(
            num_scalar_prefetch=1, grid=(r2 // tr,),
            in_specs=[pl.BlockSpec((4, tr, cc), lambda i, chip_ref: (0, i, 0)),
                      pl.BlockSpec((None, tr, cc), own_block)],
            out_specs=pl.BlockSpec((tr, cc), lambda i, chip_ref: (i, 0))),
        compiler_params=pltpu.CompilerParams(dimension_semantics=("parallel",), vmem_limit_bytes=VMEM_LIMIT),
    )(chip, got, own)


def _pair_gather(halves, name):
    n = len(halves)

    def body(*refs):
        ins, outs = refs[:n], refs[n:2 * n]
        send_sems, recv_sems = refs[2 * n:]
        x, y, c = lax.axis_index("x"), lax.axis_index("y"), lax.axis_index("c")
        copies = [pltpu.make_async_remote_copy(
            src_ref=ins[wi], dst_ref=outs[wi], send_sem=send_sems.at[wi], recv_sem=recv_sems.at[wi],
            device_id=(x, y, 1 - c), device_id_type=MESH) for wi in range(n)]
        for cp in copies:
            cp.start()
        for cp in copies:
            cp.wait()

    return pl.pallas_call(
        body, name=name, in_specs=[ANY] * n, out_specs=[ANY] * n,
        out_shape=[jax.ShapeDtypeStruct(s.shape, s.dtype) for s in halves],
        scratch_shapes=[pltpu.SemaphoreType.DMA((n,))] * 2, compiler_params=pltpu.CompilerParams(has_side_effects=True),
    )(*halves)


def _adamw(mine, other, cidx, w, m, v, name):
    r, c = w.shape
    tr = _row_tile(r // 2, c, 1 << 18)
    nh = (r // 2) // tr

    def body(c_ref, mine_ref, other_ref, w_ref, m_ref, v_ref, g_ref, d_ref, nm_ref, nv_ref):
        g = jnp.where(pl.program_id(0) // nh == c_ref[0], mine_ref[...], other_ref[...])
        g_ref[...] = g
        m_new = ADAM_B1 * m_ref[...] + (1.0 - ADAM_B1) * g
        v_new = ADAM_B2 * v_ref[...] + (1.0 - ADAM_B2) * jnp.square(g)
        m_hat = m_new / (1.0 - ADAM_B1 ** ADAM_STEP)
        v_hat = v_new / (1.0 - ADAM_B2 ** ADAM_STEP)
        d_ref[...] = -ADAM_LR * (m_hat / (jnp.sqrt(v_hat) + ADAM_EPS) + ADAM_WD * w_ref[...])
        nm_ref[...] = m_new
        nv_ref[...] = v_new

    blk = pl.BlockSpec((tr, c), lambda i, c_ref: (i, 0))
    mine_spec = pl.BlockSpec((tr, c), lambda i, c_ref: (jnp.where(i // nh == c_ref[0], i % nh, 0), 0))
    other_spec = pl.BlockSpec((tr, c), lambda i, c_ref: (jnp.where(i // nh == c_ref[0], 0, i % nh), 0))
    return pl.pallas_call(
        body, name=name, out_shape=[jax.ShapeDtypeStruct((r, c), F32)] * 4,
        grid_spec=pltpu.PrefetchScalarGridSpec(
            num_scalar_prefetch=1, grid=(r // tr,), in_specs=[mine_spec, other_spec, blk, blk, blk], out_specs=[blk] * 4),
        compiler_params=pltpu.CompilerParams(dimension_semantics=("arbitrary",), vmem_limit_bytes=VMEM_LIMIT),
    )(cidx, mine, other, w, m, v)


BIG = ["ffn1_w_gate", "ffn1_w_up", "ffn1_w_down", "w_in", "gla_gate_up", "w_branch_fox", "w_branch_gla", "w_merge_gate", "w_out",
       "ffn2_w_gate", "ffn2_w_up", "ffn2_w_down", "w_ple_proj", "w_ple_gate"]
ROW_SHARDED = ("ffn1_w_down", "w_out", "ffn2_w_down", "w_ple_gate")
FUSED = {"ffn1_w_gate": ("gu1", 0, 2), "ffn1_w_up": ("gu1", 1, 2), "ffn2_w_gate": ("gu2", 0, 2), "ffn2_w_up": ("gu2", 1, 2)}
SMALL = ["ffn1_norm", "mix_norm", "fox_forget_bias", "gla_gate_bias", "gla_head_norm", "b_merge_gate", "ffn2_norm", "ple_norm", "final_norm"]
NAMES = ["ffn1_norm", "ffn1_w_gate", "ffn1_w_up", "ffn1_w_down", "mix_norm", "w_in", "fox_forget_bias", "gla_gate_up", "gla_gate_bias",
         "gla_head_norm", "w_branch_fox", "w_branch_gla", "w_merge_gate", "b_merge_gate", "w_out", "ffn2_norm", "ffn2_w_gate", "ffn2_w_up",
         "ffn2_w_down", "ple_norm", "w_ple_proj", "w_ple_gate", "final_norm"]
W_IN_COLS = (FOX_W, FOX_W, FOX_W, FOX_HEADS, GLA_KW, GLA_KW, GLA_VW, GLA_VW, GLA_RANK)
SMALL_ROWS, SMALL_COLS = 16, 1024


def _shard_parts(full, name):
    if name in ROW_SHARDED:
        return full.reshape(4, full.shape[0] // 4, full.shape[1])
    return jnp.transpose(full.reshape(full.shape[0], 4, full.shape[1] // 4), (1, 0, 2))


def _in_layout(w_in, w_merge_gate):
    offs = [0]
    for cw in W_IN_COLS:
        offs.append(offs[-1] + cw)
    col = lambda i: w_in[:, offs[i]:offs[i + 1]]
    big = jnp.concatenate([col(0), col(1), col(2), col(4), col(5), col(6), col(7), w_merge_gate], axis=1)
    sm = jnp.concatenate([col(3), col(8), jnp.zeros((D_MODEL, SMALL_W - FOX_HEADS - GLA_RANK), BF16)], axis=1)
    return big, sm


def _pad_lanes(a, width):
    return jnp.pad(a, ((0, 0), (0, width - a.shape[1])))


def kernel(x, p, ffn1_norm, ffn1_w_gate, ffn1_w_up, ffn1_w_down, mix_norm, w_in, fox_forget_bias, gla_gate_up, gla_gate_bias, gla_head_norm, w_branch_fox, w_branch_gla, w_merge_gate, b_merge_gate, w_out, ffn2_norm, ffn2_w_gate, ffn2_w_up, ffn2_w_down, ple_norm, w_ple_proj, w_ple_gate, final_norm, loss_target, m_ffn1_norm, m_ffn1_w_gate, m_ffn1_w_up, m_ffn1_w_down, m_mix_norm, m_w_in, m_fox_forget_bias, m_gla_gate_up, m_gla_gate_bias, m_gla_head_norm, m_w_branch_fox, m_w_branch_gla, m_w_merge_gate, m_b_merge_gate, m_w_out, m_ffn2_norm, m_ffn2_w_gate, m_ffn2_w_up, m_ffn2_w_down, m_ple_norm, m_w_ple_proj, m_w_ple_gate, m_final_norm, v_ffn1_norm, v_ffn1_w_gate, v_ffn1_w_up, v_ffn1_w_down, v_mix_norm, v_w_in, v_fox_forget_bias, v_gla_gate_up, v_gla_gate_bias, v_gla_head_norm, v_w_branch_fox, v_w_branch_gla, v_w_merge_gate, v_b_merge_gate, v_w_out, v_ffn2_norm, v_ffn2_w_gate, v_ffn2_w_up, v_ffn2_w_down, v_ple_norm, v_w_ple_proj, v_w_ple_gate, v_final_norm):
    args = dict(locals())
    wts = {n: args[n] for n in NAMES}
    mom = {n: args["m_" + n] for n in NAMES}
    var = {n: args["v_" + n] for n in NAMES}
    two_d = lambda a: a.reshape(-1, a.shape[-1])

    wire = lambda n: F32 if n == "gla_gate_up" else BF16
    cidx = lax.axis_index("c").astype(jnp.int32).reshape(1)
    chip = (2 * lax.axis_index("x") + lax.axis_index("y")).astype(jnp.int32)
    shards = {n: two_d(wts[n]).astype(wire(n)) for n in BIG}
    sp = {
        "ffn1_norm": two_d(ffn1_norm), "mix_norm": two_d(mix_norm), "fb": _pad_lanes(two_d(fox_forget_bias), SMALL_W),
        "gb": two_d(gla_gate_bias), "ghn": two_d(gla_head_norm), "bm": two_d(b_merge_gate), "ffn2_norm": two_d(ffn2_norm),
        "ple_norm": two_d(ple_norm), "final_norm": two_d(final_norm),
    }

    loss, grad_x, reduced, ds_ = _local_step(x[0], p[0, 0], loss_target[0], shards, sp, cidx, chip)

    small_g = {"ffn1_norm": ds_["ffn1_norm"], "mix_norm": ds_["mix_norm"], "fox_forget_bias": ds_["fb"][:, :FOX_HEADS],
               "gla_gate_bias": ds_["gb"], "gla_head_norm": ds_["ghn"], "b_merge_gate": ds_["bm"], "ffn2_norm": ds_["ffn2_norm"],
               "ple_norm": ds_["ple_norm"], "final_norm": ds_["final_norm"]}
    small_w = sum(two_d(wts[n]).shape[1] for n in SMALL)
    assert small_w <= SMALL_ROWS * SMALL_COLS
    packed = lambda d: _pad_lanes(jnp.concatenate([two_d(d[n]) for n in SMALL], axis=1), SMALL_ROWS * SMALL_COLS).reshape(SMALL_ROWS, SMALL_COLS)
    parts = [packed(small_g)[None]]
    pair_sums = [_sum_half(a, b, cidx, "sum_half") for a, b in zip(parts, _pair_swap(parts, "pair_swap"))]
    reduced["small"] = (_run_comm(_exchange_comm(pair_sums, [(0, "same", 0, SMALL_COLS)]), "chip_exchange")[0], pair_sums[0], "same", 0)
    mine = [_sum_chips(*reduced[n], chip.reshape(1), "sum_chips") for n in BIG + ["small"]]
    other = _pair_gather(mine, "pair_gather")

    out = {}
    for n, a, b in zip(BIG, mine[:-1], other[:-1]):
        out[n] = [r.reshape(wts[n].shape) for r in _adamw(a, b, cidx, two_d(wts[n]), two_d(mom[n]), two_d(var[n]), "adamw_" + n)]
    small_out = [r.reshape(1, SMALL_ROWS * SMALL_COLS) for r in _adamw(mine[-1], other[-1], cidx, packed(wts), packed(mom), packed(var), "adamw_small")]
    off = 0
    for n in SMALL:
        cw = two_d(wts[n]).shape[1]
        out[n] = [r[:, off:off + cw].reshape(wts[n].shape) for r in small_out]
        off += cw

    total = lax.psum(loss[0, 0], ("x", "y", "c"))
    return (total, grad_x[None], *[out[n][0] for n in NAMES], *[out[n][1] for n in NAMES],
            *[out[n][2] for n in NAMES], *[out[n][3] for n in NAMES])
```

```python
import functools

import jax
import jax.numpy as jnp
from jax import lax
from jax.experimental import pallas as pl
from jax.experimental.pallas import tpu as pltpu

F32 = jnp.float32
BF16 = jnp.bfloat16
MESH = pl.DeviceIdType.MESH
ANY = pl.BlockSpec(memory_space=pl.ANY)

D_MODEL = 2048
FOX_HEADS = 8
HEAD_DIM = 128
GLA_HEADS = 4
GLA_VAL_DIM = 256
GLA_RANK = 16
GLA_TAU = 16.0
CHUNK = 64
EPS = 1e-6
FOX_W = FOX_HEADS * HEAD_DIM
GLA_KW = GLA_HEADS * HEAD_DIM
GLA_VW = GLA_HEADS * GLA_VAL_DIM
Z_FQ, Z_FK, Z_FV, Z_GQ, Z_GK, Z_GV, Z_GR, Z_GL = 0, 1024, 2048, 3072, 3584, 4096, 5120, 6144
Z_W = Z_GL + 2 * D_MODEL
SMALL_W = 128
NEG = -1e30

ADAM_LR, ADAM_B1, ADAM_B2, ADAM_EPS, ADAM_WD, ADAM_STEP = 0.001, 0.9, 0.999, 1e-08, 0.01, 10

VMEM_LIMIT = 56 * 1024 * 1024
BOUNCE_BYTES = 2 * 1024 * 1024


def _pick(n, target, mult=128):
    if n <= target:
        return n
    best = None
    for d in range(mult, target + 1, mult):
        if n % d == 0:
            best = d
    assert best is not None, (n, target)
    return best


def _mm(a, b, *, ta=False, tb=False, out_dtype=BF16, name, comm=None):
    m, k = (a.shape[1], a.shape[0]) if ta else a.shape
    n = b.shape[0] if tb else b.shape[1]
    assert (b.shape[1] if tb else b.shape[0]) == k
    bk = _pick(k, 2048)
    nk = k // bk
    bm, bn = _pick(m, 1024), _pick(n, 1024 if nk > 1 else 512)
    dims = (((0 if ta else 1,), (1 if tb else 0,)), ((), ()))

    def body(a_ref, b_ref, o_ref, acc_ref):
        part = lax.dot_general(a_ref[...], b_ref[...], dims, preferred_element_type=F32)
        if nk == 1:
            o_ref[...] = part.astype(o_ref.dtype)
            return
        kk = pl.program_id(2)

        @pl.when(kk == 0)
        def _():
            acc_ref[...] = part

        @pl.when(kk > 0)
        def _():
            acc_ref[...] += part

        @pl.when(kk == nk - 1)
        def _():
            o_ref[...] = acc_ref[...].astype(o_ref.dtype)

    a_spec = pl.BlockSpec((bk, bm), lambda i, j, kk: (kk, i)) if ta else pl.BlockSpec((bm, bk), lambda i, j, kk: (i, kk))
    b_spec = pl.BlockSpec((bn, bk), lambda i, j, kk: (j, kk)) if tb else pl.BlockSpec((bk, bn), lambda i, j, kk: (kk, j))
    (out,), travelled = _hosted(
        body, comm, name=name, grid=(m // bm, n // bn, nk),
        in_specs=[a_spec, b_spec], out_specs=[pl.BlockSpec((bm, bn), lambda i, j, kk: (i, j))],
        out_shape=[jax.ShapeDtypeStruct((m, n), out_dtype)], scratch_shapes=[pltpu.VMEM((bm, bn), F32)],
        semantics=("parallel", "parallel", "arbitrary"), args=(a, b))
    return out if comm is None else (out, travelled)


def _rowwise(fn, tiled, bcast, outs, reds=(), *, tt, name):
    t = tiled[0][0].shape[0]
    tt = min(tt, t)
    nin, nout = len(tiled) + len(bcast), len(outs)
    splits = [s[3] for s in tiled] + [s[1] for s in bcast]

    def store(ref, val, acc):
        off = 0
        for piece in val if isinstance(val, (tuple, list)) else (val,):
            w = piece.shape[-1]
            if acc:
                ref[:, off:off + w] += piece.astype(ref.dtype)
            else:
                ref[:, off:off + w] = piece.astype(ref.dtype)
            off += w
        assert off == ref.shape[-1], (name, off, ref.shape)

    def body(*refs):
        args = []
        for ref, sp in zip(refs[:nin], splits):
            if sp is None:
                args.append(ref[...])
            else:
                off = 0
                for w in sp:
                    args.append(ref[:, off:off + w])
                    off += w
        res = fn(*args)
        res = res if isinstance(res, (tuple, list)) else (res,)
        assert len(res) == nout + len(reds), (name, len(res))
        for ref, val in zip(refs[nin:nin + nout], res[:nout]):
            store(ref, val, False)
        if reds:
            @pl.when(pl.program_id(0) == 0)
            def _():
                for ref in refs[nin + nout:]:
                    ref[...] = jnp.zeros(ref.shape, ref.dtype)
            for ref, val in zip(refs[nin + nout:], res[nout:]):
                store(ref, val, True)

    in_specs = [pl.BlockSpec((tt, w), functools.partial(lambda i, cb: (i, cb), cb=cb)) for (_, w, cb, _) in tiled]
    in_specs += [pl.BlockSpec(arr.shape, lambda i: (0, 0)) for (arr, _) in bcast]
    out_specs = [pl.BlockSpec((tt, w), lambda i: (i, 0)) for (w, _) in outs]
    out_specs += [pl.BlockSpec((r, w), lambda i: (0, 0)) for (r, w) in reds]
    out_shape = [jax.ShapeDtypeStruct((t, w), dt) for (w, dt) in outs] + [jax.ShapeDtypeStruct((r, w), F32) for (r, w) in reds]
    return pl.pallas_call(
        body, name=name, grid=(t // tt,), in_specs=in_specs, out_specs=out_specs, out_shape=out_shape,
        compiler_params=pltpu.CompilerParams(dimension_semantics=("arbitrary" if reds else "parallel",), vmem_limit_bytes=VMEM_LIMIT),
    )(*[s[0] for s in tiled], *[s[0] for s in bcast])


def _full(arr):
    return (arr, arr.shape[1], 0, None)


def _f(x):
    return x.astype(F32)


def _rms(x, g):
    return x * lax.rsqrt(jnp.mean(x * x, axis=-1, keepdims=True) + EPS) * g


def _log_sigmoid(x):
    return jnp.minimum(x, 0.0) - jnp.log1p(jnp.exp(-jnp.abs(x)))


def _silu(x):
    return x * jax.nn.sigmoid(x)


def _norm_fwd(x, g, name):
    return _rowwise(lambda xb, gb: _rms(_f(xb), gb), [_full(x)], [(g, None)], [(x.shape[1], BF16)], tt=256, name=name)[0]


def _resnorm_fwd(res, branch, g, coef, name):
    def fn(rb, bb, gb):
        h = rb + coef * _f(bb)
        return h, _rms(h, gb)
    d = res.shape[1]
    return _rowwise(fn, [_full(res), _full(branch)], [(g, None)], [(d, F32), (d, BF16)], tt=256, name=name)


def _norm_bwd(h, dns, dres, g, coef, name):
    nd = len(dns)

    def fn(hb, *rest):
        dn = _f(rest[0])
        for extra in rest[1:nd]:
            dn = dn + _f(extra)
        dr, gb = rest[nd], rest[nd + 1]
        _, vjp = jax.vjp(_rms, hb, gb)
        dh, dg = vjp(dn)
        dh = dh + dr
        return dh, coef * dh, dg
    d = h.shape[1]
    return _rowwise(fn, [_full(h)] + [_full(x) for x in dns] + [_full(dres)], [(g, None)],
                    [(d, F32), (d, BF16)], [(1, d)], tt=256, name=name)


def _act_fwd(gu, name):
    ff = gu.shape[1] // 2
    return _rowwise(lambda gb, ub: _silu(_f(gb)) * _f(ub), [(gu, 2 * ff, 0, (ff, ff))], [], [(ff, BF16)], tt=256, name=name)[0]


def _act_bwd(gu, da, name):
    ff = gu.shape[1] // 2

    def fn(gb, ub, dab):
        _, vjp = jax.vjp(lambda p, q: _silu(p) * q, _f(gb), _f(ub))
        return (vjp(_f(dab)),)
    return _rowwise(fn, [(gu, 2 * ff, 0, (ff, ff)), _full(da)], [], [(2 * ff, BF16)], tt=128, name=name)[0]


def _merge(glf, glg, bf, bg, bmf, bmg):
    return jax.nn.sigmoid(_f(glf) + bmf) * _f(bf) + jax.nn.sigmoid(_f(glg) + bmg) * _f(bg)


def _merge_fwd(z, bf, bg, bm, name):
    d = D_MODEL
    return _rowwise(_merge, [(z, d, Z_GL // d, None), (z, d, Z_GL // d + 1, None), _full(bf), _full(bg)], [(bm, (d, d))],
                    [(d, BF16)], tt=256, name=name)[0]


def _merge_bwd(z, bf, bg, bm, dm, name):
    d = D_MODEL

    def fn(glf, glg, bfb, bgb, dmb, bmf, bmg):
        _, vjp = jax.vjp(_merge, _f(glf), _f(glg), _f(bfb), _f(bgb), bmf, bmg)
        dglf, dglg, dbf, dbg, dbmf, dbmg = vjp(_f(dmb))
        return (dglf, dglg), dbf, dbg, (dbmf, dbmg)
    return _rowwise(fn, [(z, d, Z_GL // d, None), (z, d, Z_GL // d + 1, None), _full(bf), _full(bg), _full(dm)], [(bm, (d, d))],
                    [(2 * d, BF16), (d, BF16), (d, BF16)], [(1, 2 * d)], tt=128, name=name)


def _gla_out(o, gr, g):
    return _rms(o, g) * _silu(_f(gr))


_PER_HEAD = (GLA_VAL_DIM,) * GLA_HEADS


def _gla_out_fwd(o, z, g, name):
    nh = GLA_HEADS

    def fn(*blocks):
        return (tuple(_gla_out(blocks[h], blocks[nh + h], blocks[2 * nh]) for h in range(nh)),)
    return _rowwise(fn, [(o, GLA_VW, 0, _PER_HEAD), (z, GLA_VW, Z_GR // GLA_VW, _PER_HEAD)], [(g, None)], [(GLA_VW, BF16)], tt=256, name=name)[0]


def _gla_out_bwd(o, z, g, dy, name):
    nh = GLA_HEADS

    def fn(*blocks):
        gb = blocks[3 * nh]
        grads = []
        for h in range(nh):
            _, vjp = jax.vjp(_gla_out, blocks[h], _f(blocks[nh + h]), gb)
            grads.append(vjp(_f(blocks[2 * nh + h])))
        dg = grads[0][2]
        for h in range(1, nh):
            dg = dg + grads[h][2]
        return tuple(gr[0] for gr in grads), tuple(gr[1] for gr in grads), dg
    return _rowwise(fn, [(o, GLA_VW, 0, _PER_HEAD), (z, GLA_VW, Z_GR // GLA_VW, _PER_HEAD), (dy, GLA_VW, 0, _PER_HEAD)], [(g, None)],
                    [(GLA_VW, F32), (GLA_VW, BF16)], [(1, GLA_VAL_DIM)], tt=256, name=name)


def _small_gates(s, fb, gup, gb):
    lane = lax.broadcasted_iota(jnp.int32, s.shape, 1)
    lf = jnp.where(lane < FOX_HEADS, _log_sigmoid(s + fb), 0.0)
    pre = jnp.dot(s.astype(BF16), gup.astype(BF16), preferred_element_type=F32) + gb
    return lf, _log_sigmoid(pre) / GLA_TAU


def _small_fwd(s, fb, gup, gb, name):
    return _rowwise(_small_gates, [_full(s)], [(fb, None), (gup, None), (gb, None)], [(SMALL_W, F32), (GLA_KW, F32)], tt=256, name=name)


def _small_bwd(s, fb, gup, gb, dlf, dla, name):
    def fn(sb, dlfb, dlab, fbb, gupb, gbb):
        _, vjp = jax.vjp(_small_gates, sb, fbb, gupb, gbb)
        return vjp((dlfb, dlab))
    return _rowwise(fn, [_full(s), _full(dlf), _full(dla)], [(fb, None), (gup, None), (gb, None)],
                    [(SMALL_W, BF16)], [(1, SMALL_W), (SMALL_W, GLA_KW), (1, GLA_KW)], tt=256, name=name)


def _head_fn(h3, pgl, pp, tgt, gf):
    h4 = h3 + jax.nn.sigmoid(pgl) * pp
    err = _rms(h4, gf) - tgt
    return 0.5 * jnp.sum(jnp.mean(err * err, axis=-1, keepdims=True))


def _head(h3, pgl, pp, tgt, gf, name):
    def fn(hb, gl, pb, tb, gfb):
        loss, vjp = jax.vjp(_head_fn, hb, _f(gl), _f(pb), tb, gfb)
        dh, dgl, dpp, _, dgf = vjp(jnp.ones((), F32))
        return dh, dgl, dpp, jnp.full((1, 128), loss, F32), dgf
    d = h3.shape[1]
    return _rowwise(fn, [_full(h3), _full(pgl), _full(pp), _full(tgt)], [(gf, None)],
                    [(d, F32), (d, BF16), (d, BF16)], [(1, 128), (1, d)], tt=256, name=name)


def _cumsum_tokens(a, reverse, name):
    t, w = a.shape
    r = min(256, t)
    nb = t // r

    def body(a_ref, o_ref, carry_ref):
        @pl.when(pl.program_id(0) == 0)
        def _():
            carry_ref[...] = jnp.zeros(carry_ref.shape, F32)
        row = lax.broadcasted_iota(jnp.int32, (r, r), 0)
        col = lax.broadcasted_iota(jnp.int32, (r, r), 1)
        tri = ((col >= row) if reverse else (col <= row)).astype(F32)
        blk = a_ref[...]
        o_ref[...] = jnp.dot(tri, blk, preferred_element_type=F32, precision=lax.Precision.HIGHEST) + carry_ref[...]
        carry_ref[...] += jnp.sum(blk, axis=0, keepdims=True)

    idx = (lambda i: (nb - 1 - i, 0)) if reverse else (lambda i: (i, 0))
    return pl.pallas_call(
        body, name=name, grid=(nb,), in_specs=[pl.BlockSpec((r, w), idx)], out_specs=pl.BlockSpec((r, w), idx),
        out_shape=jax.ShapeDtypeStruct((t, w), F32), scratch_shapes=[pltpu.VMEM((1, w), F32)],
        compiler_params=pltpu.CompilerParams(dimension_semantics=("arbitrary",)),
    )(a)


FOX_TQ, FOX_TK = 256, 512
FOX_SCALE = HEAD_DIM ** -0.5


def _fox_tiles(t):
    tq, tk = min(FOX_TQ, t), min(FOX_TK, t)
    return tq, tk, t // tq, t // tk


def _blocked_t(a, blk):
    return a.reshape(a.shape[0] // blk, blk, a.shape[1]).transpose(0, 2, 1)


def _unblocked_t(b):
    return b.transpose(0, 2, 1).reshape(b.shape[0] * b.shape[2], b.shape[1])


def _fox_scores(k, qt, frep, i, j, masked):
    tk, tq = k.shape[0], qt.shape[1]
    st = jnp.dot(k, qt, preferred_element_type=F32) * FOX_SCALE - jnp.tile(frep, (1, tq // HEAD_DIM))
    if masked:
        key = j * tk + lax.broadcasted_iota(jnp.int32, (tk, tq), 0)
        query = i * tq + lax.broadcasted_iota(jnp.int32, (tk, tq), 1)
        st = jnp.where(key <= query, st, NEG)
    return st


def _fox_fwd(z, qt, vt, frep, name, comm=None):
    t = z.shape[0]
    tq, tk, nq, nk = _fox_tiles(t)
    kb = Z_FK // HEAD_DIM

    def body(qt_ref, k_ref, vt_ref, frep_ref, ot_ref, lse_ref):
        i = pl.program_id(1)
        qt = qt_ref[...]
        last = ((i + 1) * tq - 1) // tk

        def block(j, carry, masked):
            m, l, acc = carry
            rows = pl.ds(pl.multiple_of(j * tk, tk), tk)
            st = _fox_scores(k_ref[rows, :], qt, frep_ref[rows, :], i, j, masked)
            m_new = jnp.maximum(m, jnp.max(st, axis=0, keepdims=True))
            alpha = jnp.exp(m - m_new)
            p = jnp.exp(st - m_new)
            l = alpha * l + jnp.sum(p, axis=0, keepdims=True)
            acc = alpha * acc + jnp.dot(vt_ref[j], p.astype(BF16), preferred_element_type=F32)
            return m_new, l, acc

        init = (jnp.full((1, tq), NEG, F32), jnp.zeros((1, tq), F32), jnp.zeros((HEAD_DIM, tq), F32))
        m, l, acc = block(last, lax.fori_loop(0, last, lambda j, c: block(j, c, False), init), True)
        ot_ref[...] = (acc / l).astype(ot_ref.dtype)
        lse_ref[...] = m + jnp.log(l)

    stat = pl.BlockSpec((None, None, 1, tq), lambda h, i: (h, i, 0, 0))
    (ot, lse), travelled = _hosted(
        body, comm, name=name, grid=(FOX_HEADS, nq),
        in_specs=[pl.BlockSpec((None, HEAD_DIM, tq), lambda h, i: (i, h, 0)),
                  pl.BlockSpec((t, HEAD_DIM), lambda h, i: (0, kb + h)),
                  pl.BlockSpec((nk, HEAD_DIM, tk), lambda h, i: (0, h, 0)),
                  pl.BlockSpec((None, t, HEAD_DIM), lambda h, i: (h, 0, 0))],
        out_specs=[pl.BlockSpec((None, HEAD_DIM, tq), lambda h, i: (i, h, 0)), stat],
        out_shape=[jax.ShapeDtypeStruct((nq, FOX_W, tq), BF16), jax.ShapeDtypeStruct((FOX_HEADS, nq, 1, tq), F32)],
        scratch_shapes=[], semantics=("parallel", "parallel"), args=(qt, z, vt, frep))
    return ot, lse, travelled


def _fox_bwd_q(z, qt, kt, ot, dot, lse, frep, name):
    t = z.shape[0]
    tq, tk, nq, nk = _fox_tiles(t)
    kb, vb = Z_FK // HEAD_DIM, Z_FV // HEAD_DIM

    def body(qt_ref, k_ref, kt_ref, v_ref, ot_ref, dot_ref, lse_ref, frep_ref, dqt_ref, delta_ref, dfq_ref):
        i = pl.program_id(1)
        qt, dot = qt_ref[...], dot_ref[...]
        lse = lse_ref[...]
        delta = jnp.sum(_f(dot) * _f(ot_ref[...]), axis=0, keepdims=True)
        delta_ref[...] = delta
        last = ((i + 1) * tq - 1) // tk

        def block(j, carry, masked):
            dq, dfq = carry
            rows = pl.ds(pl.multiple_of(j * tk, tk), tk)
            p = jnp.exp(_fox_scores(k_ref[rows, :], qt, frep_ref[rows, :], i, j, masked) - lse)
            dp = jnp.dot(v_ref[rows, :], dot, preferred_element_type=F32)
            ds = p * (dp - delta)
            return dq + jnp.dot(kt_ref[j], ds.astype(BF16), preferred_element_type=F32), dfq + jnp.sum(ds, axis=0, keepdims=True)

        init = (jnp.zeros((HEAD_DIM, tq), F32), jnp.zeros((1, tq), F32))
        dq, dfq = block(last, lax.fori_loop(0, last, lambda j, c: block(j, c, False), init), True)
        dqt_ref[...] = (dq * FOX_SCALE).astype(dqt_ref.dtype)
        dfq_ref[...] = dfq

    mine = pl.BlockSpec((None, HEAD_DIM, tq), lambda h, i: (i, h, 0))
    stat = pl.BlockSpec((None, None, 1, tq), lambda h, i: (h, i, 0, 0))
    return pl.pallas_call(
        body, name=name, grid=(FOX_HEADS, nq),
        in_specs=[mine,
                  pl.BlockSpec((t, HEAD_DIM), lambda h, i: (0, kb + h)),
                  pl.BlockSpec((nk, HEAD_DIM, tk), lambda h, i: (0, h, 0)),
                  pl.BlockSpec((t, HEAD_DIM), lambda h, i: (0, vb + h)),
                  mine, mine, stat,
                  pl.BlockSpec((None, t, HEAD_DIM), lambda h, i: (h, 0, 0))],
        out_specs=[mine, stat, stat],
        out_shape=[jax.ShapeDtypeStruct((nq, FOX_W, tq), BF16), jax.ShapeDtypeStruct((FOX_HEADS, nq, 1, tq), F32),
                   jax.ShapeDtypeStruct((FOX_HEADS, nq, 1, tq), F32)],
        compiler_params=pltpu.CompilerParams(dimension_semantics=("parallel", "parallel"), vmem_limit_bytes=VMEM_LIMIT),
    )(qt, z, kt, z, ot, dot, lse, frep)


def _fox_bwd_kv(z, qt, do, dot, lse, delta, frep, name):
    t = z.shape[0]
    tq, tk, nq, nk = _fox_tiles(t)
    qb, kb, vb = Z_FQ // HEAD_DIM, Z_FK // HEAD_DIM, Z_FV // HEAD_DIM
    per = tk // tq

    def body(k_ref, v_ref, frep_ref, q_ref, qt_ref, do_ref, dot_ref, lse_ref, delta_ref, dk_ref, dv_ref, dfk_ref):
        j = pl.program_id(1)
        k, v, frep = k_ref[...], v_ref[...], frep_ref[...]

        def block(i, carry, masked):
            dk, dv, dfk = carry
            rows = pl.ds(pl.multiple_of(i * tq, tq), tq)
            p = jnp.exp(_fox_scores(k, qt_ref[i], frep, i, j, masked) - lse_ref[i])
            dv = dv + jnp.dot(p.astype(BF16), do_ref[rows, :], preferred_element_type=F32)
            dp = jnp.dot(v, dot_ref[i], preferred_element_type=F32)
            ds = p * (dp - delta_ref[i])
            dk = dk + jnp.dot(ds.astype(BF16), q_ref[rows, :], preferred_element_type=F32)
            for part in range(tq // HEAD_DIM):
                dfk = dfk + ds[:, part * HEAD_DIM:(part + 1) * HEAD_DIM]
            return dk, dv, dfk

        zero = jnp.zeros((tk, HEAD_DIM), F32)
        carry = (zero, zero, zero)
        for step in range(per):
            carry = block(j * per + step, carry, True)
        dk, dv, dfk = lax.fori_loop((j + 1) * per, nq, lambda i, c: block(i, c, False), carry)
        dk_ref[...] = (dk * FOX_SCALE).astype(dk_ref.dtype)
        dv_ref[...] = dv.astype(dv_ref.dtype)
        dfk_ref[...] = jnp.sum(dfk, axis=1, keepdims=True)

    whole_t = pl.BlockSpec((nq, HEAD_DIM, tq), lambda h, j: (0, h, 0))
    whole_stat = pl.BlockSpec((None, nq, 1, tq), lambda h, j: (h, 0, 0, 0))
    return pl.pallas_call(
        body, name=name, grid=(FOX_HEADS, nk),
        in_specs=[pl.BlockSpec((tk, HEAD_DIM), lambda h, j: (j, kb + h)),
                  pl.BlockSpec((tk, HEAD_DIM), lambda h, j: (j, vb + h)),
                  pl.BlockSpec((None, tk, HEAD_DIM), lambda h, j: (h, j, 0)),
                  pl.BlockSpec((t, HEAD_DIM), lambda h, j: (0, qb + h)),
                  whole_t,
                  pl.BlockSpec((t, HEAD_DIM), lambda h, j: (0, h)),
                  whole_t, whole_stat, whole_stat],
        out_specs=[pl.BlockSpec((tk, HEAD_DIM), lambda h, j: (j, h)), pl.BlockSpec((tk, HEAD_DIM), lambda h, j: (j, h)),
                   pl.BlockSpec((None, tk, 1), lambda h, j: (h, j, 0))],
        out_shape=[jax.ShapeDtypeStruct((t, FOX_W), BF16), jax.ShapeDtypeStruct((t, FOX_W), BF16),
                   jax.ShapeDtypeStruct((FOX_HEADS, t, 1), F32)],
        compiler_params=pltpu.CompilerParams(dimension_semantics=("parallel", "parallel"), vmem_limit_bytes=VMEM_LIMIT),
    )(z, z, frep, z, qt, do, dot, lse, delta)


def _gla_step(st, q, k, v, la):
    row = lax.broadcasted_iota(jnp.int32, (CHUNK, CHUNK), 0)
    col = lax.broadcasted_iota(jnp.int32, (CHUNK, CHUNK), 1)
    tri = (col <= row).astype(F32)
    a_cum = jnp.dot(tri, la, preferred_element_type=F32, precision=lax.Precision.HIGHEST)
    a_tot = jnp.sum(la, axis=0, keepdims=True)
    k_dec = (_f(k) * jnp.exp(a_tot - a_cum)).astype(BF16)
    qs = (_f(q) * (HEAD_DIM ** -0.5)).astype(BF16)
    st = st * jnp.exp(a_tot) + lax.dot_general(v.astype(BF16), k_dec, (((0,), (0,)), ((), ())), preferred_element_type=F32)
    o = lax.dot_general(qs, st.astype(BF16), (((1,), (1,)), ((), ())), preferred_element_type=F32)
    return st, o


def _gla_blocks(t):
    r = min(256, t)
    return r, t // r, r // CHUNK


def _gla_fwd(z, la, name):
    t = z.shape[0]
    r, nb, nch = _gla_blocks(t)

    def body(q_ref, k_ref, v_ref, la_ref, o_ref, sp_ref, st_ref):
        @pl.when(pl.program_id(0) == 0)
        def _():
            st_ref[...] = jnp.zeros(st_ref.shape, F32)
        for c in range(nch):
            rows = slice(c * CHUNK, (c + 1) * CHUNK)
            for h in range(GLA_HEADS):
                kc = slice(h * HEAD_DIM, (h + 1) * HEAD_DIM)
                vc = slice(h * GLA_VAL_DIM, (h + 1) * GLA_VAL_DIM)
                st = st_ref[h]
                sp_ref[c, h] = st
                st, o = _gla_step(st, q_ref[rows, kc], k_ref[rows, kc], v_ref[rows, vc], la_ref[rows, kc])
                st_ref[h] = st
                o_ref[rows, vc] = o

    return pl.pallas_call(
        body, name=name, grid=(nb,),
        in_specs=[pl.BlockSpec((r, GLA_KW), lambda i: (i, Z_GQ // GLA_KW)), pl.BlockSpec((r, GLA_KW), lambda i: (i, Z_GK // GLA_KW)),
                  pl.BlockSpec((r, GLA_VW), lambda i: (i, Z_GV // GLA_VW)), pl.BlockSpec((r, GLA_KW), lambda i: (i, 0))],
        out_specs=[pl.BlockSpec((r, GLA_VW), lambda i: (i, 0)),
                   pl.BlockSpec((nch, GLA_HEADS, GLA_VAL_DIM, HEAD_DIM), lambda i: (i, 0, 0, 0))],
        out_shape=[jax.ShapeDtypeStruct((t, GLA_VW), F32),
                   jax.ShapeDtypeStruct((t // CHUNK, GLA_HEADS, GLA_VAL_DIM, HEAD_DIM), F32)],
        scratch_shapes=[pltpu.VMEM((GLA_HEADS, GLA_VAL_DIM, HEAD_DIM), F32)],
        compiler_params=pltpu.CompilerParams(dimension_semantics=("arbitrary",), vmem_limit_bytes=VMEM_LIMIT),
    )(z, z, z, la)


def _gla_bwd(z, la, sprev, do, name):
    t = z.shape[0]
    r, nb, nch = _gla_blocks(t)

    def body(q_ref, k_ref, v_ref, la_ref, sp_ref, do_ref, dq_ref, dk_ref, dv_ref, dla_ref, dst_ref):
        @pl.when(pl.program_id(0) == 0)
        def _():
            dst_ref[...] = jnp.zeros(dst_ref.shape, F32)
        for c in reversed(range(nch)):
            rows = slice(c * CHUNK, (c + 1) * CHUNK)
            for h in range(GLA_HEADS):
                kc = slice(h * HEAD_DIM, (h + 1) * HEAD_DIM)
                vc = slice(h * GLA_VAL_DIM, (h + 1) * GLA_VAL_DIM)
                _, vjp = jax.vjp(_gla_step, sp_ref[c, h], q_ref[rows, kc], k_ref[rows, kc], v_ref[rows, vc], la_ref[rows, kc])
                dst, dq, dk, dv, dla = vjp((dst_ref[h], do_ref[rows, vc]))
                dst_ref[h] = dst
                dq_ref[rows, kc] = dq
                dk_ref[rows, kc] = dk
                dv_ref[rows, vc] = dv
                dla_ref[rows, kc] = dla

    rev = lambda i: (nb - 1 - i, 0)
    return pl.pallas_call(
        body, name=name, grid=(nb,),
        in_specs=[pl.BlockSpec((r, GLA_KW), lambda i: (nb - 1 - i, Z_GQ // GLA_KW)), pl.BlockSpec((r, GLA_KW), lambda i: (nb - 1 - i, Z_GK // GLA_KW)),
                  pl.BlockSpec((r, GLA_VW), lambda i: (nb - 1 - i, Z_GV // GLA_VW)), pl.BlockSpec((r, GLA_KW), rev),
                  pl.BlockSpec((nch, GLA_HEADS, GLA_VAL_DIM, HEAD_DIM), lambda i: (nb - 1 - i, 0, 0, 0)),
                  pl.BlockSpec((r, GLA_VW), rev)],
        out_specs=[pl.BlockSpec((r, GLA_KW), rev), pl.BlockSpec((r, GLA_KW), rev), pl.BlockSpec((r, GLA_VW), rev), pl.BlockSpec((r, GLA_KW), rev)],
        out_shape=[jax.ShapeDtypeStruct((t, GLA_KW), BF16), jax.ShapeDtypeStruct((t, GLA_KW), BF16),
                   jax.ShapeDtypeStruct((t, GLA_VW), BF16), jax.ShapeDtypeStruct((t, GLA_KW), F32)],
        scratch_shapes=[pltpu.VMEM((GLA_HEADS, GLA_VAL_DIM, HEAD_DIM), F32)],
        compiler_params=pltpu.CompilerParams(dimension_semantics=("arbitrary",), vmem_limit_bytes=VMEM_LIMIT),
    )(z, z, z, la, sprev, do)


def _local_step(x, p, tgt, shards, sp, cidx, chip):
    t = x.shape[0]
    tq, tk, _, _ = _fox_tiles(t)
    full, reduced = {}, {}

    def plan(names):
        keys, shapes, places = [], [], []
        for n in names:
            r, cc = shards[n].shape
            key, part, parts = FUSED.get(n, (n, 0, 1))
            if key not in keys:
                keys.append(key)
                stacked = n in ROW_SHARDED or n == "w_in"
                shapes.append(jax.ShapeDtypeStruct((4 * r, cc) if stacked else (r, 4 * cc * parts), shards[n].dtype))
            places.append((keys.index(key), r, 0, 0) if n in ROW_SHARDED or n == "w_in" else (keys.index(key), 0, part * 4 * cc, cc))
        return keys, shapes, places

    def gather(names):
        _, shapes, places = plan(names)
        return _ag_comm([shards[n] for n in names], shapes, places)

    def landed(names, got):
        keys, _, _ = plan(names)
        for key, g in zip(keys, got):
            full[key] = g

    def pair_sums(grads):
        parts, entries = [], []
        for g, names in grads:
            cols = g.shape[1] // (4 * len(names))
            if names[0] in ROW_SHARDED or names[0] == "w_in":
                parts.append(g if g.ndim == 3 else _shard_parts(g, names[0]))
                entries.append((names[0], len(parts) - 1, "slot", 0, parts[-1].shape[2]))
            else:
                parts.append(g[None])
                entries += [(n, len(parts) - 1, "cols", k * 4 * cols, cols) for k, n in enumerate(names)]
        swapped = _pair_swap(parts, "pair_swap")
        return entries, [_sum_half(a, b, cidx, "sum_half") for a, b in zip(parts, swapped)]

    def exchange(entries, sums):
        return _exchange_comm(sums, [e[1:] for e in entries])

    def exchanged(entries, sums, got):
        for (n, si, mode, first, _), g in zip(entries, got):
            reduced[n] = (g, sums[si], mode, first)

    first = ["ffn1_w_gate", "ffn1_w_up"]
    landed(first, _run_comm(gather(first), "all_gather"))
    w_gu1 = full["gu1"]
    n1 = _norm_fwd(x, sp["ffn1_norm"], "norm1_fwd")
    names = ["ffn1_w_down", "w_in", "gla_gate_up"]
    gu1, got = _mm(n1, w_gu1, name="mm_gu_gather", comm=gather(names))
    landed(names, got)
    a1 = _act_fwd(gu1, "act_fwd")
    names = ["w_merge_gate", "w_branch_fox", "w_branch_gla"]
    f1, got = _mm(a1, full["ffn1_w_down"], out_dtype=F32, name="mm_down_gather", comm=gather(names))
    landed(names, got)
    w_big, w_sm = _in_layout(full["w_in"], full["w_merge_gate"])
    gup = jnp.zeros((SMALL_W, GLA_KW), F32).at[FOX_HEADS:FOX_HEADS + GLA_RANK].set(full["gla_gate_up"])
    h1, u = _resnorm_fwd(x, f1, sp["mix_norm"], 0.5, "resnorm_fwd_half")
    names = ["w_out", "ffn2_w_down", "w_ple_proj", "w_ple_gate"]
    z, got = _mm(u, w_big, name="mm_in_gather", comm=gather(names))
    landed(names, got)
    s = _mm(u, w_sm, out_dtype=F32, name="mm_in_small")
    lf, la = _small_fwd(s, sp["fb"], gup, sp["gb"], "small_fwd")
    fp = _cumsum_tokens(lf, False, "cumsum_fwd")
    frep = jnp.broadcast_to(fp[:, :FOX_HEADS].T[:, :, None], (FOX_HEADS, t, HEAD_DIM))
    qt = _blocked_t(z[:, Z_FQ:Z_FQ + FOX_W], tq)
    kt = _blocked_t(z[:, Z_FK:Z_FK + FOX_W], tk)
    vt = _blocked_t(z[:, Z_FV:Z_FV + FOX_W], tk)
    names = ["ffn2_w_gate", "ffn2_w_up"]
    ot, lse, got = _fox_fwd(z, qt, vt, frep, "fox_fwd_gather", comm=gather(names))
    landed(names, got)
    w_gu2 = full["gu2"]
    y_fox = _unblocked_t(ot)
    o_gla, sprev = _gla_fwd(z, la, "gla_fwd")
    y_gla = _gla_out_fwd(o_gla, z, sp["ghn"], "gla_out_fwd")
    bf = _mm(y_fox, full["w_branch_fox"], name="mm_branch")
    bg = _mm(y_gla, full["w_branch_gla"], name="mm_branch")
    merged = _merge_fwd(z, bf, bg, sp["bm"], "merge_fwd")
    mo = _mm(merged, full["w_out"], out_dtype=F32, name="mm_out")
    h2, n2 = _resnorm_fwd(h1, mo, sp["ffn2_norm"], 1.0, "resnorm_fwd_one")
    gu2 = _mm(n2, w_gu2, name="mm_gu")
    a2 = _act_fwd(gu2, "act_fwd")
    f2 = _mm(a2, full["ffn2_w_down"], out_dtype=F32, name="mm_down")
    h3, n4 = _resnorm_fwd(h2, f2, sp["ple_norm"], 0.5, "resnorm_fwd_half")
    pgl = _mm(n4, full["w_ple_gate"], name="mm_pg")
    pb = p.astype(BF16)
    pp = _mm(pb, full["w_ple_proj"], name="mm_pp")

    dh3, dpgl, dpp, loss, d_final = _head(h3, pgl, pp, tgt, sp["final_norm"], "head")
    ds_ = {"final_norm": d_final}
    entries, sums = pair_sums([(_mm(n4, dpgl, ta=True, name="mm_dw_sq"), ["w_ple_gate"]), (_mm(pb, dpp, ta=True, name="mm_dw_pp"), ["w_ple_proj"])])
    dn4, got = _mm(dpgl, full["w_ple_gate"], tb=True, out_dtype=F32, name="mm_dx_sq_f32_exchange", comm=exchange(entries, sums))
    exchanged(entries, sums, got)
    dh3, df2, ds_["ple_norm"] = _norm_bwd(h3, [dn4], dh3, sp["ple_norm"], 0.5, "norm_bwd_1")

    def ffn_bwd(n, gu, a, df, wgu, wd, which):
        entries, sums = pair_sums([(_mm(a, df, ta=True, name="mm_dw_down"), [which + "_w_down"])])
        da, got = _mm(df, wd, tb=True, name="mm_dx_down_exchange", comm=exchange(entries, sums))
        exchanged(entries, sums, got)
        dgu = _act_bwd(gu, da, "act_bwd")
        entries, sums = pair_sums([(_mm(n, dgu, ta=True, name="mm_dw_gu"), [which + "_w_gate", which + "_w_up"])])
        dn, got = _mm(dgu, wgu, tb=True, out_dtype=F32, name="mm_dx_gu_exchange", comm=exchange(entries, sums))
        exchanged(entries, sums, got)
        return dn

    dn2 = ffn_bwd(n2, gu2, a2, df2, w_gu2, full["ffn2_w_down"], "ffn2")
    dh2, dmix, ds_["ffn2_norm"] = _norm_bwd(h2, [dn2], dh3, sp["ffn2_norm"], 1.0, "norm_bwd_1")

    dw_out = _mm(merged, dmix, ta=True, name="mm_dw_sq")
    dmerged = _mm(dmix, full["w_out"], tb=True, name="mm_dx_sq")
    dgl, dbf, dbg, ds_["bm"] = _merge_bwd(z, bf, bg, sp["bm"], dmerged, "merge_bwd")
    mix_entries, mix_sums = pair_sums([(dw_out, ["w_out"]), (_mm(y_fox, dbf, ta=True, name="mm_dw_branch"), ["w_branch_fox"]),
                                       (_mm(y_gla, dbg, ta=True, name="mm_dw_branch"), ["w_branch_gla"])])
    dy_fox = _mm(dbf, full["w_branch_fox"], tb=True, name="mm_dx_branch")
    dy_gla = _mm(dbg, full["w_branch_gla"], tb=True, name="mm_dx_branch")

    do_gla, dgr, ds_["ghn"] = _gla_out_bwd(o_gla, z, sp["ghn"], dy_gla, "gla_out_bwd")
    dgq, dgk, dgv, dla = _gla_bwd(z, la, sprev, do_gla, "gla_bwd")
    dot = _blocked_t(dy_fox, tq)
    dqt, delta, df_query = _fox_bwd_q(z, qt, kt, ot, dot, lse, frep, "fox_bwd_q")
    dfq = _unblocked_t(dqt)
    dfk, dfv, df_key = _fox_bwd_kv(z, qt, dy_fox, dot, lse, delta, frep, "fox_bwd_kv")
    df = df_query.reshape(FOX_HEADS, t) - df_key.reshape(FOX_HEADS, t)
    dfp = jnp.pad(df.T, ((0, 0), (0, SMALL_W - FOX_HEADS)))
    dlf = _cumsum_tokens(dfp, True, "cumsum_bwd")
    dsm, ds_["fb"], dgup, ds_["gb"] = _small_bwd(s, sp["fb"], gup, sp["gb"], dlf, dla, "small_bwd")
    dz = jnp.concatenate([dfq, dfk, dfv, dgq, dgk, dgv, dgr, dgl], axis=1)
    dw_big, got = _mm(u, dz, ta=True, name="mm_dw_in_exchange", comm=exchange(mix_entries, mix_sums))
    exchanged(mix_entries, mix_sums, got)
    dw_sm = _mm(u, dsm, ta=True, out_dtype=F32, name="mm_dw_in_small").astype(BF16)
    entries, sums = pair_sums([(_w_in_parts(dw_big, dw_sm), ["w_in"]), (dw_big[:, Z_GL:], ["w_merge_gate"]), (dgup[FOX_HEADS:FOX_HEADS + GLA_RANK], ["gla_gate_up"])])
    du1, got = _mm(dz, w_big, tb=True, out_dtype=F32, name="mm_dx_in_exchange", comm=exchange(entries, sums))
    exchanged(entries, sums, got)
    du2 = _mm(dsm, w_sm, tb=True, out_dtype=F32, name="mm_dx_in_small")
    dh1, df1, ds_["mix_norm"] = _norm_bwd(h1, [du1, du2], dh2, sp["mix_norm"], 0.5, "norm_bwd_2")

    dn1 = ffn_bwd(n1, gu1, a1, df1, w_gu1, full["ffn1_w_down"], "ffn1")
    grad_x, _, ds_["ffn1_norm"] = _norm_bwd(x, [dn1], dh1, sp["ffn1_norm"], 1.0, "norm_bwd_1")
    return loss, grad_x, reduced, ds_


def _half_rows(ref, which):
    r2 = ref.shape[0] // 2
    return ref.at[pl.ds(pl.multiple_of(which * r2, r2), r2)]


class _Comm:
    def __init__(self, ins, out_shape, sems, start, finish):
        self.ins, self.out_shape, self.sems, self.start, self.finish = ins, out_shape, sems, start, finish


def _run_comm(comm, name):
    n_in, n_out = len(comm.ins), len(comm.out_shape)

    def body(*refs):
        parts = refs[:n_in], refs[n_in:n_in + n_out], refs[n_in + n_out:]
        comm.start(*parts)
        comm.finish(*parts)

    return pl.pallas_call(
        body, name=name, in_specs=[ANY] * n_in, out_specs=[ANY] * n_out, out_shape=comm.out_shape,
        scratch_shapes=comm.sems, compiler_params=pltpu.CompilerParams(has_side_effects=True),
    )(*comm.ins)


def _hosted(body, comm, *, name, grid, in_specs, out_specs, out_shape, scratch_shapes, semantics, args):
    if comm is None:
        res = pl.pallas_call(
            body, name=name, grid=grid, in_specs=in_specs, out_specs=out_specs, out_shape=out_shape, scratch_shapes=scratch_shapes,
            compiler_params=pltpu.CompilerParams(dimension_semantics=semantics, vmem_limit_bytes=VMEM_LIMIT),
        )(*args)
        return res, None
    ni, no, ns = len(in_specs), len(out_shape), len(scratch_shapes)
    ci, co = len(comm.ins), len(comm.out_shape)

    def wrapped(*refs):
        h_in, c_in = refs[:ni], refs[ni:ni + ci]
        h_out, c_out = refs[ni + ci:ni + ci + no], refs[ni + ci + no:ni + ci + no + co]
        h_scr, c_sem = refs[ni + ci + no + co:ni + ci + no + co + ns], refs[ni + ci + no + co + ns:]
        ids = [pl.program_id(axis) for axis in range(len(grid))]
        first = functools.reduce(jnp.logical_and, [i == 0 for i in ids])
        last = functools.reduce(jnp.logical_and, [i == g - 1 for i, g in zip(ids, grid)])

        @pl.when(first)
        def _():
            comm.start(c_in, c_out, c_sem)

        body(*h_in, *h_out, *h_scr)

        @pl.when(last)
        def _():
            comm.finish(c_in, c_out, c_sem)

    res = pl.pallas_call(
        wrapped, name=name, grid=grid, in_specs=list(in_specs) + [ANY] * ci, out_specs=list(out_specs) + [ANY] * co,
        out_shape=list(out_shape) + list(comm.out_shape), scratch_shapes=list(scratch_shapes) + list(comm.sems),
        compiler_params=pltpu.CompilerParams(dimension_semantics=("arbitrary",) * len(grid), vmem_limit_bytes=VMEM_LIMIT, has_side_effects=True),
    )(*args, *comm.ins)
    return res[:no], res[no:]


def _ag_comm(shards, out_shape, places):
    n = len(shards)

    def copies(ins, outs, sems):
        ici_send, ici_recv, d2d_send, d2d_recv = sems
        x, y, c = lax.axis_index("x"), lax.axis_index("y"), lax.axis_index("c")
        chips = [(1 - x, y), (x, 1 - y), (1 - x, 1 - y)]
        slot = lambda chip: 2 * chip[0] + chip[1]

        def window(wi, origin, half):
            out, row_step, col_base, col_step = places[wi]
            r, cc = shards[wi].shape
            rows = pl.ds(pl.multiple_of(slot(origin) * row_step + half * (r // 2), r // 2), r // 2)
            cols = pl.ds(pl.multiple_of(col_base + slot(origin) * col_step, HEAD_DIM), cc) if col_step else pl.ds(col_base, cc)
            return outs[out].at[rows, cols]

        def over_ici(wi, j, origin):
            return pltpu.make_async_remote_copy(
                src_ref=_half_rows(ins[wi], c), dst_ref=window(wi, origin, c),
                send_sem=ici_send.at[3 * wi + j], recv_sem=ici_recv.at[3 * wi + j],
                device_id=(chips[j][0], chips[j][1], c), device_id_type=MESH)

        def over_d2d(wi, j, half):
            place = window(wi, chips[j], half)
            return pltpu.make_async_remote_copy(
                src_ref=place, dst_ref=place, send_sem=d2d_send.at[3 * wi + j], recv_sem=d2d_recv.at[3 * wi + j],
                device_id=(x, y, 1 - c), device_id_type=MESH)

        return over_ici, over_d2d, (x, y), chips, c

    def chunk_rows(wi):
        r, cc = shards[wi].shape
        item = shards[wi].dtype.itemsize
        return _pick(r, max(32 // item, BOUNCE_BYTES // (cc * item)), 32 // item)

    def start(ins, outs, scratch):
        over_ici, _, me, _, _ = copies(ins, outs, scratch[:4])
        for wi in range(n):
            for j in range(3):
                over_ici(wi, j, me).start()
        loc_sems = scratch[4]
        for wi in range(n):
            out, row_step, col_base, col_step = places[wi]
            r, cc = shards[wi].shape
            rc = chunk_rows(wi)
            buf = scratch[5 + wi]
            slot = 2 * me[0] + me[1]
            cols = pl.ds(pl.multiple_of(col_base + slot * col_step, HEAD_DIM), cc) if col_step else pl.ds(col_base, cc)

            def load(k):
                return pltpu.make_async_copy(ins[wi].at[pl.ds(k * rc, rc)], buf.at[k % 2], loc_sems.at[2 * wi])

            def store(k):
                rows = pl.ds(pl.multiple_of(slot * row_step + k * rc, rc), rc)
                return pltpu.make_async_copy(buf.at[k % 2], outs[out].at[rows, cols], loc_sems.at[2 * wi + 1])

            load(0).start()
            for k in range(r // rc):
                load(k).wait()
                if k + 1 < r // rc:
                    load(k + 1).start()
                store(k).start()
                store(k).wait()

    def finish(ins, outs, scratch):
        over_ici, over_d2d, me, chips, c = copies(ins, outs, scratch[:4])
        for wi in range(n):
            for j in range(3):
                over_ici(wi, j, chips[j]).wait_recv()
                over_d2d(wi, j, c).start()
        for wi in range(n):
            for j in range(3):
                over_d2d(wi, j, 1 - c).wait_recv()
        for wi in range(n):
            for j in range(3):
                over_ici(wi, j, me).wait_send()
                over_d2d(wi, j, c).wait_send()

    bounce = [pltpu.VMEM((min(2, s.shape[0] // chunk_rows(wi)), chunk_rows(wi), s.shape[1]), s.dtype) for wi, s in enumerate(shards)]
    return _Comm(list(shards), list(out_shape), [pltpu.SemaphoreType.DMA((3 * n,))] * 4 + [pltpu.SemaphoreType.DMA((2 * n,))] + bounce,
                 start, finish)


def _pair_swap(parts, name):
    n = len(parts)

    def body(*refs):
        ins, outs = refs[:n], refs[n:2 * n]
        send_sems, recv_sems = refs[2 * n:]
        x, y, c = lax.axis_index("x"), lax.axis_index("y"), lax.axis_index("c")

        def swap(wi):
            r2 = parts[wi].shape[1] // 2
            return pltpu.make_async_remote_copy(
                src_ref=ins[wi].at[:, pl.ds(pl.multiple_of((1 - c) * r2, r2), r2)], dst_ref=outs[wi],
                send_sem=send_sems.at[wi], recv_sem=recv_sems.at[wi], device_id=(x, y, 1 - c), device_id_type=MESH)

        copies = [swap(wi) for wi in range(n)]
        for cp in copies:
            cp.start()
        for cp in copies:
            cp.wait()

    return pl.pallas_call(
        body, name=name, in_specs=[ANY] * n, out_specs=[ANY] * n,
        out_shape=[jax.ShapeDtypeStruct((s.shape[0], s.shape[1] // 2, s.shape[2]), s.dtype) for s in parts],
        scratch_shapes=[pltpu.SemaphoreType.DMA((n,))] * 2, compiler_params=pltpu.CompilerParams(has_side_effects=True),
    )(*parts)


def _row_tile(r, c, budget=1 << 19):
    return r if r <= 8 else _pick(r, max(8, budget // c), 8)


def _sum_half(parts, other, cidx, name):
    nl, r, cc = parts.shape
    r2 = r // 2
    tr = _row_tile(r2, cc)

    def body(c_ref, p_ref, q_ref, o_ref):
        o_ref[...] = (_f(p_ref[...]) + _f(q_ref[...])).astype(o_ref.dtype)

    return pl.pallas_call(
        body, name=name, out_shape=jax.ShapeDtypeStruct((nl, r2, cc), parts.dtype),
        grid_spec=pltpu.PrefetchScalarGridSpec(
            num_scalar_prefetch=1, grid=(nl, r2 // tr),
            in_specs=[pl.BlockSpec((None, None, tr, cc), lambda l, i, c_ref: (l, c_ref[0], i, 0)),
                      pl.BlockSpec((None, tr, cc), lambda l, i, c_ref: (l, i, 0))],
            out_specs=pl.BlockSpec((None, tr, cc), lambda l, i, c_ref: (l, i, 0))),
        compiler_params=pltpu.CompilerParams(dimension_semantics=("parallel", "parallel"), vmem_limit_bytes=VMEM_LIMIT),
    )(cidx, parts.reshape(nl, 2, r2, cc), other)


def _exchange_comm(sums, entries):
    n = len(entries)

    def copies(ins, outs, sems):
        send_sems, recv_sems = sems
        x, y, c = lax.axis_index("x"), lax.axis_index("y"), lax.axis_index("c")
        chips = [(1 - x, y), (x, 1 - y), (1 - x, 1 - y)]
        slot = lambda chip: 2 * chip[0] + chip[1]

        def piece(wi, dest):
            si, mode, first, cols = entries[wi]
            if mode == "cols":
                return ins[si].at[0, :, pl.ds(pl.multiple_of(first + slot(dest) * cols, HEAD_DIM), cols)]
            return ins[si].at[slot(dest) if mode == "slot" else 0]

        def remote(wi, j, origin):
            return pltpu.make_async_remote_copy(
                src_ref=piece(wi, chips[j]), dst_ref=outs[wi].at[slot(origin)],
                send_sem=send_sems.at[3 * wi + j], recv_sem=recv_sems.at[3 * wi + j],
                device_id=(chips[j][0], chips[j][1], c), device_id_type=MESH)

        return remote, (x, y), chips

    def start(ins, outs, sems):
        remote, me, _ = copies(ins, outs, sems)
        for wi in range(n):
            for j in range(3):
                remote(wi, j, me).start()

    def finish(ins, outs, sems):
        remote, me, chips = copies(ins, outs, sems)
        for wi in range(n):
            for j in range(3):
                remote(wi, j, chips[j]).wait_recv()
        for wi in range(n):
            for j in range(3):
                remote(wi, j, me).wait_send()

    out_shape = [jax.ShapeDtypeStruct((4, sums[si].shape[1], cols), sums[si].dtype) for si, _, _, cols in entries]
    return _Comm(list(sums), out_shape, [pltpu.SemaphoreType.DMA((3 * n,))] * 2, start, finish)


def _sum_chips(got, own, mode, first, chip, name):
    _, r2, cc = got.shape
    tr = _row_tile(r2, cc)
    own_block = {"slot": lambda i, chip_ref: (chip_ref[0], i, 0), "same": lambda i, chip_ref: (0, i, 0),
                 "cols": lambda i, chip_ref: (0, i, first // cc + chip_ref[0])}[mode]

    def body(chip_ref, g_ref, own_ref, o_ref):
        term = lambda k: jnp.where(chip_ref[0] == k, _f(own_ref[...]), _f(g_ref[k]))
        o_ref[...] = ((term(0) + term(1)) + term(2)) + term(3)

    return pl.pallas_call(
        body, name=name, out_shape=jax.ShapeDtypeStruct((r2, cc), F32),
        grid_spec=pltpu.PrefetchScalarGridSpec(
            num_scalar_prefetch=1, grid=(r2 // tr,),
            in_specs=[pl.BlockSpec((4, tr, cc), lambda i, chip_ref: (0, i, 0)),
                      pl.BlockSpec((None, tr, cc), own_block)],
            out_specs=pl.BlockSpec((tr, cc), lambda i, chip_ref: (i, 0))),
        compiler_params=pltpu.CompilerParams(dimension_semantics=("parallel",), vmem_limit_bytes=VMEM_LIMIT),
    )(chip, got, own)


def _pair_gather(halves, name):
    n = len(halves)

    def body(*refs):
        ins, outs = refs[:n], refs[n:2 * n]
        send_sems, recv_sems = refs[2 * n:]
        x, y, c = lax.axis_index("x"), lax.axis_index("y"), lax.axis_index("c")
        copies = [pltpu.make_async_remote_copy(
            src_ref=ins[wi], dst_ref=outs[wi], send_sem=send_sems.at[wi], recv_sem=recv_sems.at[wi],
            device_id=(x, y, 1 - c), device_id_type=MESH) for wi in range(n)]
        for cp in copies:
            cp.start()
        for cp in copies:
            cp.wait()

    return pl.pallas_call(
        body, name=name, in_specs=[ANY] * n, out_specs=[ANY] * n,
        out_shape=[jax.ShapeDtypeStruct(s.shape, s.dtype) for s in halves],
        scratch_shapes=[pltpu.SemaphoreType.DMA((n,))] * 2, compiler_params=pltpu.CompilerParams(has_side_effects=True),
    )(*halves)


def _adamw_update(g, w, m, v):
    m_new = ADAM_B1 * m + (1.0 - ADAM_B1) * g
    v_new = ADAM_B2 * v + (1.0 - ADAM_B2) * jnp.square(g)
    m_hat = m_new / (1.0 - ADAM_B1 ** ADAM_STEP)
    v_hat = v_new / (1.0 - ADAM_B2 ** ADAM_STEP)
    return -ADAM_LR * (m_hat / (jnp.sqrt(v_hat) + ADAM_EPS) + ADAM_WD * w), m_new, v_new


def _adamw_whole(g, w, m, v, name):
    r, c = w.shape
    tc = 256

    def body(g_ref, w_ref, m_ref, v_ref, d_ref, nm_ref, nv_ref):
        d_ref[...], nm_ref[...], nv_ref[...] = _adamw_update(g_ref[...], w_ref[...], m_ref[...], v_ref[...])

    blk = pl.BlockSpec((r, tc), lambda i: (0, i))
    return pl.pallas_call(
        body, name=name, grid=(c // tc,), in_specs=[blk] * 4, out_specs=[blk] * 3, out_shape=[jax.ShapeDtypeStruct((r, c), F32)] * 3,
        compiler_params=pltpu.CompilerParams(dimension_semantics=("parallel",), vmem_limit_bytes=VMEM_LIMIT),
    )(g, w, m, v)


def _adamw(mine, other, cidx, w, m, v, name):
    r, c = w.shape
    tr = _row_tile(r // 2, c, 1 << 18)
    nh = (r // 2) // tr

    def body(c_ref, mine_ref, other_ref, w_ref, m_ref, v_ref, g_ref, d_ref, nm_ref, nv_ref):
        g = jnp.where(pl.program_id(0) // nh == c_ref[0], mine_ref[...], other_ref[...])
        g_ref[...] = g
        d_ref[...], nm_ref[...], nv_ref[...] = _adamw_update(g, w_ref[...], m_ref[...], v_ref[...])

    blk = pl.BlockSpec((tr, c), lambda i, c_ref: (i, 0))
    mine_spec = pl.BlockSpec((tr, c), lambda i, c_ref: (jnp.where(i // nh == c_ref[0], i % nh, 0), 0))
    other_spec = pl.BlockSpec((tr, c), lambda i, c_ref: (jnp.where(i // nh == c_ref[0], 0, i % nh), 0))
    return pl.pallas_call(
        body, name=name, out_shape=[jax.ShapeDtypeStruct((r, c), F32)] * 4,
        grid_spec=pltpu.PrefetchScalarGridSpec(
            num_scalar_prefetch=1, grid=(r // tr,), in_specs=[mine_spec, other_spec, blk, blk, blk], out_specs=[blk] * 4),
        compiler_params=pltpu.CompilerParams(dimension_semantics=("arbitrary",), vmem_limit_bytes=VMEM_LIMIT),
    )(cidx, mine, other, w, m, v)


BIG = ["ffn1_w_gate", "ffn1_w_up", "ffn1_w_down", "w_in", "gla_gate_up", "w_branch_fox", "w_branch_gla", "w_merge_gate", "w_out",
       "ffn2_w_gate", "ffn2_w_up", "ffn2_w_down", "w_ple_proj", "w_ple_gate"]
ROW_SHARDED = ("ffn1_w_down", "w_out", "ffn2_w_down", "w_ple_gate")
FUSED = {"ffn1_w_gate": ("gu1", 0, 2), "ffn1_w_up": ("gu1", 1, 2), "ffn2_w_gate": ("gu2", 0, 2), "ffn2_w_up": ("gu2", 1, 2)}
SMALL = ["ffn1_norm", "mix_norm", "fox_forget_bias", "gla_gate_bias", "gla_head_norm", "b_merge_gate", "ffn2_norm", "ple_norm", "final_norm"]
NAMES = ["ffn1_norm", "ffn1_w_gate", "ffn1_w_up", "ffn1_w_down", "mix_norm", "w_in", "fox_forget_bias", "gla_gate_up", "gla_gate_bias",
         "gla_head_norm", "w_branch_fox", "w_branch_gla", "w_merge_gate", "b_merge_gate", "w_out", "ffn2_norm", "ffn2_w_gate", "ffn2_w_up",
         "ffn2_w_down", "ple_norm", "w_ple_proj", "w_ple_gate", "final_norm"]
W_IN_COLS = (FOX_W, FOX_W, FOX_W, FOX_HEADS, GLA_KW, GLA_KW, GLA_VW, GLA_VW, GLA_RANK)
SMALL_ROWS, SMALL_COLS = 16, 1024


def _shard_parts(full, name):
    if name in ROW_SHARDED:
        return full.reshape(4, full.shape[0] // 4, full.shape[1])
    return jnp.transpose(full.reshape(full.shape[0], 4, full.shape[1] // 4), (1, 0, 2))


W_IN_WIDE = ((0, 3 * FOX_W), (3 * FOX_W + FOX_HEADS, 3 * FOX_W + FOX_HEADS + 2 * GLA_KW + 2 * GLA_VW))
W_IN_NARROW = ((3 * FOX_W, 3 * FOX_W + FOX_HEADS), (sum(W_IN_COLS) - GLA_RANK, sum(W_IN_COLS)))


def _in_layout(stacked, w_merge_gate):
    per = stacked.shape[1]

    def columns(lo, hi):
        out = []
        while lo < hi:
            j, end = lo // per, min(hi, (lo // per + 1) * per)
            out.append(stacked[j * D_MODEL:(j + 1) * D_MODEL, lo - j * per:end - j * per])
            lo = end
        return out
    big = jnp.concatenate(columns(*W_IN_WIDE[0]) + columns(*W_IN_WIDE[1]) + [w_merge_gate], axis=1)
    sm = jnp.concatenate(columns(*W_IN_NARROW[0]) + columns(*W_IN_NARROW[1]) + [jnp.zeros((D_MODEL, SMALL_W - FOX_HEADS - GLA_RANK), BF16)], axis=1)
    return big, sm


def _w_in_parts(dw_big, dw_sm):
    runs = [(W_IN_WIDE[0], dw_big, Z_FQ), (W_IN_NARROW[0], dw_sm, 0), (W_IN_WIDE[1], dw_big, Z_GQ), (W_IN_NARROW[1], dw_sm, FOX_HEADS)]
    per = sum(W_IN_COLS) // 4
    parts = []
    for j in range(4):
        lo, hi, pieces = j * per, (j + 1) * per, []
        for (a, b), src, at in runs:
            if max(a, lo) < min(b, hi):
                pieces.append(src[:, at + max(a, lo) - a:at + min(b, hi) - a])
        parts.append(jnp.concatenate(pieces, axis=1))
    return jnp.stack(parts)


def _pad_lanes(a, width):
    return jnp.pad(a, ((0, 0), (0, width - a.shape[1])))


def kernel(x, p, ffn1_norm, ffn1_w_gate, ffn1_w_up, ffn1_w_down, mix_norm, w_in, fox_forget_bias, gla_gate_up, gla_gate_bias, gla_head_norm, w_branch_fox, w_branch_gla, w_merge_gate, b_merge_gate, w_out, ffn2_norm, ffn2_w_gate, ffn2_w_up, ffn2_w_down, ple_norm, w_ple_proj, w_ple_gate, final_norm, loss_target, m_ffn1_norm, m_ffn1_w_gate, m_ffn1_w_up, m_ffn1_w_down, m_mix_norm, m_w_in, m_fox_forget_bias, m_gla_gate_up, m_gla_gate_bias, m_gla_head_norm, m_w_branch_fox, m_w_branch_gla, m_w_merge_gate, m_b_merge_gate, m_w_out, m_ffn2_norm, m_ffn2_w_gate, m_ffn2_w_up, m_ffn2_w_down, m_ple_norm, m_w_ple_proj, m_w_ple_gate, m_final_norm, v_ffn1_norm, v_ffn1_w_gate, v_ffn1_w_up, v_ffn1_w_down, v_mix_norm, v_w_in, v_fox_forget_bias, v_gla_gate_up, v_gla_gate_bias, v_gla_head_norm, v_w_branch_fox, v_w_branch_gla, v_w_merge_gate, v_b_merge_gate, v_w_out, v_ffn2_norm, v_ffn2_w_gate, v_ffn2_w_up, v_ffn2_w_down, v_ple_norm, v_w_ple_proj, v_w_ple_gate, v_final_norm):
    args = dict(locals())
    wts = {n: args[n] for n in NAMES}
    mom = {n: args["m_" + n] for n in NAMES}
    var = {n: args["v_" + n] for n in NAMES}
    two_d = lambda a: a.reshape(-1, a.shape[-1])

    wire = lambda n: F32 if n == "gla_gate_up" else BF16
    cidx = lax.axis_index("c").astype(jnp.int32).reshape(1)
    chip = (2 * lax.axis_index("x") + lax.axis_index("y")).astype(jnp.int32)
    shards = {n: two_d(wts[n]).astype(wire(n)) for n in BIG}
    sp = {
        "ffn1_norm": two_d(ffn1_norm), "mix_norm": two_d(mix_norm), "fb": _pad_lanes(two_d(fox_forget_bias), SMALL_W),
        "gb": two_d(gla_gate_bias), "ghn": two_d(gla_head_norm), "bm": two_d(b_merge_gate), "ffn2_norm": two_d(ffn2_norm),
        "ple_norm": two_d(ple_norm), "final_norm": two_d(final_norm),
    }

    loss, grad_x, reduced, ds_ = _local_step(x[0], p[0, 0], loss_target[0], shards, sp, cidx, chip)

    small_g = {"ffn1_norm": ds_["ffn1_norm"], "mix_norm": ds_["mix_norm"], "fox_forget_bias": ds_["fb"][:, :FOX_HEADS],
               "gla_gate_bias": ds_["gb"], "gla_head_norm": ds_["ghn"], "b_merge_gate": ds_["bm"], "ffn2_norm": ds_["ffn2_norm"],
               "ple_norm": ds_["ple_norm"], "final_norm": ds_["final_norm"]}
    small_w = sum(two_d(wts[n]).shape[1] for n in SMALL)
    assert small_w <= SMALL_ROWS * SMALL_COLS
    packed = lambda d: _pad_lanes(jnp.concatenate([two_d(d[n]) for n in SMALL], axis=1), SMALL_ROWS * SMALL_COLS).reshape(SMALL_ROWS, SMALL_COLS)
    parts = [packed(small_g)[None]]
    pair_sums = [_sum_half(a, b, cidx, "sum_half") for a, b in zip(parts, _pair_swap(parts, "pair_swap"))]
    reduced["small"] = (_run_comm(_exchange_comm(pair_sums, [(0, "same", 0, SMALL_COLS)]), "chip_exchange")[0], pair_sums[0], "same", 0)
    mine = [_sum_chips(*reduced[n], chip.reshape(1), "sum_chips") for n in BIG + ["small"]]
    other = _pair_gather(mine, "pair_gather")

    out = {}
    for n, a, b in zip(BIG, mine[:-1], other[:-1]):
        if n == "w_in":
            g_t = jnp.where(cidx[0] == 0, jnp.concatenate([a, b]), jnp.concatenate([b, a])).T
            flip = lambda t: jnp.swapaxes(t, 1, 2)[0]
            res = _adamw_whole(g_t, flip(wts[n]), flip(mom[n]), flip(var[n]), "adamw_w_in")
            out[n] = [jnp.swapaxes(r[None], 1, 2) for r in [g_t, *res]]
            continue
        out[n] = [r.reshape(wts[n].shape) for r in _adamw(a, b, cidx, two_d(wts[n]), two_d(mom[n]), two_d(var[n]), "adamw_" + n)]
    small_out = [r.reshape(1, SMALL_ROWS * SMALL_COLS) for r in _adamw(mine[-1], other[-1], cidx, packed(wts), packed(mom), packed(var), "adamw_small")]
    off = 0
    for n in SMALL:
        cw = two_d(wts[n]).shape[1]
        out[n] = [r[:, off:off + cw].reshape(wts[n].shape) for r in small_out]
        off += cw

    total = lax.psum(loss[0, 0], ("x", "y", "c"))
    return (total, grad_x[None], *[out[n][0] for n in NAMES], *[out[n][1] for n in NAMES],
            *[out[n][2] for n in NAMES], *[out[n][3] for n in NAMES])
```

```python
import functools

import jax
import jax.numpy as jnp
from jax import lax
from jax.experimental import pallas as pl
from jax.experimental.pallas import tpu as pltpu

F32 = jnp.float32
BF16 = jnp.bfloat16
MESH = pl.DeviceIdType.MESH
ANY = pl.BlockSpec(memory_space=pl.ANY)

D_MODEL = 2048
FOX_HEADS = 8
HEAD_DIM = 128
GLA_HEADS = 4
GLA_VAL_DIM = 256
GLA_RANK = 16
GLA_TAU = 16.0
CHUNK = 64
EPS = 1e-6
FOX_W = FOX_HEADS * HEAD_DIM
GLA_KW = GLA_HEADS * HEAD_DIM
GLA_VW = GLA_HEADS * GLA_VAL_DIM
Z_FQ, Z_FK, Z_FV, Z_GQ, Z_GK, Z_GV, Z_GR, Z_GL = 0, 1024, 2048, 3072, 3584, 4096, 5120, 6144
Z_W = Z_GL + 2 * D_MODEL
SMALL_W = 128
NEG = -1e30

ADAM_LR, ADAM_B1, ADAM_B2, ADAM_EPS, ADAM_WD, ADAM_STEP = 0.001, 0.9, 0.999, 1e-08, 0.01, 10

VMEM_LIMIT = 56 * 1024 * 1024
BOUNCE_BYTES = 2 * 1024 * 1024


def _pick(n, target, mult=128):
    if n <= target:
        return n
    best = None
    for d in range(mult, target + 1, mult):
        if n % d == 0:
            best = d
    assert best is not None, (n, target)
    return best


def _mm(a, b, *, ta=False, tb=False, out_dtype=BF16, name, comm=None):
    m, k = (a.shape[1], a.shape[0]) if ta else a.shape
    n = b.shape[0] if tb else b.shape[1]
    assert (b.shape[1] if tb else b.shape[0]) == k
    bk = _pick(k, 2048)
    nk = k // bk
    bm, bn = _pick(m, 1024), _pick(n, 1024 if nk > 1 else 512)
    dims = (((0 if ta else 1,), (1 if tb else 0,)), ((), ()))

    def body(a_ref, b_ref, o_ref, acc_ref):
        part = lax.dot_general(a_ref[...], b_ref[...], dims, preferred_element_type=F32)
        if nk == 1:
            o_ref[...] = part.astype(o_ref.dtype)
            return
        kk = pl.program_id(2)

        @pl.when(kk == 0)
        def _():
            acc_ref[...] = part

        @pl.when(kk > 0)
        def _():
            acc_ref[...] += part

        @pl.when(kk == nk - 1)
        def _():
            o_ref[...] = acc_ref[...].astype(o_ref.dtype)

    a_spec = pl.BlockSpec((bk, bm), lambda i, j, kk: (kk, i)) if ta else pl.BlockSpec((bm, bk), lambda i, j, kk: (i, kk))
    b_spec = pl.BlockSpec((bn, bk), lambda i, j, kk: (j, kk)) if tb else pl.BlockSpec((bk, bn), lambda i, j, kk: (kk, j))
    (out,), travelled = _hosted(
        body, comm, name=name, grid=(m // bm, n // bn, nk),
        in_specs=[a_spec, b_spec], out_specs=[pl.BlockSpec((bm, bn), lambda i, j, kk: (i, j))],
        out_shape=[jax.ShapeDtypeStruct((m, n), out_dtype)], scratch_shapes=[pltpu.VMEM((bm, bn), F32)],
        semantics=("parallel", "parallel", "arbitrary"), args=(a, b))
    return out if comm is None else (out, travelled)


def _rowwise(fn, tiled, bcast, outs, reds=(), *, tt, name):
    t = tiled[0][0].shape[0]
    tt = min(tt, t)
    nin, nout = len(tiled) + len(bcast), len(outs)
    splits = [s[3] for s in tiled] + [s[1] for s in bcast]

    def store(ref, val, acc):
        off = 0
        for piece in val if isinstance(val, (tuple, list)) else (val,):
            w = piece.shape[-1]
            if acc:
                ref[:, off:off + w] += piece.astype(ref.dtype)
            else:
                ref[:, off:off + w] = piece.astype(ref.dtype)
            off += w
        assert off == ref.shape[-1], (name, off, ref.shape)

    def body(*refs):
        args = []
        for ref, sp in zip(refs[:nin], splits):
            if sp is None:
                args.append(ref[...])
            else:
                off = 0
                for w in sp:
                    args.append(ref[:, off:off + w])
                    off += w
        res = fn(*args)
        res = res if isinstance(res, (tuple, list)) else (res,)
        assert len(res) == nout + len(reds), (name, len(res))
        for ref, val in zip(refs[nin:nin + nout], res[:nout]):
            store(ref, val, False)
        if reds:
            @pl.when(pl.program_id(0) == 0)
            def _():
                for ref in refs[nin + nout:]:
                    ref[...] = jnp.zeros(ref.shape, ref.dtype)
            for ref, val in zip(refs[nin + nout:], res[nout:]):
                store(ref, val, True)

    in_specs = [pl.BlockSpec((tt, w), functools.partial(lambda i, cb: (i, cb), cb=cb)) for (_, w, cb, _) in tiled]
    in_specs += [pl.BlockSpec(arr.shape, lambda i: (0, 0)) for (arr, _) in bcast]
    out_specs = [pl.BlockSpec((tt, w), lambda i: (i, 0)) for (w, _) in outs]
    out_specs += [pl.BlockSpec((r, w), lambda i: (0, 0)) for (r, w) in reds]
    out_shape = [jax.ShapeDtypeStruct((t, w), dt) for (w, dt) in outs] + [jax.ShapeDtypeStruct((r, w), F32) for (r, w) in reds]
    return pl.pallas_call(
        body, name=name, grid=(t // tt,), in_specs=in_specs, out_specs=out_specs, out_shape=out_shape,
        compiler_params=pltpu.CompilerParams(dimension_semantics=("arbitrary" if reds else "parallel",), vmem_limit_bytes=VMEM_LIMIT),
    )(*[s[0] for s in tiled], *[s[0] for s in bcast])


def _full(arr):
    return (arr, arr.shape[1], 0, None)


def _f(x):
    return x.astype(F32)


def _rms(x, g):
    return x * lax.rsqrt(jnp.mean(x * x, axis=-1, keepdims=True) + EPS) * g


def _log_sigmoid(x):
    return jnp.minimum(x, 0.0) - jnp.log1p(jnp.exp(-jnp.abs(x)))


def _silu(x):
    return x * jax.nn.sigmoid(x)


def _norm_fwd(x, g, name):
    return _rowwise(lambda xb, gb: _rms(_f(xb), gb), [_full(x)], [(g, None)], [(x.shape[1], BF16)], tt=256, name=name)[0]


def _resnorm_fwd(res, branch, g, coef, name):
    def fn(rb, bb, gb):
        h = rb + coef * _f(bb)
        return h, _rms(h, gb)
    d = res.shape[1]
    return _rowwise(fn, [_full(res), _full(branch)], [(g, None)], [(d, F32), (d, BF16)], tt=256, name=name)


def _norm_bwd(h, dns, dres, g, coef, name):
    nd = len(dns)

    def fn(hb, *rest):
        dn = _f(rest[0])
        for extra in rest[1:nd]:
            dn = dn + _f(extra)
        dr, gb = rest[nd], rest[nd + 1]
        _, vjp = jax.vjp(_rms, hb, gb)
        dh, dg = vjp(dn)
        dh = dh + dr
        return dh, coef * dh, dg
    d = h.shape[1]
    return _rowwise(fn, [_full(h)] + [_full(x) for x in dns] + [_full(dres)], [(g, None)],
                    [(d, F32), (d, BF16)], [(1, d)], tt=256, name=name)


def _act_fwd(gu, name):
    ff = gu.shape[1] // 2
    return _rowwise(lambda gb, ub: _silu(_f(gb)) * _f(ub), [(gu, 2 * ff, 0, (ff, ff))], [], [(ff, BF16)], tt=256, name=name)[0]


def _act_bwd(gu, da, name):
    ff = gu.shape[1] // 2

    def fn(gb, ub, dab):
        _, vjp = jax.vjp(lambda p, q: _silu(p) * q, _f(gb), _f(ub))
        return (vjp(_f(dab)),)
    return _rowwise(fn, [(gu, 2 * ff, 0, (ff, ff)), _full(da)], [], [(2 * ff, BF16)], tt=128, name=name)[0]


def _merge(glf, glg, bf, bg, bmf, bmg):
    return jax.nn.sigmoid(_f(glf) + bmf) * _f(bf) + jax.nn.sigmoid(_f(glg) + bmg) * _f(bg)


def _merge_fwd(z, bf, bg, bm, name):
    d = D_MODEL
    return _rowwise(_merge, [(z, d, Z_GL // d, None), (z, d, Z_GL // d + 1, None), _full(bf), _full(bg)], [(bm, (d, d))],
                    [(d, BF16)], tt=256, name=name)[0]


def _merge_bwd(z, bf, bg, bm, dm, name):
    d = D_MODEL

    def fn(glf, glg, bfb, bgb, dmb, bmf, bmg):
        _, vjp = jax.vjp(_merge, _f(glf), _f(glg), _f(bfb), _f(bgb), bmf, bmg)
        dglf, dglg, dbf, dbg, dbmf, dbmg = vjp(_f(dmb))
        return (dglf, dglg), dbf, dbg, (dbmf, dbmg)
    return _rowwise(fn, [(z, d, Z_GL // d, None), (z, d, Z_GL // d + 1, None), _full(bf), _full(bg), _full(dm)], [(bm, (d, d))],
                    [(2 * d, BF16), (d, BF16), (d, BF16)], [(1, 2 * d)], tt=128, name=name)


def _gla_out(o, gr, g):
    return _rms(o, g) * _silu(_f(gr))


_PER_HEAD = (GLA_VAL_DIM,) * GLA_HEADS


def _gla_out_fwd(o, z, g, name):
    nh = GLA_HEADS

    def fn(*blocks):
        return (tuple(_gla_out(blocks[h], blocks[nh + h], blocks[2 * nh]) for h in range(nh)),)
    return _rowwise(fn, [(o, GLA_VW, 0, _PER_HEAD), (z, GLA_VW, Z_GR // GLA_VW, _PER_HEAD)], [(g, None)], [(GLA_VW, BF16)], tt=256, name=name)[0]


def _gla_out_bwd(o, z, g, dy, name):
    nh = GLA_HEADS

    def fn(*blocks):
        gb = blocks[3 * nh]
        grads = []
        for h in range(nh):
            _, vjp = jax.vjp(_gla_out, blocks[h], _f(blocks[nh + h]), gb)
            grads.append(vjp(_f(blocks[2 * nh + h])))
        dg = grads[0][2]
        for h in range(1, nh):
            dg = dg + grads[h][2]
        return tuple(gr[0] for gr in grads), tuple(gr[1] for gr in grads), dg
    return _rowwise(fn, [(o, GLA_VW, 0, _PER_HEAD), (z, GLA_VW, Z_GR // GLA_VW, _PER_HEAD), (dy, GLA_VW, 0, _PER_HEAD)], [(g, None)],
                    [(GLA_VW, F32), (GLA_VW, BF16)], [(1, GLA_VAL_DIM)], tt=256, name=name)


def _small_gates(s, fb, gup, gb):
    lane = lax.broadcasted_iota(jnp.int32, s.shape, 1)
    lf = jnp.where(lane < FOX_HEADS, _log_sigmoid(s + fb), 0.0)
    pre = jnp.dot(s.astype(BF16), gup.astype(BF16), preferred_element_type=F32) + gb
    return lf, _log_sigmoid(pre) / GLA_TAU


def _small_fwd(s, fb, gup, gb, name):
    return _rowwise(_small_gates, [_full(s)], [(fb, None), (gup, None), (gb, None)], [(SMALL_W, F32), (GLA_KW, F32)], tt=256, name=name)


def _small_bwd(s, fb, gup, gb, dlf, dla, name):
    def fn(sb, dlfb, dlab, fbb, gupb, gbb):
        _, vjp = jax.vjp(_small_gates, sb, fbb, gupb, gbb)
        return vjp((dlfb, dlab))
    return _rowwise(fn, [_full(s), _full(dlf), _full(dla)], [(fb, None), (gup, None), (gb, None)],
                    [(SMALL_W, BF16)], [(1, SMALL_W), (SMALL_W, GLA_KW), (1, GLA_KW)], tt=256, name=name)


def _head_fn(h3, pgl, pp, tgt, gf):
    h4 = h3 + jax.nn.sigmoid(pgl) * pp
    err = _rms(h4, gf) - tgt
    return 0.5 * jnp.sum(jnp.mean(err * err, axis=-1, keepdims=True))


def _head(h3, pgl, pp, tgt, gf, name):
    def fn(hb, gl, pb, tb, gfb):
        loss, vjp = jax.vjp(_head_fn, hb, _f(gl), _f(pb), tb, gfb)
        dh, dgl, dpp, _, dgf = vjp(jnp.ones((), F32))
        return dh, dgl, dpp, jnp.full((1, 128), loss, F32), dgf
    d = h3.shape[1]
    return _rowwise(fn, [_full(h3), _full(pgl), _full(pp), _full(tgt)], [(gf, None)],
                    [(d, F32), (d, BF16), (d, BF16)], [(1, 128), (1, d)], tt=256, name=name)


def _cumsum_tokens(a, reverse, name):
    t, w = a.shape
    r = min(256, t)
    nb = t // r

    def body(a_ref, o_ref, carry_ref):
        @pl.when(pl.program_id(0) == 0)
        def _():
            carry_ref[...] = jnp.zeros(carry_ref.shape, F32)
        row = lax.broadcasted_iota(jnp.int32, (r, r), 0)
        col = lax.broadcasted_iota(jnp.int32, (r, r), 1)
        tri = ((col >= row) if reverse else (col <= row)).astype(F32)
        blk = a_ref[...]
        o_ref[...] = jnp.dot(tri, blk, preferred_element_type=F32, precision=lax.Precision.HIGHEST) + carry_ref[...]
        carry_ref[...] += jnp.sum(blk, axis=0, keepdims=True)

    idx = (lambda i: (nb - 1 - i, 0)) if reverse else (lambda i: (i, 0))
    return pl.pallas_call(
        body, name=name, grid=(nb,), in_specs=[pl.BlockSpec((r, w), idx)], out_specs=pl.BlockSpec((r, w), idx),
        out_shape=jax.ShapeDtypeStruct((t, w), F32), scratch_shapes=[pltpu.VMEM((1, w), F32)],
        compiler_params=pltpu.CompilerParams(dimension_semantics=("arbitrary",)),
    )(a)


FOX_TQ, FOX_TK = 256, 512
FOX_SCALE = HEAD_DIM ** -0.5


def _fox_tiles(t):
    tq, tk = min(FOX_TQ, t), min(FOX_TK, t)
    return tq, tk, t // tq, t // tk


def _blocked_t(a, blk):
    return a.reshape(a.shape[0] // blk, blk, a.shape[1]).transpose(0, 2, 1)


def _unblocked_t(b):
    return b.transpose(0, 2, 1).reshape(b.shape[0] * b.shape[2], b.shape[1])


def _fox_scores(k, qt, frep, i, j, masked):
    tk, tq = k.shape[0], qt.shape[1]
    st = jnp.dot(k, qt, preferred_element_type=F32) * FOX_SCALE - jnp.tile(frep, (1, tq // HEAD_DIM))
    if masked:
        key = j * tk + lax.broadcasted_iota(jnp.int32, (tk, tq), 0)
        query = i * tq + lax.broadcasted_iota(jnp.int32, (tk, tq), 1)
        st = jnp.where(key <= query, st, NEG)
    return st


def _fox_fwd(z, qt, vt, frep, name, comm=None):
    t = z.shape[0]
    tq, tk, nq, nk = _fox_tiles(t)
    kb = Z_FK // HEAD_DIM

    def body(qt_ref, k_ref, vt_ref, frep_ref, ot_ref, lse_ref):
        i = pl.program_id(1)
        qt = qt_ref[...]
        last = ((i + 1) * tq - 1) // tk

        def block(j, carry, masked):
            m, l, acc = carry
            rows = pl.ds(pl.multiple_of(j * tk, tk), tk)
            st = _fox_scores(k_ref[rows, :], qt, frep_ref[rows, :], i, j, masked)
            m_new = jnp.maximum(m, jnp.max(st, axis=0, keepdims=True))
            alpha = jnp.exp(m - m_new)
            p = jnp.exp(st - m_new)
            l = alpha * l + jnp.sum(p, axis=0, keepdims=True)
            acc = alpha * acc + jnp.dot(vt_ref[j], p.astype(BF16), preferred_element_type=F32)
            return m_new, l, acc

        init = (jnp.full((1, tq), NEG, F32), jnp.zeros((1, tq), F32), jnp.zeros((HEAD_DIM, tq), F32))
        m, l, acc = block(last, lax.fori_loop(0, last, lambda j, c: block(j, c, False), init), True)
        ot_ref[...] = (acc / l).astype(ot_ref.dtype)
        lse_ref[...] = m + jnp.log(l)

    stat = pl.BlockSpec((None, None, 1, tq), lambda h, i: (h, i, 0, 0))
    (ot, lse), travelled = _hosted(
        body, comm, name=name, grid=(FOX_HEADS, nq),
        in_specs=[pl.BlockSpec((None, HEAD_DIM, tq), lambda h, i: (i, h, 0)),
                  pl.BlockSpec((t, HEAD_DIM), lambda h, i: (0, kb + h)),
                  pl.BlockSpec((nk, HEAD_DIM, tk), lambda h, i: (0, h, 0)),
                  pl.BlockSpec((None, t, HEAD_DIM), lambda h, i: (h, 0, 0))],
        out_specs=[pl.BlockSpec((None, HEAD_DIM, tq), lambda h, i: (i, h, 0)), stat],
        out_shape=[jax.ShapeDtypeStruct((nq, FOX_W, tq), BF16), jax.ShapeDtypeStruct((FOX_HEADS, nq, 1, tq), F32)],
        scratch_shapes=[], semantics=("parallel", "parallel"), args=(qt, z, vt, frep))
    return ot, lse, travelled


def _fox_bwd_q(z, qt, kt, ot, dot, lse, frep, name):
    t = z.shape[0]
    tq, tk, nq, nk = _fox_tiles(t)
    kb, vb = Z_FK // HEAD_DIM, Z_FV // HEAD_DIM

    def body(qt_ref, k_ref, kt_ref, v_ref, ot_ref, dot_ref, lse_ref, frep_ref, dqt_ref, delta_ref, dfq_ref):
        i = pl.program_id(1)
        qt, dot = qt_ref[...], dot_ref[...]
        lse = lse_ref[...]
        delta = jnp.sum(_f(dot) * _f(ot_ref[...]), axis=0, keepdims=True)
        delta_ref[...] = delta
        last = ((i + 1) * tq - 1) // tk

        def block(j, carry, masked):
            dq, dfq = carry
            rows = pl.ds(pl.multiple_of(j * tk, tk), tk)
            p = jnp.exp(_fox_scores(k_ref[rows, :], qt, frep_ref[rows, :], i, j, masked) - lse)
            dp = jnp.dot(v_ref[rows, :], dot, preferred_element_type=F32)
            ds = p * (dp - delta)
            return dq + jnp.dot(kt_ref[j], ds.astype(BF16), preferred_element_type=F32), dfq + jnp.sum(ds, axis=0, keepdims=True)

        init = (jnp.zeros((HEAD_DIM, tq), F32), jnp.zeros((1, tq), F32))
        dq, dfq = block(last, lax.fori_loop(0, last, lambda j, c: block(j, c, False), init), True)
        dqt_ref[...] = (dq * FOX_SCALE).astype(dqt_ref.dtype)
        dfq_ref[...] = dfq

    mine = pl.BlockSpec((None, HEAD_DIM, tq), lambda h, i: (i, h, 0))
    stat = pl.BlockSpec((None, None, 1, tq), lambda h, i: (h, i, 0, 0))
    return pl.pallas_call(
        body, name=name, grid=(FOX_HEADS, nq),
        in_specs=[mine,
                  pl.BlockSpec((t, HEAD_DIM), lambda h, i: (0, kb + h)),
                  pl.BlockSpec((nk, HEAD_DIM, tk), lambda h, i: (0, h, 0)),
                  pl.BlockSpec((t, HEAD_DIM), lambda h, i: (0, vb + h)),
                  mine, mine, stat,
                  pl.BlockSpec((None, t, HEAD_DIM), lambda h, i: (h, 0, 0))],
        out_specs=[mine, stat, stat],
        out_shape=[jax.ShapeDtypeStruct((nq, FOX_W, tq), BF16), jax.ShapeDtypeStruct((FOX_HEADS, nq, 1, tq), F32),
                   jax.ShapeDtypeStruct((FOX_HEADS, nq, 1, tq), F32)],
        compiler_params=pltpu.CompilerParams(dimension_semantics=("parallel", "parallel"), vmem_limit_bytes=VMEM_LIMIT),
    )(qt, z, kt, z, ot, dot, lse, frep)


def _fox_bwd_kv(z, qt, do, dot, lse, delta, frep, name):
    t = z.shape[0]
    tq, tk, nq, nk = _fox_tiles(t)
    qb, kb, vb = Z_FQ // HEAD_DIM, Z_FK // HEAD_DIM, Z_FV // HEAD_DIM
    per = tk // tq

    def body(k_ref, v_ref, frep_ref, q_ref, qt_ref, do_ref, dot_ref, lse_ref, delta_ref, dk_ref, dv_ref, dfk_ref):
        j = pl.program_id(1)
        k, v, frep = k_ref[...], v_ref[...], frep_ref[...]

        def block(i, carry, masked):
            dk, dv, dfk = carry
            rows = pl.ds(pl.multiple_of(i * tq, tq), tq)
            p = jnp.exp(_fox_scores(k, qt_ref[i], frep, i, j, masked) - lse_ref[i])
            dv = dv + jnp.dot(p.astype(BF16), do_ref[rows, :], preferred_element_type=F32)
            dp = jnp.dot(v, dot_ref[i], preferred_element_type=F32)
            ds = p * (dp - delta_ref[i])
            dk = dk + jnp.dot(ds.astype(BF16), q_ref[rows, :], preferred_element_type=F32)
            for part in range(tq // HEAD_DIM):
                dfk = dfk + ds[:, part * HEAD_DIM:(part + 1) * HEAD_DIM]
            return dk, dv, dfk

        zero = jnp.zeros((tk, HEAD_DIM), F32)
        carry = (zero, zero, zero)
        for step in range(per):
            carry = block(j * per + step, carry, True)
        dk, dv, dfk = lax.fori_loop((j + 1) * per, nq, lambda i, c: block(i, c, False), carry)
        dk_ref[...] = (dk * FOX_SCALE).astype(dk_ref.dtype)
        dv_ref[...] = dv.astype(dv_ref.dtype)
        dfk_ref[...] = jnp.sum(dfk, axis=1, keepdims=True)

    whole_t = pl.BlockSpec((nq, HEAD_DIM, tq), lambda h, j: (0, h, 0))
    whole_stat = pl.BlockSpec((None, nq, 1, tq), lambda h, j: (h, 0, 0, 0))
    return pl.pallas_call(
        body, name=name, grid=(FOX_HEADS, nk),
        in_specs=[pl.BlockSpec((tk, HEAD_DIM), lambda h, j: (j, kb + h)),
                  pl.BlockSpec((tk, HEAD_DIM), lambda h, j: (j, vb + h)),
                  pl.BlockSpec((None, tk, HEAD_DIM), lambda h, j: (h, j, 0)),
                  pl.BlockSpec((t, HEAD_DIM), lambda h, j: (0, qb + h)),
                  whole_t,
                  pl.BlockSpec((t, HEAD_DIM), lambda h, j: (0, h)),
                  whole_t, whole_stat, whole_stat],
        out_specs=[pl.BlockSpec((tk, HEAD_DIM), lambda h, j: (j, h)), pl.BlockSpec((tk, HEAD_DIM), lambda h, j: (j, h)),
                   pl.BlockSpec((None, tk, 1), lambda h, j: (h, j, 0))],
        out_shape=[jax.ShapeDtypeStruct((t, FOX_W), BF16), jax.ShapeDtypeStruct((t, FOX_W), BF16),
                   jax.ShapeDtypeStruct((FOX_HEADS, t, 1), F32)],
        compiler_params=pltpu.CompilerParams(dimension_semantics=("parallel", "parallel"), vmem_limit_bytes=VMEM_LIMIT),
    )(z, z, frep, z, qt, do, dot, lse, delta)


def _gla_step(st, q, k, v, la):
    row = lax.broadcasted_iota(jnp.int32, (CHUNK, CHUNK), 0)
    col = lax.broadcasted_iota(jnp.int32, (CHUNK, CHUNK), 1)
    tri = (col <= row).astype(F32)
    a_cum = jnp.dot(tri, la, preferred_element_type=F32, precision=lax.Precision.HIGHEST)
    a_tot = jnp.sum(la, axis=0, keepdims=True)
    k_dec = (_f(k) * jnp.exp(a_tot - a_cum)).astype(BF16)
    qs = (_f(q) * (HEAD_DIM ** -0.5)).astype(BF16)
    st = st * jnp.exp(a_tot) + lax.dot_general(v.astype(BF16), k_dec, (((0,), (0,)), ((), ())), preferred_element_type=F32)
    o = lax.dot_general(qs, st.astype(BF16), (((1,), (1,)), ((), ())), preferred_element_type=F32)
    return st, o


def _gla_blocks(t):
    r = min(256, t)
    return r, t // r, r // CHUNK


def _gla_fwd(z, la, name):
    t = z.shape[0]
    r, nb, nch = _gla_blocks(t)

    def body(q_ref, k_ref, v_ref, la_ref, o_ref, sp_ref, st_ref):
        @pl.when(pl.program_id(0) == 0)
        def _():
            st_ref[...] = jnp.zeros(st_ref.shape, F32)
        for c in range(nch):
            rows = slice(c * CHUNK, (c + 1) * CHUNK)
            for h in range(GLA_HEADS):
                kc = slice(h * HEAD_DIM, (h + 1) * HEAD_DIM)
                vc = slice(h * GLA_VAL_DIM, (h + 1) * GLA_VAL_DIM)
                st = st_ref[h]
                sp_ref[c, h] = st
                st, o = _gla_step(st, q_ref[rows, kc], k_ref[rows, kc], v_ref[rows, vc], la_ref[rows, kc])
                st_ref[h] = st
                o_ref[rows, vc] = o

    return pl.pallas_call(
        body, name=name, grid=(nb,),
        in_specs=[pl.BlockSpec((r, GLA_KW), lambda i: (i, Z_GQ // GLA_KW)), pl.BlockSpec((r, GLA_KW), lambda i: (i, Z_GK // GLA_KW)),
                  pl.BlockSpec((r, GLA_VW), lambda i: (i, Z_GV // GLA_VW)), pl.BlockSpec((r, GLA_KW), lambda i: (i, 0))],
        out_specs=[pl.BlockSpec((r, GLA_VW), lambda i: (i, 0)),
                   pl.BlockSpec((nch, GLA_HEADS, GLA_VAL_DIM, HEAD_DIM), lambda i: (i, 0, 0, 0))],
        out_shape=[jax.ShapeDtypeStruct((t, GLA_VW), F32),
                   jax.ShapeDtypeStruct((t // CHUNK, GLA_HEADS, GLA_VAL_DIM, HEAD_DIM), F32)],
        scratch_shapes=[pltpu.VMEM((GLA_HEADS, GLA_VAL_DIM, HEAD_DIM), F32)],
        compiler_params=pltpu.CompilerParams(dimension_semantics=("arbitrary",), vmem_limit_bytes=VMEM_LIMIT),
    )(z, z, z, la)


def _gla_bwd(z, la, sprev, do, name):
    t = z.shape[0]
    r, nb, nch = _gla_blocks(t)

    def body(q_ref, k_ref, v_ref, la_ref, sp_ref, do_ref, dq_ref, dk_ref, dv_ref, dla_ref, dst_ref):
        @pl.when(pl.program_id(0) == 0)
        def _():
            dst_ref[...] = jnp.zeros(dst_ref.shape, F32)
        for c in reversed(range(nch)):
            rows = slice(c * CHUNK, (c + 1) * CHUNK)
            for h in range(GLA_HEADS):
                kc = slice(h * HEAD_DIM, (h + 1) * HEAD_DIM)
                vc = slice(h * GLA_VAL_DIM, (h + 1) * GLA_VAL_DIM)
                _, vjp = jax.vjp(_gla_step, sp_ref[c, h], q_ref[rows, kc], k_ref[rows, kc], v_ref[rows, vc], la_ref[rows, kc])
                dst, dq, dk, dv, dla = vjp((dst_ref[h], do_ref[rows, vc]))
                dst_ref[h] = dst
                dq_ref[rows, kc] = dq
                dk_ref[rows, kc] = dk
                dv_ref[rows, vc] = dv
                dla_ref[rows, kc] = dla

    rev = lambda i: (nb - 1 - i, 0)
    return pl.pallas_call(
        body, name=name, grid=(nb,),
        in_specs=[pl.BlockSpec((r, GLA_KW), lambda i: (nb - 1 - i, Z_GQ // GLA_KW)), pl.BlockSpec((r, GLA_KW), lambda i: (nb - 1 - i, Z_GK // GLA_KW)),
                  pl.BlockSpec((r, GLA_VW), lambda i: (nb - 1 - i, Z_GV // GLA_VW)), pl.BlockSpec((r, GLA_KW), rev),
                  pl.BlockSpec((nch, GLA_HEADS, GLA_VAL_DIM, HEAD_DIM), lambda i: (nb - 1 - i, 0, 0, 0)),
                  pl.BlockSpec((r, GLA_VW), rev)],
        out_specs=[pl.BlockSpec((r, GLA_KW), rev), pl.BlockSpec((r, GLA_KW), rev), pl.BlockSpec((r, GLA_VW), rev), pl.BlockSpec((r, GLA_KW), rev)],
        out_shape=[jax.ShapeDtypeStruct((t, GLA_KW), BF16), jax.ShapeDtypeStruct((t, GLA_KW), BF16),
                   jax.ShapeDtypeStruct((t, GLA_VW), BF16), jax.ShapeDtypeStruct((t, GLA_KW), F32)],
        scratch_shapes=[pltpu.VMEM((GLA_HEADS, GLA_VAL_DIM, HEAD_DIM), F32)],
        compiler_params=pltpu.CompilerParams(dimension_semantics=("arbitrary",), vmem_limit_bytes=VMEM_LIMIT),
    )(z, z, z, la, sprev, do)


def _local_step(x, p, tgt, shards, sp, cidx, chip):
    t = x.shape[0]
    tq, tk, _, _ = _fox_tiles(t)
    full, reduced = {}, {}

    def plan(names):
        keys, shapes, places = [], [], []
        for n in names:
            r, cc = shards[n].shape
            key, part, parts = FUSED.get(n, (n, 0, 1))
            if key not in keys:
                keys.append(key)
                stacked = n in ROW_SHARDED or n == "w_in"
                shapes.append(jax.ShapeDtypeStruct((4 * r, cc) if stacked else (r, 4 * cc * parts), shards[n].dtype))
            places.append((keys.index(key), r, 0, 0) if n in ROW_SHARDED or n == "w_in" else (keys.index(key), 0, part * 4 * cc, cc))
        return keys, shapes, places

    def gather(names):
        _, shapes, places = plan(names)
        return _ag_comm([shards[n] for n in names], shapes, places)

    def landed(names, got):
        keys, _, _ = plan(names)
        for key, g in zip(keys, got):
            full[key] = g

    def pair_sums(grads):
        parts, entries = [], []
        for g, names in grads:
            cols = g.shape[1] // (4 * len(names))
            if names[0] in ROW_SHARDED or names[0] == "w_in":
                parts.append(g if g.ndim == 3 else _shard_parts(g, names[0]))
                entries.append((names[0], len(parts) - 1, "slot", 0, parts[-1].shape[2]))
            else:
                parts.append(g[None])
                entries += [(n, len(parts) - 1, "cols", k * 4 * cols, cols) for k, n in enumerate(names)]
        swapped = _pair_swap(parts, "pair_swap")
        return entries, [_sum_half(a, b, cidx, "sum_half") for a, b in zip(parts, swapped)]

    def exchange(entries, sums):
        return _exchange_comm(sums, [e[1:] for e in entries])

    def exchanged(entries, sums, got):
        for (n, si, mode, first, _), g in zip(entries, got):
            reduced[n] = (g, sums[si], mode, first)

    first = ["ffn1_w_gate", "ffn1_w_up"]
    landed(first, _run_comm(gather(first), "all_gather"))
    w_gu1 = full["gu1"]
    n1 = _norm_fwd(x, sp["ffn1_norm"], "norm1_fwd")
    names = ["ffn1_w_down", "w_merge_gate", "gla_gate_up"]
    gu1, got = _mm(n1, w_gu1, name="mm_gu_gather", comm=gather(names))
    landed(names, got)
    a1 = _act_fwd(gu1, "act_fwd")
    names = ["w_in"]
    f1, got = _mm(a1, full["ffn1_w_down"], out_dtype=F32, name="mm_down_gather", comm=gather(names))
    landed(names, got)
    w_big, w_sm = _in_layout(full["w_in"], full["w_merge_gate"])
    gup = jnp.zeros((SMALL_W, GLA_KW), F32).at[FOX_HEADS:FOX_HEADS + GLA_RANK].set(full["gla_gate_up"])
    h1, u = _resnorm_fwd(x, f1, sp["mix_norm"], 0.5, "resnorm_fwd_half")
    names = ["w_branch_fox", "w_branch_gla", "w_out", "w_ple_proj", "w_ple_gate"]
    z, got = _mm(u, w_big, name="mm_in_gather", comm=gather(names))
    landed(names, got)
    s = _mm(u, w_sm, out_dtype=F32, name="mm_in_small")
    lf, la = _small_fwd(s, sp["fb"], gup, sp["gb"], "small_fwd")
    fp = _cumsum_tokens(lf, False, "cumsum_fwd")
    frep = jnp.broadcast_to(fp[:, :FOX_HEADS].T[:, :, None], (FOX_HEADS, t, HEAD_DIM))
    qt = _blocked_t(z[:, Z_FQ:Z_FQ + FOX_W], tq)
    kt = _blocked_t(z[:, Z_FK:Z_FK + FOX_W], tk)
    vt = _blocked_t(z[:, Z_FV:Z_FV + FOX_W], tk)
    names = ["ffn2_w_gate", "ffn2_w_up"]
    ot, lse, got = _fox_fwd(z, qt, vt, frep, "fox_fwd_gather", comm=gather(names))
    landed(names, got)
    w_gu2 = full["gu2"]
    y_fox = _unblocked_t(ot)
    o_gla, sprev = _gla_fwd(z, la, "gla_fwd")
    y_gla = _gla_out_fwd(o_gla, z, sp["ghn"], "gla_out_fwd")
    bf = _mm(y_fox, full["w_branch_fox"], name="mm_branch")
    bg = _mm(y_gla, full["w_branch_gla"], name="mm_branch")
    merged = _merge_fwd(z, bf, bg, sp["bm"], "merge_fwd")
    mo = _mm(merged, full["w_out"], out_dtype=F32, name="mm_out")
    h2, n2 = _resnorm_fwd(h1, mo, sp["ffn2_norm"], 1.0, "resnorm_fwd_one")
    names = ["ffn2_w_down"]
    gu2, got = _mm(n2, w_gu2, name="mm_gu_gather_down", comm=gather(names))
    landed(names, got)
    a2 = _act_fwd(gu2, "act_fwd")
    f2 = _mm(a2, full["ffn2_w_down"], out_dtype=F32, name="mm_down")
    h3, n4 = _resnorm_fwd(h2, f2, sp["ple_norm"], 0.5, "resnorm_fwd_half")
    pgl = _mm(n4, full["w_ple_gate"], name="mm_pg")
    pb = p.astype(BF16)
    pp = _mm(pb, full["w_ple_proj"], name="mm_pp")

    dh3, dpgl, dpp, loss, d_final = _head(h3, pgl, pp, tgt, sp["final_norm"], "head")
    ds_ = {"final_norm": d_final}
    entries, sums = pair_sums([(_mm(n4, dpgl, ta=True, name="mm_dw_sq"), ["w_ple_gate"]), (_mm(pb, dpp, ta=True, name="mm_dw_pp"), ["w_ple_proj"])])
    dn4, got = _mm(dpgl, full["w_ple_gate"], tb=True, out_dtype=F32, name="mm_dx_sq_f32_exchange", comm=exchange(entries, sums))
    exchanged(entries, sums, got)
    dh3, df2, ds_["ple_norm"] = _norm_bwd(h3, [dn4], dh3, sp["ple_norm"], 0.5, "norm_bwd_1")

    def ffn_bwd(n, gu, a, df, wgu, wd, which):
        entries, sums = pair_sums([(_mm(a, df, ta=True, name="mm_dw_down"), [which + "_w_down"])])
        da, got = _mm(df, wd, tb=True, name="mm_dx_down_exchange", comm=exchange(entries, sums))
        exchanged(entries, sums, got)
        dgu = _act_bwd(gu, da, "act_bwd")
        entries, sums = pair_sums([(_mm(n, dgu, ta=True, name="mm_dw_gu"), [which + "_w_gate", which + "_w_up"])])
        dn, got = _mm(dgu, wgu, tb=True, out_dtype=F32, name="mm_dx_gu_exchange", comm=exchange(entries, sums))
        exchanged(entries, sums, got)
        return dn

    dn2 = ffn_bwd(n2, gu2, a2, df2, w_gu2, full["ffn2_w_down"], "ffn2")
    dh2, dmix, ds_["ffn2_norm"] = _norm_bwd(h2, [dn2], dh3, sp["ffn2_norm"], 1.0, "norm_bwd_1")

    dw_out = _mm(merged, dmix, ta=True, name="mm_dw_sq")
    dmerged = _mm(dmix, full["w_out"], tb=True, name="mm_dx_sq")
    dgl, dbf, dbg, ds_["bm"] = _merge_bwd(z, bf, bg, sp["bm"], dmerged, "merge_bwd")
    mix_entries, mix_sums = pair_sums([(dw_out, ["w_out"]), (_mm(y_fox, dbf, ta=True, name="mm_dw_branch"), ["w_branch_fox"]),
                                       (_mm(y_gla, dbg, ta=True, name="mm_dw_branch"), ["w_branch_gla"])])
    dy_fox = _mm(dbf, full["w_branch_fox"], tb=True, name="mm_dx_branch")
    dy_gla = _mm(dbg, full["w_branch_gla"], tb=True, name="mm_dx_branch")

    do_gla, dgr, ds_["ghn"] = _gla_out_bwd(o_gla, z, sp["ghn"], dy_gla, "gla_out_bwd")
    dgq, dgk, dgv, dla = _gla_bwd(z, la, sprev, do_gla, "gla_bwd")
    dot = _blocked_t(dy_fox, tq)
    dqt, delta, df_query = _fox_bwd_q(z, qt, kt, ot, dot, lse, frep, "fox_bwd_q")
    dfq = _unblocked_t(dqt)
    dfk, dfv, df_key = _fox_bwd_kv(z, qt, dy_fox, dot, lse, delta, frep, "fox_bwd_kv")
    df = df_query.reshape(FOX_HEADS, t) - df_key.reshape(FOX_HEADS, t)
    dfp = jnp.pad(df.T, ((0, 0), (0, SMALL_W - FOX_HEADS)))
    dlf = _cumsum_tokens(dfp, True, "cumsum_bwd")
    dsm, ds_["fb"], dgup, ds_["gb"] = _small_bwd(s, sp["fb"], gup, sp["gb"], dlf, dla, "small_bwd")
    dz = jnp.concatenate([dfq, dfk, dfv, dgq, dgk, dgv, dgr, dgl], axis=1)
    dw_big, got = _mm(u, dz, ta=True, name="mm_dw_in_exchange", comm=exchange(mix_entries, mix_sums))
    exchanged(mix_entries, mix_sums, got)
    dw_sm = _mm(u, dsm, ta=True, out_dtype=F32, name="mm_dw_in_small").astype(BF16)
    entries, sums = pair_sums([(_w_in_parts(dw_big, dw_sm), ["w_in"]), (dw_big[:, Z_GL:], ["w_merge_gate"]), (dgup[FOX_HEADS:FOX_HEADS + GLA_RANK], ["gla_gate_up"])])
    du1, got = _mm(dz, w_big, tb=True, out_dtype=F32, name="mm_dx_in_exchange", comm=exchange(entries, sums))
    exchanged(entries, sums, got)
    du2 = _mm(dsm, w_sm, tb=True, out_dtype=F32, name="mm_dx_in_small")
    dh1, df1, ds_["mix_norm"] = _norm_bwd(h1, [du1, du2], dh2, sp["mix_norm"], 0.5, "norm_bwd_2")

    dn1 = ffn_bwd(n1, gu1, a1, df1, w_gu1, full["ffn1_w_down"], "ffn1")
    grad_x, _, ds_["ffn1_norm"] = _norm_bwd(x, [dn1], dh1, sp["ffn1_norm"], 1.0, "norm_bwd_1")
    return loss, grad_x, reduced, ds_


def _half_rows(ref, which):
    r2 = ref.shape[0] // 2
    return ref.at[pl.ds(pl.multiple_of(which * r2, r2), r2)]


class _Comm:
    def __init__(self, ins, out_shape, sems, start, finish):
        self.ins, self.out_shape, self.sems, self.start, self.finish = ins, out_shape, sems, start, finish


def _run_comm(comm, name):
    n_in, n_out = len(comm.ins), len(comm.out_shape)

    def body(*refs):
        parts = refs[:n_in], refs[n_in:n_in + n_out], refs[n_in + n_out:]
        comm.start(*parts)
        comm.finish(*parts)

    return pl.pallas_call(
        body, name=name, in_specs=[ANY] * n_in, out_specs=[ANY] * n_out, out_shape=comm.out_shape,
        scratch_shapes=comm.sems, compiler_params=pltpu.CompilerParams(has_side_effects=True),
    )(*comm.ins)


def _hosted(body, comm, *, name, grid, in_specs, out_specs, out_shape, scratch_shapes, semantics, args):
    if comm is None:
        res = pl.pallas_call(
            body, name=name, grid=grid, in_specs=in_specs, out_specs=out_specs, out_shape=out_shape, scratch_shapes=scratch_shapes,
            compiler_params=pltpu.CompilerParams(dimension_semantics=semantics, vmem_limit_bytes=VMEM_LIMIT),
        )(*args)
        return res, None
    ni, no, ns = len(in_specs), len(out_shape), len(scratch_shapes)
    ci, co = len(comm.ins), len(comm.out_shape)

    def wrapped(*refs):
        h_in, c_in = refs[:ni], refs[ni:ni + ci]
        h_out, c_out = refs[ni + ci:ni + ci + no], refs[ni + ci + no:ni + ci + no + co]
        h_scr, c_sem = refs[ni + ci + no + co:ni + ci + no + co + ns], refs[ni + ci + no + co + ns:]
        ids = [pl.program_id(axis) for axis in range(len(grid))]
        first = functools.reduce(jnp.logical_and, [i == 0 for i in ids])
        last = functools.reduce(jnp.logical_and, [i == g - 1 for i, g in zip(ids, grid)])

        @pl.when(first)
        def _():
            comm.start(c_in, c_out, c_sem)

        body(*h_in, *h_out, *h_scr)

        @pl.when(last)
        def _():
            comm.finish(c_in, c_out, c_sem)

    res = pl.pallas_call(
        wrapped, name=name, grid=grid, in_specs=list(in_specs) + [ANY] * ci, out_specs=list(out_specs) + [ANY] * co,
        out_shape=list(out_shape) + list(comm.out_shape), scratch_shapes=list(scratch_shapes) + list(comm.sems),
        compiler_params=pltpu.CompilerParams(dimension_semantics=("arbitrary",) * len(grid), vmem_limit_bytes=VMEM_LIMIT, has_side_effects=True),
    )(*args, *comm.ins)
    return res[:no], res[no:]


def _ag_comm(shards, out_shape, places):
    n = len(shards)

    def copies(ins, outs, sems):
        ici_send, ici_recv, d2d_send, d2d_recv = sems
        x, y, c = lax.axis_index("x"), lax.axis_index("y"), lax.axis_index("c")
        chips = [(1 - x, y), (x, 1 - y), (1 - x, 1 - y)]
        slot = lambda chip: 2 * chip[0] + chip[1]

        def window(wi, origin, half):
            out, row_step, col_base, col_step = places[wi]
            r, cc = shards[wi].shape
            rows = pl.ds(pl.multiple_of(slot(origin) * row_step + half * (r // 2), r // 2), r // 2)
            cols = pl.ds(pl.multiple_of(col_base + slot(origin) * col_step, HEAD_DIM), cc) if col_step else pl.ds(col_base, cc)
            return outs[out].at[rows, cols]

        def over_ici(wi, j, origin):
            return pltpu.make_async_remote_copy(
                src_ref=_half_rows(ins[wi], c), dst_ref=window(wi, origin, c),
                send_sem=ici_send.at[3 * wi + j], recv_sem=ici_recv.at[3 * wi + j],
                device_id=(chips[j][0], chips[j][1], c), device_id_type=MESH)

        def over_d2d(wi, j, half):
            place = window(wi, chips[j], half)
            return pltpu.make_async_remote_copy(
                src_ref=place, dst_ref=place, send_sem=d2d_send.at[3 * wi + j], recv_sem=d2d_recv.at[3 * wi + j],
                device_id=(x, y, 1 - c), device_id_type=MESH)

        return over_ici, over_d2d, (x, y), chips, c

    def chunk_rows(wi):
        r, cc = shards[wi].shape
        item = shards[wi].dtype.itemsize
        return _pick(r, max(32 // item, BOUNCE_BYTES // (cc * item)), 32 // item)

    def start(ins, outs, scratch):
        over_ici, _, me, _, _ = copies(ins, outs, scratch[:4])
        for wi in range(n):
            for j in range(3):
                over_ici(wi, j, me).start()
        loc_sems = scratch[4]
        for wi in range(n):
            out, row_step, col_base, col_step = places[wi]
            r, cc = shards[wi].shape
            rc = chunk_rows(wi)
            buf = scratch[5 + wi]
            slot = 2 * me[0] + me[1]
            cols = pl.ds(pl.multiple_of(col_base + slot * col_step, HEAD_DIM), cc) if col_step else pl.ds(col_base, cc)

            def load(k):
                return pltpu.make_async_copy(ins[wi].at[pl.ds(k * rc, rc)], buf.at[k % 2], loc_sems.at[2 * wi])

            def store(k):
                rows = pl.ds(pl.multiple_of(slot * row_step + k * rc, rc), rc)
                return pltpu.make_async_copy(buf.at[k % 2], outs[out].at[rows, cols], loc_sems.at[2 * wi + 1])

            load(0).start()
            for k in range(r // rc):
                load(k).wait()
                if k + 1 < r // rc:
                    load(k + 1).start()
                store(k).start()
                store(k).wait()

    def finish(ins, outs, scratch):
        over_ici, over_d2d, me, chips, c = copies(ins, outs, scratch[:4])
        for wi in range(n):
            for j in range(3):
                over_ici(wi, j, chips[j]).wait_recv()
                over_d2d(wi, j, c).start()
        for wi in range(n):
            for j in range(3):
                over_d2d(wi, j, 1 - c).wait_recv()
        for wi in range(n):
            for j in range(3):
                over_ici(wi, j, me).wait_send()
                over_d2d(wi, j, c).wait_send()

    bounce = [pltpu.VMEM((min(2, s.shape[0] // chunk_rows(wi)), chunk_rows(wi), s.shape[1]), s.dtype) for wi, s in enumerate(shards)]
    return _Comm(list(shards), list(out_shape), [pltpu.SemaphoreType.DMA((3 * n,))] * 4 + [pltpu.SemaphoreType.DMA((2 * n,))] + bounce,
                 start, finish)


def _pair_swap(parts, name):
    n = len(parts)

    def body(*refs):
        ins, outs = refs[:n], refs[n:2 * n]
        send_sems, recv_sems = refs[2 * n:]
        x, y, c = lax.axis_index("x"), lax.axis_index("y"), lax.axis_index("c")

        def swap(wi):
            r2 = parts[wi].shape[1] // 2
            return pltpu.make_async_remote_copy(
                src_ref=ins[wi].at[:, pl.ds(pl.multiple_of((1 - c) * r2, r2), r2)], dst_ref=outs[wi],
                send_sem=send_sems.at[wi], recv_sem=recv_sems.at[wi], device_id=(x, y, 1 - c), device_id_type=MESH)

        copies = [swap(wi) for wi in range(n)]
        for cp in copies:
            cp.start()
        for cp in copies:
            cp.wait()

    return pl.pallas_call(
        body, name=name, in_specs=[ANY] * n, out_specs=[ANY] * n,
        out_shape=[jax.ShapeDtypeStruct((s.shape[0], s.shape[1] // 2, s.shape[2]), s.dtype) for s in parts],
        scratch_shapes=[pltpu.SemaphoreType.DMA((n,))] * 2, compiler_params=pltpu.CompilerParams(has_side_effects=True),
    )(*parts)


def _row_tile(r, c, budget=1 << 19):
    return r if r <= 8 else _pick(r, max(8, budget // c), 8)


def _sum_half(parts, other, cidx, name):
    nl, r, cc = parts.shape
    r2 = r // 2
    tr = _row_tile(r2, cc)

    def body(c_ref, p_ref, q_ref, o_ref):
        o_ref[...] = (_f(p_ref[...]) + _f(q_ref[...])).astype(o_ref.dtype)

    return pl.pallas_call(
        body, name=name, out_shape=jax.ShapeDtypeStruct((nl, r2, cc), parts.dtype),
        grid_spec=pltpu.PrefetchScalarGridSpec(
            num_scalar_prefetch=1, grid=(nl, r2 // tr),
            in_specs=[pl.BlockSpec((None, None, tr, cc), lambda l, i, c_ref: (l, c_ref[0], i, 0)),
                      pl.BlockSpec((None, tr, cc), lambda l, i, c_ref: (l, i, 0))],
            out_specs=pl.BlockSpec((None, tr, cc), lambda l, i, c_ref: (l, i, 0))),
        compiler_params=pltpu.CompilerParams(dimension_semantics=("parallel", "parallel"), vmem_limit_bytes=VMEM_LIMIT),
    )(cidx, parts.reshape(nl, 2, r2, cc), other)


def _exchange_comm(sums, entries):
    n = len(entries)

    def copies(ins, outs, sems):
        send_sems, recv_sems = sems
        x, y, c = lax.axis_index("x"), lax.axis_index("y"), lax.axis_index("c")
        chips = [(1 - x, y), (x, 1 - y), (1 - x, 1 - y)]
        slot = lambda chip: 2 * chip[0] + chip[1]

        def piece(wi, dest):
            si, mode, first, cols = entries[wi]
            if mode == "cols":
                return ins[si].at[0, :, pl.ds(pl.multiple_of(first + slot(dest) * cols, HEAD_DIM), cols)]
            return ins[si].at[slot(dest) if mode == "slot" else 0]

        def remote(wi, j, origin):
            return pltpu.make_async_remote_copy(
                src_ref=piece(wi, chips[j]), dst_ref=outs[wi].at[slot(origin)],
                send_sem=send_sems.at[3 * wi + j], recv_sem=recv_sems.at[3 * wi + j],
                device_id=(chips[j][0], chips[j][1], c), device_id_type=MESH)

        return remote, (x, y), chips

    def start(ins, outs, sems):
        remote, me, _ = copies(ins, outs, sems)
        for wi in range(n):
            for j in range(3):
                remote(wi, j, me).start()

    def finish(ins, outs, sems):
        remote, me, chips = copies(ins, outs, sems)
        for wi in range(n):
            for j in range(3):
                remote(wi, j, chips[j]).wait_recv()
        for wi in range(n):
            for j in range(3):
                remote(wi, j, me).wait_send()

    out_shape = [jax.ShapeDtypeStruct((4, sums[si].shape[1], cols), sums[si].dtype) for si, _, _, cols in entries]
    return _Comm(list(sums), out_shape, [pltpu.SemaphoreType.DMA((3 * n,))] * 2, start, finish)


def _sum_chips(got, own, mode, first, chip, name):
    _, r2, cc = got.shape
    tr = _row_tile(r2, cc)
    own_block = {"slot": lambda i, chip_ref: (chip_ref[0], i, 0), "same": lambda i, chip_ref: (0, i, 0),
                 "cols": lambda i, chip_ref: (0, i, first // cc + chip_ref[0])}[mode]

    def body(chip_ref, g_ref, own_ref, o_ref):
        term = lambda k: jnp.where(chip_ref[0] == k, _f(own_ref[...]), _f(g_ref[k]))
        o_ref[...] = ((term(0) + term(1)) + term(2)) + term(3)

    return pl.pallas_call(
        body, name=name, out_shape=jax.ShapeDtypeStruct((r2, cc), F32),
        grid_spec=pltpu.PrefetchScalarGridSpec(
            num_scalar_prefetch=1, grid=(r2 // tr,),
            in_specs=[pl.BlockSpec((4, tr, cc), lambda i, chip_ref: (0, i, 0)),
                      pl.BlockSpec((None, tr, cc), own_block)],
            out_specs=pl.BlockSpec((tr, cc), lambda i, chip_ref: (i, 0))),
        compiler_params=pltpu.CompilerParams(dimension_semantics=("parallel",), vmem_limit_bytes=VMEM_LIMIT),
    )(chip, got, own)


def _pair_gather(halves, name):
    n = len(halves)

    def body(*refs):
        ins, outs = refs[:n], refs[n:2 * n]
        send_sems, recv_sems = refs[2 * n:]
        x, y, c = lax.axis_index("x"), lax.axis_index("y"), lax.axis_index("c")
        copies = [pltpu.make_async_remote_copy(
            src_ref=ins[wi], dst_ref=outs[wi], send_sem=send_sems.at[wi], recv_sem=recv_sems.at[wi],
            device_id=(x, y, 1 - c), device_id_type=MESH) for wi in range(n)]
        for cp in copies:
            cp.start()
        for cp in copies:
            cp.wait()

    return pl.pallas_call(
        body, name=name, in_specs=[ANY] * n, out_specs=[ANY] * n,
        out_shape=[jax.ShapeDtypeStruct(s.shape, s.dtype) for s in halves],
        scratch_shapes=[pltpu.SemaphoreType.DMA((n,))] * 2, compiler_params=pltpu.CompilerParams(has_side_effects=True),
    )(*halves)


def _adamw_update(g, w, m, v):
    m_new = ADAM_B1 * m + (1.0 - ADAM_B1) * g
    v_new = ADAM_B2 * v + (1.0 - ADAM_B2) * jnp.square(g)
    m_hat = m_new / (1.0 - ADAM_B1 ** ADAM_STEP)
    v_hat = v_new / (1.0 - ADAM_B2 ** ADAM_STEP)
    return -ADAM_LR * (m_hat / (jnp.sqrt(v_hat) + ADAM_EPS) + ADAM_WD * w), m_new, v_new


def _adamw_whole(g, w, m, v, name):
    r, c = w.shape
    tc = 256

    def body(g_ref, w_ref, m_ref, v_ref, d_ref, nm_ref, nv_ref):
        d_ref[...], nm_ref[...], nv_ref[...] = _adamw_update(g_ref[...], w_ref[...], m_ref[...], v_ref[...])

    blk = pl.BlockSpec((r, tc), lambda i: (0, i))
    return pl.pallas_call(
        body, name=name, grid=(c // tc,), in_specs=[blk] * 4, out_specs=[blk] * 3, out_shape=[jax.ShapeDtypeStruct((r, c), F32)] * 3,
        compiler_params=pltpu.CompilerParams(dimension_semantics=("parallel",), vmem_limit_bytes=VMEM_LIMIT),
    )(g, w, m, v)


def _adamw(mine, other, cidx, w, m, v, name):
    r, c = w.shape
    tr = _row_tile(r // 2, c, 1 << 18)
    nh = (r // 2) // tr

    def body(c_ref, mine_ref, other_ref, w_ref, m_ref, v_ref, g_ref, d_ref, nm_ref, nv_ref):
        g = jnp.where(pl.program_id(0) // nh == c_ref[0], mine_ref[...], other_ref[...])
        g_ref[...] = g
        d_ref[...], nm_ref[...], nv_ref[...] = _adamw_update(g, w_ref[...], m_ref[...], v_ref[...])

    blk = pl.BlockSpec((tr, c), lambda i, c_ref: (i, 0))
    mine_spec = pl.BlockSpec((tr, c), lambda i, c_ref: (jnp.where(i // nh == c_ref[0], i % nh, 0), 0))
    other_spec = pl.BlockSpec((tr, c), lambda i, c_ref: (jnp.where(i // nh == c_ref[0], 0, i % nh), 0))
    return pl.pallas_call(
        body, name=name, out_shape=[jax.ShapeDtypeStruct((r, c), F32)] * 4,
        grid_spec=pltpu.PrefetchScalarGridSpec(
            num_scalar_prefetch=1, grid=(r // tr,), in_specs=[mine_spec, other_spec, blk, blk, blk], out_specs=[blk] * 4),
        compiler_params=pltpu.CompilerParams(dimension_semantics=("arbitrary",), vmem_limit_bytes=VMEM_LIMIT),
    )(cidx, mine, other, w, m, v)


BIG = ["ffn1_w_gate", "ffn1_w_up", "ffn1_w_down", "w_in", "gla_gate_up", "w_branch_fox", "w_branch_gla", "w_merge_gate", "w_out",
       "ffn2_w_gate", "ffn2_w_up", "ffn2_w_down", "w_ple_proj", "w_ple_gate"]
ROW_SHARDED = ("ffn1_w_down", "w_out", "ffn2_w_down", "w_ple_gate")
FUSED = {"ffn1_w_gate": ("gu1", 0, 2), "ffn1_w_up": ("gu1", 1, 2), "ffn2_w_gate": ("gu2", 0, 2), "ffn2_w_up": ("gu2", 1, 2)}
SMALL = ["ffn1_norm", "mix_norm", "fox_forget_bias", "gla_gate_bias", "gla_head_norm", "b_merge_gate", "ffn2_norm", "ple_norm", "final_norm"]
NAMES = ["ffn1_norm", "ffn1_w_gate", "ffn1_w_up", "ffn1_w_down", "mix_norm", "w_in", "fox_forget_bias", "gla_gate_up", "gla_gate_bias",
         "gla_head_norm", "w_branch_fox", "w_branch_gla", "w_merge_gate", "b_merge_gate", "w_out", "ffn2_norm", "ffn2_w_gate", "ffn2_w_up",
         "ffn2_w_down", "ple_norm", "w_ple_proj", "w_ple_gate", "final_norm"]
W_IN_COLS = (FOX_W, FOX_W, FOX_W, FOX_HEADS, GLA_KW, GLA_KW, GLA_VW, GLA_VW, GLA_RANK)
SMALL_ROWS, SMALL_COLS = 16, 1024


def _shard_parts(full, name):
    if name in ROW_SHARDED:
        return full.reshape(4, full.shape[0] // 4, full.shape[1])
    return jnp.transpose(full.reshape(full.shape[0], 4, full.shape[1] // 4), (1, 0, 2))


W_IN_WIDE = ((0, 3 * FOX_W), (3 * FOX_W + FOX_HEADS, 3 * FOX_W + FOX_HEADS + 2 * GLA_KW + 2 * GLA_VW))
W_IN_NARROW = ((3 * FOX_W, 3 * FOX_W + FOX_HEADS), (sum(W_IN_COLS) - GLA_RANK, sum(W_IN_COLS)))


def _in_layout(stacked, w_merge_gate):
    per = stacked.shape[1]

    def columns(lo, hi):
        out = []
        while lo < hi:
            j, end = lo // per, min(hi, (lo // per + 1) * per)
            out.append(stacked[j * D_MODEL:(j + 1) * D_MODEL, lo - j * per:end - j * per])
            lo = end
        return out
    big = jnp.concatenate(columns(*W_IN_WIDE[0]) + columns(*W_IN_WIDE[1]) + [w_merge_gate], axis=1)
    sm = jnp.concatenate(columns(*W_IN_NARROW[0]) + columns(*W_IN_NARROW[1]) + [jnp.zeros((D_MODEL, SMALL_W - FOX_HEADS - GLA_RANK), BF16)], axis=1)
    return big, sm


def _w_in_parts(dw_big, dw_sm):
    runs = [(W_IN_WIDE[0], dw_big, Z_FQ), (W_IN_NARROW[0], dw_sm, 0), (W_IN_WIDE[1], dw_big, Z_GQ), (W_IN_NARROW[1], dw_sm, FOX_HEADS)]
    per = sum(W_IN_COLS) // 4
    parts = []
    for j in range(4):
        lo, hi, pieces = j * per, (j + 1) * per, []
        for (a, b), src, at in runs:
            if max(a, lo) < min(b, hi):
                pieces.append(src[:, at + max(a, lo) - a:at + min(b, hi) - a])
        parts.append(jnp.concatenate(pieces, axis=1))
    return jnp.stack(parts)


def _pad_lanes(a, width):
    return jnp.pad(a, ((0, 0), (0, width - a.shape[1])))


def kernel(x, p, ffn1_norm, ffn1_w_gate, ffn1_w_up, ffn1_w_down, mix_norm, w_in, fox_forget_bias, gla_gate_up, gla_gate_bias, gla_head_norm, w_branch_fox, w_branch_gla, w_merge_gate, b_merge_gate, w_out, ffn2_norm, ffn2_w_gate, ffn2_w_up, ffn2_w_down, ple_norm, w_ple_proj, w_ple_gate, final_norm, loss_target, m_ffn1_norm, m_ffn1_w_gate, m_ffn1_w_up, m_ffn1_w_down, m_mix_norm, m_w_in, m_fox_forget_bias, m_gla_gate_up, m_gla_gate_bias, m_gla_head_norm, m_w_branch_fox, m_w_branch_gla, m_w_merge_gate, m_b_merge_gate, m_w_out, m_ffn2_norm, m_ffn2_w_gate, m_ffn2_w_up, m_ffn2_w_down, m_ple_norm, m_w_ple_proj, m_w_ple_gate, m_final_norm, v_ffn1_norm, v_ffn1_w_gate, v_ffn1_w_up, v_ffn1_w_down, v_mix_norm, v_w_in, v_fox_forget_bias, v_gla_gate_up, v_gla_gate_bias, v_gla_head_norm, v_w_branch_fox, v_w_branch_gla, v_w_merge_gate, v_b_merge_gate, v_w_out, v_ffn2_norm, v_ffn2_w_gate, v_ffn2_w_up, v_ffn2_w_down, v_ple_norm, v_w_ple_proj, v_w_ple_gate, v_final_norm):
    args = dict(locals())
    wts = {n: args[n] for n in NAMES}
    mom = {n: args["m_" + n] for n in NAMES}
    var = {n: args["v_" + n] for n in NAMES}
    two_d = lambda a: a.reshape(-1, a.shape[-1])

    wire = lambda n: F32 if n == "gla_gate_up" else BF16
    cidx = lax.axis_index("c").astype(jnp.int32).reshape(1)
    chip = (2 * lax.axis_index("x") + lax.axis_index("y")).astype(jnp.int32)
    shards = {n: two_d(wts[n]).astype(wire(n)) for n in BIG}
    sp = {
        "ffn1_norm": two_d(ffn1_norm), "mix_norm": two_d(mix_norm), "fb": _pad_lanes(two_d(fox_forget_bias), SMALL_W),
        "gb": two_d(gla_gate_bias), "ghn": two_d(gla_head_norm), "bm": two_d(b_merge_gate), "ffn2_norm": two_d(ffn2_norm),
        "ple_norm": two_d(ple_norm), "final_norm": two_d(final_norm),
    }

    loss, grad_x, reduced, ds_ = _local_step(x[0], p[0, 0], loss_target[0], shards, sp, cidx, chip)

    small_g = {"ffn1_norm": ds_["ffn1_norm"], "mix_norm": ds_["mix_norm"], "fox_forget_bias": ds_["fb"][:, :FOX_HEADS],
               "gla_gate_bias": ds_["gb"], "gla_head_norm": ds_["ghn"], "b_merge_gate": ds_["bm"], "ffn2_norm": ds_["ffn2_norm"],
               "ple_norm": ds_["ple_norm"], "final_norm": ds_["final_norm"]}
    small_w = sum(two_d(wts[n]).shape[1] for n in SMALL)
    assert small_w <= SMALL_ROWS * SMALL_COLS
    packed = lambda d: _pad_lanes(jnp.concatenate([two_d(d[n]) for n in SMALL], axis=1), SMALL_ROWS * SMALL_COLS).reshape(SMALL_ROWS, SMALL_COLS)
    parts = [packed(small_g)[None]]
    pair_sums = [_sum_half(a, b, cidx, "sum_half") for a, b in zip(parts, _pair_swap(parts, "pair_swap"))]
    reduced["small"] = (_run_comm(_exchange_comm(pair_sums, [(0, "same", 0, SMALL_COLS)]), "chip_exchange")[0], pair_sums[0], "same", 0)
    mine = [_sum_chips(*reduced[n], chip.reshape(1), "sum_chips") for n in BIG + ["small"]]
    other = _pair_gather(mine, "pair_gather")

    out = {}
    for n, a, b in zip(BIG, mine[:-1], other[:-1]):
        if n == "w_in":
            g_t = jnp.where(cidx[0] == 0, jnp.concatenate([a, b]), jnp.concatenate([b, a])).T
            flip = lambda t: jnp.swapaxes(t, 1, 2)[0]
            res = _adamw_whole(g_t, flip(wts[n]), flip(mom[n]), flip(var[n]), "adamw_w_in")
            out[n] = [jnp.swapaxes(r[None], 1, 2) for r in [g_t, *res]]
            continue
        out[n] = [r.reshape(wts[n].shape) for r in _adamw(a, b, cidx, two_d(wts[n]), two_d(mom[n]), two_d(var[n]), "adamw_" + n)]
    small_out = [r.reshape(1, SMALL_ROWS * SMALL_COLS) for r in _adamw(mine[-1], other[-1], cidx, packed(wts), packed(mom), packed(var), "adamw_small")]
    off = 0
    for n in SMALL:
        cw = two_d(wts[n]).shape[1]
        out[n] = [r[:, off:off + cw].reshape(wts[n].shape) for r in small_out]
        off += cw

    total = lax.psum(loss[0, 0], ("x", "y", "c"))
    return (total, grad_x[None], *[out[n][0] for n in NAMES], *[out[n][1] for n in NAMES],
            *[out[n][2] for n in NAMES], *[out[n][3] for n in NAMES])
```

```python
import functools

import jax
import jax.numpy as jnp
from jax import lax
from jax.experimental import pallas as pl
from jax.experimental.pallas import tpu as pltpu

F32 = jnp.float32
BF16 = jnp.bfloat16
MESH = pl.DeviceIdType.MESH
ANY = pl.BlockSpec(memory_space=pl.ANY)

D_MODEL = 2048
FOX_HEADS = 8
HEAD_DIM = 128
GLA_HEADS = 4
GLA_VAL_DIM = 256
GLA_RANK = 16
GLA_TAU = 16.0
CHUNK = 64
EPS = 1e-6
FOX_W = FOX_HEADS * HEAD_DIM
GLA_KW = GLA_HEADS * HEAD_DIM
GLA_VW = GLA_HEADS * GLA_VAL_DIM
Z_FQ, Z_FK, Z_FV, Z_GQ, Z_GK, Z_GV, Z_GR, Z_GL = 0, 1024, 2048, 3072, 3584, 4096, 5120, 6144
Z_W = Z_GL + 2 * D_MODEL
SMALL_W = 128
NEG = -1e30

ADAM_LR, ADAM_B1, ADAM_B2, ADAM_EPS, ADAM_WD, ADAM_STEP = 0.001, 0.9, 0.999, 1e-08, 0.01, 10

VMEM_LIMIT = 56 * 1024 * 1024
BOUNCE_BYTES = 2 * 1024 * 1024


def _pick(n, target, mult=128):
    if n <= target:
        return n
    best = None
    for d in range(mult, target + 1, mult):
        if n % d == 0:
            best = d
    assert best is not None, (n, target)
    return best


def _mm(a, b, *, ta=False, tb=False, out_dtype=BF16, name, comm=None):
    m, k = (a.shape[1], a.shape[0]) if ta else a.shape
    n = b.shape[0] if tb else b.shape[1]
    assert (b.shape[1] if tb else b.shape[0]) == k
    bk = _pick(k, 4096)
    nk = k // bk
    bm, bn = _pick(m, 1024), _pick(n, 1024 if nk > 1 else 512)
    dims = (((0 if ta else 1,), (1 if tb else 0,)), ((), ()))

    def body(a_ref, b_ref, o_ref, acc_ref):
        part = lax.dot_general(a_ref[...], b_ref[...], dims, preferred_element_type=F32)
        if nk == 1:
            o_ref[...] = part.astype(o_ref.dtype)
            return
        kk = pl.program_id(2)

        @pl.when(kk == 0)
        def _():
            acc_ref[...] = part

        @pl.when(kk > 0)
        def _():
            acc_ref[...] += part

        @pl.when(kk == nk - 1)
        def _():
            o_ref[...] = acc_ref[...].astype(o_ref.dtype)

    a_spec = pl.BlockSpec((bk, bm), lambda i, j, kk: (kk, i)) if ta else pl.BlockSpec((bm, bk), lambda i, j, kk: (i, kk))
    b_spec = pl.BlockSpec((bn, bk), lambda i, j, kk: (j, kk)) if tb else pl.BlockSpec((bk, bn), lambda i, j, kk: (kk, j))
    (out,), travelled = _hosted(
        body, comm, name=name, grid=(m // bm, n // bn, nk),
        in_specs=[a_spec, b_spec], out_specs=[pl.BlockSpec((bm, bn), lambda i, j, kk: (i, j))],
        out_shape=[jax.ShapeDtypeStruct((m, n), out_dtype)], scratch_shapes=[pltpu.VMEM((bm, bn), F32)],
        semantics=("parallel", "parallel", "arbitrary"), args=(a, b))
    return out if comm is None else (out, travelled)


def _rowwise(fn, tiled, bcast, outs, reds=(), *, tt, name):
    t = tiled[0][0].shape[0]
    tt = min(tt, t)
    nin, nout = len(tiled) + len(bcast), len(outs)
    splits = [s[3] for s in tiled] + [s[1] for s in bcast]

    def store(ref, val, acc):
        off = 0
        for piece in val if isinstance(val, (tuple, list)) else (val,):
            w = piece.shape[-1]
            if acc:
                ref[:, off:off + w] += piece.astype(ref.dtype)
            else:
                ref[:, off:off + w] = piece.astype(ref.dtype)
            off += w
        assert off == ref.shape[-1], (name, off, ref.shape)

    def body(*refs):
        args = []
        for ref, sp in zip(refs[:nin], splits):
            if sp is None:
                args.append(ref[...])
            else:
                off = 0
                for w in sp:
                    args.append(ref[:, off:off + w])
                    off += w
        res = fn(*args)
        res = res if isinstance(res, (tuple, list)) else (res,)
        assert len(res) == nout + len(reds), (name, len(res))
        for ref, val in zip(refs[nin:nin + nout], res[:nout]):
            store(ref, val, False)
        if reds:
            @pl.when(pl.program_id(0) == 0)
            def _():
                for ref in refs[nin + nout:]:
                    ref[...] = jnp.zeros(ref.shape, ref.dtype)
            for ref, val in zip(refs[nin + nout:], res[nout:]):
                store(ref, val, True)

    in_specs = [pl.BlockSpec((tt, w), functools.partial(lambda i, cb: (i, cb), cb=cb)) for (_, w, cb, _) in tiled]
    in_specs += [pl.BlockSpec(arr.shape, lambda i: (0, 0)) for (arr, _) in bcast]
    out_specs = [pl.BlockSpec((tt, w), lambda i: (i, 0)) for (w, _) in outs]
    out_specs += [pl.BlockSpec((r, w), lambda i: (0, 0)) for (r, w) in reds]
    out_shape = [jax.ShapeDtypeStruct((t, w), dt) for (w, dt) in outs] + [jax.ShapeDtypeStruct((r, w), F32) for (r, w) in reds]
    return pl.pallas_call(
        body, name=name, grid=(t // tt,), in_specs=in_specs, out_specs=out_specs, out_shape=out_shape,
        compiler_params=pltpu.CompilerParams(dimension_semantics=("arbitrary" if reds else "parallel",), vmem_limit_bytes=VMEM_LIMIT),
    )(*[s[0] for s in tiled], *[s[0] for s in bcast])


def _full(arr):
    return (arr, arr.shape[1], 0, None)


def _f(x):
    return x.astype(F32)


def _rms(x, g):
    return x * lax.rsqrt(jnp.mean(x * x, axis=-1, keepdims=True) + EPS) * g


def _log_sigmoid(x):
    return jnp.minimum(x, 0.0) - jnp.log1p(jnp.exp(-jnp.abs(x)))


def _silu(x):
    return x * jax.nn.sigmoid(x)


def _norm_fwd(x, g, name):
    return _rowwise(lambda xb, gb: _rms(_f(xb), gb), [_full(x)], [(g, None)], [(x.shape[1], BF16)], tt=256, name=name)[0]


def _resnorm_fwd(res, branch, g, coef, name):
    def fn(rb, bb, gb):
        h = rb + coef * _f(bb)
        return h, _rms(h, gb)
    d = res.shape[1]
    return _rowwise(fn, [_full(res), _full(branch)], [(g, None)], [(d, F32), (d, BF16)], tt=256, name=name)


def _norm_bwd(h, dns, dres, g, coef, name):
    nd = len(dns)

    def fn(hb, *rest):
        dn = _f(rest[0])
        for extra in rest[1:nd]:
            dn = dn + _f(extra)
        dr, gb = rest[nd], rest[nd + 1]
        _, vjp = jax.vjp(_rms, hb, gb)
        dh, dg = vjp(dn)
        dh = dh + dr
        return dh, coef * dh, dg
    d = h.shape[1]
    return _rowwise(fn, [_full(h)] + [_full(x) for x in dns] + [_full(dres)], [(g, None)],
                    [(d, F32), (d, BF16)], [(1, d)], tt=256, name=name)


def _act_fwd(gu, name):
    ff = gu.shape[1] // 2
    return _rowwise(lambda gb, ub: _silu(_f(gb)) * _f(ub), [(gu, 2 * ff, 0, (ff, ff))], [], [(ff, BF16)], tt=256, name=name)[0]


def _act_bwd(gu, da, name):
    ff = gu.shape[1] // 2

    def fn(gb, ub, dab):
        _, vjp = jax.vjp(lambda p, q: _silu(p) * q, _f(gb), _f(ub))
        return (vjp(_f(dab)),)
    return _rowwise(fn, [(gu, 2 * ff, 0, (ff, ff)), _full(da)], [], [(2 * ff, BF16)], tt=128, name=name)[0]


def _merge(glf, glg, bf, bg, bmf, bmg):
    return jax.nn.sigmoid(_f(glf) + bmf) * _f(bf) + jax.nn.sigmoid(_f(glg) + bmg) * _f(bg)


def _merge_fwd(z, bf, bg, bm, name):
    d = D_MODEL
    return _rowwise(_merge, [(z, d, Z_GL // d, None), (z, d, Z_GL // d + 1, None), _full(bf), _full(bg)], [(bm, (d, d))],
                    [(d, BF16)], tt=256, name=name)[0]


def _merge_bwd(z, bf, bg, bm, dm, name):
    d = D_MODEL

    def fn(glf, glg, bfb, bgb, dmb, bmf, bmg):
        _, vjp = jax.vjp(_merge, _f(glf), _f(glg), _f(bfb), _f(bgb), bmf, bmg)
        dglf, dglg, dbf, dbg, dbmf, dbmg = vjp(_f(dmb))
        return (dglf, dglg), dbf, dbg, (dbmf, dbmg)
    return _rowwise(fn, [(z, d, Z_GL // d, None), (z, d, Z_GL // d + 1, None), _full(bf), _full(bg), _full(dm)], [(bm, (d, d))],
                    [(2 * d, BF16), (d, BF16), (d, BF16)], [(1, 2 * d)], tt=128, name=name)


def _gla_out(o, gr, g):
    return _rms(o, g) * _silu(_f(gr))


_PER_HEAD = (GLA_VAL_DIM,) * GLA_HEADS


def _gla_out_fwd(o, z, g, name):
    nh = GLA_HEADS

    def fn(*blocks):
        return (tuple(_gla_out(blocks[h], blocks[nh + h], blocks[2 * nh]) for h in range(nh)),)
    return _rowwise(fn, [(o, GLA_VW, 0, _PER_HEAD), (z, GLA_VW, Z_GR // GLA_VW, _PER_HEAD)], [(g, None)], [(GLA_VW, BF16)], tt=256, name=name)[0]


def _gla_out_bwd(o, z, g, dy, name):
    nh = GLA_HEADS

    def fn(*blocks):
        gb = blocks[3 * nh]
        grads = []
        for h in range(nh):
            _, vjp = jax.vjp(_gla_out, blocks[h], _f(blocks[nh + h]), gb)
            grads.append(vjp(_f(blocks[2 * nh + h])))
        dg = grads[0][2]
        for h in range(1, nh):
            dg = dg + grads[h][2]
        return tuple(gr[0] for gr in grads), tuple(gr[1] for gr in grads), dg
    return _rowwise(fn, [(o, GLA_VW, 0, _PER_HEAD), (z, GLA_VW, Z_GR // GLA_VW, _PER_HEAD), (dy, GLA_VW, 0, _PER_HEAD)], [(g, None)],
                    [(GLA_VW, F32), (GLA_VW, BF16)], [(1, GLA_VAL_DIM)], tt=256, name=name)


def _small_gates(s, fb, gup, gb):
    lane = lax.broadcasted_iota(jnp.int32, s.shape, 1)
    lf = jnp.where(lane < FOX_HEADS, _log_sigmoid(s + fb), 0.0)
    pre = jnp.dot(s.astype(BF16), gup.astype(BF16), preferred_element_type=F32) + gb
    return lf, _log_sigmoid(pre) / GLA_TAU


def _small_fwd(s, fb, gup, gb, name):
    return _rowwise(_small_gates, [_full(s)], [(fb, None), (gup, None), (gb, None)], [(SMALL_W, F32), (GLA_KW, F32)], tt=256, name=name)


def _small_bwd(s, fb, gup, gb, dlf, dla, name):
    def fn(sb, dlfb, dlab, fbb, gupb, gbb):
        _, vjp = jax.vjp(_small_gates, sb, fbb, gupb, gbb)
        return vjp((dlfb, dlab))
    return _rowwise(fn, [_full(s), _full(dlf), _full(dla)], [(fb, None), (gup, None), (gb, None)],
                    [(SMALL_W, BF16)], [(1, SMALL_W), (SMALL_W, GLA_KW), (1, GLA_KW)], tt=256, name=name)


def _head_fn(h3, pgl, pp, tgt, gf):
    h4 = h3 + jax.nn.sigmoid(pgl) * pp
    err = _rms(h4, gf) - tgt
    return 0.5 * jnp.sum(jnp.mean(err * err, axis=-1, keepdims=True))


def _head(h3, pgl, pp, tgt, gf, name):
    def fn(hb, gl, pb, tb, gfb):
        loss, vjp = jax.vjp(_head_fn, hb, _f(gl), _f(pb), tb, gfb)
        dh, dgl, dpp, _, dgf = vjp(jnp.ones((), F32))
        return dh, dgl, dpp, jnp.full((1, 128), loss, F32), dgf
    d = h3.shape[1]
    return _rowwise(fn, [_full(h3), _full(pgl), _full(pp), _full(tgt)], [(gf, None)],
                    [(d, F32), (d, BF16), (d, BF16)], [(1, 128), (1, d)], tt=256, name=name)


def _cumsum_tokens(a, reverse, name):
    t, w = a.shape
    r = min(256, t)
    nb = t // r

    def body(a_ref, o_ref, carry_ref):
        @pl.when(pl.program_id(0) == 0)
        def _():
            carry_ref[...] = jnp.zeros(carry_ref.shape, F32)
        row = lax.broadcasted_iota(jnp.int32, (r, r), 0)
        col = lax.broadcasted_iota(jnp.int32, (r, r), 1)
        tri = ((col >= row) if reverse else (col <= row)).astype(F32)
        blk = a_ref[...]
        o_ref[...] = jnp.dot(tri, blk, preferred_element_type=F32, precision=lax.Precision.HIGHEST) + carry_ref[...]
        carry_ref[...] += jnp.sum(blk, axis=0, keepdims=True)

    idx = (lambda i: (nb - 1 - i, 0)) if reverse else (lambda i: (i, 0))
    return pl.pallas_call(
        body, name=name, grid=(nb,), in_specs=[pl.BlockSpec((r, w), idx)], out_specs=pl.BlockSpec((r, w), idx),
        out_shape=jax.ShapeDtypeStruct((t, w), F32), scratch_shapes=[pltpu.VMEM((1, w), F32)],
        compiler_params=pltpu.CompilerParams(dimension_semantics=("arbitrary",)),
    )(a)


FOX_TQ, FOX_TK = 256, 512
FOX_SCALE = HEAD_DIM ** -0.5


def _fox_tiles(t):
    tq, tk = min(FOX_TQ, t), min(FOX_TK, t)
    return tq, tk, t // tq, t // tk


def _blocked_t(a, blk):
    return a.reshape(a.shape[0] // blk, blk, a.shape[1]).transpose(0, 2, 1)


def _unblocked_t(b):
    return b.transpose(0, 2, 1).reshape(b.shape[0] * b.shape[2], b.shape[1])


def _fox_scores(k, qt, frep, i, j, masked):
    tk, tq = k.shape[0], qt.shape[1]
    st = jnp.dot(k, qt, preferred_element_type=F32) * FOX_SCALE - jnp.tile(frep, (1, tq // HEAD_DIM))
    if masked:
        key = j * tk + lax.broadcasted_iota(jnp.int32, (tk, tq), 0)
        query = i * tq + lax.broadcasted_iota(jnp.int32, (tk, tq), 1)
        st = jnp.where(key <= query, st, NEG)
    return st


def _fox_fwd(z, qt, vt, frep, name, comm=None):
    t = z.shape[0]
    tq, tk, nq, nk = _fox_tiles(t)
    kb = Z_FK // HEAD_DIM

    def body(qt_ref, k_ref, vt_ref, frep_ref, ot_ref, lse_ref):
        i = pl.program_id(1)
        qt = qt_ref[...]
        last = ((i + 1) * tq - 1) // tk

        def block(j, carry, masked):
            m, l, acc = carry
            rows = pl.ds(pl.multiple_of(j * tk, tk), tk)
            st = _fox_scores(k_ref[rows, :], qt, frep_ref[rows, :], i, j, masked)
            m_new = jnp.maximum(m, jnp.max(st, axis=0, keepdims=True))
            alpha = jnp.exp(m - m_new)
            p = jnp.exp(st - m_new)
            l = alpha * l + jnp.sum(p, axis=0, keepdims=True)
            acc = alpha * acc + jnp.dot(vt_ref[j], p.astype(BF16), preferred_element_type=F32)
            return m_new, l, acc

        init = (jnp.full((1, tq), NEG, F32), jnp.zeros((1, tq), F32), jnp.zeros((HEAD_DIM, tq), F32))
        m, l, acc = block(last, lax.fori_loop(0, last, lambda j, c: block(j, c, False), init), True)
        ot_ref[...] = (acc / l).astype(ot_ref.dtype)
        lse_ref[...] = m + jnp.log(l)

    stat = pl.BlockSpec((None, None, 1, tq), lambda h, i: (h, i, 0, 0))
    (ot, lse), travelled = _hosted(
        body, comm, name=name, grid=(FOX_HEADS, nq),
        in_specs=[pl.BlockSpec((None, HEAD_DIM, tq), lambda h, i: (i, h, 0)),
                  pl.BlockSpec((t, HEAD_DIM), lambda h, i: (0, kb + h)),
                  pl.BlockSpec((nk, HEAD_DIM, tk), lambda h, i: (0, h, 0)),
                  pl.BlockSpec((None, t, HEAD_DIM), lambda h, i: (h, 0, 0))],
        out_specs=[pl.BlockSpec((None, HEAD_DIM, tq), lambda h, i: (i, h, 0)), stat],
        out_shape=[jax.ShapeDtypeStruct((nq, FOX_W, tq), BF16), jax.ShapeDtypeStruct((FOX_HEADS, nq, 1, tq), F32)],
        scratch_shapes=[], semantics=("parallel", "parallel"), args=(qt, z, vt, frep))
    return ot, lse, travelled


def _fox_bwd_q(z, qt, kt, ot, dot, lse, frep, name):
    t = z.shape[0]
    tq, tk, nq, nk = _fox_tiles(t)
    kb, vb = Z_FK // HEAD_DIM, Z_FV // HEAD_DIM

    def body(qt_ref, k_ref, kt_ref, v_ref, ot_ref, dot_ref, lse_ref, frep_ref, dqt_ref, delta_ref, dfq_ref):
        i = pl.program_id(1)
        qt, dot = qt_ref[...], dot_ref[...]
        lse = lse_ref[...]
        delta = jnp.sum(_f(dot) * _f(ot_ref[...]), axis=0, keepdims=True)
        delta_ref[...] = delta
        last = ((i + 1) * tq - 1) // tk

        def block(j, carry, masked):
            dq, dfq = carry
            rows = pl.ds(pl.multiple_of(j * tk, tk), tk)
            p = jnp.exp(_fox_scores(k_ref[rows, :], qt, frep_ref[rows, :], i, j, masked) - lse)
            dp = jnp.dot(v_ref[rows, :], dot, preferred_element_type=F32)
            ds = p * (dp - delta)
            return dq + jnp.dot(kt_ref[j], ds.astype(BF16), preferred_element_type=F32), dfq + jnp.sum(ds, axis=0, keepdims=True)

        init = (jnp.zeros((HEAD_DIM, tq), F32), jnp.zeros((1, tq), F32))
        dq, dfq = block(last, lax.fori_loop(0, last, lambda j, c: block(j, c, False), init), True)
        dqt_ref[...] = (dq * FOX_SCALE).astype(dqt_ref.dtype)
        dfq_ref[...] = dfq

    mine = pl.BlockSpec((None, HEAD_DIM, tq), lambda h, i: (i, h, 0))
    stat = pl.BlockSpec((None, None, 1, tq), lambda h, i: (h, i, 0, 0))
    return pl.pallas_call(
        body, name=name, grid=(FOX_HEADS, nq),
        in_specs=[mine,
                  pl.BlockSpec((t, HEAD_DIM), lambda h, i: (0, kb + h)),
                  pl.BlockSpec((nk, HEAD_DIM, tk), lambda h, i: (0, h, 0)),
                  pl.BlockSpec((t, HEAD_DIM), lambda h, i: (0, vb + h)),
                  mine, mine, stat,
                  pl.BlockSpec((None, t, HEAD_DIM), lambda h, i: (h, 0, 0))],
        out_specs=[mine, stat, stat],
        out_shape=[jax.ShapeDtypeStruct((nq, FOX_W, tq), BF16), jax.ShapeDtypeStruct((FOX_HEADS, nq, 1, tq), F32),
                   jax.ShapeDtypeStruct((FOX_HEADS, nq, 1, tq), F32)],
        compiler_params=pltpu.CompilerParams(dimension_semantics=("parallel", "parallel"), vmem_limit_bytes=VMEM_LIMIT),
    )(qt, z, kt, z, ot, dot, lse, frep)


def _fox_bwd_kv(z, qt, do, dot, lse, delta, frep, name):
    t = z.shape[0]
    tq, tk, nq, nk = _fox_tiles(t)
    qb, kb, vb = Z_FQ // HEAD_DIM, Z_FK // HEAD_DIM, Z_FV // HEAD_DIM
    per = tk // tq

    def body(k_ref, v_ref, frep_ref, q_ref, qt_ref, do_ref, dot_ref, lse_ref, delta_ref, dk_ref, dv_ref, dfk_ref):
        j = pl.program_id(1)
        k, v, frep = k_ref[...], v_ref[...], frep_ref[...]

        def block(i, carry, masked):
            dk, dv, dfk = carry
            rows = pl.ds(pl.multiple_of(i * tq, tq), tq)
            p = jnp.exp(_fox_scores(k, qt_ref[i], frep, i, j, masked) - lse_ref[i])
            dv = dv + jnp.dot(p.astype(BF16), do_ref[rows, :], preferred_element_type=F32)
            dp = jnp.dot(v, dot_ref[i], preferred_element_type=F32)
            ds = p * (dp - delta_ref[i])
            dk = dk + jnp.dot(ds.astype(BF16), q_ref[rows, :], preferred_element_type=F32)
            for part in range(tq // HEAD_DIM):
                dfk = dfk + ds[:, part * HEAD_DIM:(part + 1) * HEAD_DIM]
            return dk, dv, dfk

        zero = jnp.zeros((tk, HEAD_DIM), F32)
        carry = (zero, zero, zero)
        for step in range(per):
            carry = block(j * per + step, carry, True)
        dk, dv, dfk = lax.fori_loop((j + 1) * per, nq, lambda i, c: block(i, c, False), carry)
        dk_ref[...] = (dk * FOX_SCALE).astype(dk_ref.dtype)
        dv_ref[...] = dv.astype(dv_ref.dtype)
        dfk_ref[...] = jnp.sum(dfk, axis=1, keepdims=True)

    whole_t = pl.BlockSpec((nq, HEAD_DIM, tq), lambda h, j: (0, h, 0))
    whole_stat = pl.BlockSpec((None, nq, 1, tq), lambda h, j: (h, 0, 0, 0))
    return pl.pallas_call(
        body, name=name, grid=(FOX_HEADS, nk),
        in_specs=[pl.BlockSpec((tk, HEAD_DIM), lambda h, j: (j, kb + h)),
                  pl.BlockSpec((tk, HEAD_DIM), lambda h, j: (j, vb + h)),
                  pl.BlockSpec((None, tk, HEAD_DIM), lambda h, j: (h, j, 0)),
                  pl.BlockSpec((t, HEAD_DIM), lambda h, j: (0, qb + h)),
                  whole_t,
                  pl.BlockSpec((t, HEAD_DIM), lambda h, j: (0, h)),
                  whole_t, whole_stat, whole_stat],
        out_specs=[pl.BlockSpec((tk, HEAD_DIM), lambda h, j: (j, h)), pl.BlockSpec((tk, HEAD_DIM), lambda h, j: (j, h)),
                   pl.BlockSpec((None, tk, 1), lambda h, j: (h, j, 0))],
        out_shape=[jax.ShapeDtypeStruct((t, FOX_W), BF16), jax.ShapeDtypeStruct((t, FOX_W), BF16),
                   jax.ShapeDtypeStruct((FOX_HEADS, t, 1), F32)],
        compiler_params=pltpu.CompilerParams(dimension_semantics=("parallel", "parallel"), vmem_limit_bytes=VMEM_LIMIT),
    )(z, z, frep, z, qt, do, dot, lse, delta)


def _gla_step(st, q, k, v, la):
    row = lax.broadcasted_iota(jnp.int32, (CHUNK, CHUNK), 0)
    col = lax.broadcasted_iota(jnp.int32, (CHUNK, CHUNK), 1)
    tri = (col <= row).astype(F32)
    a_cum = jnp.dot(tri, la, preferred_element_type=F32, precision=lax.Precision.HIGHEST)
    a_tot = jnp.sum(la, axis=0, keepdims=True)
    k_dec = (_f(k) * jnp.exp(a_tot - a_cum)).astype(BF16)
    qs = (_f(q) * (HEAD_DIM ** -0.5)).astype(BF16)
    st = st * jnp.exp(a_tot) + lax.dot_general(v.astype(BF16), k_dec, (((0,), (0,)), ((), ())), preferred_element_type=F32)
    o = lax.dot_general(qs, st.astype(BF16), (((1,), (1,)), ((), ())), preferred_element_type=F32)
    return st, o


def _gla_blocks(t):
    r = min(256, t)
    return r, t // r, r // CHUNK


def _gla_fwd(z, la, name):
    t = z.shape[0]
    r, nb, nch = _gla_blocks(t)

    def body(q_ref, k_ref, v_ref, la_ref, o_ref, sp_ref, st_ref):
        @pl.when(pl.program_id(0) == 0)
        def _():
            st_ref[...] = jnp.zeros(st_ref.shape, F32)
        for c in range(nch):
            rows = slice(c * CHUNK, (c + 1) * CHUNK)
            for h in range(GLA_HEADS):
                kc = slice(h * HEAD_DIM, (h + 1) * HEAD_DIM)
                vc = slice(h * GLA_VAL_DIM, (h + 1) * GLA_VAL_DIM)
                st = st_ref[h]
                sp_ref[c, h] = st
                st, o = _gla_step(st, q_ref[rows, kc], k_ref[rows, kc], v_ref[rows, vc], la_ref[rows, kc])
                st_ref[h] = st
                o_ref[rows, vc] = o

    return pl.pallas_call(
        body, name=name, grid=(nb,),
        in_specs=[pl.BlockSpec((r, GLA_KW), lambda i: (i, Z_GQ // GLA_KW)), pl.BlockSpec((r, GLA_KW), lambda i: (i, Z_GK // GLA_KW)),
                  pl.BlockSpec((r, GLA_VW), lambda i: (i, Z_GV // GLA_VW)), pl.BlockSpec((r, GLA_KW), lambda i: (i, 0))],
        out_specs=[pl.BlockSpec((r, GLA_VW), lambda i: (i, 0)),
                   pl.BlockSpec((nch, GLA_HEADS, GLA_VAL_DIM, HEAD_DIM), lambda i: (i, 0, 0, 0))],
        out_shape=[jax.ShapeDtypeStruct((t, GLA_VW), F32),
                   jax.ShapeDtypeStruct((t // CHUNK, GLA_HEADS, GLA_VAL_DIM, HEAD_DIM), F32)],
        scratch_shapes=[pltpu.VMEM((GLA_HEADS, GLA_VAL_DIM, HEAD_DIM), F32)],
        compiler_params=pltpu.CompilerParams(dimension_semantics=("arbitrary",), vmem_limit_bytes=VMEM_LIMIT),
    )(z, z, z, la)


def _gla_bwd(z, la, sprev, do, name):
    t = z.shape[0]
    r, nb, nch = _gla_blocks(t)

    def body(q_ref, k_ref, v_ref, la_ref, sp_ref, do_ref, dq_ref, dk_ref, dv_ref, dla_ref, dst_ref):
        @pl.when(pl.program_id(0) == 0)
        def _():
            dst_ref[...] = jnp.zeros(dst_ref.shape, F32)
        for c in reversed(range(nch)):
            rows = slice(c * CHUNK, (c + 1) * CHUNK)
            for h in range(GLA_HEADS):
                kc = slice(h * HEAD_DIM, (h + 1) * HEAD_DIM)
                vc = slice(h * GLA_VAL_DIM, (h + 1) * GLA_VAL_DIM)
                _, vjp = jax.vjp(_gla_step, sp_ref[c, h], q_ref[rows, kc], k_ref[rows, kc], v_ref[rows, vc], la_ref[rows, kc])
                dst, dq, dk, dv, dla = vjp((dst_ref[h], do_ref[rows, vc]))
                dst_ref[h] = dst
                dq_ref[rows, kc] = dq
                dk_ref[rows, kc] = dk
                dv_ref[rows, vc] = dv
                dla_ref[rows, kc] = dla

    rev = lambda i: (nb - 1 - i, 0)
    return pl.pallas_call(
        body, name=name, grid=(nb,),
        in_specs=[pl.BlockSpec((r, GLA_KW), lambda i: (nb - 1 - i, Z_GQ // GLA_KW)), pl.BlockSpec((r, GLA_KW), lambda i: (nb - 1 - i, Z_GK // GLA_KW)),
                  pl.BlockSpec((r, GLA_VW), lambda i: (nb - 1 - i, Z_GV // GLA_VW)), pl.BlockSpec((r, GLA_KW), rev),
                  pl.BlockSpec((nch, GLA_HEADS, GLA_VAL_DIM, HEAD_DIM), lambda i: (nb - 1 - i, 0, 0, 0)),
                  pl.BlockSpec((r, GLA_VW), rev)],
        out_specs=[pl.BlockSpec((r, GLA_KW), rev), pl.BlockSpec((r, GLA_KW), rev), pl.BlockSpec((r, GLA_VW), rev), pl.BlockSpec((r, GLA_KW), rev)],
        out_shape=[jax.ShapeDtypeStruct((t, GLA_KW), BF16), jax.ShapeDtypeStruct((t, GLA_KW), BF16),
                   jax.ShapeDtypeStruct((t, GLA_VW), BF16), jax.ShapeDtypeStruct((t, GLA_KW), F32)],
        scratch_shapes=[pltpu.VMEM((GLA_HEADS, GLA_VAL_DIM, HEAD_DIM), F32)],
        compiler_params=pltpu.CompilerParams(dimension_semantics=("arbitrary",), vmem_limit_bytes=VMEM_LIMIT),
    )(z, z, z, la, sprev, do)


def _local_step(x, p, tgt, shards, sp, cidx, chip):
    t = x.shape[0]
    tq, tk, _, _ = _fox_tiles(t)
    full, reduced = {}, {}

    def plan(names):
        keys, shapes, places = [], [], []
        for n in names:
            r, cc = shards[n].shape
            key, part, parts = FUSED.get(n, (n, 0, 1))
            if key not in keys:
                keys.append(key)
                stacked = n in ROW_SHARDED or n == "w_in"
                shapes.append(jax.ShapeDtypeStruct((4 * r, cc) if stacked else (r, 4 * cc * parts), shards[n].dtype))
            places.append((keys.index(key), r, 0, 0) if n in ROW_SHARDED or n == "w_in" else (keys.index(key), 0, part * 4 * cc, cc))
        return keys, shapes, places

    def gather(names):
        _, shapes, places = plan(names)
        return _ag_comm([shards[n] for n in names], shapes, places)

    def landed(names, got):
        keys, _, _ = plan(names)
        for key, g in zip(keys, got):
            full[key] = g

    def pair_sums(grads):
        parts, entries = [], []
        for g, names in grads:
            cols = g.shape[1] // (4 * len(names))
            if names[0] in ROW_SHARDED or names[0] == "w_in":
                parts.append(g if g.ndim == 3 else _shard_parts(g, names[0]))
                entries.append((names[0], len(parts) - 1, "slot", 0, parts[-1].shape[2]))
            else:
                parts.append(g[None])
                entries += [(n, len(parts) - 1, "cols", k * 4 * cols, cols) for k, n in enumerate(names)]
        swapped = _pair_swap(parts, "pair_swap")
        return entries, [_sum_half(a, b, cidx, "sum_half") for a, b in zip(parts, swapped)]

    def exchange(entries, sums):
        return _exchange_comm(sums, [e[1:] for e in entries])

    def exchanged(entries, sums, got):
        for (n, si, mode, first, _), g in zip(entries, got):
            reduced[n] = (g, sums[si], mode, first)

    first = ["ffn1_w_gate", "ffn1_w_up"]
    landed(first, _run_comm(gather(first), "all_gather"))
    w_gu1 = full["gu1"]
    n1 = _norm_fwd(x, sp["ffn1_norm"], "norm1_fwd")
    names = ["ffn1_w_down", "w_merge_gate", "gla_gate_up"]
    gu1, got = _mm(n1, w_gu1, name="mm_gu_gather", comm=gather(names))
    landed(names, got)
    a1 = _act_fwd(gu1, "act_fwd")
    names = ["w_in"]
    f1, got = _mm(a1, full["ffn1_w_down"], out_dtype=F32, name="mm_down_gather", comm=gather(names))
    landed(names, got)
    w_big, w_sm = _in_layout(full["w_in"], full["w_merge_gate"])
    gup = jnp.zeros((SMALL_W, GLA_KW), F32).at[FOX_HEADS:FOX_HEADS + GLA_RANK].set(full["gla_gate_up"])
    h1, u = _resnorm_fwd(x, f1, sp["mix_norm"], 0.5, "resnorm_fwd_half")
    names = ["w_branch_fox", "w_branch_gla", "w_out", "w_ple_proj", "w_ple_gate"]
    z, got = _mm(u, w_big, name="mm_in_gather", comm=gather(names))
    landed(names, got)
    s = _mm(u, w_sm, out_dtype=F32, name="mm_in_small")
    lf, la = _small_fwd(s, sp["fb"], gup, sp["gb"], "small_fwd")
    fp = _cumsum_tokens(lf, False, "cumsum_fwd")
    frep = jnp.broadcast_to(fp[:, :FOX_HEADS].T[:, :, None], (FOX_HEADS, t, HEAD_DIM))
    qt = _blocked_t(z[:, Z_FQ:Z_FQ + FOX_W], tq)
    kt = _blocked_t(z[:, Z_FK:Z_FK + FOX_W], tk)
    vt = _blocked_t(z[:, Z_FV:Z_FV + FOX_W], tk)
    names = ["ffn2_w_gate", "ffn2_w_up"]
    ot, lse, got = _fox_fwd(z, qt, vt, frep, "fox_fwd_gather", comm=gather(names))
    landed(names, got)
    w_gu2 = full["gu2"]
    y_fox = _unblocked_t(ot)
    o_gla, sprev = _gla_fwd(z, la, "gla_fwd")
    y_gla = _gla_out_fwd(o_gla, z, sp["ghn"], "gla_out_fwd")
    bf = _mm(y_fox, full["w_branch_fox"], name="mm_branch")
    bg = _mm(y_gla, full["w_branch_gla"], name="mm_branch")
    merged = _merge_fwd(z, bf, bg, sp["bm"], "merge_fwd")
    mo = _mm(merged, full["w_out"], out_dtype=F32, name="mm_out")
    h2, n2 = _resnorm_fwd(h1, mo, sp["ffn2_norm"], 1.0, "resnorm_fwd_one")
    names = ["ffn2_w_down"]
    gu2, got = _mm(n2, w_gu2, name="mm_gu_gather_down", comm=gather(names))
    landed(names, got)
    a2 = _act_fwd(gu2, "act_fwd")
    f2 = _mm(a2, full["ffn2_w_down"], out_dtype=F32, name="mm_down")
    h3, n4 = _resnorm_fwd(h2, f2, sp["ple_norm"], 0.5, "resnorm_fwd_half")
    pgl = _mm(n4, full["w_ple_gate"], name="mm_pg")
    pb = p.astype(BF16)
    pp = _mm(pb, full["w_ple_proj"], name="mm_pp")

    dh3, dpgl, dpp, loss, d_final = _head(h3, pgl, pp, tgt, sp["final_norm"], "head")
    ds_ = {"final_norm": d_final}
    entries, sums = pair_sums([(_mm(n4, dpgl, ta=True, name="mm_dw_sq"), ["w_ple_gate"]), (_mm(pb, dpp, ta=True, name="mm_dw_pp"), ["w_ple_proj"])])
    dn4, got = _mm(dpgl, full["w_ple_gate"], tb=True, out_dtype=F32, name="mm_dx_sq_f32_exchange", comm=exchange(entries, sums))
    exchanged(entries, sums, got)
    dh3, df2, ds_["ple_norm"] = _norm_bwd(h3, [dn4], dh3, sp["ple_norm"], 0.5, "norm_bwd_1")

    def ffn_bwd(n, gu, a, df, wgu, wd, which):
        entries, sums = pair_sums([(_mm(a, df, ta=True, name="mm_dw_down"), [which + "_w_down"])])
        da, got = _mm(df, wd, tb=True, name="mm_dx_down_exchange", comm=exchange(entries, sums))
        exchanged(entries, sums, got)
        dgu = _act_bwd(gu, da, "act_bwd")
        entries, sums = pair_sums([(_mm(n, dgu, ta=True, name="mm_dw_gu"), [which + "_w_gate", which + "_w_up"])])
        dn, got = _mm(dgu, wgu, tb=True, out_dtype=F32, name="mm_dx_gu_exchange", comm=exchange(entries, sums))
        exchanged(entries, sums, got)
        return dn

    dn2 = ffn_bwd(n2, gu2, a2, df2, w_gu2, full["ffn2_w_down"], "ffn2")
    dh2, dmix, ds_["ffn2_norm"] = _norm_bwd(h2, [dn2], dh3, sp["ffn2_norm"], 1.0, "norm_bwd_1")

    dw_out = _mm(merged, dmix, ta=True, name="mm_dw_sq")
    dmerged = _mm(dmix, full["w_out"], tb=True, name="mm_dx_sq")
    dgl, dbf, dbg, ds_["bm"] = _merge_bwd(z, bf, bg, sp["bm"], dmerged, "merge_bwd")
    mix_entries, mix_sums = pair_sums([(dw_out, ["w_out"]), (_mm(y_fox, dbf, ta=True, name="mm_dw_branch"), ["w_branch_fox"]),
                                       (_mm(y_gla, dbg, ta=True, name="mm_dw_branch"), ["w_branch_gla"])])
    dy_fox = _mm(dbf, full["w_branch_fox"], tb=True, name="mm_dx_branch")
    dy_gla = _mm(dbg, full["w_branch_gla"], tb=True, name="mm_dx_branch")

    do_gla, dgr, ds_["ghn"] = _gla_out_bwd(o_gla, z, sp["ghn"], dy_gla, "gla_out_bwd")
    dgq, dgk, dgv, dla = _gla_bwd(z, la, sprev, do_gla, "gla_bwd")
    dot = _blocked_t(dy_fox, tq)
    dqt, delta, df_query = _fox_bwd_q(z, qt, kt, ot, dot, lse, frep, "fox_bwd_q")
    dfq = _unblocked_t(dqt)
    dfk, dfv, df_key = _fox_bwd_kv(z, qt, dy_fox, dot, lse, delta, frep, "fox_bwd_kv")
    df = df_query.reshape(FOX_HEADS, t) - df_key.reshape(FOX_HEADS, t)
    dfp = jnp.pad(df.T, ((0, 0), (0, SMALL_W - FOX_HEADS)))
    dlf = _cumsum_tokens(dfp, True, "cumsum_bwd")
    dsm, ds_["fb"], dgup, ds_["gb"] = _small_bwd(s, sp["fb"], gup, sp["gb"], dlf, dla, "small_bwd")
    dz = jnp.concatenate([dfq, dfk, dfv, dgq, dgk, dgv, dgr, dgl], axis=1)
    dw_big, got = _mm(u, dz, ta=True, name="mm_dw_in_exchange", comm=exchange(mix_entries, mix_sums))
    exchanged(mix_entries, mix_sums, got)
    dw_sm = _mm(u, dsm, ta=True, out_dtype=F32, name="mm_dw_in_small").astype(BF16)
    entries, sums = pair_sums([(_w_in_parts(dw_big, dw_sm), ["w_in"]), (dw_big[:, Z_GL:], ["w_merge_gate"]), (dgup[FOX_HEADS:FOX_HEADS + GLA_RANK], ["gla_gate_up"])])
    du1, got = _mm(dz, w_big, tb=True, out_dtype=F32, name="mm_dx_in_exchange", comm=exchange(entries, sums))
    exchanged(entries, sums, got)
    du2 = _mm(dsm, w_sm, tb=True, out_dtype=F32, name="mm_dx_in_small")
    dh1, df1, ds_["mix_norm"] = _norm_bwd(h1, [du1, du2], dh2, sp["mix_norm"], 0.5, "norm_bwd_2")

    dn1 = ffn_bwd(n1, gu1, a1, df1, w_gu1, full["ffn1_w_down"], "ffn1")
    grad_x, _, ds_["ffn1_norm"] = _norm_bwd(x, [dn1], dh1, sp["ffn1_norm"], 1.0, "norm_bwd_1")
    return loss, grad_x, reduced, ds_


def _half_rows(ref, which):
    r2 = ref.shape[0] // 2
    return ref.at[pl.ds(pl.multiple_of(which * r2, r2), r2)]


class _Comm:
    def __init__(self, ins, out_shape, sems, start, finish):
        self.ins, self.out_shape, self.sems, self.start, self.finish = ins, out_shape, sems, start, finish


def _run_comm(comm, name):
    n_in, n_out = len(comm.ins), len(comm.out_shape)

    def body(*refs):
        parts = refs[:n_in], refs[n_in:n_in + n_out], refs[n_in + n_out:]
        comm.start(*parts)
        comm.finish(*parts)

    return pl.pallas_call(
        body, name=name, in_specs=[ANY] * n_in, out_specs=[ANY] * n_out, out_shape=comm.out_shape,
        scratch_shapes=comm.sems, compiler_params=pltpu.CompilerParams(has_side_effects=True),
    )(*comm.ins)


def _hosted(body, comm, *, name, grid, in_specs, out_specs, out_shape, scratch_shapes, semantics, args):
    if comm is None:
        res = pl.pallas_call(
            body, name=name, grid=grid, in_specs=in_specs, out_specs=out_specs, out_shape=out_shape, scratch_shapes=scratch_shapes,
            compiler_params=pltpu.CompilerParams(dimension_semantics=semantics, vmem_limit_bytes=VMEM_LIMIT),
        )(*args)
        return res, None
    ni, no, ns = len(in_specs), len(out_shape), len(scratch_shapes)
    ci, co = len(comm.ins), len(comm.out_shape)

    def wrapped(*refs):
        h_in, c_in = refs[:ni], refs[ni:ni + ci]
        h_out, c_out = refs[ni + ci:ni + ci + no], refs[ni + ci + no:ni + ci + no + co]
        h_scr, c_sem = refs[ni + ci + no + co:ni + ci + no + co + ns], refs[ni + ci + no + co + ns:]
        ids = [pl.program_id(axis) for axis in range(len(grid))]
        first = functools.reduce(jnp.logical_and, [i == 0 for i in ids])
        last = functools.reduce(jnp.logical_and, [i == g - 1 for i, g in zip(ids, grid)])

        @pl.when(first)
        def _():
            comm.start(c_in, c_out, c_sem)

        body(*h_in, *h_out, *h_scr)

        @pl.when(last)
        def _():
            comm.finish(c_in, c_out, c_sem)

    res = pl.pallas_call(
        wrapped, name=name, grid=grid, in_specs=list(in_specs) + [ANY] * ci, out_specs=list(out_specs) + [ANY] * co,
        out_shape=list(out_shape) + list(comm.out_shape), scratch_shapes=list(scratch_shapes) + list(comm.sems),
        compiler_params=pltpu.CompilerParams(dimension_semantics=("arbitrary",) * len(grid), vmem_limit_bytes=VMEM_LIMIT, has_side_effects=True),
    )(*args, *comm.ins)
    return res[:no], res[no:]


def _ag_comm(shards, out_shape, places):
    n = len(shards)

    def copies(ins, outs, sems):
        ici_send, ici_recv, d2d_send, d2d_recv = sems
        x, y, c = lax.axis_index("x"), lax.axis_index("y"), lax.axis_index("c")
        chips = [(1 - x, y), (x, 1 - y), (1 - x, 1 - y)]
        slot = lambda chip: 2 * chip[0] + chip[1]

        def window(wi, origin, half):
            out, row_step, col_base, col_step = places[wi]
            r, cc = shards[wi].shape
            rows = pl.ds(pl.multiple_of(slot(origin) * row_step + half * (r // 2), r // 2), r // 2)
            cols = pl.ds(pl.multiple_of(col_base + slot(origin) * col_step, HEAD_DIM), cc) if col_step else pl.ds(col_base, cc)
            return outs[out].at[rows, cols]

        def over_ici(wi, j, origin):
            return pltpu.make_async_remote_copy(
                src_ref=_half_rows(ins[wi], c), dst_ref=window(wi, origin, c),
                send_sem=ici_send.at[3 * wi + j], recv_sem=ici_recv.at[3 * wi + j],
                device_id=(chips[j][0], chips[j][1], c), device_id_type=MESH)

        def over_d2d(wi, j, half):
            place = window(wi, chips[j], half)
            return pltpu.make_async_remote_copy(
                src_ref=place, dst_ref=place, send_sem=d2d_send.at[3 * wi + j], recv_sem=d2d_recv.at[3 * wi + j],
                device_id=(x, y, 1 - c), device_id_type=MESH)

        return over_ici, over_d2d, (x, y), chips, c

    def chunk_rows(wi):
        r, cc = shards[wi].shape
        item = shards[wi].dtype.itemsize
        return _pick(r, max(32 // item, BOUNCE_BYTES // (cc * item)), 32 // item)

    def start(ins, outs, scratch):
        over_ici, _, me, _, _ = copies(ins, outs, scratch[:4])
        for wi in range(n):
            for j in range(3):
                over_ici(wi, j, me).start()
        loc_sems = scratch[4]
        for wi in range(n):
            out, row_step, col_base, col_step = places[wi]
            r, cc = shards[wi].shape
            rc = chunk_rows(wi)
            buf = scratch[5 + wi]
            slot = 2 * me[0] + me[1]
            cols = pl.ds(pl.multiple_of(col_base + slot * col_step, HEAD_DIM), cc) if col_step else pl.ds(col_base, cc)

            def load(k):
                return pltpu.make_async_copy(ins[wi].at[pl.ds(k * rc, rc)], buf.at[k % 2], loc_sems.at[2 * wi])

            def store(k):
                rows = pl.ds(pl.multiple_of(slot * row_step + k * rc, rc), rc)
                return pltpu.make_async_copy(buf.at[k % 2], outs[out].at[rows, cols], loc_sems.at[2 * wi + 1])

            load(0).start()
            for k in range(r // rc):
                load(k).wait()
                if k + 1 < r // rc:
                    load(k + 1).start()
                store(k).start()
                store(k).wait()

    def finish(ins, outs, scratch):
        over_ici, over_d2d, me, chips, c = copies(ins, outs, scratch[:4])
        for wi in range(n):
            for j in range(3):
                over_ici(wi, j, chips[j]).wait_recv()
                over_d2d(wi, j, c).start()
        for wi in range(n):
            for j in range(3):
                over_d2d(wi, j, 1 - c).wait_recv()
        for wi in range(n):
            for j in range(3):
                over_ici(wi, j, me).wait_send()
                over_d2d(wi, j, c).wait_send()

    bounce = [pltpu.VMEM((min(2, s.shape[0] // chunk_rows(wi)), chunk_rows(wi), s.shape[1]), s.dtype) for wi, s in enumerate(shards)]
    return _Comm(list(shards), list(out_shape), [pltpu.SemaphoreType.DMA((3 * n,))] * 4 + [pltpu.SemaphoreType.DMA((2 * n,))] + bounce,
                 start, finish)


def _pair_swap(parts, name):
    n = len(parts)

    def body(*refs):
        ins, outs = refs[:n], refs[n:2 * n]
        send_sems, recv_sems = refs[2 * n:]
        x, y, c = lax.axis_index("x"), lax.axis_index("y"), lax.axis_index("c")

        def swap(wi):
            r2 = parts[wi].shape[1] // 2
            return pltpu.make_async_remote_copy(
                src_ref=ins[wi].at[:, pl.ds(pl.multiple_of((1 - c) * r2, r2), r2)], dst_ref=outs[wi],
                send_sem=send_sems.at[wi], recv_sem=recv_sems.at[wi], device_id=(x, y, 1 - c), device_id_type=MESH)

        copies = [swap(wi) for wi in range(n)]
        for cp in copies:
            cp.start()
        for cp in copies:
            cp.wait()

    return pl.pallas_call(
        body, name=name, in_specs=[ANY] * n, out_specs=[ANY] * n,
        out_shape=[jax.ShapeDtypeStruct((s.shape[0], s.shape[1] // 2, s.shape[2]), s.dtype) for s in parts],
        scratch_shapes=[pltpu.SemaphoreType.DMA((n,))] * 2, compiler_params=pltpu.CompilerParams(has_side_effects=True),
    )(*parts)


def _row_tile(r, c, budget=1 << 19):
    return r if r <= 8 else _pick(r, max(8, budget // c), 8)


def _sum_half(parts, other, cidx, name):
    nl, r, cc = parts.shape
    r2 = r // 2
    tr = _row_tile(r2, cc)

    def body(c_ref, p_ref, q_ref, o_ref):
        o_ref[...] = (_f(p_ref[...]) + _f(q_ref[...])).astype(o_ref.dtype)

    return pl.pallas_call(
        body, name=name, out_shape=jax.ShapeDtypeStruct((nl, r2, cc), parts.dtype),
        grid_spec=pltpu.PrefetchScalarGridSpec(
            num_scalar_prefetch=1, grid=(nl, r2 // tr),
            in_specs=[pl.BlockSpec((None, None, tr, cc), lambda l, i, c_ref: (l, c_ref[0], i, 0)),
                      pl.BlockSpec((None, tr, cc), lambda l, i, c_ref: (l, i, 0))],
            out_specs=pl.BlockSpec((None, tr, cc), lambda l, i, c_ref: (l, i, 0))),
        compiler_params=pltpu.CompilerParams(dimension_semantics=("parallel", "parallel"), vmem_limit_bytes=VMEM_LIMIT),
    )(cidx, parts.reshape(nl, 2, r2, cc), other)


def _exchange_comm(sums, entries):
    n = len(entries)

    def copies(ins, outs, sems):
        send_sems, recv_sems = sems
        x, y, c = lax.axis_index("x"), lax.axis_index("y"), lax.axis_index("c")
        chips = [(1 - x, y), (x, 1 - y), (1 - x, 1 - y)]
        slot = lambda chip: 2 * chip[0] + chip[1]

        def piece(wi, dest):
            si, mode, first, cols = entries[wi]
            if mode == "cols":
                return ins[si].at[0, :, pl.ds(pl.multiple_of(first + slot(dest) * cols, HEAD_DIM), cols)]
            return ins[si].at[slot(dest) if mode == "slot" else 0]

        def remote(wi, j, origin):
            return pltpu.make_async_remote_copy(
                src_ref=piece(wi, chips[j]), dst_ref=outs[wi].at[slot(origin)],
                send_sem=send_sems.at[3 * wi + j], recv_sem=recv_sems.at[3 * wi + j],
                device_id=(chips[j][0], chips[j][1], c), device_id_type=MESH)

        return remote, (x, y), chips

    def start(ins, outs, sems):
        remote, me, _ = copies(ins, outs, sems)
        for wi in range(n):
            for j in range(3):
                remote(wi, j, me).start()

    def finish(ins, outs, sems):
        remote, me, chips = copies(ins, outs, sems)
        for wi in range(n):
            for j in range(3):
                remote(wi, j, chips[j]).wait_recv()
        for wi in range(n):
            for j in range(3):
                remote(wi, j, me).wait_send()

    out_shape = [jax.ShapeDtypeStruct((4, sums[si].shape[1], cols), sums[si].dtype) for si, _, _, cols in entries]
    return _Comm(list(sums), out_shape, [pltpu.SemaphoreType.DMA((3 * n,))] * 2, start, finish)


def _sum_chips(got, own, mode, first, chip, name):
    _, r2, cc = got.shape
    tr = _row_tile(r2, cc)
    own_block = {"slot": lambda i, chip_ref: (chip_ref[0], i, 0), "same": lambda i, chip_ref: (0, i, 0),
                 "cols": lambda i, chip_ref: (0, i, first // cc + chip_ref[0])}[mode]

    def body(chip_ref, g_ref, own_ref, o_ref):
        term = lambda k: jnp.where(chip_ref[0] == k, _f(own_ref[...]), _f(g_ref[k]))
        o_ref[...] = ((term(0) + term(1)) + term(2)) + term(3)

    return pl.pallas_call(
        body, name=name, out_shape=jax.ShapeDtypeStruct((r2, cc), F32),
        grid_spec=pltpu.PrefetchScalarGridSpec(
            num_scalar_prefetch=1, grid=(r2 // tr,),
            in_specs=[pl.BlockSpec((4, tr, cc), lambda i, chip_ref: (0, i, 0)),
                      pl.BlockSpec((None, tr, cc), own_block)],
            out_specs=pl.BlockSpec((tr, cc), lambda i, chip_ref: (i, 0))),
        compiler_params=pltpu.CompilerParams(dimension_semantics=("parallel",), vmem_limit_bytes=VMEM_LIMIT),
    )(chip, got, own)


def _pair_gather(halves, name):
    n = len(halves)

    def body(*refs):
        ins, outs = refs[:n], refs[n:2 * n]
        send_sems, recv_sems = refs[2 * n:]
        x, y, c = lax.axis_index("x"), lax.axis_index("y"), lax.axis_index("c")
        copies = [pltpu.make_async_remote_copy(
            src_ref=ins[wi], dst_ref=outs[wi], send_sem=send_sems.at[wi], recv_sem=recv_sems.at[wi],
            device_id=(x, y, 1 - c), device_id_type=MESH) for wi in range(n)]
        for cp in copies:
            cp.start()
        for cp in copies:
            cp.wait()

    return pl.pallas_call(
        body, name=name, in_specs=[ANY] * n, out_specs=[ANY] * n,
        out_shape=[jax.ShapeDtypeStruct(s.shape, s.dtype) for s in halves],
        scratch_shapes=[pltpu.SemaphoreType.DMA((n,))] * 2, compiler_params=pltpu.CompilerParams(has_side_effects=True),
    )(*halves)


def _adamw_update(g, w, m, v):
    m_new = ADAM_B1 * m + (1.0 - ADAM_B1) * g
    v_new = ADAM_B2 * v + (1.0 - ADAM_B2) * jnp.square(g)
    m_hat = m_new / (1.0 - ADAM_B1 ** ADAM_STEP)
    v_hat = v_new / (1.0 - ADAM_B2 ** ADAM_STEP)
    return -ADAM_LR * (m_hat / (jnp.sqrt(v_hat) + ADAM_EPS) + ADAM_WD * w), m_new, v_new


def _adamw_whole(g, w, m, v, name):
    r, c = w.shape
    tc = 256

    def body(g_ref, w_ref, m_ref, v_ref, d_ref, nm_ref, nv_ref):
        d_ref[...], nm_ref[...], nv_ref[...] = _adamw_update(g_ref[...], w_ref[...], m_ref[...], v_ref[...])

    blk = pl.BlockSpec((r, tc), lambda i: (0, i))
    return pl.pallas_call(
        body, name=name, grid=(c // tc,), in_specs=[blk] * 4, out_specs=[blk] * 3, out_shape=[jax.ShapeDtypeStruct((r, c), F32)] * 3,
        compiler_params=pltpu.CompilerParams(dimension_semantics=("parallel",), vmem_limit_bytes=VMEM_LIMIT),
    )(g, w, m, v)


def _adamw(mine, other, cidx, w, m, v, name):
    r, c = w.shape
    tr = _row_tile(r // 2, c, 1 << 18)
    nh = (r // 2) // tr

    def body(c_ref, mine_ref, other_ref, w_ref, m_ref, v_ref, g_ref, d_ref, nm_ref, nv_ref):
        g = jnp.where(pl.program_id(0) // nh == c_ref[0], mine_ref[...], other_ref[...])
        g_ref[...] = g
        d_ref[...], nm_ref[...], nv_ref[...] = _adamw_update(g, w_ref[...], m_ref[...], v_ref[...])

    blk = pl.BlockSpec((tr, c), lambda i, c_ref: (i, 0))
    mine_spec = pl.BlockSpec((tr, c), lambda i, c_ref: (jnp.where(i // nh == c_ref[0], i % nh, 0), 0))
    other_spec = pl.BlockSpec((tr, c), lambda i, c_ref: (jnp.where(i // nh == c_ref[0], 0, i % nh), 0))
    return pl.pallas_call(
        body, name=name, out_shape=[jax.ShapeDtypeStruct((r, c), F32)] * 4,
        grid_spec=pltpu.PrefetchScalarGridSpec(
            num_scalar_prefetch=1, grid=(r // tr,), in_specs=[mine_spec, other_spec, blk, blk, blk], out_specs=[blk] * 4),
        compiler_params=pltpu.CompilerParams(dimension_semantics=("arbitrary",), vmem_limit_bytes=VMEM_LIMIT),
    )(cidx, mine, other, w, m, v)


BIG = ["ffn1_w_gate", "ffn1_w_up", "ffn1_w_down", "w_in", "gla_gate_up", "w_branch_fox", "w_branch_gla", "w_merge_gate", "w_out",
       "ffn2_w_gate", "ffn2_w_up", "ffn2_w_down", "w_ple_proj", "w_ple_gate"]
ROW_SHARDED = ("ffn1_w_down", "w_out", "ffn2_w_down", "w_ple_gate")
FUSED = {"ffn1_w_gate": ("gu1", 0, 2), "ffn1_w_up": ("gu1", 1, 2), "ffn2_w_gate": ("gu2", 0, 2), "ffn2_w_up": ("gu2", 1, 2)}
SMALL = ["ffn1_norm", "mix_norm", "fox_forget_bias", "gla_gate_bias", "gla_head_norm", "b_merge_gate", "ffn2_norm", "ple_norm", "final_norm"]
NAMES = ["ffn1_norm", "ffn1_w_gate", "ffn1_w_up", "ffn1_w_down", "mix_norm", "w_in", "fox_forget_bias", "gla_gate_up", "gla_gate_bias",
         "gla_head_norm", "w_branch_fox", "w_branch_gla", "w_merge_gate", "b_merge_gate", "w_out", "ffn2_norm", "ffn2_w_gate", "ffn2_w_up",
         "ffn2_w_down", "ple_norm", "w_ple_proj", "w_ple_gate", "final_norm"]
W_IN_COLS = (FOX_W, FOX_W, FOX_W, FOX_HEADS, GLA_KW, GLA_KW, GLA_VW, GLA_VW, GLA_RANK)
SMALL_ROWS, SMALL_COLS = 16, 1024


def _shard_parts(full, name):
    if name in ROW_SHARDED:
        return full.reshape(4, full.shape[0] // 4, full.shape[1])
    return jnp.transpose(full.reshape(full.shape[0], 4, full.shape[1] // 4), (1, 0, 2))


W_IN_WIDE = ((0, 3 * FOX_W), (3 * FOX_W + FOX_HEADS, 3 * FOX_W + FOX_HEADS + 2 * GLA_KW + 2 * GLA_VW))
W_IN_NARROW = ((3 * FOX_W, 3 * FOX_W + FOX_HEADS), (sum(W_IN_COLS) - GLA_RANK, sum(W_IN_COLS)))


def _in_layout(stacked, w_merge_gate):
    per = stacked.shape[1]

    def columns(lo, hi):
        out = []
        while lo < hi:
            j, end = lo // per, min(hi, (lo // per + 1) * per)
            out.append(stacked[j * D_MODEL:(j + 1) * D_MODEL, lo - j * per:end - j * per])
            lo = end
        return out
    big = jnp.concatenate(columns(*W_IN_WIDE[0]) + columns(*W_IN_WIDE[1]) + [w_merge_gate], axis=1)
    sm = jnp.concatenate(columns(*W_IN_NARROW[0]) + columns(*W_IN_NARROW[1]) + [jnp.zeros((D_MODEL, SMALL_W - FOX_HEADS - GLA_RANK), BF16)], axis=1)
    return big, sm


def _w_in_parts(dw_big, dw_sm):
    runs = [(W_IN_WIDE[0], dw_big, Z_FQ), (W_IN_NARROW[0], dw_sm, 0), (W_IN_WIDE[1], dw_big, Z_GQ), (W_IN_NARROW[1], dw_sm, FOX_HEADS)]
    per = sum(W_IN_COLS) // 4
    parts = []
    for j in range(4):
        lo, hi, pieces = j * per, (j + 1) * per, []
        for (a, b), src, at in runs:
            if max(a, lo) < min(b, hi):
                pieces.append(src[:, at + max(a, lo) - a:at + min(b, hi) - a])
        parts.append(jnp.concatenate(pieces, axis=1))
    return jnp.stack(parts)


def _pad_lanes(a, width):
    return jnp.pad(a, ((0, 0), (0, width - a.shape[1])))


def kernel(x, p, ffn1_norm, ffn1_w_gate, ffn1_w_up, ffn1_w_down, mix_norm, w_in, fox_forget_bias, gla_gate_up, gla_gate_bias, gla_head_norm, w_branch_fox, w_branch_gla, w_merge_gate, b_merge_gate, w_out, ffn2_norm, ffn2_w_gate, ffn2_w_up, ffn2_w_down, ple_norm, w_ple_proj, w_ple_gate, final_norm, loss_target, m_ffn1_norm, m_ffn1_w_gate, m_ffn1_w_up, m_ffn1_w_down, m_mix_norm, m_w_in, m_fox_forget_bias, m_gla_gate_up, m_gla_gate_bias, m_gla_head_norm, m_w_branch_fox, m_w_branch_gla, m_w_merge_gate, m_b_merge_gate, m_w_out, m_ffn2_norm, m_ffn2_w_gate, m_ffn2_w_up, m_ffn2_w_down, m_ple_norm, m_w_ple_proj, m_w_ple_gate, m_final_norm, v_ffn1_norm, v_ffn1_w_gate, v_ffn1_w_up, v_ffn1_w_down, v_mix_norm, v_w_in, v_fox_forget_bias, v_gla_gate_up, v_gla_gate_bias, v_gla_head_norm, v_w_branch_fox, v_w_branch_gla, v_w_merge_gate, v_b_merge_gate, v_w_out, v_ffn2_norm, v_ffn2_w_gate, v_ffn2_w_up, v_ffn2_w_down, v_ple_norm, v_w_ple_proj, v_w_ple_gate, v_final_norm):
    args = dict(locals())
    wts = {n: args[n] for n in NAMES}
    mom = {n: args["m_" + n] for n in NAMES}
    var = {n: args["v_" + n] for n in NAMES}
    two_d = lambda a: a.reshape(-1, a.shape[-1])

    wire = lambda n: F32 if n == "gla_gate_up" else BF16
    cidx = lax.axis_index("c").astype(jnp.int32).reshape(1)
    chip = (2 * lax.axis_index("x") + lax.axis_index("y")).astype(jnp.int32)
    shards = {n: two_d(wts[n]).astype(wire(n)) for n in BIG}
    sp = {
        "ffn1_norm": two_d(ffn1_norm), "mix_norm": two_d(mix_norm), "fb": _pad_lanes(two_d(fox_forget_bias), SMALL_W),
        "gb": two_d(gla_gate_bias), "ghn": two_d(gla_head_norm), "bm": two_d(b_merge_gate), "ffn2_norm": two_d(ffn2_norm),
        "ple_norm": two_d(ple_norm), "final_norm": two_d(final_norm),
    }

    loss, grad_x, reduced, ds_ = _local_step(x[0], p[0, 0], loss_target[0], shards, sp, cidx, chip)

    small_g = {"ffn1_norm": ds_["ffn1_norm"], "mix_norm": ds_["mix_norm"], "fox_forget_bias": ds_["fb"][:, :FOX_HEADS],
               "gla_gate_bias": ds_["gb"], "gla_head_norm": ds_["ghn"], "b_merge_gate": ds_["bm"], "ffn2_norm": ds_["ffn2_norm"],
               "ple_norm": ds_["ple_norm"], "final_norm": ds_["final_norm"]}
    small_w = sum(two_d(wts[n]).shape[1] for n in SMALL)
    assert small_w <= SMALL_ROWS * SMALL_COLS
    packed = lambda d: _pad_lanes(jnp.concatenate([two_d(d[n]) for n in SMALL], axis=1), SMALL_ROWS * SMALL_COLS).reshape(SMALL_ROWS, SMALL_COLS)
    parts = [packed(small_g)[None]]
    pair_sums = [_sum_half(a, b, cidx, "sum_half") for a, b in zip(parts, _pair_swap(parts, "pair_swap"))]
    reduced["small"] = (_run_comm(_exchange_comm(pair_sums, [(0, "same", 0, SMALL_COLS)]), "chip_exchange")[0], pair_sums[0], "same", 0)
    mine = [_sum_chips(*reduced[n], chip.reshape(1), "sum_chips") for n in BIG + ["small"]]
    other = _pair_gather(mine, "pair_gather")

    out = {}
    for n, a, b in zip(BIG, mine[:-1], other[:-1]):
        if n == "w_in":
            g_t = jnp.where(cidx[0] == 0, jnp.concatenate([a, b]), jnp.concatenate([b, a])).T
            flip = lambda t: jnp.swapaxes(t, 1, 2)[0]
            res = _adamw_whole(g_t, flip(wts[n]), flip(mom[n]), flip(var[n]), "adamw_w_in")
            out[n] = [jnp.swapaxes(r[None], 1, 2) for r in [g_t, *res]]
            continue
        out[n] = [r.reshape(wts[n].shape) for r in _adamw(a, b, cidx, two_d(wts[n]), two_d(mom[n]), two_d(var[n]), "adamw_" + n)]
    small_out = [r.reshape(1, SMALL_ROWS * SMALL_COLS) for r in _adamw(mine[-1], other[-1], cidx, packed(wts), packed(mom), packed(var), "adamw_small")]
    off = 0
    for n in SMALL:
        cw = two_d(wts[n]).shape[1]
        out[n] = [r[:, off:off + cw].reshape(wts[n].shape) for r in small_out]
        off += cw

    total = lax.psum(loss[0, 0], ("x", "y", "c"))
    return (total, grad_x[None], *[out[n][0] for n in NAMES], *[out[n][1] for n in NAMES],
            *[out[n][2] for n in NAMES], *[out[n][3] for n in NAMES])
```

```python
import functools

import jax
import jax.numpy as jnp
from jax import lax
from jax.experimental import pallas as pl
from jax.experimental.pallas import tpu as pltpu

F32 = jnp.float32
BF16 = jnp.bfloat16
MESH = pl.DeviceIdType.MESH
ANY = pl.BlockSpec(memory_space=pl.ANY)

D_MODEL = 2048
FOX_HEADS = 8
HEAD_DIM = 128
GLA_HEADS = 4
GLA_VAL_DIM = 256
GLA_RANK = 16
GLA_TAU = 16.0
CHUNK = 64
EPS = 1e-6
FOX_W = FOX_HEADS * HEAD_DIM
GLA_KW = GLA_HEADS * HEAD_DIM
GLA_VW = GLA_HEADS * GLA_VAL_DIM
Z_FQ, Z_FK, Z_FV, Z_GQ, Z_GK, Z_GV, Z_GR, Z_GL = 0, 1024, 2048, 3072, 3584, 4096, 5120, 6144
Z_W = Z_GL + 2 * D_MODEL
SMALL_W = 128
NEG = -1e30

ADAM_LR, ADAM_B1, ADAM_B2, ADAM_EPS, ADAM_WD, ADAM_STEP = 0.001, 0.9, 0.999, 1e-08, 0.01, 10

VMEM_LIMIT = 56 * 1024 * 1024
BOUNCE_BYTES = 2 * 1024 * 1024


def _pick(n, target, mult=128):
    if n <= target:
        return n
    best = None
    for d in range(mult, target + 1, mult):
        if n % d == 0:
            best = d
    assert best is not None, (n, target)
    return best


def _mm(a, b, *, ta=False, tb=False, out_dtype=BF16, name, comm=None):
    m, k = (a.shape[1], a.shape[0]) if ta else a.shape
    n = b.shape[0] if tb else b.shape[1]
    assert (b.shape[1] if tb else b.shape[0]) == k
    bk = _pick(k, 4096)
    nk = k // bk
    bm, bn = _pick(m, 1024), _pick(n, 1024)
    dims = (((0 if ta else 1,), (1 if tb else 0,)), ((), ()))

    def body(a_ref, b_ref, o_ref, acc_ref):
        part = lax.dot_general(a_ref[...], b_ref[...], dims, preferred_element_type=F32)
        if nk == 1:
            o_ref[...] = part.astype(o_ref.dtype)
            return
        kk = pl.program_id(2)

        @pl.when(kk == 0)
        def _():
            acc_ref[...] = part

        @pl.when(kk > 0)
        def _():
            acc_ref[...] += part

        @pl.when(kk == nk - 1)
        def _():
            o_ref[...] = acc_ref[...].astype(o_ref.dtype)

    a_spec = pl.BlockSpec((bk, bm), lambda i, j, kk: (kk, i)) if ta else pl.BlockSpec((bm, bk), lambda i, j, kk: (i, kk))
    b_spec = pl.BlockSpec((bn, bk), lambda i, j, kk: (j, kk)) if tb else pl.BlockSpec((bk, bn), lambda i, j, kk: (kk, j))
    (out,), travelled = _hosted(
        body, comm, name=name, grid=(m // bm, n // bn, nk),
        in_specs=[a_spec, b_spec], out_specs=[pl.BlockSpec((bm, bn), lambda i, j, kk: (i, j))],
        out_shape=[jax.ShapeDtypeStruct((m, n), out_dtype)], scratch_shapes=[pltpu.VMEM((bm, bn), F32)],
        semantics=("parallel", "parallel", "arbitrary"), args=(a, b))
    return out if comm is None else (out, travelled)


def _rowwise(fn, tiled, bcast, outs, reds=(), *, tt, name):
    t = tiled[0][0].shape[0]
    tt = min(tt, t)
    nin, nout = len(tiled) + len(bcast), len(outs)
    splits = [s[3] for s in tiled] + [s[1] for s in bcast]

    def store(ref, val, acc):
        off = 0
        for piece in val if isinstance(val, (tuple, list)) else (val,):
            w = piece.shape[-1]
            if acc:
                ref[:, off:off + w] += piece.astype(ref.dtype)
            else:
                ref[:, off:off + w] = piece.astype(ref.dtype)
            off += w
        assert off == ref.shape[-1], (name, off, ref.shape)

    def body(*refs):
        args = []
        for ref, sp in zip(refs[:nin], splits):
            if sp is None:
                args.append(ref[...])
            else:
                off = 0
                for w in sp:
                    args.append(ref[:, off:off + w])
                    off += w
        res = fn(*args)
        res = res if isinstance(res, (tuple, list)) else (res,)
        assert len(res) == nout + len(reds), (name, len(res))
        for ref, val in zip(refs[nin:nin + nout], res[:nout]):
            store(ref, val, False)
        if reds:
            @pl.when(pl.program_id(0) == 0)
            def _():
                for ref in refs[nin + nout:]:
                    ref[...] = jnp.zeros(ref.shape, ref.dtype)
            for ref, val in zip(refs[nin + nout:], res[nout:]):
                store(ref, val, True)

    in_specs = [pl.BlockSpec((tt, w), functools.partial(lambda i, cb: (i, cb), cb=cb)) for (_, w, cb, _) in tiled]
    in_specs += [pl.BlockSpec(arr.shape, lambda i: (0, 0)) for (arr, _) in bcast]
    out_specs = [pl.BlockSpec((tt, w), lambda i: (i, 0)) for (w, _) in outs]
    out_specs += [pl.BlockSpec((r, w), lambda i: (0, 0)) for (r, w) in reds]
    out_shape = [jax.ShapeDtypeStruct((t, w), dt) for (w, dt) in outs] + [jax.ShapeDtypeStruct((r, w), F32) for (r, w) in reds]
    return pl.pallas_call(
        body, name=name, grid=(t // tt,), in_specs=in_specs, out_specs=out_specs, out_shape=out_shape,
        compiler_params=pltpu.CompilerParams(dimension_semantics=("arbitrary" if reds else "parallel",), vmem_limit_bytes=VMEM_LIMIT),
    )(*[s[0] for s in tiled], *[s[0] for s in bcast])


def _full(arr):
    return (arr, arr.shape[1], 0, None)


def _f(x):
    return x.astype(F32)


def _rms(x, g):
    return x * lax.rsqrt(jnp.mean(x * x, axis=-1, keepdims=True) + EPS) * g


def _log_sigmoid(x):
    return jnp.minimum(x, 0.0) - jnp.log1p(jnp.exp(-jnp.abs(x)))


def _silu(x):
    return x * jax.nn.sigmoid(x)


def _norm_fwd(x, g, name):
    return _rowwise(lambda xb, gb: _rms(_f(xb), gb), [_full(x)], [(g, None)], [(x.shape[1], BF16)], tt=256, name=name)[0]


def _resnorm_fwd(res, branch, g, coef, name):
    def fn(rb, bb, gb):
        h = rb + coef * _f(bb)
        return h, _rms(h, gb)
    d = res.shape[1]
    return _rowwise(fn, [_full(res), _full(branch)], [(g, None)], [(d, F32), (d, BF16)], tt=256, name=name)


def _norm_bwd(h, dns, dres, g, coef, name):
    nd = len(dns)

    def fn(hb, *rest):
        dn = _f(rest[0])
        for extra in rest[1:nd]:
            dn = dn + _f(extra)
        dr, gb = rest[nd], rest[nd + 1]
        _, vjp = jax.vjp(_rms, hb, gb)
        dh, dg = vjp(dn)
        dh = dh + dr
        return dh, coef * dh, dg
    d = h.shape[1]
    return _rowwise(fn, [_full(h)] + [_full(x) for x in dns] + [_full(dres)], [(g, None)],
                    [(d, F32), (d, BF16)], [(1, d)], tt=256, name=name)


def _act_fwd(gu, name):
    ff = gu.shape[1] // 2
    return _rowwise(lambda gb, ub: _silu(_f(gb)) * _f(ub), [(gu, 2 * ff, 0, (ff, ff))], [], [(ff, BF16)], tt=256, name=name)[0]


def _act_bwd(gu, da, name):
    ff = gu.shape[1] // 2

    def fn(gb, ub, dab):
        _, vjp = jax.vjp(lambda p, q: _silu(p) * q, _f(gb), _f(ub))
        return (vjp(_f(dab)),)
    return _rowwise(fn, [(gu, 2 * ff, 0, (ff, ff)), _full(da)], [], [(2 * ff, BF16)], tt=128, name=name)[0]


def _merge(glf, glg, bf, bg, bmf, bmg):
    return jax.nn.sigmoid(_f(glf) + bmf) * _f(bf) + jax.nn.sigmoid(_f(glg) + bmg) * _f(bg)


def _merge_fwd(z, bf, bg, bm, name):
    d = D_MODEL
    return _rowwise(_merge, [(z, d, Z_GL // d, None), (z, d, Z_GL // d + 1, None), _full(bf), _full(bg)], [(bm, (d, d))],
                    [(d, BF16)], tt=256, name=name)[0]


def _merge_bwd(z, bf, bg, bm, dm, name):
    d = D_MODEL

    def fn(glf, glg, bfb, bgb, dmb, bmf, bmg):
        _, vjp = jax.vjp(_merge, _f(glf), _f(glg), _f(bfb), _f(bgb), bmf, bmg)
        dglf, dglg, dbf, dbg, dbmf, dbmg = vjp(_f(dmb))
        return (dglf, dglg), dbf, dbg, (dbmf, dbmg)
    return _rowwise(fn, [(z, d, Z_GL // d, None), (z, d, Z_GL // d + 1, None), _full(bf), _full(bg), _full(dm)], [(bm, (d, d))],
                    [(2 * d, BF16), (d, BF16), (d, BF16)], [(1, 2 * d)], tt=128, name=name)


def _gla_out(o, gr, g):
    return _rms(o, g) * _silu(_f(gr))


_PER_HEAD = (GLA_VAL_DIM,) * GLA_HEADS


def _gla_out_fwd(o, z, g, name):
    nh = GLA_HEADS

    def fn(*blocks):
        return (tuple(_gla_out(blocks[h], blocks[nh + h], blocks[2 * nh]) for h in range(nh)),)
    return _rowwise(fn, [(o, GLA_VW, 0, _PER_HEAD), (z, GLA_VW, Z_GR // GLA_VW, _PER_HEAD)], [(g, None)], [(GLA_VW, BF16)], tt=256, name=name)[0]


def _gla_out_bwd(o, z, g, dy, name):
    nh = GLA_HEADS

    def fn(*blocks):
        gb = blocks[3 * nh]
        grads = []
        for h in range(nh):
            _, vjp = jax.vjp(_gla_out, blocks[h], _f(blocks[nh + h]), gb)
            grads.append(vjp(_f(blocks[2 * nh + h])))
        dg = grads[0][2]
        for h in range(1, nh):
            dg = dg + grads[h][2]
        return tuple(gr[0] for gr in grads), tuple(gr[1] for gr in grads), dg
    return _rowwise(fn, [(o, GLA_VW, 0, _PER_HEAD), (z, GLA_VW, Z_GR // GLA_VW, _PER_HEAD), (dy, GLA_VW, 0, _PER_HEAD)], [(g, None)],
                    [(GLA_VW, F32), (GLA_VW, BF16)], [(1, GLA_VAL_DIM)], tt=256, name=name)


def _small_gates(s, fb, gup, gb):
    lane = lax.broadcasted_iota(jnp.int32, s.shape, 1)
    lf = jnp.where(lane < FOX_HEADS, _log_sigmoid(s + fb), 0.0)
    pre = jnp.dot(s.astype(BF16), gup.astype(BF16), preferred_element_type=F32) + gb
    return lf, _log_sigmoid(pre) / GLA_TAU


def _small_fwd(s, fb, gup, gb, name):
    return _rowwise(_small_gates, [_full(s)], [(fb, None), (gup, None), (gb, None)], [(SMALL_W, F32), (GLA_KW, F32)], tt=256, name=name)


def _small_bwd(s, fb, gup, gb, dlf, dla, name):
    def fn(sb, dlfb, dlab, fbb, gupb, gbb):
        _, vjp = jax.vjp(_small_gates, sb, fbb, gupb, gbb)
        return vjp((dlfb, dlab))
    return _rowwise(fn, [_full(s), _full(dlf), _full(dla)], [(fb, None), (gup, None), (gb, None)],
                    [(SMALL_W, BF16)], [(1, SMALL_W), (SMALL_W, GLA_KW), (1, GLA_KW)], tt=256, name=name)


def _head_fn(h3, pgl, pp, tgt, gf):
    h4 = h3 + jax.nn.sigmoid(pgl) * pp
    err = _rms(h4, gf) - tgt
    return 0.5 * jnp.sum(jnp.mean(err * err, axis=-1, keepdims=True))


def _head(h3, pgl, pp, tgt, gf, name):
    def fn(hb, gl, pb, tb, gfb):
        loss, vjp = jax.vjp(_head_fn, hb, _f(gl), _f(pb), tb, gfb)
        dh, dgl, dpp, _, dgf = vjp(jnp.ones((), F32))
        return dh, dgl, dpp, jnp.full((1, 128), loss, F32), dgf
    d = h3.shape[1]
    return _rowwise(fn, [_full(h3), _full(pgl), _full(pp), _full(tgt)], [(gf, None)],
                    [(d, F32), (d, BF16), (d, BF16)], [(1, 128), (1, d)], tt=256, name=name)


def _cumsum_tokens(a, reverse, name):
    t, w = a.shape
    r = min(256, t)
    nb = t // r

    def body(a_ref, o_ref, carry_ref):
        @pl.when(pl.program_id(0) == 0)
        def _():
            carry_ref[...] = jnp.zeros(carry_ref.shape, F32)
        row = lax.broadcasted_iota(jnp.int32, (r, r), 0)
        col = lax.broadcasted_iota(jnp.int32, (r, r), 1)
        tri = ((col >= row) if reverse else (col <= row)).astype(F32)
        blk = a_ref[...]
        o_ref[...] = jnp.dot(tri, blk, preferred_element_type=F32, precision=lax.Precision.HIGHEST) + carry_ref[...]
        carry_ref[...] += jnp.sum(blk, axis=0, keepdims=True)

    idx = (lambda i: (nb - 1 - i, 0)) if reverse else (lambda i: (i, 0))
    return pl.pallas_call(
        body, name=name, grid=(nb,), in_specs=[pl.BlockSpec((r, w), idx)], out_specs=pl.BlockSpec((r, w), idx),
        out_shape=jax.ShapeDtypeStruct((t, w), F32), scratch_shapes=[pltpu.VMEM((1, w), F32)],
        compiler_params=pltpu.CompilerParams(dimension_semantics=("arbitrary",)),
    )(a)


FOX_TQ, FOX_TK = 256, 512
FOX_SCALE = HEAD_DIM ** -0.5


def _fox_tiles(t):
    tq, tk = min(FOX_TQ, t), min(FOX_TK, t)
    return tq, tk, t // tq, t // tk


def _blocked_t(a, blk):
    return a.reshape(a.shape[0] // blk, blk, a.shape[1]).transpose(0, 2, 1)


def _unblocked_t(b):
    return b.transpose(0, 2, 1).reshape(b.shape[0] * b.shape[2], b.shape[1])


def _fox_scores(k, qt, frep, i, j, masked):
    tk, tq = k.shape[0], qt.shape[1]
    st = jnp.dot(k, qt, preferred_element_type=F32) * FOX_SCALE - jnp.tile(frep, (1, tq // HEAD_DIM))
    if masked:
        key = j * tk + lax.broadcasted_iota(jnp.int32, (tk, tq), 0)
        query = i * tq + lax.broadcasted_iota(jnp.int32, (tk, tq), 1)
        st = jnp.where(key <= query, st, NEG)
    return st


def _fox_fwd(z, qt, vt, frep, name, comm=None):
    t = z.shape[0]
    tq, tk, nq, nk = _fox_tiles(t)
    kb = Z_FK // HEAD_DIM

    def body(qt_ref, k_ref, vt_ref, frep_ref, ot_ref, lse_ref):
        i = pl.program_id(1)
        qt = qt_ref[...]
        last = ((i + 1) * tq - 1) // tk

        def block(j, carry, masked):
            m, l, acc = carry
            rows = pl.ds(pl.multiple_of(j * tk, tk), tk)
            st = _fox_scores(k_ref[rows, :], qt, frep_ref[rows, :], i, j, masked)
            m_new = jnp.maximum(m, jnp.max(st, axis=0, keepdims=True))
            alpha = jnp.exp(m - m_new)
            p = jnp.exp(st - m_new)
            l = alpha * l + jnp.sum(p, axis=0, keepdims=True)
            acc = alpha * acc + jnp.dot(vt_ref[j], p.astype(BF16), preferred_element_type=F32)
            return m_new, l, acc

        init = (jnp.full((1, tq), NEG, F32), jnp.zeros((1, tq), F32), jnp.zeros((HEAD_DIM, tq), F32))
        m, l, acc = block(last, lax.fori_loop(0, last, lambda j, c: block(j, c, False), init), True)
        ot_ref[...] = (acc / l).astype(ot_ref.dtype)
        lse_ref[...] = m + jnp.log(l)

    stat = pl.BlockSpec((None, None, 1, tq), lambda h, i: (h, i, 0, 0))
    (ot, lse), travelled = _hosted(
        body, comm, name=name, grid=(FOX_HEADS, nq),
        in_specs=[pl.BlockSpec((None, HEAD_DIM, tq), lambda h, i: (i, h, 0)),
                  pl.BlockSpec((t, HEAD_DIM), lambda h, i: (0, kb + h)),
                  pl.BlockSpec((nk, HEAD_DIM, tk), lambda h, i: (0, h, 0)),
                  pl.BlockSpec((None, t, HEAD_DIM), lambda h, i: (h, 0, 0))],
        out_specs=[pl.BlockSpec((None, HEAD_DIM, tq), lambda h, i: (i, h, 0)), stat],
        out_shape=[jax.ShapeDtypeStruct((nq, FOX_W, tq), BF16), jax.ShapeDtypeStruct((FOX_HEADS, nq, 1, tq), F32)],
        scratch_shapes=[], semantics=("parallel", "parallel"), args=(qt, z, vt, frep))
    return ot, lse, travelled


def _fox_bwd_q(z, qt, kt, ot, dot, lse, frep, name):
    t = z.shape[0]
    tq, tk, nq, nk = _fox_tiles(t)
    kb, vb = Z_FK // HEAD_DIM, Z_FV // HEAD_DIM

    def body(qt_ref, k_ref, kt_ref, v_ref, ot_ref, dot_ref, lse_ref, frep_ref, dqt_ref, delta_ref, dfq_ref):
        i = pl.program_id(1)
        qt, dot = qt_ref[...], dot_ref[...]
        lse = lse_ref[...]
        delta = jnp.sum(_f(dot) * _f(ot_ref[...]), axis=0, keepdims=True)
        delta_ref[...] = delta
        last = ((i + 1) * tq - 1) // tk

        def block(j, carry, masked):
            dq, dfq = carry
            rows = pl.ds(pl.multiple_of(j * tk, tk), tk)
            p = jnp.exp(_fox_scores(k_ref[rows, :], qt, frep_ref[rows, :], i, j, masked) - lse)
            dp = jnp.dot(v_ref[rows, :], dot, preferred_element_type=F32)
            ds = p * (dp - delta)
            return dq + jnp.dot(kt_ref[j], ds.astype(BF16), preferred_element_type=F32), dfq + jnp.sum(ds, axis=0, keepdims=True)

        init = (jnp.zeros((HEAD_DIM, tq), F32), jnp.zeros((1, tq), F32))
        dq, dfq = block(last, lax.fori_loop(0, last, lambda j, c: block(j, c, False), init), True)
        dqt_ref[...] = (dq * FOX_SCALE).astype(dqt_ref.dtype)
        dfq_ref[...] = dfq

    mine = pl.BlockSpec((None, HEAD_DIM, tq), lambda h, i: (i, h, 0))
    stat = pl.BlockSpec((None, None, 1, tq), lambda h, i: (h, i, 0, 0))
    return pl.pallas_call(
        body, name=name, grid=(FOX_HEADS, nq),
        in_specs=[mine,
                  pl.BlockSpec((t, HEAD_DIM), lambda h, i: (0, kb + h)),
                  pl.BlockSpec((nk, HEAD_DIM, tk), lambda h, i: (0, h, 0)),
                  pl.BlockSpec((t, HEAD_DIM), lambda h, i: (0, vb + h)),
                  mine, mine, stat,
                  pl.BlockSpec((None, t, HEAD_DIM), lambda h, i: (h, 0, 0))],
        out_specs=[mine, stat, stat],
        out_shape=[jax.ShapeDtypeStruct((nq, FOX_W, tq), BF16), jax.ShapeDtypeStruct((FOX_HEADS, nq, 1, tq), F32),
                   jax.ShapeDtypeStruct((FOX_HEADS, nq, 1, tq), F32)],
        compiler_params=pltpu.CompilerParams(dimension_semantics=("parallel", "parallel"), vmem_limit_bytes=VMEM_LIMIT),
    )(qt, z, kt, z, ot, dot, lse, frep)


def _fox_bwd_kv(z, qt, do, dot, lse, delta, frep, name):
    t = z.shape[0]
    tq, tk, nq, nk = _fox_tiles(t)
    qb, kb, vb = Z_FQ // HEAD_DIM, Z_FK // HEAD_DIM, Z_FV // HEAD_DIM
    per = tk // tq

    def body(k_ref, v_ref, frep_ref, q_ref, qt_ref, do_ref, dot_ref, lse_ref, delta_ref, dk_ref, dv_ref, dfk_ref):
        j = pl.program_id(1)
        k, v, frep = k_ref[...], v_ref[...], frep_ref[...]

        def block(i, carry, masked):
            dk, dv, dfk = carry
            rows = pl.ds(pl.multiple_of(i * tq, tq), tq)
            p = jnp.exp(_fox_scores(k, qt_ref[i], frep, i, j, masked) - lse_ref[i])
            dv = dv + jnp.dot(p.astype(BF16), do_ref[rows, :], preferred_element_type=F32)
            dp = jnp.dot(v, dot_ref[i], preferred_element_type=F32)
            ds = p * (dp - delta_ref[i])
            dk = dk + jnp.dot(ds.astype(BF16), q_ref[rows, :], preferred_element_type=F32)
            for part in range(tq // HEAD_DIM):
                dfk = dfk + ds[:, part * HEAD_DIM:(part + 1) * HEAD_DIM]
            return dk, dv, dfk

        zero = jnp.zeros((tk, HEAD_DIM), F32)
        carry = (zero, zero, zero)
        for step in range(per):
            carry = block(j * per + step, carry, True)
        dk, dv, dfk = lax.fori_loop((j + 1) * per, nq, lambda i, c: block(i, c, False), carry)
        dk_ref[...] = (dk * FOX_SCALE).astype(dk_ref.dtype)
        dv_ref[...] = dv.astype(dv_ref.dtype)
        dfk_ref[...] = jnp.sum(dfk, axis=1, keepdims=True)

    whole_t = pl.BlockSpec((nq, HEAD_DIM, tq), lambda h, j: (0, h, 0))
    whole_stat = pl.BlockSpec((None, nq, 1, tq), lambda h, j: (h, 0, 0, 0))
    return pl.pallas_call(
        body, name=name, grid=(FOX_HEADS, nk),
        in_specs=[pl.BlockSpec((tk, HEAD_DIM), lambda h, j: (j, kb + h)),
                  pl.BlockSpec((tk, HEAD_DIM), lambda h, j: (j, vb + h)),
                  pl.BlockSpec((None, tk, HEAD_DIM), lambda h, j: (h, j, 0)),
                  pl.BlockSpec((t, HEAD_DIM), lambda h, j: (0, qb + h)),
                  whole_t,
                  pl.BlockSpec((t, HEAD_DIM), lambda h, j: (0, h)),
                  whole_t, whole_stat, whole_stat],
        out_specs=[pl.BlockSpec((tk, HEAD_DIM), lambda h, j: (j, h)), pl.BlockSpec((tk, HEAD_DIM), lambda h, j: (j, h)),
                   pl.BlockSpec((None, tk, 1), lambda h, j: (h, j, 0))],
        out_shape=[jax.ShapeDtypeStruct((t, FOX_W), BF16), jax.ShapeDtypeStruct((t, FOX_W), BF16),
                   jax.ShapeDtypeStruct((FOX_HEADS, t, 1), F32)],
        compiler_params=pltpu.CompilerParams(dimension_semantics=("parallel", "parallel"), vmem_limit_bytes=VMEM_LIMIT),
    )(z, z, frep, z, qt, do, dot, lse, delta)


def _gla_step(st, q, k, v, la):
    row = lax.broadcasted_iota(jnp.int32, (CHUNK, CHUNK), 0)
    col = lax.broadcasted_iota(jnp.int32, (CHUNK, CHUNK), 1)
    tri = (col <= row).astype(F32)
    a_cum = jnp.dot(tri, la, preferred_element_type=F32, precision=lax.Precision.HIGHEST)
    a_tot = jnp.sum(la, axis=0, keepdims=True)
    k_dec = (_f(k) * jnp.exp(a_tot - a_cum)).astype(BF16)
    qs = (_f(q) * (HEAD_DIM ** -0.5)).astype(BF16)
    st = st * jnp.exp(a_tot) + lax.dot_general(v.astype(BF16), k_dec, (((0,), (0,)), ((), ())), preferred_element_type=F32)
    o = lax.dot_general(qs, st.astype(BF16), (((1,), (1,)), ((), ())), preferred_element_type=F32)
    return st, o


def _gla_blocks(t):
    r = min(256, t)
    return r, t // r, r // CHUNK


def _gla_fwd(z, la, name):
    t = z.shape[0]
    r, nb, nch = _gla_blocks(t)

    def body(q_ref, k_ref, v_ref, la_ref, o_ref, sp_ref, st_ref):
        @pl.when(pl.program_id(0) == 0)
        def _():
            st_ref[...] = jnp.zeros(st_ref.shape, F32)
        for c in range(nch):
            rows = slice(c * CHUNK, (c + 1) * CHUNK)
            for h in range(GLA_HEADS):
                kc = slice(h * HEAD_DIM, (h + 1) * HEAD_DIM)
                vc = slice(h * GLA_VAL_DIM, (h + 1) * GLA_VAL_DIM)
                st = st_ref[h]
                sp_ref[c, h] = st
                st, o = _gla_step(st, q_ref[rows, kc], k_ref[rows, kc], v_ref[rows, vc], la_ref[rows, kc])
                st_ref[h] = st
                o_ref[rows, vc] = o

    return pl.pallas_call(
        body, name=name, grid=(nb,),
        in_specs=[pl.BlockSpec((r, GLA_KW), lambda i: (i, Z_GQ // GLA_KW)), pl.BlockSpec((r, GLA_KW), lambda i: (i, Z_GK // GLA_KW)),
                  pl.BlockSpec((r, GLA_VW), lambda i: (i, Z_GV // GLA_VW)), pl.BlockSpec((r, GLA_KW), lambda i: (i, 0))],
        out_specs=[pl.BlockSpec((r, GLA_VW), lambda i: (i, 0)),
                   pl.BlockSpec((nch, GLA_HEADS, GLA_VAL_DIM, HEAD_DIM), lambda i: (i, 0, 0, 0))],
        out_shape=[jax.ShapeDtypeStruct((t, GLA_VW), F32),
                   jax.ShapeDtypeStruct((t // CHUNK, GLA_HEADS, GLA_VAL_DIM, HEAD_DIM), F32)],
        scratch_shapes=[pltpu.VMEM((GLA_HEADS, GLA_VAL_DIM, HEAD_DIM), F32)],
        compiler_params=pltpu.CompilerParams(dimension_semantics=("arbitrary",), vmem_limit_bytes=VMEM_LIMIT),
    )(z, z, z, la)


def _gla_bwd(z, la, sprev, do, name):
    t = z.shape[0]
    r, nb, nch = _gla_blocks(t)

    def body(q_ref, k_ref, v_ref, la_ref, sp_ref, do_ref, dq_ref, dk_ref, dv_ref, dla_ref, dst_ref):
        @pl.when(pl.program_id(0) == 0)
        def _():
            dst_ref[...] = jnp.zeros(dst_ref.shape, F32)
        for c in reversed(range(nch)):
            rows = slice(c * CHUNK, (c + 1) * CHUNK)
            for h in range(GLA_HEADS):
                kc = slice(h * HEAD_DIM, (h + 1) * HEAD_DIM)
                vc = slice(h * GLA_VAL_DIM, (h + 1) * GLA_VAL_DIM)
                _, vjp = jax.vjp(_gla_step, sp_ref[c, h], q_ref[rows, kc], k_ref[rows, kc], v_ref[rows, vc], la_ref[rows, kc])
                dst, dq, dk, dv, dla = vjp((dst_ref[h], do_ref[rows, vc]))
                dst_ref[h] = dst
                dq_ref[rows, kc] = dq
                dk_ref[rows, kc] = dk
                dv_ref[rows, vc] = dv
                dla_ref[rows, kc] = dla

    rev = lambda i: (nb - 1 - i, 0)
    return pl.pallas_call(
        body, name=name, grid=(nb,),
        in_specs=[pl.BlockSpec((r, GLA_KW), lambda i: (nb - 1 - i, Z_GQ // GLA_KW)), pl.BlockSpec((r, GLA_KW), lambda i: (nb - 1 - i, Z_GK // GLA_KW)),
                  pl.BlockSpec((r, GLA_VW), lambda i: (nb - 1 - i, Z_GV // GLA_VW)), pl.BlockSpec((r, GLA_KW), rev),
                  pl.BlockSpec((nch, GLA_HEADS, GLA_VAL_DIM, HEAD_DIM), lambda i: (nb - 1 - i, 0, 0, 0)),
                  pl.BlockSpec((r, GLA_VW), rev)],
        out_specs=[pl.BlockSpec((r, GLA_KW), rev), pl.BlockSpec((r, GLA_KW), rev), pl.BlockSpec((r, GLA_VW), rev), pl.BlockSpec((r, GLA_KW), rev)],
        out_shape=[jax.ShapeDtypeStruct((t, GLA_KW), BF16), jax.ShapeDtypeStruct((t, GLA_KW), BF16),
                   jax.ShapeDtypeStruct((t, GLA_VW), BF16), jax.ShapeDtypeStruct((t, GLA_KW), F32)],
        scratch_shapes=[pltpu.VMEM((GLA_HEADS, GLA_VAL_DIM, HEAD_DIM), F32)],
        compiler_params=pltpu.CompilerParams(dimension_semantics=("arbitrary",), vmem_limit_bytes=VMEM_LIMIT),
    )(z, z, z, la, sprev, do)


def _local_step(x, p, tgt, shards, sp, cidx, chip):
    t = x.shape[0]
    tq, tk, _, _ = _fox_tiles(t)
    full, reduced = {}, {}

    def plan(names):
        keys, shapes, places = [], [], []
        for n in names:
            r, cc = shards[n].shape
            key, part, parts = FUSED.get(n, (n, 0, 1))
            if key not in keys:
                keys.append(key)
                stacked = n in ROW_SHARDED or n == "w_in"
                shapes.append(jax.ShapeDtypeStruct((4 * r, cc) if stacked else (r, 4 * cc * parts), shards[n].dtype))
            places.append((keys.index(key), r, 0, 0) if n in ROW_SHARDED or n == "w_in" else (keys.index(key), 0, part * 4 * cc, cc))
        return keys, shapes, places

    def gather(names):
        _, shapes, places = plan(names)
        return _ag_comm([shards[n] for n in names], shapes, places)

    def landed(names, got):
        keys, _, _ = plan(names)
        for key, g in zip(keys, got):
            full[key] = g

    def pair_sums(grads):
        parts, entries = [], []
        for g, names in grads:
            cols = g.shape[1] // (4 * len(names))
            if names[0] in ROW_SHARDED or names[0] == "w_in":
                parts.append(g if g.ndim == 3 else _shard_parts(g, names[0]))
                entries.append((names[0], len(parts) - 1, "slot", 0, parts[-1].shape[2]))
            else:
                parts.append(g[None])
                entries += [(n, len(parts) - 1, "cols", k * 4 * cols, cols) for k, n in enumerate(names)]
        swapped = _pair_swap(parts, "pair_swap")
        return entries, [_sum_half(a, b, cidx, "sum_half") for a, b in zip(parts, swapped)]

    def exchange(entries, sums):
        return _exchange_comm(sums, [e[1:] for e in entries])

    def exchanged(entries, sums, got):
        for (n, si, mode, first, _), g in zip(entries, got):
            reduced[n] = (g, sums[si], mode, first)

    first = ["ffn1_w_gate", "ffn1_w_up"]
    landed(first, _run_comm(gather(first), "all_gather"))
    w_gu1 = full["gu1"]
    n1 = _norm_fwd(x, sp["ffn1_norm"], "norm1_fwd")
    names = ["ffn1_w_down", "w_merge_gate", "gla_gate_up"]
    gu1, got = _mm(n1, w_gu1, name="mm_gu_gather", comm=gather(names))
    landed(names, got)
    a1 = _act_fwd(gu1, "act_fwd")
    names = ["w_in"]
    f1, got = _mm(a1, full["ffn1_w_down"], out_dtype=F32, name="mm_down_gather", comm=gather(names))
    landed(names, got)
    w_big, w_sm = _in_layout(full["w_in"], full["w_merge_gate"])
    gup = jnp.zeros((SMALL_W, GLA_KW), F32).at[FOX_HEADS:FOX_HEADS + GLA_RANK].set(full["gla_gate_up"])
    h1, u = _resnorm_fwd(x, f1, sp["mix_norm"], 0.5, "resnorm_fwd_half")
    names = ["w_branch_fox", "w_branch_gla", "w_out", "w_ple_proj", "w_ple_gate"]
    z, got = _mm(u, w_big, name="mm_in_gather", comm=gather(names))
    landed(names, got)
    s = _mm(u, w_sm, out_dtype=F32, name="mm_in_small")
    lf, la = _small_fwd(s, sp["fb"], gup, sp["gb"], "small_fwd")
    fp = _cumsum_tokens(lf, False, "cumsum_fwd")
    frep = jnp.broadcast_to(fp[:, :FOX_HEADS].T[:, :, None], (FOX_HEADS, t, HEAD_DIM))
    qt = _blocked_t(z[:, Z_FQ:Z_FQ + FOX_W], tq)
    kt = _blocked_t(z[:, Z_FK:Z_FK + FOX_W], tk)
    vt = _blocked_t(z[:, Z_FV:Z_FV + FOX_W], tk)
    names = ["ffn2_w_gate", "ffn2_w_up"]
    ot, lse, got = _fox_fwd(z, qt, vt, frep, "fox_fwd_gather", comm=gather(names))
    landed(names, got)
    w_gu2 = full["gu2"]
    y_fox = _unblocked_t(ot)
    o_gla, sprev = _gla_fwd(z, la, "gla_fwd")
    y_gla = _gla_out_fwd(o_gla, z, sp["ghn"], "gla_out_fwd")
    bf = _mm(y_fox, full["w_branch_fox"], name="mm_branch")
    bg = _mm(y_gla, full["w_branch_gla"], name="mm_branch")
    merged = _merge_fwd(z, bf, bg, sp["bm"], "merge_fwd")
    mo = _mm(merged, full["w_out"], out_dtype=F32, name="mm_out")
    h2, n2 = _resnorm_fwd(h1, mo, sp["ffn2_norm"], 1.0, "resnorm_fwd_one")
    names = ["ffn2_w_down"]
    gu2, got = _mm(n2, w_gu2, name="mm_gu_gather_down", comm=gather(names))
    landed(names, got)
    a2 = _act_fwd(gu2, "act_fwd")
    f2 = _mm(a2, full["ffn2_w_down"], out_dtype=F32, name="mm_down")
    h3, n4 = _resnorm_fwd(h2, f2, sp["ple_norm"], 0.5, "resnorm_fwd_half")
    pgl = _mm(n4, full["w_ple_gate"], name="mm_pg")
    pb = p.astype(BF16)
    pp = _mm(pb, full["w_ple_proj"], name="mm_pp")

    dh3, dpgl, dpp, loss, d_final = _head(h3, pgl, pp, tgt, sp["final_norm"], "head")
    ds_ = {"final_norm": d_final}
    entries, sums = pair_sums([(_mm(n4, dpgl, ta=True, name="mm_dw_sq"), ["w_ple_gate"]), (_mm(pb, dpp, ta=True, name="mm_dw_pp"), ["w_ple_proj"])])
    dn4, got = _mm(dpgl, full["w_ple_gate"], tb=True, out_dtype=F32, name="mm_dx_sq_f32_exchange", comm=exchange(entries, sums))
    exchanged(entries, sums, got)
    dh3, df2, ds_["ple_norm"] = _norm_bwd(h3, [dn4], dh3, sp["ple_norm"], 0.5, "norm_bwd_1")

    def ffn_bwd(n, gu, a, df, wgu, wd, which):
        entries, sums = pair_sums([(_mm(a, df, ta=True, name="mm_dw_down"), [which + "_w_down"])])
        da, got = _mm(df, wd, tb=True, name="mm_dx_down_exchange", comm=exchange(entries, sums))
        exchanged(entries, sums, got)
        dgu = _act_bwd(gu, da, "act_bwd")
        entries, sums = pair_sums([(_mm(n, dgu, ta=True, name="mm_dw_gu"), [which + "_w_gate", which + "_w_up"])])
        dn, got = _mm(dgu, wgu, tb=True, out_dtype=F32, name="mm_dx_gu_exchange", comm=exchange(entries, sums))
        exchanged(entries, sums, got)
        return dn

    dn2 = ffn_bwd(n2, gu2, a2, df2, w_gu2, full["ffn2_w_down"], "ffn2")
    dh2, dmix, ds_["ffn2_norm"] = _norm_bwd(h2, [dn2], dh3, sp["ffn2_norm"], 1.0, "norm_bwd_1")

    dw_out = _mm(merged, dmix, ta=True, name="mm_dw_sq")
    dmerged = _mm(dmix, full["w_out"], tb=True, name="mm_dx_sq")
    dgl, dbf, dbg, ds_["bm"] = _merge_bwd(z, bf, bg, sp["bm"], dmerged, "merge_bwd")
    mix_entries, mix_sums = pair_sums([(dw_out, ["w_out"]), (_mm(y_fox, dbf, ta=True, name="mm_dw_branch"), ["w_branch_fox"]),
                                       (_mm(y_gla, dbg, ta=True, name="mm_dw_branch"), ["w_branch_gla"])])
    dy_fox = _mm(dbf, full["w_branch_fox"], tb=True, name="mm_dx_branch")
    dy_gla = _mm(dbg, full["w_branch_gla"], tb=True, name="mm_dx_branch")

    do_gla, dgr, ds_["ghn"] = _gla_out_bwd(o_gla, z, sp["ghn"], dy_gla, "gla_out_bwd")
    dgq, dgk, dgv, dla = _gla_bwd(z, la, sprev, do_gla, "gla_bwd")
    dot = _blocked_t(dy_fox, tq)
    dqt, delta, df_query = _fox_bwd_q(z, qt, kt, ot, dot, lse, frep, "fox_bwd_q")
    dfq = _unblocked_t(dqt)
    dfk, dfv, df_key = _fox_bwd_kv(z, qt, dy_fox, dot, lse, delta, frep, "fox_bwd_kv")
    df = df_query.reshape(FOX_HEADS, t) - df_key.reshape(FOX_HEADS, t)
    dfp = jnp.pad(df.T, ((0, 0), (0, SMALL_W - FOX_HEADS)))
    dlf = _cumsum_tokens(dfp, True, "cumsum_bwd")
    dsm, ds_["fb"], dgup, ds_["gb"] = _small_bwd(s, sp["fb"], gup, sp["gb"], dlf, dla, "small_bwd")
    dz = jnp.concatenate([dfq, dfk, dfv, dgq, dgk, dgv, dgr, dgl], axis=1)
    dw_big, got = _mm(u, dz, ta=True, name="mm_dw_in_exchange", comm=exchange(mix_entries, mix_sums))
    exchanged(mix_entries, mix_sums, got)
    dw_sm = _mm(u, dsm, ta=True, out_dtype=F32, name="mm_dw_in_small").astype(BF16)
    entries, sums = pair_sums([(_w_in_parts(dw_big, dw_sm), ["w_in"]), (dw_big[:, Z_GL:], ["w_merge_gate"]), (dgup[FOX_HEADS:FOX_HEADS + GLA_RANK], ["gla_gate_up"])])
    du1, got = _mm(dz, w_big, tb=True, out_dtype=F32, name="mm_dx_in_exchange", comm=exchange(entries, sums))
    exchanged(entries, sums, got)
    du2 = _mm(dsm, w_sm, tb=True, out_dtype=F32, name="mm_dx_in_small")
    dh1, df1, ds_["mix_norm"] = _norm_bwd(h1, [du1, du2], dh2, sp["mix_norm"], 0.5, "norm_bwd_2")

    dn1 = ffn_bwd(n1, gu1, a1, df1, w_gu1, full["ffn1_w_down"], "ffn1")
    grad_x, _, ds_["ffn1_norm"] = _norm_bwd(x, [dn1], dh1, sp["ffn1_norm"], 1.0, "norm_bwd_1")
    return loss, grad_x, reduced, ds_


def _half_rows(ref, which):
    r2 = ref.shape[0] // 2
    return ref.at[pl.ds(pl.multiple_of(which * r2, r2), r2)]


class _Comm:
    def __init__(self, ins, out_shape, sems, start, finish):
        self.ins, self.out_shape, self.sems, self.start, self.finish = ins, out_shape, sems, start, finish


def _run_comm(comm, name):
    n_in, n_out = len(comm.ins), len(comm.out_shape)

    def body(*refs):
        parts = refs[:n_in], refs[n_in:n_in + n_out], refs[n_in + n_out:]
        comm.start(*parts)
        comm.finish(*parts)

    return pl.pallas_call(
        body, name=name, in_specs=[ANY] * n_in, out_specs=[ANY] * n_out, out_shape=comm.out_shape,
        scratch_shapes=comm.sems, compiler_params=pltpu.CompilerParams(has_side_effects=True),
    )(*comm.ins)


def _hosted(body, comm, *, name, grid, in_specs, out_specs, out_shape, scratch_shapes, semantics, args):
    if comm is None:
        res = pl.pallas_call(
            body, name=name, grid=grid, in_specs=in_specs, out_specs=out_specs, out_shape=out_shape, scratch_shapes=scratch_shapes,
            compiler_params=pltpu.CompilerParams(dimension_semantics=semantics, vmem_limit_bytes=VMEM_LIMIT),
        )(*args)
        return res, None
    ni, no, ns = len(in_specs), len(out_shape), len(scratch_shapes)
    ci, co = len(comm.ins), len(comm.out_shape)

    def wrapped(*refs):
        h_in, c_in = refs[:ni], refs[ni:ni + ci]
        h_out, c_out = refs[ni + ci:ni + ci + no], refs[ni + ci + no:ni + ci + no + co]
        h_scr, c_sem = refs[ni + ci + no + co:ni + ci + no + co + ns], refs[ni + ci + no + co + ns:]
        ids = [pl.program_id(axis) for axis in range(len(grid))]
        first = functools.reduce(jnp.logical_and, [i == 0 for i in ids])
        last = functools.reduce(jnp.logical_and, [i == g - 1 for i, g in zip(ids, grid)])

        @pl.when(first)
        def _():
            comm.start(c_in, c_out, c_sem)

        body(*h_in, *h_out, *h_scr)

        @pl.when(last)
        def _():
            comm.finish(c_in, c_out, c_sem)

    res = pl.pallas_call(
        wrapped, name=name, grid=grid, in_specs=list(in_specs) + [ANY] * ci, out_specs=list(out_specs) + [ANY] * co,
        out_shape=list(out_shape) + list(comm.out_shape), scratch_shapes=list(scratch_shapes) + list(comm.sems),
        compiler_params=pltpu.CompilerParams(dimension_semantics=("arbitrary",) * len(grid), vmem_limit_bytes=VMEM_LIMIT, has_side_effects=True),
    )(*args, *comm.ins)
    return res[:no], res[no:]


def _ag_comm(shards, out_shape, places):
    n = len(shards)

    def copies(ins, outs, sems):
        ici_send, ici_recv, d2d_send, d2d_recv = sems
        x, y, c = lax.axis_index("x"), lax.axis_index("y"), lax.axis_index("c")
        chips = [(1 - x, y), (x, 1 - y), (1 - x, 1 - y)]
        slot = lambda chip: 2 * chip[0] + chip[1]

        def window(wi, origin, half):
            out, row_step, col_base, col_step = places[wi]
            r, cc = shards[wi].shape
            rows = pl.ds(pl.multiple_of(slot(origin) * row_step + half * (r // 2), r // 2), r // 2)
            cols = pl.ds(pl.multiple_of(col_base + slot(origin) * col_step, HEAD_DIM), cc) if col_step else pl.ds(col_base, cc)
            return outs[out].at[rows, cols]

        def over_ici(wi, j, origin):
            return pltpu.make_async_remote_copy(
                src_ref=_half_rows(ins[wi], c), dst_ref=window(wi, origin, c),
                send_sem=ici_send.at[3 * wi + j], recv_sem=ici_recv.at[3 * wi + j],
                device_id=(chips[j][0], chips[j][1], c), device_id_type=MESH)

        def over_d2d(wi, j, half):
            place = window(wi, chips[j], half)
            return pltpu.make_async_remote_copy(
                src_ref=place, dst_ref=place, send_sem=d2d_send.at[3 * wi + j], recv_sem=d2d_recv.at[3 * wi + j],
                device_id=(x, y, 1 - c), device_id_type=MESH)

        return over_ici, over_d2d, (x, y), chips, c

    def chunk_rows(wi):
        r, cc = shards[wi].shape
        item = shards[wi].dtype.itemsize
        return _pick(r, max(32 // item, BOUNCE_BYTES // (cc * item)), 32 // item)

    def start(ins, outs, scratch):
        over_ici, _, me, _, _ = copies(ins, outs, scratch[:4])
        for wi in range(n):
            for j in range(3):
                over_ici(wi, j, me).start()
        loc_sems = scratch[4]
        for wi in range(n):
            out, row_step, col_base, col_step = places[wi]
            r, cc = shards[wi].shape
            rc = chunk_rows(wi)
            buf = scratch[5 + wi]
            slot = 2 * me[0] + me[1]
            cols = pl.ds(pl.multiple_of(col_base + slot * col_step, HEAD_DIM), cc) if col_step else pl.ds(col_base, cc)

            def load(k):
                return pltpu.make_async_copy(ins[wi].at[pl.ds(k * rc, rc)], buf.at[k % 2], loc_sems.at[2 * wi])

            def store(k):
                rows = pl.ds(pl.multiple_of(slot * row_step + k * rc, rc), rc)
                return pltpu.make_async_copy(buf.at[k % 2], outs[out].at[rows, cols], loc_sems.at[2 * wi + 1])

            load(0).start()
            for k in range(r // rc):
                load(k).wait()
                if k + 1 < r // rc:
                    load(k + 1).start()
                store(k).start()
                store(k).wait()

    def finish(ins, outs, scratch):
        over_ici, over_d2d, me, chips, c = copies(ins, outs, scratch[:4])
        for wi in range(n):
            for j in range(3):
                over_ici(wi, j, chips[j]).wait_recv()
                over_d2d(wi, j, c).start()
        for wi in range(n):
            for j in range(3):
                over_d2d(wi, j, 1 - c).wait_recv()
        for wi in range(n):
            for j in range(3):
                over_ici(wi, j, me).wait_send()
                over_d2d(wi, j, c).wait_send()

    bounce = [pltpu.VMEM((min(2, s.shape[0] // chunk_rows(wi)), chunk_rows(wi), s.shape[1]), s.dtype) for wi, s in enumerate(shards)]
    return _Comm(list(shards), list(out_shape), [pltpu.SemaphoreType.DMA((3 * n,))] * 4 + [pltpu.SemaphoreType.DMA((2 * n,))] + bounce,
                 start, finish)


def _pair_swap(parts, name):
    n = len(parts)

    def body(*refs):
        ins, outs = refs[:n], refs[n:2 * n]
        send_sems, recv_sems = refs[2 * n:]
        x, y, c = lax.axis_index("x"), lax.axis_index("y"), lax.axis_index("c")

        def swap(wi):
            r2 = parts[wi].shape[1] // 2
            return pltpu.make_async_remote_copy(
                src_ref=ins[wi].at[:, pl.ds(pl.multiple_of((1 - c) * r2, r2), r2)], dst_ref=outs[wi],
                send_sem=send_sems.at[wi], recv_sem=recv_sems.at[wi], device_id=(x, y, 1 - c), device_id_type=MESH)

        copies = [swap(wi) for wi in range(n)]
        for cp in copies:
            cp.start()
        for cp in copies:
            cp.wait()

    return pl.pallas_call(
        body, name=name, in_specs=[ANY] * n, out_specs=[ANY] * n,
        out_shape=[jax.ShapeDtypeStruct((s.shape[0], s.shape[1] // 2, s.shape[2]), s.dtype) for s in parts],
        scratch_shapes=[pltpu.SemaphoreType.DMA((n,))] * 2, compiler_params=pltpu.CompilerParams(has_side_effects=True),
    )(*parts)


def _row_tile(r, c, budget=1 << 19):
    return r if r <= 8 else _pick(r, max(8, budget // c), 8)


def _sum_half(parts, other, cidx, name):
    nl, r, cc = parts.shape
    r2 = r // 2
    tr = _row_tile(r2, cc)

    def body(c_ref, p_ref, q_ref, o_ref):
        o_ref[...] = (_f(p_ref[...]) + _f(q_ref[...])).astype(o_ref.dtype)

    return pl.pallas_call(
        body, name=name, out_shape=jax.ShapeDtypeStruct((nl, r2, cc), parts.dtype),
        grid_spec=pltpu.PrefetchScalarGridSpec(
            num_scalar_prefetch=1, grid=(nl, r2 // tr),
            in_specs=[pl.BlockSpec((None, None, tr, cc), lambda l, i, c_ref: (l, c_ref[0], i, 0)),
                      pl.BlockSpec((None, tr, cc), lambda l, i, c_ref: (l, i, 0))],
            out_specs=pl.BlockSpec((None, tr, cc), lambda l, i, c_ref: (l, i, 0))),
        compiler_params=pltpu.CompilerParams(dimension_semantics=("parallel", "parallel"), vmem_limit_bytes=VMEM_LIMIT),
    )(cidx, parts.reshape(nl, 2, r2, cc), other)


def _exchange_comm(sums, entries):
    n = len(entries)

    def copies(ins, outs, sems):
        send_sems, recv_sems = sems
        x, y, c = lax.axis_index("x"), lax.axis_index("y"), lax.axis_index("c")
        chips = [(1 - x, y), (x, 1 - y), (1 - x, 1 - y)]
        slot = lambda chip: 2 * chip[0] + chip[1]

        def piece(wi, dest):
            si, mode, first, cols = entries[wi]
            if mode == "cols":
                return ins[si].at[0, :, pl.ds(pl.multiple_of(first + slot(dest) * cols, HEAD_DIM), cols)]
            return ins[si].at[slot(dest) if mode == "slot" else 0]

        def remote(wi, j, origin):
            return pltpu.make_async_remote_copy(
                src_ref=piece(wi, chips[j]), dst_ref=outs[wi].at[slot(origin)],
                send_sem=send_sems.at[3 * wi + j], recv_sem=recv_sems.at[3 * wi + j],
                device_id=(chips[j][0], chips[j][1], c), device_id_type=MESH)

        return remote, (x, y), chips

    def start(ins, outs, sems):
        remote, me, _ = copies(ins, outs, sems)
        for wi in range(n):
            for j in range(3):
                remote(wi, j, me).start()

    def finish(ins, outs, sems):
        remote, me, chips = copies(ins, outs, sems)
        for wi in range(n):
            for j in range(3):
                remote(wi, j, chips[j]).wait_recv()
        for wi in range(n):
            for j in range(3):
                remote(wi, j, me).wait_send()

    out_shape = [jax.ShapeDtypeStruct((4, sums[si].shape[1], cols), sums[si].dtype) for si, _, _, cols in entries]
    return _Comm(list(sums), out_shape, [pltpu.SemaphoreType.DMA((3 * n,))] * 2, start, finish)


def _sum_chips(got, own, mode, first, chip, name):
    _, r2, cc = got.shape
    tr = _row_tile(r2, cc)
    own_block = {"slot": lambda i, chip_ref: (chip_ref[0], i, 0), "same": lambda i, chip_ref: (0, i, 0),
                 "cols": lambda i, chip_ref: (0, i, first // cc + chip_ref[0])}[mode]

    def body(chip_ref, g_ref, own_ref, o_ref):
        term = lambda k: jnp.where(chip_ref[0] == k, _f(own_ref[...]), _f(g_ref[k]))
        o_ref[...] = ((term(0) + term(1)) + term(2)) + term(3)

    return pl.pallas_call(
        body, name=name, out_shape=jax.ShapeDtypeStruct((r2, cc), F32),
        grid_spec=pltpu.PrefetchScalarGridSpec(
            num_scalar_prefetch=1, grid=(r2 // tr,),
            in_specs=[pl.BlockSpec((4, tr, cc), lambda i, chip_ref: (0, i, 0)),
                      pl.BlockSpec((None, tr, cc), own_block)],
            out_specs=pl.BlockSpec((tr, cc), lambda i, chip_ref: (i, 0))),
        compiler_params=pltpu.CompilerParams(dimension_semantics=("parallel",), vmem_limit_bytes=VMEM_LIMIT),
    )(chip, got, own)


def _pair_gather(halves, name):
    n = len(halves)

    def body(*refs):
        ins, outs = refs[:n], refs[n:2 * n]
        send_sems, recv_sems = refs[2 * n:]
        x, y, c = lax.axis_index("x"), lax.axis_index("y"), lax.axis_index("c")
        copies = [pltpu.make_async_remote_copy(
            src_ref=ins[wi], dst_ref=outs[wi], send_sem=send_sems.at[wi], recv_sem=recv_sems.at[wi],
            device_id=(x, y, 1 - c), device_id_type=MESH) for wi in range(n)]
        for cp in copies:
            cp.start()
        for cp in copies:
            cp.wait()

    return pl.pallas_call(
        body, name=name, in_specs=[ANY] * n, out_specs=[ANY] * n,
        out_shape=[jax.ShapeDtypeStruct(s.shape, s.dtype) for s in halves],
        scratch_shapes=[pltpu.SemaphoreType.DMA((n,))] * 2, compiler_params=pltpu.CompilerParams(has_side_effects=True),
    )(*halves)


def _adamw_update(g, w, m, v):
    m_new = ADAM_B1 * m + (1.0 - ADAM_B1) * g
    v_new = ADAM_B2 * v + (1.0 - ADAM_B2) * jnp.square(g)
    m_hat = m_new / (1.0 - ADAM_B1 ** ADAM_STEP)
    v_hat = v_new / (1.0 - ADAM_B2 ** ADAM_STEP)
    return -ADAM_LR * (m_hat / (jnp.sqrt(v_hat) + ADAM_EPS) + ADAM_WD * w), m_new, v_new


def _adamw_whole(g, w, m, v, name):
    r, c = w.shape
    tc = 256

    def body(g_ref, w_ref, m_ref, v_ref, d_ref, nm_ref, nv_ref):
        d_ref[...], nm_ref[...], nv_ref[...] = _adamw_update(g_ref[...], w_ref[...], m_ref[...], v_ref[...])

    blk = pl.BlockSpec((r, tc), lambda i: (0, i))
    return pl.pallas_call(
        body, name=name, grid=(c // tc,), in_specs=[blk] * 4, out_specs=[blk] * 3, out_shape=[jax.ShapeDtypeStruct((r, c), F32)] * 3,
        compiler_params=pltpu.CompilerParams(dimension_semantics=("parallel",), vmem_limit_bytes=VMEM_LIMIT),
    )(g, w, m, v)


def _adamw(mine, other, cidx, w, m, v, name):
    r, c = w.shape
    tr = _row_tile(r // 2, c, 1 << 18)
    nh = (r // 2) // tr

    def body(c_ref, mine_ref, other_ref, w_ref, m_ref, v_ref, g_ref, d_ref, nm_ref, nv_ref):
        g = jnp.where(pl.program_id(0) // nh == c_ref[0], mine_ref[...], other_ref[...])
        g_ref[...] = g
        d_ref[...], nm_ref[...], nv_ref[...] = _adamw_update(g, w_ref[...], m_ref[...], v_ref[...])

    blk = pl.BlockSpec((tr, c), lambda i, c_ref: (i, 0))
    mine_spec = pl.BlockSpec((tr, c), lambda i, c_ref: (jnp.where(i // nh == c_ref[0], i % nh, 0), 0))
    other_spec = pl.BlockSpec((tr, c), lambda i, c_ref: (jnp.where(i // nh == c_ref[0], 0, i % nh), 0))
    return pl.pallas_call(
        body, name=name, out_shape=[jax.ShapeDtypeStruct((r, c), F32)] * 4,
        grid_spec=pltpu.PrefetchScalarGridSpec(
            num_scalar_prefetch=1, grid=(r // tr,), in_specs=[mine_spec, other_spec, blk, blk, blk], out_specs=[blk] * 4),
        compiler_params=pltpu.CompilerParams(dimension_semantics=("arbitrary",), vmem_limit_bytes=VMEM_LIMIT),
    )(cidx, mine, other, w, m, v)


BIG = ["ffn1_w_gate", "ffn1_w_up", "ffn1_w_down", "w_in", "gla_gate_up", "w_branch_fox", "w_branch_gla", "w_merge_gate", "w_out",
       "ffn2_w_gate", "ffn2_w_up", "ffn2_w_down", "w_ple_proj", "w_ple_gate"]
ROW_SHARDED = ("ffn1_w_down", "w_out", "ffn2_w_down", "w_ple_gate")
FUSED = {"ffn1_w_gate": ("gu1", 0, 2), "ffn1_w_up": ("gu1", 1, 2), "ffn2_w_gate": ("gu2", 0, 2), "ffn2_w_up": ("gu2", 1, 2)}
SMALL = ["ffn1_norm", "mix_norm", "fox_forget_bias", "gla_gate_bias", "gla_head_norm", "b_merge_gate", "ffn2_norm", "ple_norm", "final_norm"]
NAMES = ["ffn1_norm", "ffn1_w_gate", "ffn1_w_up", "ffn1_w_down", "mix_norm", "w_in", "fox_forget_bias", "gla_gate_up", "gla_gate_bias",
         "gla_head_norm", "w_branch_fox", "w_branch_gla", "w_merge_gate", "b_merge_gate", "w_out", "ffn2_norm", "ffn2_w_gate", "ffn2_w_up",
         "ffn2_w_down", "ple_norm", "w_ple_proj", "w_ple_gate", "final_norm"]
W_IN_COLS = (FOX_W, FOX_W, FOX_W, FOX_HEADS, GLA_KW, GLA_KW, GLA_VW, GLA_VW, GLA_RANK)
SMALL_ROWS, SMALL_COLS = 16, 1024


def _shard_parts(full, name):
    if name in ROW_SHARDED:
        return full.reshape(4, full.shape[0] // 4, full.shape[1])
    return jnp.transpose(full.reshape(full.shape[0], 4, full.shape[1] // 4), (1, 0, 2))


W_IN_WIDE = ((0, 3 * FOX_W), (3 * FOX_W + FOX_HEADS, 3 * FOX_W + FOX_HEADS + 2 * GLA_KW + 2 * GLA_VW))
W_IN_NARROW = ((3 * FOX_W, 3 * FOX_W + FOX_HEADS), (sum(W_IN_COLS) - GLA_RANK, sum(W_IN_COLS)))


def _in_layout(stacked, w_merge_gate):
    per = stacked.shape[1]

    def columns(lo, hi):
        out = []
        while lo < hi:
            j, end = lo // per, min(hi, (lo // per + 1) * per)
            out.append(stacked[j * D_MODEL:(j + 1) * D_MODEL, lo - j * per:end - j * per])
            lo = end
        return out
    big = jnp.concatenate(columns(*W_IN_WIDE[0]) + columns(*W_IN_WIDE[1]) + [w_merge_gate], axis=1)
    sm = jnp.concatenate(columns(*W_IN_NARROW[0]) + columns(*W_IN_NARROW[1]) + [jnp.zeros((D_MODEL, SMALL_W - FOX_HEADS - GLA_RANK), BF16)], axis=1)
    return big, sm


def _w_in_parts(dw_big, dw_sm):
    runs = [(W_IN_WIDE[0], dw_big, Z_FQ), (W_IN_NARROW[0], dw_sm, 0), (W_IN_WIDE[1], dw_big, Z_GQ), (W_IN_NARROW[1], dw_sm, FOX_HEADS)]
    per = sum(W_IN_COLS) // 4
    parts = []
    for j in range(4):
        lo, hi, pieces = j * per, (j + 1) * per, []
        for (a, b), src, at in runs:
            if max(a, lo) < min(b, hi):
                pieces.append(src[:, at + max(a, lo) - a:at + min(b, hi) - a])
        parts.append(jnp.concatenate(pieces, axis=1))
    return jnp.stack(parts)


def _pad_lanes(a, width):
    return jnp.pad(a, ((0, 0), (0, width - a.shape[1])))


def kernel(x, p, ffn1_norm, ffn1_w_gate, ffn1_w_up, ffn1_w_down, mix_norm, w_in, fox_forget_bias, gla_gate_up, gla_gate_bias, gla_head_norm, w_branch_fox, w_branch_gla, w_merge_gate, b_merge_gate, w_out, ffn2_norm, ffn2_w_gate, ffn2_w_up, ffn2_w_down, ple_norm, w_ple_proj, w_ple_gate, final_norm, loss_target, m_ffn1_norm, m_ffn1_w_gate, m_ffn1_w_up, m_ffn1_w_down, m_mix_norm, m_w_in, m_fox_forget_bias, m_gla_gate_up, m_gla_gate_bias, m_gla_head_norm, m_w_branch_fox, m_w_branch_gla, m_w_merge_gate, m_b_merge_gate, m_w_out, m_ffn2_norm, m_ffn2_w_gate, m_ffn2_w_up, m_ffn2_w_down, m_ple_norm, m_w_ple_proj, m_w_ple_gate, m_final_norm, v_ffn1_norm, v_ffn1_w_gate, v_ffn1_w_up, v_ffn1_w_down, v_mix_norm, v_w_in, v_fox_forget_bias, v_gla_gate_up, v_gla_gate_bias, v_gla_head_norm, v_w_branch_fox, v_w_branch_gla, v_w_merge_gate, v_b_merge_gate, v_w_out, v_ffn2_norm, v_ffn2_w_gate, v_ffn2_w_up, v_ffn2_w_down, v_ple_norm, v_w_ple_proj, v_w_ple_gate, v_final_norm):
    args = dict(locals())
    wts = {n: args[n] for n in NAMES}
    mom = {n: args["m_" + n] for n in NAMES}
    var = {n: args["v_" + n] for n in NAMES}
    two_d = lambda a: a.reshape(-1, a.shape[-1])

    wire = lambda n: F32 if n == "gla_gate_up" else BF16
    cidx = lax.axis_index("c").astype(jnp.int32).reshape(1)
    chip = (2 * lax.axis_index("x") + lax.axis_index("y")).astype(jnp.int32)
    shards = {n: two_d(wts[n]).astype(wire(n)) for n in BIG}
    sp = {
        "ffn1_norm": two_d(ffn1_norm), "mix_norm": two_d(mix_norm), "fb": _pad_lanes(two_d(fox_forget_bias), SMALL_W),
        "gb": two_d(gla_gate_bias), "ghn": two_d(gla_head_norm), "bm": two_d(b_merge_gate), "ffn2_norm": two_d(ffn2_norm),
        "ple_norm": two_d(ple_norm), "final_norm": two_d(final_norm),
    }

    loss, grad_x, reduced, ds_ = _local_step(x[0], p[0, 0], loss_target[0], shards, sp, cidx, chip)

    small_g = {"ffn1_norm": ds_["ffn1_norm"], "mix_norm": ds_["mix_norm"], "fox_forget_bias": ds_["fb"][:, :FOX_HEADS],
               "gla_gate_bias": ds_["gb"], "gla_head_norm": ds_["ghn"], "b_merge_gate": ds_["bm"], "ffn2_norm": ds_["ffn2_norm"],
               "ple_norm": ds_["ple_norm"], "final_norm": ds_["final_norm"]}
    small_w = sum(two_d(wts[n]).shape[1] for n in SMALL)
    assert small_w <= SMALL_ROWS * SMALL_COLS
    packed = lambda d: _pad_lanes(jnp.concatenate([two_d(d[n]) for n in SMALL], axis=1), SMALL_ROWS * SMALL_COLS).reshape(SMALL_ROWS, SMALL_COLS)
    parts = [packed(small_g)[None]]
    pair_sums = [_sum_half(a, b, cidx, "sum_half") for a, b in zip(parts, _pair_swap(parts, "pair_swap"))]
    reduced["small"] = (_run_comm(_exchange_comm(pair_sums, [(0, "same", 0, SMALL_COLS)]), "chip_exchange")[0], pair_sums[0], "same", 0)
    mine = [_sum_chips(*reduced[n], chip.reshape(1), "sum_chips") for n in BIG + ["small"]]
    other = _pair_gather(mine, "pair_gather")

    out = {}
    for n, a, b in zip(BIG, mine[:-1], other[:-1]):
        if n == "w_in":
            g_t = jnp.where(cidx[0] == 0, jnp.concatenate([a, b]), jnp.concatenate([b, a])).T
            flip = lambda t: jnp.swapaxes(t, 1, 2)[0]
            res = _adamw_whole(g_t, flip(wts[n]), flip(mom[n]), flip(var[n]), "adamw_w_in")
            out[n] = [jnp.swapaxes(r[None], 1, 2) for r in [g_t, *res]]
            continue
        out[n] = [r.reshape(wts[n].shape) for r in _adamw(a, b, cidx, two_d(wts[n]), two_d(mom[n]), two_d(var[n]), "adamw_" + n)]
    small_out = [r.reshape(1, SMALL_ROWS * SMALL_COLS) for r in _adamw(mine[-1], other[-1], cidx, packed(wts), packed(mom), packed(var), "adamw_small")]
    off = 0
    for n in SMALL:
        cw = two_d(wts[n]).shape[1]
        out[n] = [r[:, off:off + cw].reshape(wts[n].shape) for r in small_out]
        off += cw

    total = lax.psum(loss[0, 0], ("x", "y", "c"))
    return (total, grad_x[None], *[out[n][0] for n in NAMES], *[out[n][1] for n in NAMES],
            *[out[n][2] for n in NAMES], *[out[n][3] for n in NAMES])
```

```python
import functools

import jax
import jax.numpy as jnp
from jax import lax
from jax.experimental import pallas as pl
from jax.experimental.pallas import tpu as pltpu

F32 = jnp.float32
BF16 = jnp.bfloat16
MESH = pl.DeviceIdType.MESH
ANY = pl.BlockSpec(memory_space=pl.ANY)

D_MODEL = 2048
FOX_HEADS = 8
HEAD_DIM = 128
GLA_HEADS = 4
GLA_VAL_DIM = 256
GLA_RANK = 16
GLA_TAU = 16.0
CHUNK = 64
EPS = 1e-6
FOX_W = FOX_HEADS * HEAD_DIM
GLA_KW = GLA_HEADS * HEAD_DIM
GLA_VW = GLA_HEADS * GLA_VAL_DIM
Z_FQ, Z_FK, Z_FV, Z_GQ, Z_GK, Z_GV, Z_GR, Z_GL = 0, 1024, 2048, 3072, 3584, 4096, 5120, 6144
Z_W = Z_GL + 2 * D_MODEL
SMALL_W = 128
NEG = -1e30

ADAM_LR, ADAM_B1, ADAM_B2, ADAM_EPS, ADAM_WD, ADAM_STEP = 0.001, 0.9, 0.999, 1e-08, 0.01, 10

VMEM_LIMIT = 56 * 1024 * 1024
BOUNCE_BYTES = 2 * 1024 * 1024


def _pick(n, target, mult=128):
    if n <= target:
        return n
    best = None
    for d in range(mult, target + 1, mult):
        if n % d == 0:
            best = d
    assert best is not None, (n, target)
    return best


def _mm(a, b, *, ta=False, tb=False, out_dtype=BF16, name, comm=None):
    m, k = (a.shape[1], a.shape[0]) if ta else a.shape
    n = b.shape[0] if tb else b.shape[1]
    assert (b.shape[1] if tb else b.shape[0]) == k
    bk = _pick(k, 4096)
    nk = k // bk
    bm, bn = _pick(m, 1024), _pick(n, 1024)
    dims = (((0 if ta else 1,), (1 if tb else 0,)), ((), ()))

    def body(a_ref, b_ref, o_ref, acc_ref):
        part = lax.dot_general(a_ref[...], b_ref[...], dims, preferred_element_type=F32)
        if nk == 1:
            o_ref[...] = part.astype(o_ref.dtype)
            return
        kk = pl.program_id(2)

        @pl.when(kk == 0)
        def _():
            acc_ref[...] = part

        @pl.when(kk > 0)
        def _():
            acc_ref[...] += part

        @pl.when(kk == nk - 1)
        def _():
            o_ref[...] = acc_ref[...].astype(o_ref.dtype)

    a_spec = pl.BlockSpec((bk, bm), lambda i, j, kk: (kk, i)) if ta else pl.BlockSpec((bm, bk), lambda i, j, kk: (i, kk))
    b_spec = pl.BlockSpec((bn, bk), lambda i, j, kk: (j, kk)) if tb else pl.BlockSpec((bk, bn), lambda i, j, kk: (kk, j))
    (out,), travelled = _hosted(
        body, comm, name=name, grid=(m // bm, n // bn, nk),
        in_specs=[a_spec, b_spec], out_specs=[pl.BlockSpec((bm, bn), lambda i, j, kk: (i, j))],
        out_shape=[jax.ShapeDtypeStruct((m, n), out_dtype)], scratch_shapes=[pltpu.VMEM((bm, bn), F32)],
        semantics=("parallel", "parallel", "arbitrary"), args=(a, b))
    return out if comm is None else (out, travelled)


def _rowwise(fn, tiled, bcast, outs, reds=(), *, tt, name):
    t = tiled[0][0].shape[0]
    tt = min(tt, t)
    nin, nout = len(tiled) + len(bcast), len(outs)
    splits = [s[3] for s in tiled] + [s[1] for s in bcast]

    def store(ref, val, acc):
        off = 0
        for piece in val if isinstance(val, (tuple, list)) else (val,):
            w = piece.shape[-1]
            if acc:
                ref[:, off:off + w] += piece.astype(ref.dtype)
            else:
                ref[:, off:off + w] = piece.astype(ref.dtype)
            off += w
        assert off == ref.shape[-1], (name, off, ref.shape)

    def body(*refs):
        args = []
        for ref, sp in zip(refs[:nin], splits):
            if sp is None:
                args.append(ref[...])
            else:
                off = 0
                for w in sp:
                    args.append(ref[:, off:off + w])
                    off += w
        res = fn(*args)
        res = res if isinstance(res, (tuple, list)) else (res,)
        assert len(res) == nout + len(reds), (name, len(res))
        for ref, val in zip(refs[nin:nin + nout], res[:nout]):
            store(ref, val, False)
        if reds:
            @pl.when(pl.program_id(0) == 0)
            def _():
                for ref in refs[nin + nout:]:
                    ref[...] = jnp.zeros(ref.shape, ref.dtype)
            for ref, val in zip(refs[nin + nout:], res[nout:]):
                store(ref, val, True)

    in_specs = [pl.BlockSpec((tt, w), functools.partial(lambda i, cb: (i, cb), cb=cb)) for (_, w, cb, _) in tiled]
    in_specs += [pl.BlockSpec(arr.shape, lambda i: (0, 0)) for (arr, _) in bcast]
    out_specs = [pl.BlockSpec((tt, w), lambda i: (i, 0)) for (w, _) in outs]
    out_specs += [pl.BlockSpec((r, w), lambda i: (0, 0)) for (r, w) in reds]
    out_shape = [jax.ShapeDtypeStruct((t, w), dt) for (w, dt) in outs] + [jax.ShapeDtypeStruct((r, w), F32) for (r, w) in reds]
    return pl.pallas_call(
        body, name=name, grid=(t // tt,), in_specs=in_specs, out_specs=out_specs, out_shape=out_shape,
        compiler_params=pltpu.CompilerParams(dimension_semantics=("arbitrary" if reds else "parallel",), vmem_limit_bytes=VMEM_LIMIT),
    )(*[s[0] for s in tiled], *[s[0] for s in bcast])


def _full(arr):
    return (arr, arr.shape[1], 0, None)


def _f(x):
    return x.astype(F32)


def _rms(x, g):
    return x * lax.rsqrt(jnp.mean(x * x, axis=-1, keepdims=True) + EPS) * g


def _log_sigmoid(x):
    return jnp.minimum(x, 0.0) - jnp.log1p(jnp.exp(-jnp.abs(x)))


def _silu(x):
    return x * jax.nn.sigmoid(x)


def _norm_fwd(x, g, name):
    return _rowwise(lambda xb, gb: _rms(_f(xb), gb), [_full(x)], [(g, None)], [(x.shape[1], BF16)], tt=256, name=name)[0]


def _resnorm_fwd(res, branch, g, coef, name):
    def fn(rb, bb, gb):
        h = rb + coef * _f(bb)
        return h, _rms(h, gb)
    d = res.shape[1]
    return _rowwise(fn, [_full(res), _full(branch)], [(g, None)], [(d, F32), (d, BF16)], tt=256, name=name)


def _norm_bwd(h, dns, dres, g, coef, name):
    nd = len(dns)

    def fn(hb, *rest):
        dn = _f(rest[0])
        for extra in rest[1:nd]:
            dn = dn + _f(extra)
        dr, gb = rest[nd], rest[nd + 1]
        _, vjp = jax.vjp(_rms, hb, gb)
        dh, dg = vjp(dn)
        dh = dh + dr
        return dh, coef * dh, dg
    d = h.shape[1]
    return _rowwise(fn, [_full(h)] + [_full(x) for x in dns] + [_full(dres)], [(g, None)],
                    [(d, F32), (d, BF16)], [(1, d)], tt=256, name=name)


def _act_fwd(gu, name):
    ff = gu.shape[1] // 2
    return _rowwise(lambda gb, ub: _silu(_f(gb)) * _f(ub), [(gu, 2 * ff, 0, (ff, ff))], [], [(ff, BF16)], tt=256, name=name)[0]


def _act_bwd(gu, da, name):
    ff = gu.shape[1] // 2

    def fn(gb, ub, dab):
        _, vjp = jax.vjp(lambda p, q: _silu(p) * q, _f(gb), _f(ub))
        return (vjp(_f(dab)),)
    return _rowwise(fn, [(gu, 2 * ff, 0, (ff, ff)), _full(da)], [], [(2 * ff, BF16)], tt=128, name=name)[0]


def _merge(glf, glg, bf, bg, bmf, bmg):
    return jax.nn.sigmoid(_f(glf) + bmf) * _f(bf) + jax.nn.sigmoid(_f(glg) + bmg) * _f(bg)


def _merge_fwd(z, bf, bg, bm, name):
    d = D_MODEL
    return _rowwise(_merge, [(z, d, Z_GL // d, None), (z, d, Z_GL // d + 1, None), _full(bf), _full(bg)], [(bm, (d, d))],
                    [(d, BF16)], tt=256, name=name)[0]


def _merge_bwd(z, bf, bg, bm, dm, name):
    d = D_MODEL

    def fn(glf, glg, bfb, bgb, dmb, bmf, bmg):
        _, vjp = jax.vjp(_merge, _f(glf), _f(glg), _f(bfb), _f(bgb), bmf, bmg)
        dglf, dglg, dbf, dbg, dbmf, dbmg = vjp(_f(dmb))
        return (dglf, dglg), dbf, dbg, (dbmf, dbmg)
    return _rowwise(fn, [(z, d, Z_GL // d, None), (z, d, Z_GL // d + 1, None), _full(bf), _full(bg), _full(dm)], [(bm, (d, d))],
                    [(2 * d, BF16), (d, BF16), (d, BF16)], [(1, 2 * d)], tt=128, name=name)


def _gla_out(o, gr, g):
    return _rms(o, g) * _silu(_f(gr))


_PER_HEAD = (GLA_VAL_DIM,) * GLA_HEADS


def _gla_out_fwd(o, z, g, name):
    nh = GLA_HEADS

    def fn(*blocks):
        return (tuple(_gla_out(blocks[h], blocks[nh + h], blocks[2 * nh]) for h in range(nh)),)
    return _rowwise(fn, [(o, GLA_VW, 0, _PER_HEAD), (z, GLA_VW, Z_GR // GLA_VW, _PER_HEAD)], [(g, None)], [(GLA_VW, BF16)], tt=256, name=name)[0]


def _gla_out_bwd(o, z, g, dy, name):
    nh = GLA_HEADS

    def fn(*blocks):
        gb = blocks[3 * nh]
        grads = []
        for h in range(nh):
            _, vjp = jax.vjp(_gla_out, blocks[h], _f(blocks[nh + h]), gb)
            grads.append(vjp(_f(blocks[2 * nh + h])))
        dg = grads[0][2]
        for h in range(1, nh):
            dg = dg + grads[h][2]
        return tuple(gr[0] for gr in grads), tuple(gr[1] for gr in grads), dg
    return _rowwise(fn, [(o, GLA_VW, 0, _PER_HEAD), (z, GLA_VW, Z_GR // GLA_VW, _PER_HEAD), (dy, GLA_VW, 0, _PER_HEAD)], [(g, None)],
                    [(GLA_VW, F32), (GLA_VW, BF16)], [(1, GLA_VAL_DIM)], tt=256, name=name)


def _small_gates(s, fb, gup, gb):
    lane = lax.broadcasted_iota(jnp.int32, s.shape, 1)
    lf = jnp.where(lane < FOX_HEADS, _log_sigmoid(s + fb), 0.0)
    pre = jnp.dot(s.astype(BF16), gup.astype(BF16), preferred_element_type=F32) + gb
    return lf, _log_sigmoid(pre) / GLA_TAU


def _small_fwd(s, fb, gup, gb, name):
    return _rowwise(_small_gates, [_full(s)], [(fb, None), (gup, None), (gb, None)], [(SMALL_W, F32), (GLA_KW, F32)], tt=256, name=name)


def _small_bwd(s, fb, gup, gb, dlf, dla, name):
    def fn(sb, dlfb, dlab, fbb, gupb, gbb):
        _, vjp = jax.vjp(_small_gates, sb, fbb, gupb, gbb)
        return vjp((dlfb, dlab))
    return _rowwise(fn, [_full(s), _full(dlf), _full(dla)], [(fb, None), (gup, None), (gb, None)],
                    [(SMALL_W, BF16)], [(1, SMALL_W), (SMALL_W, GLA_KW), (1, GLA_KW)], tt=256, name=name)


def _head_fn(h3, pgl, pp, tgt, gf):
    h4 = h3 + jax.nn.sigmoid(pgl) * pp
    err = _rms(h4, gf) - tgt
    return 0.5 * jnp.sum(jnp.mean(err * err, axis=-1, keepdims=True))


def _head(h3, pgl, pp, tgt, gf, name):
    def fn(hb, gl, pb, tb, gfb):
        loss, vjp = jax.vjp(_head_fn, hb, _f(gl), _f(pb), tb, gfb)
        dh, dgl, dpp, _, dgf = vjp(jnp.ones((), F32))
        return dh, dgl, dpp, jnp.full((1, 128), loss, F32), dgf
    d = h3.shape[1]
    return _rowwise(fn, [_full(h3), _full(pgl), _full(pp), _full(tgt)], [(gf, None)],
                    [(d, F32), (d, BF16), (d, BF16)], [(1, 128), (1, d)], tt=256, name=name)


def _cumsum_tokens(a, reverse, name):
    t, w = a.shape
    r = min(256, t)
    nb = t // r

    def body(a_ref, o_ref, carry_ref):
        @pl.when(pl.program_id(0) == 0)
        def _():
            carry_ref[...] = jnp.zeros(carry_ref.shape, F32)
        row = lax.broadcasted_iota(jnp.int32, (r, r), 0)
        col = lax.broadcasted_iota(jnp.int32, (r, r), 1)
        tri = ((col >= row) if reverse else (col <= row)).astype(F32)
        blk = a_ref[...]
        o_ref[...] = jnp.dot(tri, blk, preferred_element_type=F32, precision=lax.Precision.HIGHEST) + carry_ref[...]
        carry_ref[...] += jnp.sum(blk, axis=0, keepdims=True)

    idx = (lambda i: (nb - 1 - i, 0)) if reverse else (lambda i: (i, 0))
    return pl.pallas_call(
        body, name=name, grid=(nb,), in_specs=[pl.BlockSpec((r, w), idx)], out_specs=pl.BlockSpec((r, w), idx),
        out_shape=jax.ShapeDtypeStruct((t, w), F32), scratch_shapes=[pltpu.VMEM((1, w), F32)],
        compiler_params=pltpu.CompilerParams(dimension_semantics=("arbitrary",)),
    )(a)


FOX_TQ, FOX_TK = 512, 512
FOX_SCALE = HEAD_DIM ** -0.5


def _fox_tiles(t):
    tq, tk = min(FOX_TQ, t), min(FOX_TK, t)
    return tq, tk, t // tq, t // tk


def _blocked_t(a, blk):
    return a.reshape(a.shape[0] // blk, blk, a.shape[1]).transpose(0, 2, 1)


def _unblocked_t(b):
    return b.transpose(0, 2, 1).reshape(b.shape[0] * b.shape[2], b.shape[1])


def _fox_scores(k, qt, frep, i, j, masked):
    tk, tq = k.shape[0], qt.shape[1]
    st = jnp.dot(k, qt, preferred_element_type=F32) * FOX_SCALE - jnp.tile(frep, (1, tq // HEAD_DIM))
    if masked:
        key = j * tk + lax.broadcasted_iota(jnp.int32, (tk, tq), 0)
        query = i * tq + lax.broadcasted_iota(jnp.int32, (tk, tq), 1)
        st = jnp.where(key <= query, st, NEG)
    return st


def _fox_fwd(z, qt, vt, frep, name, comm=None):
    t = z.shape[0]
    tq, tk, nq, nk = _fox_tiles(t)
    kb = Z_FK // HEAD_DIM

    def body(qt_ref, k_ref, vt_ref, frep_ref, ot_ref, lse_ref):
        i = pl.program_id(1)
        qt = qt_ref[...]
        last = ((i + 1) * tq - 1) // tk

        def block(j, carry, masked):
            m, l, acc = carry
            rows = pl.ds(pl.multiple_of(j * tk, tk), tk)
            st = _fox_scores(k_ref[rows, :], qt, frep_ref[rows, :], i, j, masked)
            m_new = jnp.maximum(m, jnp.max(st, axis=0, keepdims=True))
            alpha = jnp.exp(m - m_new)
            p = jnp.exp(st - m_new)
            l = alpha * l + jnp.sum(p, axis=0, keepdims=True)
            acc = alpha * acc + jnp.dot(vt_ref[j], p.astype(BF16), preferred_element_type=F32)
            return m_new, l, acc

        init = (jnp.full((1, tq), NEG, F32), jnp.zeros((1, tq), F32), jnp.zeros((HEAD_DIM, tq), F32))
        m, l, acc = block(last, lax.fori_loop(0, last, lambda j, c: block(j, c, False), init), True)
        ot_ref[...] = (acc / l).astype(ot_ref.dtype)
        lse_ref[...] = m + jnp.log(l)

    stat = pl.BlockSpec((None, None, 1, tq), lambda h, i: (h, i, 0, 0))
    (ot, lse), travelled = _hosted(
        body, comm, name=name, grid=(FOX_HEADS, nq),
        in_specs=[pl.BlockSpec((None, HEAD_DIM, tq), lambda h, i: (i, h, 0)),
                  pl.BlockSpec((t, HEAD_DIM), lambda h, i: (0, kb + h)),
                  pl.BlockSpec((nk, HEAD_DIM, tk), lambda h, i: (0, h, 0)),
                  pl.BlockSpec((None, t, HEAD_DIM), lambda h, i: (h, 0, 0))],
        out_specs=[pl.BlockSpec((None, HEAD_DIM, tq), lambda h, i: (i, h, 0)), stat],
        out_shape=[jax.ShapeDtypeStruct((nq, FOX_W, tq), BF16), jax.ShapeDtypeStruct((FOX_HEADS, nq, 1, tq), F32)],
        scratch_shapes=[], semantics=("parallel", "parallel"), args=(qt, z, vt, frep))
    return ot, lse, travelled


def _fox_bwd_q(z, qt, kt, ot, dot, lse, frep, name):
    t = z.shape[0]
    tq, tk, nq, nk = _fox_tiles(t)
    kb, vb = Z_FK // HEAD_DIM, Z_FV // HEAD_DIM

    def body(qt_ref, k_ref, kt_ref, v_ref, ot_ref, dot_ref, lse_ref, frep_ref, dqt_ref, delta_ref, dfq_ref):
        i = pl.program_id(1)
        qt, dot = qt_ref[...], dot_ref[...]
        lse = lse_ref[...]
        delta = jnp.sum(_f(dot) * _f(ot_ref[...]), axis=0, keepdims=True)
        delta_ref[...] = delta
        last = ((i + 1) * tq - 1) // tk

        def block(j, carry, masked):
            dq, dfq = carry
            rows = pl.ds(pl.multiple_of(j * tk, tk), tk)
            p = jnp.exp(_fox_scores(k_ref[rows, :], qt, frep_ref[rows, :], i, j, masked) - lse)
            dp = jnp.dot(v_ref[rows, :], dot, preferred_element_type=F32)
            ds = p * (dp - delta)
            return dq + jnp.dot(kt_ref[j], ds.astype(BF16), preferred_element_type=F32), dfq + jnp.sum(ds, axis=0, keepdims=True)

        init = (jnp.zeros((HEAD_DIM, tq), F32), jnp.zeros((1, tq), F32))
        dq, dfq = block(last, lax.fori_loop(0, last, lambda j, c: block(j, c, False), init), True)
        dqt_ref[...] = (dq * FOX_SCALE).astype(dqt_ref.dtype)
        dfq_ref[...] = dfq

    mine = pl.BlockSpec((None, HEAD_DIM, tq), lambda h, i: (i, h, 0))
    stat = pl.BlockSpec((None, None, 1, tq), lambda h, i: (h, i, 0, 0))
    return pl.pallas_call(
        body, name=name, grid=(FOX_HEADS, nq),
        in_specs=[mine,
                  pl.BlockSpec((t, HEAD_DIM), lambda h, i: (0, kb + h)),
                  pl.BlockSpec((nk, HEAD_DIM, tk), lambda h, i: (0, h, 0)),
                  pl.BlockSpec((t, HEAD_DIM), lambda h, i: (0, vb + h)),
                  mine, mine, stat,
                  pl.BlockSpec((None, t, HEAD_DIM), lambda h, i: (h, 0, 0))],
        out_specs=[mine, stat, stat],
        out_shape=[jax.ShapeDtypeStruct((nq, FOX_W, tq), BF16), jax.ShapeDtypeStruct((FOX_HEADS, nq, 1, tq), F32),
                   jax.ShapeDtypeStruct((FOX_HEADS, nq, 1, tq), F32)],
        compiler_params=pltpu.CompilerParams(dimension_semantics=("parallel", "parallel"), vmem_limit_bytes=VMEM_LIMIT),
    )(qt, z, kt, z, ot, dot, lse, frep)


def _fox_bwd_kv(z, qt, do, dot, lse, delta, frep, name):
    t = z.shape[0]
    tq, tk, nq, nk = _fox_tiles(t)
    qb, kb, vb = Z_FQ // HEAD_DIM, Z_FK // HEAD_DIM, Z_FV // HEAD_DIM
    per = tk // tq

    def body(k_ref, v_ref, frep_ref, q_ref, qt_ref, do_ref, dot_ref, lse_ref, delta_ref, dk_ref, dv_ref, dfk_ref):
        j = pl.program_id(1)
        k, v, frep = k_ref[...], v_ref[...], frep_ref[...]

        def block(i, carry, masked):
            dk, dv, dfk = carry
            rows = pl.ds(pl.multiple_of(i * tq, tq), tq)
            p = jnp.exp(_fox_scores(k, qt_ref[i], frep, i, j, masked) - lse_ref[i])
            dv = dv + jnp.dot(p.astype(BF16), do_ref[rows, :], preferred_element_type=F32)
            dp = jnp.dot(v, dot_ref[i], preferred_element_type=F32)
            ds = p * (dp - delta_ref[i])
            dk = dk + jnp.dot(ds.astype(BF16), q_ref[rows, :], preferred_element_type=F32)
            for part in range(tq // HEAD_DIM):
                dfk = dfk + ds[:, part * HEAD_DIM:(part + 1) * HEAD_DIM]
            return dk, dv, dfk

        zero = jnp.zeros((tk, HEAD_DIM), F32)
        carry = (zero, zero, zero)
        for step in range(per):
            carry = block(j * per + step, carry, True)
        dk, dv, dfk = lax.fori_loop((j + 1) * per, nq, lambda i, c: block(i, c, False), carry)
        dk_ref[...] = (dk * FOX_SCALE).astype(dk_ref.dtype)
        dv_ref[...] = dv.astype(dv_ref.dtype)
        dfk_ref[...] = jnp.sum(dfk, axis=1, keepdims=True)

    whole_t = pl.BlockSpec((nq, HEAD_DIM, tq), lambda h, j: (0, h, 0))
    whole_stat = pl.BlockSpec((None, nq, 1, tq), lambda h, j: (h, 0, 0, 0))
    return pl.pallas_call(
        body, name=name, grid=(FOX_HEADS, nk),
        in_specs=[pl.BlockSpec((tk, HEAD_DIM), lambda h, j: (j, kb + h)),
                  pl.BlockSpec((tk, HEAD_DIM), lambda h, j: (j, vb + h)),
                  pl.BlockSpec((None, tk, HEAD_DIM), lambda h, j: (h, j, 0)),
                  pl.BlockSpec((t, HEAD_DIM), lambda h, j: (0, qb + h)),
                  whole_t,
                  pl.BlockSpec((t, HEAD_DIM), lambda h, j: (0, h)),
                  whole_t, whole_stat, whole_stat],
        out_specs=[pl.BlockSpec((tk, HEAD_DIM), lambda h, j: (j, h)), pl.BlockSpec((tk, HEAD_DIM), lambda h, j: (j, h)),
                   pl.BlockSpec((None, tk, 1), lambda h, j: (h, j, 0))],
        out_shape=[jax.ShapeDtypeStruct((t, FOX_W), BF16), jax.ShapeDtypeStruct((t, FOX_W), BF16),
                   jax.ShapeDtypeStruct((FOX_HEADS, t, 1), F32)],
        compiler_params=pltpu.CompilerParams(dimension_semantics=("parallel", "parallel"), vmem_limit_bytes=VMEM_LIMIT),
    )(z, z, frep, z, qt, do, dot, lse, delta)


def _gla_step(st, q, k, v, la):
    row = lax.broadcasted_iota(jnp.int32, (CHUNK, CHUNK), 0)
    col = lax.broadcasted_iota(jnp.int32, (CHUNK, CHUNK), 1)
    tri = (col <= row).astype(F32)
    a_cum = jnp.dot(tri, la, preferred_element_type=F32, precision=lax.Precision.HIGHEST)
    a_tot = jnp.sum(la, axis=0, keepdims=True)
    k_dec = (_f(k) * jnp.exp(a_tot - a_cum)).astype(BF16)
    qs = (_f(q) * (HEAD_DIM ** -0.5)).astype(BF16)
    st = st * jnp.exp(a_tot) + lax.dot_general(v.astype(BF16), k_dec, (((0,), (0,)), ((), ())), preferred_element_type=F32)
    o = lax.dot_general(qs, st.astype(BF16), (((1,), (1,)), ((), ())), preferred_element_type=F32)
    return st, o


def _gla_blocks(t):
    r = min(256, t)
    return r, t // r, r // CHUNK


def _gla_fwd(z, la, name):
    t = z.shape[0]
    r, nb, nch = _gla_blocks(t)

    def body(q_ref, k_ref, v_ref, la_ref, o_ref, sp_ref, st_ref):
        @pl.when(pl.program_id(0) == 0)
        def _():
            st_ref[...] = jnp.zeros(st_ref.shape, F32)
        for c in range(nch):
            rows = slice(c * CHUNK, (c + 1) * CHUNK)
            for h in range(GLA_HEADS):
                kc = slice(h * HEAD_DIM, (h + 1) * HEAD_DIM)
                vc = slice(h * GLA_VAL_DIM, (h + 1) * GLA_VAL_DIM)
                st = st_ref[h]
                sp_ref[c, h] = st
                st, o = _gla_step(st, q_ref[rows, kc], k_ref[rows, kc], v_ref[rows, vc], la_ref[rows, kc])
                st_ref[h] = st
                o_ref[rows, vc] = o

    return pl.pallas_call(
        body, name=name, grid=(nb,),
        in_specs=[pl.BlockSpec((r, GLA_KW), lambda i: (i, Z_GQ // GLA_KW)), pl.BlockSpec((r, GLA_KW), lambda i: (i, Z_GK // GLA_KW)),
                  pl.BlockSpec((r, GLA_VW), lambda i: (i, Z_GV // GLA_VW)), pl.BlockSpec((r, GLA_KW), lambda i: (i, 0))],
        out_specs=[pl.BlockSpec((r, GLA_VW), lambda i: (i, 0)),
                   pl.BlockSpec((nch, GLA_HEADS, GLA_VAL_DIM, HEAD_DIM), lambda i: (i, 0, 0, 0))],
        out_shape=[jax.ShapeDtypeStruct((t, GLA_VW), F32),
                   jax.ShapeDtypeStruct((t // CHUNK, GLA_HEADS, GLA_VAL_DIM, HEAD_DIM), F32)],
        scratch_shapes=[pltpu.VMEM((GLA_HEADS, GLA_VAL_DIM, HEAD_DIM), F32)],
        compiler_params=pltpu.CompilerParams(dimension_semantics=("arbitrary",), vmem_limit_bytes=VMEM_LIMIT),
    )(z, z, z, la)


def _gla_bwd(z, la, sprev, do, name):
    t = z.shape[0]
    r, nb, nch = _gla_blocks(t)

    def body(q_ref, k_ref, v_ref, la_ref, sp_ref, do_ref, dq_ref, dk_ref, dv_ref, dla_ref, dst_ref):
        @pl.when(pl.program_id(0) == 0)
        def _():
            dst_ref[...] = jnp.zeros(dst_ref.shape, F32)
        for c in reversed(range(nch)):
            rows = slice(c * CHUNK, (c + 1) * CHUNK)
            for h in range(GLA_HEADS):
                kc = slice(h * HEAD_DIM, (h + 1) * HEAD_DIM)
                vc = slice(h * GLA_VAL_DIM, (h + 1) * GLA_VAL_DIM)
                _, vjp = jax.vjp(_gla_step, sp_ref[c, h], q_ref[rows, kc], k_ref[rows, kc], v_ref[rows, vc], la_ref[rows, kc])
                dst, dq, dk, dv, dla = vjp((dst_ref[h], do_ref[rows, vc]))
                dst_ref[h] = dst
                dq_ref[rows, kc] = dq
                dk_ref[rows, kc] = dk
                dv_ref[rows, vc] = dv
                dla_ref[rows, kc] = dla

    rev = lambda i: (nb - 1 - i, 0)
    return pl.pallas_call(
        body, name=name, grid=(nb,),
        in_specs=[pl.BlockSpec((r, GLA_KW), lambda i: (nb - 1 - i, Z_GQ // GLA_KW)), pl.BlockSpec((r, GLA_KW), lambda i: (nb - 1 - i, Z_GK // GLA_KW)),
                  pl.BlockSpec((r, GLA_VW), lambda i: (nb - 1 - i, Z_GV // GLA_VW)), pl.BlockSpec((r, GLA_KW), rev),
                  pl.BlockSpec((nch, GLA_HEADS, GLA_VAL_DIM, HEAD_DIM), lambda i: (nb - 1 - i, 0, 0, 0)),
                  pl.BlockSpec((r, GLA_VW), rev)],
        out_specs=[pl.BlockSpec((r, GLA_KW), rev), pl.BlockSpec((r, GLA_KW), rev), pl.BlockSpec((r, GLA_VW), rev), pl.BlockSpec((r, GLA_KW), rev)],
        out_shape=[jax.ShapeDtypeStruct((t, GLA_KW), BF16), jax.ShapeDtypeStruct((t, GLA_KW), BF16),
                   jax.ShapeDtypeStruct((t, GLA_VW), BF16), jax.ShapeDtypeStruct((t, GLA_KW), F32)],
        scratch_shapes=[pltpu.VMEM((GLA_HEADS, GLA_VAL_DIM, HEAD_DIM), F32)],
        compiler_params=pltpu.CompilerParams(dimension_semantics=("arbitrary",), vmem_limit_bytes=VMEM_LIMIT),
    )(z, z, z, la, sprev, do)


def _local_step(x, p, tgt, shards, sp, cidx, chip):
    t = x.shape[0]
    tq, tk, _, _ = _fox_tiles(t)
    full, reduced = {}, {}

    def plan(names):
        keys, shapes, places = [], [], []
        for n in names:
            r, cc = shards[n].shape
            key, part, parts = FUSED.get(n, (n, 0, 1))
            if key not in keys:
                keys.append(key)
                stacked = n in ROW_SHARDED or n == "w_in"
                shapes.append(jax.ShapeDtypeStruct((4 * r, cc) if stacked else (r, 4 * cc * parts), shards[n].dtype))
            places.append((keys.index(key), r, 0, 0) if n in ROW_SHARDED or n == "w_in" else (keys.index(key), 0, part * 4 * cc, cc))
        return keys, shapes, places

    def gather(names):
        _, shapes, places = plan(names)
        return _ag_comm([shards[n] for n in names], shapes, places)

    def landed(names, got):
        keys, _, _ = plan(names)
        for key, g in zip(keys, got):
            full[key] = g

    def pair_sums(grads):
        parts, entries = [], []
        for g, names in grads:
            cols = g.shape[1] // (4 * len(names))
            if names[0] in ROW_SHARDED or names[0] == "w_in":
                parts.append(g if g.ndim == 3 else _shard_parts(g, names[0]))
                entries.append((names[0], len(parts) - 1, "slot", 0, parts[-1].shape[2]))
            else:
                parts.append(g[None])
                entries += [(n, len(parts) - 1, "cols", k * 4 * cols, cols) for k, n in enumerate(names)]
        swapped = _pair_swap(parts, "pair_swap")
        return entries, [_sum_half(a, b, cidx, "sum_half") for a, b in zip(parts, swapped)]

    def exchange(entries, sums):
        return _exchange_comm(sums, [e[1:] for e in entries])

    def exchanged(entries, sums, got):
        for (n, si, mode, first, _), g in zip(entries, got):
            reduced[n] = (g, sums[si], mode, first)

    first = ["ffn1_w_gate", "ffn1_w_up"]
    landed(first, _run_comm(gather(first), "all_gather"))
    w_gu1 = full["gu1"]
    n1 = _norm_fwd(x, sp["ffn1_norm"], "norm1_fwd")
    names = ["ffn1_w_down", "w_merge_gate", "gla_gate_up"]
    gu1, got = _mm(n1, w_gu1, name="mm_gu_gather", comm=gather(names))
    landed(names, got)
    a1 = _act_fwd(gu1, "act_fwd")
    names = ["w_in"]
    f1, got = _mm(a1, full["ffn1_w_down"], out_dtype=F32, name="mm_down_gather", comm=gather(names))
    landed(names, got)
    w_big, w_sm = _in_layout(full["w_in"], full["w_merge_gate"])
    gup = jnp.zeros((SMALL_W, GLA_KW), F32).at[FOX_HEADS:FOX_HEADS + GLA_RANK].set(full["gla_gate_up"])
    h1, u = _resnorm_fwd(x, f1, sp["mix_norm"], 0.5, "resnorm_fwd_half")
    names = ["w_branch_fox", "w_branch_gla", "w_out", "w_ple_proj", "w_ple_gate"]
    z, got = _mm(u, w_big, name="mm_in_gather", comm=gather(names))
    landed(names, got)
    s = _mm(u, w_sm, out_dtype=F32, name="mm_in_small")
    lf, la = _small_fwd(s, sp["fb"], gup, sp["gb"], "small_fwd")
    fp = _cumsum_tokens(lf, False, "cumsum_fwd")
    frep = jnp.broadcast_to(fp[:, :FOX_HEADS].T[:, :, None], (FOX_HEADS, t, HEAD_DIM))
    qt = _blocked_t(z[:, Z_FQ:Z_FQ + FOX_W], tq)
    kt = _blocked_t(z[:, Z_FK:Z_FK + FOX_W], tk)
    vt = _blocked_t(z[:, Z_FV:Z_FV + FOX_W], tk)
    names = ["ffn2_w_gate", "ffn2_w_up"]
    ot, lse, got = _fox_fwd(z, qt, vt, frep, "fox_fwd_gather", comm=gather(names))
    landed(names, got)
    w_gu2 = full["gu2"]
    y_fox = _unblocked_t(ot)
    o_gla, sprev = _gla_fwd(z, la, "gla_fwd")
    y_gla = _gla_out_fwd(o_gla, z, sp["ghn"], "gla_out_fwd")
    bf = _mm(y_fox, full["w_branch_fox"], name="mm_branch")
    bg = _mm(y_gla, full["w_branch_gla"], name="mm_branch")
    merged = _merge_fwd(z, bf, bg, sp["bm"], "merge_fwd")
    mo = _mm(merged, full["w_out"], out_dtype=F32, name="mm_out")
    h2, n2 = _resnorm_fwd(h1, mo, sp["ffn2_norm"], 1.0, "resnorm_fwd_one")
    names = ["ffn2_w_down"]
    gu2, got = _mm(n2, w_gu2, name="mm_gu_gather_down", comm=gather(names))
    landed(names, got)
    a2 = _act_fwd(gu2, "act_fwd")
    f2 = _mm(a2, full["ffn2_w_down"], out_dtype=F32, name="mm_down")
    h3, n4 = _resnorm_fwd(h2, f2, sp["ple_norm"], 0.5, "resnorm_fwd_half")
    pgl = _mm(n4, full["w_ple_gate"], name="mm_pg")
    pb = p.astype(BF16)
    pp = _mm(pb, full["w_ple_proj"], name="mm_pp")

    dh3, dpgl, dpp, loss, d_final = _head(h3, pgl, pp, tgt, sp["final_norm"], "head")
    ds_ = {"final_norm": d_final}
    entries, sums = pair_sums([(_mm(n4, dpgl, ta=True, name="mm_dw_sq"), ["w_ple_gate"]), (_mm(pb, dpp, ta=True, name="mm_dw_pp"), ["w_ple_proj"])])
    dn4, got = _mm(dpgl, full["w_ple_gate"], tb=True, out_dtype=F32, name="mm_dx_sq_f32_exchange", comm=exchange(entries, sums))
    exchanged(entries, sums, got)
    dh3, df2, ds_["ple_norm"] = _norm_bwd(h3, [dn4], dh3, sp["ple_norm"], 0.5, "norm_bwd_1")

    def ffn_bwd(n, gu, a, df, wgu, wd, which):
        entries, sums = pair_sums([(_mm(a, df, ta=True, name="mm_dw_down"), [which + "_w_down"])])
        da, got = _mm(df, wd, tb=True, name="mm_dx_down_exchange", comm=exchange(entries, sums))
        exchanged(entries, sums, got)
        dgu = _act_bwd(gu, da, "act_bwd")
        entries, sums = pair_sums([(_mm(n, dgu, ta=True, name="mm_dw_gu"), [which + "_w_gate", which + "_w_up"])])
        dn, got = _mm(dgu, wgu, tb=True, out_dtype=F32, name="mm_dx_gu_exchange", comm=exchange(entries, sums))
        exchanged(entries, sums, got)
        return dn

    dn2 = ffn_bwd(n2, gu2, a2, df2, w_gu2, full["ffn2_w_down"], "ffn2")
    dh2, dmix, ds_["ffn2_norm"] = _norm_bwd(h2, [dn2], dh3, sp["ffn2_norm"], 1.0, "norm_bwd_1")

    dw_out = _mm(merged, dmix, ta=True, name="mm_dw_sq")
    dmerged = _mm(dmix, full["w_out"], tb=True, name="mm_dx_sq")
    dgl, dbf, dbg, ds_["bm"] = _merge_bwd(z, bf, bg, sp["bm"], dmerged, "merge_bwd")
    mix_entries, mix_sums = pair_sums([(dw_out, ["w_out"]), (_mm(y_fox, dbf, ta=True, name="mm_dw_branch"), ["w_branch_fox"]),
                                       (_mm(y_gla, dbg, ta=True, name="mm_dw_branch"), ["w_branch_gla"])])
    dy_fox = _mm(dbf, full["w_branch_fox"], tb=True, name="mm_dx_branch")
    dy_gla = _mm(dbg, full["w_branch_gla"], tb=True, name="mm_dx_branch")

    do_gla, dgr, ds_["ghn"] = _gla_out_bwd(o_gla, z, sp["ghn"], dy_gla, "gla_out_bwd")
    dgq, dgk, dgv, dla = _gla_bwd(z, la, sprev, do_gla, "gla_bwd")
    dot = _blocked_t(dy_fox, tq)
    dqt, delta, df_query = _fox_bwd_q(z, qt, kt, ot, dot, lse, frep, "fox_bwd_q")
    dfq = _unblocked_t(dqt)
    dfk, dfv, df_key = _fox_bwd_kv(z, qt, dy_fox, dot, lse, delta, frep, "fox_bwd_kv")
    df = df_query.reshape(FOX_HEADS, t) - df_key.reshape(FOX_HEADS, t)
    dfp = jnp.pad(df.T, ((0, 0), (0, SMALL_W - FOX_HEADS)))
    dlf = _cumsum_tokens(dfp, True, "cumsum_bwd")
    dsm, ds_["fb"], dgup, ds_["gb"] = _small_bwd(s, sp["fb"], gup, sp["gb"], dlf, dla, "small_bwd")
    dz = jnp.concatenate([dfq, dfk, dfv, dgq, dgk, dgv, dgr, dgl], axis=1)
    dw_big, got = _mm(u, dz, ta=True, name="mm_dw_in_exchange", comm=exchange(mix_entries, mix_sums))
    exchanged(mix_entries, mix_sums, got)
    dw_sm = _mm(u, dsm, ta=True, out_dtype=F32, name="mm_dw_in_small").astype(BF16)
    entries, sums = pair_sums([(_w_in_parts(dw_big, dw_sm), ["w_in"]), (dw_big[:, Z_GL:], ["w_merge_gate"]), (dgup[FOX_HEADS:FOX_HEADS + GLA_RANK], ["gla_gate_up"])])
    du1, got = _mm(dz, w_big, tb=True, out_dtype=F32, name="mm_dx_in_exchange", comm=exchange(entries, sums))
    exchanged(entries, sums, got)
    du2 = _mm(dsm, w_sm, tb=True, out_dtype=F32, name="mm_dx_in_small")
    dh1, df1, ds_["mix_norm"] = _norm_bwd(h1, [du1, du2], dh2, sp["mix_norm"], 0.5, "norm_bwd_2")

    dn1 = ffn_bwd(n1, gu1, a1, df1, w_gu1, full["ffn1_w_down"], "ffn1")
    grad_x, _, ds_["ffn1_norm"] = _norm_bwd(x, [dn1], dh1, sp["ffn1_norm"], 1.0, "norm_bwd_1")
    return loss, grad_x, reduced, ds_


def _half_rows(ref, which):
    r2 = ref.shape[0] // 2
    return ref.at[pl.ds(pl.multiple_of(which * r2, r2), r2)]


class _Comm:
    def __init__(self, ins, out_shape, sems, start, finish):
        self.ins, self.out_shape, self.sems, self.start, self.finish = ins, out_shape, sems, start, finish


def _run_comm(comm, name):
    n_in, n_out = len(comm.ins), len(comm.out_shape)

    def body(*refs):
        parts = refs[:n_in], refs[n_in:n_in + n_out], refs[n_in + n_out:]
        comm.start(*parts)
        comm.finish(*parts)

    return pl.pallas_call(
        body, name=name, in_specs=[ANY] * n_in, out_specs=[ANY] * n_out, out_shape=comm.out_shape,
        scratch_shapes=comm.sems, compiler_params=pltpu.CompilerParams(has_side_effects=True),
    )(*comm.ins)


def _hosted(body, comm, *, name, grid, in_specs, out_specs, out_shape, scratch_shapes, semantics, args):
    if comm is None:
        res = pl.pallas_call(
            body, name=name, grid=grid, in_specs=in_specs, out_specs=out_specs, out_shape=out_shape, scratch_shapes=scratch_shapes,
            compiler_params=pltpu.CompilerParams(dimension_semantics=semantics, vmem_limit_bytes=VMEM_LIMIT),
        )(*args)
        return res, None
    ni, no, ns = len(in_specs), len(out_shape), len(scratch_shapes)
    ci, co = len(comm.ins), len(comm.out_shape)

    def wrapped(*refs):
        h_in, c_in = refs[:ni], refs[ni:ni + ci]
        h_out, c_out = refs[ni + ci:ni + ci + no], refs[ni + ci + no:ni + ci + no + co]
        h_scr, c_sem = refs[ni + ci + no + co:ni + ci + no + co + ns], refs[ni + ci + no + co + ns:]
        ids = [pl.program_id(axis) for axis in range(len(grid))]
        first = functools.reduce(jnp.logical_and, [i == 0 for i in ids])
        last = functools.reduce(jnp.logical_and, [i == g - 1 for i, g in zip(ids, grid)])

        @pl.when(first)
        def _():
            comm.start(c_in, c_out, c_sem)

        body(*h_in, *h_out, *h_scr)

        @pl.when(last)
        def _():
            comm.finish(c_in, c_out, c_sem)

    res = pl.pallas_call(
        wrapped, name=name, grid=grid, in_specs=list(in_specs) + [ANY] * ci, out_specs=list(out_specs) + [ANY] * co,
        out_shape=list(out_shape) + list(comm.out_shape), scratch_shapes=list(scratch_shapes) + list(comm.sems),
        compiler_params=pltpu.CompilerParams(dimension_semantics=("arbitrary",) * len(grid), vmem_limit_bytes=VMEM_LIMIT, has_side_effects=True),
    )(*args, *comm.ins)
    return res[:no], res[no:]


def _ag_comm(shards, out_shape, places):
    n = len(shards)

    def copies(ins, outs, sems):
        ici_send, ici_recv, d2d_send, d2d_recv = sems
        x, y, c = lax.axis_index("x"), lax.axis_index("y"), lax.axis_index("c")
        chips = [(1 - x, y), (x, 1 - y), (1 - x, 1 - y)]
        slot = lambda chip: 2 * chip[0] + chip[1]

        def window(wi, origin, half):
            out, row_step, col_base, col_step = places[wi]
            r, cc = shards[wi].shape
            rows = pl.ds(pl.multiple_of(slot(origin) * row_step + half * (r // 2), r // 2), r // 2)
            cols = pl.ds(pl.multiple_of(col_base + slot(origin) * col_step, HEAD_DIM), cc) if col_step else pl.ds(col_base, cc)
            return outs[out].at[rows, cols]

        def over_ici(wi, j, origin):
            return pltpu.make_async_remote_copy(
                src_ref=_half_rows(ins[wi], c), dst_ref=window(wi, origin, c),
                send_sem=ici_send.at[3 * wi + j], recv_sem=ici_recv.at[3 * wi + j],
                device_id=(chips[j][0], chips[j][1], c), device_id_type=MESH)

        def over_d2d(wi, j, half):
            place = window(wi, chips[j], half)
            return pltpu.make_async_remote_copy(
                src_ref=place, dst_ref=place, send_sem=d2d_send.at[3 * wi + j], recv_sem=d2d_recv.at[3 * wi + j],
                device_id=(x, y, 1 - c), device_id_type=MESH)

        return over_ici, over_d2d, (x, y), chips, c

    def chunk_rows(wi):
        r, cc = shards[wi].shape
        item = shards[wi].dtype.itemsize
        return _pick(r, max(32 // item, BOUNCE_BYTES // (cc * item)), 32 // item)

    def start(ins, outs, scratch):
        over_ici, _, me, _, _ = copies(ins, outs, scratch[:4])
        for wi in range(n):
            for j in range(3):
                over_ici(wi, j, me).start()
        loc_sems = scratch[4]
        for wi in range(n):
            out, row_step, col_base, col_step = places[wi]
            r, cc = shards[wi].shape
            rc = chunk_rows(wi)
            buf = scratch[5 + wi]
            slot = 2 * me[0] + me[1]
            cols = pl.ds(pl.multiple_of(col_base + slot * col_step, HEAD_DIM), cc) if col_step else pl.ds(col_base, cc)

            def load(k):
                return pltpu.make_async_copy(ins[wi].at[pl.ds(k * rc, rc)], buf.at[k % 2], loc_sems.at[2 * wi])

            def store(k):
                rows = pl.ds(pl.multiple_of(slot * row_step + k * rc, rc), rc)
                return pltpu.make_async_copy(buf.at[k % 2], outs[out].at[rows, cols], loc_sems.at[2 * wi + 1])

            load(0).start()
            for k in range(r // rc):
                load(k).wait()
                if k + 1 < r // rc:
                    load(k + 1).start()
                store(k).start()
                store(k).wait()

    def finish(ins, outs, scratch):
        over_ici, over_d2d, me, chips, c = copies(ins, outs, scratch[:4])
        for wi in range(n):
            for j in range(3):
                over_ici(wi, j, chips[j]).wait_recv()
                over_d2d(wi, j, c).start()
        for wi in range(n):
            for j in range(3):
                over_d2d(wi, j, 1 - c).wait_recv()
        for wi in range(n):
            for j in range(3):
                over_ici(wi, j, me).wait_send()
                over_d2d(wi, j, c).wait_send()

    bounce = [pltpu.VMEM((min(2, s.shape[0] // chunk_rows(wi)), chunk_rows(wi), s.shape[1]), s.dtype) for wi, s in enumerate(shards)]
    return _Comm(list(shards), list(out_shape), [pltpu.SemaphoreType.DMA((3 * n,))] * 4 + [pltpu.SemaphoreType.DMA((2 * n,))] + bounce,
                 start, finish)


def _pair_swap(parts, name):
    n = len(parts)

    def body(*refs):
        ins, outs = refs[:n], refs[n:2 * n]
        send_sems, recv_sems = refs[2 * n:]
        x, y, c = lax.axis_index("x"), lax.axis_index("y"), lax.axis_index("c")

        def swap(wi):
            r2 = parts[wi].shape[1] // 2
            return pltpu.make_async_remote_copy(
                src_ref=ins[wi].at[:, pl.ds(pl.multiple_of((1 - c) * r2, r2), r2)], dst_ref=outs[wi],
                send_sem=send_sems.at[wi], recv_sem=recv_sems.at[wi], device_id=(x, y, 1 - c), device_id_type=MESH)

        copies = [swap(wi) for wi in range(n)]
        for cp in copies:
            cp.start()
        for cp in copies:
            cp.wait()

    return pl.pallas_call(
        body, name=name, in_specs=[ANY] * n, out_specs=[ANY] * n,
        out_shape=[jax.ShapeDtypeStruct((s.shape[0], s.shape[1] // 2, s.shape[2]), s.dtype) for s in parts],
        scratch_shapes=[pltpu.SemaphoreType.DMA((n,))] * 2, compiler_params=pltpu.CompilerParams(has_side_effects=True),
    )(*parts)


def _row_tile(r, c, budget=1 << 19):
    return r if r <= 8 else _pick(r, max(8, budget // c), 8)


def _sum_half(parts, other, cidx, name):
    nl, r, cc = parts.shape
    r2 = r // 2
    tr = _row_tile(r2, cc)

    def body(c_ref, p_ref, q_ref, o_ref):
        o_ref[...] = (_f(p_ref[...]) + _f(q_ref[...])).astype(o_ref.dtype)

    return pl.pallas_call(
        body, name=name, out_shape=jax.ShapeDtypeStruct((nl, r2, cc), parts.dtype),
        grid_spec=pltpu.PrefetchScalarGridSpec(
            num_scalar_prefetch=1, grid=(nl, r2 // tr),
            in_specs=[pl.BlockSpec((None, None, tr, cc), lambda l, i, c_ref: (l, c_ref[0], i, 0)),
                      pl.BlockSpec((None, tr, cc), lambda l, i, c_ref: (l, i, 0))],
            out_specs=pl.BlockSpec((None, tr, cc), lambda l, i, c_ref: (l, i, 0))),
        compiler_params=pltpu.CompilerParams(dimension_semantics=("parallel", "parallel"), vmem_limit_bytes=VMEM_LIMIT),
    )(cidx, parts.reshape(nl, 2, r2, cc), other)


def _exchange_comm(sums, entries):
    n = len(entries)

    def copies(ins, outs, sems):
        send_sems, recv_sems = sems
        x, y, c = lax.axis_index("x"), lax.axis_index("y"), lax.axis_index("c")
        chips = [(1 - x, y), (x, 1 - y), (1 - x, 1 - y)]
        slot = lambda chip: 2 * chip[0] + chip[1]

        def piece(wi, dest):
            si, mode, first, cols = entries[wi]
            if mode == "cols":
                return ins[si].at[0, :, pl.ds(pl.multiple_of(first + slot(dest) * cols, HEAD_DIM), cols)]
            return ins[si].at[slot(dest) if mode == "slot" else 0]

        def remote(wi, j, origin):
            return pltpu.make_async_remote_copy(
                src_ref=piece(wi, chips[j]), dst_ref=outs[wi].at[slot(origin)],
                send_sem=send_sems.at[3 * wi + j], recv_sem=recv_sems.at[3 * wi + j],
                device_id=(chips[j][0], chips[j][1], c), device_id_type=MESH)

        return remote, (x, y), chips

    def start(ins, outs, sems):
        remote, me, _ = copies(ins, outs, sems)
        for wi in range(n):
            for j in range(3):
                remote(wi, j, me).start()

    def finish(ins, outs, sems):
        remote, me, chips = copies(ins, outs, sems)
        for wi in range(n):
            for j in range(3):
                remote(wi, j, chips[j]).wait_recv()
        for wi in range(n):
            for j in range(3):
                remote(wi, j, me).wait_send()

    out_shape = [jax.ShapeDtypeStruct((4, sums[si].shape[1], cols), sums[si].dtype) for si, _, _, cols in entries]
    return _Comm(list(sums), out_shape, [pltpu.SemaphoreType.DMA((3 * n,))] * 2, start, finish)


def _sum_chips(got, own, mode, first, chip, name):
    _, r2, cc = got.shape
    tr = _row_tile(r2, cc)
    own_block = {"slot": lambda i, chip_ref: (chip_ref[0], i, 0), "same": lambda i, chip_ref: (0, i, 0),
                 "cols": lambda i, chip_ref: (0, i, first // cc + chip_ref[0])}[mode]

    def body(chip_ref, g_ref, own_ref, o_ref):
        term = lambda k: jnp.where(chip_ref[0] == k, _f(own_ref[...]), _f(g_ref[k]))
        o_ref[...] = ((term(0) + term(1)) + term(2)) + term(3)

    return pl.pallas_call(
        body, name=name, out_shape=jax.ShapeDtypeStruct((r2, cc), F32),
        grid_spec=pltpu.PrefetchScalarGridSpec(
            num_scalar_prefetch=1, grid=(r2 // tr,),
            in_specs=[pl.BlockSpec((4, tr, cc), lambda i, chip_ref: (0, i, 0)),
                      pl.BlockSpec((None, tr, cc), own_block)],
            out_specs=pl.BlockSpec((tr, cc), lambda i, chip_ref: (i, 0))),
        compiler_params=pltpu.CompilerParams(dimension_semantics=("parallel",), vmem_limit_bytes=VMEM_LIMIT),
    )(chip, got, own)


def _pair_gather(halves, name):
    n = len(halves)

    def body(*refs):
        ins, outs = refs[:n], refs[n:2 * n]
        send_sems, recv_sems = refs[2 * n:]
        x, y, c = lax.axis_index("x"), lax.axis_index("y"), lax.axis_index("c")
        copies = [pltpu.make_async_remote_copy(
            src_ref=ins[wi], dst_ref=outs[wi], send_sem=send_sems.at[wi], recv_sem=recv_sems.at[wi],
            device_id=(x, y, 1 - c), device_id_type=MESH) for wi in range(n)]
        for cp in copies:
            cp.start()
        for cp in copies:
            cp.wait()

    return pl.pallas_call(
        body, name=name, in_specs=[ANY] * n, out_specs=[ANY] * n,
        out_shape=[jax.ShapeDtypeStruct(s.shape, s.dtype) for s in halves],
        scratch_shapes=[pltpu.SemaphoreType.DMA((n,))] * 2, compiler_params=pltpu.CompilerParams(has_side_effects=True),
    )(*halves)


def _adamw_update(g, w, m, v):
    m_new = ADAM_B1 * m + (1.0 - ADAM_B1) * g
    v_new = ADAM_B2 * v + (1.0 - ADAM_B2) * jnp.square(g)
    m_hat = m_new / (1.0 - ADAM_B1 ** ADAM_STEP)
    v_hat = v_new / (1.0 - ADAM_B2 ** ADAM_STEP)
    return -ADAM_LR * (m_hat / (jnp.sqrt(v_hat) + ADAM_EPS) + ADAM_WD * w), m_new, v_new


def _adamw_whole(g, w, m, v, name):
    r, c = w.shape
    tc = 256

    def body(g_ref, w_ref, m_ref, v_ref, d_ref, nm_ref, nv_ref):
        d_ref[...], nm_ref[...], nv_ref[...] = _adamw_update(g_ref[...], w_ref[...], m_ref[...], v_ref[...])

    blk = pl.BlockSpec((r, tc), lambda i: (0, i))
    return pl.pallas_call(
        body, name=name, grid=(c // tc,), in_specs=[blk] * 4, out_specs=[blk] * 3, out_shape=[jax.ShapeDtypeStruct((r, c), F32)] * 3,
        compiler_params=pltpu.CompilerParams(dimension_semantics=("parallel",), vmem_limit_bytes=VMEM_LIMIT),
    )(g, w, m, v)


def _adamw(mine, other, cidx, w, m, v, name):
    r, c = w.shape
    tr = _row_tile(r // 2, c, 1 << 18)
    nh = (r // 2) // tr

    def body(c_ref, mine_ref, other_ref, w_ref, m_ref, v_ref, g_ref, d_ref, nm_ref, nv_ref):
        g = jnp.where(pl.program_id(0) // nh == c_ref[0], mine_ref[...], other_ref[...])
        g_ref[...] = g
        d_ref[...], nm_ref[...], nv_ref[...] = _adamw_update(g, w_ref[...], m_ref[...], v_ref[...])

    blk = pl.BlockSpec((tr, c), lambda i, c_ref: (i, 0))
    mine_spec = pl.BlockSpec((tr, c), lambda i, c_ref: (jnp.where(i // nh == c_ref[0], i % nh, 0), 0))
    other_spec = pl.BlockSpec((tr, c), lambda i, c_ref: (jnp.where(i // nh == c_ref[0], 0, i % nh), 0))
    return pl.pallas_call(
        body, name=name, out_shape=[jax.ShapeDtypeStruct((r, c), F32)] * 4,
        grid_spec=pltpu.PrefetchScalarGridSpec(
            num_scalar_prefetch=1, grid=(r // tr,), in_specs=[mine_spec, other_spec, blk, blk, blk], out_specs=[blk] * 4),
        compiler_params=pltpu.CompilerParams(dimension_semantics=("arbitrary",), vmem_limit_bytes=VMEM_LIMIT),
    )(cidx, mine, other, w, m, v)


BIG = ["ffn1_w_gate", "ffn1_w_up", "ffn1_w_down", "w_in", "gla_gate_up", "w_branch_fox", "w_branch_gla", "w_merge_gate", "w_out",
       "ffn2_w_gate", "ffn2_w_up", "ffn2_w_down", "w_ple_proj", "w_ple_gate"]
ROW_SHARDED = ("ffn1_w_down", "w_out", "ffn2_w_down", "w_ple_gate")
FUSED = {"ffn1_w_gate": ("gu1", 0, 2), "ffn1_w_up": ("gu1", 1, 2), "ffn2_w_gate": ("gu2", 0, 2), "ffn2_w_up": ("gu2", 1, 2)}
SMALL = ["ffn1_norm", "mix_norm", "fox_forget_bias", "gla_gate_bias", "gla_head_norm", "b_merge_gate", "ffn2_norm", "ple_norm", "final_norm"]
NAMES = ["ffn1_norm", "ffn1_w_gate", "ffn1_w_up", "ffn1_w_down", "mix_norm", "w_in", "fox_forget_bias", "gla_gate_up", "gla_gate_bias",
         "gla_head_norm", "w_branch_fox", "w_branch_gla", "w_merge_gate", "b_merge_gate", "w_out", "ffn2_norm", "ffn2_w_gate", "ffn2_w_up",
         "ffn2_w_down", "ple_norm", "w_ple_proj", "w_ple_gate", "final_norm"]
W_IN_COLS = (FOX_W, FOX_W, FOX_W, FOX_HEADS, GLA_KW, GLA_KW, GLA_VW, GLA_VW, GLA_RANK)
SMALL_ROWS, SMALL_COLS = 16, 1024


def _shard_parts(full, name):
    assert name in ROW_SHARDED, name
    return full.reshape(4, full.shape[0] // 4, full.shape[1])


W_IN_WIDE = ((0, 3 * FOX_W), (3 * FOX_W + FOX_HEADS, 3 * FOX_W + FOX_HEADS + 2 * GLA_KW + 2 * GLA_VW))
W_IN_NARROW = ((3 * FOX_W, 3 * FOX_W + FOX_HEADS), (sum(W_IN_COLS) - GLA_RANK, sum(W_IN_COLS)))


def _in_layout(stacked, w_merge_gate):
    per = stacked.shape[1]

    def columns(lo, hi):
        out = []
        while lo < hi:
            j, end = lo // per, min(hi, (lo // per + 1) * per)
            out.append(stacked[j * D_MODEL:(j + 1) * D_MODEL, lo - j * per:end - j * per])
            lo = end
        return out
    big = jnp.concatenate(columns(*W_IN_WIDE[0]) + columns(*W_IN_WIDE[1]) + [w_merge_gate], axis=1)
    sm = jnp.concatenate(columns(*W_IN_NARROW[0]) + columns(*W_IN_NARROW[1]) + [jnp.zeros((D_MODEL, SMALL_W - FOX_HEADS - GLA_RANK), BF16)], axis=1)
    return big, sm


def _w_in_parts(dw_big, dw_sm):
    runs = [(W_IN_WIDE[0], dw_big, Z_FQ), (W_IN_NARROW[0], dw_sm, 0), (W_IN_WIDE[1], dw_big, Z_GQ), (W_IN_NARROW[1], dw_sm, FOX_HEADS)]
    per = sum(W_IN_COLS) // 4
    parts = []
    for j in range(4):
        lo, hi, pieces = j * per, (j + 1) * per, []
        for (a, b), src, at in runs:
            if max(a, lo) < min(b, hi):
                pieces.append(src[:, at + max(a, lo) - a:at + min(b, hi) - a])
        parts.append(jnp.concatenate(pieces, axis=1))
    return jnp.stack(parts)


def _pad_lanes(a, width):
    return jnp.pad(a, ((0, 0), (0, width - a.shape[1])))


def kernel(x, p, ffn1_norm, ffn1_w_gate, ffn1_w_up, ffn1_w_down, mix_norm, w_in, fox_forget_bias, gla_gate_up, gla_gate_bias, gla_head_norm, w_branch_fox, w_branch_gla, w_merge_gate, b_merge_gate, w_out, ffn2_norm, ffn2_w_gate, ffn2_w_up, ffn2_w_down, ple_norm, w_ple_proj, w_ple_gate, final_norm, loss_target, m_ffn1_norm, m_ffn1_w_gate, m_ffn1_w_up, m_ffn1_w_down, m_mix_norm, m_w_in, m_fox_forget_bias, m_gla_gate_up, m_gla_gate_bias, m_gla_head_norm, m_w_branch_fox, m_w_branch_gla, m_w_merge_gate, m_b_merge_gate, m_w_out, m_ffn2_norm, m_ffn2_w_gate, m_ffn2_w_up, m_ffn2_w_down, m_ple_norm, m_w_ple_proj, m_w_ple_gate, m_final_norm, v_ffn1_norm, v_ffn1_w_gate, v_ffn1_w_up, v_ffn1_w_down, v_mix_norm, v_w_in, v_fox_forget_bias, v_gla_gate_up, v_gla_gate_bias, v_gla_head_norm, v_w_branch_fox, v_w_branch_gla, v_w_merge_gate, v_b_merge_gate, v_w_out, v_ffn2_norm, v_ffn2_w_gate, v_ffn2_w_up, v_ffn2_w_down, v_ple_norm, v_w_ple_proj, v_w_ple_gate, v_final_norm):
    args = dict(locals())
    wts = {n: args[n] for n in NAMES}
    mom = {n: args["m_" + n] for n in NAMES}
    var = {n: args["v_" + n] for n in NAMES}
    two_d = lambda a: a.reshape(-1, a.shape[-1])

    wire = lambda n: F32 if n == "gla_gate_up" else BF16
    cidx = lax.axis_index("c").astype(jnp.int32).reshape(1)
    chip = (2 * lax.axis_index("x") + lax.axis_index("y")).astype(jnp.int32)
    shards = {n: two_d(wts[n]).astype(wire(n)) for n in BIG}
    sp = {
        "ffn1_norm": two_d(ffn1_norm), "mix_norm": two_d(mix_norm), "fb": _pad_lanes(two_d(fox_forget_bias), SMALL_W),
        "gb": two_d(gla_gate_bias), "ghn": two_d(gla_head_norm), "bm": two_d(b_merge_gate), "ffn2_norm": two_d(ffn2_norm),
        "ple_norm": two_d(ple_norm), "final_norm": two_d(final_norm),
    }

    loss, grad_x, reduced, ds_ = _local_step(x[0], p[0, 0], loss_target[0], shards, sp, cidx, chip)

    small_g = {"ffn1_norm": ds_["ffn1_norm"], "mix_norm": ds_["mix_norm"], "fox_forget_bias": ds_["fb"][:, :FOX_HEADS],
               "gla_gate_bias": ds_["gb"], "gla_head_norm": ds_["ghn"], "b_merge_gate": ds_["bm"], "ffn2_norm": ds_["ffn2_norm"],
               "ple_norm": ds_["ple_norm"], "final_norm": ds_["final_norm"]}
    small_w = sum(two_d(wts[n]).shape[1] for n in SMALL)
    assert small_w <= SMALL_ROWS * SMALL_COLS
    packed = lambda d: _pad_lanes(jnp.concatenate([two_d(d[n]) for n in SMALL], axis=1), SMALL_ROWS * SMALL_COLS).reshape(SMALL_ROWS, SMALL_COLS)
    parts = [packed(small_g)[None]]
    pair_sums = [_sum_half(a, b, cidx, "sum_half") for a, b in zip(parts, _pair_swap(parts, "pair_swap"))]
    reduced["small"] = (_run_comm(_exchange_comm(pair_sums, [(0, "same", 0, SMALL_COLS)]), "chip_exchange")[0], pair_sums[0], "same", 0)
    mine = [_sum_chips(*reduced[n], chip.reshape(1), "sum_chips") for n in BIG + ["small"]]
    other = _pair_gather(mine, "pair_gather")

    out = {}
    for n, a, b in zip(BIG, mine[:-1], other[:-1]):
        if n == "w_in":
            g_t = jnp.where(cidx[0] == 0, jnp.concatenate([a, b]), jnp.concatenate([b, a])).T
            flip = lambda t: jnp.swapaxes(t, 1, 2)[0]
            res = _adamw_whole(g_t, flip(wts[n]), flip(mom[n]), flip(var[n]), "adamw_w_in")
            out[n] = [jnp.swapaxes(r[None], 1, 2) for r in [g_t, *res]]
            continue
        out[n] = [r.reshape(wts[n].shape) for r in _adamw(a, b, cidx, two_d(wts[n]), two_d(mom[n]), two_d(var[n]), "adamw_" + n)]
    small_out = [r.reshape(1, SMALL_ROWS * SMALL_COLS) for r in _adamw(mine[-1], other[-1], cidx, packed(wts), packed(mom), packed(var), "adamw_small")]
    off = 0
    for n in SMALL:
        cw = two_d(wts[n]).shape[1]
        out[n] = [r[:, off:off + cw].reshape(wts[n].shape) for r in small_out]
        off += cw

    total = lax.psum(loss[0, 0], ("x", "y", "c"))
    return (total, grad_x[None], *[out[n][0] for n in NAMES], *[out[n][1] for n in NAMES],
            *[out[n][2] for n in NAMES], *[out[n][3] for n in NAMES])
```

```python
import functools

import jax
import jax.numpy as jnp
from jax import lax
from jax.experimental import pallas as pl
from jax.experimental.pallas import tpu as pltpu

F32 = jnp.float32
BF16 = jnp.bfloat16
MESH = pl.DeviceIdType.MESH
ANY = pl.BlockSpec(memory_space=pl.ANY)

D_MODEL = 2048
FOX_HEADS = 8
HEAD_DIM = 128
GLA_HEADS = 4
GLA_VAL_DIM = 256
GLA_RANK = 16
GLA_TAU = 16.0
CHUNK = 64
EPS = 1e-6
FOX_W = FOX_HEADS * HEAD_DIM
GLA_KW = GLA_HEADS * HEAD_DIM
GLA_VW = GLA_HEADS * GLA_VAL_DIM
Z_FQ, Z_FK, Z_FV, Z_GQ, Z_GK, Z_GV, Z_GR, Z_GL = 0, 1024, 2048, 3072, 3584, 4096, 5120, 6144
Z_W = Z_GL + 2 * D_MODEL
SMALL_W = 128
NEG = -1e30

ADAM_LR, ADAM_B1, ADAM_B2, ADAM_EPS, ADAM_WD, ADAM_STEP = 0.001, 0.9, 0.999, 1e-08, 0.01, 10

VMEM_LIMIT = 56 * 1024 * 1024
BOUNCE_BYTES = 2 * 1024 * 1024


def _pick(n, target, mult=128):
    if n <= target:
        return n
    best = None
    for d in range(mult, target + 1, mult):
        if n % d == 0:
            best = d
    assert best is not None, (n, target)
    return best


def _mm(a, b, *, ta=False, tb=False, out_dtype=BF16, name, comm=None):
    m, k = (a.shape[1], a.shape[0]) if ta else a.shape
    n = b.shape[0] if tb else b.shape[1]
    assert (b.shape[1] if tb else b.shape[0]) == k
    bk = _pick(k, 4096)
    nk = k // bk
    bm, bn = _pick(m, 1024), _pick(n, 1024)
    dims = (((0 if ta else 1,), (1 if tb else 0,)), ((), ()))

    def body(a_ref, b_ref, o_ref, acc_ref):
        part = lax.dot_general(a_ref[...], b_ref[...], dims, preferred_element_type=F32)
        if nk == 1:
            o_ref[...] = part.astype(o_ref.dtype)
            return
        kk = pl.program_id(2)

        @pl.when(kk == 0)
        def _():
            acc_ref[...] = part

        @pl.when(kk > 0)
        def _():
            acc_ref[...] += part

        @pl.when(kk == nk - 1)
        def _():
            o_ref[...] = acc_ref[...].astype(o_ref.dtype)

    a_spec = pl.BlockSpec((bk, bm), lambda i, j, kk: (kk, i)) if ta else pl.BlockSpec((bm, bk), lambda i, j, kk: (i, kk))
    b_spec = pl.BlockSpec((bn, bk), lambda i, j, kk: (j, kk)) if tb else pl.BlockSpec((bk, bn), lambda i, j, kk: (kk, j))
    (out,), travelled = _hosted(
        body, comm, name=name, grid=(m // bm, n // bn, nk),
        in_specs=[a_spec, b_spec], out_specs=[pl.BlockSpec((bm, bn), lambda i, j, kk: (i, j))],
        out_shape=[jax.ShapeDtypeStruct((m, n), out_dtype)], scratch_shapes=[pltpu.VMEM((bm, bn), F32)],
        semantics=("parallel", "parallel", "arbitrary"), args=(a, b))
    return out if comm is None else (out, travelled)


def _rowwise(fn, tiled, bcast, outs, reds=(), *, tt, name):
    t = tiled[0][0].shape[0]
    tt = min(tt, t)
    nin, nout = len(tiled) + len(bcast), len(outs)
    splits = [s[3] for s in tiled] + [s[1] for s in bcast]

    def store(ref, val, acc):
        off = 0
        for piece in val if isinstance(val, (tuple, list)) else (val,):
            w = piece.shape[-1]
            if acc:
                ref[:, off:off + w] += piece.astype(ref.dtype)
            else:
                ref[:, off:off + w] = piece.astype(ref.dtype)
            off += w
        assert off == ref.shape[-1], (name, off, ref.shape)

    def body(*refs):
        args = []
        for ref, sp in zip(refs[:nin], splits):
            if sp is None:
                args.append(ref[...])
            else:
                off = 0
                for w in sp:
                    args.append(ref[:, off:off + w])
                    off += w
        res = fn(*args)
        res = res if isinstance(res, (tuple, list)) else (res,)
        assert len(res) == nout + len(reds), (name, len(res))
        for ref, val in zip(refs[nin:nin + nout], res[:nout]):
            store(ref, val, False)
        if reds:
            @pl.when(pl.program_id(0) == 0)
            def _():
                for ref in refs[nin + nout:]:
                    ref[...] = jnp.zeros(ref.shape, ref.dtype)
            for ref, val in zip(refs[nin + nout:], res[nout:]):
                store(ref, val, True)

    in_specs = [pl.BlockSpec((tt, w), functools.partial(lambda i, cb: (i, cb), cb=cb)) for (_, w, cb, _) in tiled]
    in_specs += [pl.BlockSpec(arr.shape, lambda i: (0, 0)) for (arr, _) in bcast]
    out_specs = [pl.BlockSpec((tt, w), lambda i: (i, 0)) for (w, _) in outs]
    out_specs += [pl.BlockSpec((r, w), lambda i: (0, 0)) for (r, w) in reds]
    out_shape = [jax.ShapeDtypeStruct((t, w), dt) for (w, dt) in outs] + [jax.ShapeDtypeStruct((r, w), F32) for (r, w) in reds]
    return pl.pallas_call(
        body, name=name, grid=(t // tt,), in_specs=in_specs, out_specs=out_specs, out_shape=out_shape,
        compiler_params=pltpu.CompilerParams(dimension_semantics=("arbitrary" if reds else "parallel",), vmem_limit_bytes=VMEM_LIMIT),
    )(*[s[0] for s in tiled], *[s[0] for s in bcast])


def _full(arr):
    return (arr, arr.shape[1], 0, None)


def _f(x):
    return x.astype(F32)


def _rms(x, g):
    return x * lax.rsqrt(jnp.mean(x * x, axis=-1, keepdims=True) + EPS) * g


def _log_sigmoid(x):
    return jnp.minimum(x, 0.0) - jnp.log1p(jnp.exp(-jnp.abs(x)))


def _silu(x):
    return x * jax.nn.sigmoid(x)


def _norm_fwd(x, g, name):
    return _rowwise(lambda xb, gb: _rms(_f(xb), gb), [_full(x)], [(g, None)], [(x.shape[1], BF16)], tt=256, name=name)[0]


def _resnorm_fwd(res, branch, g, coef, name):
    def fn(rb, bb, gb):
        h = rb + coef * _f(bb)
        return h, _rms(h, gb)
    d = res.shape[1]
    return _rowwise(fn, [_full(res), _full(branch)], [(g, None)], [(d, F32), (d, BF16)], tt=256, name=name)


def _norm_bwd(h, dns, dres, g, coef, name):
    nd = len(dns)

    def fn(hb, *rest):
        dn = _f(rest[0])
        for extra in rest[1:nd]:
            dn = dn + _f(extra)
        dr, gb = rest[nd], rest[nd + 1]
        _, vjp = jax.vjp(_rms, hb, gb)
        dh, dg = vjp(dn)
        dh = dh + dr
        return dh, coef * dh, dg
    d = h.shape[1]
    return _rowwise(fn, [_full(h)] + [_full(x) for x in dns] + [_full(dres)], [(g, None)],
                    [(d, F32), (d, BF16)], [(1, d)], tt=256, name=name)


def _act_fwd(gu, name):
    ff = gu.shape[1] // 2
    return _rowwise(lambda gb, ub: _silu(_f(gb)) * _f(ub), [(gu, 2 * ff, 0, (ff, ff))], [], [(ff, BF16)], tt=256, name=name)[0]


def _act_bwd(gu, da, name):
    ff = gu.shape[1] // 2

    def fn(gb, ub, dab):
        _, vjp = jax.vjp(lambda p, q: _silu(p) * q, _f(gb), _f(ub))
        return (vjp(_f(dab)),)
    return _rowwise(fn, [(gu, 2 * ff, 0, (ff, ff)), _full(da)], [], [(2 * ff, BF16)], tt=128, name=name)[0]


def _merge(glf, glg, bf, bg, bmf, bmg):
    return jax.nn.sigmoid(_f(glf) + bmf) * _f(bf) + jax.nn.sigmoid(_f(glg) + bmg) * _f(bg)


def _merge_fwd(z, bf, bg, bm, name):
    d = D_MODEL
    return _rowwise(_merge, [(z, d, Z_GL // d, None), (z, d, Z_GL // d + 1, None), _full(bf), _full(bg)], [(bm, (d, d))],
                    [(d, BF16)], tt=256, name=name)[0]


def _merge_bwd(z, bf, bg, bm, dm, name):
    d = D_MODEL

    def fn(glf, glg, bfb, bgb, dmb, bmf, bmg):
        _, vjp = jax.vjp(_merge, _f(glf), _f(glg), _f(bfb), _f(bgb), bmf, bmg)
        dglf, dglg, dbf, dbg, dbmf, dbmg = vjp(_f(dmb))
        return (dglf, dglg), dbf, dbg, (dbmf, dbmg)
    return _rowwise(fn, [(z, d, Z_GL // d, None), (z, d, Z_GL // d + 1, None), _full(bf), _full(bg), _full(dm)], [(bm, (d, d))],
                    [(2 * d, BF16), (d, BF16), (d, BF16)], [(1, 2 * d)], tt=128, name=name)


def _gla_out(o, gr, g):
    return _rms(o, g) * _silu(_f(gr))


_PER_HEAD = (GLA_VAL_DIM,) * GLA_HEADS


def _gla_out_fwd(o, z, g, name):
    nh = GLA_HEADS

    def fn(*blocks):
        return (tuple(_gla_out(blocks[h], blocks[nh + h], blocks[2 * nh]) for h in range(nh)),)
    return _rowwise(fn, [(o, GLA_VW, 0, _PER_HEAD), (z, GLA_VW, Z_GR // GLA_VW, _PER_HEAD)], [(g, None)], [(GLA_VW, BF16)], tt=256, name=name)[0]


def _gla_out_bwd(o, z, g, dy, name):
    nh = GLA_HEADS

    def fn(*blocks):
        gb = blocks[3 * nh]
        grads = []
        for h in range(nh):
            _, vjp = jax.vjp(_gla_out, blocks[h], _f(blocks[nh + h]), gb)
            grads.append(vjp(_f(blocks[2 * nh + h])))
        dg = grads[0][2]
        for h in range(1, nh):
            dg = dg + grads[h][2]
        return tuple(gr[0] for gr in grads), tuple(gr[1] for gr in grads), dg
    return _rowwise(fn, [(o, GLA_VW, 0, _PER_HEAD), (z, GLA_VW, Z_GR // GLA_VW, _PER_HEAD), (dy, GLA_VW, 0, _PER_HEAD)], [(g, None)],
                    [(GLA_VW, F32), (GLA_VW, BF16)], [(1, GLA_VAL_DIM)], tt=256, name=name)


def _small_gates(s, fb, gup, gb):
    lane = lax.broadcasted_iota(jnp.int32, s.shape, 1)
    lf = jnp.where(lane < FOX_HEADS, _log_sigmoid(s + fb), 0.0)
    pre = jnp.dot(s.astype(BF16), gup.astype(BF16), preferred_element_type=F32) + gb
    return lf, _log_sigmoid(pre) / GLA_TAU


def _small_fwd(s, fb, gup, gb, name):
    return _rowwise(_small_gates, [_full(s)], [(fb, None), (gup, None), (gb, None)], [(SMALL_W, F32), (GLA_KW, F32)], tt=256, name=name)


def _small_bwd(s, fb, gup, gb, dlf, dla, name):
    def fn(sb, dlfb, dlab, fbb, gupb, gbb):
        _, vjp = jax.vjp(_small_gates, sb, fbb, gupb, gbb)
        return vjp((dlfb, dlab))
    return _rowwise(fn, [_full(s), _full(dlf), _full(dla)], [(fb, None), (gup, None), (gb, None)],
                    [(SMALL_W, BF16)], [(1, SMALL_W), (SMALL_W, GLA_KW), (1, GLA_KW)], tt=256, name=name)


def _head_fn(h3, pgl, pp, tgt, gf):
    h4 = h3 + jax.nn.sigmoid(pgl) * pp
    err = _rms(h4, gf) - tgt
    return 0.5 * jnp.sum(jnp.mean(err * err, axis=-1, keepdims=True))


def _head(h3, pgl, pp, tgt, gf, name):
    def fn(hb, gl, pb, tb, gfb):
        loss, vjp = jax.vjp(_head_fn, hb, _f(gl), _f(pb), tb, gfb)
        dh, dgl, dpp, _, dgf = vjp(jnp.ones((), F32))
        return dh, dgl, dpp, jnp.full((1, 128), loss, F32), dgf
    d = h3.shape[1]
    return _rowwise(fn, [_full(h3), _full(pgl), _full(pp), _full(tgt)], [(gf, None)],
                    [(d, F32), (d, BF16), (d, BF16)], [(1, 128), (1, d)], tt=256, name=name)


def _cumsum_tokens(a, reverse, name):
    t, w = a.shape
    r = min(256, t)
    nb = t // r

    def body(a_ref, o_ref, carry_ref):
        @pl.when(pl.program_id(0) == 0)
        def _():
            carry_ref[...] = jnp.zeros(carry_ref.shape, F32)
        row = lax.broadcasted_iota(jnp.int32, (r, r), 0)
        col = lax.broadcasted_iota(jnp.int32, (r, r), 1)
        tri = ((col >= row) if reverse else (col <= row)).astype(F32)
        blk = a_ref[...]
        o_ref[...] = jnp.dot(tri, blk, preferred_element_type=F32, precision=lax.Precision.HIGHEST) + carry_ref[...]
        carry_ref[...] += jnp.sum(blk, axis=0, keepdims=True)

    idx = (lambda i: (nb - 1 - i, 0)) if reverse else (lambda i: (i, 0))
    return pl.pallas_call(
        body, name=name, grid=(nb,), in_specs=[pl.BlockSpec((r, w), idx)], out_specs=pl.BlockSpec((r, w), idx),
        out_shape=jax.ShapeDtypeStruct((t, w), F32), scratch_shapes=[pltpu.VMEM((1, w), F32)],
        compiler_params=pltpu.CompilerParams(dimension_semantics=("arbitrary",)),
    )(a)


FOX_TQ, FOX_TK = 512, 1024
FOX_SCALE = HEAD_DIM ** -0.5


def _fox_tiles(t):
    tq, tk = min(FOX_TQ, t), min(FOX_TK, t)
    return tq, tk, t // tq, t // tk


def _blocked_t(a, blk):
    return a.reshape(a.shape[0] // blk, blk, a.shape[1]).transpose(0, 2, 1)


def _unblocked_t(b):
    return b.transpose(0, 2, 1).reshape(b.shape[0] * b.shape[2], b.shape[1])


def _fox_scores(k, qt, frep, i, j, masked):
    tk, tq = k.shape[0], qt.shape[1]
    st = jnp.dot(k, qt, preferred_element_type=F32) * FOX_SCALE - jnp.tile(frep, (1, tq // HEAD_DIM))
    if masked:
        key = j * tk + lax.broadcasted_iota(jnp.int32, (tk, tq), 0)
        query = i * tq + lax.broadcasted_iota(jnp.int32, (tk, tq), 1)
        st = jnp.where(key <= query, st, NEG)
    return st


def _fox_fwd(z, qt, vt, frep, name, comm=None):
    t = z.shape[0]
    tq, tk, nq, nk = _fox_tiles(t)
    kb = Z_FK // HEAD_DIM

    def body(qt_ref, k_ref, vt_ref, frep_ref, ot_ref, lse_ref):
        i = pl.program_id(1)
        qt = qt_ref[...]
        last = ((i + 1) * tq - 1) // tk

        def block(j, carry, masked):
            m, l, acc = carry
            rows = pl.ds(pl.multiple_of(j * tk, tk), tk)
            st = _fox_scores(k_ref[rows, :], qt, frep_ref[rows, :], i, j, masked)
            m_new = jnp.maximum(m, jnp.max(st, axis=0, keepdims=True))
            alpha = jnp.exp(m - m_new)
            p = jnp.exp(st - m_new)
            l = alpha * l + jnp.sum(p, axis=0, keepdims=True)
            acc = alpha * acc + jnp.dot(vt_ref[j], p.astype(BF16), preferred_element_type=F32)
            return m_new, l, acc

        init = (jnp.full((1, tq), NEG, F32), jnp.zeros((1, tq), F32), jnp.zeros((HEAD_DIM, tq), F32))
        m, l, acc = block(last, lax.fori_loop(0, last, lambda j, c: block(j, c, False), init), True)
        ot_ref[...] = (acc / l).astype(ot_ref.dtype)
        lse_ref[...] = m + jnp.log(l)

    stat = pl.BlockSpec((None, None, 1, tq), lambda h, i: (h, i, 0, 0))
    (ot, lse), travelled = _hosted(
        body, comm, name=name, grid=(FOX_HEADS, nq),
        in_specs=[pl.BlockSpec((None, HEAD_DIM, tq), lambda h, i: (i, h, 0)),
                  pl.BlockSpec((t, HEAD_DIM), lambda h, i: (0, kb + h)),
                  pl.BlockSpec((nk, HEAD_DIM, tk), lambda h, i: (0, h, 0)),
                  pl.BlockSpec((None, t, HEAD_DIM), lambda h, i: (h, 0, 0))],
        out_specs=[pl.BlockSpec((None, HEAD_DIM, tq), lambda h, i: (i, h, 0)), stat],
        out_shape=[jax.ShapeDtypeStruct((nq, FOX_W, tq), BF16), jax.ShapeDtypeStruct((FOX_HEADS, nq, 1, tq), F32)],
        scratch_shapes=[], semantics=("parallel", "parallel"), args=(qt, z, vt, frep))
    return ot, lse, travelled


def _fox_bwd_q(z, qt, kt, ot, dot, lse, frep, name):
    t = z.shape[0]
    tq, tk, nq, nk = _fox_tiles(t)
    kb, vb = Z_FK // HEAD_DIM, Z_FV // HEAD_DIM

    def body(qt_ref, k_ref, kt_ref, v_ref, ot_ref, dot_ref, lse_ref, frep_ref, dqt_ref, delta_ref, dfq_ref):
        i = pl.program_id(1)
        qt, dot = qt_ref[...], dot_ref[...]
        lse = lse_ref[...]
        delta = jnp.sum(_f(dot) * _f(ot_ref[...]), axis=0, keepdims=True)
        delta_ref[...] = delta
        last = ((i + 1) * tq - 1) // tk

        def block(j, carry, masked):
            dq, dfq = carry
            rows = pl.ds(pl.multiple_of(j * tk, tk), tk)
            p = jnp.exp(_fox_scores(k_ref[rows, :], qt, frep_ref[rows, :], i, j, masked) - lse)
            dp = jnp.dot(v_ref[rows, :], dot, preferred_element_type=F32)
            ds = p * (dp - delta)
            return dq + jnp.dot(kt_ref[j], ds.astype(BF16), preferred_element_type=F32), dfq + jnp.sum(ds, axis=0, keepdims=True)

        init = (jnp.zeros((HEAD_DIM, tq), F32), jnp.zeros((1, tq), F32))
        dq, dfq = block(last, lax.fori_loop(0, last, lambda j, c: block(j, c, False), init), True)
        dqt_ref[...] = (dq * FOX_SCALE).astype(dqt_ref.dtype)
        dfq_ref[...] = dfq

    mine = pl.BlockSpec((None, HEAD_DIM, tq), lambda h, i: (i, h, 0))
    stat = pl.BlockSpec((None, None, 1, tq), lambda h, i: (h, i, 0, 0))
    return pl.pallas_call(
        body, name=name, grid=(FOX_HEADS, nq),
        in_specs=[mine,
                  pl.BlockSpec((t, HEAD_DIM), lambda h, i: (0, kb + h)),
                  pl.BlockSpec((nk, HEAD_DIM, tk), lambda h, i: (0, h, 0)),
                  pl.BlockSpec((t, HEAD_DIM), lambda h, i: (0, vb + h)),
                  mine, mine, stat,
                  pl.BlockSpec((None, t, HEAD_DIM), lambda h, i: (h, 0, 0))],
        out_specs=[mine, stat, stat],
        out_shape=[jax.ShapeDtypeStruct((nq, FOX_W, tq), BF16), jax.ShapeDtypeStruct((FOX_HEADS, nq, 1, tq), F32),
                   jax.ShapeDtypeStruct((FOX_HEADS, nq, 1, tq), F32)],
        compiler_params=pltpu.CompilerParams(dimension_semantics=("parallel", "parallel"), vmem_limit_bytes=VMEM_LIMIT),
    )(qt, z, kt, z, ot, dot, lse, frep)


def _fox_bwd_kv(z, qt, do, dot, lse, delta, frep, name):
    t = z.shape[0]
    tq, tk, nq, nk = _fox_tiles(t)
    qb, kb, vb = Z_FQ // HEAD_DIM, Z_FK // HEAD_DIM, Z_FV // HEAD_DIM
    per = tk // tq

    def body(k_ref, v_ref, frep_ref, q_ref, qt_ref, do_ref, dot_ref, lse_ref, delta_ref, dk_ref, dv_ref, dfk_ref):
        j = pl.program_id(1)
        k, v, frep = k_ref[...], v_ref[...], frep_ref[...]

        def block(i, carry, masked):
            dk, dv, dfk = carry
            rows = pl.ds(pl.multiple_of(i * tq, tq), tq)
            p = jnp.exp(_fox_scores(k, qt_ref[i], frep, i, j, masked) - lse_ref[i])
            dv = dv + jnp.dot(p.astype(BF16), do_ref[rows, :], preferred_element_type=F32)
            dp = jnp.dot(v, dot_ref[i], preferred_element_type=F32)
            ds = p * (dp - delta_ref[i])
            dk = dk + jnp.dot(ds.astype(BF16), q_ref[rows, :], preferred_element_type=F32)
            for part in range(tq // HEAD_DIM):
                dfk = dfk + ds[:, part * HEAD_DIM:(part + 1) * HEAD_DIM]
            return dk, dv, dfk

        zero = jnp.zeros((tk, HEAD_DIM), F32)
        carry = (zero, zero, zero)
        for step in range(per):
            carry = block(j * per + step, carry, True)
        dk, dv, dfk = lax.fori_loop((j + 1) * per, nq, lambda i, c: block(i, c, False), carry)
        dk_ref[...] = (dk * FOX_SCALE).astype(dk_ref.dtype)
        dv_ref[...] = dv.astype(dv_ref.dtype)
        dfk_ref[...] = jnp.sum(dfk, axis=1, keepdims=True)

    whole_t = pl.BlockSpec((nq, HEAD_DIM, tq), lambda h, j: (0, h, 0))
    whole_stat = pl.BlockSpec((None, nq, 1, tq), lambda h, j: (h, 0, 0, 0))
    return pl.pallas_call(
        body, name=name, grid=(FOX_HEADS, nk),
        in_specs=[pl.BlockSpec((tk, HEAD_DIM), lambda h, j: (j, kb + h)),
                  pl.BlockSpec((tk, HEAD_DIM), lambda h, j: (j, vb + h)),
                  pl.BlockSpec((None, tk, HEAD_DIM), lambda h, j: (h, j, 0)),
                  pl.BlockSpec((t, HEAD_DIM), lambda h, j: (0, qb + h)),
                  whole_t,
                  pl.BlockSpec((t, HEAD_DIM), lambda h, j: (0, h)),
                  whole_t, whole_stat, whole_stat],
        out_specs=[pl.BlockSpec((tk, HEAD_DIM), lambda h, j: (j, h)), pl.BlockSpec((tk, HEAD_DIM), lambda h, j: (j, h)),
                   pl.BlockSpec((None, tk, 1), lambda h, j: (h, j, 0))],
        out_shape=[jax.ShapeDtypeStruct((t, FOX_W), BF16), jax.ShapeDtypeStruct((t, FOX_W), BF16),
                   jax.ShapeDtypeStruct((FOX_HEADS, t, 1), F32)],
        compiler_params=pltpu.CompilerParams(dimension_semantics=("parallel", "parallel"), vmem_limit_bytes=VMEM_LIMIT),
    )(z, z, frep, z, qt, do, dot, lse, delta)


def _gla_step(st, q, k, v, la):
    row = lax.broadcasted_iota(jnp.int32, (CHUNK, CHUNK), 0)
    col = lax.broadcasted_iota(jnp.int32, (CHUNK, CHUNK), 1)
    tri = (col <= row).astype(F32)
    a_cum = jnp.dot(tri, la, preferred_element_type=F32, precision=lax.Precision.HIGHEST)
    a_tot = jnp.sum(la, axis=0, keepdims=True)
    k_dec = (_f(k) * jnp.exp(a_tot - a_cum)).astype(BF16)
    qs = (_f(q) * (HEAD_DIM ** -0.5)).astype(BF16)
    st = st * jnp.exp(a_tot) + lax.dot_general(v.astype(BF16), k_dec, (((0,), (0,)), ((), ())), preferred_element_type=F32)
    o = lax.dot_general(qs, st.astype(BF16), (((1,), (1,)), ((), ())), preferred_element_type=F32)
    return st, o


def _gla_blocks(t):
    r = min(256, t)
    return r, t // r, r // CHUNK


def _gla_fwd(z, la, name):
    t = z.shape[0]
    r, nb, nch = _gla_blocks(t)

    def body(q_ref, k_ref, v_ref, la_ref, o_ref, sp_ref, st_ref):
        @pl.when(pl.program_id(0) == 0)
        def _():
            st_ref[...] = jnp.zeros(st_ref.shape, F32)
        for c in range(nch):
            rows = slice(c * CHUNK, (c + 1) * CHUNK)
            for h in range(GLA_HEADS):
                kc = slice(h * HEAD_DIM, (h + 1) * HEAD_DIM)
                vc = slice(h * GLA_VAL_DIM, (h + 1) * GLA_VAL_DIM)
                st = st_ref[h]
                sp_ref[c, h] = st
                st, o = _gla_step(st, q_ref[rows, kc], k_ref[rows, kc], v_ref[rows, vc], la_ref[rows, kc])
                st_ref[h] = st
                o_ref[rows, vc] = o

    return pl.pallas_call(
        body, name=name, grid=(nb,),
        in_specs=[pl.BlockSpec((r, GLA_KW), lambda i: (i, Z_GQ // GLA_KW)), pl.BlockSpec((r, GLA_KW), lambda i: (i, Z_GK // GLA_KW)),
                  pl.BlockSpec((r, GLA_VW), lambda i: (i, Z_GV // GLA_VW)), pl.BlockSpec((r, GLA_KW), lambda i: (i, 0))],
        out_specs=[pl.BlockSpec((r, GLA_VW), lambda i: (i, 0)),
                   pl.BlockSpec((nch, GLA_HEADS, GLA_VAL_DIM, HEAD_DIM), lambda i: (i, 0, 0, 0))],
        out_shape=[jax.ShapeDtypeStruct((t, GLA_VW), F32),
                   jax.ShapeDtypeStruct((t // CHUNK, GLA_HEADS, GLA_VAL_DIM, HEAD_DIM), F32)],
        scratch_shapes=[pltpu.VMEM((GLA_HEADS, GLA_VAL_DIM, HEAD_DIM), F32)],
        compiler_params=pltpu.CompilerParams(dimension_semantics=("arbitrary",), vmem_limit_bytes=VMEM_LIMIT),
    )(z, z, z, la)


def _gla_bwd(z, la, sprev, do, name):
    t = z.shape[0]
    r, nb, nch = _gla_blocks(t)

    def body(q_ref, k_ref, v_ref, la_ref, sp_ref, do_ref, dq_ref, dk_ref, dv_ref, dla_ref, dst_ref):
        @pl.when(pl.program_id(0) == 0)
        def _():
            dst_ref[...] = jnp.zeros(dst_ref.shape, F32)
        for c in reversed(range(nch)):
            rows = slice(c * CHUNK, (c + 1) * CHUNK)
            for h in range(GLA_HEADS):
                kc = slice(h * HEAD_DIM, (h + 1) * HEAD_DIM)
                vc = slice(h * GLA_VAL_DIM, (h + 1) * GLA_VAL_DIM)
                _, vjp = jax.vjp(_gla_step, sp_ref[c, h], q_ref[rows, kc], k_ref[rows, kc], v_ref[rows, vc], la_ref[rows, kc])
                dst, dq, dk, dv, dla = vjp((dst_ref[h], do_ref[rows, vc]))
                dst_ref[h] = dst
                dq_ref[rows, kc] = dq
                dk_ref[rows, kc] = dk
                dv_ref[rows, vc] = dv
                dla_ref[rows, kc] = dla

    rev = lambda i: (nb - 1 - i, 0)
    return pl.pallas_call(
        body, name=name, grid=(nb,),
        in_specs=[pl.BlockSpec((r, GLA_KW), lambda i: (nb - 1 - i, Z_GQ // GLA_KW)), pl.BlockSpec((r, GLA_KW), lambda i: (nb - 1 - i, Z_GK // GLA_KW)),
                  pl.BlockSpec((r, GLA_VW), lambda i: (nb - 1 - i, Z_GV // GLA_VW)), pl.BlockSpec((r, GLA_KW), rev),
                  pl.BlockSpec((nch, GLA_HEADS, GLA_VAL_DIM, HEAD_DIM), lambda i: (nb - 1 - i, 0, 0, 0)),
                  pl.BlockSpec((r, GLA_VW), rev)],
        out_specs=[pl.BlockSpec((r, GLA_KW), rev), pl.BlockSpec((r, GLA_KW), rev), pl.BlockSpec((r, GLA_VW), rev), pl.BlockSpec((r, GLA_KW), rev)],
        out_shape=[jax.ShapeDtypeStruct((t, GLA_KW), BF16), jax.ShapeDtypeStruct((t, GLA_KW), BF16),
                   jax.ShapeDtypeStruct((t, GLA_VW), BF16), jax.ShapeDtypeStruct((t, GLA_KW), F32)],
        scratch_shapes=[pltpu.VMEM((GLA_HEADS, GLA_VAL_DIM, HEAD_DIM), F32)],
        compiler_params=pltpu.CompilerParams(dimension_semantics=("arbitrary",), vmem_limit_bytes=VMEM_LIMIT),
    )(z, z, z, la, sprev, do)


def _local_step(x, p, tgt, shards, sp, cidx, chip):
    t = x.shape[0]
    tq, tk, _, _ = _fox_tiles(t)
    full, reduced = {}, {}

    def plan(names):
        keys, shapes, places = [], [], []
        for n in names:
            r, cc = shards[n].shape
            key, part, parts = FUSED.get(n, (n, 0, 1))
            if key not in keys:
                keys.append(key)
                stacked = n in ROW_SHARDED or n == "w_in"
                shapes.append(jax.ShapeDtypeStruct((4 * r, cc) if stacked else (r, 4 * cc * parts), shards[n].dtype))
            places.append((keys.index(key), r, 0, 0) if n in ROW_SHARDED or n == "w_in" else (keys.index(key), 0, part * 4 * cc, cc))
        return keys, shapes, places

    def gather(names):
        _, shapes, places = plan(names)
        return _ag_comm([shards[n] for n in names], shapes, places)

    def landed(names, got):
        keys, _, _ = plan(names)
        for key, g in zip(keys, got):
            full[key] = g

    def pair_sums(grads):
        parts, entries = [], []
        for g, names in grads:
            cols = g.shape[1] // (4 * len(names))
            if names[0] in ROW_SHARDED or names[0] == "w_in":
                parts.append(g if g.ndim == 3 else _shard_parts(g, names[0]))
                entries.append((names[0], len(parts) - 1, "slot", 0, parts[-1].shape[2]))
            else:
                parts.append(g[None])
                entries += [(n, len(parts) - 1, "cols", k * 4 * cols, cols) for k, n in enumerate(names)]
        swapped = _pair_swap(parts, "pair_swap")
        return entries, [_sum_half(a, b, cidx, "sum_half") for a, b in zip(parts, swapped)]

    def exchange(entries, sums):
        return _exchange_comm(sums, [e[1:] for e in entries])

    def exchanged(entries, sums, got):
        for (n, si, mode, first, _), g in zip(entries, got):
            reduced[n] = (g, sums[si], mode, first)

    first = ["ffn1_w_gate", "ffn1_w_up"]
    landed(first, _run_comm(gather(first), "all_gather"))
    w_gu1 = full["gu1"]
    n1 = _norm_fwd(x, sp["ffn1_norm"], "norm1_fwd")
    names = ["ffn1_w_down", "w_merge_gate", "gla_gate_up"]
    gu1, got = _mm(n1, w_gu1, name="mm_gu_gather", comm=gather(names))
    landed(names, got)
    a1 = _act_fwd(gu1, "act_fwd")
    names = ["w_in"]
    f1, got = _mm(a1, full["ffn1_w_down"], out_dtype=F32, name="mm_down_gather", comm=gather(names))
    landed(names, got)
    w_big, w_sm = _in_layout(full["w_in"], full["w_merge_gate"])
    gup = jnp.zeros((SMALL_W, GLA_KW), F32).at[FOX_HEADS:FOX_HEADS + GLA_RANK].set(full["gla_gate_up"])
    h1, u = _resnorm_fwd(x, f1, sp["mix_norm"], 0.5, "resnorm_fwd_half")
    names = ["w_branch_fox", "w_branch_gla", "w_out", "w_ple_proj", "w_ple_gate"]
    z, got = _mm(u, w_big, name="mm_in_gather", comm=gather(names))
    landed(names, got)
    s = _mm(u, w_sm, out_dtype=F32, name="mm_in_small")
    lf, la = _small_fwd(s, sp["fb"], gup, sp["gb"], "small_fwd")
    fp = _cumsum_tokens(lf, False, "cumsum_fwd")
    frep = jnp.broadcast_to(fp[:, :FOX_HEADS].T[:, :, None], (FOX_HEADS, t, HEAD_DIM))
    qt = _blocked_t(z[:, Z_FQ:Z_FQ + FOX_W], tq)
    kt = _blocked_t(z[:, Z_FK:Z_FK + FOX_W], tk)
    vt = _blocked_t(z[:, Z_FV:Z_FV + FOX_W], tk)
    names = ["ffn2_w_gate", "ffn2_w_up"]
    ot, lse, got = _fox_fwd(z, qt, vt, frep, "fox_fwd_gather", comm=gather(names))
    landed(names, got)
    w_gu2 = full["gu2"]
    y_fox = _unblocked_t(ot)
    o_gla, sprev = _gla_fwd(z, la, "gla_fwd")
    y_gla = _gla_out_fwd(o_gla, z, sp["ghn"], "gla_out_fwd")
    bf = _mm(y_fox, full["w_branch_fox"], name="mm_branch")
    bg = _mm(y_gla, full["w_branch_gla"], name="mm_branch")
    merged = _merge_fwd(z, bf, bg, sp["bm"], "merge_fwd")
    mo = _mm(merged, full["w_out"], out_dtype=F32, name="mm_out")
    h2, n2 = _resnorm_fwd(h1, mo, sp["ffn2_norm"], 1.0, "resnorm_fwd_one")
    names = ["ffn2_w_down"]
    gu2, got = _mm(n2, w_gu2, name="mm_gu_gather_down", comm=gather(names))
    landed(names, got)
    a2 = _act_fwd(gu2, "act_fwd")
    f2 = _mm(a2, full["ffn2_w_down"], out_dtype=F32, name="mm_down")
    h3, n4 = _resnorm_fwd(h2, f2, sp["ple_norm"], 0.5, "resnorm_fwd_half")
    pgl = _mm(n4, full["w_ple_gate"], name="mm_pg")
    pb = p.astype(BF16)
    pp = _mm(pb, full["w_ple_proj"], name="mm_pp")

    dh3, dpgl, dpp, loss, d_final = _head(h3, pgl, pp, tgt, sp["final_norm"], "head")
    ds_ = {"final_norm": d_final}
    entries, sums = pair_sums([(_mm(n4, dpgl, ta=True, name="mm_dw_sq"), ["w_ple_gate"]), (_mm(pb, dpp, ta=True, name="mm_dw_pp"), ["w_ple_proj"])])
    dn4, got = _mm(dpgl, full["w_ple_gate"], tb=True, out_dtype=F32, name="mm_dx_sq_f32_exchange", comm=exchange(entries, sums))
    exchanged(entries, sums, got)
    dh3, df2, ds_["ple_norm"] = _norm_bwd(h3, [dn4], dh3, sp["ple_norm"], 0.5, "norm_bwd_1")

    def ffn_bwd(n, gu, a, df, wgu, wd, which):
        entries, sums = pair_sums([(_mm(a, df, ta=True, name="mm_dw_down"), [which + "_w_down"])])
        da, got = _mm(df, wd, tb=True, name="mm_dx_down_exchange", comm=exchange(entries, sums))
        exchanged(entries, sums, got)
        dgu = _act_bwd(gu, da, "act_bwd")
        entries, sums = pair_sums([(_mm(n, dgu, ta=True, name="mm_dw_gu"), [which + "_w_gate", which + "_w_up"])])
        dn, got = _mm(dgu, wgu, tb=True, out_dtype=F32, name="mm_dx_gu_exchange", comm=exchange(entries, sums))
        exchanged(entries, sums, got)
        return dn

    dn2 = ffn_bwd(n2, gu2, a2, df2, w_gu2, full["ffn2_w_down"], "ffn2")
    dh2, dmix, ds_["ffn2_norm"] = _norm_bwd(h2, [dn2], dh3, sp["ffn2_norm"], 1.0, "norm_bwd_1")

    dw_out = _mm(merged, dmix, ta=True, name="mm_dw_sq")
    dmerged = _mm(dmix, full["w_out"], tb=True, name="mm_dx_sq")
    dgl, dbf, dbg, ds_["bm"] = _merge_bwd(z, bf, bg, sp["bm"], dmerged, "merge_bwd")
    mix_entries, mix_sums = pair_sums([(dw_out, ["w_out"]), (_mm(y_fox, dbf, ta=True, name="mm_dw_branch"), ["w_branch_fox"]),
                                       (_mm(y_gla, dbg, ta=True, name="mm_dw_branch"), ["w_branch_gla"])])
    dy_fox = _mm(dbf, full["w_branch_fox"], tb=True, name="mm_dx_branch")
    dy_gla = _mm(dbg, full["w_branch_gla"], tb=True, name="mm_dx_branch")

    do_gla, dgr, ds_["ghn"] = _gla_out_bwd(o_gla, z, sp["ghn"], dy_gla, "gla_out_bwd")
    dgq, dgk, dgv, dla = _gla_bwd(z, la, sprev, do_gla, "gla_bwd")
    dot = _blocked_t(dy_fox, tq)
    dqt, delta, df_query = _fox_bwd_q(z, qt, kt, ot, dot, lse, frep, "fox_bwd_q")
    dfq = _unblocked_t(dqt)
    dfk, dfv, df_key = _fox_bwd_kv(z, qt, dy_fox, dot, lse, delta, frep, "fox_bwd_kv")
    df = df_query.reshape(FOX_HEADS, t) - df_key.reshape(FOX_HEADS, t)
    dfp = jnp.pad(df.T, ((0, 0), (0, SMALL_W - FOX_HEADS)))
    dlf = _cumsum_tokens(dfp, True, "cumsum_bwd")
    dsm, ds_["fb"], dgup, ds_["gb"] = _small_bwd(s, sp["fb"], gup, sp["gb"], dlf, dla, "small_bwd")
    dz = jnp.concatenate([dfq, dfk, dfv, dgq, dgk, dgv, dgr, dgl], axis=1)
    dw_big, got = _mm(u, dz, ta=True, name="mm_dw_in_exchange", comm=exchange(mix_entries, mix_sums))
    exchanged(mix_entries, mix_sums, got)
    dw_sm = _mm(u, dsm, ta=True, out_dtype=F32, name="mm_dw_in_small").astype(BF16)
    entries, sums = pair_sums([(_w_in_parts(dw_big, dw_sm), ["w_in"]), (dw_big[:, Z_GL:], ["w_merge_gate"]), (dgup[FOX_HEADS:FOX_HEADS + GLA_RANK], ["gla_gate_up"])])
    du1, got = _mm(dz, w_big, tb=True, out_dtype=F32, name="mm_dx_in_exchange", comm=exchange(entries, sums))
    exchanged(entries, sums, got)
    du2 = _mm(dsm, w_sm, tb=True, out_dtype=F32, name="mm_dx_in_small")
    dh1, df1, ds_["mix_norm"] = _norm_bwd(h1, [du1, du2], dh2, sp["mix_norm"], 0.5, "norm_bwd_2")

    dn1 = ffn_bwd(n1, gu1, a1, df1, w_gu1, full["ffn1_w_down"], "ffn1")
    grad_x, _, ds_["ffn1_norm"] = _norm_bwd(x, [dn1], dh1, sp["ffn1_norm"], 1.0, "norm_bwd_1")
    return loss, grad_x, reduced, ds_


def _half_rows(ref, which):
    r2 = ref.shape[0] // 2
    return ref.at[pl.ds(pl.multiple_of(which * r2, r2), r2)]


class _Comm:
    def __init__(self, ins, out_shape, sems, start, finish):
        self.ins, self.out_shape, self.sems, self.start, self.finish = ins, out_shape, sems, start, finish


def _run_comm(comm, name):
    n_in, n_out = len(comm.ins), len(comm.out_shape)

    def body(*refs):
        parts = refs[:n_in], refs[n_in:n_in + n_out], refs[n_in + n_out:]
        comm.start(*parts)
        comm.finish(*parts)

    return pl.pallas_call(
        body, name=name, in_specs=[ANY] * n_in, out_specs=[ANY] * n_out, out_shape=comm.out_shape,
        scratch_shapes=comm.sems, compiler_params=pltpu.CompilerParams(has_side_effects=True),
    )(*comm.ins)


def _hosted(body, comm, *, name, grid, in_specs, out_specs, out_shape, scratch_shapes, semantics, args):
    if comm is None:
        res = pl.pallas_call(
            body, name=name, grid=grid, in_specs=in_specs, out_specs=out_specs, out_shape=out_shape, scratch_shapes=scratch_shapes,
            compiler_params=pltpu.CompilerParams(dimension_semantics=semantics, vmem_limit_bytes=VMEM_LIMIT),
        )(*args)
        return res, None
    ni, no, ns = len(in_specs), len(out_shape), len(scratch_shapes)
    ci, co = len(comm.ins), len(comm.out_shape)

    def wrapped(*refs):
        h_in, c_in = refs[:ni], refs[ni:ni + ci]
        h_out, c_out = refs[ni + ci:ni + ci + no], refs[ni + ci + no:ni + ci + no + co]
        h_scr, c_sem = refs[ni + ci + no + co:ni + ci + no + co + ns], refs[ni + ci + no + co + ns:]
        ids = [pl.program_id(axis) for axis in range(len(grid))]
        first = functools.reduce(jnp.logical_and, [i == 0 for i in ids])
        last = functools.reduce(jnp.logical_and, [i == g - 1 for i, g in zip(ids, grid)])

        @pl.when(first)
        def _():
            comm.start(c_in, c_out, c_sem)

        body(*h_in, *h_out, *h_scr)

        @pl.when(last)
        def _():
            comm.finish(c_in, c_out, c_sem)

    res = pl.pallas_call(
        wrapped, name=name, grid=grid, in_specs=list(in_specs) + [ANY] * ci, out_specs=list(out_specs) + [ANY] * co,
        out_shape=list(out_shape) + list(comm.out_shape), scratch_shapes=list(scratch_shapes) + list(comm.sems),
        compiler_params=pltpu.CompilerParams(dimension_semantics=("arbitrary",) * len(grid), vmem_limit_bytes=VMEM_LIMIT, has_side_effects=True),
    )(*args, *comm.ins)
    return res[:no], res[no:]


def _ag_comm(shards, out_shape, places):
    n = len(shards)

    def copies(ins, outs, sems):
        ici_send, ici_recv, d2d_send, d2d_recv = sems
        x, y, c = lax.axis_index("x"), lax.axis_index("y"), lax.axis_index("c")
        chips = [(1 - x, y), (x, 1 - y), (1 - x, 1 - y)]
        slot = lambda chip: 2 * chip[0] + chip[1]

        def window(wi, origin, half):
            out, row_step, col_base, col_step = places[wi]
            r, cc = shards[wi].shape
            rows = pl.ds(pl.multiple_of(slot(origin) * row_step + half * (r // 2), r // 2), r // 2)
            cols = pl.ds(pl.multiple_of(col_base + slot(origin) * col_step, HEAD_DIM), cc) if col_step else pl.ds(col_base, cc)
            return outs[out].at[rows, cols]

        def over_ici(wi, j, origin):
            return pltpu.make_async_remote_copy(
                src_ref=_half_rows(ins[wi], c), dst_ref=window(wi, origin, c),
                send_sem=ici_send.at[3 * wi + j], recv_sem=ici_recv.at[3 * wi + j],
                device_id=(chips[j][0], chips[j][1], c), device_id_type=MESH)

        def over_d2d(wi, j, half):
            place = window(wi, chips[j], half)
            return pltpu.make_async_remote_copy(
                src_ref=place, dst_ref=place, send_sem=d2d_send.at[3 * wi + j], recv_sem=d2d_recv.at[3 * wi + j],
                device_id=(x, y, 1 - c), device_id_type=MESH)

        return over_ici, over_d2d, (x, y), chips, c

    def chunk_rows(wi):
        r, cc = shards[wi].shape
        item = shards[wi].dtype.itemsize
        return _pick(r, max(32 // item, BOUNCE_BYTES // (cc * item)), 32 // item)

    def start(ins, outs, scratch):
        over_ici, _, me, _, _ = copies(ins, outs, scratch[:4])
        for wi in range(n):
            for j in range(3):
                over_ici(wi, j, me).start()
        loc_sems = scratch[4]
        for wi in range(n):
            out, row_step, col_base, col_step = places[wi]
            r, cc = shards[wi].shape
            rc = chunk_rows(wi)
            buf = scratch[5 + wi]
            slot = 2 * me[0] + me[1]
            cols = pl.ds(pl.multiple_of(col_base + slot * col_step, HEAD_DIM), cc) if col_step else pl.ds(col_base, cc)

            def load(k):
                return pltpu.make_async_copy(ins[wi].at[pl.ds(k * rc, rc)], buf.at[k % 2], loc_sems.at[2 * wi])

            def store(k):
                rows = pl.ds(pl.multiple_of(slot * row_step + k * rc, rc), rc)
                return pltpu.make_async_copy(buf.at[k % 2], outs[out].at[rows, cols], loc_sems.at[2 * wi + 1])

            load(0).start()
            for k in range(r // rc):
                load(k).wait()
                if k + 1 < r // rc:
                    load(k + 1).start()
                store(k).start()
                store(k).wait()

    def finish(ins, outs, scratch):
        over_ici, over_d2d, me, chips, c = copies(ins, outs, scratch[:4])
        for wi in range(n):
            for j in range(3):
                over_ici(wi, j, chips[j]).wait_recv()
                over_d2d(wi, j, c).start()
        for wi in range(n):
            for j in range(3):
                over_d2d(wi, j, 1 - c).wait_recv()
        for wi in range(n):
            for j in range(3):
                over_ici(wi, j, me).wait_send()
                over_d2d(wi, j, c).wait_send()

    bounce = [pltpu.VMEM((min(2, s.shape[0] // chunk_rows(wi)), chunk_rows(wi), s.shape[1]), s.dtype) for wi, s in enumerate(shards)]
    return _Comm(list(shards), list(out_shape), [pltpu.SemaphoreType.DMA((3 * n,))] * 4 + [pltpu.SemaphoreType.DMA((2 * n,))] + bounce,
                 start, finish)


def _pair_swap(parts, name):
    n = len(parts)

    def body(*refs):
        ins, outs = refs[:n], refs[n:2 * n]
        send_sems, recv_sems = refs[2 * n:]
        x, y, c = lax.axis_index("x"), lax.axis_index("y"), lax.axis_index("c")

        def swap(wi):
            r2 = parts[wi].shape[1] // 2
            return pltpu.make_async_remote_copy(
                src_ref=ins[wi].at[:, pl.ds(pl.multiple_of((1 - c) * r2, r2), r2)], dst_ref=outs[wi],
                send_sem=send_sems.at[wi], recv_sem=recv_sems.at[wi], device_id=(x, y, 1 - c), device_id_type=MESH)

        copies = [swap(wi) for wi in range(n)]
        for cp in copies:
            cp.start()
        for cp in copies:
            cp.wait()

    return pl.pallas_call(
        body, name=name, in_specs=[ANY] * n, out_specs=[ANY] * n,
        out_shape=[jax.ShapeDtypeStruct((s.shape[0], s.shape[1] // 2, s.shape[2]), s.dtype) for s in parts],
        scratch_shapes=[pltpu.SemaphoreType.DMA((n,))] * 2, compiler_params=pltpu.CompilerParams(has_side_effects=True),
    )(*parts)


def _row_tile(r, c, budget=1 << 19):
    return r if r <= 8 else _pick(r, max(8, budget // c), 8)


def _sum_half(parts, other, cidx, name):
    nl, r, cc = parts.shape
    r2 = r // 2
    tr = _row_tile(r2, cc)

    def body(c_ref, p_ref, q_ref, o_ref):
        o_ref[...] = (_f(p_ref[...]) + _f(q_ref[...])).astype(o_ref.dtype)

    return pl.pallas_call(
        body, name=name, out_shape=jax.ShapeDtypeStruct((nl, r2, cc), parts.dtype),
        grid_spec=pltpu.PrefetchScalarGridSpec(
            num_scalar_prefetch=1, grid=(nl, r2 // tr),
            in_specs=[pl.BlockSpec((None, None, tr, cc), lambda l, i, c_ref: (l, c_ref[0], i, 0)),
                      pl.BlockSpec((None, tr, cc), lambda l, i, c_ref: (l, i, 0))],
            out_specs=pl.BlockSpec((None, tr, cc), lambda l, i, c_ref: (l, i, 0))),
        compiler_params=pltpu.CompilerParams(dimension_semantics=("parallel", "parallel"), vmem_limit_bytes=VMEM_LIMIT),
    )(cidx, parts.reshape(nl, 2, r2, cc), other)


def _exchange_comm(sums, entries):
    n = len(entries)

    def copies(ins, outs, sems):
        send_sems, recv_sems = sems
        x, y, c = lax.axis_index("x"), lax.axis_index("y"), lax.axis_index("c")
        chips = [(1 - x, y), (x, 1 - y), (1 - x, 1 - y)]
        slot = lambda chip: 2 * chip[0] + chip[1]

        def piece(wi, dest):
            si, mode, first, cols = entries[wi]
            if mode == "cols":
                return ins[si].at[0, :, pl.ds(pl.multiple_of(first + slot(dest) * cols, HEAD_DIM), cols)]
            return ins[si].at[slot(dest) if mode == "slot" else 0]

        def remote(wi, j, origin):
            return pltpu.make_async_remote_copy(
                src_ref=piece(wi, chips[j]), dst_ref=outs[wi].at[slot(origin)],
                send_sem=send_sems.at[3 * wi + j], recv_sem=recv_sems.at[3 * wi + j],
                device_id=(chips[j][0], chips[j][1], c), device_id_type=MESH)

        return remote, (x, y), chips

    def start(ins, outs, sems):
        remote, me, _ = copies(ins, outs, sems)
        for wi in range(n):
            for j in range(3):
                remote(wi, j, me).start()

    def finish(ins, outs, sems):
        remote, me, chips = copies(ins, outs, sems)
        for wi in range(n):
            for j in range(3):
                remote(wi, j, chips[j]).wait_recv()
        for wi in range(n):
            for j in range(3):
                remote(wi, j, me).wait_send()

    out_shape = [jax.ShapeDtypeStruct((4, sums[si].shape[1], cols), sums[si].dtype) for si, _, _, cols in entries]
    return _Comm(list(sums), out_shape, [pltpu.SemaphoreType.DMA((3 * n,))] * 2, start, finish)


def _sum_chips(got, own, mode, first, chip, name):
    _, r2, cc = got.shape
    tr = _row_tile(r2, cc)
    own_block = {"slot": lambda i, chip_ref: (chip_ref[0], i, 0), "same": lambda i, chip_ref: (0, i, 0),
                 "cols": lambda i, chip_ref: (0, i, first // cc + chip_ref[0])}[mode]

    def body(chip_ref, g_ref, own_ref, o_ref):
        term = lambda k: jnp.where(chip_ref[0] == k, _f(own_ref[...]), _f(g_ref[k]))
        o_ref[...] = ((term(0) + term(1)) + term(2)) + term(3)

    return pl.pallas_call(
        body, name=name, out_shape=jax.ShapeDtypeStruct((r2, cc), F32),
        grid_spec=pltpu.PrefetchScalarGridSpec(
            num_scalar_prefetch=1, grid=(r2 // tr,),
            in_specs=[pl.BlockSpec((4, tr, cc), lambda i, chip_ref: (0, i, 0)),
                      pl.BlockSpec((None, tr, cc), own_block)],
            out_specs=pl.BlockSpec((tr, cc), lambda i, chip_ref: (i, 0))),
        compiler_params=pltpu.CompilerParams(dimension_semantics=("parallel",), vmem_limit_bytes=VMEM_LIMIT),
    )(chip, got, own)


def _pair_gather(halves, name):
    n = len(halves)

    def body(*refs):
        ins, outs = refs[:n], refs[n:2 * n]
        send_sems, recv_sems = refs[2 * n:]
        x, y, c = lax.axis_index("x"), lax.axis_index("y"), lax.axis_index("c")
        copies = [pltpu.make_async_remote_copy(
            src_ref=ins[wi], dst_ref=outs[wi], send_sem=send_sems.at[wi], recv_sem=recv_sems.at[wi],
            device_id=(x, y, 1 - c), device_id_type=MESH) for wi in range(n)]
        for cp in copies:
            cp.start()
        for cp in copies:
            cp.wait()

    return pl.pallas_call(
        body, name=name, in_specs=[ANY] * n, out_specs=[ANY] * n,
        out_shape=[jax.ShapeDtypeStruct(s.shape, s.dtype) for s in halves],
        scratch_shapes=[pltpu.SemaphoreType.DMA((n,))] * 2, compiler_params=pltpu.CompilerParams(has_side_effects=True),
    )(*halves)


def _adamw_update(g, w, m, v):
    m_new = ADAM_B1 * m + (1.0 - ADAM_B1) * g
    v_new = ADAM_B2 * v + (1.0 - ADAM_B2) * jnp.square(g)
    m_hat = m_new / (1.0 - ADAM_B1 ** ADAM_STEP)
    v_hat = v_new / (1.0 - ADAM_B2 ** ADAM_STEP)
    return -ADAM_LR * (m_hat / (jnp.sqrt(v_hat) + ADAM_EPS) + ADAM_WD * w), m_new, v_new


def _adamw_whole(g, w, m, v, name):
    r, c = w.shape
    tc = 256

    def body(g_ref, w_ref, m_ref, v_ref, d_ref, nm_ref, nv_ref):
        d_ref[...], nm_ref[...], nv_ref[...] = _adamw_update(g_ref[...], w_ref[...], m_ref[...], v_ref[...])

    blk = pl.BlockSpec((r, tc), lambda i: (0, i))
    return pl.pallas_call(
        body, name=name, grid=(c // tc,), in_specs=[blk] * 4, out_specs=[blk] * 3, out_shape=[jax.ShapeDtypeStruct((r, c), F32)] * 3,
        compiler_params=pltpu.CompilerParams(dimension_semantics=("parallel",), vmem_limit_bytes=VMEM_LIMIT),
    )(g, w, m, v)


def _adamw(mine, other, cidx, w, m, v, name):
    r, c = w.shape
    tr = _row_tile(r // 2, c, 1 << 18)
    nh = (r // 2) // tr

    def body(c_ref, mine_ref, other_ref, w_ref, m_ref, v_ref, g_ref, d_ref, nm_ref, nv_ref):
        g = jnp.where(pl.program_id(0) // nh == c_ref[0], mine_ref[...], other_ref[...])
        g_ref[...] = g
        d_ref[...], nm_ref[...], nv_ref[...] = _adamw_update(g, w_ref[...], m_ref[...], v_ref[...])

    blk = pl.BlockSpec((tr, c), lambda i, c_ref: (i, 0))
    mine_spec = pl.BlockSpec((tr, c), lambda i, c_ref: (jnp.where(i // nh == c_ref[0], i % nh, 0), 0))
    other_spec = pl.BlockSpec((tr, c), lambda i, c_ref: (jnp.where(i // nh == c_ref[0], 0, i % nh), 0))
    return pl.pallas_call(
        body, name=name, out_shape=[jax.ShapeDtypeStruct((r, c), F32)] * 4,
        grid_spec=pltpu.PrefetchScalarGridSpec(
            num_scalar_prefetch=1, grid=(r // tr,), in_specs=[mine_spec, other_spec, blk, blk, blk], out_specs=[blk] * 4),
        compiler_params=pltpu.CompilerParams(dimension_semantics=("arbitrary",), vmem_limit_bytes=VMEM_LIMIT),
    )(cidx, mine, other, w, m, v)


BIG = ["ffn1_w_gate", "ffn1_w_up", "ffn1_w_down", "w_in", "gla_gate_up", "w_branch_fox", "w_branch_gla", "w_merge_gate", "w_out",
       "ffn2_w_gate", "ffn2_w_up", "ffn2_w_down", "w_ple_proj", "w_ple_gate"]
ROW_SHARDED = ("ffn1_w_down", "w_out", "ffn2_w_down", "w_ple_gate")
FUSED = {"ffn1_w_gate": ("gu1", 0, 2), "ffn1_w_up": ("gu1", 1, 2), "ffn2_w_gate": ("gu2", 0, 2), "ffn2_w_up": ("gu2", 1, 2)}
SMALL = ["ffn1_norm", "mix_norm", "fox_forget_bias", "gla_gate_bias", "gla_head_norm", "b_merge_gate", "ffn2_norm", "ple_norm", "final_norm"]
NAMES = ["ffn1_norm", "ffn1_w_gate", "ffn1_w_up", "ffn1_w_down", "mix_norm", "w_in", "fox_forget_bias", "gla_gate_up", "gla_gate_bias",
         "gla_head_norm", "w_branch_fox", "w_branch_gla", "w_merge_gate", "b_merge_gate", "w_out", "ffn2_norm", "ffn2_w_gate", "ffn2_w_up",
         "ffn2_w_down", "ple_norm", "w_ple_proj", "w_ple_gate", "final_norm"]
W_IN_COLS = (FOX_W, FOX_W, FOX_W, FOX_HEADS, GLA_KW, GLA_KW, GLA_VW, GLA_VW, GLA_RANK)
SMALL_ROWS, SMALL_COLS = 16, 1024


def _shard_parts(full, name):
    assert name in ROW_SHARDED, name
    return full.reshape(4, full.shape[0] // 4, full.shape[1])


W_IN_WIDE = ((0, 3 * FOX_W), (3 * FOX_W + FOX_HEADS, 3 * FOX_W + FOX_HEADS + 2 * GLA_KW + 2 * GLA_VW))
W_IN_NARROW = ((3 * FOX_W, 3 * FOX_W + FOX_HEADS), (sum(W_IN_COLS) - GLA_RANK, sum(W_IN_COLS)))


def _in_layout(stacked, w_merge_gate):
    per = stacked.shape[1]

    def columns(lo, hi):
        out = []
        while lo < hi:
            j, end = lo // per, min(hi, (lo // per + 1) * per)
            out.append(stacked[j * D_MODEL:(j + 1) * D_MODEL, lo - j * per:end - j * per])
            lo = end
        return out
    big = jnp.concatenate(columns(*W_IN_WIDE[0]) + columns(*W_IN_WIDE[1]) + [w_merge_gate], axis=1)
    sm = jnp.concatenate(columns(*W_IN_NARROW[0]) + columns(*W_IN_NARROW[1]) + [jnp.zeros((D_MODEL, SMALL_W - FOX_HEADS - GLA_RANK), BF16)], axis=1)
    return big, sm


def _w_in_parts(dw_big, dw_sm):
    runs = [(W_IN_WIDE[0], dw_big, Z_FQ), (W_IN_NARROW[0], dw_sm, 0), (W_IN_WIDE[1], dw_big, Z_GQ), (W_IN_NARROW[1], dw_sm, FOX_HEADS)]
    per = sum(W_IN_COLS) // 4
    parts = []
    for j in range(4):
        lo, hi, pieces = j * per, (j + 1) * per, []
        for (a, b), src, at in runs:
            if max(a, lo) < min(b, hi):
                pieces.append(src[:, at + max(a, lo) - a:at + min(b, hi) - a])
        parts.append(jnp.concatenate(pieces, axis=1))
    return jnp.stack(parts)


def _pad_lanes(a, width):
    return jnp.pad(a, ((0, 0), (0, width - a.shape[1])))


def kernel(x, p, ffn1_norm, ffn1_w_gate, ffn1_w_up, ffn1_w_down, mix_norm, w_in, fox_forget_bias, gla_gate_up, gla_gate_bias, gla_head_norm, w_branch_fox, w_branch_gla, w_merge_gate, b_merge_gate, w_out, ffn2_norm, ffn2_w_gate, ffn2_w_up, ffn2_w_down, ple_norm, w_ple_proj, w_ple_gate, final_norm, loss_target, m_ffn1_norm, m_ffn1_w_gate, m_ffn1_w_up, m_ffn1_w_down, m_mix_norm, m_w_in, m_fox_forget_bias, m_gla_gate_up, m_gla_gate_bias, m_gla_head_norm, m_w_branch_fox, m_w_branch_gla, m_w_merge_gate, m_b_merge_gate, m_w_out, m_ffn2_norm, m_ffn2_w_gate, m_ffn2_w_up, m_ffn2_w_down, m_ple_norm, m_w_ple_proj, m_w_ple_gate, m_final_norm, v_ffn1_norm, v_ffn1_w_gate, v_ffn1_w_up, v_ffn1_w_down, v_mix_norm, v_w_in, v_fox_forget_bias, v_gla_gate_up, v_gla_gate_bias, v_gla_head_norm, v_w_branch_fox, v_w_branch_gla, v_w_merge_gate, v_b_merge_gate, v_w_out, v_ffn2_norm, v_ffn2_w_gate, v_ffn2_w_up, v_ffn2_w_down, v_ple_norm, v_w_ple_proj, v_w_ple_gate, v_final_norm):
    args = dict(locals())
    wts = {n: args[n] for n in NAMES}
    mom = {n: args["m_" + n] for n in NAMES}
    var = {n: args["v_" + n] for n in NAMES}
    two_d = lambda a: a.reshape(-1, a.shape[-1])

    wire = lambda n: F32 if n == "gla_gate_up" else BF16
    cidx = lax.axis_index("c").astype(jnp.int32).reshape(1)
    chip = (2 * lax.axis_index("x") + lax.axis_index("y")).astype(jnp.int32)
    shards = {n: two_d(wts[n]).astype(wire(n)) for n in BIG}
    sp = {
        "ffn1_norm": two_d(ffn1_norm), "mix_norm": two_d(mix_norm), "fb": _pad_lanes(two_d(fox_forget_bias), SMALL_W),
        "gb": two_d(gla_gate_bias), "ghn": two_d(gla_head_norm), "bm": two_d(b_merge_gate), "ffn2_norm": two_d(ffn2_norm),
        "ple_norm": two_d(ple_norm), "final_norm": two_d(final_norm),
    }

    loss, grad_x, reduced, ds_ = _local_step(x[0], p[0, 0], loss_target[0], shards, sp, cidx, chip)

    small_g = {"ffn1_norm": ds_["ffn1_norm"], "mix_norm": ds_["mix_norm"], "fox_forget_bias": ds_["fb"][:, :FOX_HEADS],
               "gla_gate_bias": ds_["gb"], "gla_head_norm": ds_["ghn"], "b_merge_gate": ds_["bm"], "ffn2_norm": ds_["ffn2_norm"],
               "ple_norm": ds_["ple_norm"], "final_norm": ds_["final_norm"]}
    small_w = sum(two_d(wts[n]).shape[1] for n in SMALL)
    assert small_w <= SMALL_ROWS * SMALL_COLS
    packed = lambda d: _pad_lanes(jnp.concatenate([two_d(d[n]) for n in SMALL], axis=1), SMALL_ROWS * SMALL_COLS).reshape(SMALL_ROWS, SMALL_COLS)
    parts = [packed(small_g)[None]]
    pair_sums = [_sum_half(a, b, cidx, "sum_half") for a, b in zip(parts, _pair_swap(parts, "pair_swap"))]
    reduced["small"] = (_run_comm(_exchange_comm(pair_sums, [(0, "same", 0, SMALL_COLS)]), "chip_exchange")[0], pair_sums[0], "same", 0)
    mine = [_sum_chips(*reduced[n], chip.reshape(1), "sum_chips") for n in BIG + ["small"]]
    other = _pair_gather(mine, "pair_gather")

    out = {}
    for n, a, b in zip(BIG, mine[:-1], other[:-1]):
        if n == "w_in":
            g_t = jnp.where(cidx[0] == 0, jnp.concatenate([a, b]), jnp.concatenate([b, a])).T
            flip = lambda t: jnp.swapaxes(t, 1, 2)[0]
            res = _adamw_whole(g_t, flip(wts[n]), flip(mom[n]), flip(var[n]), "adamw_w_in")
            out[n] = [jnp.swapaxes(r[None], 1, 2) for r in [g_t, *res]]
            continue
        out[n] = [r.reshape(wts[n].shape) for r in _adamw(a, b, cidx, two_d(wts[n]), two_d(mom[n]), two_d(var[n]), "adamw_" + n)]
    small_out = [r.reshape(1, SMALL_ROWS * SMALL_COLS) for r in _adamw(mine[-1], other[-1], cidx, packed(wts), packed(mom), packed(var), "adamw_small")]
    off = 0
    for n in SMALL:
        cw = two_d(wts[n]).shape[1]
        out[n] = [r[:, off:off + cw].reshape(wts[n].shape) for r in small_out]
        off += cw

    total = lax.psum(loss[0, 0], ("x", "y", "c"))
    return (total, grad_x[None], *[out[n][0] for n in NAMES], *[out[n][1] for n in NAMES],
            *[out[n][2] for n in NAMES], *[out[n][3] for n in NAMES])
```

```python
import functools

import jax
import jax.numpy as jnp
from jax import lax
from jax.experimental import pallas as pl
from jax.experimental.pallas import tpu as pltpu

F32 = jnp.float32
BF16 = jnp.bfloat16
MESH = pl.DeviceIdType.MESH
ANY = pl.BlockSpec(memory_space=pl.ANY)

D_MODEL = 2048
FOX_HEADS = 8
HEAD_DIM = 128
GLA_HEADS = 4
GLA_VAL_DIM = 256
GLA_RANK = 16
GLA_TAU = 16.0
CHUNK = 64
EPS = 1e-6
FOX_W = FOX_HEADS * HEAD_DIM
GLA_KW = GLA_HEADS * HEAD_DIM
GLA_VW = GLA_HEADS * GLA_VAL_DIM
Z_FQ, Z_FK, Z_FV, Z_GQ, Z_GK, Z_GV, Z_GR, Z_GL = 0, 1024, 2048, 3072, 3584, 4096, 5120, 6144
Z_W = Z_GL + 2 * D_MODEL
SMALL_W = 128
NEG = -1e30

ADAM_LR, ADAM_B1, ADAM_B2, ADAM_EPS, ADAM_WD, ADAM_STEP = 0.001, 0.9, 0.999, 1e-08, 0.01, 10

VMEM_LIMIT = 56 * 1024 * 1024
BOUNCE_BYTES = 2 * 1024 * 1024


def _pick(n, target, mult=128):
    if n <= target:
        return n
    best = None
    for d in range(mult, target + 1, mult):
        if n % d == 0:
            best = d
    assert best is not None, (n, target)
    return best


def _mm(a, b, *, ta=False, tb=False, out_dtype=BF16, name, comm=None):
    m, k = (a.shape[1], a.shape[0]) if ta else a.shape
    n = b.shape[0] if tb else b.shape[1]
    assert (b.shape[1] if tb else b.shape[0]) == k
    bk = _pick(k, 4096)
    nk = k // bk
    bm, bn = _pick(m, 1024), _pick(n, 1024)
    dims = (((0 if ta else 1,), (1 if tb else 0,)), ((), ()))

    def body(a_ref, b_ref, o_ref, acc_ref):
        part = lax.dot_general(a_ref[...], b_ref[...], dims, preferred_element_type=F32)
        if nk == 1:
            o_ref[...] = part.astype(o_ref.dtype)
            return
        kk = pl.program_id(2)

        @pl.when(kk == 0)
        def _():
            acc_ref[...] = part

        @pl.when(kk > 0)
        def _():
            acc_ref[...] += part

        @pl.when(kk == nk - 1)
        def _():
            o_ref[...] = acc_ref[...].astype(o_ref.dtype)

    a_spec = pl.BlockSpec((bk, bm), lambda i, j, kk: (kk, i)) if ta else pl.BlockSpec((bm, bk), lambda i, j, kk: (i, kk))
    b_spec = pl.BlockSpec((bn, bk), lambda i, j, kk: (j, kk)) if tb else pl.BlockSpec((bk, bn), lambda i, j, kk: (kk, j))
    (out,), travelled = _hosted(
        body, comm, name=name, grid=(m // bm, n // bn, nk),
        in_specs=[a_spec, b_spec], out_specs=[pl.BlockSpec((bm, bn), lambda i, j, kk: (i, j))],
        out_shape=[jax.ShapeDtypeStruct((m, n), out_dtype)], scratch_shapes=[pltpu.VMEM((bm, bn), F32)],
        semantics=("parallel", "parallel", "arbitrary"), args=(a, b))
    return out if comm is None else (out, travelled)


def _rowwise(fn, tiled, bcast, outs, reds=(), *, tt, name):
    t = tiled[0][0].shape[0]
    tt = min(tt, t)
    nin, nout = len(tiled) + len(bcast), len(outs)
    splits = [s[3] for s in tiled] + [s[1] for s in bcast]

    def store(ref, val, acc):
        off = 0
        for piece in val if isinstance(val, (tuple, list)) else (val,):
            w = piece.shape[-1]
            if acc:
                ref[:, off:off + w] += piece.astype(ref.dtype)
            else:
                ref[:, off:off + w] = piece.astype(ref.dtype)
            off += w
        assert off == ref.shape[-1], (name, off, ref.shape)

    def body(*refs):
        args = []
        for ref, sp in zip(refs[:nin], splits):
            if sp is None:
                args.append(ref[...])
            else:
                off = 0
                for w in sp:
                    args.append(ref[:, off:off + w])
                    off += w
        res = fn(*args)
        res = res if isinstance(res, (tuple, list)) else (res,)
        assert len(res) == nout + len(reds), (name, len(res))
        for ref, val in zip(refs[nin:nin + nout], res[:nout]):
            store(ref, val, False)
        if reds:
            @pl.when(pl.program_id(0) == 0)
            def _():
                for ref in refs[nin + nout:]:
                    ref[...] = jnp.zeros(ref.shape, ref.dtype)
            for ref, val in zip(refs[nin + nout:], res[nout:]):
                store(ref, val, True)

    in_specs = [pl.BlockSpec((tt, w), functools.partial(lambda i, cb: (i, cb), cb=cb)) for (_, w, cb, _) in tiled]
    in_specs += [pl.BlockSpec(arr.shape, lambda i: (0, 0)) for (arr, _) in bcast]
    out_specs = [pl.BlockSpec((tt, w), lambda i: (i, 0)) for (w, _) in outs]
    out_specs += [pl.BlockSpec((r, w), lambda i: (0, 0)) for (r, w) in reds]
    out_shape = [jax.ShapeDtypeStruct((t, w), dt) for (w, dt) in outs] + [jax.ShapeDtypeStruct((r, w), F32) for (r, w) in reds]
    return pl.pallas_call(
        body, name=name, grid=(t // tt,), in_specs=in_specs, out_specs=out_specs, out_shape=out_shape,
        compiler_params=pltpu.CompilerParams(dimension_semantics=("arbitrary" if reds else "parallel",), vmem_limit_bytes=VMEM_LIMIT),
    )(*[s[0] for s in tiled], *[s[0] for s in bcast])


def _full(arr):
    return (arr, arr.shape[1], 0, None)


def _f(x):
    return x.astype(F32)


def _rms(x, g):
    return x * lax.rsqrt(jnp.mean(x * x, axis=-1, keepdims=True) + EPS) * g


def _log_sigmoid(x):
    return jnp.minimum(x, 0.0) - jnp.log1p(jnp.exp(-jnp.abs(x)))


def _silu(x):
    return x * jax.nn.sigmoid(x)


def _norm_fwd(x, g, name):
    return _rowwise(lambda xb, gb: _rms(_f(xb), gb), [_full(x)], [(g, None)], [(x.shape[1], BF16)], tt=256, name=name)[0]


def _resnorm_fwd(res, branch, g, coef, name):
    def fn(rb, bb, gb):
        h = rb + coef * _f(bb)
        return h, _rms(h, gb)
    d = res.shape[1]
    return _rowwise(fn, [_full(res), _full(branch)], [(g, None)], [(d, F32), (d, BF16)], tt=256, name=name)


def _norm_bwd(h, dns, dres, g, coef, name):
    nd = len(dns)

    def fn(hb, *rest):
        dn = _f(rest[0])
        for extra in rest[1:nd]:
            dn = dn + _f(extra)
        dr, gb = rest[nd], rest[nd + 1]
        _, vjp = jax.vjp(_rms, hb, gb)
        dh, dg = vjp(dn)
        dh = dh + dr
        return dh, coef * dh, dg
    d = h.shape[1]
    return _rowwise(fn, [_full(h)] + [_full(x) for x in dns] + [_full(dres)], [(g, None)],
                    [(d, F32), (d, BF16)], [(1, d)], tt=256, name=name)


def _act_fwd(gu, name):
    ff = gu.shape[1] // 2
    return _rowwise(lambda gb, ub: _silu(_f(gb)) * _f(ub), [(gu, 2 * ff, 0, (ff, ff))], [], [(ff, BF16)], tt=256, name=name)[0]


def _act_bwd(gu, da, name):
    ff = gu.shape[1] // 2

    def fn(gb, ub, dab):
        _, vjp = jax.vjp(lambda p, q: _silu(p) * q, _f(gb), _f(ub))
        return (vjp(_f(dab)),)
    return _rowwise(fn, [(gu, 2 * ff, 0, (ff, ff)), _full(da)], [], [(2 * ff, BF16)], tt=128, name=name)[0]


def _merge(glf, glg, bf, bg, bmf, bmg):
    return jax.nn.sigmoid(_f(glf) + bmf) * _f(bf) + jax.nn.sigmoid(_f(glg) + bmg) * _f(bg)


def _merge_fwd(z, bf, bg, bm, name):
    d = D_MODEL
    return _rowwise(_merge, [(z, d, Z_GL // d, None), (z, d, Z_GL // d + 1, None), _full(bf), _full(bg)], [(bm, (d, d))],
                    [(d, BF16)], tt=256, name=name)[0]


def _merge_bwd(z, bf, bg, bm, dm, name):
    d = D_MODEL

    def fn(glf, glg, bfb, bgb, dmb, bmf, bmg):
        _, vjp = jax.vjp(_merge, _f(glf), _f(glg), _f(bfb), _f(bgb), bmf, bmg)
        dglf, dglg, dbf, dbg, dbmf, dbmg = vjp(_f(dmb))
        return (dglf, dglg), dbf, dbg, (dbmf, dbmg)
    return _rowwise(fn, [(z, d, Z_GL // d, None), (z, d, Z_GL // d + 1, None), _full(bf), _full(bg), _full(dm)], [(bm, (d, d))],
                    [(2 * d, BF16), (d, BF16), (d, BF16)], [(1, 2 * d)], tt=128, name=name)


def _gla_out(o, gr, g):
    return _rms(o, g) * _silu(_f(gr))


_PER_HEAD = (GLA_VAL_DIM,) * GLA_HEADS


def _gla_out_fwd(o, z, g, name):
    nh = GLA_HEADS

    def fn(*blocks):
        return (tuple(_gla_out(blocks[h], blocks[nh + h], blocks[2 * nh]) for h in range(nh)),)
    return _rowwise(fn, [(o, GLA_VW, 0, _PER_HEAD), (z, GLA_VW, Z_GR // GLA_VW, _PER_HEAD)], [(g, None)], [(GLA_VW, BF16)], tt=256, name=name)[0]


def _gla_out_bwd(o, z, g, dy, name):
    nh = GLA_HEADS

    def fn(*blocks):
        gb = blocks[3 * nh]
        grads = []
        for h in range(nh):
            _, vjp = jax.vjp(_gla_out, blocks[h], _f(blocks[nh + h]), gb)
            grads.append(vjp(_f(blocks[2 * nh + h])))
        dg = grads[0][2]
        for h in range(1, nh):
            dg = dg + grads[h][2]
        return tuple(gr[0] for gr in grads), tuple(gr[1] for gr in grads), dg
    return _rowwise(fn, [(o, GLA_VW, 0, _PER_HEAD), (z, GLA_VW, Z_GR // GLA_VW, _PER_HEAD), (dy, GLA_VW, 0, _PER_HEAD)], [(g, None)],
                    [(GLA_VW, F32), (GLA_VW, BF16)], [(1, GLA_VAL_DIM)], tt=256, name=name)


def _small_gates(s, fb, gup, gb):
    lane = lax.broadcasted_iota(jnp.int32, s.shape, 1)
    lf = jnp.where(lane < FOX_HEADS, _log_sigmoid(s + fb), 0.0)
    pre = jnp.dot(s.astype(BF16), gup.astype(BF16), preferred_element_type=F32) + gb
    return lf, _log_sigmoid(pre) / GLA_TAU


def _small_fwd(s, fb, gup, gb, name):
    return _rowwise(_small_gates, [_full(s)], [(fb, None), (gup, None), (gb, None)], [(SMALL_W, F32), (GLA_KW, F32)], tt=256, name=name)


def _small_bwd(s, fb, gup, gb, dlf, dla, name):
    def fn(sb, dlfb, dlab, fbb, gupb, gbb):
        _, vjp = jax.vjp(_small_gates, sb, fbb, gupb, gbb)
        return vjp((dlfb, dlab))
    return _rowwise(fn, [_full(s), _full(dlf), _full(dla)], [(fb, None), (gup, None), (gb, None)],
                    [(SMALL_W, BF16)], [(1, SMALL_W), (SMALL_W, GLA_KW), (1, GLA_KW)], tt=256, name=name)


def _head_fn(h3, pgl, pp, tgt, gf):
    h4 = h3 + jax.nn.sigmoid(pgl) * pp
    err = _rms(h4, gf) - tgt
    return 0.5 * jnp.sum(jnp.mean(err * err, axis=-1, keepdims=True))


def _head(h3, pgl, pp, tgt, gf, name):
    def fn(hb, gl, pb, tb, gfb):
        loss, vjp = jax.vjp(_head_fn, hb, _f(gl), _f(pb), tb, gfb)
        dh, dgl, dpp, _, dgf = vjp(jnp.ones((), F32))
        return dh, dgl, dpp, jnp.full((1, 128), loss, F32), dgf
    d = h3.shape[1]
    return _rowwise(fn, [_full(h3), _full(pgl), _full(pp), _full(tgt)], [(gf, None)],
                    [(d, F32), (d, BF16), (d, BF16)], [(1, 128), (1, d)], tt=256, name=name)


def _cumsum_tokens(a, reverse, name):
    t, w = a.shape
    r = min(256, t)
    nb = t // r

    def body(a_ref, o_ref, carry_ref):
        @pl.when(pl.program_id(0) == 0)
        def _():
            carry_ref[...] = jnp.zeros(carry_ref.shape, F32)
        row = lax.broadcasted_iota(jnp.int32, (r, r), 0)
        col = lax.broadcasted_iota(jnp.int32, (r, r), 1)
        tri = ((col >= row) if reverse else (col <= row)).astype(F32)
        blk = a_ref[...]
        o_ref[...] = jnp.dot(tri, blk, preferred_element_type=F32, precision=lax.Precision.HIGHEST) + carry_ref[...]
        carry_ref[...] += jnp.sum(blk, axis=0, keepdims=True)

    idx = (lambda i: (nb - 1 - i, 0)) if reverse else (lambda i: (i, 0))
    return pl.pallas_call(
        body, name=name, grid=(nb,), in_specs=[pl.BlockSpec((r, w), idx)], out_specs=pl.BlockSpec((r, w), idx),
        out_shape=jax.ShapeDtypeStruct((t, w), F32), scratch_shapes=[pltpu.VMEM((1, w), F32)],
        compiler_params=pltpu.CompilerParams(dimension_semantics=("arbitrary",)),
    )(a)


FOX_TQ, FOX_TK = 1024, 1024
FOX_SCALE = HEAD_DIM ** -0.5


def _fox_tiles(t):
    tq, tk = min(FOX_TQ, t), min(FOX_TK, t)
    return tq, tk, t // tq, t // tk


def _blocked_t(a, blk):
    return a.reshape(a.shape[0] // blk, blk, a.shape[1]).transpose(0, 2, 1)


def _unblocked_t(b):
    return b.transpose(0, 2, 1).reshape(b.shape[0] * b.shape[2], b.shape[1])


def _fox_scores(k, qt, frep, i, j, masked):
    tk, tq = k.shape[0], qt.shape[1]
    st = jnp.dot(k, qt, preferred_element_type=F32) * FOX_SCALE - jnp.tile(frep, (1, tq // HEAD_DIM))
    if masked:
        key = j * tk + lax.broadcasted_iota(jnp.int32, (tk, tq), 0)
        query = i * tq + lax.broadcasted_iota(jnp.int32, (tk, tq), 1)
        st = jnp.where(key <= query, st, NEG)
    return st


def _fox_fwd(z, qt, vt, frep, name, comm=None):
    t = z.shape[0]
    tq, tk, nq, nk = _fox_tiles(t)
    kb = Z_FK // HEAD_DIM

    def body(qt_ref, k_ref, vt_ref, frep_ref, ot_ref, lse_ref):
        i = pl.program_id(1)
        qt = qt_ref[...]
        last = ((i + 1) * tq - 1) // tk

        def block(j, carry, masked):
            m, l, acc = carry
            rows = pl.ds(pl.multiple_of(j * tk, tk), tk)
            st = _fox_scores(k_ref[rows, :], qt, frep_ref[rows, :], i, j, masked)
            m_new = jnp.maximum(m, jnp.max(st, axis=0, keepdims=True))
            alpha = jnp.exp(m - m_new)
            p = jnp.exp(st - m_new)
            l = alpha * l + jnp.sum(p, axis=0, keepdims=True)
            acc = alpha * acc + jnp.dot(vt_ref[j], p.astype(BF16), preferred_element_type=F32)
            return m_new, l, acc

        init = (jnp.full((1, tq), NEG, F32), jnp.zeros((1, tq), F32), jnp.zeros((HEAD_DIM, tq), F32))
        m, l, acc = block(last, lax.fori_loop(0, last, lambda j, c: block(j, c, False), init), True)
        ot_ref[...] = (acc / l).astype(ot_ref.dtype)
        lse_ref[...] = m + jnp.log(l)

    stat = pl.BlockSpec((None, None, 1, tq), lambda h, i: (h, i, 0, 0))
    (ot, lse), travelled = _hosted(
        body, comm, name=name, grid=(FOX_HEADS, nq),
        in_specs=[pl.BlockSpec((None, HEAD_DIM, tq), lambda h, i: (i, h, 0)),
                  pl.BlockSpec((t, HEAD_DIM), lambda h, i: (0, kb + h)),
                  pl.BlockSpec((nk, HEAD_DIM, tk), lambda h, i: (0, h, 0)),
                  pl.BlockSpec((None, t, HEAD_DIM), lambda h, i: (h, 0, 0))],
        out_specs=[pl.BlockSpec((None, HEAD_DIM, tq), lambda h, i: (i, h, 0)), stat],
        out_shape=[jax.ShapeDtypeStruct((nq, FOX_W, tq), BF16), jax.ShapeDtypeStruct((FOX_HEADS, nq, 1, tq), F32)],
        scratch_shapes=[], semantics=("parallel", "parallel"), args=(qt, z, vt, frep))
    return ot, lse, travelled


def _fox_bwd_q(z, qt, kt, ot, dot, lse, frep, name):
    t = z.shape[0]
    tq, tk, nq, nk = _fox_tiles(t)
    kb, vb = Z_FK // HEAD_DIM, Z_FV // HEAD_DIM

    def body(qt_ref, k_ref, kt_ref, v_ref, ot_ref, dot_ref, lse_ref, frep_ref, dqt_ref, delta_ref, dfq_ref):
        i = pl.program_id(1)
        qt, dot = qt_ref[...], dot_ref[...]
        lse = lse_ref[...]
        delta = jnp.sum(_f(dot) * _f(ot_ref[...]), axis=0, keepdims=True)
        delta_ref[...] = delta
        last = ((i + 1) * tq - 1) // tk

        def block(j, carry, masked):
            dq, dfq = carry
            rows = pl.ds(pl.multiple_of(j * tk, tk), tk)
            p = jnp.exp(_fox_scores(k_ref[rows, :], qt, frep_ref[rows, :], i, j, masked) - lse)
            dp = jnp.dot(v_ref[rows, :], dot, preferred_element_type=F32)
            ds = p * (dp - delta)
            return dq + jnp.dot(kt_ref[j], ds.astype(BF16), preferred_element_type=F32), dfq + jnp.sum(ds, axis=0, keepdims=True)

        init = (jnp.zeros((HEAD_DIM, tq), F32), jnp.zeros((1, tq), F32))
        dq, dfq = block(last, lax.fori_loop(0, last, lambda j, c: block(j, c, False), init), True)
        dqt_ref[...] = (dq * FOX_SCALE).astype(dqt_ref.dtype)
        dfq_ref[...] = dfq

    mine = pl.BlockSpec((None, HEAD_DIM, tq), lambda h, i: (i, h, 0))
    stat = pl.BlockSpec((None, None, 1, tq), lambda h, i: (h, i, 0, 0))
    return pl.pallas_call(
        body, name=name, grid=(FOX_HEADS, nq),
        in_specs=[mine,
                  pl.BlockSpec((t, HEAD_DIM), lambda h, i: (0, kb + h)),
                  pl.BlockSpec((nk, HEAD_DIM, tk), lambda h, i: (0, h, 0)),
                  pl.BlockSpec((t, HEAD_DIM), lambda h, i: (0, vb + h)),
                  mine, mine, stat,
                  pl.BlockSpec((None, t, HEAD_DIM), lambda h, i: (h, 0, 0))],
        out_specs=[mine, stat, stat],
        out_shape=[jax.ShapeDtypeStruct((nq, FOX_W, tq), BF16), jax.ShapeDtypeStruct((FOX_HEADS, nq, 1, tq), F32),
                   jax.ShapeDtypeStruct((FOX_HEADS, nq, 1, tq), F32)],
        compiler_params=pltpu.CompilerParams(dimension_semantics=("parallel", "parallel"), vmem_limit_bytes=VMEM_LIMIT),
    )(qt, z, kt, z, ot, dot, lse, frep)


def _fox_bwd_kv(z, qt, do, dot, lse, delta, frep, name):
    t = z.shape[0]
    tq, tk, nq, nk = _fox_tiles(t)
    qb, kb, vb = Z_FQ // HEAD_DIM, Z_FK // HEAD_DIM, Z_FV // HEAD_DIM
    per = tk // tq

    def body(k_ref, v_ref, frep_ref, q_ref, qt_ref, do_ref, dot_ref, lse_ref, delta_ref, dk_ref, dv_ref, dfk_ref):
        j = pl.program_id(1)
        k, v, frep = k_ref[...], v_ref[...], frep_ref[...]

        def block(i, carry, masked):
            dk, dv, dfk = carry
            rows = pl.ds(pl.multiple_of(i * tq, tq), tq)
            p = jnp.exp(_fox_scores(k, qt_ref[i], frep, i, j, masked) - lse_ref[i])
            dv = dv + jnp.dot(p.astype(BF16), do_ref[rows, :], preferred_element_type=F32)
            dp = jnp.dot(v, dot_ref[i], preferred_element_type=F32)
            ds = p * (dp - delta_ref[i])
            dk = dk + jnp.dot(ds.astype(BF16), q_ref[rows, :], preferred_element_type=F32)
            for part in range(tq // HEAD_DIM):
                dfk = dfk + ds[:, part * HEAD_DIM:(part + 1) * HEAD_DIM]
            return dk, dv, dfk

        zero = jnp.zeros((tk, HEAD_DIM), F32)
        carry = (zero, zero, zero)
        for step in range(per):
            carry = block(j * per + step, carry, True)
        dk, dv, dfk = lax.fori_loop((j + 1) * per, nq, lambda i, c: block(i, c, False), carry)
        dk_ref[...] = (dk * FOX_SCALE).astype(dk_ref.dtype)
        dv_ref[...] = dv.astype(dv_ref.dtype)
        dfk_ref[...] = jnp.sum(dfk, axis=1, keepdims=True)

    whole_t = pl.BlockSpec((nq, HEAD_DIM, tq), lambda h, j: (0, h, 0))
    whole_stat = pl.BlockSpec((None, nq, 1, tq), lambda h, j: (h, 0, 0, 0))
    return pl.pallas_call(
        body, name=name, grid=(FOX_HEADS, nk),
        in_specs=[pl.BlockSpec((tk, HEAD_DIM), lambda h, j: (j, kb + h)),
                  pl.BlockSpec((tk, HEAD_DIM), lambda h, j: (j, vb + h)),
                  pl.BlockSpec((None, tk, HEAD_DIM), lambda h, j: (h, j, 0)),
                  pl.BlockSpec((t, HEAD_DIM), lambda h, j: (0, qb + h)),
                  whole_t,
                  pl.BlockSpec((t, HEAD_DIM), lambda h, j: (0, h)),
                  whole_t, whole_stat, whole_stat],
        out_specs=[pl.BlockSpec((tk, HEAD_DIM), lambda h, j: (j, h)), pl.BlockSpec((tk, HEAD_DIM), lambda h, j: (j, h)),
                   pl.BlockSpec((None, tk, 1), lambda h, j: (h, j, 0))],
        out_shape=[jax.ShapeDtypeStruct((t, FOX_W), BF16), jax.ShapeDtypeStruct((t, FOX_W), BF16),
                   jax.ShapeDtypeStruct((FOX_HEADS, t, 1), F32)],
        compiler_params=pltpu.CompilerParams(dimension_semantics=("parallel", "parallel"), vmem_limit_bytes=VMEM_LIMIT),
    )(z, z, frep, z, qt, do, dot, lse, delta)


def _gla_step(st, q, k, v, la):
    row = lax.broadcasted_iota(jnp.int32, (CHUNK, CHUNK), 0)
    col = lax.broadcasted_iota(jnp.int32, (CHUNK, CHUNK), 1)
    tri = (col <= row).astype(F32)
    a_cum = jnp.dot(tri, la, preferred_element_type=F32, precision=lax.Precision.HIGHEST)
    a_tot = jnp.sum(la, axis=0, keepdims=True)
    k_dec = (_f(k) * jnp.exp(a_tot - a_cum)).astype(BF16)
    qs = (_f(q) * (HEAD_DIM ** -0.5)).astype(BF16)
    st = st * jnp.exp(a_tot) + lax.dot_general(v.astype(BF16), k_dec, (((0,), (0,)), ((), ())), preferred_element_type=F32)
    o = lax.dot_general(qs, st.astype(BF16), (((1,), (1,)), ((), ())), preferred_element_type=F32)
    return st, o


def _gla_blocks(t):
    r = min(256, t)
    return r, t // r, r // CHUNK


def _gla_fwd(z, la, name):
    t = z.shape[0]
    r, nb, nch = _gla_blocks(t)

    def body(q_ref, k_ref, v_ref, la_ref, o_ref, sp_ref, st_ref):
        @pl.when(pl.program_id(0) == 0)
        def _():
            st_ref[...] = jnp.zeros(st_ref.shape, F32)
        for c in range(nch):
            rows = slice(c * CHUNK, (c + 1) * CHUNK)
            for h in range(GLA_HEADS):
                kc = slice(h * HEAD_DIM, (h + 1) * HEAD_DIM)
                vc = slice(h * GLA_VAL_DIM, (h + 1) * GLA_VAL_DIM)
                st = st_ref[h]
                sp_ref[c, h] = st
                st, o = _gla_step(st, q_ref[rows, kc], k_ref[rows, kc], v_ref[rows, vc], la_ref[rows, kc])
                st_ref[h] = st
                o_ref[rows, vc] = o

    return pl.pallas_call(
        body, name=name, grid=(nb,),
        in_specs=[pl.BlockSpec((r, GLA_KW), lambda i: (i, Z_GQ // GLA_KW)), pl.BlockSpec((r, GLA_KW), lambda i: (i, Z_GK // GLA_KW)),
                  pl.BlockSpec((r, GLA_VW), lambda i: (i, Z_GV // GLA_VW)), pl.BlockSpec((r, GLA_KW), lambda i: (i, 0))],
        out_specs=[pl.BlockSpec((r, GLA_VW), lambda i: (i, 0)),
                   pl.BlockSpec((nch, GLA_HEADS, GLA_VAL_DIM, HEAD_DIM), lambda i: (i, 0, 0, 0))],
        out_shape=[jax.ShapeDtypeStruct((t, GLA_VW), F32),
                   jax.ShapeDtypeStruct((t // CHUNK, GLA_HEADS, GLA_VAL_DIM, HEAD_DIM), F32)],
        scratch_shapes=[pltpu.VMEM((GLA_HEADS, GLA_VAL_DIM, HEAD_DIM), F32)],
        compiler_params=pltpu.CompilerParams(dimension_semantics=("arbitrary",), vmem_limit_bytes=VMEM_LIMIT),
    )(z, z, z, la)


def _gla_bwd(z, la, sprev, do, name):
    t = z.shape[0]
    r, nb, nch = _gla_blocks(t)

    def body(q_ref, k_ref, v_ref, la_ref, sp_ref, do_ref, dq_ref, dk_ref, dv_ref, dla_ref, dst_ref):
        @pl.when(pl.program_id(0) == 0)
        def _():
            dst_ref[...] = jnp.zeros(dst_ref.shape, F32)
        for c in reversed(range(nch)):
            rows = slice(c * CHUNK, (c + 1) * CHUNK)
            for h in range(GLA_HEADS):
                kc = slice(h * HEAD_DIM, (h + 1) * HEAD_DIM)
                vc = slice(h * GLA_VAL_DIM, (h + 1) * GLA_VAL_DIM)
                _, vjp = jax.vjp(_gla_step, sp_ref[c, h], q_ref[rows, kc], k_ref[rows, kc], v_ref[rows, vc], la_ref[rows, kc])
                dst, dq, dk, dv, dla = vjp((dst_ref[h], do_ref[rows, vc]))
                dst_ref[h] = dst
                dq_ref[rows, kc] = dq
                dk_ref[rows, kc] = dk
                dv_ref[rows, vc] = dv
                dla_ref[rows, kc] = dla

    rev = lambda i: (nb - 1 - i, 0)
    return pl.pallas_call(
        body, name=name, grid=(nb,),
        in_specs=[pl.BlockSpec((r, GLA_KW), lambda i: (nb - 1 - i, Z_GQ // GLA_KW)), pl.BlockSpec((r, GLA_KW), lambda i: (nb - 1 - i, Z_GK // GLA_KW)),
                  pl.BlockSpec((r, GLA_VW), lambda i: (nb - 1 - i, Z_GV // GLA_VW)), pl.BlockSpec((r, GLA_KW), rev),
                  pl.BlockSpec((nch, GLA_HEADS, GLA_VAL_DIM, HEAD_DIM), lambda i: (nb - 1 - i, 0, 0, 0)),
                  pl.BlockSpec((r, GLA_VW), rev)],
        out_specs=[pl.BlockSpec((r, GLA_KW), rev), pl.BlockSpec((r, GLA_KW), rev), pl.BlockSpec((r, GLA_VW), rev), pl.BlockSpec((r, GLA_KW), rev)],
        out_shape=[jax.ShapeDtypeStruct((t, GLA_KW), BF16), jax.ShapeDtypeStruct((t, GLA_KW), BF16),
                   jax.ShapeDtypeStruct((t, GLA_VW), BF16), jax.ShapeDtypeStruct((t, GLA_KW), F32)],
        scratch_shapes=[pltpu.VMEM((GLA_HEADS, GLA_VAL_DIM, HEAD_DIM), F32)],
        compiler_params=pltpu.CompilerParams(dimension_semantics=("arbitrary",), vmem_limit_bytes=VMEM_LIMIT),
    )(z, z, z, la, sprev, do)


def _local_step(x, p, tgt, shards, sp, cidx, chip):
    t = x.shape[0]
    tq, tk, _, _ = _fox_tiles(t)
    full, reduced = {}, {}

    def plan(names):
        keys, shapes, places = [], [], []
        for n in names:
            r, cc = shards[n].shape
            key, part, parts = FUSED.get(n, (n, 0, 1))
            if key not in keys:
                keys.append(key)
                stacked = n in ROW_SHARDED or n == "w_in"
                shapes.append(jax.ShapeDtypeStruct((4 * r, cc) if stacked else (r, 4 * cc * parts), shards[n].dtype))
            places.append((keys.index(key), r, 0, 0) if n in ROW_SHARDED or n == "w_in" else (keys.index(key), 0, part * 4 * cc, cc))
        return keys, shapes, places

    def gather(names):
        _, shapes, places = plan(names)
        return _ag_comm([shards[n] for n in names], shapes, places)

    def landed(names, got):
        keys, _, _ = plan(names)
        for key, g in zip(keys, got):
            full[key] = g

    def pair_sums(grads):
        parts, entries = [], []
        for g, names in grads:
            cols = g.shape[1] // (4 * len(names))
            if names[0] in ROW_SHARDED or names[0] == "w_in":
                parts.append(g if g.ndim == 3 else _shard_parts(g, names[0]))
                entries.append((names[0], len(parts) - 1, "slot", 0, parts[-1].shape[2]))
            else:
                parts.append(g[None])
                entries += [(n, len(parts) - 1, "cols", k * 4 * cols, cols) for k, n in enumerate(names)]
        swapped = _pair_swap(parts, "pair_swap")
        return entries, [_sum_half(a, b, cidx, "sum_half") for a, b in zip(parts, swapped)]

    def exchange(entries, sums):
        return _exchange_comm(sums, [e[1:] for e in entries])

    def exchanged(entries, sums, got):
        for (n, si, mode, first, _), g in zip(entries, got):
            reduced[n] = (g, sums[si], mode, first)

    first = ["ffn1_w_gate", "ffn1_w_up"]
    landed(first, _run_comm(gather(first), "all_gather"))
    w_gu1 = full["gu1"]
    n1 = _norm_fwd(x, sp["ffn1_norm"], "norm1_fwd")
    names = ["ffn1_w_down", "w_merge_gate", "gla_gate_up"]
    gu1, got = _mm(n1, w_gu1, name="mm_gu_gather", comm=gather(names))
    landed(names, got)
    a1 = _act_fwd(gu1, "act_fwd")
    names = ["w_in"]
    f1, got = _mm(a1, full["ffn1_w_down"], out_dtype=F32, name="mm_down_gather", comm=gather(names))
    landed(names, got)
    w_big, w_sm = _in_layout(full["w_in"], full["w_merge_gate"])
    gup = jnp.zeros((SMALL_W, GLA_KW), F32).at[FOX_HEADS:FOX_HEADS + GLA_RANK].set(full["gla_gate_up"])
    h1, u = _resnorm_fwd(x, f1, sp["mix_norm"], 0.5, "resnorm_fwd_half")
    names = ["w_branch_fox", "w_branch_gla", "w_out", "w_ple_proj", "w_ple_gate"]
    z, got = _mm(u, w_big, name="mm_in_gather", comm=gather(names))
    landed(names, got)
    s = _mm(u, w_sm, out_dtype=F32, name="mm_in_small")
    lf, la = _small_fwd(s, sp["fb"], gup, sp["gb"], "small_fwd")
    fp = _cumsum_tokens(lf, False, "cumsum_fwd")
    frep = jnp.broadcast_to(fp[:, :FOX_HEADS].T[:, :, None], (FOX_HEADS, t, HEAD_DIM))
    qt = _blocked_t(z[:, Z_FQ:Z_FQ + FOX_W], tq)
    kt = _blocked_t(z[:, Z_FK:Z_FK + FOX_W], tk)
    vt = _blocked_t(z[:, Z_FV:Z_FV + FOX_W], tk)
    names = ["ffn2_w_gate", "ffn2_w_up"]
    ot, lse, got = _fox_fwd(z, qt, vt, frep, "fox_fwd_gather", comm=gather(names))
    landed(names, got)
    w_gu2 = full["gu2"]
    y_fox = _unblocked_t(ot)
    o_gla, sprev = _gla_fwd(z, la, "gla_fwd")
    y_gla = _gla_out_fwd(o_gla, z, sp["ghn"], "gla_out_fwd")
    bf = _mm(y_fox, full["w_branch_fox"], name="mm_branch")
    bg = _mm(y_gla, full["w_branch_gla"], name="mm_branch")
    merged = _merge_fwd(z, bf, bg, sp["bm"], "merge_fwd")
    mo = _mm(merged, full["w_out"], out_dtype=F32, name="mm_out")
    h2, n2 = _resnorm_fwd(h1, mo, sp["ffn2_norm"], 1.0, "resnorm_fwd_one")
    names = ["ffn2_w_down"]
    gu2, got = _mm(n2, w_gu2, name="mm_gu_gather_down", comm=gather(names))
    landed(names, got)
    a2 = _act_fwd(gu2, "act_fwd")
    f2 = _mm(a2, full["ffn2_w_down"], out_dtype=F32, name="mm_down")
    h3, n4 = _resnorm_fwd(h2, f2, sp["ple_norm"], 0.5, "resnorm_fwd_half")
    pgl = _mm(n4, full["w_ple_gate"], name="mm_pg")
    pb = p.astype(BF16)
    pp = _mm(pb, full["w_ple_proj"], name="mm_pp")

    dh3, dpgl, dpp, loss, d_final = _head(h3, pgl, pp, tgt, sp["final_norm"], "head")
    ds_ = {"final_norm": d_final}
    entries, sums = pair_sums([(_mm(n4, dpgl, ta=True, name="mm_dw_sq"), ["w_ple_gate"]), (_mm(pb, dpp, ta=True, name="mm_dw_pp"), ["w_ple_proj"])])
    dn4, got = _mm(dpgl, full["w_ple_gate"], tb=True, out_dtype=F32, name="mm_dx_sq_f32_exchange", comm=exchange(entries, sums))
    exchanged(entries, sums, got)
    dh3, df2, ds_["ple_norm"] = _norm_bwd(h3, [dn4], dh3, sp["ple_norm"], 0.5, "norm_bwd_1")

    def ffn_bwd(n, gu, a, df, wgu, wd, which):
        entries, sums = pair_sums([(_mm(a, df, ta=True, name="mm_dw_down"), [which + "_w_down"])])
        da, got = _mm(df, wd, tb=True, name="mm_dx_down_exchange", comm=exchange(entries, sums))
        exchanged(entries, sums, got)
        dgu = _act_bwd(gu, da, "act_bwd")
        entries, sums = pair_sums([(_mm(n, dgu, ta=True, name="mm_dw_gu"), [which + "_w_gate", which + "_w_up"])])
        dn, got = _mm(dgu, wgu, tb=True, out_dtype=F32, name="mm_dx_gu_exchange", comm=exchange(entries, sums))
        exchanged(entries, sums, got)
        return dn

    dn2 = ffn_bwd(n2, gu2, a2, df2, w_gu2, full["ffn2_w_down"], "ffn2")
    dh2, dmix, ds_["ffn2_norm"] = _norm_bwd(h2, [dn2], dh3, sp["ffn2_norm"], 1.0, "norm_bwd_1")

    dw_out = _mm(merged, dmix, ta=True, name="mm_dw_sq")
    dmerged = _mm(dmix, full["w_out"], tb=True, name="mm_dx_sq")
    dgl, dbf, dbg, ds_["bm"] = _merge_bwd(z, bf, bg, sp["bm"], dmerged, "merge_bwd")
    mix_entries, mix_sums = pair_sums([(dw_out, ["w_out"]), (_mm(y_fox, dbf, ta=True, name="mm_dw_branch"), ["w_branch_fox"]),
                                       (_mm(y_gla, dbg, ta=True, name="mm_dw_branch"), ["w_branch_gla"])])
    dy_fox = _mm(dbf, full["w_branch_fox"], tb=True, name="mm_dx_branch")
    dy_gla = _mm(dbg, full["w_branch_gla"], tb=True, name="mm_dx_branch")

    do_gla, dgr, ds_["ghn"] = _gla_out_bwd(o_gla, z, sp["ghn"], dy_gla, "gla_out_bwd")
    dgq, dgk, dgv, dla = _gla_bwd(z, la, sprev, do_gla, "gla_bwd")
    dot = _blocked_t(dy_fox, tq)
    dqt, delta, df_query = _fox_bwd_q(z, qt, kt, ot, dot, lse, frep, "fox_bwd_q")
    dfq = _unblocked_t(dqt)
    dfk, dfv, df_key = _fox_bwd_kv(z, qt, dy_fox, dot, lse, delta, frep, "fox_bwd_kv")
    df = df_query.reshape(FOX_HEADS, t) - df_key.reshape(FOX_HEADS, t)
    dfp = jnp.pad(df.T, ((0, 0), (0, SMALL_W - FOX_HEADS)))
    dlf = _cumsum_tokens(dfp, True, "cumsum_bwd")
    dsm, ds_["fb"], dgup, ds_["gb"] = _small_bwd(s, sp["fb"], gup, sp["gb"], dlf, dla, "small_bwd")
    dz = jnp.concatenate([dfq, dfk, dfv, dgq, dgk, dgv, dgr, dgl], axis=1)
    dw_big, got = _mm(u, dz, ta=True, name="mm_dw_in_exchange", comm=exchange(mix_entries, mix_sums))
    exchanged(mix_entries, mix_sums, got)
    dw_sm = _mm(u, dsm, ta=True, out_dtype=F32, name="mm_dw_in_small").astype(BF16)
    entries, sums = pair_sums([(_w_in_parts(dw_big, dw_sm), ["w_in"]), (dw_big[:, Z_GL:], ["w_merge_gate"]), (dgup[FOX_HEADS:FOX_HEADS + GLA_RANK], ["gla_gate_up"])])
    du1, got = _mm(dz, w_big, tb=True, out_dtype=F32, name="mm_dx_in_exchange", comm=exchange(entries, sums))
    exchanged(entries, sums, got)
    du2 = _mm(dsm, w_sm, tb=True, out_dtype=F32, name="mm_dx_in_small")
    dh1, df1, ds_["mix_norm"] = _norm_bwd(h1, [du1, du2], dh2, sp["mix_norm"], 0.5, "norm_bwd_2")

    dn1 = ffn_bwd(n1, gu1, a1, df1, w_gu1, full["ffn1_w_down"], "ffn1")
    grad_x, _, ds_["ffn1_norm"] = _norm_bwd(x, [dn1], dh1, sp["ffn1_norm"], 1.0, "norm_bwd_1")
    return loss, grad_x, reduced, ds_


def _half_rows(ref, which):
    r2 = ref.shape[0] // 2
    return ref.at[pl.ds(pl.multiple_of(which * r2, r2), r2)]


class _Comm:
    def __init__(self, ins, out_shape, sems, start, finish):
        self.ins, self.out_shape, self.sems, self.start, self.finish = ins, out_shape, sems, start, finish


def _run_comm(comm, name):
    n_in, n_out = len(comm.ins), len(comm.out_shape)

    def body(*refs):
        parts = refs[:n_in], refs[n_in:n_in + n_out], refs[n_in + n_out:]
        comm.start(*parts)
        comm.finish(*parts)

    return pl.pallas_call(
        body, name=name, in_specs=[ANY] * n_in, out_specs=[ANY] * n_out, out_shape=comm.out_shape,
        scratch_shapes=comm.sems, compiler_params=pltpu.CompilerParams(has_side_effects=True),
    )(*comm.ins)


def _hosted(body, comm, *, name, grid, in_specs, out_specs, out_shape, scratch_shapes, semantics, args):
    if comm is None:
        res = pl.pallas_call(
            body, name=name, grid=grid, in_specs=in_specs, out_specs=out_specs, out_shape=out_shape, scratch_shapes=scratch_shapes,
            compiler_params=pltpu.CompilerParams(dimension_semantics=semantics, vmem_limit_bytes=VMEM_LIMIT),
        )(*args)
        return res, None
    ni, no, ns = len(in_specs), len(out_shape), len(scratch_shapes)
    ci, co = len(comm.ins), len(comm.out_shape)

    def wrapped(*refs):
        h_in, c_in = refs[:ni], refs[ni:ni + ci]
        h_out, c_out = refs[ni + ci:ni + ci + no], refs[ni + ci + no:ni + ci + no + co]
        h_scr, c_sem = refs[ni + ci + no + co:ni + ci + no + co + ns], refs[ni + ci + no + co + ns:]
        ids = [pl.program_id(axis) for axis in range(len(grid))]
        first = functools.reduce(jnp.logical_and, [i == 0 for i in ids])
        last = functools.reduce(jnp.logical_and, [i == g - 1 for i, g in zip(ids, grid)])

        @pl.when(first)
        def _():
            comm.start(c_in, c_out, c_sem)

        body(*h_in, *h_out, *h_scr)

        @pl.when(last)
        def _():
            comm.finish(c_in, c_out, c_sem)

    res = pl.pallas_call(
        wrapped, name=name, grid=grid, in_specs=list(in_specs) + [ANY] * ci, out_specs=list(out_specs) + [ANY] * co,
        out_shape=list(out_shape) + list(comm.out_shape), scratch_shapes=list(scratch_shapes) + list(comm.sems),
        compiler_params=pltpu.CompilerParams(dimension_semantics=("arbitrary",) * len(grid), vmem_limit_bytes=VMEM_LIMIT, has_side_effects=True),
    )(*args, *comm.ins)
    return res[:no], res[no:]


def _ag_comm(shards, out_shape, places):
    n = len(shards)

    def copies(ins, outs, sems):
        ici_send, ici_recv, d2d_send, d2d_recv = sems
        x, y, c = lax.axis_index("x"), lax.axis_index("y"), lax.axis_index("c")
        chips = [(1 - x, y), (x, 1 - y), (1 - x, 1 - y)]
        slot = lambda chip: 2 * chip[0] + chip[1]

        def window(wi, origin, half):
            out, row_step, col_base, col_step = places[wi]
            r, cc = shards[wi].shape
            rows = pl.ds(pl.multiple_of(slot(origin) * row_step + half * (r // 2), r // 2), r // 2)
            cols = pl.ds(pl.multiple_of(col_base + slot(origin) * col_step, HEAD_DIM), cc) if col_step else pl.ds(col_base, cc)
            return outs[out].at[rows, cols]

        def over_ici(wi, j, origin):
            return pltpu.make_async_remote_copy(
                src_ref=_half_rows(ins[wi], c), dst_ref=window(wi, origin, c),
                send_sem=ici_send.at[3 * wi + j], recv_sem=ici_recv.at[3 * wi + j],
                device_id=(chips[j][0], chips[j][1], c), device_id_type=MESH)

        def over_d2d(wi, j, half):
            place = window(wi, chips[j], half)
            return pltpu.make_async_remote_copy(
                src_ref=place, dst_ref=place, send_sem=d2d_send.at[3 * wi + j], recv_sem=d2d_recv.at[3 * wi + j],
                device_id=(x, y, 1 - c), device_id_type=MESH)

        return over_ici, over_d2d, (x, y), chips, c

    def chunk_rows(wi):
        r, cc = shards[wi].shape
        item = shards[wi].dtype.itemsize
        return _pick(r, max(32 // item, BOUNCE_BYTES // (cc * item)), 32 // item)

    def start(ins, outs, scratch):
        over_ici, _, me, _, _ = copies(ins, outs, scratch[:4])
        for wi in range(n):
            for j in range(3):
                over_ici(wi, j, me).start()
        loc_sems = scratch[4]
        for wi in range(n):
            out, row_step, col_base, col_step = places[wi]
            r, cc = shards[wi].shape
            rc = chunk_rows(wi)
            buf = scratch[5 + wi]
            slot = 2 * me[0] + me[1]
            cols = pl.ds(pl.multiple_of(col_base + slot * col_step, HEAD_DIM), cc) if col_step else pl.ds(col_base, cc)

            def load(k):
                return pltpu.make_async_copy(ins[wi].at[pl.ds(k * rc, rc)], buf.at[k % 2], loc_sems.at[2 * wi])

            def store(k):
                rows = pl.ds(pl.multiple_of(slot * row_step + k * rc, rc), rc)
                return pltpu.make_async_copy(buf.at[k % 2], outs[out].at[rows, cols], loc_sems.at[2 * wi + 1])

            load(0).start()
            for k in range(r // rc):
                load(k).wait()
                if k + 1 < r // rc:
                    load(k + 1).start()
                store(k).start()
                store(k).wait()

    def finish(ins, outs, scratch):
        over_ici, over_d2d, me, chips, c = copies(ins, outs, scratch[:4])
        for wi in range(n):
            for j in range(3):
                over_ici(wi, j, chips[j]).wait_recv()
                over_d2d(wi, j, c).start()
        for wi in range(n):
            for j in range(3):
                over_d2d(wi, j, 1 - c).wait_recv()
        for wi in range(n):
            for j in range(3):
                over_ici(wi, j, me).wait_send()
                over_d2d(wi, j, c).wait_send()

    bounce = [pltpu.VMEM((min(2, s.shape[0] // chunk_rows(wi)), chunk_rows(wi), s.shape[1]), s.dtype) for wi, s in enumerate(shards)]
    return _Comm(list(shards), list(out_shape), [pltpu.SemaphoreType.DMA((3 * n,))] * 4 + [pltpu.SemaphoreType.DMA((2 * n,))] + bounce,
                 start, finish)


def _pair_swap(parts, name):
    n = len(parts)

    def body(*refs):
        ins, outs = refs[:n], refs[n:2 * n]
        send_sems, recv_sems = refs[2 * n:]
        x, y, c = lax.axis_index("x"), lax.axis_index("y"), lax.axis_index("c")

        def swap(wi):
            r2 = parts[wi].shape[1] // 2
            return pltpu.make_async_remote_copy(
                src_ref=ins[wi].at[:, pl.ds(pl.multiple_of((1 - c) * r2, r2), r2)], dst_ref=outs[wi],
                send_sem=send_sems.at[wi], recv_sem=recv_sems.at[wi], device_id=(x, y, 1 - c), device_id_type=MESH)

        copies = [swap(wi) for wi in range(n)]
        for cp in copies:
            cp.start()
        for cp in copies:
            cp.wait()

    return pl.pallas_call(
        body, name=name, in_specs=[ANY] * n, out_specs=[ANY] * n,
        out_shape=[jax.ShapeDtypeStruct((s.shape[0], s.shape[1] // 2, s.shape[2]), s.dtype) for s in parts],
        scratch_shapes=[pltpu.SemaphoreType.DMA((n,))] * 2, compiler_params=pltpu.CompilerParams(has_side_effects=True),
    )(*parts)


def _row_tile(r, c, budget=1 << 19):
    return r if r <= 8 else _pick(r, max(8, budget // c), 8)


def _sum_half(parts, other, cidx, name):
    nl, r, cc = parts.shape
    r2 = r // 2
    tr = _row_tile(r2, cc)

    def body(c_ref, p_ref, q_ref, o_ref):
        o_ref[...] = (_f(p_ref[...]) + _f(q_ref[...])).astype(o_ref.dtype)

    return pl.pallas_call(
        body, name=name, out_shape=jax.ShapeDtypeStruct((nl, r2, cc), parts.dtype),
        grid_spec=pltpu.PrefetchScalarGridSpec(
            num_scalar_prefetch=1, grid=(nl, r2 // tr),
            in_specs=[pl.BlockSpec((None, None, tr, cc), lambda l, i, c_ref: (l, c_ref[0], i, 0)),
                      pl.BlockSpec((None, tr, cc), lambda l, i, c_ref: (l, i, 0))],
            out_specs=pl.BlockSpec((None, tr, cc), lambda l, i, c_ref: (l, i, 0))),
        compiler_params=pltpu.CompilerParams(dimension_semantics=("parallel", "parallel"), vmem_limit_bytes=VMEM_LIMIT),
    )(cidx, parts.reshape(nl, 2, r2, cc), other)


def _exchange_comm(sums, entries):
    n = len(entries)

    def copies(ins, outs, sems):
        send_sems, recv_sems = sems
        x, y, c = lax.axis_index("x"), lax.axis_index("y"), lax.axis_index("c")
        chips = [(1 - x, y), (x, 1 - y), (1 - x, 1 - y)]
        slot = lambda chip: 2 * chip[0] + chip[1]

        def piece(wi, dest):
            si, mode, first, cols = entries[wi]
            if mode == "cols":
                return ins[si].at[0, :, pl.ds(pl.multiple_of(first + slot(dest) * cols, HEAD_DIM), cols)]
            return ins[si].at[slot(dest) if mode == "slot" else 0]

        def remote(wi, j, origin):
            return pltpu.make_async_remote_copy(
                src_ref=piece(wi, chips[j]), dst_ref=outs[wi].at[slot(origin)],
                send_sem=send_sems.at[3 * wi + j], recv_sem=recv_sems.at[3 * wi + j],
                device_id=(chips[j][0], chips[j][1], c), device_id_type=MESH)

        return remote, (x, y), chips

    def start(ins, outs, sems):
        remote, me, _ = copies(ins, outs, sems)
        for wi in range(n):
            for j in range(3):
                remote(wi, j, me).start()

    def finish(ins, outs, sems):
        remote, me, chips = copies(ins, outs, sems)
        for wi in range(n):
            for j in range(3):
                remote(wi, j, chips[j]).wait_recv()
        for wi in range(n):
            for j in range(3):
                remote(wi, j, me).wait_send()

    out_shape = [jax.ShapeDtypeStruct((4, sums[si].shape[1], cols), sums[si].dtype) for si, _, _, cols in entries]
    return _Comm(list(sums), out_shape, [pltpu.SemaphoreType.DMA((3 * n,))] * 2, start, finish)


def _sum_chips(got, own, mode, first, chip, name):
    _, r2, cc = got.shape
    tr = _row_tile(r2, cc)
    own_block = {"slot": lambda i, chip_ref: (chip_ref[0], i, 0), "same": lambda i, chip_ref: (0, i, 0),
                 "cols": lambda i, chip_ref: (0, i, first // cc + chip_ref[0])}[mode]

    def body(chip_ref, g_ref, own_ref, o_ref):
        term = lambda k: jnp.where(chip_ref[0] == k, _f(own_ref[...]), _f(g_ref[k]))
        o_ref[...] = ((term(0) + term(1)) + term(2)) + term(3)

    return pl.pallas_call(
        body, name=name, out_shape=jax.ShapeDtypeStruct((r2, cc), F32),
        grid_spec=pltpu.PrefetchScalarGridSpec(
            num_scalar_prefetch=1, grid=(r2 // tr,),
            in_specs=[pl.BlockSpec((4, tr, cc), lambda i, chip_ref: (0, i, 0)),
                      pl.BlockSpec((None, tr, cc), own_block)],
            out_specs=pl.BlockSpec((tr, cc), lambda i, chip_ref: (i, 0))),
        compiler_params=pltpu.CompilerParams(dimension_semantics=("parallel",), vmem_limit_bytes=VMEM_LIMIT),
    )(chip, got, own)


def _pair_gather(halves, name):
    n = len(halves)

    def body(*refs):
        ins, outs = refs[:n], refs[n:2 * n]
        send_sems, recv_sems = refs[2 * n:]
        x, y, c = lax.axis_index("x"), lax.axis_index("y"), lax.axis_index("c")
        copies = [pltpu.make_async_remote_copy(
            src_ref=ins[wi], dst_ref=outs[wi], send_sem=send_sems.at[wi], recv_sem=recv_sems.at[wi],
            device_id=(x, y, 1 - c), device_id_type=MESH) for wi in range(n)]
        for cp in copies:
            cp.start()
        for cp in copies:
            cp.wait()

    return pl.pallas_call(
        body, name=name, in_specs=[ANY] * n, out_specs=[ANY] * n,
        out_shape=[jax.ShapeDtypeStruct(s.shape, s.dtype) for s in halves],
        scratch_shapes=[pltpu.SemaphoreType.DMA((n,))] * 2, compiler_params=pltpu.CompilerParams(has_side_effects=True),
    )(*halves)


def _adamw_update(g, w, m, v):
    m_new = ADAM_B1 * m + (1.0 - ADAM_B1) * g
    v_new = ADAM_B2 * v + (1.0 - ADAM_B2) * jnp.square(g)
    m_hat = m_new / (1.0 - ADAM_B1 ** ADAM_STEP)
    v_hat = v_new / (1.0 - ADAM_B2 ** ADAM_STEP)
    return -ADAM_LR * (m_hat / (jnp.sqrt(v_hat) + ADAM_EPS) + ADAM_WD * w), m_new, v_new


def _adamw_whole(g, w, m, v, name):
    r, c = w.shape
    tc = 256

    def body(g_ref, w_ref, m_ref, v_ref, d_ref, nm_ref, nv_ref):
        d_ref[...], nm_ref[...], nv_ref[...] = _adamw_update(g_ref[...], w_ref[...], m_ref[...], v_ref[...])

    blk = pl.BlockSpec((r, tc), lambda i: (0, i))
    return pl.pallas_call(
        body, name=name, grid=(c // tc,), in_specs=[blk] * 4, out_specs=[blk] * 3, out_shape=[jax.ShapeDtypeStruct((r, c), F32)] * 3,
        compiler_params=pltpu.CompilerParams(dimension_semantics=("parallel",), vmem_limit_bytes=VMEM_LIMIT),
    )(g, w, m, v)


def _adamw(mine, other, cidx, w, m, v, name):
    r, c = w.shape
    tr = _row_tile(r // 2, c, 1 << 18)
    nh = (r // 2) // tr

    def body(c_ref, mine_ref, other_ref, w_ref, m_ref, v_ref, g_ref, d_ref, nm_ref, nv_ref):
        g = jnp.where(pl.program_id(0) // nh == c_ref[0], mine_ref[...], other_ref[...])
        g_ref[...] = g
        d_ref[...], nm_ref[...], nv_ref[...] = _adamw_update(g, w_ref[...], m_ref[...], v_ref[...])

    blk = pl.BlockSpec((tr, c), lambda i, c_ref: (i, 0))
    mine_spec = pl.BlockSpec((tr, c), lambda i, c_ref: (jnp.where(i // nh == c_ref[0], i % nh, 0), 0))
    other_spec = pl.BlockSpec((tr, c), lambda i, c_ref: (jnp.where(i // nh == c_ref[0], 0, i % nh), 0))
    return pl.pallas_call(
        body, name=name, out_shape=[jax.ShapeDtypeStruct((r, c), F32)] * 4,
        grid_spec=pltpu.PrefetchScalarGridSpec(
            num_scalar_prefetch=1, grid=(r // tr,), in_specs=[mine_spec, other_spec, blk, blk, blk], out_specs=[blk] * 4),
        compiler_params=pltpu.CompilerParams(dimension_semantics=("arbitrary",), vmem_limit_bytes=VMEM_LIMIT),
    )(cidx, mine, other, w, m, v)


BIG = ["ffn1_w_gate", "ffn1_w_up", "ffn1_w_down", "w_in", "gla_gate_up", "w_branch_fox", "w_branch_gla", "w_merge_gate", "w_out",
       "ffn2_w_gate", "ffn2_w_up", "ffn2_w_down", "w_ple_proj", "w_ple_gate"]
ROW_SHARDED = ("ffn1_w_down", "w_out", "ffn2_w_down", "w_ple_gate")
FUSED = {"ffn1_w_gate": ("gu1", 0, 2), "ffn1_w_up": ("gu1", 1, 2), "ffn2_w_gate": ("gu2", 0, 2), "ffn2_w_up": ("gu2", 1, 2)}
SMALL = ["ffn1_norm", "mix_norm", "fox_forget_bias", "gla_gate_bias", "gla_head_norm", "b_merge_gate", "ffn2_norm", "ple_norm", "final_norm"]
NAMES = ["ffn1_norm", "ffn1_w_gate", "ffn1_w_up", "ffn1_w_down", "mix_norm", "w_in", "fox_forget_bias", "gla_gate_up", "gla_gate_bias",
         "gla_head_norm", "w_branch_fox", "w_branch_gla", "w_merge_gate", "b_merge_gate", "w_out", "ffn2_norm", "ffn2_w_gate", "ffn2_w_up",
         "ffn2_w_down", "ple_norm", "w_ple_proj", "w_ple_gate", "final_norm"]
W_IN_COLS = (FOX_W, FOX_W, FOX_W, FOX_HEADS, GLA_KW, GLA_KW, GLA_VW, GLA_VW, GLA_RANK)
SMALL_ROWS, SMALL_COLS = 16, 1024


def _shard_parts(full, name):
    assert name in ROW_SHARDED, name
    return full.reshape(4, full.shape[0] // 4, full.shape[1])


W_IN_WIDE = ((0, 3 * FOX_W), (3 * FOX_W + FOX_HEADS, 3 * FOX_W + FOX_HEADS + 2 * GLA_KW + 2 * GLA_VW))
W_IN_NARROW = ((3 * FOX_W, 3 * FOX_W + FOX_HEADS), (sum(W_IN_COLS) - GLA_RANK, sum(W_IN_COLS)))


def _in_layout(stacked, w_merge_gate):
    per = stacked.shape[1]

    def columns(lo, hi):
        out = []
        while lo < hi:
            j, end = lo // per, min(hi, (lo // per + 1) * per)
            out.append(stacked[j * D_MODEL:(j + 1) * D_MODEL, lo - j * per:end - j * per])
            lo = end
        return out
    big = jnp.concatenate(columns(*W_IN_WIDE[0]) + columns(*W_IN_WIDE[1]) + [w_merge_gate], axis=1)
    sm = jnp.concatenate(columns(*W_IN_NARROW[0]) + columns(*W_IN_NARROW[1]) + [jnp.zeros((D_MODEL, SMALL_W - FOX_HEADS - GLA_RANK), BF16)], axis=1)
    return big, sm


def _w_in_parts(dw_big, dw_sm):
    runs = [(W_IN_WIDE[0], dw_big, Z_FQ), (W_IN_NARROW[0], dw_sm, 0), (W_IN_WIDE[1], dw_big, Z_GQ), (W_IN_NARROW[1], dw_sm, FOX_HEADS)]
    per = sum(W_IN_COLS) // 4
    parts = []
    for j in range(4):
        lo, hi, pieces = j * per, (j + 1) * per, []
        for (a, b), src, at in runs:
            if max(a, lo) < min(b, hi):
                pieces.append(src[:, at + max(a, lo) - a:at + min(b, hi) - a])
        parts.append(jnp.concatenate(pieces, axis=1))
    return jnp.stack(parts)


def _pad_lanes(a, width):
    return jnp.pad(a, ((0, 0), (0, width - a.shape[1])))


def kernel(x, p, ffn1_norm, ffn1_w_gate, ffn1_w_up, ffn1_w_down, mix_norm, w_in, fox_forget_bias, gla_gate_up, gla_gate_bias, gla_head_norm, w_branch_fox, w_branch_gla, w_merge_gate, b_merge_gate, w_out, ffn2_norm, ffn2_w_gate, ffn2_w_up, ffn2_w_down, ple_norm, w_ple_proj, w_ple_gate, final_norm, loss_target, m_ffn1_norm, m_ffn1_w_gate, m_ffn1_w_up, m_ffn1_w_down, m_mix_norm, m_w_in, m_fox_forget_bias, m_gla_gate_up, m_gla_gate_bias, m_gla_head_norm, m_w_branch_fox, m_w_branch_gla, m_w_merge_gate, m_b_merge_gate, m_w_out, m_ffn2_norm, m_ffn2_w_gate, m_ffn2_w_up, m_ffn2_w_down, m_ple_norm, m_w_ple_proj, m_w_ple_gate, m_final_norm, v_ffn1_norm, v_ffn1_w_gate, v_ffn1_w_up, v_ffn1_w_down, v_mix_norm, v_w_in, v_fox_forget_bias, v_gla_gate_up, v_gla_gate_bias, v_gla_head_norm, v_w_branch_fox, v_w_branch_gla, v_w_merge_gate, v_b_merge_gate, v_w_out, v_ffn2_norm, v_ffn2_w_gate, v_ffn2_w_up, v_ffn2_w_down, v_ple_norm, v_w_ple_proj, v_w_ple_gate, v_final_norm):
    args = dict(locals())
    wts = {n: args[n] for n in NAMES}
    mom = {n: args["m_" + n] for n in NAMES}
    var = {n: args["v_" + n] for n in NAMES}
    two_d = lambda a: a.reshape(-1, a.shape[-1])

    wire = lambda n: F32 if n == "gla_gate_up" else BF16
    cidx = lax.axis_index("c").astype(jnp.int32).reshape(1)
    chip = (2 * lax.axis_index("x") + lax.axis_index("y")).astype(jnp.int32)
    shards = {n: two_d(wts[n]).astype(wire(n)) for n in BIG}
    sp = {
        "ffn1_norm": two_d(ffn1_norm), "mix_norm": two_d(mix_norm), "fb": _pad_lanes(two_d(fox_forget_bias), SMALL_W),
        "gb": two_d(gla_gate_bias), "ghn": two_d(gla_head_norm), "bm": two_d(b_merge_gate), "ffn2_norm": two_d(ffn2_norm),
        "ple_norm": two_d(ple_norm), "final_norm": two_d(final_norm),
    }

    loss, grad_x, reduced, ds_ = _local_step(x[0], p[0, 0], loss_target[0], shards, sp, cidx, chip)

    small_g = {"ffn1_norm": ds_["ffn1_norm"], "mix_norm": ds_["mix_norm"], "fox_forget_bias": ds_["fb"][:, :FOX_HEADS],
               "gla_gate_bias": ds_["gb"], "gla_head_norm": ds_["ghn"], "b_merge_gate": ds_["bm"], "ffn2_norm": ds_["ffn2_norm"],
               "ple_norm": ds_["ple_norm"], "final_norm": ds_["final_norm"]}
    small_w = sum(two_d(wts[n]).shape[1] for n in SMALL)
    assert small_w <= SMALL_ROWS * SMALL_COLS
    packed = lambda d: _pad_lanes(jnp.concatenate([two_d(d[n]) for n in SMALL], axis=1), SMALL_ROWS * SMALL_COLS).reshape(SMALL_ROWS, SMALL_COLS)
    parts = [packed(small_g)[None]]
    pair_sums = [_sum_half(a, b, cidx, "sum_half") for a, b in zip(parts, _pair_swap(parts, "pair_swap"))]
    reduced["small"] = (_run_comm(_exchange_comm(pair_sums, [(0, "same", 0, SMALL_COLS)]), "chip_exchange")[0], pair_sums[0], "same", 0)
    mine = [_sum_chips(*reduced[n], chip.reshape(1), "sum_chips") for n in BIG + ["small"]]
    other = _pair_gather(mine, "pair_gather")

    out = {}
    for n, a, b in zip(BIG, mine[:-1], other[:-1]):
        if n == "w_in":
            g_t = jnp.where(cidx[0] == 0, jnp.concatenate([a, b]), jnp.concatenate([b, a])).T
            flip = lambda t: jnp.swapaxes(t, 1, 2)[0]
            res = _adamw_whole(g_t, flip(wts[n]), flip(mom[n]), flip(var[n]), "adamw_w_in")
            out[n] = [jnp.swapaxes(r[None], 1, 2) for r in [g_t, *res]]
            continue
        out[n] = [r.reshape(wts[n].shape) for r in _adamw(a, b, cidx, two_d(wts[n]), two_d(mom[n]), two_d(var[n]), "adamw_" + n)]
    small_out = [r.reshape(1, SMALL_ROWS * SMALL_COLS) for r in _adamw(mine[-1], other[-1], cidx, packed(wts), packed(mom), packed(var), "adamw_small")]
    off = 0
    for n in SMALL:
        cw = two_d(wts[n]).shape[1]
        out[n] = [r[:, off:off + cw].reshape(wts[n].shape) for r in small_out]
        off += cw

    total = lax.psum(loss[0, 0], ("x", "y", "c"))
    return (total, grad_x[None], *[out[n][0] for n in NAMES], *[out[n][1] for n in NAMES],
            *[out[n][2] for n in NAMES], *[out[n][3] for n in NAMES])
```
